```python
import jax, jax.numpy as jnp
from jax import lax
import numpy as np

D_MODEL = 1024
BATCH = 16
SEQ = 2048
DEPTH = 4

N_MIXERS = 2
D_FF = 2816
CONV_WIDTH = 31
DN_HEADS = 8
DN_HEAD_DIM = 128
DN_WIDTH = DN_HEADS * DN_HEAD_DIM
SHORT_CONV = 4
CHUNK = 64
N_CONV_LAYERS = (DEPTH + 1) // 2
N_DN_LAYERS = DEPTH // 2
N_SUB = 3
EPS = 1e-6
FFN_RES_WEIGHT = 0.5

kernel_name = "hybrid_conformer_gated_deltanet_block"


def rms_norm(x, g):
    xf = x.astype(jnp.float32)
    y = xf * lax.rsqrt(jnp.mean(xf * xf, axis=-1, keepdims=True) + EPS)
    return (y * g.astype(jnp.float32)).astype(x.dtype)


def layer_norm(x, g, b):
    xf = x.astype(jnp.float32)
    mu = jnp.mean(xf, axis=-1, keepdims=True)
    xc = xf - mu
    var = jnp.mean(xc * xc, axis=-1, keepdims=True)
    y = xc * lax.rsqrt(var + EPS) * g.astype(jnp.float32) + b.astype(jnp.float32)
    return y.astype(x.dtype)


def causal_dwconv(x, w):
    K, C = w.shape
    return lax.conv_general_dilated(
        x, w[:, None, :].astype(x.dtype), window_strides=(1,), padding=[(K - 1, 0)],
        dimension_numbers=("NWC", "WIO", "NWC"), feature_group_count=C)


def swiglu_ffn(h, w_in, w_out):
    gate, up = jnp.split(h @ w_in, 2, axis=-1)
    return (jax.nn.silu(gate) * up) @ w_out


def conv_module(h, w_glu, b_glu, w_dw, b_dw, ln_g, ln_b, w_pw, b_pw):
    a, b = jnp.split(h @ w_glu + b_glu, 2, axis=-1)
    u = a * jax.nn.sigmoid(b)
    u = causal_dwconv(u, w_dw) + b_dw
    u = jax.nn.silu(layer_norm(u, ln_g, ln_b))
    return u @ w_pw + b_pw


def l2norm(t):
    return t * lax.rsqrt(jnp.sum(t * t, axis=-1, keepdims=True) + EPS)


def chunk_gated_delta_rule(q, k, v, g, beta):
    f32 = jnp.float32
    B, T, H, Dh = q.shape
    N = T // CHUNK
    q = l2norm(q.astype(f32)) * (Dh ** -0.5)
    k = l2norm(k.astype(f32))
    v = v.astype(f32)

    def to_chunks(t):
        return t.reshape(B, N, CHUNK, H, -1).transpose(1, 0, 3, 2, 4)

    def to_chunks_s(t):
        return t.reshape(B, N, CHUNK, H).transpose(1, 0, 3, 2)

    q, k, v = to_chunks(q), to_chunks(k), to_chunks(v)
    beta = to_chunks_s(beta.astype(f32))
    g = jnp.cumsum(to_chunks_s(g.astype(f32)), axis=-1)

    causal = jnp.tril(jnp.ones((CHUNK, CHUNK), dtype=bool))
    strict = jnp.tril(jnp.ones((CHUNK, CHUNK), dtype=bool), -1)
    decay = jnp.exp(jnp.where(causal, g[..., :, None] - g[..., None, :], -jnp.inf))

    kb = k * beta[..., None]
    vb = v * beta[..., None]
    A = jnp.where(strict, jnp.einsum("nbhid,nbhjd->nbhij", kb, k) * decay, 0.0)
    eye = jnp.broadcast_to(jnp.eye(CHUNK, dtype=f32), A.shape)
    Tm = lax.linalg.triangular_solve(A, eye, left_side=True, lower=True, unit_diagonal=True)

    u = jnp.einsum("nbhij,nbhjd->nbhid", Tm, vb)
    w = jnp.einsum("nbhij,nbhjd->nbhid", Tm, kb * jnp.exp(g)[..., None])
    qg = q * jnp.exp(g)[..., None]
    intra = jnp.einsum("nbhid,nbhjd->nbhij", q, k) * decay
    g_last = g[..., -1]
    kd = k * jnp.exp(g_last[..., None] - g)[..., None]

    def step(S, inp):
        qg_i, w_i, u_i, intra_i, kd_i, gl_i = inp
        v_new = u_i - jnp.einsum("bhcd,bhde->bhce", w_i, S)
        o_i = jnp.einsum("bhcd,bhde->bhce", qg_i, S) + jnp.einsum("bhij,bhje->bhie", intra_i, v_new)
        S = S * jnp.exp(gl_i)[..., None, None] + jnp.einsum("bhcd,bhce->bhde", kd_i, v_new)
        return S, o_i

    S0 = jnp.zeros((B, H, Dh, v.shape[-1]), dtype=f32)
    _, o = lax.scan(step, S0, (qg, w, u, intra, kd, g_last))
    return o.transpose(1, 0, 3, 2, 4).reshape(B, T, H, -1)


def gated_deltanet(h, w_in, w_sconv, a_log, dt_bias, o_g, w_out):
    B, T, _ = h.shape
    W, H = DN_WIDTH, DN_HEADS
    proj = h @ w_in
    qkv = proj[..., :3 * W]
    z = proj[..., 3 * W:4 * W]
    a = proj[..., 4 * W:4 * W + H]
    b = proj[..., 4 * W + H:]
    qkv = jax.nn.silu(causal_dwconv(qkv, w_sconv))
    q, k, v = [t.reshape(B, T, H, DN_HEAD_DIM) for t in jnp.split(qkv, 3, axis=-1)]
    g = -jnp.exp(a_log.astype(jnp.float32)) * jax.nn.softplus(a.astype(jnp.float32) + dt_bias.astype(jnp.float32))
    beta = jax.nn.sigmoid(b.astype(jnp.float32))
    o = chunk_gated_delta_rule(q, k, v, g, beta).astype(h.dtype)
    o = rms_norm(o, o_g) * jax.nn.silu(z.reshape(B, T, H, DN_HEAD_DIM))
    return o.reshape(B, T, W) @ w_out


def _fwd_setup_inputs(seed: int = 0) -> dict:
    key = jax.random.key(seed)
    ks = iter(jax.random.split(key, 32))
    f32 = jnp.float32
    D, F, W, H = D_MODEL, D_FF, DN_WIDTH, DN_HEADS
    NA, NB = N_CONV_LAYERS, N_DN_LAYERS

    def nrm(shape, fan_in, mult=1.0):
        return jax.random.normal(next(ks), shape, f32) * (mult * fan_in ** -0.5)

    def small(shape, s=0.02):
        return jax.random.normal(next(ks), shape, f32) * s

    x = jax.random.normal(next(ks), (BATCH, SEQ, D), f32)
    c = jax.random.normal(next(ks), (BATCH, D), f32)
    norm_g = 1.0 + small((DEPTH, N_SUB, D))
    w_ada = nrm((DEPTH, D, N_SUB * 3 * D), D, 0.2)
    b_ada = small((DEPTH, N_SUB * 3 * D))
    w_ffn_in = nrm((DEPTH, 2, D, 2 * F), D)
    w_ffn_out = nrm((DEPTH, 2, F, D), F)
    cm_w_glu = nrm((NA, D, 2 * D), D)
    cm_b_glu = small((NA, 2 * D))
    cm_w_dw = nrm((NA, CONV_WIDTH, D), CONV_WIDTH)
    cm_b_dw = small((NA, D))
    cm_ln_g = 1.0 + small((NA, D))
    cm_ln_b = small((NA, D))
    cm_w_pw = nrm((NA, D, D), D)
    cm_b_pw = small((NA, D))
    dn_w_in = nrm((NB, D, 4 * W + 2 * H), D)
    dn_w_sconv = nrm((NB, SHORT_CONV, 3 * W), SHORT_CONV)
    dn_a_log = jnp.log(jax.random.uniform(next(ks), (NB, H), f32, 1.0, 16.0))
    dt = jnp.exp(jax.random.uniform(next(ks), (NB, H), f32, np.log(1e-3), np.log(1e-1)))
    dn_dt_bias = dt + jnp.log(-jnp.expm1(-dt))
    dn_o_g = 1.0 + small((NB, DN_HEAD_DIM))
    dn_w_out = nrm((NB, W, D), W)
    final_g = 1.0 + small((D,))
    return {"x": x, "c": c, "norm_g": norm_g, "w_ada": w_ada, "b_ada": b_ada,
            "w_ffn_in": w_ffn_in, "w_ffn_out": w_ffn_out,
            "cm_w_glu": cm_w_glu, "cm_b_glu": cm_b_glu, "cm_w_dw": cm_w_dw, "cm_b_dw": cm_b_dw,
            "cm_ln_g": cm_ln_g, "cm_ln_b": cm_ln_b, "cm_w_pw": cm_w_pw, "cm_b_pw": cm_b_pw,
            "dn_w_in": dn_w_in, "dn_w_sconv": dn_w_sconv, "dn_a_log": dn_a_log,
            "dn_dt_bias": dn_dt_bias, "dn_o_g": dn_o_g, "dn_w_out": dn_w_out,
            "final_g": final_g}


def _fwd_reference(x, c, norm_g, w_ada, b_ada, w_ffn_in, w_ffn_out,
              cm_w_glu, cm_b_glu, cm_w_dw, cm_b_dw, cm_ln_g, cm_ln_b, cm_w_pw, cm_b_pw,
              dn_w_in, dn_w_sconv, dn_a_log, dn_dt_bias, dn_o_g, dn_w_out, final_g):
    B = x.shape[0]
    D = x.shape[-1]
    cs = jax.nn.silu(c)
    for i in range(DEPTH):
        mod = (cs @ w_ada[i] + b_ada[i]).reshape(B, N_SUB, 3, D)

        def modulated(x, j):
            shift, scale = mod[:, j, 0][:, None, :], mod[:, j, 1][:, None, :]
            return rms_norm(x, norm_g[i, j]) * (1.0 + scale) + shift

        def gate(j):
            return 1.0 + mod[:, j, 2][:, None, :]

        h = modulated(x, 0)
        x = x + FFN_RES_WEIGHT * gate(0) * swiglu_ffn(h, w_ffn_in[i, 0], w_ffn_out[i, 0])

        h = modulated(x, 1)
        if i % N_MIXERS == 0:
            a = i // N_MIXERS
            y = conv_module(h, cm_w_glu[a], cm_b_glu[a], cm_w_dw[a], cm_b_dw[a],
                            cm_ln_g[a], cm_ln_b[a], cm_w_pw[a], cm_b_pw[a])
        else:
            m = i // N_MIXERS
            y = gated_deltanet(h, dn_w_in[m], dn_w_sconv[m], dn_a_log[m], dn_dt_bias[m],
                               dn_o_g[m], dn_w_out[m])
        x = x + gate(1) * y

        h = modulated(x, 2)
        x = x + FFN_RES_WEIGHT * gate(2) * swiglu_ffn(h, w_ffn_in[i, 1], w_ffn_out[i, 1])
    return rms_norm(x, final_g)


import jax as _jax
import jax.numpy as _jnp

TWIN_FORMAT = 'train_step'
FWD_PARAMS = ['x', 'c', 'norm_g', 'w_ada', 'b_ada', 'w_ffn_in', 'w_ffn_out', 'cm_w_glu', 'cm_b_glu', 'cm_w_dw', 'cm_b_dw', 'cm_ln_g', 'cm_ln_b', 'cm_w_pw', 'cm_b_pw', 'dn_w_in', 'dn_w_sconv', 'dn_a_log', 'dn_dt_bias', 'dn_o_g', 'dn_w_out', 'final_g']
TWIN_WEIGHTS = ['norm_g', 'w_ada', 'b_ada', 'w_ffn_in', 'w_ffn_out', 'cm_w_glu', 'cm_b_glu', 'cm_w_dw', 'cm_b_dw', 'cm_ln_g', 'cm_ln_b', 'cm_w_pw', 'cm_b_pw', 'dn_w_in', 'dn_w_sconv', 'dn_a_log', 'dn_dt_bias', 'dn_o_g', 'dn_w_out', 'final_g']
TWIN_DIFF_INPUT = 'x'
TWIN_INPUTS = ['x', 'c', 'norm_g', 'w_ada', 'b_ada', 'w_ffn_in', 'w_ffn_out', 'cm_w_glu', 'cm_b_glu', 'cm_w_dw', 'cm_b_dw', 'cm_ln_g', 'cm_ln_b', 'cm_w_pw', 'cm_b_pw', 'dn_w_in', 'dn_w_sconv', 'dn_a_log', 'dn_dt_bias', 'dn_o_g', 'dn_w_out', 'final_g', 'loss_target', 'm_norm_g', 'm_w_ada', 'm_b_ada', 'm_w_ffn_in', 'm_w_ffn_out', 'm_cm_w_glu', 'm_cm_b_glu', 'm_cm_w_dw', 'm_cm_b_dw', 'm_cm_ln_g', 'm_cm_ln_b', 'm_cm_w_pw', 'm_cm_b_pw', 'm_dn_w_in', 'm_dn_w_sconv', 'm_dn_a_log', 'm_dn_dt_bias', 'm_dn_o_g', 'm_dn_w_out', 'm_final_g', 'v_norm_g', 'v_w_ada', 'v_b_ada', 'v_w_ffn_in', 'v_w_ffn_out', 'v_cm_w_glu', 'v_cm_b_glu', 'v_cm_w_dw', 'v_cm_b_dw', 'v_cm_ln_g', 'v_cm_ln_b', 'v_cm_w_pw', 'v_cm_b_pw', 'v_dn_w_in', 'v_dn_w_sconv', 'v_dn_a_log', 'v_dn_dt_bias', 'v_dn_o_g', 'v_dn_w_out', 'v_final_g']
TWIN_OUTPUTS = ['loss', 'grad_x', 'grad_norm_g', 'grad_w_ada', 'grad_b_ada', 'grad_w_ffn_in', 'grad_w_ffn_out', 'grad_cm_w_glu', 'grad_cm_b_glu', 'grad_cm_w_dw', 'grad_cm_b_dw', 'grad_cm_ln_g', 'grad_cm_ln_b', 'grad_cm_w_pw', 'grad_cm_b_pw', 'grad_dn_w_in', 'grad_dn_w_sconv', 'grad_dn_a_log', 'grad_dn_dt_bias', 'grad_dn_o_g', 'grad_dn_w_out', 'grad_final_g', 'delta_norm_g', 'delta_w_ada', 'delta_b_ada', 'delta_w_ffn_in', 'delta_w_ffn_out', 'delta_cm_w_glu', 'delta_cm_b_glu', 'delta_cm_w_dw', 'delta_cm_b_dw', 'delta_cm_ln_g', 'delta_cm_ln_b', 'delta_cm_w_pw', 'delta_cm_b_pw', 'delta_dn_w_in', 'delta_dn_w_sconv', 'delta_dn_a_log', 'delta_dn_dt_bias', 'delta_dn_o_g', 'delta_dn_w_out', 'delta_final_g', 'new_m_norm_g', 'new_m_w_ada', 'new_m_b_ada', 'new_m_w_ffn_in', 'new_m_w_ffn_out', 'new_m_cm_w_glu', 'new_m_cm_b_glu', 'new_m_cm_w_dw', 'new_m_cm_b_dw', 'new_m_cm_ln_g', 'new_m_cm_ln_b', 'new_m_cm_w_pw', 'new_m_cm_b_pw', 'new_m_dn_w_in', 'new_m_dn_w_sconv', 'new_m_dn_a_log', 'new_m_dn_dt_bias', 'new_m_dn_o_g', 'new_m_dn_w_out', 'new_m_final_g', 'new_v_norm_g', 'new_v_w_ada', 'new_v_b_ada', 'new_v_w_ffn_in', 'new_v_w_ffn_out', 'new_v_cm_w_glu', 'new_v_cm_b_glu', 'new_v_cm_w_dw', 'new_v_cm_b_dw', 'new_v_cm_ln_g', 'new_v_cm_ln_b', 'new_v_cm_w_pw', 'new_v_cm_b_pw', 'new_v_dn_w_in', 'new_v_dn_w_sconv', 'new_v_dn_a_log', 'new_v_dn_dt_bias', 'new_v_dn_o_g', 'new_v_dn_w_out', 'new_v_final_g']
TWIN_LEAF_KINDS = {'loss': 'loss', 'grad_x': 'grad_x', 'grad_norm_g': 'grad_w', 'grad_w_ada': 'grad_w', 'grad_b_ada': 'grad_w', 'grad_w_ffn_in': 'grad_w', 'grad_w_ffn_out': 'grad_w', 'grad_cm_w_glu': 'grad_w', 'grad_cm_b_glu': 'grad_w', 'grad_cm_w_dw': 'grad_w', 'grad_cm_b_dw': 'grad_w', 'grad_cm_ln_g': 'grad_w', 'grad_cm_ln_b': 'grad_w', 'grad_cm_w_pw': 'grad_w', 'grad_cm_b_pw': 'grad_w', 'grad_dn_w_in': 'grad_w', 'grad_dn_w_sconv': 'grad_w', 'grad_dn_a_log': 'grad_w', 'grad_dn_dt_bias': 'grad_w', 'grad_dn_o_g': 'grad_w', 'grad_dn_w_out': 'grad_w', 'grad_final_g': 'grad_w', 'delta_norm_g': 'delta_w', 'delta_w_ada': 'delta_w', 'delta_b_ada': 'delta_w', 'delta_w_ffn_in': 'delta_w', 'delta_w_ffn_out': 'delta_w', 'delta_cm_w_glu': 'delta_w', 'delta_cm_b_glu': 'delta_w', 'delta_cm_w_dw': 'delta_w', 'delta_cm_b_dw': 'delta_w', 'delta_cm_ln_g': 'delta_w', 'delta_cm_ln_b': 'delta_w', 'delta_cm_w_pw': 'delta_w', 'delta_cm_b_pw': 'delta_w', 'delta_dn_w_in': 'delta_w', 'delta_dn_w_sconv': 'delta_w', 'delta_dn_a_log': 'delta_w', 'delta_dn_dt_bias': 'delta_w', 'delta_dn_o_g': 'delta_w', 'delta_dn_w_out': 'delta_w', 'delta_final_g': 'delta_w', 'new_m_norm_g': 'new_m', 'new_m_w_ada': 'new_m', 'new_m_b_ada': 'new_m', 'new_m_w_ffn_in': 'new_m', 'new_m_w_ffn_out': 'new_m', 'new_m_cm_w_glu': 'new_m', 'new_m_cm_b_glu': 'new_m', 'new_m_cm_w_dw': 'new_m', 'new_m_cm_b_dw': 'new_m', 'new_m_cm_ln_g': 'new_m', 'new_m_cm_ln_b': 'new_m', 'new_m_cm_w_pw': 'new_m', 'new_m_cm_b_pw': 'new_m', 'new_m_dn_w_in': 'new_m', 'new_m_dn_w_sconv': 'new_m', 'new_m_dn_a_log': 'new_m', 'new_m_dn_dt_bias': 'new_m', 'new_m_dn_o_g': 'new_m', 'new_m_dn_w_out': 'new_m', 'new_m_final_g': 'new_m', 'new_v_norm_g': 'new_v', 'new_v_w_ada': 'new_v', 'new_v_b_ada': 'new_v', 'new_v_w_ffn_in': 'new_v', 'new_v_w_ffn_out': 'new_v', 'new_v_cm_w_glu': 'new_v', 'new_v_cm_b_glu': 'new_v', 'new_v_cm_w_dw': 'new_v', 'new_v_cm_b_dw': 'new_v', 'new_v_cm_ln_g': 'new_v', 'new_v_cm_ln_b': 'new_v', 'new_v_cm_w_pw': 'new_v', 'new_v_cm_b_pw': 'new_v', 'new_v_dn_w_in': 'new_v', 'new_v_dn_w_sconv': 'new_v', 'new_v_dn_a_log': 'new_v', 'new_v_dn_dt_bias': 'new_v', 'new_v_dn_o_g': 'new_v', 'new_v_dn_w_out': 'new_v', 'new_v_final_g': 'new_v'}


def _forward(args):
    return _fwd_reference(*[args[k] for k in FWD_PARAMS])


def _output_shape():
    out = _jax.eval_shape(lambda: _forward(_fwd_setup_inputs(0)))
    return out.shape, out.dtype

N_MICROBATCH = 1
ADAM_LR = 0.001
ADAM_B1 = 0.9
ADAM_B2 = 0.999
ADAM_EPS = 1e-08
ADAM_WD = 0.01
ADAM_STEP = 10
PER_EXAMPLE_BATCH_AXIS = {'x': 0, 'c': 0, 'loss_target': 0}
SHARED_INPUTS = []
_WEIGHT_DTYPES = {'norm_g': _jnp.float32, 'w_ada': _jnp.float32, 'b_ada': _jnp.float32, 'w_ffn_in': _jnp.float32, 'w_ffn_out': _jnp.float32, 'cm_w_glu': _jnp.float32, 'cm_b_glu': _jnp.float32, 'cm_w_dw': _jnp.float32, 'cm_b_dw': _jnp.float32, 'cm_ln_g': _jnp.float32, 'cm_ln_b': _jnp.float32, 'cm_w_pw': _jnp.float32, 'cm_b_pw': _jnp.float32, 'dn_w_in': _jnp.float32, 'dn_w_sconv': _jnp.float32, 'dn_a_log': _jnp.float32, 'dn_dt_bias': _jnp.float32, 'dn_o_g': _jnp.float32, 'dn_w_out': _jnp.float32, 'final_g': _jnp.float32}
MOMENT_SCALE = {'norm_g': 9.588006e-02, 'w_ada': 5.533951e-02, 'b_ada': 9.524903e-02, 'w_ffn_in': 3.221420e-02, 'w_ffn_out': 5.251807e-02, 'cm_w_glu': 8.584122e-02, 'cm_b_glu': 9.627182e-02, 'cm_w_dw': 1.129399e-01, 'cm_b_dw': 2.654731e-01, 'cm_ln_g': 1.441833e-01, 'cm_ln_b': 1.218907e-01, 'cm_w_pw': 1.112137e-01, 'cm_b_pw': 2.055235e-01, 'dn_w_in': 6.565119e-02, 'dn_w_sconv': 6.046147e-02, 'dn_a_log': 3.559391e-01, 'dn_dt_bias': 3.403394e-01, 'dn_o_g': 2.905269e-01, 'dn_w_out': 7.925804e-02, 'final_g': 3.211042e+01}


def _to_microbatches(a, axis):
    t = _jnp.moveaxis(a, axis, 0)
    t = t.reshape((N_MICROBATCH, t.shape[0] // N_MICROBATCH) + t.shape[1:])
    return _jnp.moveaxis(t, 1, axis + 1)


def setup_inputs(seed: int = 0) -> dict:
    inp = _fwd_setup_inputs(seed)
    key = _jax.random.fold_in(_jax.random.key(seed), 7919)
    shape, _ = _output_shape()
    out = dict(inp)
    out["loss_target"] = _jax.random.normal(_jax.random.fold_in(key, 0), shape, _jnp.float32)
    for i, name in enumerate(TWIN_WEIGHTS):
        w = inp[name].astype(_jnp.float32)
        if MOMENT_SCALE is None:
            s = _jnp.sqrt(_jnp.mean(_jnp.square(w)) + 1e-30)
        else:
            s = MOMENT_SCALE[name]
        km, kv = _jax.random.split(_jax.random.fold_in(key, i + 1))
        out[name] = w
        out["m_" + name] = s * _jax.random.normal(km, w.shape, _jnp.float32)
        out["v_" + name] = (s * s) * _jax.random.uniform(kv, w.shape, _jnp.float32, 0.5, 1.5)
    if N_MICROBATCH > 1:
        for name, axis in PER_EXAMPLE_BATCH_AXIS.items():
            out[name] = _to_microbatches(out[name], axis)
    return {'x': out['x'], 'c': out['c'], 'norm_g': out['norm_g'], 'w_ada': out['w_ada'], 'b_ada': out['b_ada'], 'w_ffn_in': out['w_ffn_in'], 'w_ffn_out': out['w_ffn_out'], 'cm_w_glu': out['cm_w_glu'], 'cm_b_glu': out['cm_b_glu'], 'cm_w_dw': out['cm_w_dw'], 'cm_b_dw': out['cm_b_dw'], 'cm_ln_g': out['cm_ln_g'], 'cm_ln_b': out['cm_ln_b'], 'cm_w_pw': out['cm_w_pw'], 'cm_b_pw': out['cm_b_pw'], 'dn_w_in': out['dn_w_in'], 'dn_w_sconv': out['dn_w_sconv'], 'dn_a_log': out['dn_a_log'], 'dn_dt_bias': out['dn_dt_bias'], 'dn_o_g': out['dn_o_g'], 'dn_w_out': out['dn_w_out'], 'final_g': out['final_g'], 'loss_target': out['loss_target'], 'm_norm_g': out['m_norm_g'], 'm_w_ada': out['m_w_ada'], 'm_b_ada': out['m_b_ada'], 'm_w_ffn_in': out['m_w_ffn_in'], 'm_w_ffn_out': out['m_w_ffn_out'], 'm_cm_w_glu': out['m_cm_w_glu'], 'm_cm_b_glu': out['m_cm_b_glu'], 'm_cm_w_dw': out['m_cm_w_dw'], 'm_cm_b_dw': out['m_cm_b_dw'], 'm_cm_ln_g': out['m_cm_ln_g'], 'm_cm_ln_b': out['m_cm_ln_b'], 'm_cm_w_pw': out['m_cm_w_pw'], 'm_cm_b_pw': out['m_cm_b_pw'], 'm_dn_w_in': out['m_dn_w_in'], 'm_dn_w_sconv': out['m_dn_w_sconv'], 'm_dn_a_log': out['m_dn_a_log'], 'm_dn_dt_bias': out['m_dn_dt_bias'], 'm_dn_o_g': out['m_dn_o_g'], 'm_dn_w_out': out['m_dn_w_out'], 'm_final_g': out['m_final_g'], 'v_norm_g': out['v_norm_g'], 'v_w_ada': out['v_w_ada'], 'v_b_ada': out['v_b_ada'], 'v_w_ffn_in': out['v_w_ffn_in'], 'v_w_ffn_out': out['v_w_ffn_out'], 'v_cm_w_glu': out['v_cm_w_glu'], 'v_cm_b_glu': out['v_cm_b_glu'], 'v_cm_w_dw': out['v_cm_w_dw'], 'v_cm_b_dw': out['v_cm_b_dw'], 'v_cm_ln_g': out['v_cm_ln_g'], 'v_cm_ln_b': out['v_cm_ln_b'], 'v_cm_w_pw': out['v_cm_w_pw'], 'v_cm_b_pw': out['v_cm_b_pw'], 'v_dn_w_in': out['v_dn_w_in'], 'v_dn_w_sconv': out['v_dn_w_sconv'], 'v_dn_a_log': out['v_dn_a_log'], 'v_dn_dt_bias': out['v_dn_dt_bias'], 'v_dn_o_g': out['v_dn_o_g'], 'v_dn_w_out': out['v_dn_w_out'], 'v_final_g': out['v_final_g']}


def _loss(weights, diff, rest, loss_target):
    with _jax.named_scope("forward"):
        args = {**rest, TWIN_DIFF_INPUT: diff, **{k: w.astype(_WEIGHT_DTYPES[k]) for k, w in weights.items()}}
        y = _forward(args)
    with _jax.named_scope("loss_head"):
        err = _jnp.square(y.astype(_jnp.float32) - loss_target)
        return 0.5 * _jnp.sum(_jnp.mean(err, axis=-1)) if err.ndim else 0.5 * err


def _adamw(w, g, m, v):
    m = ADAM_B1 * m + (1.0 - ADAM_B1) * g
    v = ADAM_B2 * v + (1.0 - ADAM_B2) * _jnp.square(g)
    m_hat = m / (1.0 - ADAM_B1 ** ADAM_STEP)
    v_hat = v / (1.0 - ADAM_B2 ** ADAM_STEP)
    delta = -ADAM_LR * (m_hat / (_jnp.sqrt(v_hat) + ADAM_EPS) + ADAM_WD * w)
    return delta, m, v


def reference(x, c, norm_g, w_ada, b_ada, w_ffn_in, w_ffn_out, cm_w_glu, cm_b_glu, cm_w_dw, cm_b_dw, cm_ln_g, cm_ln_b, cm_w_pw, cm_b_pw, dn_w_in, dn_w_sconv, dn_a_log, dn_dt_bias, dn_o_g, dn_w_out, final_g, loss_target, m_norm_g, m_w_ada, m_b_ada, m_w_ffn_in, m_w_ffn_out, m_cm_w_glu, m_cm_b_glu, m_cm_w_dw, m_cm_b_dw, m_cm_ln_g, m_cm_ln_b, m_cm_w_pw, m_cm_b_pw, m_dn_w_in, m_dn_w_sconv, m_dn_a_log, m_dn_dt_bias, m_dn_o_g, m_dn_w_out, m_final_g, v_norm_g, v_w_ada, v_b_ada, v_w_ffn_in, v_w_ffn_out, v_cm_w_glu, v_cm_b_glu, v_cm_w_dw, v_cm_b_dw, v_cm_ln_g, v_cm_ln_b, v_cm_w_pw, v_cm_b_pw, v_dn_w_in, v_dn_w_sconv, v_dn_a_log, v_dn_dt_bias, v_dn_o_g, v_dn_w_out, v_final_g):
    given = dict(x=x, c=c, norm_g=norm_g, w_ada=w_ada, b_ada=b_ada, w_ffn_in=w_ffn_in, w_ffn_out=w_ffn_out, cm_w_glu=cm_w_glu, cm_b_glu=cm_b_glu, cm_w_dw=cm_w_dw, cm_b_dw=cm_b_dw, cm_ln_g=cm_ln_g, cm_ln_b=cm_ln_b, cm_w_pw=cm_w_pw, cm_b_pw=cm_b_pw, dn_w_in=dn_w_in, dn_w_sconv=dn_w_sconv, dn_a_log=dn_a_log, dn_dt_bias=dn_dt_bias, dn_o_g=dn_o_g, dn_w_out=dn_w_out, final_g=final_g, loss_target=loss_target, m_norm_g=m_norm_g, m_w_ada=m_w_ada, m_b_ada=m_b_ada, m_w_ffn_in=m_w_ffn_in, m_w_ffn_out=m_w_ffn_out, m_cm_w_glu=m_cm_w_glu, m_cm_b_glu=m_cm_b_glu, m_cm_w_dw=m_cm_w_dw, m_cm_b_dw=m_cm_b_dw, m_cm_ln_g=m_cm_ln_g, m_cm_ln_b=m_cm_ln_b, m_cm_w_pw=m_cm_w_pw, m_cm_b_pw=m_cm_b_pw, m_dn_w_in=m_dn_w_in, m_dn_w_sconv=m_dn_w_sconv, m_dn_a_log=m_dn_a_log, m_dn_dt_bias=m_dn_dt_bias, m_dn_o_g=m_dn_o_g, m_dn_w_out=m_dn_w_out, m_final_g=m_final_g, v_norm_g=v_norm_g, v_w_ada=v_w_ada, v_b_ada=v_b_ada, v_w_ffn_in=v_w_ffn_in, v_w_ffn_out=v_w_ffn_out, v_cm_w_glu=v_cm_w_glu, v_cm_b_glu=v_cm_b_glu, v_cm_w_dw=v_cm_w_dw, v_cm_b_dw=v_cm_b_dw, v_cm_ln_g=v_cm_ln_g, v_cm_ln_b=v_cm_ln_b, v_cm_w_pw=v_cm_w_pw, v_cm_b_pw=v_cm_b_pw, v_dn_w_in=v_dn_w_in, v_dn_w_sconv=v_dn_w_sconv, v_dn_a_log=v_dn_a_log, v_dn_dt_bias=v_dn_dt_bias, v_dn_o_g=v_dn_o_g, v_dn_w_out=v_dn_w_out, v_final_g=v_final_g)
    weights = {n: given[n] for n in TWIN_WEIGHTS}
    shared = {n: given[n] for n in SHARED_INPUTS}
    per_example = {n: given[n] for n in ['x', 'c']}
    grad_fn = _jax.value_and_grad(_loss, argnums=(0, 1))

    def one_microbatch(ex, loss_target):
        ex = dict(ex)
        diff = ex.pop(TWIN_DIFF_INPUT)
        return grad_fn(weights, diff, {**shared, **ex}, loss_target)

    if N_MICROBATCH == 1:
        loss, (grad_w, grad_x) = one_microbatch(per_example, given["loss_target"])
    else:
        def body(carry, xs):
            loss_sum, grad_sum = carry
            l_k, (gw_k, gx_k) = one_microbatch(xs[0], xs[1])
            with _jax.named_scope("update"):
                return (loss_sum + l_k, _jax.tree.map(_jnp.add, grad_sum, gw_k)), gx_k

        init = (_jnp.zeros((), _jnp.float32), _jax.tree.map(_jnp.zeros_like, weights))
        (loss, grad_w), grad_x = _jax.lax.scan(body, init, (per_example, given["loss_target"]))
    with _jax.named_scope("update"):
        delta_w, new_m, new_v = {}, {}, {}
        for n in TWIN_WEIGHTS:
            delta_w[n], new_m[n], new_v[n] = _adamw(weights[n], grad_w[n], given["m_" + n], given["v_" + n])
    return (loss, grad_x, *[grad_w[n] for n in TWIN_WEIGHTS], *[delta_w[n] for n in TWIN_WEIGHTS],
            *[new_m[n] for n in TWIN_WEIGHTS], *[new_v[n] for n in TWIN_WEIGHTS])
```

```python
import functools

import jax
import jax.numpy as jnp
from jax import lax
from jax.experimental import pallas as pl
from jax.experimental.pallas import tpu as pltpu

F32 = jnp.float32
BF16 = jnp.bfloat16
EPS = 1e-6
CHUNK = 64
N_CHIPS = 4
N_DEV = 8
LANES = 128
CONV_HALO = 32
SCONV_HALO = 8
VMEM_LIMIT_V7X = 60 * 1024 * 1024
HI = lax.Precision.HIGHEST
MESH = pl.DeviceIdType.MESH
HBM_SPEC = pl.BlockSpec(memory_space=pltpu.HBM)

ADAM_LR, ADAM_B1, ADAM_B2, ADAM_EPS, ADAM_WD, ADAM_STEP = 0.001, 0.9, 0.999, 1e-08, 0.01, 10


def _cparams(n_axes):
    return pltpu.CompilerParams(dimension_semantics=("arbitrary",) * n_axes, vmem_limit_bytes=VMEM_LIMIT_V7X)


def _tile(n, pref, mult=8):
    for t in range(min(n, pref) // mult * mult, 0, -mult):
        if n % t == 0:
            return t
    return n


def _mm(a, b):
    return lax.dot_general(a.astype(BF16), b.astype(BF16), (((1,), (0,)), ((), ())), preferred_element_type=F32)


def _mm_nt(a, b):
    return lax.dot_general(a.astype(BF16), b.astype(BF16), (((1,), (1,)), ((), ())), preferred_element_type=F32)


def _mm_tn(a, b):
    return lax.dot_general(a.astype(BF16), b.astype(BF16), (((0,), (0,)), ((), ())), preferred_element_type=F32)


def _sigmoid(x):
    return jax.nn.sigmoid(x)


def _dsilu(x, s):
    return s * (1.0 + x * (1.0 - s))


def _softplus(x):
    return jnp.maximum(x, 0.0) + jnp.log(1.0 + jnp.exp(-jnp.abs(x)))


def _modnorm(x, g, scale, shift):
    r = lax.rsqrt(jnp.mean(x * x, axis=-1, keepdims=True) + EPS)
    return (x * r) * g * (1.0 + scale) + shift


def _modnorm_bwd(x, g, scale, dh):
    r = lax.rsqrt(jnp.mean(x * x, axis=-1, keepdims=True) + EPS)
    xn = x * r
    dshift = jnp.sum(dh, axis=0, keepdims=True)
    dscale = jnp.sum(dh * (xn * g), axis=0, keepdims=True)
    dhn = dh * (1.0 + scale)
    dg = jnp.sum(dhn * xn, axis=0, keepdims=True)
    dxn = dhn * g
    dx = r * (dxn - xn * jnp.mean(dxn * xn, axis=-1, keepdims=True))
    return dx, dg, dscale, dshift


def _sum0(a):
    return jnp.sum(a, axis=0, keepdims=True)


def ffn_fwd(x, mod3, g, w_in, w_out):
    B, T, D = x.shape
    Fc = w_in.shape[2]
    w_in = w_in.reshape(2, 2, D, Fc)
    w_out = w_out.reshape(2, Fc, D)
    tm = _tile(T, 512)

    def body(x_ref, mod_ref, g_ref, wi_ref, wo_ref, xo_ref, y_ref, h_s, acc_s):
        f = pl.program_id(2)

        @pl.when(f == 0)
        def _():
            h = _modnorm(x_ref[...], g_ref[...], mod_ref[1:2, :], mod_ref[0:1, :])
            h_s[...] = h.astype(BF16)
            acc_s[...] = jnp.zeros_like(acc_s)

        h = h_s[...]
        gt = _mm(h, wi_ref[0])
        up = _mm(h, wi_ref[1])
        a = gt * _sigmoid(gt) * up
        acc_s[...] += _mm(a, wo_ref[...])

        @pl.when(f == 1)
        def _():
            y = acc_s[...]
            y_ref[...] = y
            xo_ref[...] = x_ref[...] + 0.5 * (1.0 + mod_ref[2:3, :]) * y

    tok = pl.BlockSpec((None, tm, D), lambda b, t, f: (b, t, 0))
    return pl.pallas_call(
        body, name="ffn_fwd", grid=(B, T // tm, 2),
        in_specs=[tok,
                  pl.BlockSpec((None, 3, D), lambda b, t, f: (b, 0, 0)),
                  pl.BlockSpec((1, D), lambda b, t, f: (0, 0)),
                  pl.BlockSpec((2, None, D, Fc), lambda b, t, f: (0, f, 0, 0)),
                  pl.BlockSpec((None, Fc, D), lambda b, t, f: (f, 0, 0))],
        out_specs=[tok, tok],
        out_shape=[jax.ShapeDtypeStruct((B, T, D), F32)] * 2,
        scratch_shapes=[pltpu.VMEM((tm, D), BF16), pltpu.VMEM((tm, D), F32)],
        compiler_params=_cparams(3),
    )(x, mod3, g, w_in, w_out)


def ffn_bwd(x, dres, y, mod3, g, w_in, w_out):
    B, T, D = x.shape
    Fc = w_in.shape[2]
    F = 2 * Fc
    w_in = w_in.reshape(2, 2, D, Fc)
    w_out = w_out.reshape(2, Fc, D)
    tm = _tile(T, 256)

    def body(x_ref, dres_ref, y_ref, mod_ref, g_ref, wi_ref, wo_ref,
             dx_ref, h_ref, a_ref, dgu_ref, dy_ref, dmod_ref, dg_ref, h_s, dy_s, dh_s):
        t, f = pl.program_id(1), pl.program_id(2)

        @pl.when(f == 0)
        def _():
            h = _modnorm(x_ref[...], g_ref[...], mod_ref[1:2, :], mod_ref[0:1, :]).astype(BF16)
            h_s[...] = h
            h_ref[...] = h
            dres = dres_ref[...]
            dy = (0.5 * (1.0 + mod_ref[2:3, :]) * dres).astype(BF16)
            dy_s[...] = dy
            dy_ref[...] = dy
            dh_s[...] = jnp.zeros_like(dh_s)
            dgate = _sum0(dres * (0.5 * y_ref[...]))

            @pl.when(t == 0)
            def _():
                dmod_ref[...] = jnp.zeros_like(dmod_ref)
                dg_ref[...] = jnp.zeros_like(dg_ref)

            dmod_ref[2:3, :] += dgate

        h = h_s[...]
        dy = dy_s[...]
        gt = _mm(h, wi_ref[0])
        up = _mm(h, wi_ref[1])
        sg = _sigmoid(gt)
        silu = gt * sg
        a_ref[...] = (silu * up).astype(BF16)
        da = _mm_nt(dy, wo_ref[...])
        dup = (da * silu).astype(BF16)
        dgt = (da * up * _dsilu(gt, sg)).astype(BF16)
        dgu_ref[0] = dgt
        dgu_ref[1] = dup
        dh_s[...] += _mm_nt(dgt, wi_ref[0]) + _mm_nt(dup, wi_ref[1])

        @pl.when(f == 1)
        def _():
            dxn, dg, dscale, dshift = _modnorm_bwd(x_ref[...], g_ref[...], mod_ref[1:2, :], dh_s[...])
            dx_ref[...] = dres_ref[...] + dxn
            dmod_ref[0:1, :] += dshift
            dmod_ref[1:2, :] += dscale
            dg_ref[...] += dg

    tok = pl.BlockSpec((None, tm, D), lambda b, t, f: (b, t, 0))
    per_b3 = pl.BlockSpec((None, 3, D), lambda b, t, f: (b, 0, 0))
    return pl.pallas_call(
        body, name="ffn_bwd", grid=(B, T // tm, 2),
        in_specs=[tok, tok, tok, per_b3,
                  pl.BlockSpec((1, D), lambda b, t, f: (0, 0)),
                  pl.BlockSpec((2, None, D, Fc), lambda b, t, f: (0, f, 0, 0)),
                  pl.BlockSpec((None, Fc, D), lambda b, t, f: (f, 0, 0))],
        out_specs=[tok, tok,
                   pl.BlockSpec((None, tm, Fc), lambda b, t, f: (b, t, f)),
                   pl.BlockSpec((2, None, tm, Fc), lambda b, t, f: (0, b, t, f)),
                   tok, per_b3,
                   pl.BlockSpec((None, 1, D), lambda b, t, f: (b, 0, 0))],
        out_shape=[jax.ShapeDtypeStruct((B, T, D), F32), jax.ShapeDtypeStruct((B, T, D), BF16),
                   jax.ShapeDtypeStruct((B, T, F), BF16), jax.ShapeDtypeStruct((2, B, T, F), BF16),
                   jax.ShapeDtypeStruct((B, T, D), BF16), jax.ShapeDtypeStruct((B, 3, D), F32),
                   jax.ShapeDtypeStruct((B, 1, D), F32)],
        scratch_shapes=[pltpu.VMEM((tm, D), BF16), pltpu.VMEM((tm, D), BF16), pltpu.VMEM((tm, D), F32)],
        compiler_params=_cparams(3),
    )(x, dres, y, mod3, g, w_in, w_out)


def matmul_tn(xm, ym, bm, name):
    N, K = xm.shape
    GY, _, MY = ym.shape
    per = MY // bm
    nb = GY * per
    tn = _tile(N, 512)

    def body(x_ref, y_ref, o_ref, acc_s):
        n = pl.program_id(1)

        @pl.when(n == 0)
        def _():
            acc_s[...] = jnp.zeros_like(acc_s)

        acc_s[...] += _mm_tn(x_ref[...], y_ref[...])

        @pl.when(n == N // tn - 1)
        def _():
            o_ref[...] = acc_s[...].astype(BF16)

    return pl.pallas_call(
        body, name=name, grid=(nb, N // tn),
        in_specs=[pl.BlockSpec((tn, K), lambda m, n: (n, 0)),
                  pl.BlockSpec((None, tn, bm), lambda m, n: (m // per, n, m % per))],
        out_specs=pl.BlockSpec((None, K, bm), lambda m, n: (m, 0, 0)),
        out_shape=jax.ShapeDtypeStruct((nb, K, bm), BF16),
        scratch_shapes=[pltpu.VMEM((K, bm), F32)],
        compiler_params=_cparams(2),
    )(xm, ym)


def final_loss(x, fg, target):
    B, T, D = x.shape
    tm = _tile(T, 512)

    def body(x_ref, g_ref, t_ref, dx_ref, dfg_ref, loss_ref):
        t = pl.program_id(1)

        @pl.when(t == 0)
        def _():
            dfg_ref[...] = jnp.zeros_like(dfg_ref)
            loss_ref[...] = jnp.zeros_like(loss_ref)

        xv = x_ref[...]
        g = g_ref[...]
        r = lax.rsqrt(jnp.mean(xv * xv, axis=-1, keepdims=True) + EPS)
        xn = xv * r
        err = xn * g - t_ref[...]
        tok_loss = jnp.mean(err * err, axis=-1, keepdims=True)
        loss_ref[...] += 0.5 * jnp.sum(tok_loss, axis=0, keepdims=True)
        dy = err * (1.0 / D)
        dfg_ref[...] += _sum0(dy * xn)
        dxn = dy * g
        dx_ref[...] = r * (dxn - xn * jnp.mean(dxn * xn, axis=-1, keepdims=True))

    tok = pl.BlockSpec((None, tm, D), lambda b, t: (b, t, 0))
    return pl.pallas_call(
        body, name="final_loss", grid=(B, T // tm),
        in_specs=[tok, pl.BlockSpec((1, D), lambda b, t: (0, 0)), tok],
        out_specs=[tok, pl.BlockSpec((None, 1, D), lambda b, t: (b, 0, 0)),
                   pl.BlockSpec((None, 1, LANES), lambda b, t: (b, 0, 0))],
        out_shape=[jax.ShapeDtypeStruct((B, T, D), F32), jax.ShapeDtypeStruct((B, 1, D), F32),
                   jax.ShapeDtypeStruct((B, 1, LANES), F32)],
        compiler_params=_cparams(2),
    )(x, fg, target)


def _past_halo_spec(tm, halo, width):
    return pl.BlockSpec((None, halo, width), lambda b, t: (b, jnp.maximum(t * (tm // halo) - 1, 0), 0))


def _future_halo_spec(tm, halo, width, T):
    return pl.BlockSpec((None, halo, width), lambda b, t: (b, jnp.minimum((t + 1) * (tm // halo), T // halo - 1), 0))


def _glu_fwd(h, w_ref, bias):
    D = h.shape[1]
    a = jnp.concatenate([_mm(h, w_ref[0]), _mm(h, w_ref[1])], axis=1) + bias[:, :D]
    b = jnp.concatenate([_mm(h, w_ref[2]), _mm(h, w_ref[3])], axis=1) + bias[:, D:]
    return a, b


def conv_glu_fwd(x, mod3, g, w_glu, b_glu):
    B, T, D = x.shape
    tm = _tile(T, 512)

    def body(x_ref, mod_ref, g_ref, w_ref, b_ref, u_ref):
        h = _modnorm(x_ref[...], g_ref[...], mod_ref[1:2, :], mod_ref[0:1, :]).astype(BF16)
        a, b = _glu_fwd(h, w_ref, b_ref[...])
        u_ref[...] = a * _sigmoid(b)

    tok = pl.BlockSpec((None, tm, D), lambda b, t: (b, t, 0))
    return pl.pallas_call(
        body, name="conv_glu_fwd", grid=(B, T // tm),
        in_specs=[tok, pl.BlockSpec((None, 3, D), lambda b, t: (b, 0, 0)),
                  pl.BlockSpec((1, D), lambda b, t: (0, 0)),
                  pl.BlockSpec((4, D, D // 2), lambda b, t: (0, 0, 0)),
                  pl.BlockSpec((1, 2 * D), lambda b, t: (0, 0))],
        out_specs=tok, out_shape=jax.ShapeDtypeStruct((B, T, D), F32),
        compiler_params=_cparams(2),
    )(x, mod3, g, w_glu, b_glu)


def _layer_norm_parts(u2):
    mu = jnp.mean(u2, axis=-1, keepdims=True)
    xc = u2 - mu
    rs = lax.rsqrt(jnp.mean(xc * xc, axis=-1, keepdims=True) + EPS)
    return xc * rs, rs


def conv_out_fwd(x, u, mod3, w_dw, b_dw, ln_g, ln_b, w_pw, b_pw):
    B, T, D = x.shape
    K = w_dw.shape[0] - 1
    tm = _tile(T, 512)

    def body(x_ref, u_ref, halo_ref, mod_ref, wdw_ref, bdw_ref, lg_ref, lb_ref, wpw_ref, bpw_ref,
             xo_ref, y_ref, u2_ref, ext_s):
        t = pl.program_id(1)
        ext_s[0:CONV_HALO, :] = jnp.where(t > 0, halo_ref[...], 0.0)
        ext_s[CONV_HALO:, :] = u_ref[...]
        acc = jnp.broadcast_to(bdw_ref[...], (tm, D))
        for k in range(K):
            acc = acc + wdw_ref[k:k + 1, :] * ext_s[pl.ds(CONV_HALO - (K - 1) + k, tm), :]
        u2_ref[...] = acc
        xh, _ = _layer_norm_parts(acc)
        l = xh * lg_ref[...] + lb_ref[...]
        u3 = l * _sigmoid(l)
        y = _mm(u3, wpw_ref[...]) + bpw_ref[...]
        y_ref[...] = y
        xo_ref[...] = x_ref[...] + (1.0 + mod_ref[2:3, :]) * y

    tok = pl.BlockSpec((None, tm, D), lambda b, t: (b, t, 0))
    vec = pl.BlockSpec((1, D), lambda b, t: (0, 0))
    return pl.pallas_call(
        body, name="conv_out_fwd", grid=(B, T // tm),
        in_specs=[tok, tok, _past_halo_spec(tm, CONV_HALO, D), pl.BlockSpec((None, 3, D), lambda b, t: (b, 0, 0)),
                  pl.BlockSpec((K + 1, D), lambda b, t: (0, 0)), vec, vec, vec,
                  pl.BlockSpec((D, D), lambda b, t: (0, 0)), vec],
        out_specs=[tok, tok, tok], out_shape=[jax.ShapeDtypeStruct((B, T, D), F32)] * 3,
        scratch_shapes=[pltpu.VMEM((tm + CONV_HALO, D), F32)],
        compiler_params=_cparams(2),
    )(x, u, u, mod3, w_dw, b_dw, ln_g, ln_b, w_pw, b_pw)


def conv_out_bwd(dres, y, u2, mod3, ln_g, ln_b, w_pw):
    B, T, D = dres.shape
    tm = _tile(T, 512)

    def body(dres_ref, y_ref, u2_ref, mod_ref, lg_ref, lb_ref, wpw_ref, du2_ref, u3_ref, dy_ref, dgate_ref, vec_ref):
        t = pl.program_id(1)

        @pl.when(t == 0)
        def _():
            dgate_ref[...] = jnp.zeros_like(dgate_ref)
            vec_ref[...] = jnp.zeros_like(vec_ref)

        dres = dres_ref[...]
        dy = (1.0 + mod_ref[2:3, :]) * dres
        dy_ref[...] = dy.astype(BF16)
        dgate_ref[...] += _sum0(dres * y_ref[...])
        xh, rs = _layer_norm_parts(u2_ref[...])
        lg = lg_ref[...]
        l = xh * lg + lb_ref[...]
        sg = _sigmoid(l)
        u3_ref[...] = (l * sg).astype(BF16)
        du3 = _mm_nt(dy, wpw_ref[...])
        dl = du3 * _dsilu(l, sg)
        dxh = dl * lg
        du2 = rs * (dxh - jnp.mean(dxh, axis=-1, keepdims=True) - xh * jnp.mean(dxh * xh, axis=-1, keepdims=True))
        du2_ref[...] = du2
        vec_ref[0:1, :] += _sum0(dy)
        vec_ref[1:2, :] += _sum0(dl * xh)
        vec_ref[2:3, :] += _sum0(dl)
        vec_ref[3:4, :] += _sum0(du2)

    tok = pl.BlockSpec((None, tm, D), lambda b, t: (b, t, 0))
    tokb = pl.BlockSpec((None, tm, D), lambda b, t: (b, t, 0))
    vec = pl.BlockSpec((1, D), lambda b, t: (0, 0))
    return pl.pallas_call(
        body, name="conv_out_bwd", grid=(B, T // tm),
        in_specs=[tok, tok, tok, pl.BlockSpec((None, 3, D), lambda b, t: (b, 0, 0)), vec, vec,
                  pl.BlockSpec((D, D), lambda b, t: (0, 0))],
        out_specs=[tok, tokb, tokb, pl.BlockSpec((None, 1, D), lambda b, t: (b, 0, 0)),
                   pl.BlockSpec((None, 4, D), lambda b, t: (b, 0, 0))],
        out_shape=[jax.ShapeDtypeStruct((B, T, D), F32), jax.ShapeDtypeStruct((B, T, D), BF16),
                   jax.ShapeDtypeStruct((B, T, D), BF16), jax.ShapeDtypeStruct((B, 1, D), F32),
                   jax.ShapeDtypeStruct((B, 4, D), F32)],
        compiler_params=_cparams(2),
    )(dres, y, u2, mod3, ln_g, ln_b, w_pw)


def conv_glu_bwd(x, dres, du2, u, mod3, g, w_glu, b_glu, w_dw):
    B, T, D = x.shape
    K = w_dw.shape[0] - 1
    tm = _tile(T, 256)
    nt = T // tm

    def body(x_ref, dres_ref, du2_ref, du2h_ref, u_ref, uh_ref, mod_ref, g_ref, w_ref, b_ref, wdw_ref,
             dx_ref, h_ref, dab_ref, dwdw_ref, dbglu_ref, dmod_ref, dg_ref, extu_s, extd_s):
        t = pl.program_id(1)

        @pl.when(t == 0)
        def _():
            dwdw_ref[...] = jnp.zeros_like(dwdw_ref)
            dbglu_ref[...] = jnp.zeros_like(dbglu_ref)
            dmod_ref[...] = jnp.zeros_like(dmod_ref)
            dg_ref[...] = jnp.zeros_like(dg_ref)

        du2 = du2_ref[...]
        extu_s[0:CONV_HALO, :] = jnp.where(t > 0, uh_ref[...], 0.0)
        extu_s[CONV_HALO:, :] = u_ref[...]
        extd_s[0:tm, :] = du2
        extd_s[tm:, :] = jnp.where(t < nt - 1, du2h_ref[...], 0.0)
        du = jnp.zeros((tm, D), F32)
        for k in range(K):
            du = du + wdw_ref[k:k + 1, :] * extd_s[pl.ds(K - 1 - k, tm), :]
            dwdw_ref[k:k + 1, :] += _sum0(du2 * extu_s[pl.ds(CONV_HALO - (K - 1) + k, tm), :])
        xv = x_ref[...]
        h = _modnorm(xv, g_ref[...], mod_ref[1:2, :], mod_ref[0:1, :]).astype(BF16)
        h_ref[...] = h
        a, b = _glu_fwd(h, w_ref, b_ref[...])
        sb = _sigmoid(b)
        da = du * sb
        db = du * a * sb * (1.0 - sb)
        dbglu_ref[:, 0:D] += _sum0(da)
        dbglu_ref[:, D:] += _sum0(db)
        da = da.astype(BF16)
        db = db.astype(BF16)
        dab_ref[:, 0:D] = da
        dab_ref[:, D:] = db
        Dh2 = D // 2
        dh = (_mm_nt(da[:, :Dh2], w_ref[0]) + _mm_nt(da[:, Dh2:], w_ref[1])
              + _mm_nt(db[:, :Dh2], w_ref[2]) + _mm_nt(db[:, Dh2:], w_ref[3]))
        dxn, dg, dscale, dshift = _modnorm_bwd(xv, g_ref[...], mod_ref[1:2, :], dh)
        dx_ref[...] = dres_ref[...] + dxn
        dmod_ref[0:1, :] += dshift
        dmod_ref[1:2, :] += dscale
        dg_ref[...] += dg

    tok = pl.BlockSpec((None, tm, D), lambda b, t: (b, t, 0))
    return pl.pallas_call(
        body, name="conv_glu_bwd", grid=(B, nt),
        in_specs=[tok, tok, tok, _future_halo_spec(tm, CONV_HALO, D, T), tok, _past_halo_spec(tm, CONV_HALO, D),
                  pl.BlockSpec((None, 3, D), lambda b, t: (b, 0, 0)), pl.BlockSpec((1, D), lambda b, t: (0, 0)),
                  pl.BlockSpec((4, D, D // 2), lambda b, t: (0, 0, 0)), pl.BlockSpec((1, 2 * D), lambda b, t: (0, 0)),
                  pl.BlockSpec((K + 1, D), lambda b, t: (0, 0))],
        out_specs=[tok, tok, pl.BlockSpec((None, tm, 2 * D), lambda b, t: (b, t, 0)),
                   pl.BlockSpec((None, K + 1, D), lambda b, t: (b, 0, 0)),
                   pl.BlockSpec((None, 1, 2 * D), lambda b, t: (b, 0, 0)),
                   pl.BlockSpec((None, 3, D), lambda b, t: (b, 0, 0)),
                   pl.BlockSpec((None, 1, D), lambda b, t: (b, 0, 0))],
        out_shape=[jax.ShapeDtypeStruct((B, T, D), F32), jax.ShapeDtypeStruct((B, T, D), BF16),
                   jax.ShapeDtypeStruct((B, T, 2 * D), BF16), jax.ShapeDtypeStruct((B, K + 1, D), F32),
                   jax.ShapeDtypeStruct((B, 1, 2 * D), F32), jax.ShapeDtypeStruct((B, 3, D), F32),
                   jax.ShapeDtypeStruct((B, 1, D), F32)],
        scratch_shapes=[pltpu.VMEM((tm + CONV_HALO, D), F32), pltpu.VMEM((tm + CONV_HALO, D), F32)],
        compiler_params=_cparams(2),
    )(x, dres, du2, du2, u, u, mod3, g, w_glu, b_glu, w_dw)


def dn_proj_fwd(x, mod3, g, w_main, w_ab):
    B, T, D = x.shape
    W = w_main.shape[1] // 4
    tm = _tile(T, 512)

    def body(x_ref, mod_ref, g_ref, wm_ref, wab_ref, pre_ref, z_ref, ab_ref):
        h = _modnorm(x_ref[...], g_ref[...], mod_ref[1:2, :], mod_ref[0:1, :]).astype(BF16)
        for p in range(3):
            pre_ref[:, p * W:(p + 1) * W] = _mm(h, wm_ref[:, p * W:(p + 1) * W])
        z_ref[...] = _mm(h, wm_ref[:, 3 * W:])
        ab_ref[...] = _mm(h, wab_ref[...])

    return pl.pallas_call(
        body, name="dn_proj_fwd", grid=(B, T // tm),
        in_specs=[pl.BlockSpec((None, tm, D), lambda b, t: (b, t, 0)), pl.BlockSpec((None, 3, D), lambda b, t: (b, 0, 0)),
                  pl.BlockSpec((1, D), lambda b, t: (0, 0)), pl.BlockSpec((D, 4 * W), lambda b, t: (0, 0)),
                  pl.BlockSpec((D, LANES), lambda b, t: (0, 0))],
        out_specs=[pl.BlockSpec((None, tm, 3 * W), lambda b, t: (b, t, 0)),
                   pl.BlockSpec((None, tm, W), lambda b, t: (b, t, 0)),
                   pl.BlockSpec((None, tm, LANES), lambda b, t: (b, t, 0))],
        out_shape=[jax.ShapeDtypeStruct((B, T, 3 * W), F32), jax.ShapeDtypeStruct((B, T, W), F32),
                   jax.ShapeDtypeStruct((B, T, LANES), F32)],
        compiler_params=_cparams(2),
    )(x, mod3, g, w_main, w_ab)


def _sconv(ext_s, w_ref, tm, K):
    acc = w_ref[0:1, :] * ext_s[pl.ds(SCONV_HALO - (K - 1), tm), :]
    for k in range(1, K):
        acc = acc + w_ref[k:k + 1, :] * ext_s[pl.ds(SCONV_HALO - (K - 1) + k, tm), :]
    return acc


def _lane_col(val, lane, idx):
    return jnp.sum(jnp.where(lane == idx, val, 0.0), axis=1, keepdims=True)


def dn_conv_fwd(pre, ab, w_sconv, alog_row, dt_row, H):
    B, T, W3 = pre.shape
    W = W3 // 3
    Dh = W // H
    K = w_sconv.shape[0]
    tm = _tile(T, 512)

    def body(pre_ref, halo_ref, ab_ref, w_ref, alog_ref, dt_ref, q_ref, k_ref, v_ref, gb_ref, bb_ref, ext_s):
        t = pl.program_id(1)
        ext_s[0:SCONV_HALO, :] = jnp.where(t > 0, halo_ref[...], 0.0)
        ext_s[SCONV_HALO:, :] = pre_ref[...]
        cv = _sconv(ext_s, w_ref, tm, K)
        qkv = cv * _sigmoid(cv)
        ab = ab_ref[...]
        lane = lax.broadcasted_iota(jnp.int32, ab.shape, 1)
        g_all = -jnp.exp(alog_ref[...]) * _softplus(ab + dt_ref[...])
        beta_all = _sigmoid(ab)
        for h in range(H):
            q_ref[h] = qkv[:, h * Dh:(h + 1) * Dh]
            k_ref[h] = qkv[:, W + h * Dh:W + (h + 1) * Dh]
            v_ref[h] = qkv[:, 2 * W + h * Dh:2 * W + (h + 1) * Dh]
            gb_ref[h] = jnp.broadcast_to(_lane_col(g_all, lane, h), (tm, Dh))
            bb_ref[h] = jnp.broadcast_to(_lane_col(beta_all, lane, H + h), (tm, Dh))

    hm = pl.BlockSpec((None, H, tm, Dh), lambda b, t: (b, 0, t, 0))
    row = pl.BlockSpec((1, LANES), lambda b, t: (0, 0))
    return pl.pallas_call(
        body, name="dn_conv_fwd", grid=(B, T // tm),
        in_specs=[pl.BlockSpec((None, tm, W3), lambda b, t: (b, t, 0)), _past_halo_spec(tm, SCONV_HALO, W3),
                  pl.BlockSpec((None, tm, LANES), lambda b, t: (b, t, 0)),
                  pl.BlockSpec((K, W3), lambda b, t: (0, 0)), row, row],
        out_specs=[hm] * 5, out_shape=[jax.ShapeDtypeStruct((B, H, T, Dh), F32)] * 5,
        scratch_shapes=[pltpu.VMEM((tm + SCONV_HALO, W3), F32)],
        compiler_params=_cparams(2),
    )(pre, pre, ab, w_sconv, alog_row, dt_row)


def _bdot(spec):
    return lambda a, b: jnp.einsum(spec, a.astype(BF16), b.astype(BF16), preferred_element_type=F32)


_NN, _NT, _TN = "gij,gjk->gik", "gik,gjk->gij", "gki,gkj->gij"


def _make_bdots():
    nn_, nt_, tn_ = _bdot(_NN), _bdot(_NT), _bdot(_TN)

    @jax.custom_vjp
    def nn(a, b):
        return nn_(a, b)

    @jax.custom_vjp
    def nt(a, b):
        return nt_(a, b)

    @jax.custom_vjp
    def tn(a, b):
        return tn_(a, b)

    nn.defvjp(lambda a, b: (nn_(a, b), (a, b)), lambda r, d: (nt_(d, r[1]), tn_(r[0], d)))
    nt.defvjp(lambda a, b: (nt_(a, b), (a, b)), lambda r, d: (nn_(d, r[1]), tn_(d, r[0])))
    tn.defvjp(lambda a, b: (tn_(a, b), (a, b)), lambda r, d: (nt_(r[1], d), nn_(r[0], d)))
    return nn, nt, tn


def _chunk_fn(q, k, v, gb, bb, S):
    nn, nt, tn = _make_bdots()
    G, C, Dh = q.shape
    hdot = functools.partial(jnp.einsum, precision=HI, preferred_element_type=F32)
    q = q * lax.rsqrt(jnp.sum(q * q, axis=-1, keepdims=True) + EPS) * (Dh ** -0.5)
    k = k * lax.rsqrt(jnp.sum(k * k, axis=-1, keepdims=True) + EPS)
    row = lax.broadcasted_iota(jnp.int32, (G, C, C), 1)
    col = lax.broadcasted_iota(jnp.int32, (G, C, C), 2)
    causal = row >= col
    strict = row > col
    eye = (row == col).astype(F32)
    gc = hdot(_NN, causal.astype(F32), gb)
    spread = jnp.full((G, C, Dh), 1.0 / Dh, F32)
    gi = hdot(_NT, gc, spread)
    gj = hdot(_NT, spread, gc)
    decay = jnp.where(causal, jnp.exp(jnp.where(causal, gi - gj, 0.0)), 0.0)
    kb = k * bb
    vb = v * bb
    A = jnp.where(strict, nt(kb, k) * decay, 0.0)
    Tm = eye - A
    Ap = A
    steps = max(1, (C - 1).bit_length()) - 1
    for _ in range(steps):
        Ap = hdot(_NN, Ap, Ap)
        Tm = Tm + hdot(_NN, Tm, Ap)
    eg = jnp.exp(gc)
    u = nn(Tm, vb)
    w = nn(Tm, kb * eg)
    qg = q * eg
    intra = nt(q, k) * decay
    glast = hdot(_NN, jnp.ones((G, C, C), F32), gb)
    kd = k * jnp.exp(glast - gc)
    v_new = u - nn(w, S)
    o = nn(qg, S) + nn(intra, v_new)
    egl = jnp.exp(glast)
    S_new = S * jnp.concatenate([egl] * (Dh // C), axis=1) + tn(kd, v_new)
    return o, S_new


def dn_chunk_fwd(q, k, v, gb, bb):
    B, H, T, Dh = q.shape
    NC = T // CHUNK

    def body(q_ref, k_ref, v_ref, gb_ref, bb_ref, o_ref, sp_ref, S_s):
        @pl.when(pl.program_id(1) == 0)
        def _():
            S_s[...] = jnp.zeros_like(S_s)

        S = S_s[...]
        sp_ref[...] = S
        o, S_new = _chunk_fn(q_ref[...], k_ref[...], v_ref[...], gb_ref[...], bb_ref[...], S)
        o_ref[...] = o
        S_s[...] = S_new

    hm = pl.BlockSpec((None, H, CHUNK, Dh), lambda b, n: (b, 0, n, 0))
    return pl.pallas_call(
        body, name="dn_chunk_fwd", grid=(B, NC),
        in_specs=[hm] * 5,
        out_specs=[hm, pl.BlockSpec((None, None, H, Dh, Dh), lambda b, n: (b, n, 0, 0, 0))],
        out_shape=[jax.ShapeDtypeStruct((B, H, T, Dh), F32), jax.ShapeDtypeStruct((B, NC, H, Dh, Dh), F32)],
        scratch_shapes=[pltpu.VMEM((H, Dh, Dh), F32)],
        compiler_params=_cparams(2),
    )(q, k, v, gb, bb)


def dn_chunk_bwd(q, k, v, gb, bb, s_prev, do):
    B, H, T, Dh = q.shape
    NC = T // CHUNK

    def body(q_ref, k_ref, v_ref, gb_ref, bb_ref, sp_ref, do_ref, dq_ref, dk_ref, dv_ref, dgb_ref, dbb_ref, dS_s):
        @pl.when(pl.program_id(1) == 0)
        def _():
            dS_s[...] = jnp.zeros_like(dS_s)

        _, vjp = jax.vjp(_chunk_fn, q_ref[...], k_ref[...], v_ref[...], gb_ref[...], bb_ref[...], sp_ref[...])
        dq, dk, dv, dgb, dbb, dS = vjp((do_ref[...], dS_s[...]))
        dq_ref[...] = dq
        dk_ref[...] = dk
        dv_ref[...] = dv
        dgb_ref[...] = dgb
        dbb_ref[...] = dbb
        dS_s[...] = dS

    hm = pl.BlockSpec((None, H, CHUNK, Dh), lambda b, n: (b, 0, NC - 1 - n, 0))
    return pl.pallas_call(
        body, name="dn_chunk_bwd", grid=(B, NC),
        in_specs=[hm] * 5 + [pl.BlockSpec((None, None, H, Dh, Dh), lambda b, n: (b, NC - 1 - n, 0, 0, 0)), hm],
        out_specs=[hm] * 5, out_shape=[jax.ShapeDtypeStruct((B, H, T, Dh), F32)] * 5,
        scratch_shapes=[pltpu.VMEM((H, Dh, Dh), F32)],
        compiler_params=_cparams(2),
    )(q, k, v, gb, bb, s_prev, do)


def _head_norm(o, og):
    r = lax.rsqrt(jnp.mean(o * o, axis=-1, keepdims=True) + EPS)
    return o * r, r


def dn_out_fwd(x, o, z, mod3, o_g, w_out):
    B, T, D = x.shape
    _, H, _, Dh = o.shape
    W = H * Dh
    tm = _tile(T, 512)

    def body(x_ref, o_ref, z_ref, mod_ref, og_ref, w_ref, xo_ref, y_ref):
        parts = []
        for h in range(H):
            on, _ = _head_norm(o_ref[h], og_ref[...])
            zz = z_ref[:, h * Dh:(h + 1) * Dh]
            parts.append((on * og_ref[...] * (zz * _sigmoid(zz))).astype(BF16))
        y = _mm(jnp.concatenate(parts, axis=1), w_ref[...])
        y_ref[...] = y
        xo_ref[...] = x_ref[...] + (1.0 + mod_ref[2:3, :]) * y

    tok = pl.BlockSpec((None, tm, D), lambda b, t: (b, t, 0))
    return pl.pallas_call(
        body, name="dn_out_fwd", grid=(B, T // tm),
        in_specs=[tok, pl.BlockSpec((None, H, tm, Dh), lambda b, t: (b, 0, t, 0)),
                  pl.BlockSpec((None, tm, W), lambda b, t: (b, t, 0)), pl.BlockSpec((None, 3, D), lambda b, t: (b, 0, 0)),
                  pl.BlockSpec((1, Dh), lambda b, t: (0, 0)), pl.BlockSpec((W, D), lambda b, t: (0, 0))],
        out_specs=[tok, tok], out_shape=[jax.ShapeDtypeStruct((B, T, D), F32)] * 2,
        compiler_params=_cparams(2),
    )(x, o, z, mod3, o_g, w_out)


def dn_out_bwd(dres, y, o, z, mod3, o_g, w_out):
    B, T, D = dres.shape
    _, H, _, Dh = o.shape
    W = H * Dh
    tm = _tile(T, 512)

    def body(dres_ref, y_ref, o_ref, z_ref, mod_ref, og_ref, w_ref, do_ref, dz_ref, ogb_ref, dy_ref, dgate_ref, dog_ref):
        t = pl.program_id(1)

        @pl.when(t == 0)
        def _():
            dgate_ref[...] = jnp.zeros_like(dgate_ref)
            dog_ref[...] = jnp.zeros_like(dog_ref)

        dres = dres_ref[...]
        dy = ((1.0 + mod_ref[2:3, :]) * dres).astype(BF16)
        dy_ref[...] = dy
        dgate_ref[...] += _sum0(dres * y_ref[...])
        dog = _mm_nt(dy, w_ref[...])
        og = og_ref[...]
        for h in range(H):
            ov = o_ref[h]
            xn, r = _head_norm(ov, og)
            zz = z_ref[:, h * Dh:(h + 1) * Dh]
            sg = _sigmoid(zz)
            sz = zz * sg
            d = dog[:, h * Dh:(h + 1) * Dh]
            ogb_ref[:, h * Dh:(h + 1) * Dh] = (xn * og * sz).astype(BF16)
            dz_ref[:, h * Dh:(h + 1) * Dh] = d * (xn * og) * _dsilu(zz, sg)
            don = d * sz
            dog_ref[...] += _sum0(don * xn)
            dxn = don * og
            do_ref[h] = r * (dxn - xn * jnp.mean(dxn * xn, axis=-1, keepdims=True))

    tok = pl.BlockSpec((None, tm, D), lambda b, t: (b, t, 0))
    tokw = pl.BlockSpec((None, tm, W), lambda b, t: (b, t, 0))
    hm = pl.BlockSpec((None, H, tm, Dh), lambda b, t: (b, 0, t, 0))
    return pl.pallas_call(
        body, name="dn_out_bwd", grid=(B, T // tm),
        in_specs=[tok, tok, hm, tokw, pl.BlockSpec((None, 3, D), lambda b, t: (b, 0, 0)),
                  pl.BlockSpec((1, Dh), lambda b, t: (0, 0)), pl.BlockSpec((W, D), lambda b, t: (0, 0))],
        out_specs=[hm, tokw, tokw, tok, pl.BlockSpec((None, 1, D), lambda b, t: (b, 0, 0)),
                   pl.BlockSpec((None, 1, Dh), lambda b, t: (b, 0, 0))],
        out_shape=[jax.ShapeDtypeStruct((B, H, T, Dh), F32), jax.ShapeDtypeStruct((B, T, W), F32),
                   jax.ShapeDtypeStruct((B, T, W), BF16), jax.ShapeDtypeStruct((B, T, D), BF16),
                   jax.ShapeDtypeStruct((B, 1, D), F32), jax.ShapeDtypeStruct((B, 1, Dh), F32)],
        compiler_params=_cparams(2),
    )(dres, y, o, z, mod3, o_g, w_out)


def dn_conv_bwd(dq, dk, dv, dgb, dbb, pre, ab, w_sconv, alog_row, dt_row):
    B, H, T, Dh = dq.shape
    W = H * Dh
    W3 = 3 * W
    K = w_sconv.shape[0]
    tm = _tile(T, 256)

    def body(dq_ref, dk_ref, dv_ref, dgb_ref, dbb_ref, pre_ref, halo_ref, ab_ref, w_ref, alog_ref, dt_ref,
             dc_ref, dab_ref, small_ref, ext_s):
        t = pl.program_id(1)

        @pl.when(t == 0)
        def _():
            small_ref[...] = jnp.zeros_like(small_ref)

        ext_s[0:SCONV_HALO, :] = jnp.where(t > 0, halo_ref[...], 0.0)
        ext_s[SCONV_HALO:, :] = pre_ref[...]
        cv = _sconv(ext_s, w_ref, tm, K)
        dsl = _dsilu(cv, _sigmoid(cv))
        ab = ab_ref[...]
        lane = lax.broadcasted_iota(jnp.int32, ab.shape, 1)
        dg_all = jnp.zeros_like(ab)
        db_all = jnp.zeros_like(ab)
        for h in range(H):
            dc_ref[:, h * Dh:(h + 1) * Dh] = dq_ref[h] * dsl[:, h * Dh:(h + 1) * Dh]
            dc_ref[:, W + h * Dh:W + (h + 1) * Dh] = dk_ref[h] * dsl[:, W + h * Dh:W + (h + 1) * Dh]
            dc_ref[:, 2 * W + h * Dh:2 * W + (h + 1) * Dh] = dv_ref[h] * dsl[:, 2 * W + h * Dh:2 * W + (h + 1) * Dh]
            dg_all = dg_all + jnp.where(lane == h, jnp.sum(dgb_ref[h], axis=1, keepdims=True), 0.0)
            db_all = db_all + jnp.where(lane == H + h, jnp.sum(dbb_ref[h], axis=1, keepdims=True), 0.0)
        xa = ab + dt_ref[...]
        ea = -jnp.exp(alog_ref[...])
        g_all = ea * _softplus(xa)
        da = dg_all * ea * _sigmoid(xa)
        beta = _sigmoid(ab)
        dab_ref[...] = da + db_all * beta * (1.0 - beta)
        small_ref[0:1, :] += _sum0(dg_all * g_all)
        small_ref[1:2, :] += _sum0(da)

    hm = pl.BlockSpec((None, H, tm, Dh), lambda b, t: (b, 0, t, 0))
    row = pl.BlockSpec((1, LANES), lambda b, t: (0, 0))
    return pl.pallas_call(
        body, name="dn_conv_bwd", grid=(B, T // tm),
        in_specs=[hm] * 5 + [pl.BlockSpec((None, tm, W3), lambda b, t: (b, t, 0)), _past_halo_spec(tm, SCONV_HALO, W3),
                             pl.BlockSpec((None, tm, LANES), lambda b, t: (b, t, 0)),
                             pl.BlockSpec((K, W3), lambda b, t: (0, 0)), row, row],
        out_specs=[pl.BlockSpec((None, tm, W3), lambda b, t: (b, t, 0)), pl.BlockSpec((None, tm, LANES), lambda b, t: (b, t, 0)),
                   pl.BlockSpec((None, 2, LANES), lambda b, t: (b, 0, 0))],
        out_shape=[jax.ShapeDtypeStruct((B, T, W3), F32), jax.ShapeDtypeStruct((B, T, LANES), F32),
                   jax.ShapeDtypeStruct((B, 2, LANES), F32)],
        scratch_shapes=[pltpu.VMEM((tm + SCONV_HALO, W3), F32)],
        compiler_params=_cparams(2),
    )(dq, dk, dv, dgb, dbb, pre, pre, ab, w_sconv, alog_row, dt_row)


def dn_proj_bwd(x, dres, dc, pre, dz, dab, mod3, g, w_main, w_ab, w_sconv):
    B, T, D = x.shape
    W3 = dc.shape[2]
    W = W3 // 3
    K = w_sconv.shape[0]
    tm = _tile(T, 256)
    nt = T // tm

    def body(x_ref, dres_ref, dc_ref, dch_ref, pre_ref, preh_ref, dz_ref, dab_ref, mod_ref, g_ref, wm_ref, wab_ref, ws_ref,
             dx_ref, h_ref, dproj_ref, dws_ref, dmod_ref, dg_ref, extp_s, extd_s):
        t = pl.program_id(1)

        @pl.when(t == 0)
        def _():
            dws_ref[...] = jnp.zeros_like(dws_ref)
            dmod_ref[...] = jnp.zeros_like(dmod_ref)
            dg_ref[...] = jnp.zeros_like(dg_ref)

        dc = dc_ref[...]
        extp_s[0:SCONV_HALO, :] = jnp.where(t > 0, preh_ref[...], 0.0)
        extp_s[SCONV_HALO:, :] = pre_ref[...]
        extd_s[0:tm, :] = dc
        extd_s[tm:, :] = jnp.where(t < nt - 1, dch_ref[...], 0.0)
        dpre = jnp.zeros((tm, W3), F32)
        for k in range(K):
            dpre = dpre + ws_ref[k:k + 1, :] * extd_s[pl.ds(K - 1 - k, tm), :]
            dws_ref[k:k + 1, :] += _sum0(dc * extp_s[pl.ds(SCONV_HALO - (K - 1) + k, tm), :])
        dpre = dpre.astype(BF16)
        dzb = dz_ref[...].astype(BF16)
        dproj_ref[:, 0:W3] = dpre
        dproj_ref[:, W3:] = dzb
        dh = _mm_nt(dab_ref[...], wab_ref[...]) + _mm_nt(dzb, wm_ref[:, W3:])
        for p in range(3):
            dh = dh + _mm_nt(dpre[:, p * W:(p + 1) * W], wm_ref[:, p * W:(p + 1) * W])
        xv = x_ref[...]
        h_ref[...] = _modnorm(xv, g_ref[...], mod_ref[1:2, :], mod_ref[0:1, :]).astype(BF16)
        dxn, dg, dscale, dshift = _modnorm_bwd(xv, g_ref[...], mod_ref[1:2, :], dh)
        dx_ref[...] = dres_ref[...] + dxn
        dmod_ref[0:1, :] += dshift
        dmod_ref[1:2, :] += dscale
        dg_ref[...] += dg

    tok = pl.BlockSpec((None, tm, D), lambda b, t: (b, t, 0))
    tok3 = pl.BlockSpec((None, tm, W3), lambda b, t: (b, t, 0))
    return pl.pallas_call(
        body, name="dn_proj_bwd", grid=(B, nt),
        in_specs=[tok, tok, tok3, _future_halo_spec(tm, SCONV_HALO, W3, T), tok3, _past_halo_spec(tm, SCONV_HALO, W3),
                  pl.BlockSpec((None, tm, W), lambda b, t: (b, t, 0)), pl.BlockSpec((None, tm, LANES), lambda b, t: (b, t, 0)),
                  pl.BlockSpec((None, 3, D), lambda b, t: (b, 0, 0)), pl.BlockSpec((1, D), lambda b, t: (0, 0)),
                  pl.BlockSpec((D, 4 * W), lambda b, t: (0, 0)), pl.BlockSpec((D, LANES), lambda b, t: (0, 0)),
                  pl.BlockSpec((K, W3), lambda b, t: (0, 0))],
        out_specs=[tok, tok, pl.BlockSpec((None, tm, 4 * W), lambda b, t: (b, t, 0)),
                   pl.BlockSpec((None, K, W3), lambda b, t: (b, 0, 0)), pl.BlockSpec((None, 3, D), lambda b, t: (b, 0, 0)),
                   pl.BlockSpec((None, 1, D), lambda b, t: (b, 0, 0))],
        out_shape=[jax.ShapeDtypeStruct((B, T, D), F32), jax.ShapeDtypeStruct((B, T, D), BF16),
                   jax.ShapeDtypeStruct((B, T, 4 * W), BF16), jax.ShapeDtypeStruct((B, K, W3), F32),
                   jax.ShapeDtypeStruct((B, 3, D), F32), jax.ShapeDtypeStruct((B, 1, D), F32)],
        scratch_shapes=[pltpu.VMEM((tm + SCONV_HALO, W3), F32), pltpu.VMEM((tm + SCONV_HALO, W3), F32)],
        compiler_params=_cparams(2),
    )(x, dres, dc, dc, pre, pre, dz, dab, mod3, g, w_main, w_ab, w_sconv)


def ada_fwd(c_all, w_ada, b_cols):
    L, D, Ca = w_ada.shape
    NB = c_all.shape[0]

    def body(c_ref, w_ref, b_ref, o_ref):
        cv = c_ref[...]
        o_ref[...] = _mm(cv * _sigmoid(cv), w_ref[...]) + b_ref[...]

    return pl.pallas_call(
        body, name="ada_fwd", grid=(L,),
        in_specs=[pl.BlockSpec((NB, D), lambda i: (0, 0)), pl.BlockSpec((None, D, Ca), lambda i: (i, 0, 0)),
                  pl.BlockSpec((None, 1, Ca), lambda i: (i, 0, 0))],
        out_specs=pl.BlockSpec((None, NB, Ca), lambda i: (i, 0, 0)),
        out_shape=jax.ShapeDtypeStruct((L, NB, Ca), F32),
        compiler_params=_cparams(1),
    )(c_all, w_ada, b_cols)


def ada_bwd(c_all, dmod_cols, dmod_all):
    L, NB, Ca = dmod_cols.shape
    D = c_all.shape[1]
    C9 = dmod_all.shape[2]

    def body(c_ref, dc_ref, da_ref, gw_ref, gb_ref):
        cv = c_ref[...]
        gw_ref[...] = _mm_tn(cv * _sigmoid(cv), dc_ref[...])
        gb_ref[...] = _sum0(da_ref[...])

    return pl.pallas_call(
        body, name="ada_bwd", grid=(L,),
        in_specs=[pl.BlockSpec((NB, D), lambda i: (0, 0)), pl.BlockSpec((None, NB, Ca), lambda i: (i, 0, 0)),
                  pl.BlockSpec((None, NB, C9), lambda i: (i, 0, 0))],
        out_specs=[pl.BlockSpec((None, D, Ca), lambda i: (i, 0, 0)), pl.BlockSpec((None, 1, C9), lambda i: (i, 0, 0))],
        out_shape=[jax.ShapeDtypeStruct((L, D, Ca), F32), jax.ShapeDtypeStruct((L, 1, C9), F32)],
        compiler_params=_cparams(1),
    )(c_all, dmod_cols, dmod_all)


def adamw(w, g, m, v, name):
    R, C = w.shape
    tr = _tile(R, max(8, (1 << 18) // C))

    def body(w_ref, g_ref, m_ref, v_ref, d_ref, mo_ref, vo_ref):
        gv = g_ref[...]
        mn = ADAM_B1 * m_ref[...] + (1.0 - ADAM_B1) * gv
        vn = ADAM_B2 * v_ref[...] + (1.0 - ADAM_B2) * (gv * gv)
        m_hat = mn / (1.0 - ADAM_B1 ** ADAM_STEP)
        v_hat = vn / (1.0 - ADAM_B2 ** ADAM_STEP)
        d_ref[...] = -ADAM_LR * (m_hat / (jnp.sqrt(v_hat) + ADAM_EPS) + ADAM_WD * w_ref[...])
        mo_ref[...] = mn
        vo_ref[...] = vn

    blk = pl.BlockSpec((tr, C), lambda i: (i, 0))
    return pl.pallas_call(
        body, name=name, grid=(R // tr,), in_specs=[blk] * 4, out_specs=[blk] * 3,
        out_shape=[jax.ShapeDtypeStruct((R, C), F32)] * 3, compiler_params=_cparams(1),
    )(w, g, m, v)


def sum_devices(a):
    n, R, C = a.shape

    def body(a_ref, o_ref):
        s = a_ref[0]
        for d in range(1, n):
            s = s + a_ref[d]
        o_ref[...] = s

    return pl.pallas_call(
        body, name="sum_devices", out_shape=jax.ShapeDtypeStruct((R, C), F32),
        compiler_params=pltpu.CompilerParams(vmem_limit_bytes=VMEM_LIMIT_V7X),
    )(a)


def _place():
    x, y, c = lax.axis_index("x"), lax.axis_index("y"), lax.axis_index("c")
    return x, y, c


def _other_chips(x, y):
    return [(2 * (1 - x) + y, 1 - x, y), (2 * x + (1 - y), x, 1 - y), (2 * (1 - x) + (1 - y), 1 - x, 1 - y)]


def allgather8(block):
    m_per, n = block.shape

    def body(x_ref, out_ref, send_sems, recv_sems, local_sem):
        x, y, c = _place()
        me, sibling = (x, y, c), (x, y, 1 - c)
        chips = [(1 - x, y), (x, 1 - y), (1 - x, 1 - y)]

        def rows(px, py, pc):
            return out_ref.at[pl.ds((4 * px + 2 * py + pc) * m_per, m_per), :]

        def copy(k, blk, to, src=None):
            return pltpu.make_async_remote_copy(
                src_ref=rows(*blk) if src is None else src, dst_ref=rows(*blk),
                send_sem=send_sems.at[k], recv_sem=recv_sems.at[k], device_id=to, device_id_type=MESH)

        mine = pltpu.make_async_copy(x_ref, rows(*me), local_sem)
        mine.start()
        first = [copy(0, me, sibling, src=x_ref)]
        first += [copy(1 + j, me, (*chip, c), src=x_ref) for j, chip in enumerate(chips)]
        for cp in first:
            cp.start()
        passed = [copy(4 + j, (*chip, c), sibling) for j, chip in enumerate(chips)]
        for j, chip in enumerate(chips):
            copy(1 + j, (*chip, c), me).wait_recv()
            passed[j].start()
        copy(0, sibling, me).wait_recv()
        for j, chip in enumerate(chips):
            copy(4 + j, (*chip, 1 - c), me).wait_recv()
        for cp in first + passed:
            cp.wait_send()
        mine.wait()

    return pl.pallas_call(
        body, name="allgather8", out_shape=jax.ShapeDtypeStruct((N_DEV * m_per, n), block.dtype),
        in_specs=[pl.BlockSpec(memory_space=pltpu.VMEM)], out_specs=pl.BlockSpec(memory_space=pltpu.VMEM),
        scratch_shapes=[pltpu.SemaphoreType.DMA((7,)), pltpu.SemaphoreType.DMA((7,)), pltpu.SemaphoreType.DMA],
        compiler_params=pltpu.CompilerParams(vmem_limit_bytes=VMEM_LIMIT_V7X),
    )(block)


def _half(ref, c, rh):
    return ref.at[pl.ds(pl.multiple_of(c * rh, 16), rh), :]


def gather_weights(shards):
    K = len(shards)

    def body(*refs):
        ins, outs = refs[:K], refs[K:2 * K]
        ici_send, ici_recv, d2d_send, d2d_recv, local_sems = refs[2 * K:]
        x, y, c = _place()
        me = 2 * x + y
        sibling = (x, y, 1 - c)
        others = _other_chips(x, y)
        local, sent = [], []
        for k in range(K):
            rh = ins[k].shape[0] // 2
            own = pltpu.make_async_copy(ins[k], outs[k].at[me], local_sems.at[k])
            own.start()
            local.append(own)
            for r, (_, px, py) in enumerate(others):
                cp = pltpu.make_async_remote_copy(
                    src_ref=_half(ins[k], c, rh), dst_ref=_half(outs[k].at[me], c, rh),
                    send_sem=ici_send.at[k, r], recv_sem=ici_recv.at[k, r], device_id=(px, py, c), device_id_type=MESH)
                cp.start()
                sent.append(cp)
        forwards = []
        for k in range(K):
            rh = ins[k].shape[0] // 2
            for r, (pchip, px, py) in enumerate(others):
                landed = _half(outs[k].at[pchip], c, rh)
                pltpu.make_async_remote_copy(
                    src_ref=landed, dst_ref=landed, send_sem=ici_send.at[k, r], recv_sem=ici_recv.at[k, r],
                    device_id=(px, py, c), device_id_type=MESH).wait_recv()
                fw = pltpu.make_async_remote_copy(
                    src_ref=landed, dst_ref=landed, send_sem=d2d_send.at[k, r], recv_sem=d2d_recv.at[k, r],
                    device_id=sibling, device_id_type=MESH)
                fw.start()
                forwards.append(fw)
        for k in range(K):
            rh = ins[k].shape[0] // 2
            for r, (pchip, _, _) in enumerate(others):
                theirs = _half(outs[k].at[pchip], 1 - c, rh)
                pltpu.make_async_remote_copy(
                    src_ref=theirs, dst_ref=theirs, send_sem=d2d_send.at[k, r], recv_sem=d2d_recv.at[k, r],
                    device_id=sibling, device_id_type=MESH).wait_recv()
        for cp in sent + forwards:
            cp.wait_send()
        for own in local:
            own.wait()

    return pl.pallas_call(
        body, name="gather_weights",
        out_shape=[jax.ShapeDtypeStruct((N_CHIPS,) + s.shape, s.dtype) for s in shards],
        in_specs=[HBM_SPEC] * K, out_specs=[HBM_SPEC] * K,
        scratch_shapes=[pltpu.SemaphoreType.DMA((K, 3))] * 4 + [pltpu.SemaphoreType.DMA((K,))],
    )(*shards)


def pair_exchange(grads):
    K = len(grads)

    def body(*refs):
        ins, outs = refs[:K], refs[K:2 * K]
        send_sems, recv_sems = refs[2 * K:]
        x, y, c = _place()
        sibling = (x, y, 1 - c)
        copies = []
        for k in range(K):
            n, r, _ = ins[k].shape
            rh = r // 2
            cp = pltpu.make_async_remote_copy(
                src_ref=ins[k].at[:, pl.ds(pl.multiple_of((1 - c) * rh, 16), rh), :], dst_ref=outs[k],
                send_sem=send_sems.at[k], recv_sem=recv_sems.at[k], device_id=sibling, device_id_type=MESH)
            cp.start()
            copies.append(cp)
        for cp in copies:
            cp.wait_recv()
        for cp in copies:
            cp.wait_send()

    return pl.pallas_call(
        body, name="pair_exchange",
        out_shape=[jax.ShapeDtypeStruct((g.shape[0], g.shape[1] // 2, g.shape[2]), g.dtype) for g in grads],
        in_specs=[HBM_SPEC] * K, out_specs=[HBM_SPEC] * K,
        scratch_shapes=[pltpu.SemaphoreType.DMA((K,))] * 2,
    )(*grads)


def pair_add(grad, recv, c_idx):
    n, r, C = grad.shape
    rh = r // 2
    tr = _tile(rh, max(16, (1 << 19) // C), 16)
    grad = grad.reshape(n, 2, rh, C)

    def body(c_ref, g_ref, r_ref, o_ref):
        o_ref[...] = (g_ref[...].astype(F32) + r_ref[...].astype(F32)).astype(BF16)

    return pl.pallas_call(
        body, name="pair_add",
        grid_spec=pltpu.PrefetchScalarGridSpec(
            num_scalar_prefetch=1, grid=(n, rh // tr),
            in_specs=[pl.BlockSpec((None, None, tr, C), lambda d, i, c_ref: (d, c_ref[0], i, 0)),
                      pl.BlockSpec((None, tr, C), lambda d, i, c_ref: (d, i, 0))],
            out_specs=pl.BlockSpec((None, tr, C), lambda d, i, c_ref: (d, i, 0))),
        out_shape=jax.ShapeDtypeStruct((n, rh, C), BF16), compiler_params=_cparams(2),
    )(c_idx, grad, recv)


def chip_exchange(parts):
    K = len(parts)

    def body(*refs):
        ins, outs = refs[:K], refs[K:2 * K]
        send_sems, recv_sems, local_sems = refs[2 * K:]
        x, y, c = _place()
        me = 2 * x + y
        others = _other_chips(x, y)
        started = []
        for k in range(K):
            own = pltpu.make_async_copy(ins[k].at[me], outs[k].at[me], local_sems.at[k])
            own.start()
            started.append(own)
            for r, (pchip, px, py) in enumerate(others):
                cp = pltpu.make_async_remote_copy(
                    src_ref=ins[k].at[pchip], dst_ref=outs[k].at[me], send_sem=send_sems.at[k, r],
                    recv_sem=recv_sems.at[k, r], device_id=(px, py, c), device_id_type=MESH)
                cp.start()
                started.append(cp)
        for k in range(K):
            for r, (pchip, px, py) in enumerate(others):
                pltpu.make_async_remote_copy(
                    src_ref=outs[k].at[pchip], dst_ref=outs[k].at[pchip], send_sem=send_sems.at[k, r],
                    recv_sem=recv_sems.at[k, r], device_id=(px, py, c), device_id_type=MESH).wait_recv()
        i = 0
        for k in range(K):
            started[i].wait()
            for r in range(3):
                started[i + 1 + r].wait_send()
            i += 4

    return pl.pallas_call(
        body, name="chip_exchange",
        out_shape=[jax.ShapeDtypeStruct(p.shape, p.dtype) for p in parts],
        in_specs=[HBM_SPEC] * K, out_specs=[HBM_SPEC] * K,
        scratch_shapes=[pltpu.SemaphoreType.DMA((K, 3))] * 2 + [pltpu.SemaphoreType.DMA((K,))],
    )(*parts)


def chip_sum(parts):
    n, rh, C = parts.shape
    tr = _tile(rh, max(16, (1 << 19) // C), 16)

    def body(p_ref, o_ref):
        s = p_ref[0].astype(F32)
        for d in range(1, n):
            s = s + p_ref[d].astype(F32)
        o_ref[...] = s

    return pl.pallas_call(
        body, name="chip_sum", grid=(rh // tr,),
        in_specs=[pl.BlockSpec((n, tr, C), lambda i: (0, i, 0))], out_specs=pl.BlockSpec((tr, C), lambda i: (i, 0)),
        out_shape=jax.ShapeDtypeStruct((rh, C), F32), compiler_params=_cparams(1),
    )(parts)


def pair_share(halves):
    K = len(halves)

    def body(*refs):
        ins, outs = refs[:K], refs[K:2 * K]
        send_sems, recv_sems, local_sems = refs[2 * K:]
        x, y, c = _place()
        sibling = (x, y, 1 - c)
        started = []
        for k in range(K):
            rh = ins[k].shape[0]
            own = pltpu.make_async_copy(ins[k], _half(outs[k], c, rh), local_sems.at[k])
            own.start()
            cp = pltpu.make_async_remote_copy(
                src_ref=ins[k], dst_ref=_half(outs[k], c, rh), send_sem=send_sems.at[k], recv_sem=recv_sems.at[k],
                device_id=sibling, device_id_type=MESH)
            cp.start()
            started.append((own, cp))
        for k in range(K):
            rh = ins[k].shape[0]
            theirs = _half(outs[k], 1 - c, rh)
            pltpu.make_async_remote_copy(
                src_ref=theirs, dst_ref=theirs, send_sem=send_sems.at[k], recv_sem=recv_sems.at[k],
                device_id=sibling, device_id_type=MESH).wait_recv()
        for own, cp in started:
            own.wait()
            cp.wait_send()

    return pl.pallas_call(
        body, name="pair_share",
        out_shape=[jax.ShapeDtypeStruct((2 * h.shape[0], h.shape[1]), h.dtype) for h in halves],
        in_specs=[HBM_SPEC] * K, out_specs=[HBM_SPEC] * K,
        scratch_shapes=[pltpu.SemaphoreType.DMA((K,))] * 3,
    )(*halves)


def reduce_scatter(grads, c_idx):
    recv = pair_exchange(grads)
    parts = [pair_add(g, r, c_idx) for g, r in zip(grads, recv)]
    got = chip_exchange(parts)
    halves = [chip_sum(p) for p in got]
    return pair_share(halves)


def _pack(arrs):
    flat = jnp.concatenate([a.reshape(-1).astype(F32) for a in arrs])
    pad = (-flat.shape[0]) % (8 * LANES)
    return jnp.pad(flat, (0, pad)).reshape(-1, LANES)


def _unpack(flat, shapes):
    out, off = [], 0
    for s in shapes:
        n = 1
        for d in s:
            n *= d
        out.append(flat[off:off + n].reshape(s))
        off += n
    return out


def _adamw_any(w, g, m, v, name):
    shp = w.shape
    C = shp[-1]
    d, nm, nv = adamw(w.reshape(-1, C), g.reshape(-1, C), m.reshape(-1, C), v.reshape(-1, C), name)
    return d.reshape(shp), nm.reshape(shp), nv.reshape(shp)


def kernel(x, c, norm_g, w_ada, b_ada, w_ffn_in, w_ffn_out, cm_w_glu, cm_b_glu, cm_w_dw, cm_b_dw, cm_ln_g, cm_ln_b, cm_w_pw, cm_b_pw, dn_w_in, dn_w_sconv, dn_a_log, dn_dt_bias, dn_o_g, dn_w_out, final_g, loss_target, m_norm_g, m_w_ada, m_b_ada, m_w_ffn_in, m_w_ffn_out, m_cm_w_glu, m_cm_b_glu, m_cm_w_dw, m_cm_b_dw, m_cm_ln_g, m_cm_ln_b, m_cm_w_pw, m_cm_b_pw, m_dn_w_in, m_dn_w_sconv, m_dn_a_log, m_dn_dt_bias, m_dn_o_g, m_dn_w_out, m_final_g, v_norm_g, v_w_ada, v_b_ada, v_w_ffn_in, v_w_ffn_out, v_cm_w_glu, v_cm_b_glu, v_cm_w_dw, v_cm_b_dw, v_cm_ln_g, v_cm_ln_b, v_cm_w_pw, v_cm_b_pw, v_dn_w_in, v_dn_w_sconv, v_dn_a_log, v_dn_dt_bias, v_dn_o_g, v_dn_w_out, v_final_g):
    weights = dict(norm_g=norm_g, w_ada=w_ada, b_ada=b_ada, w_ffn_in=w_ffn_in, w_ffn_out=w_ffn_out, cm_w_glu=cm_w_glu,
                   cm_b_glu=cm_b_glu, cm_w_dw=cm_w_dw, cm_b_dw=cm_b_dw, cm_ln_g=cm_ln_g, cm_ln_b=cm_ln_b, cm_w_pw=cm_w_pw,
                   cm_b_pw=cm_b_pw, dn_w_in=dn_w_in, dn_w_sconv=dn_w_sconv, dn_a_log=dn_a_log, dn_dt_bias=dn_dt_bias,
                   dn_o_g=dn_o_g, dn_w_out=dn_w_out, final_g=final_g)
    mom_m = dict(norm_g=m_norm_g, w_ada=m_w_ada, b_ada=m_b_ada, w_ffn_in=m_w_ffn_in, w_ffn_out=m_w_ffn_out,
                 cm_w_glu=m_cm_w_glu, cm_b_glu=m_cm_b_glu, cm_w_dw=m_cm_w_dw, cm_b_dw=m_cm_b_dw, cm_ln_g=m_cm_ln_g,
                 cm_ln_b=m_cm_ln_b, cm_w_pw=m_cm_w_pw, cm_b_pw=m_cm_b_pw, dn_w_in=m_dn_w_in, dn_w_sconv=m_dn_w_sconv,
                 dn_a_log=m_dn_a_log, dn_dt_bias=m_dn_dt_bias, dn_o_g=m_dn_o_g, dn_w_out=m_dn_w_out, final_g=m_final_g)
    mom_v = dict(norm_g=v_norm_g, w_ada=v_w_ada, b_ada=v_b_ada, w_ffn_in=v_w_ffn_in, w_ffn_out=v_w_ffn_out,
                 cm_w_glu=v_cm_w_glu, cm_b_glu=v_cm_b_glu, cm_w_dw=v_cm_w_dw, cm_b_dw=v_cm_b_dw, cm_ln_g=v_cm_ln_g,
                 cm_ln_b=v_cm_ln_b, cm_w_pw=v_cm_w_pw, cm_b_pw=v_cm_b_pw, dn_w_in=v_dn_w_in, dn_w_sconv=v_dn_w_sconv,
                 dn_a_log=v_dn_a_log, dn_dt_bias=v_dn_dt_bias, dn_o_g=v_dn_o_g, dn_w_out=v_dn_w_out, final_g=v_final_g)
    names = list(weights)

    BL, T, D = x.shape
    L = norm_g.shape[0]
    NB = BL * N_DEV
    Ca = w_ada.shape[2]
    C9 = b_ada.shape[1]
    H = dn_a_log.shape[1]
    Dh = dn_o_g.shape[1]
    W = H * Dh
    KC = cm_w_dw.shape[1]
    KS = dn_w_sconv.shape[1]
    n_cm, n_dn = cm_w_glu.shape[0], dn_w_in.shape[0]
    ax, ay, ac = lax.axis_index("x"), lax.axis_index("y"), lax.axis_index("c")
    chip = 2 * ax + ay
    dev = 2 * chip + ac
    c_idx = ac.astype(jnp.int32).reshape(1)

    small_in = [c, norm_g, cm_w_dw, dn_w_sconv]
    packed = _pack(small_in)
    gathered = allgather8(packed).reshape(N_DEV, -1)
    per_dev = [_unpack(gathered[d], [a.shape for a in small_in]) for d in range(N_DEV)]
    c_all = jnp.concatenate([p[0] for p in per_dev], axis=0)
    norm_g_full = jnp.concatenate([per_dev[2 * s][1] for s in range(N_CHIPS)], axis=-1)
    w_dw_full = jnp.concatenate([per_dev[2 * s][2] for s in range(N_CHIPS)], axis=-1)
    w_sconv_full = jnp.concatenate([per_dev[2 * s][3] for s in range(N_CHIPS)], axis=-1)

    b_cols = lax.dynamic_slice_in_dim(b_ada, chip * Ca, Ca, axis=1).reshape(L, 1, Ca)
    mod_part = ada_fwd(c_all, w_ada, b_cols)
    mod_g = allgather8(mod_part.reshape(-1, LANES)).reshape(N_DEV, L, NB, Ca)
    mod_all = jnp.concatenate([mod_g[2 * s] for s in range(N_CHIPS)], axis=-1)
    mod = lax.dynamic_slice_in_dim(mod_all, dev * BL, BL, axis=1).reshape(L, BL, 9, D)

    def layer_shards(i):
        sh = [w_ffn_in[i, 0], w_ffn_in[i, 1], w_ffn_out[i, 0], w_ffn_out[i, 1]]
        if i % 2 == 0:
            sh += [cm_w_glu[i // 2], cm_w_pw[i // 2]]
        else:
            sh += [dn_w_in[i // 2], dn_w_out[i // 2]]
        return [s.astype(BF16) for s in sh]

    wts = [gather_weights(layer_shards(i)) for i in range(L)]

    def dn_weights(i):
        full = jnp.transpose(wts[i][4], (1, 0, 2)).reshape(D, -1)
        return full[:, :4 * W], jnp.pad(full[:, 4 * W:], ((0, 0), (0, LANES - 2 * H)))

    def row128(v):
        return jnp.pad(v.reshape(1, -1), ((0, 0), (0, LANES - v.shape[-1])))

    def pad_taps(w):
        return jnp.pad(w, ((0, 1), (0, 0)))

    saved = []
    xs = x
    for i in range(L):
        wl = wts[i]
        sv = {}
        m3 = [mod[i, :, 3 * j:3 * j + 3] for j in range(3)]
        gs = [norm_g_full[i, j].reshape(1, D) for j in range(3)]
        sv["x0"] = xs
        xs, sv["y0"] = ffn_fwd(xs, m3[0], gs[0], wl[0], wl[2])
        sv["x1"] = xs
        if i % 2 == 0:
            a = i // 2
            sv["u"] = conv_glu_fwd(xs, m3[1], gs[1], wl[4], cm_b_glu[a].reshape(1, -1))
            xs, sv["y1"], sv["u2"] = conv_out_fwd(
                xs, sv["u"], m3[1], pad_taps(w_dw_full[a]), cm_b_dw[a].reshape(1, D), cm_ln_g[a].reshape(1, D),
                cm_ln_b[a].reshape(1, D), wl[5].reshape(D, D), cm_b_pw[a].reshape(1, D))
        else:
            a = i // 2
            w_main, w_ab = dn_weights(i)
            sv["pre"], sv["z"], sv["ab"] = dn_proj_fwd(xs, m3[1], gs[1], w_main, w_ab)
            qkvgb = dn_conv_fwd(sv["pre"], sv["ab"], w_sconv_full[a], row128(dn_a_log[a]), row128(dn_dt_bias[a]), H)
            sv["qkvgb"] = qkvgb
            sv["o"], sv["sp"] = dn_chunk_fwd(*qkvgb)
            xs, sv["y1"] = dn_out_fwd(xs, sv["o"], sv["z"], m3[1], dn_o_g[a].reshape(1, Dh), wl[5].reshape(W, D))
        sv["x2"] = xs
        xs, sv["y2"] = ffn_fwd(xs, m3[2], gs[2], wl[1], wl[3])
        saved.append(sv)

    dx, d_final_g, loss_part = final_loss(xs, final_g.reshape(1, D), loss_target)

    g_small = {n: None for n in names}
    d_norm_g = [[None] * 3 for _ in range(L)]
    dmod = [[None] * 3 for _ in range(L)]
    g_cm = {k: [None] * n_cm for k in ("b_glu", "w_dw", "b_dw", "ln_g", "ln_b", "b_pw")}
    g_dn = {k: [None] * n_dn for k in ("w_sconv", "a_log", "dt_bias", "o_g")}
    big = [None] * L

    def ffn_back(i, j, slot, dx):
        wl, sv = wts[i], saved[i]
        m3 = mod[i, :, 3 * j:3 * j + 3]
        g = norm_g_full[i, j].reshape(1, D)
        dx, hb, ab_, dgu, dyb, dm, dg = ffn_bwd(sv["x%d" % j], dx, sv["y%d" % j], m3, g, wl[slot], wl[2 + slot])
        dmod[i][j] = dm
        d_norm_g[i][j] = jnp.sum(dg, axis=(0, 1))
        Fc = wl[slot].shape[2]
        dw_in = matmul_tn(hb.reshape(-1, D), dgu.reshape(2, BL * T, 2 * Fc), Fc, "dw_ffn_in")
        dw_out = matmul_tn(ab_.reshape(-1, 2 * Fc), dyb.reshape(1, -1, D), D, "dw_ffn_out")
        return dx, dw_in, dw_out.reshape(N_CHIPS, -1, D)

    for i in reversed(range(L)):
        wl, sv = wts[i], saved[i]
        a = i // 2
        dx, dw_in1, dw_out1 = ffn_back(i, 2, 1, dx)
        m3 = mod[i, :, 3:6]
        g = norm_g_full[i, 1].reshape(1, D)
        if i % 2 == 0:
            w_pw = wl[5].reshape(D, D)
            wdw = pad_taps(w_dw_full[a])
            du2, u3b, dyb, dgate, vec = conv_out_bwd(dx, sv["y1"], sv["u2"], m3, cm_ln_g[a].reshape(1, D),
                                                     cm_ln_b[a].reshape(1, D), w_pw)
            dx, hb, dab, dwdw, dbglu, dm, dg = conv_glu_bwd(sv["x1"], dx, du2, sv["u"], m3, g, wl[4],
                                                            cm_b_glu[a].reshape(1, -1), wdw)
            dm = dm.at[:, 2:3, :].set(dgate)
            vec = jnp.sum(vec, axis=0)
            g_cm["b_pw"][a], g_cm["ln_g"][a], g_cm["ln_b"][a], g_cm["b_dw"][a] = vec[0], vec[1], vec[2], vec[3]
            g_cm["w_dw"][a] = jnp.sum(dwdw, axis=0)[:KC]
            g_cm["b_glu"][a] = jnp.sum(dbglu, axis=(0, 1))
            dw_a = matmul_tn(hb.reshape(-1, D), dab.reshape(1, -1, 2 * D), D // 2, "dw_glu")
            dw_b = matmul_tn(u3b.reshape(-1, D), dyb.reshape(1, -1, D), D, "dw_sq").reshape(N_CHIPS, -1, D)
        else:
            w_main, w_ab = dn_weights(i)
            w_out = wl[5].reshape(W, D)
            do, dz, ogb, dyb, dgate, dog = dn_out_bwd(dx, sv["y1"], sv["o"], sv["z"], m3, dn_o_g[a].reshape(1, Dh), w_out)
            dq, dk, dv, dgb, dbb = dn_chunk_bwd(*sv["qkvgb"], sv["sp"], do)
            dc, dab, small = dn_conv_bwd(dq, dk, dv, dgb, dbb, sv["pre"], sv["ab"], w_sconv_full[a],
                                         row128(dn_a_log[a]), row128(dn_dt_bias[a]))
            dx, hb, dproj, dws, dm, dg = dn_proj_bwd(sv["x1"], dx, dc, sv["pre"], dz, dab, m3, g, w_main, w_ab,
                                                     w_sconv_full[a])
            dm = dm.at[:, 2:3, :].set(dgate)
            small = jnp.sum(small, axis=0)
            g_dn["a_log"][a], g_dn["dt_bias"][a] = small[0, :H], small[1, :H]
            g_dn["o_g"][a] = jnp.sum(dog, axis=(0, 1))
            g_dn["w_sconv"][a] = jnp.sum(dws, axis=0)
            dw_main = matmul_tn(hb.reshape(-1, D), dproj.reshape(1, -1, 4 * W), W, "dw_dn_main")
            dw_ab = matmul_tn(hb.reshape(-1, D), dab.reshape(1, -1, LANES), LANES, "dw_dn_ab")
            full = jnp.concatenate([jnp.transpose(dw_main, (1, 0, 2)).reshape(D, 4 * W), dw_ab[0][:, :2 * H]], axis=1)
            dw_a = jnp.transpose(full.reshape(D, N_CHIPS, -1), (1, 0, 2))
            dw_b = matmul_tn(ogb.reshape(-1, W), dyb.reshape(1, -1, D), D, "dw_sq").reshape(N_CHIPS, -1, D)
        dmod[i][1] = dm
        d_norm_g[i][1] = jnp.sum(dg, axis=(0, 1))
        dx, dw_in0, dw_out0 = ffn_back(i, 0, 0, dx)
        big[i] = reduce_scatter([dw_in0, dw_in1, dw_out0, dw_out1, dw_a, dw_b], c_idx)

    part = dict(
        norm_g=jnp.stack([jnp.stack(r) for r in d_norm_g]),
        cm_b_glu=jnp.stack(g_cm["b_glu"]), cm_w_dw=jnp.stack(g_cm["w_dw"]), cm_b_dw=jnp.stack(g_cm["b_dw"]),
        cm_ln_g=jnp.stack(g_cm["ln_g"]), cm_ln_b=jnp.stack(g_cm["ln_b"]), cm_b_pw=jnp.stack(g_cm["b_pw"]),
        dn_w_sconv=jnp.stack(g_dn["w_sconv"]), dn_a_log=jnp.stack(g_dn["a_log"]), dn_dt_bias=jnp.stack(g_dn["dt_bias"]),
        dn_o_g=jnp.stack(g_dn["o_g"]), final_g=jnp.sum(d_final_g, axis=(0, 1)),
        loss=jnp.sum(loss_part[:, 0, 0]).reshape(1))
    dmod_loc = jnp.stack([jnp.concatenate(r, axis=1) for r in dmod]).reshape(L, BL, C9)
    keys = list(part)
    packed = _pack([part[k] for k in keys] + [dmod_loc])
    R = packed.shape[0]
    gathered = allgather8(packed).reshape(N_DEV, R, LANES)
    summed = _unpack(sum_devices(gathered).reshape(-1), [part[k].shape for k in keys])
    tot = dict(zip(keys, summed))
    n_small = sum(int(part[k].size) for k in keys)
    dmod_all = gathered.reshape(N_DEV, -1)[:, n_small:n_small + L * BL * C9].reshape(N_DEV, L, BL, C9)
    dmod_all = jnp.transpose(dmod_all, (1, 0, 2, 3)).reshape(L, NB, C9)
    dmod_cols = lax.dynamic_slice_in_dim(dmod_all, chip * Ca, Ca, axis=2)
    g_w_ada, g_b_ada = ada_bwd(c_all, dmod_cols, dmod_all)

    def my_cols(full):
        n = full.shape[-1] // N_CHIPS
        return lax.dynamic_slice_in_dim(full, chip * n, n, axis=full.ndim - 1)

    grads = dict(
        norm_g=my_cols(tot["norm_g"]), w_ada=g_w_ada, b_ada=g_b_ada.reshape(L, C9),
        w_ffn_in=jnp.stack([jnp.stack([big[i][0], big[i][1]]) for i in range(L)]),
        w_ffn_out=jnp.stack([jnp.stack([big[i][2], big[i][3]]) for i in range(L)]),
        cm_w_glu=jnp.stack([big[i][4] for i in range(0, L, 2)]), cm_b_glu=tot["cm_b_glu"], cm_w_dw=my_cols(tot["cm_w_dw"]),
        cm_b_dw=tot["cm_b_dw"], cm_ln_g=tot["cm_ln_g"], cm_ln_b=tot["cm_ln_b"],
        cm_w_pw=jnp.stack([big[i][5] for i in range(0, L, 2)]), cm_b_pw=tot["cm_b_pw"],
        dn_w_in=jnp.stack([big[i][4] for i in range(1, L, 2)]), dn_w_sconv=my_cols(tot["dn_w_sconv"]),
        dn_a_log=tot["dn_a_log"], dn_dt_bias=tot["dn_dt_bias"], dn_o_g=tot["dn_o_g"],
        dn_w_out=jnp.stack([big[i][5] for i in range(1, L, 2)]), final_g=tot["final_g"])

    large = ("w_ada", "w_ffn_in", "w_ffn_out", "cm_w_glu", "cm_w_pw", "dn_w_in", "dn_w_out")
    delta, new_m, new_v = {}, {}, {}
    for n in large:
        delta[n], new_m[n], new_v[n] = _adamw_any(weights[n], grads[n], mom_m[n], mom_v[n], "adamw_" + n)
    rest = [n for n in names if n not in large]
    shapes = [weights[n].shape for n in rest]
    pd, pm, pv = adamw(_pack([weights[n] for n in rest]), _pack([grads[n] for n in rest]),
                       _pack([mom_m[n] for n in rest]), _pack([mom_v[n] for n in rest]), "adamw_small")
    for n, d_, m_, v_ in zip(rest, _unpack(pd.reshape(-1), shapes), _unpack(pm.reshape(-1), shapes),
                             _unpack(pv.reshape(-1), shapes)):
        delta[n], new_m[n], new_v[n] = d_, m_, v_

    return (tot["loss"].reshape(()), dx, *[grads[n] for n in names], *[delta[n] for n in names],
            *[new_m[n] for n in names], *[new_v[n] for n in names])
```

```python
import functools

import jax
import jax.numpy as jnp
from jax import lax
from jax.experimental import pallas as pl
from jax.experimental.pallas import tpu as pltpu

F32 = jnp.float32
BF16 = jnp.bfloat16
EPS = 1e-6
CHUNK = 64
N_CHIPS = 4
N_DEV = 8
LANES = 128
CONV_HALO = 32
SCONV_HALO = 8
VMEM_LIMIT_V7X = 60 * 1024 * 1024
HI = lax.Precision.HIGHEST
MESH = pl.DeviceIdType.MESH
HBM_SPEC = pl.BlockSpec(memory_space=pltpu.HBM)

ADAM_LR, ADAM_B1, ADAM_B2, ADAM_EPS, ADAM_WD, ADAM_STEP = 0.001, 0.9, 0.999, 1e-08, 0.01, 10


def _cparams(n_axes):
    return pltpu.CompilerParams(dimension_semantics=("arbitrary",) * n_axes, vmem_limit_bytes=VMEM_LIMIT_V7X)


def _tile(n, pref, mult=8):
    for t in range(min(n, pref) // mult * mult, 0, -mult):
        if n % t == 0:
            return t
    return n


def _mm(a, b):
    return lax.dot_general(a.astype(BF16), b.astype(BF16), (((1,), (0,)), ((), ())), preferred_element_type=F32)


def _mm_nt(a, b):
    return lax.dot_general(a.astype(BF16), b.astype(BF16), (((1,), (1,)), ((), ())), preferred_element_type=F32)


def _mm_tn(a, b):
    return lax.dot_general(a.astype(BF16), b.astype(BF16), (((0,), (0,)), ((), ())), preferred_element_type=F32)


def _sigmoid(x):
    return jax.nn.sigmoid(x)


def _dsilu(x, s):
    return s * (1.0 + x * (1.0 - s))


def _softplus(x):
    return jnp.maximum(x, 0.0) + jnp.log(1.0 + jnp.exp(-jnp.abs(x)))


def _modnorm(x, g, scale, shift):
    r = lax.rsqrt(jnp.mean(x * x, axis=-1, keepdims=True) + EPS)
    return (x * r) * g * (1.0 + scale) + shift


def _modnorm_bwd(x, g, scale, dh):
    r = lax.rsqrt(jnp.mean(x * x, axis=-1, keepdims=True) + EPS)
    xn = x * r
    dshift = jnp.sum(dh, axis=0, keepdims=True)
    dscale = jnp.sum(dh * (xn * g), axis=0, keepdims=True)
    dhn = dh * (1.0 + scale)
    dg = jnp.sum(dhn * xn, axis=0, keepdims=True)
    dxn = dhn * g
    dx = r * (dxn - xn * jnp.mean(dxn * xn, axis=-1, keepdims=True))
    return dx, dg, dscale, dshift


def _sum0(a):
    return jnp.sum(a, axis=0, keepdims=True)


def ffn_fwd(x, mod3, g, w_in, w_out):
    B, T, D = x.shape
    Fc = w_in.shape[2]
    w_in = w_in.reshape(2, 2, D, Fc)
    w_out = w_out.reshape(2, Fc, D)
    tm = _tile(T, 512)

    def body(x_ref, mod_ref, g_ref, wi_ref, wo_ref, xo_ref, y_ref, h_s, acc_s):
        f = pl.program_id(2)

        @pl.when(f == 0)
        def _():
            h = _modnorm(x_ref[...], g_ref[...], mod_ref[1:2, :], mod_ref[0:1, :])
            h_s[...] = h.astype(BF16)
            acc_s[...] = jnp.zeros_like(acc_s)

        h = h_s[...]
        gt = _mm(h, wi_ref[0])
        up = _mm(h, wi_ref[1])
        a = gt * _sigmoid(gt) * up
        acc_s[...] += _mm(a, wo_ref[...])

        @pl.when(f == 1)
        def _():
            y = acc_s[...]
            y_ref[...] = y
            xo_ref[...] = x_ref[...] + 0.5 * (1.0 + mod_ref[2:3, :]) * y

    tok = pl.BlockSpec((None, tm, D), lambda b, t, f: (b, t, 0))
    return pl.pallas_call(
        body, name="ffn_fwd", grid=(B, T // tm, 2),
        in_specs=[tok,
                  pl.BlockSpec((None, 3, D), lambda b, t, f: (b, 0, 0)),
                  pl.BlockSpec((1, D), lambda b, t, f: (0, 0)),
                  pl.BlockSpec((2, None, D, Fc), lambda b, t, f: (0, f, 0, 0)),
                  pl.BlockSpec((None, Fc, D), lambda b, t, f: (f, 0, 0))],
        out_specs=[tok, tok],
        out_shape=[jax.ShapeDtypeStruct((B, T, D), F32)] * 2,
        scratch_shapes=[pltpu.VMEM((tm, D), BF16), pltpu.VMEM((tm, D), F32)],
        compiler_params=_cparams(3),
    )(x, mod3, g, w_in, w_out)


def ffn_bwd(x, dres, y, mod3, g, w_in, w_out):
    B, T, D = x.shape
    Fc = w_in.shape[2]
    F = 2 * Fc
    w_in = w_in.reshape(2, 2, D, Fc)
    w_out = w_out.reshape(2, Fc, D)
    tm = _tile(T, 256)

    def body(x_ref, dres_ref, y_ref, mod_ref, g_ref, wi_ref, wo_ref,
             dx_ref, h_ref, a_ref, dgu_ref, dy_ref, dmod_ref, dg_ref, h_s, dy_s, dh_s):
        t, f = pl.program_id(1), pl.program_id(2)

        @pl.when(f == 0)
        def _():
            h = _modnorm(x_ref[...], g_ref[...], mod_ref[1:2, :], mod_ref[0:1, :]).astype(BF16)
            h_s[...] = h
            h_ref[...] = h
            dres = dres_ref[...]
            dy = (0.5 * (1.0 + mod_ref[2:3, :]) * dres).astype(BF16)
            dy_s[...] = dy
            dy_ref[...] = dy
            dh_s[...] = jnp.zeros_like(dh_s)
            dgate = _sum0(dres * (0.5 * y_ref[...]))

            @pl.when(t == 0)
            def _():
                dmod_ref[...] = jnp.zeros_like(dmod_ref)
                dg_ref[...] = jnp.zeros_like(dg_ref)

            dmod_ref[2:3, :] += dgate

        h = h_s[...]
        dy = dy_s[...]
        gt = _mm(h, wi_ref[0])
        up = _mm(h, wi_ref[1])
        sg = _sigmoid(gt)
        silu = gt * sg
        a_ref[...] = (silu * up).astype(BF16)
        da = _mm_nt(dy, wo_ref[...])
        dup = (da * silu).astype(BF16)
        dgt = (da * up * _dsilu(gt, sg)).astype(BF16)
        dgu_ref[0] = dgt
        dgu_ref[1] = dup
        dh_s[...] += _mm_nt(dgt, wi_ref[0]) + _mm_nt(dup, wi_ref[1])

        @pl.when(f == 1)
        def _():
            dxn, dg, dscale, dshift = _modnorm_bwd(x_ref[...], g_ref[...], mod_ref[1:2, :], dh_s[...])
            dx_ref[...] = dres_ref[...] + dxn
            dmod_ref[0:1, :] += dshift
            dmod_ref[1:2, :] += dscale
            dg_ref[...] += dg

    tok = pl.BlockSpec((None, tm, D), lambda b, t, f: (b, t, 0))
    per_b3 = pl.BlockSpec((None, 3, D), lambda b, t, f: (b, 0, 0))
    return pl.pallas_call(
        body, name="ffn_bwd", grid=(B, T // tm, 2),
        in_specs=[tok, tok, tok, per_b3,
                  pl.BlockSpec((1, D), lambda b, t, f: (0, 0)),
                  pl.BlockSpec((2, None, D, Fc), lambda b, t, f: (0, f, 0, 0)),
                  pl.BlockSpec((None, Fc, D), lambda b, t, f: (f, 0, 0))],
        out_specs=[tok, tok,
                   pl.BlockSpec((None, tm, Fc), lambda b, t, f: (b, t, f)),
                   pl.BlockSpec((2, None, tm, Fc), lambda b, t, f: (0, b, t, f)),
                   tok, per_b3,
                   pl.BlockSpec((None, 1, D), lambda b, t, f: (b, 0, 0))],
        out_shape=[jax.ShapeDtypeStruct((B, T, D), F32), jax.ShapeDtypeStruct((B, T, D), BF16),
                   jax.ShapeDtypeStruct((B, T, F), BF16), jax.ShapeDtypeStruct((2, B, T, F), BF16),
                   jax.ShapeDtypeStruct((B, T, D), BF16), jax.ShapeDtypeStruct((B, 3, D), F32),
                   jax.ShapeDtypeStruct((B, 1, D), F32)],
        scratch_shapes=[pltpu.VMEM((tm, D), BF16), pltpu.VMEM((tm, D), BF16), pltpu.VMEM((tm, D), F32)],
        compiler_params=_cparams(3),
    )(x, dres, y, mod3, g, w_in, w_out)


def matmul_tn(xm, ym, bm, name):
    N, K = xm.shape
    GY, _, MY = ym.shape
    per = MY // bm
    nb = GY * per
    tn = _tile(N, 512)

    def body(x_ref, y_ref, o_ref, acc_s):
        n = pl.program_id(1)

        @pl.when(n == 0)
        def _():
            acc_s[...] = jnp.zeros_like(acc_s)

        acc_s[...] += _mm_tn(x_ref[...], y_ref[...])

        @pl.when(n == N // tn - 1)
        def _():
            o_ref[...] = acc_s[...].astype(BF16)

    return pl.pallas_call(
        body, name=name, grid=(nb, N // tn),
        in_specs=[pl.BlockSpec((tn, K), lambda m, n: (n, 0)),
                  pl.BlockSpec((None, tn, bm), lambda m, n: (m // per, n, m % per))],
        out_specs=pl.BlockSpec((None, K, bm), lambda m, n: (m, 0, 0)),
        out_shape=jax.ShapeDtypeStruct((nb, K, bm), BF16),
        scratch_shapes=[pltpu.VMEM((K, bm), F32)],
        compiler_params=_cparams(2),
    )(xm, ym)


def final_loss(x, fg, target):
    B, T, D = x.shape
    tm = _tile(T, 512)

    def body(x_ref, g_ref, t_ref, dx_ref, dfg_ref, loss_ref):
        t = pl.program_id(1)

        @pl.when(t == 0)
        def _():
            dfg_ref[...] = jnp.zeros_like(dfg_ref)
            loss_ref[...] = jnp.zeros_like(loss_ref)

        xv = x_ref[...]
        g = g_ref[...]
        r = lax.rsqrt(jnp.mean(xv * xv, axis=-1, keepdims=True) + EPS)
        xn = xv * r
        err = xn * g - t_ref[...]
        tok_loss = jnp.mean(err * err, axis=-1, keepdims=True)
        loss_ref[...] += 0.5 * jnp.sum(tok_loss, axis=0, keepdims=True)
        dy = err * (1.0 / D)
        dfg_ref[...] += _sum0(dy * xn)
        dxn = dy * g
        dx_ref[...] = r * (dxn - xn * jnp.mean(dxn * xn, axis=-1, keepdims=True))

    tok = pl.BlockSpec((None, tm, D), lambda b, t: (b, t, 0))
    return pl.pallas_call(
        body, name="final_loss", grid=(B, T // tm),
        in_specs=[tok, pl.BlockSpec((1, D), lambda b, t: (0, 0)), tok],
        out_specs=[tok, pl.BlockSpec((None, 1, D), lambda b, t: (b, 0, 0)),
                   pl.BlockSpec((None, 1, LANES), lambda b, t: (b, 0, 0))],
        out_shape=[jax.ShapeDtypeStruct((B, T, D), F32), jax.ShapeDtypeStruct((B, 1, D), F32),
                   jax.ShapeDtypeStruct((B, 1, LANES), F32)],
        compiler_params=_cparams(2),
    )(x, fg, target)


def _past_halo_spec(tm, halo, width):
    return pl.BlockSpec((None, halo, width), lambda b, t: (b, jnp.maximum(t * (tm // halo) - 1, 0), 0))


def _future_halo_spec(tm, halo, width, T):
    return pl.BlockSpec((None, halo, width), lambda b, t: (b, jnp.minimum((t + 1) * (tm // halo), T // halo - 1), 0))


def _glu_fwd(h, w_ref, bias):
    D = h.shape[1]
    a = jnp.concatenate([_mm(h, w_ref[0]), _mm(h, w_ref[1])], axis=1) + bias[:, :D]
    b = jnp.concatenate([_mm(h, w_ref[2]), _mm(h, w_ref[3])], axis=1) + bias[:, D:]
    return a, b


def conv_glu_fwd(x, mod3, g, w_glu, b_glu):
    B, T, D = x.shape
    tm = _tile(T, 512)

    def body(x_ref, mod_ref, g_ref, w_ref, b_ref, u_ref):
        h = _modnorm(x_ref[...], g_ref[...], mod_ref[1:2, :], mod_ref[0:1, :]).astype(BF16)
        a, b = _glu_fwd(h, w_ref, b_ref[...])
        u_ref[...] = a * _sigmoid(b)

    tok = pl.BlockSpec((None, tm, D), lambda b, t: (b, t, 0))
    return pl.pallas_call(
        body, name="conv_glu_fwd", grid=(B, T // tm),
        in_specs=[tok, pl.BlockSpec((None, 3, D), lambda b, t: (b, 0, 0)),
                  pl.BlockSpec((1, D), lambda b, t: (0, 0)),
                  pl.BlockSpec((4, D, D // 2), lambda b, t: (0, 0, 0)),
                  pl.BlockSpec((1, 2 * D), lambda b, t: (0, 0))],
        out_specs=tok, out_shape=jax.ShapeDtypeStruct((B, T, D), F32),
        compiler_params=_cparams(2),
    )(x, mod3, g, w_glu, b_glu)


def _layer_norm_parts(u2):
    mu = jnp.mean(u2, axis=-1, keepdims=True)
    xc = u2 - mu
    rs = lax.rsqrt(jnp.mean(xc * xc, axis=-1, keepdims=True) + EPS)
    return xc * rs, rs


def conv_out_fwd(x, u, mod3, w_dw, b_dw, ln_g, ln_b, w_pw, b_pw):
    B, T, D = x.shape
    K = w_dw.shape[0] - 1
    tm = _tile(T, 512)

    def body(x_ref, u_ref, halo_ref, mod_ref, wdw_ref, bdw_ref, lg_ref, lb_ref, wpw_ref, bpw_ref,
             xo_ref, y_ref, u2_ref, ext_s):
        t = pl.program_id(1)
        ext_s[0:CONV_HALO, :] = jnp.where(t > 0, halo_ref[...], 0.0)
        ext_s[CONV_HALO:, :] = u_ref[...]
        acc = jnp.broadcast_to(bdw_ref[...], (tm, D))
        for k in range(K):
            acc = acc + wdw_ref[k:k + 1, :] * ext_s[pl.ds(CONV_HALO - (K - 1) + k, tm), :]
        u2_ref[...] = acc
        xh, _ = _layer_norm_parts(acc)
        l = xh * lg_ref[...] + lb_ref[...]
        u3 = l * _sigmoid(l)
        y = _mm(u3, wpw_ref[...]) + bpw_ref[...]
        y_ref[...] = y
        xo_ref[...] = x_ref[...] + (1.0 + mod_ref[2:3, :]) * y

    tok = pl.BlockSpec((None, tm, D), lambda b, t: (b, t, 0))
    vec = pl.BlockSpec((1, D), lambda b, t: (0, 0))
    return pl.pallas_call(
        body, name="conv_out_fwd", grid=(B, T // tm),
        in_specs=[tok, tok, _past_halo_spec(tm, CONV_HALO, D), pl.BlockSpec((None, 3, D), lambda b, t: (b, 0, 0)),
                  pl.BlockSpec((K + 1, D), lambda b, t: (0, 0)), vec, vec, vec,
                  pl.BlockSpec((D, D), lambda b, t: (0, 0)), vec],
        out_specs=[tok, tok, tok], out_shape=[jax.ShapeDtypeStruct((B, T, D), F32)] * 3,
        scratch_shapes=[pltpu.VMEM((tm + CONV_HALO, D), F32)],
        compiler_params=_cparams(2),
    )(x, u, u, mod3, w_dw, b_dw, ln_g, ln_b, w_pw, b_pw)


def conv_out_bwd(dres, y, u2, mod3, ln_g, ln_b, w_pw):
    B, T, D = dres.shape
    tm = _tile(T, 512)

    def body(dres_ref, y_ref, u2_ref, mod_ref, lg_ref, lb_ref, wpw_ref, du2_ref, u3_ref, dy_ref, dgate_ref, vec_ref):
        t = pl.program_id(1)

        @pl.when(t == 0)
        def _():
            dgate_ref[...] = jnp.zeros_like(dgate_ref)
            vec_ref[...] = jnp.zeros_like(vec_ref)

        dres = dres_ref[...]
        dy = (1.0 + mod_ref[2:3, :]) * dres
        dy_ref[...] = dy.astype(BF16)
        dgate_ref[...] += _sum0(dres * y_ref[...])
        xh, rs = _layer_norm_parts(u2_ref[...])
        lg = lg_ref[...]
        l = xh * lg + lb_ref[...]
        sg = _sigmoid(l)
        u3_ref[...] = (l * sg).astype(BF16)
        du3 = _mm_nt(dy, wpw_ref[...])
        dl = du3 * _dsilu(l, sg)
        dxh = dl * lg
        du2 = rs * (dxh - jnp.mean(dxh, axis=-1, keepdims=True) - xh * jnp.mean(dxh * xh, axis=-1, keepdims=True))
        du2_ref[...] = du2
        vec_ref[0:1, :] += _sum0(dy)
        vec_ref[1:2, :] += _sum0(dl * xh)
        vec_ref[2:3, :] += _sum0(dl)
        vec_ref[3:4, :] += _sum0(du2)

    tok = pl.BlockSpec((None, tm, D), lambda b, t: (b, t, 0))
    tokb = pl.BlockSpec((None, tm, D), lambda b, t: (b, t, 0))
    vec = pl.BlockSpec((1, D), lambda b, t: (0, 0))
    return pl.pallas_call(
        body, name="conv_out_bwd", grid=(B, T // tm),
        in_specs=[tok, tok, tok, pl.BlockSpec((None, 3, D), lambda b, t: (b, 0, 0)), vec, vec,
                  pl.BlockSpec((D, D), lambda b, t: (0, 0))],
        out_specs=[tok, tokb, tokb, pl.BlockSpec((None, 1, D), lambda b, t: (b, 0, 0)),
                   pl.BlockSpec((None, 4, D), lambda b, t: (b, 0, 0))],
        out_shape=[jax.ShapeDtypeStruct((B, T, D), F32), jax.ShapeDtypeStruct((B, T, D), BF16),
                   jax.ShapeDtypeStruct((B, T, D), BF16), jax.ShapeDtypeStruct((B, 1, D), F32),
                   jax.ShapeDtypeStruct((B, 4, D), F32)],
        compiler_params=_cparams(2),
    )(dres, y, u2, mod3, ln_g, ln_b, w_pw)


def conv_glu_bwd(x, dres, du2, u, mod3, g, w_glu, b_glu, w_dw):
    B, T, D = x.shape
    K = w_dw.shape[0] - 1
    tm = _tile(T, 256)
    nt = T // tm

    def body(x_ref, dres_ref, du2_ref, du2h_ref, u_ref, uh_ref, mod_ref, g_ref, w_ref, b_ref, wdw_ref,
             dx_ref, h_ref, dab_ref, dwdw_ref, dbglu_ref, dmod_ref, dg_ref, extu_s, extd_s):
        t = pl.program_id(1)

        @pl.when(t == 0)
        def _():
            dwdw_ref[...] = jnp.zeros_like(dwdw_ref)
            dbglu_ref[...] = jnp.zeros_like(dbglu_ref)
            dmod_ref[...] = jnp.zeros_like(dmod_ref)
            dg_ref[...] = jnp.zeros_like(dg_ref)

        du2 = du2_ref[...]
        extu_s[0:CONV_HALO, :] = jnp.where(t > 0, uh_ref[...], 0.0)
        extu_s[CONV_HALO:, :] = u_ref[...]
        extd_s[0:tm, :] = du2
        extd_s[tm:, :] = jnp.where(t < nt - 1, du2h_ref[...], 0.0)
        du = jnp.zeros((tm, D), F32)
        for k in range(K):
            du = du + wdw_ref[k:k + 1, :] * extd_s[pl.ds(K - 1 - k, tm), :]
            dwdw_ref[k:k + 1, :] += _sum0(du2 * extu_s[pl.ds(CONV_HALO - (K - 1) + k, tm), :])
        xv = x_ref[...]
        h = _modnorm(xv, g_ref[...], mod_ref[1:2, :], mod_ref[0:1, :]).astype(BF16)
        h_ref[...] = h
        a, b = _glu_fwd(h, w_ref, b_ref[...])
        sb = _sigmoid(b)
        da = du * sb
        db = du * a * sb * (1.0 - sb)
        dbglu_ref[:, 0:D] += _sum0(da)
        dbglu_ref[:, D:] += _sum0(db)
        da = da.astype(BF16)
        db = db.astype(BF16)
        dab_ref[:, 0:D] = da
        dab_ref[:, D:] = db
        Dh2 = D // 2
        dh = (_mm_nt(da[:, :Dh2], w_ref[0]) + _mm_nt(da[:, Dh2:], w_ref[1])
              + _mm_nt(db[:, :Dh2], w_ref[2]) + _mm_nt(db[:, Dh2:], w_ref[3]))
        dxn, dg, dscale, dshift = _modnorm_bwd(xv, g_ref[...], mod_ref[1:2, :], dh)
        dx_ref[...] = dres_ref[...] + dxn
        dmod_ref[0:1, :] += dshift
        dmod_ref[1:2, :] += dscale
        dg_ref[...] += dg

    tok = pl.BlockSpec((None, tm, D), lambda b, t: (b, t, 0))
    return pl.pallas_call(
        body, name="conv_glu_bwd", grid=(B, nt),
        in_specs=[tok, tok, tok, _future_halo_spec(tm, CONV_HALO, D, T), tok, _past_halo_spec(tm, CONV_HALO, D),
                  pl.BlockSpec((None, 3, D), lambda b, t: (b, 0, 0)), pl.BlockSpec((1, D), lambda b, t: (0, 0)),
                  pl.BlockSpec((4, D, D // 2), lambda b, t: (0, 0, 0)), pl.BlockSpec((1, 2 * D), lambda b, t: (0, 0)),
                  pl.BlockSpec((K + 1, D), lambda b, t: (0, 0))],
        out_specs=[tok, tok, pl.BlockSpec((None, tm, 2 * D), lambda b, t: (b, t, 0)),
                   pl.BlockSpec((None, K + 1, D), lambda b, t: (b, 0, 0)),
                   pl.BlockSpec((None, 1, 2 * D), lambda b, t: (b, 0, 0)),
                   pl.BlockSpec((None, 3, D), lambda b, t: (b, 0, 0)),
                   pl.BlockSpec((None, 1, D), lambda b, t: (b, 0, 0))],
        out_shape=[jax.ShapeDtypeStruct((B, T, D), F32), jax.ShapeDtypeStruct((B, T, D), BF16),
                   jax.ShapeDtypeStruct((B, T, 2 * D), BF16), jax.ShapeDtypeStruct((B, K + 1, D), F32),
                   jax.ShapeDtypeStruct((B, 1, 2 * D), F32), jax.ShapeDtypeStruct((B, 3, D), F32),
                   jax.ShapeDtypeStruct((B, 1, D), F32)],
        scratch_shapes=[pltpu.VMEM((tm + CONV_HALO, D), F32), pltpu.VMEM((tm + CONV_HALO, D), F32)],
        compiler_params=_cparams(2),
    )(x, dres, du2, du2, u, u, mod3, g, w_glu, b_glu, w_dw)


def dn_proj_fwd(x, mod3, g, w_main, w_ab):
    B, T, D = x.shape
    W = w_main.shape[1] // 4
    tm = _tile(T, 512)

    def body(x_ref, mod_ref, g_ref, wm_ref, wab_ref, pre_ref, z_ref, ab_ref):
        h = _modnorm(x_ref[...], g_ref[...], mod_ref[1:2, :], mod_ref[0:1, :]).astype(BF16)
        for p in range(3):
            pre_ref[:, p * W:(p + 1) * W] = _mm(h, wm_ref[:, p * W:(p + 1) * W])
        z_ref[...] = _mm(h, wm_ref[:, 3 * W:])
        ab_ref[...] = _mm(h, wab_ref[...])

    return pl.pallas_call(
        body, name="dn_proj_fwd", grid=(B, T // tm),
        in_specs=[pl.BlockSpec((None, tm, D), lambda b, t: (b, t, 0)), pl.BlockSpec((None, 3, D), lambda b, t: (b, 0, 0)),
                  pl.BlockSpec((1, D), lambda b, t: (0, 0)), pl.BlockSpec((D, 4 * W), lambda b, t: (0, 0)),
                  pl.BlockSpec((D, LANES), lambda b, t: (0, 0))],
        out_specs=[pl.BlockSpec((None, tm, 3 * W), lambda b, t: (b, t, 0)),
                   pl.BlockSpec((None, tm, W), lambda b, t: (b, t, 0)),
                   pl.BlockSpec((None, tm, LANES), lambda b, t: (b, t, 0))],
        out_shape=[jax.ShapeDtypeStruct((B, T, 3 * W), F32), jax.ShapeDtypeStruct((B, T, W), F32),
                   jax.ShapeDtypeStruct((B, T, LANES), F32)],
        compiler_params=_cparams(2),
    )(x, mod3, g, w_main, w_ab)


def _sconv(ext_s, w_ref, tm, K):
    acc = w_ref[0:1, :] * ext_s[pl.ds(SCONV_HALO - (K - 1), tm), :]
    for k in range(1, K):
        acc = acc + w_ref[k:k + 1, :] * ext_s[pl.ds(SCONV_HALO - (K - 1) + k, tm), :]
    return acc


def _lane_col(val, lane, idx):
    return jnp.sum(jnp.where(lane == idx, val, 0.0), axis=1, keepdims=True)


def dn_conv_fwd(pre, ab, w_sconv, alog_row, dt_row, H):
    B, T, W3 = pre.shape
    W = W3 // 3
    Dh = W // H
    K = w_sconv.shape[0]
    tm = _tile(T, 512)

    def body(pre_ref, halo_ref, ab_ref, w_ref, alog_ref, dt_ref, q_ref, k_ref, v_ref, gb_ref, bb_ref, ext_s):
        t = pl.program_id(1)
        ext_s[0:SCONV_HALO, :] = jnp.where(t > 0, halo_ref[...], 0.0)
        ext_s[SCONV_HALO:, :] = pre_ref[...]
        cv = _sconv(ext_s, w_ref, tm, K)
        qkv = cv * _sigmoid(cv)
        ab = ab_ref[...]
        lane = lax.broadcasted_iota(jnp.int32, ab.shape, 1)
        g_all = -jnp.exp(alog_ref[...]) * _softplus(ab + dt_ref[...])
        beta_all = _sigmoid(ab)
        for h in range(H):
            q_ref[h] = qkv[:, h * Dh:(h + 1) * Dh]
            k_ref[h] = qkv[:, W + h * Dh:W + (h + 1) * Dh]
            v_ref[h] = qkv[:, 2 * W + h * Dh:2 * W + (h + 1) * Dh]
            gb_ref[h] = jnp.broadcast_to(_lane_col(g_all, lane, h), (tm, Dh))
            bb_ref[h] = jnp.broadcast_to(_lane_col(beta_all, lane, H + h), (tm, Dh))

    hm = pl.BlockSpec((None, H, tm, Dh), lambda b, t: (b, 0, t, 0))
    row = pl.BlockSpec((1, LANES), lambda b, t: (0, 0))
    return pl.pallas_call(
        body, name="dn_conv_fwd", grid=(B, T // tm),
        in_specs=[pl.BlockSpec((None, tm, W3), lambda b, t: (b, t, 0)), _past_halo_spec(tm, SCONV_HALO, W3),
                  pl.BlockSpec((None, tm, LANES), lambda b, t: (b, t, 0)),
                  pl.BlockSpec((K, W3), lambda b, t: (0, 0)), row, row],
        out_specs=[hm] * 5, out_shape=[jax.ShapeDtypeStruct((B, H, T, Dh), F32)] * 5,
        scratch_shapes=[pltpu.VMEM((tm + SCONV_HALO, W3), F32)],
        compiler_params=_cparams(2),
    )(pre, pre, ab, w_sconv, alog_row, dt_row)


def _bdot(spec):
    return lambda a, b: jnp.einsum(spec, a.astype(BF16), b.astype(BF16), preferred_element_type=F32)


_NN, _NT, _TN = "gij,gjk->gik", "gik,gjk->gij", "gki,gkj->gij"


def _make_bdots():
    nn_, nt_, tn_ = _bdot(_NN), _bdot(_NT), _bdot(_TN)

    @jax.custom_vjp
    def nn(a, b):
        return nn_(a, b)

    @jax.custom_vjp
    def nt(a, b):
        return nt_(a, b)

    @jax.custom_vjp
    def tn(a, b):
        return tn_(a, b)

    nn.defvjp(lambda a, b: (nn_(a, b), (a, b)), lambda r, d: (nt_(d, r[1]), tn_(r[0], d)))
    nt.defvjp(lambda a, b: (nt_(a, b), (a, b)), lambda r, d: (nn_(d, r[1]), tn_(d, r[0])))
    tn.defvjp(lambda a, b: (tn_(a, b), (a, b)), lambda r, d: (nt_(r[1], d), nn_(r[0], d)))
    return nn, nt, tn


def _unit_lower_inverse(A):
    hdot = functools.partial(jnp.einsum, precision=lax.Precision.HIGH, preferred_element_type=F32)
    C = A.shape[-1]

    def impl(A):
        eye = (lax.broadcasted_iota(jnp.int32, A.shape, 1) == lax.broadcasted_iota(jnp.int32, A.shape, 2)).astype(F32)
        Tm = eye - A
        Ap = A
        for _ in range(max(1, (C - 1).bit_length()) - 1):
            Ap = hdot(_NN, Ap, Ap)
            Tm = Tm + hdot(_NN, Tm, Ap)
        return Tm

    @jax.custom_vjp
    def inv(A):
        return impl(A)

    def fwd(A):
        Tm = impl(A)
        return Tm, Tm

    def bwd(Tm, dT):
        return (-hdot(_NT, hdot(_TN, Tm, dT), Tm),)

    inv.defvjp(fwd, bwd)
    return inv(A)


def _chunk_fn(q, k, v, gb, bb, S):
    nn, nt, tn = _make_bdots()
    G, C, Dh = q.shape
    hdot = functools.partial(jnp.einsum, precision=HI, preferred_element_type=F32)
    q = q * lax.rsqrt(jnp.sum(q * q, axis=-1, keepdims=True) + EPS) * (Dh ** -0.5)
    k = k * lax.rsqrt(jnp.sum(k * k, axis=-1, keepdims=True) + EPS)
    row = lax.broadcasted_iota(jnp.int32, (G, C, C), 1)
    col = lax.broadcasted_iota(jnp.int32, (G, C, C), 2)
    causal = row >= col
    strict = row > col
    gc = hdot(_NN, causal.astype(F32), gb)
    spread = jnp.full((G, C, Dh), 1.0 / Dh, F32)
    gi = hdot(_NT, gc, spread)
    gj = hdot(_NT, spread, gc)
    decay = jnp.where(causal, jnp.exp(jnp.where(causal, gi - gj, 0.0)), 0.0)
    kb = k * bb
    vb = v * bb
    A = jnp.where(strict, nt(kb, k) * decay, 0.0)
    Tm = _unit_lower_inverse(A)
    eg = jnp.exp(gc)
    u = nn(Tm, vb)
    w = nn(Tm, kb * eg)
    qg = q * eg
    intra = nt(q, k) * decay
    glast = hdot(_NN, jnp.ones((G, C, C), F32), gb)
    kd = k * jnp.exp(glast - gc)
    v_new = u - nn(w, S)
    o = nn(qg, S) + nn(intra, v_new)
    egl = jnp.exp(glast)
    S_new = S * jnp.concatenate([egl] * (Dh // C), axis=1) + tn(kd, v_new)
    return o, S_new


def dn_chunk_fwd(q, k, v, gb, bb):
    B, H, T, Dh = q.shape
    NC = T // CHUNK

    def body(q_ref, k_ref, v_ref, gb_ref, bb_ref, o_ref, sp_ref, S_s):
        @pl.when(pl.program_id(1) == 0)
        def _():
            S_s[...] = jnp.zeros_like(S_s)

        S = S_s[...]
        sp_ref[...] = S
        o, S_new = _chunk_fn(q_ref[...], k_ref[...], v_ref[...], gb_ref[...], bb_ref[...], S)
        o_ref[...] = o
        S_s[...] = S_new

    hm = pl.BlockSpec((None, H, CHUNK, Dh), lambda b, n: (b, 0, n, 0))
    return pl.pallas_call(
        body, name="dn_chunk_fwd", grid=(B, NC),
        in_specs=[hm] * 5,
        out_specs=[hm, pl.BlockSpec((None, None, H, Dh, Dh), lambda b, n: (b, n, 0, 0, 0))],
        out_shape=[jax.ShapeDtypeStruct((B, H, T, Dh), F32), jax.ShapeDtypeStruct((B, NC, H, Dh, Dh), F32)],
        scratch_shapes=[pltpu.VMEM((H, Dh, Dh), F32)],
        compiler_params=_cparams(2),
    )(q, k, v, gb, bb)


def dn_chunk_bwd(q, k, v, gb, bb, s_prev, do):
    B, H, T, Dh = q.shape
    NC = T // CHUNK

    def body(q_ref, k_ref, v_ref, gb_ref, bb_ref, sp_ref, do_ref, dq_ref, dk_ref, dv_ref, dgb_ref, dbb_ref, dS_s):
        @pl.when(pl.program_id(1) == 0)
        def _():
            dS_s[...] = jnp.zeros_like(dS_s)

        _, vjp = jax.vjp(_chunk_fn, q_ref[...], k_ref[...], v_ref[...], gb_ref[...], bb_ref[...], sp_ref[...])
        dq, dk, dv, dgb, dbb, dS = vjp((do_ref[...], dS_s[...]))
        dq_ref[...] = dq
        dk_ref[...] = dk
        dv_ref[...] = dv
        dgb_ref[...] = dgb
        dbb_ref[...] = dbb
        dS_s[...] = dS

    hm = pl.BlockSpec((None, H, CHUNK, Dh), lambda b, n: (b, 0, NC - 1 - n, 0))
    return pl.pallas_call(
        body, name="dn_chunk_bwd", grid=(B, NC),
        in_specs=[hm] * 5 + [pl.BlockSpec((None, None, H, Dh, Dh), lambda b, n: (b, NC - 1 - n, 0, 0, 0)), hm],
        out_specs=[hm] * 5, out_shape=[jax.ShapeDtypeStruct((B, H, T, Dh), F32)] * 5,
        scratch_shapes=[pltpu.VMEM((H, Dh, Dh), F32)],
        compiler_params=_cparams(2),
    )(q, k, v, gb, bb, s_prev, do)


def _head_norm(o, og):
    r = lax.rsqrt(jnp.mean(o * o, axis=-1, keepdims=True) + EPS)
    return o * r, r


def dn_out_fwd(x, o, z, mod3, o_g, w_out):
    B, T, D = x.shape
    _, H, _, Dh = o.shape
    W = H * Dh
    tm = _tile(T, 512)

    def body(x_ref, o_ref, z_ref, mod_ref, og_ref, w_ref, xo_ref, y_ref):
        parts = []
        for h in range(H):
            on, _ = _head_norm(o_ref[h], og_ref[...])
            zz = z_ref[:, h * Dh:(h + 1) * Dh]
            parts.append((on * og_ref[...] * (zz * _sigmoid(zz))).astype(BF16))
        y = _mm(jnp.concatenate(parts, axis=1), w_ref[...])
        y_ref[...] = y
        xo_ref[...] = x_ref[...] + (1.0 + mod_ref[2:3, :]) * y

    tok = pl.BlockSpec((None, tm, D), lambda b, t: (b, t, 0))
    return pl.pallas_call(
        body, name="dn_out_fwd", grid=(B, T // tm),
        in_specs=[tok, pl.BlockSpec((None, H, tm, Dh), lambda b, t: (b, 0, t, 0)),
                  pl.BlockSpec((None, tm, W), lambda b, t: (b, t, 0)), pl.BlockSpec((None, 3, D), lambda b, t: (b, 0, 0)),
                  pl.BlockSpec((1, Dh), lambda b, t: (0, 0)), pl.BlockSpec((W, D), lambda b, t: (0, 0))],
        out_specs=[tok, tok], out_shape=[jax.ShapeDtypeStruct((B, T, D), F32)] * 2,
        compiler_params=_cparams(2),
    )(x, o, z, mod3, o_g, w_out)


def dn_out_bwd(dres, y, o, z, mod3, o_g, w_out):
    B, T, D = dres.shape
    _, H, _, Dh = o.shape
    W = H * Dh
    tm = _tile(T, 512)

    def body(dres_ref, y_ref, o_ref, z_ref, mod_ref, og_ref, w_ref, do_ref, dz_ref, ogb_ref, dy_ref, dgate_ref, dog_ref):
        t = pl.program_id(1)

        @pl.when(t == 0)
        def _():
            dgate_ref[...] = jnp.zeros_like(dgate_ref)
            dog_ref[...] = jnp.zeros_like(dog_ref)

        dres = dres_ref[...]
        dy = ((1.0 + mod_ref[2:3, :]) * dres).astype(BF16)
        dy_ref[...] = dy
        dgate_ref[...] += _sum0(dres * y_ref[...])
        dog = _mm_nt(dy, w_ref[...])
        og = og_ref[...]
        for h in range(H):
            ov = o_ref[h]
            xn, r = _head_norm(ov, og)
            zz = z_ref[:, h * Dh:(h + 1) * Dh]
            sg = _sigmoid(zz)
            sz = zz * sg
            d = dog[:, h * Dh:(h + 1) * Dh]
            ogb_ref[:, h * Dh:(h + 1) * Dh] = (xn * og * sz).astype(BF16)
            dz_ref[:, h * Dh:(h + 1) * Dh] = d * (xn * og) * _dsilu(zz, sg)
            don = d * sz
            dog_ref[...] += _sum0(don * xn)
            dxn = don * og
            do_ref[h] = r * (dxn - xn * jnp.mean(dxn * xn, axis=-1, keepdims=True))

    tok = pl.BlockSpec((None, tm, D), lambda b, t: (b, t, 0))
    tokw = pl.BlockSpec((None, tm, W), lambda b, t: (b, t, 0))
    hm = pl.BlockSpec((None, H, tm, Dh), lambda b, t: (b, 0, t, 0))
    return pl.pallas_call(
        body, name="dn_out_bwd", grid=(B, T // tm),
        in_specs=[tok, tok, hm, tokw, pl.BlockSpec((None, 3, D), lambda b, t: (b, 0, 0)),
                  pl.BlockSpec((1, Dh), lambda b, t: (0, 0)), pl.BlockSpec((W, D), lambda b, t: (0, 0))],
        out_specs=[hm, tokw, tokw, tok, pl.BlockSpec((None, 1, D), lambda b, t: (b, 0, 0)),
                   pl.BlockSpec((None, 1, Dh), lambda b, t: (b, 0, 0))],
        out_shape=[jax.ShapeDtypeStruct((B, H, T, Dh), F32), jax.ShapeDtypeStruct((B, T, W), F32),
                   jax.ShapeDtypeStruct((B, T, W), BF16), jax.ShapeDtypeStruct((B, T, D), BF16),
                   jax.ShapeDtypeStruct((B, 1, D), F32), jax.ShapeDtypeStruct((B, 1, Dh), F32)],
        compiler_params=_cparams(2),
    )(dres, y, o, z, mod3, o_g, w_out)


def dn_conv_bwd(dq, dk, dv, dgb, dbb, pre, ab, w_sconv, alog_row, dt_row):
    B, H, T, Dh = dq.shape
    W = H * Dh
    W3 = 3 * W
    K = w_sconv.shape[0]
    tm = _tile(T, 256)

    def body(dq_ref, dk_ref, dv_ref, dgb_ref, dbb_ref, pre_ref, halo_ref, ab_ref, w_ref, alog_ref, dt_ref,
             dc_ref, dab_ref, small_ref, ext_s):
        t = pl.program_id(1)

        @pl.when(t == 0)
        def _():
            small_ref[...] = jnp.zeros_like(small_ref)

        ext_s[0:SCONV_HALO, :] = jnp.where(t > 0, halo_ref[...], 0.0)
        ext_s[SCONV_HALO:, :] = pre_ref[...]
        cv = _sconv(ext_s, w_ref, tm, K)
        dsl = _dsilu(cv, _sigmoid(cv))
        ab = ab_ref[...]
        lane = lax.broadcasted_iota(jnp.int32, ab.shape, 1)
        dg_all = jnp.zeros_like(ab)
        db_all = jnp.zeros_like(ab)
        for h in range(H):
            dc_ref[:, h * Dh:(h + 1) * Dh] = dq_ref[h] * dsl[:, h * Dh:(h + 1) * Dh]
            dc_ref[:, W + h * Dh:W + (h + 1) * Dh] = dk_ref[h] * dsl[:, W + h * Dh:W + (h + 1) * Dh]
            dc_ref[:, 2 * W + h * Dh:2 * W + (h + 1) * Dh] = dv_ref[h] * dsl[:, 2 * W + h * Dh:2 * W + (h + 1) * Dh]
            dg_all = dg_all + jnp.where(lane == h, jnp.sum(dgb_ref[h], axis=1, keepdims=True), 0.0)
            db_all = db_all + jnp.where(lane == H + h, jnp.sum(dbb_ref[h], axis=1, keepdims=True), 0.0)
        xa = ab + dt_ref[...]
        ea = -jnp.exp(alog_ref[...])
        g_all = ea * _softplus(xa)
        da = dg_all * ea * _sigmoid(xa)
        beta = _sigmoid(ab)
        dab_ref[...] = da + db_all * beta * (1.0 - beta)
        small_ref[0:1, :] += _sum0(dg_all * g_all)
        small_ref[1:2, :] += _sum0(da)

    hm = pl.BlockSpec((None, H, tm, Dh), lambda b, t: (b, 0, t, 0))
    row = pl.BlockSpec((1, LANES), lambda b, t: (0, 0))
    return pl.pallas_call(
        body, name="dn_conv_bwd", grid=(B, T // tm),
        in_specs=[hm] * 5 + [pl.BlockSpec((None, tm, W3), lambda b, t: (b, t, 0)), _past_halo_spec(tm, SCONV_HALO, W3),
                             pl.BlockSpec((None, tm, LANES), lambda b, t: (b, t, 0)),
                             pl.BlockSpec((K, W3), lambda b, t: (0, 0)), row, row],
        out_specs=[pl.BlockSpec((None, tm, W3), lambda b, t: (b, t, 0)), pl.BlockSpec((None, tm, LANES), lambda b, t: (b, t, 0)),
                   pl.BlockSpec((None, 2, LANES), lambda b, t: (b, 0, 0))],
        out_shape=[jax.ShapeDtypeStruct((B, T, W3), F32), jax.ShapeDtypeStruct((B, T, LANES), F32),
                   jax.ShapeDtypeStruct((B, 2, LANES), F32)],
        scratch_shapes=[pltpu.VMEM((tm + SCONV_HALO, W3), F32)],
        compiler_params=_cparams(2),
    )(dq, dk, dv, dgb, dbb, pre, pre, ab, w_sconv, alog_row, dt_row)


def dn_proj_bwd(x, dres, dc, pre, dz, dab, mod3, g, w_main, w_ab, w_sconv):
    B, T, D = x.shape
    W3 = dc.shape[2]
    W = W3 // 3
    K = w_sconv.shape[0]
    tm = _tile(T, 256)
    nt = T // tm

    def body(x_ref, dres_ref, dc_ref, dch_ref, pre_ref, preh_ref, dz_ref, dab_ref, mod_ref, g_ref, wm_ref, wab_ref, ws_ref,
             dx_ref, h_ref, dproj_ref, dws_ref, dmod_ref, dg_ref, extp_s, extd_s):
        t = pl.program_id(1)

        @pl.when(t == 0)
        def _():
            dws_ref[...] = jnp.zeros_like(dws_ref)
            dmod_ref[...] = jnp.zeros_like(dmod_ref)
            dg_ref[...] = jnp.zeros_like(dg_ref)

        dc = dc_ref[...]
        extp_s[0:SCONV_HALO, :] = jnp.where(t > 0, preh_ref[...], 0.0)
        extp_s[SCONV_HALO:, :] = pre_ref[...]
        extd_s[0:tm, :] = dc
        extd_s[tm:, :] = jnp.where(t < nt - 1, dch_ref[...], 0.0)
        dpre = jnp.zeros((tm, W3), F32)
        for k in range(K):
            dpre = dpre + ws_ref[k:k + 1, :] * extd_s[pl.ds(K - 1 - k, tm), :]
            dws_ref[k:k + 1, :] += _sum0(dc * extp_s[pl.ds(SCONV_HALO - (K - 1) + k, tm), :])
        dpre = dpre.astype(BF16)
        dzb = dz_ref[...].astype(BF16)
        dproj_ref[:, 0:W3] = dpre
        dproj_ref[:, W3:] = dzb
        dh = _mm_nt(dab_ref[...], wab_ref[...]) + _mm_nt(dzb, wm_ref[:, W3:])
        for p in range(3):
            dh = dh + _mm_nt(dpre[:, p * W:(p + 1) * W], wm_ref[:, p * W:(p + 1) * W])
        xv = x_ref[...]
        h_ref[...] = _modnorm(xv, g_ref[...], mod_ref[1:2, :], mod_ref[0:1, :]).astype(BF16)
        dxn, dg, dscale, dshift = _modnorm_bwd(xv, g_ref[...], mod_ref[1:2, :], dh)
        dx_ref[...] = dres_ref[...] + dxn
        dmod_ref[0:1, :] += dshift
        dmod_ref[1:2, :] += dscale
        dg_ref[...] += dg

    tok = pl.BlockSpec((None, tm, D), lambda b, t: (b, t, 0))
    tok3 = pl.BlockSpec((None, tm, W3), lambda b, t: (b, t, 0))
    return pl.pallas_call(
        body, name="dn_proj_bwd", grid=(B, nt),
        in_specs=[tok, tok, tok3, _future_halo_spec(tm, SCONV_HALO, W3, T), tok3, _past_halo_spec(tm, SCONV_HALO, W3),
                  pl.BlockSpec((None, tm, W), lambda b, t: (b, t, 0)), pl.BlockSpec((None, tm, LANES), lambda b, t: (b, t, 0)),
                  pl.BlockSpec((None, 3, D), lambda b, t: (b, 0, 0)), pl.BlockSpec((1, D), lambda b, t: (0, 0)),
                  pl.BlockSpec((D, 4 * W), lambda b, t: (0, 0)), pl.BlockSpec((D, LANES), lambda b, t: (0, 0)),
                  pl.BlockSpec((K, W3), lambda b, t: (0, 0))],
        out_specs=[tok, tok, pl.BlockSpec((None, tm, 4 * W), lambda b, t: (b, t, 0)),
                   pl.BlockSpec((None, K, W3), lambda b, t: (b, 0, 0)), pl.BlockSpec((None, 3, D), lambda b, t: (b, 0, 0)),
                   pl.BlockSpec((None, 1, D), lambda b, t: (b, 0, 0))],
        out_shape=[jax.ShapeDtypeStruct((B, T, D), F32), jax.ShapeDtypeStruct((B, T, D), BF16),
                   jax.ShapeDtypeStruct((B, T, 4 * W), BF16), jax.ShapeDtypeStruct((B, K, W3), F32),
                   jax.ShapeDtypeStruct((B, 3, D), F32), jax.ShapeDtypeStruct((B, 1, D), F32)],
        scratch_shapes=[pltpu.VMEM((tm + SCONV_HALO, W3), F32), pltpu.VMEM((tm + SCONV_HALO, W3), F32)],
        compiler_params=_cparams(2),
    )(x, dres, dc, dc, pre, pre, dz, dab, mod3, g, w_main, w_ab, w_sconv)


def ada_fwd(c_all, w_ada, b_cols):
    L, D, Ca = w_ada.shape
    NB = c_all.shape[0]

    def body(c_ref, w_ref, b_ref, o_ref):
        cv = c_ref[...]
        o_ref[...] = _mm(cv * _sigmoid(cv), w_ref[...]) + b_ref[...]

    return pl.pallas_call(
        body, name="ada_fwd", grid=(L,),
        in_specs=[pl.BlockSpec((NB, D), lambda i: (0, 0)), pl.BlockSpec((None, D, Ca), lambda i: (i, 0, 0)),
                  pl.BlockSpec((None, 1, Ca), lambda i: (i, 0, 0))],
        out_specs=pl.BlockSpec((None, NB, Ca), lambda i: (i, 0, 0)),
        out_shape=jax.ShapeDtypeStruct((L, NB, Ca), F32),
        compiler_params=_cparams(1),
    )(c_all, w_ada, b_cols)


def ada_bwd(c_all, dmod_cols, dmod_all):
    L, NB, Ca = dmod_cols.shape
    D = c_all.shape[1]
    C9 = dmod_all.shape[2]

    def body(c_ref, dc_ref, da_ref, gw_ref, gb_ref):
        cv = c_ref[...]
        gw_ref[...] = _mm_tn(cv * _sigmoid(cv), dc_ref[...])
        gb_ref[...] = _sum0(da_ref[...])

    return pl.pallas_call(
        body, name="ada_bwd", grid=(L,),
        in_specs=[pl.BlockSpec((NB, D), lambda i: (0, 0)), pl.BlockSpec((None, NB, Ca), lambda i: (i, 0, 0)),
                  pl.BlockSpec((None, NB, C9), lambda i: (i, 0, 0))],
        out_specs=[pl.BlockSpec((None, D, Ca), lambda i: (i, 0, 0)), pl.BlockSpec((None, 1, C9), lambda i: (i, 0, 0))],
        out_shape=[jax.ShapeDtypeStruct((L, D, Ca), F32), jax.ShapeDtypeStruct((L, 1, C9), F32)],
        compiler_params=_cparams(1),
    )(c_all, dmod_cols, dmod_all)


def adamw(w, g, m, v, name):
    R, C = w.shape
    tr = _tile(R, max(8, (1 << 18) // C))

    def body(w_ref, g_ref, m_ref, v_ref, d_ref, mo_ref, vo_ref):
        gv = g_ref[...]
        mn = ADAM_B1 * m_ref[...] + (1.0 - ADAM_B1) * gv
        vn = ADAM_B2 * v_ref[...] + (1.0 - ADAM_B2) * (gv * gv)
        m_hat = mn / (1.0 - ADAM_B1 ** ADAM_STEP)
        v_hat = vn / (1.0 - ADAM_B2 ** ADAM_STEP)
        d_ref[...] = -ADAM_LR * (m_hat / (jnp.sqrt(v_hat) + ADAM_EPS) + ADAM_WD * w_ref[...])
        mo_ref[...] = mn
        vo_ref[...] = vn

    blk = pl.BlockSpec((tr, C), lambda i: (i, 0))
    return pl.pallas_call(
        body, name=name, grid=(R // tr,), in_specs=[blk] * 4, out_specs=[blk] * 3,
        out_shape=[jax.ShapeDtypeStruct((R, C), F32)] * 3, compiler_params=_cparams(1),
    )(w, g, m, v)


def sum_devices(a):
    n, R, C = a.shape

    def body(a_ref, o_ref):
        s = a_ref[0]
        for d in range(1, n):
            s = s + a_ref[d]
        o_ref[...] = s

    return pl.pallas_call(
        body, name="sum_devices", out_shape=jax.ShapeDtypeStruct((R, C), F32),
        compiler_params=pltpu.CompilerParams(vmem_limit_bytes=VMEM_LIMIT_V7X),
    )(a)


def _place():
    x, y, c = lax.axis_index("x"), lax.axis_index("y"), lax.axis_index("c")
    return x, y, c


def _other_chips(x, y):
    return [(2 * (1 - x) + y, 1 - x, y), (2 * x + (1 - y), x, 1 - y), (2 * (1 - x) + (1 - y), 1 - x, 1 - y)]


def allgather8(block):
    m_per, n = block.shape

    def body(x_ref, out_ref, send_sems, recv_sems, local_sem):
        x, y, c = _place()
        me, sibling = (x, y, c), (x, y, 1 - c)
        chips = [(1 - x, y), (x, 1 - y), (1 - x, 1 - y)]

        def rows(px, py, pc):
            return out_ref.at[pl.ds((4 * px + 2 * py + pc) * m_per, m_per), :]

        def copy(k, blk, to, src=None):
            return pltpu.make_async_remote_copy(
                src_ref=rows(*blk) if src is None else src, dst_ref=rows(*blk),
                send_sem=send_sems.at[k], recv_sem=recv_sems.at[k], device_id=to, device_id_type=MESH)

        mine = pltpu.make_async_copy(x_ref, rows(*me), local_sem)
        mine.start()
        first = [copy(0, me, sibling, src=x_ref)]
        first += [copy(1 + j, me, (*chip, c), src=x_ref) for j, chip in enumerate(chips)]
        for cp in first:
            cp.start()
        passed = [copy(4 + j, (*chip, c), sibling) for j, chip in enumerate(chips)]
        for j, chip in enumerate(chips):
            copy(1 + j, (*chip, c), me).wait_recv()
            passed[j].start()
        copy(0, sibling, me).wait_recv()
        for j, chip in enumerate(chips):
            copy(4 + j, (*chip, 1 - c), me).wait_recv()
        for cp in first + passed:
            cp.wait_send()
        mine.wait()

    return pl.pallas_call(
        body, name="allgather8", out_shape=jax.ShapeDtypeStruct((N_DEV * m_per, n), block.dtype),
        in_specs=[pl.BlockSpec(memory_space=pltpu.VMEM)], out_specs=pl.BlockSpec(memory_space=pltpu.VMEM),
        scratch_shapes=[pltpu.SemaphoreType.DMA((7,)), pltpu.SemaphoreType.DMA((7,)), pltpu.SemaphoreType.DMA],
        compiler_params=pltpu.CompilerParams(vmem_limit_bytes=VMEM_LIMIT_V7X),
    )(block)


def _half(ref, c, rh):
    return ref.at[pl.ds(pl.multiple_of(c * rh, 16), rh), :]


def gather_weights(lands):
    K = len(lands)

    def body(*refs):
        ins, outs = refs[:K], refs[K:2 * K]
        ici_send, ici_recv, d2d_send, d2d_recv = refs[2 * K:]
        x, y, c = _place()
        me = 2 * x + y
        sibling = (x, y, 1 - c)
        others = _other_chips(x, y)
        sent = []
        for k in range(K):
            rh = ins[k].shape[1] // 2
            for r, (_, px, py) in enumerate(others):
                cp = pltpu.make_async_remote_copy(
                    src_ref=_half(ins[k].at[me], c, rh), dst_ref=_half(outs[k].at[me], c, rh),
                    send_sem=ici_send.at[k, r], recv_sem=ici_recv.at[k, r], device_id=(px, py, c), device_id_type=MESH)
                cp.start()
                sent.append(cp)
        forwards = []
        for k in range(K):
            rh = ins[k].shape[1] // 2
            for r, (pchip, px, py) in enumerate(others):
                landed = _half(outs[k].at[pchip], c, rh)
                pltpu.make_async_remote_copy(
                    src_ref=landed, dst_ref=landed, send_sem=ici_send.at[k, r], recv_sem=ici_recv.at[k, r],
                    device_id=(px, py, c), device_id_type=MESH).wait_recv()
                fw = pltpu.make_async_remote_copy(
                    src_ref=landed, dst_ref=landed, send_sem=d2d_send.at[k, r], recv_sem=d2d_recv.at[k, r],
                    device_id=sibling, device_id_type=MESH)
                fw.start()
                forwards.append(fw)
        for k in range(K):
            rh = ins[k].shape[1] // 2
            for r, (pchip, _, _) in enumerate(others):
                theirs = _half(outs[k].at[pchip], 1 - c, rh)
                pltpu.make_async_remote_copy(
                    src_ref=theirs, dst_ref=theirs, send_sem=d2d_send.at[k, r], recv_sem=d2d_recv.at[k, r],
                    device_id=sibling, device_id_type=MESH).wait_recv()
        for cp in sent + forwards:
            cp.wait_send()

    return pl.pallas_call(
        body, name="gather_weights",
        out_shape=[jax.ShapeDtypeStruct(s.shape, s.dtype) for s in lands],
        in_specs=[HBM_SPEC] * K, out_specs=[HBM_SPEC] * K, input_output_aliases={k: k for k in range(K)},
        scratch_shapes=[pltpu.SemaphoreType.DMA((K, 3))] * 4,
    )(*lands)


def pair_exchange(grads):
    K = len(grads)

    def body(*refs):
        ins, outs = refs[:K], refs[K:2 * K]
        send_sems, recv_sems = refs[2 * K:]
        x, y, c = _place()
        sibling = (x, y, 1 - c)
        copies = []
        for k in range(K):
            n, r, _ = ins[k].shape
            rh = r // 2
            cp = pltpu.make_async_remote_copy(
                src_ref=ins[k].at[:, pl.ds(pl.multiple_of((1 - c) * rh, 16), rh), :], dst_ref=outs[k],
                send_sem=send_sems.at[k], recv_sem=recv_sems.at[k], device_id=sibling, device_id_type=MESH)
            cp.start()
            copies.append(cp)
        for cp in copies:
            cp.wait_recv()
        for cp in copies:
            cp.wait_send()

    return pl.pallas_call(
        body, name="pair_exchange",
        out_shape=[jax.ShapeDtypeStruct((g.shape[0], g.shape[1] // 2, g.shape[2]), g.dtype) for g in grads],
        in_specs=[HBM_SPEC] * K, out_specs=[HBM_SPEC] * K,
        scratch_shapes=[pltpu.SemaphoreType.DMA((K,))] * 2,
    )(*grads)


def pair_add(grad, recv, c_idx):
    n, r, C = grad.shape
    rh = r // 2
    tr = _tile(rh, max(16, (1 << 19) // C), 16)
    grad = grad.reshape(n, 2, rh, C)

    def body(c_ref, g_ref, r_ref, o_ref):
        o_ref[...] = (g_ref[...].astype(F32) + r_ref[...].astype(F32)).astype(BF16)

    return pl.pallas_call(
        body, name="pair_add",
        grid_spec=pltpu.PrefetchScalarGridSpec(
            num_scalar_prefetch=1, grid=(n, rh // tr),
            in_specs=[pl.BlockSpec((None, None, tr, C), lambda d, i, c_ref: (d, c_ref[0], i, 0)),
                      pl.BlockSpec((None, tr, C), lambda d, i, c_ref: (d, i, 0))],
            out_specs=pl.BlockSpec((None, tr, C), lambda d, i, c_ref: (d, i, 0))),
        out_shape=jax.ShapeDtypeStruct((n, rh, C), BF16), compiler_params=_cparams(2),
    )(c_idx, grad, recv)


def chip_exchange(parts):
    K = len(parts)

    def body(*refs):
        ins, outs = refs[:K], refs[K:2 * K]
        send_sems, recv_sems = refs[2 * K:]
        x, y, c = _place()
        others = _other_chips(x, y)
        started = []
        for k in range(K):
            for r, (pchip, px, py) in enumerate(others):
                cp = pltpu.make_async_remote_copy(
                    src_ref=ins[k].at[pchip], dst_ref=outs[k].at[r], send_sem=send_sems.at[k, r],
                    recv_sem=recv_sems.at[k, r], device_id=(px, py, c), device_id_type=MESH)
                cp.start()
                started.append(cp)
        for cp in started:
            cp.wait_recv()
        for cp in started:
            cp.wait_send()

    return pl.pallas_call(
        body, name="chip_exchange",
        out_shape=[jax.ShapeDtypeStruct((3,) + p.shape[1:], p.dtype) for p in parts],
        in_specs=[HBM_SPEC] * K, out_specs=[HBM_SPEC] * K,
        scratch_shapes=[pltpu.SemaphoreType.DMA((K, 3))] * 2,
    )(*parts)


def chip_sum(parts, got, where):
    _, rh, C = parts.shape
    tr = _tile(rh, max(16, (1 << 19) // C), 16)
    nt = rh // tr

    def body(w_ref, p_ref, g_ref, o_ref):
        s = p_ref[...].astype(F32)
        for r in range(3):
            s = s + g_ref[r].astype(F32)
        o_ref[...] = s

    return pl.pallas_call(
        body, name="chip_sum",
        grid_spec=pltpu.PrefetchScalarGridSpec(
            num_scalar_prefetch=1, grid=(nt,),
            in_specs=[pl.BlockSpec((None, tr, C), lambda i, w_ref: (w_ref[0], i, 0)),
                      pl.BlockSpec((3, tr, C), lambda i, w_ref: (0, i, 0))],
            out_specs=pl.BlockSpec((tr, C), lambda i, w_ref: (w_ref[1] * nt + i, 0))),
        out_shape=jax.ShapeDtypeStruct((2 * rh, C), F32), compiler_params=_cparams(1),
    )(where, parts, got)


def pair_share(sums):
    K = len(sums)

    def body(*refs):
        ins, outs = refs[:K], refs[K:2 * K]
        send_sems, recv_sems = refs[2 * K:]
        x, y, c = _place()
        sibling = (x, y, 1 - c)
        started = []
        for k in range(K):
            rh = ins[k].shape[0] // 2
            cp = pltpu.make_async_remote_copy(
                src_ref=_half(ins[k], c, rh), dst_ref=_half(outs[k], c, rh), send_sem=send_sems.at[k],
                recv_sem=recv_sems.at[k], device_id=sibling, device_id_type=MESH)
            cp.start()
            started.append(cp)
        for k in range(K):
            rh = ins[k].shape[0] // 2
            theirs = _half(outs[k], 1 - c, rh)
            pltpu.make_async_remote_copy(
                src_ref=theirs, dst_ref=theirs, send_sem=send_sems.at[k], recv_sem=recv_sems.at[k],
                device_id=sibling, device_id_type=MESH).wait_recv()
        for cp in started:
            cp.wait_send()

    return pl.pallas_call(
        body, name="pair_share",
        out_shape=[jax.ShapeDtypeStruct(s.shape, s.dtype) for s in sums],
        in_specs=[HBM_SPEC] * K, out_specs=[HBM_SPEC] * K, input_output_aliases={k: k for k in range(K)},
        scratch_shapes=[pltpu.SemaphoreType.DMA((K,))] * 2,
    )(*sums)


def reduce_scatter(grads, c_idx, where):
    recv = pair_exchange(grads)
    parts = [pair_add(g, r, c_idx) for g, r in zip(grads, recv)]
    got = chip_exchange(parts)
    return pair_share([chip_sum(p, g, where) for p, g in zip(parts, got)])


def _pack(arrs):
    flat = jnp.concatenate([a.reshape(-1).astype(F32) for a in arrs])
    pad = (-flat.shape[0]) % (8 * LANES)
    return jnp.pad(flat, (0, pad)).reshape(-1, LANES)


def _unpack(flat, shapes):
    out, off = [], 0
    for s in shapes:
        n = 1
        for d in s:
            n *= d
        out.append(flat[off:off + n].reshape(s))
        off += n
    return out


def _adamw_any(w, g, m, v, name):
    shp = w.shape
    C = shp[-1]
    d, nm, nv = adamw(w.reshape(-1, C), g.reshape(-1, C), m.reshape(-1, C), v.reshape(-1, C), name)
    return d.reshape(shp), nm.reshape(shp), nv.reshape(shp)


def kernel(x, c, norm_g, w_ada, b_ada, w_ffn_in, w_ffn_out, cm_w_glu, cm_b_glu, cm_w_dw, cm_b_dw, cm_ln_g, cm_ln_b, cm_w_pw, cm_b_pw, dn_w_in, dn_w_sconv, dn_a_log, dn_dt_bias, dn_o_g, dn_w_out, final_g, loss_target, m_norm_g, m_w_ada, m_b_ada, m_w_ffn_in, m_w_ffn_out, m_cm_w_glu, m_cm_b_glu, m_cm_w_dw, m_cm_b_dw, m_cm_ln_g, m_cm_ln_b, m_cm_w_pw, m_cm_b_pw, m_dn_w_in, m_dn_w_sconv, m_dn_a_log, m_dn_dt_bias, m_dn_o_g, m_dn_w_out, m_final_g, v_norm_g, v_w_ada, v_b_ada, v_w_ffn_in, v_w_ffn_out, v_cm_w_glu, v_cm_b_glu, v_cm_w_dw, v_cm_b_dw, v_cm_ln_g, v_cm_ln_b, v_cm_w_pw, v_cm_b_pw, v_dn_w_in, v_dn_w_sconv, v_dn_a_log, v_dn_dt_bias, v_dn_o_g, v_dn_w_out, v_final_g):
    weights = dict(norm_g=norm_g, w_ada=w_ada, b_ada=b_ada, w_ffn_in=w_ffn_in, w_ffn_out=w_ffn_out, cm_w_glu=cm_w_glu,
                   cm_b_glu=cm_b_glu, cm_w_dw=cm_w_dw, cm_b_dw=cm_b_dw, cm_ln_g=cm_ln_g, cm_ln_b=cm_ln_b, cm_w_pw=cm_w_pw,
                   cm_b_pw=cm_b_pw, dn_w_in=dn_w_in, dn_w_sconv=dn_w_sconv, dn_a_log=dn_a_log, dn_dt_bias=dn_dt_bias,
                   dn_o_g=dn_o_g, dn_w_out=dn_w_out, final_g=final_g)
    mom_m = dict(norm_g=m_norm_g, w_ada=m_w_ada, b_ada=m_b_ada, w_ffn_in=m_w_ffn_in, w_ffn_out=m_w_ffn_out,
                 cm_w_glu=m_cm_w_glu, cm_b_glu=m_cm_b_glu, cm_w_dw=m_cm_w_dw, cm_b_dw=m_cm_b_dw, cm_ln_g=m_cm_ln_g,
                 cm_ln_b=m_cm_ln_b, cm_w_pw=m_cm_w_pw, cm_b_pw=m_cm_b_pw, dn_w_in=m_dn_w_in, dn_w_sconv=m_dn_w_sconv,
                 dn_a_log=m_dn_a_log, dn_dt_bias=m_dn_dt_bias, dn_o_g=m_dn_o_g, dn_w_out=m_dn_w_out, final_g=m_final_g)
    mom_v = dict(norm_g=v_norm_g, w_ada=v_w_ada, b_ada=v_b_ada, w_ffn_in=v_w_ffn_in, w_ffn_out=v_w_ffn_out,
                 cm_w_glu=v_cm_w_glu, cm_b_glu=v_cm_b_glu, cm_w_dw=v_cm_w_dw, cm_b_dw=v_cm_b_dw, cm_ln_g=v_cm_ln_g,
                 cm_ln_b=v_cm_ln_b, cm_w_pw=v_cm_w_pw, cm_b_pw=v_cm_b_pw, dn_w_in=v_dn_w_in, dn_w_sconv=v_dn_w_sconv,
                 dn_a_log=v_dn_a_log, dn_dt_bias=v_dn_dt_bias, dn_o_g=v_dn_o_g, dn_w_out=v_dn_w_out, final_g=v_final_g)
    names = list(weights)

    BL, T, D = x.shape
    L = norm_g.shape[0]
    NB = BL * N_DEV
    Ca = w_ada.shape[2]
    C9 = b_ada.shape[1]
    H = dn_a_log.shape[1]
    Dh = dn_o_g.shape[1]
    W = H * Dh
    KC = cm_w_dw.shape[1]
    KS = dn_w_sconv.shape[1]
    n_cm, n_dn = cm_w_glu.shape[0], dn_w_in.shape[0]
    ax, ay, ac = lax.axis_index("x"), lax.axis_index("y"), lax.axis_index("c")
    chip = 2 * ax + ay
    dev = 2 * chip + ac
    c_idx = ac.astype(jnp.int32).reshape(1)
    where = jnp.stack([chip, ac]).astype(jnp.int32)

    small_in = [c, norm_g, cm_w_dw, dn_w_sconv]
    packed = _pack(small_in)
    gathered = allgather8(packed).reshape(N_DEV, -1)
    per_dev = [_unpack(gathered[d], [a.shape for a in small_in]) for d in range(N_DEV)]
    c_all = jnp.concatenate([p[0] for p in per_dev], axis=0)
    norm_g_full = jnp.concatenate([per_dev[2 * s][1] for s in range(N_CHIPS)], axis=-1)
    w_dw_full = jnp.concatenate([per_dev[2 * s][2] for s in range(N_CHIPS)], axis=-1)
    w_sconv_full = jnp.concatenate([per_dev[2 * s][3] for s in range(N_CHIPS)], axis=-1)

    b_cols = lax.dynamic_slice_in_dim(b_ada, chip * Ca, Ca, axis=1).reshape(L, 1, Ca)
    mod_part = ada_fwd(c_all, w_ada, b_cols)
    mod_g = allgather8(mod_part.reshape(-1, LANES)).reshape(N_DEV, L, NB, Ca)
    mod_all = jnp.concatenate([mod_g[2 * s] for s in range(N_CHIPS)], axis=-1)
    mod = lax.dynamic_slice_in_dim(mod_all, dev * BL, BL, axis=1).reshape(L, BL, 9, D)

    def layer_shards(i):
        sh = [w_ffn_in[i, 0], w_ffn_in[i, 1], w_ffn_out[i, 0], w_ffn_out[i, 1]]
        if i % 2 == 0:
            sh += [cm_w_glu[i // 2], cm_w_pw[i // 2]]
        else:
            sh += [dn_w_in[i // 2], dn_w_out[i // 2]]
        return [lax.dynamic_update_slice(lax.empty((N_CHIPS,) + s.shape, BF16), s.astype(BF16)[None], (chip, 0, 0))
                for s in sh]

    wts = [gather_weights(layer_shards(i)) for i in range(L)]

    def dn_weights(i):
        full = jnp.transpose(wts[i][4], (1, 0, 2)).reshape(D, -1)
        return full[:, :4 * W], jnp.pad(full[:, 4 * W:], ((0, 0), (0, LANES - 2 * H)))

    def row128(v):
        return jnp.pad(v.reshape(1, -1), ((0, 0), (0, LANES - v.shape[-1])))

    def pad_taps(w):
        return jnp.pad(w, ((0, 1), (0, 0)))

    saved = []
    xs = x
    for i in range(L):
        wl = wts[i]
        sv = {}
        m3 = [mod[i, :, 3 * j:3 * j + 3] for j in range(3)]
        gs = [norm_g_full[i, j].reshape(1, D) for j in range(3)]
        sv["x0"] = xs
        xs, sv["y0"] = ffn_fwd(xs, m3[0], gs[0], wl[0], wl[2])
        sv["x1"] = xs
        if i % 2 == 0:
            a = i // 2
            sv["u"] = conv_glu_fwd(xs, m3[1], gs[1], wl[4], cm_b_glu[a].reshape(1, -1))
            xs, sv["y1"], sv["u2"] = conv_out_fwd(
                xs, sv["u"], m3[1], pad_taps(w_dw_full[a]), cm_b_dw[a].reshape(1, D), cm_ln_g[a].reshape(1, D),
                cm_ln_b[a].reshape(1, D), wl[5].reshape(D, D), cm_b_pw[a].reshape(1, D))
        else:
            a = i // 2
            w_main, w_ab = dn_weights(i)
            sv["pre"], sv["z"], sv["ab"] = dn_proj_fwd(xs, m3[1], gs[1], w_main, w_ab)
            qkvgb = dn_conv_fwd(sv["pre"], sv["ab"], w_sconv_full[a], row128(dn_a_log[a]), row128(dn_dt_bias[a]), H)
            sv["qkvgb"] = qkvgb
            sv["o"], sv["sp"] = dn_chunk_fwd(*qkvgb)
            xs, sv["y1"] = dn_out_fwd(xs, sv["o"], sv["z"], m3[1], dn_o_g[a].reshape(1, Dh), wl[5].reshape(W, D))
        sv["x2"] = xs
        xs, sv["y2"] = ffn_fwd(xs, m3[2], gs[2], wl[1], wl[3])
        saved.append(sv)

    dx, d_final_g, loss_part = final_loss(xs, final_g.reshape(1, D), loss_target)

    g_small = {n: None for n in names}
    d_norm_g = [[None] * 3 for _ in range(L)]
    dmod = [[None] * 3 for _ in range(L)]
    g_cm = {k: [None] * n_cm for k in ("b_glu", "w_dw", "b_dw", "ln_g", "ln_b", "b_pw")}
    g_dn = {k: [None] * n_dn for k in ("w_sconv", "a_log", "dt_bias", "o_g")}
    big = [None] * L

    def ffn_back(i, j, slot, dx):
        wl, sv = wts[i], saved[i]
        m3 = mod[i, :, 3 * j:3 * j + 3]
        g = norm_g_full[i, j].reshape(1, D)
        dx, hb, ab_, dgu, dyb, dm, dg = ffn_bwd(sv["x%d" % j], dx, sv["y%d" % j], m3, g, wl[slot], wl[2 + slot])
        dmod[i][j] = dm
        d_norm_g[i][j] = jnp.sum(dg, axis=(0, 1))
        Fc = wl[slot].shape[2]
        dw_in = matmul_tn(hb.reshape(-1, D), dgu.reshape(2, BL * T, 2 * Fc), Fc, "dw_ffn_in")
        dw_out = matmul_tn(ab_.reshape(-1, 2 * Fc), dyb.reshape(1, -1, D), D, "dw_ffn_out")
        return dx, dw_in, dw_out.reshape(N_CHIPS, -1, D)

    for i in reversed(range(L)):
        wl, sv = wts[i], saved[i]
        a = i // 2
        dx, dw_in1, dw_out1 = ffn_back(i, 2, 1, dx)
        m3 = mod[i, :, 3:6]
        g = norm_g_full[i, 1].reshape(1, D)
        if i % 2 == 0:
            w_pw = wl[5].reshape(D, D)
            wdw = pad_taps(w_dw_full[a])
            du2, u3b, dyb, dgate, vec = conv_out_bwd(dx, sv["y1"], sv["u2"], m3, cm_ln_g[a].reshape(1, D),
                                                     cm_ln_b[a].reshape(1, D), w_pw)
            dx, hb, dab, dwdw, dbglu, dm, dg = conv_glu_bwd(sv["x1"], dx, du2, sv["u"], m3, g, wl[4],
                                                            cm_b_glu[a].reshape(1, -1), wdw)
            dm = dm.at[:, 2:3, :].set(dgate)
            vec = jnp.sum(vec, axis=0)
            g_cm["b_pw"][a], g_cm["ln_g"][a], g_cm["ln_b"][a], g_cm["b_dw"][a] = vec[0], vec[1], vec[2], vec[3]
            g_cm["w_dw"][a] = jnp.sum(dwdw, axis=0)[:KC]
            g_cm["b_glu"][a] = jnp.sum(dbglu, axis=(0, 1))
            dw_a = matmul_tn(hb.reshape(-1, D), dab.reshape(1, -1, 2 * D), D // 2, "dw_glu")
            dw_b = matmul_tn(u3b.reshape(-1, D), dyb.reshape(1, -1, D), D, "dw_sq").reshape(N_CHIPS, -1, D)
        else:
            w_main, w_ab = dn_weights(i)
            w_out = wl[5].reshape(W, D)
            do, dz, ogb, dyb, dgate, dog = dn_out_bwd(dx, sv["y1"], sv["o"], sv["z"], m3, dn_o_g[a].reshape(1, Dh), w_out)
            dq, dk, dv, dgb, dbb = dn_chunk_bwd(*sv["qkvgb"], sv["sp"], do)
            dc, dab, small = dn_conv_bwd(dq, dk, dv, dgb, dbb, sv["pre"], sv["ab"], w_sconv_full[a],
                                         row128(dn_a_log[a]), row128(dn_dt_bias[a]))
            dx, hb, dproj, dws, dm, dg = dn_proj_bwd(sv["x1"], dx, dc, sv["pre"], dz, dab, m3, g, w_main, w_ab,
                                                     w_sconv_full[a])
            dm = dm.at[:, 2:3, :].set(dgate)
            small = jnp.sum(small, axis=0)
            g_dn["a_log"][a], g_dn["dt_bias"][a] = small[0, :H], small[1, :H]
            g_dn["o_g"][a] = jnp.sum(dog, axis=(0, 1))
            g_dn["w_sconv"][a] = jnp.sum(dws, axis=0)
            dw_main = matmul_tn(hb.reshape(-1, D), dproj.reshape(1, -1, 4 * W), W, "dw_dn_main")
            dw_ab = matmul_tn(hb.reshape(-1, D), dab.reshape(1, -1, LANES), LANES, "dw_dn_ab")
            full = jnp.concatenate([jnp.transpose(dw_main, (1, 0, 2)).reshape(D, 4 * W), dw_ab[0][:, :2 * H]], axis=1)
            dw_a = jnp.transpose(full.reshape(D, N_CHIPS, -1), (1, 0, 2))
            dw_b = matmul_tn(ogb.reshape(-1, W), dyb.reshape(1, -1, D), D, "dw_sq").reshape(N_CHIPS, -1, D)
        dmod[i][1] = dm
        d_norm_g[i][1] = jnp.sum(dg, axis=(0, 1))
        dx, dw_in0, dw_out0 = ffn_back(i, 0, 0, dx)
        big[i] = reduce_scatter([dw_in0, dw_in1, dw_out0, dw_out1, dw_a, dw_b], c_idx, where)

    part = dict(
        norm_g=jnp.stack([jnp.stack(r) for r in d_norm_g]),
        cm_b_glu=jnp.stack(g_cm["b_glu"]), cm_w_dw=jnp.stack(g_cm["w_dw"]), cm_b_dw=jnp.stack(g_cm["b_dw"]),
        cm_ln_g=jnp.stack(g_cm["ln_g"]), cm_ln_b=jnp.stack(g_cm["ln_b"]), cm_b_pw=jnp.stack(g_cm["b_pw"]),
        dn_w_sconv=jnp.stack(g_dn["w_sconv"]), dn_a_log=jnp.stack(g_dn["a_log"]), dn_dt_bias=jnp.stack(g_dn["dt_bias"]),
        dn_o_g=jnp.stack(g_dn["o_g"]), final_g=jnp.sum(d_final_g, axis=(0, 1)),
        loss=jnp.sum(loss_part[:, 0, 0]).reshape(1))
    dmod_loc = jnp.stack([jnp.concatenate(r, axis=1) for r in dmod]).reshape(L, BL, C9)
    keys = list(part)
    packed = _pack([part[k] for k in keys] + [dmod_loc])
    R = packed.shape[0]
    gathered = allgather8(packed).reshape(N_DEV, R, LANES)
    summed = _unpack(sum_devices(gathered).reshape(-1), [part[k].shape for k in keys])
    tot = dict(zip(keys, summed))
    n_small = sum(int(part[k].size) for k in keys)
    dmod_all = gathered.reshape(N_DEV, -1)[:, n_small:n_small + L * BL * C9].reshape(N_DEV, L, BL, C9)
    dmod_all = jnp.transpose(dmod_all, (1, 0, 2, 3)).reshape(L, NB, C9)
    dmod_cols = lax.dynamic_slice_in_dim(dmod_all, chip * Ca, Ca, axis=2)
    g_w_ada, g_b_ada = ada_bwd(c_all, dmod_cols, dmod_all)

    def my_cols(full):
        n = full.shape[-1] // N_CHIPS
        return lax.dynamic_slice_in_dim(full, chip * n, n, axis=full.ndim - 1)

    grads = dict(
        norm_g=my_cols(tot["norm_g"]), w_ada=g_w_ada, b_ada=g_b_ada.reshape(L, C9),
        w_ffn_in=jnp.stack([jnp.stack([big[i][0], big[i][1]]) for i in range(L)]),
        w_ffn_out=jnp.stack([jnp.stack([big[i][2], big[i][3]]) for i in range(L)]),
        cm_w_glu=jnp.stack([big[i][4] for i in range(0, L, 2)]), cm_b_glu=tot["cm_b_glu"], cm_w_dw=my_cols(tot["cm_w_dw"]),
        cm_b_dw=tot["cm_b_dw"], cm_ln_g=tot["cm_ln_g"], cm_ln_b=tot["cm_ln_b"],
        cm_w_pw=jnp.stack([big[i][5] for i in range(0, L, 2)]), cm_b_pw=tot["cm_b_pw"],
        dn_w_in=jnp.stack([big[i][4] for i in range(1, L, 2)]), dn_w_sconv=my_cols(tot["dn_w_sconv"]),
        dn_a_log=tot["dn_a_log"], dn_dt_bias=tot["dn_dt_bias"], dn_o_g=tot["dn_o_g"],
        dn_w_out=jnp.stack([big[i][5] for i in range(1, L, 2)]), final_g=tot["final_g"])

    large = ("w_ada", "w_ffn_in", "w_ffn_out", "cm_w_glu", "cm_w_pw", "dn_w_in", "dn_w_out")
    delta, new_m, new_v = {}, {}, {}
    for n in large:
        delta[n], new_m[n], new_v[n] = _adamw_any(weights[n], grads[n], mom_m[n], mom_v[n], "adamw_" + n)
    rest = [n for n in names if n not in large]
    shapes = [weights[n].shape for n in rest]
    pd, pm, pv = adamw(_pack([weights[n] for n in rest]), _pack([grads[n] for n in rest]),
                       _pack([mom_m[n] for n in rest]), _pack([mom_v[n] for n in rest]), "adamw_small")
    for n, d_, m_, v_ in zip(rest, _unpack(pd.reshape(-1), shapes), _unpack(pm.reshape(-1), shapes),
                             _unpack(pv.reshape(-1), shapes)):
        delta[n], new_m[n], new_v[n] = d_, m_, v_

    return (tot["loss"].reshape(()), dx, *[grads[n] for n in names], *[delta[n] for n in names],
            *[new_m[n] for n in names], *[new_v[n] for n in names])
```

```python
import functools

import jax
import jax.numpy as jnp
from jax import lax
from jax.experimental import pallas as pl
from jax.experimental.pallas import tpu as pltpu

F32 = jnp.float32
BF16 = jnp.bfloat16
EPS = 1e-6
CHUNK = 64
N_CHIPS = 4
N_DEV = 8
LANES = 128
CONV_HALO = 32
SCONV_HALO = 8
VMEM_LIMIT_V7X = 60 * 1024 * 1024
HI = lax.Precision.HIGHEST
MESH = pl.DeviceIdType.MESH
HBM_SPEC = pl.BlockSpec(memory_space=pltpu.HBM)

ADAM_LR, ADAM_B1, ADAM_B2, ADAM_EPS, ADAM_WD, ADAM_STEP = 0.001, 0.9, 0.999, 1e-08, 0.01, 10


def _cparams(n_axes):
    return pltpu.CompilerParams(dimension_semantics=("arbitrary",) * n_axes, vmem_limit_bytes=VMEM_LIMIT_V7X)


def _tile(n, pref, mult=8):
    for t in range(min(n, pref) // mult * mult, 0, -mult):
        if n % t == 0:
            return t
    return n


def _mm(a, b):
    return lax.dot_general(a.astype(BF16), b.astype(BF16), (((1,), (0,)), ((), ())), preferred_element_type=F32)


def _mm_nt(a, b):
    return lax.dot_general(a.astype(BF16), b.astype(BF16), (((1,), (1,)), ((), ())), preferred_element_type=F32)


def _mm_tn(a, b):
    return lax.dot_general(a.astype(BF16), b.astype(BF16), (((0,), (0,)), ((), ())), preferred_element_type=F32)


def _sigmoid(x):
    return jax.nn.sigmoid(x)


def _dsilu(x, s):
    return s * (1.0 + x * (1.0 - s))


def _softplus(x):
    return jnp.maximum(x, 0.0) + jnp.log(1.0 + jnp.exp(-jnp.abs(x)))


def _modnorm(x, g, scale, shift):
    r = lax.rsqrt(jnp.mean(x * x, axis=-1, keepdims=True) + EPS)
    return (x * r) * g * (1.0 + scale) + shift


def _modnorm_bwd(x, g, scale, dh):
    r = lax.rsqrt(jnp.mean(x * x, axis=-1, keepdims=True) + EPS)
    xn = x * r
    dshift = jnp.sum(dh, axis=0, keepdims=True)
    dscale = jnp.sum(dh * (xn * g), axis=0, keepdims=True)
    dhn = dh * (1.0 + scale)
    dg = jnp.sum(dhn * xn, axis=0, keepdims=True)
    dxn = dhn * g
    dx = r * (dxn - xn * jnp.mean(dxn * xn, axis=-1, keepdims=True))
    return dx, dg, dscale, dshift


def _sum0(a):
    return jnp.sum(a, axis=0, keepdims=True)


def ffn_fwd(x, mod3, g, w_in, w_out):
    B, T, D = x.shape
    Fc = w_in.shape[2]
    w_in = w_in.reshape(2, 2, D, Fc)
    w_out = w_out.reshape(2, Fc, D)
    tm = _tile(T, 512)

    def body(x_ref, mod_ref, g_ref, wi_ref, wo_ref, xo_ref, y_ref, h_s, acc_s):
        f = pl.program_id(2)

        @pl.when(f == 0)
        def _():
            h = _modnorm(x_ref[...], g_ref[...], mod_ref[1:2, :], mod_ref[0:1, :])
            h_s[...] = h.astype(BF16)
            acc_s[...] = jnp.zeros_like(acc_s)

        h = h_s[...]
        gt = _mm(h, wi_ref[0])
        up = _mm(h, wi_ref[1])
        a = gt * _sigmoid(gt) * up
        acc_s[...] += _mm(a, wo_ref[...])

        @pl.when(f == 1)
        def _():
            y = acc_s[...]
            y_ref[...] = y
            xo_ref[...] = x_ref[...] + 0.5 * (1.0 + mod_ref[2:3, :]) * y

    tok = pl.BlockSpec((None, tm, D), lambda b, t, f: (b, t, 0))
    return pl.pallas_call(
        body, name="ffn_fwd", grid=(B, T // tm, 2),
        in_specs=[tok,
                  pl.BlockSpec((None, 3, D), lambda b, t, f: (b, 0, 0)),
                  pl.BlockSpec((1, D), lambda b, t, f: (0, 0)),
                  pl.BlockSpec((2, None, D, Fc), lambda b, t, f: (0, f, 0, 0)),
                  pl.BlockSpec((None, Fc, D), lambda b, t, f: (f, 0, 0))],
        out_specs=[tok, tok],
        out_shape=[jax.ShapeDtypeStruct((B, T, D), F32)] * 2,
        scratch_shapes=[pltpu.VMEM((tm, D), BF16), pltpu.VMEM((tm, D), F32)],
        compiler_params=_cparams(3),
    )(x, mod3, g, w_in, w_out)


def ffn_bwd(x, dres, y, mod3, g, w_in, w_out):
    B, T, D = x.shape
    Fc = w_in.shape[2]
    F = 2 * Fc
    w_in = w_in.reshape(2, 2, D, Fc)
    w_out = w_out.reshape(2, Fc, D)
    tm = _tile(T, 256)

    def body(x_ref, dres_ref, y_ref, mod_ref, g_ref, wi_ref, wo_ref,
             dx_ref, h_ref, a_ref, dgu_ref, dy_ref, dmod_ref, dg_ref, h_s, dy_s, dh_s):
        t, f = pl.program_id(1), pl.program_id(2)

        @pl.when(f == 0)
        def _():
            h = _modnorm(x_ref[...], g_ref[...], mod_ref[1:2, :], mod_ref[0:1, :]).astype(BF16)
            h_s[...] = h
            h_ref[...] = h
            dres = dres_ref[...]
            dy = (0.5 * (1.0 + mod_ref[2:3, :]) * dres).astype(BF16)
            dy_s[...] = dy
            dy_ref[...] = dy
            dh_s[...] = jnp.zeros_like(dh_s)
            dgate = _sum0(dres * (0.5 * y_ref[...]))

            @pl.when(t == 0)
            def _():
                dmod_ref[...] = jnp.zeros_like(dmod_ref)
                dg_ref[...] = jnp.zeros_like(dg_ref)

            dmod_ref[2:3, :] += dgate

        h = h_s[...]
        dy = dy_s[...]
        gt = _mm(h, wi_ref[0])
        up = _mm(h, wi_ref[1])
        sg = _sigmoid(gt)
        silu = gt * sg
        a_ref[...] = (silu * up).astype(BF16)
        da = _mm_nt(dy, wo_ref[...])
        dup = (da * silu).astype(BF16)
        dgt = (da * up * _dsilu(gt, sg)).astype(BF16)
        dgu_ref[0] = dgt
        dgu_ref[1] = dup
        dh_s[...] += _mm_nt(dgt, wi_ref[0]) + _mm_nt(dup, wi_ref[1])

        @pl.when(f == 1)
        def _():
            dxn, dg, dscale, dshift = _modnorm_bwd(x_ref[...], g_ref[...], mod_ref[1:2, :], dh_s[...])
            dx_ref[...] = dres_ref[...] + dxn
            dmod_ref[0:1, :] += dshift
            dmod_ref[1:2, :] += dscale
            dg_ref[...] += dg

    tok = pl.BlockSpec((None, tm, D), lambda b, t, f: (b, t, 0))
    per_b3 = pl.BlockSpec((None, 3, D), lambda b, t, f: (b, 0, 0))
    return pl.pallas_call(
        body, name="ffn_bwd", grid=(B, T // tm, 2),
        in_specs=[tok, tok, tok, per_b3,
                  pl.BlockSpec((1, D), lambda b, t, f: (0, 0)),
                  pl.BlockSpec((2, None, D, Fc), lambda b, t, f: (0, f, 0, 0)),
                  pl.BlockSpec((None, Fc, D), lambda b, t, f: (f, 0, 0))],
        out_specs=[tok, tok,
                   pl.BlockSpec((None, tm, Fc), lambda b, t, f: (b, t, f)),
                   pl.BlockSpec((2, None, tm, Fc), lambda b, t, f: (0, b, t, f)),
                   tok, per_b3,
                   pl.BlockSpec((None, 1, D), lambda b, t, f: (b, 0, 0))],
        out_shape=[jax.ShapeDtypeStruct((B, T, D), F32), jax.ShapeDtypeStruct((B, T, D), BF16),
                   jax.ShapeDtypeStruct((B, T, F), BF16), jax.ShapeDtypeStruct((2, B, T, F), BF16),
                   jax.ShapeDtypeStruct((B, T, D), BF16), jax.ShapeDtypeStruct((B, 3, D), F32),
                   jax.ShapeDtypeStruct((B, 1, D), F32)],
        scratch_shapes=[pltpu.VMEM((tm, D), BF16), pltpu.VMEM((tm, D), BF16), pltpu.VMEM((tm, D), F32)],
        compiler_params=_cparams(3),
    )(x, dres, y, mod3, g, w_in, w_out)


def matmul_tn(xm, ym, bm, name):
    N, K = xm.shape
    GY, _, MY = ym.shape
    per = MY // bm
    nb = GY * per
    tn = _tile(N, 512)

    def body(x_ref, y_ref, o_ref, acc_s):
        n = pl.program_id(1)

        @pl.when(n == 0)
        def _():
            acc_s[...] = jnp.zeros_like(acc_s)

        acc_s[...] += _mm_tn(x_ref[...], y_ref[...])

        @pl.when(n == N // tn - 1)
        def _():
            o_ref[...] = acc_s[...].astype(BF16)

    return pl.pallas_call(
        body, name=name, grid=(nb, N // tn),
        in_specs=[pl.BlockSpec((tn, K), lambda m, n: (n, 0)),
                  pl.BlockSpec((None, tn, bm), lambda m, n: (m // per, n, m % per))],
        out_specs=pl.BlockSpec((None, K, bm), lambda m, n: (m, 0, 0)),
        out_shape=jax.ShapeDtypeStruct((nb, K, bm), BF16),
        scratch_shapes=[pltpu.VMEM((K, bm), F32)],
        compiler_params=_cparams(2),
    )(xm, ym)


def final_loss(x, fg, target):
    B, T, D = x.shape
    tm = _tile(T, 512)

    def body(x_ref, g_ref, t_ref, dx_ref, dfg_ref, loss_ref):
        t = pl.program_id(1)

        @pl.when(t == 0)
        def _():
            dfg_ref[...] = jnp.zeros_like(dfg_ref)
            loss_ref[...] = jnp.zeros_like(loss_ref)

        xv = x_ref[...]
        g = g_ref[...]
        r = lax.rsqrt(jnp.mean(xv * xv, axis=-1, keepdims=True) + EPS)
        xn = xv * r
        err = xn * g - t_ref[...]
        tok_loss = jnp.mean(err * err, axis=-1, keepdims=True)
        loss_ref[...] += 0.5 * jnp.sum(tok_loss, axis=0, keepdims=True)
        dy = err * (1.0 / D)
        dfg_ref[...] += _sum0(dy * xn)
        dxn = dy * g
        dx_ref[...] = r * (dxn - xn * jnp.mean(dxn * xn, axis=-1, keepdims=True))

    tok = pl.BlockSpec((None, tm, D), lambda b, t: (b, t, 0))
    return pl.pallas_call(
        body, name="final_loss", grid=(B, T // tm),
        in_specs=[tok, pl.BlockSpec((1, D), lambda b, t: (0, 0)), tok],
        out_specs=[tok, pl.BlockSpec((None, 1, D), lambda b, t: (b, 0, 0)),
                   pl.BlockSpec((None, 1, LANES), lambda b, t: (b, 0, 0))],
        out_shape=[jax.ShapeDtypeStruct((B, T, D), F32), jax.ShapeDtypeStruct((B, 1, D), F32),
                   jax.ShapeDtypeStruct((B, 1, LANES), F32)],
        compiler_params=_cparams(2),
    )(x, fg, target)


def _past_halo_spec(tm, halo, width):
    return pl.BlockSpec((None, halo, width), lambda b, t: (b, jnp.maximum(t * (tm // halo) - 1, 0), 0))


def _future_halo_spec(tm, halo, width, T):
    return pl.BlockSpec((None, halo, width), lambda b, t: (b, jnp.minimum((t + 1) * (tm // halo), T // halo - 1), 0))


def _glu_fwd(h, w_ref, bias):
    D = h.shape[1]
    a = jnp.concatenate([_mm(h, w_ref[0]), _mm(h, w_ref[1])], axis=1) + bias[:, :D]
    b = jnp.concatenate([_mm(h, w_ref[2]), _mm(h, w_ref[3])], axis=1) + bias[:, D:]
    return a, b


def conv_glu_fwd(x, mod3, g, w_glu, b_glu):
    B, T, D = x.shape
    tm = _tile(T, 512)

    def body(x_ref, mod_ref, g_ref, w_ref, b_ref, u_ref):
        h = _modnorm(x_ref[...], g_ref[...], mod_ref[1:2, :], mod_ref[0:1, :]).astype(BF16)
        a, b = _glu_fwd(h, w_ref, b_ref[...])
        u_ref[...] = a * _sigmoid(b)

    tok = pl.BlockSpec((None, tm, D), lambda b, t: (b, t, 0))
    return pl.pallas_call(
        body, name="conv_glu_fwd", grid=(B, T // tm),
        in_specs=[tok, pl.BlockSpec((None, 3, D), lambda b, t: (b, 0, 0)),
                  pl.BlockSpec((1, D), lambda b, t: (0, 0)),
                  pl.BlockSpec((4, D, D // 2), lambda b, t: (0, 0, 0)),
                  pl.BlockSpec((1, 2 * D), lambda b, t: (0, 0))],
        out_specs=tok, out_shape=jax.ShapeDtypeStruct((B, T, D), F32),
        compiler_params=_cparams(2),
    )(x, mod3, g, w_glu, b_glu)


def _layer_norm_parts(u2):
    mu = jnp.mean(u2, axis=-1, keepdims=True)
    xc = u2 - mu
    rs = lax.rsqrt(jnp.mean(xc * xc, axis=-1, keepdims=True) + EPS)
    return xc * rs, rs


def conv_out_fwd(x, u, mod3, w_dw, b_dw, ln_g, ln_b, w_pw, b_pw):
    B, T, D = x.shape
    K = w_dw.shape[0] - 1
    tm = _tile(T, 512)

    def body(x_ref, u_ref, halo_ref, mod_ref, wdw_ref, bdw_ref, lg_ref, lb_ref, wpw_ref, bpw_ref,
             xo_ref, y_ref, u2_ref, ext_s):
        t = pl.program_id(1)
        ext_s[0:CONV_HALO, :] = jnp.where(t > 0, halo_ref[...], 0.0)
        ext_s[CONV_HALO:, :] = u_ref[...]
        acc = jnp.broadcast_to(bdw_ref[...], (tm, D))
        for k in range(K):
            acc = acc + wdw_ref[k:k + 1, :] * ext_s[pl.ds(CONV_HALO - (K - 1) + k, tm), :]
        u2_ref[...] = acc
        xh, _ = _layer_norm_parts(acc)
        l = xh * lg_ref[...] + lb_ref[...]
        u3 = l * _sigmoid(l)
        y = _mm(u3, wpw_ref[...]) + bpw_ref[...]
        y_ref[...] = y
        xo_ref[...] = x_ref[...] + (1.0 + mod_ref[2:3, :]) * y

    tok = pl.BlockSpec((None, tm, D), lambda b, t: (b, t, 0))
    vec = pl.BlockSpec((1, D), lambda b, t: (0, 0))
    return pl.pallas_call(
        body, name="conv_out_fwd", grid=(B, T // tm),
        in_specs=[tok, tok, _past_halo_spec(tm, CONV_HALO, D), pl.BlockSpec((None, 3, D), lambda b, t: (b, 0, 0)),
                  pl.BlockSpec((K + 1, D), lambda b, t: (0, 0)), vec, vec, vec,
                  pl.BlockSpec((D, D), lambda b, t: (0, 0)), vec],
        out_specs=[tok, tok, tok], out_shape=[jax.ShapeDtypeStruct((B, T, D), F32)] * 3,
        scratch_shapes=[pltpu.VMEM((tm + CONV_HALO, D), F32)],
        compiler_params=_cparams(2),
    )(x, u, u, mod3, w_dw, b_dw, ln_g, ln_b, w_pw, b_pw)


def conv_out_bwd(dres, y, u2, mod3, ln_g, ln_b, w_pw):
    B, T, D = dres.shape
    tm = _tile(T, 512)

    def body(dres_ref, y_ref, u2_ref, mod_ref, lg_ref, lb_ref, wpw_ref, du2_ref, u3_ref, dy_ref, dgate_ref, vec_ref):
        t = pl.program_id(1)

        @pl.when(t == 0)
        def _():
            dgate_ref[...] = jnp.zeros_like(dgate_ref)
            vec_ref[...] = jnp.zeros_like(vec_ref)

        dres = dres_ref[...]
        dy = (1.0 + mod_ref[2:3, :]) * dres
        dy_ref[...] = dy.astype(BF16)
        dgate_ref[...] += _sum0(dres * y_ref[...])
        xh, rs = _layer_norm_parts(u2_ref[...])
        lg = lg_ref[...]
        l = xh * lg + lb_ref[...]
        sg = _sigmoid(l)
        u3_ref[...] = (l * sg).astype(BF16)
        du3 = _mm_nt(dy, wpw_ref[...])
        dl = du3 * _dsilu(l, sg)
        dxh = dl * lg
        du2 = rs * (dxh - jnp.mean(dxh, axis=-1, keepdims=True) - xh * jnp.mean(dxh * xh, axis=-1, keepdims=True))
        du2_ref[...] = du2
        vec_ref[0:1, :] += _sum0(dy)
        vec_ref[1:2, :] += _sum0(dl * xh)
        vec_ref[2:3, :] += _sum0(dl)
        vec_ref[3:4, :] += _sum0(du2)

    tok = pl.BlockSpec((None, tm, D), lambda b, t: (b, t, 0))
    tokb = pl.BlockSpec((None, tm, D), lambda b, t: (b, t, 0))
    vec = pl.BlockSpec((1, D), lambda b, t: (0, 0))
    return pl.pallas_call(
        body, name="conv_out_bwd", grid=(B, T // tm),
        in_specs=[tok, tok, tok, pl.BlockSpec((None, 3, D), lambda b, t: (b, 0, 0)), vec, vec,
                  pl.BlockSpec((D, D), lambda b, t: (0, 0))],
        out_specs=[tok, tokb, tokb, pl.BlockSpec((None, 1, D), lambda b, t: (b, 0, 0)),
                   pl.BlockSpec((None, 4, D), lambda b, t: (b, 0, 0))],
        out_shape=[jax.ShapeDtypeStruct((B, T, D), F32), jax.ShapeDtypeStruct((B, T, D), BF16),
                   jax.ShapeDtypeStruct((B, T, D), BF16), jax.ShapeDtypeStruct((B, 1, D), F32),
                   jax.ShapeDtypeStruct((B, 4, D), F32)],
        compiler_params=_cparams(2),
    )(dres, y, u2, mod3, ln_g, ln_b, w_pw)


def conv_glu_bwd(x, dres, du2, u, mod3, g, w_glu, b_glu, w_dw):
    B, T, D = x.shape
    K = w_dw.shape[0] - 1
    tm = _tile(T, 256)
    nt = T // tm

    def body(x_ref, dres_ref, du2_ref, du2h_ref, u_ref, uh_ref, mod_ref, g_ref, w_ref, b_ref, wdw_ref,
             dx_ref, h_ref, dab_ref, dwdw_ref, dbglu_ref, dmod_ref, dg_ref, extu_s, extd_s):
        t = pl.program_id(1)

        @pl.when(t == 0)
        def _():
            dwdw_ref[...] = jnp.zeros_like(dwdw_ref)
            dbglu_ref[...] = jnp.zeros_like(dbglu_ref)
            dmod_ref[...] = jnp.zeros_like(dmod_ref)
            dg_ref[...] = jnp.zeros_like(dg_ref)

        du2 = du2_ref[...]
        extu_s[0:CONV_HALO, :] = jnp.where(t > 0, uh_ref[...], 0.0)
        extu_s[CONV_HALO:, :] = u_ref[...]
        extd_s[0:tm, :] = du2
        extd_s[tm:, :] = jnp.where(t < nt - 1, du2h_ref[...], 0.0)
        du = jnp.zeros((tm, D), F32)
        for k in range(K):
            du = du + wdw_ref[k:k + 1, :] * extd_s[pl.ds(K - 1 - k, tm), :]
            dwdw_ref[k:k + 1, :] += _sum0(du2 * extu_s[pl.ds(CONV_HALO - (K - 1) + k, tm), :])
        xv = x_ref[...]
        h = _modnorm(xv, g_ref[...], mod_ref[1:2, :], mod_ref[0:1, :]).astype(BF16)
        h_ref[...] = h
        a, b = _glu_fwd(h, w_ref, b_ref[...])
        sb = _sigmoid(b)
        da = du * sb
        db = du * a * sb * (1.0 - sb)
        dbglu_ref[:, 0:D] += _sum0(da)
        dbglu_ref[:, D:] += _sum0(db)
        da = da.astype(BF16)
        db = db.astype(BF16)
        dab_ref[:, 0:D] = da
        dab_ref[:, D:] = db
        Dh2 = D // 2
        dh = (_mm_nt(da[:, :Dh2], w_ref[0]) + _mm_nt(da[:, Dh2:], w_ref[1])
              + _mm_nt(db[:, :Dh2], w_ref[2]) + _mm_nt(db[:, Dh2:], w_ref[3]))
        dxn, dg, dscale, dshift = _modnorm_bwd(xv, g_ref[...], mod_ref[1:2, :], dh)
        dx_ref[...] = dres_ref[...] + dxn
        dmod_ref[0:1, :] += dshift
        dmod_ref[1:2, :] += dscale
        dg_ref[...] += dg

    tok = pl.BlockSpec((None, tm, D), lambda b, t: (b, t, 0))
    return pl.pallas_call(
        body, name="conv_glu_bwd", grid=(B, nt),
        in_specs=[tok, tok, tok, _future_halo_spec(tm, CONV_HALO, D, T), tok, _past_halo_spec(tm, CONV_HALO, D),
                  pl.BlockSpec((None, 3, D), lambda b, t: (b, 0, 0)), pl.BlockSpec((1, D), lambda b, t: (0, 0)),
                  pl.BlockSpec((4, D, D // 2), lambda b, t: (0, 0, 0)), pl.BlockSpec((1, 2 * D), lambda b, t: (0, 0)),
                  pl.BlockSpec((K + 1, D), lambda b, t: (0, 0))],
        out_specs=[tok, tok, pl.BlockSpec((None, tm, 2 * D), lambda b, t: (b, t, 0)),
                   pl.BlockSpec((None, K + 1, D), lambda b, t: (b, 0, 0)),
                   pl.BlockSpec((None, 1, 2 * D), lambda b, t: (b, 0, 0)),
                   pl.BlockSpec((None, 3, D), lambda b, t: (b, 0, 0)),
                   pl.BlockSpec((None, 1, D), lambda b, t: (b, 0, 0))],
        out_shape=[jax.ShapeDtypeStruct((B, T, D), F32), jax.ShapeDtypeStruct((B, T, D), BF16),
                   jax.ShapeDtypeStruct((B, T, 2 * D), BF16), jax.ShapeDtypeStruct((B, K + 1, D), F32),
                   jax.ShapeDtypeStruct((B, 1, 2 * D), F32), jax.ShapeDtypeStruct((B, 3, D), F32),
                   jax.ShapeDtypeStruct((B, 1, D), F32)],
        scratch_shapes=[pltpu.VMEM((tm + CONV_HALO, D), F32), pltpu.VMEM((tm + CONV_HALO, D), F32)],
        compiler_params=_cparams(2),
    )(x, dres, du2, du2, u, u, mod3, g, w_glu, b_glu, w_dw)


def dn_proj_fwd(x, mod3, g, w_main, w_ab):
    B, T, D = x.shape
    W = w_main.shape[1] // 4
    tm = _tile(T, 512)

    def body(x_ref, mod_ref, g_ref, wm_ref, wab_ref, pre_ref, z_ref, ab_ref):
        h = _modnorm(x_ref[...], g_ref[...], mod_ref[1:2, :], mod_ref[0:1, :]).astype(BF16)
        for p in range(3):
            pre_ref[:, p * W:(p + 1) * W] = _mm(h, wm_ref[:, p * W:(p + 1) * W])
        z_ref[...] = _mm(h, wm_ref[:, 3 * W:])
        ab_ref[...] = _mm(h, wab_ref[...])

    return pl.pallas_call(
        body, name="dn_proj_fwd", grid=(B, T // tm),
        in_specs=[pl.BlockSpec((None, tm, D), lambda b, t: (b, t, 0)), pl.BlockSpec((None, 3, D), lambda b, t: (b, 0, 0)),
                  pl.BlockSpec((1, D), lambda b, t: (0, 0)), pl.BlockSpec((D, 4 * W), lambda b, t: (0, 0)),
                  pl.BlockSpec((D, LANES), lambda b, t: (0, 0))],
        out_specs=[pl.BlockSpec((None, tm, 3 * W), lambda b, t: (b, t, 0)),
                   pl.BlockSpec((None, tm, W), lambda b, t: (b, t, 0)),
                   pl.BlockSpec((None, tm, LANES), lambda b, t: (b, t, 0))],
        out_shape=[jax.ShapeDtypeStruct((B, T, 3 * W), F32), jax.ShapeDtypeStruct((B, T, W), F32),
                   jax.ShapeDtypeStruct((B, T, LANES), F32)],
        compiler_params=_cparams(2),
    )(x, mod3, g, w_main, w_ab)


def _sconv(ext_s, w_ref, tm, K):
    acc = w_ref[0:1, :] * ext_s[pl.ds(SCONV_HALO - (K - 1), tm), :]
    for k in range(1, K):
        acc = acc + w_ref[k:k + 1, :] * ext_s[pl.ds(SCONV_HALO - (K - 1) + k, tm), :]
    return acc


def _lane_col(val, lane, idx):
    return jnp.sum(jnp.where(lane == idx, val, 0.0), axis=1, keepdims=True)


def dn_conv_fwd(pre, ab, w_sconv, alog_row, dt_row, H):
    B, T, W3 = pre.shape
    W = W3 // 3
    Dh = W // H
    K = w_sconv.shape[0]
    tm = _tile(T, 512)

    def body(pre_ref, halo_ref, ab_ref, w_ref, alog_ref, dt_ref, q_ref, k_ref, v_ref, gb_ref, bb_ref, ext_s):
        t = pl.program_id(1)
        ext_s[0:SCONV_HALO, :] = jnp.where(t > 0, halo_ref[...], 0.0)
        ext_s[SCONV_HALO:, :] = pre_ref[...]
        cv = _sconv(ext_s, w_ref, tm, K)
        qkv = cv * _sigmoid(cv)
        ab = ab_ref[...]
        lane = lax.broadcasted_iota(jnp.int32, ab.shape, 1)
        g_all = -jnp.exp(alog_ref[...]) * _softplus(ab + dt_ref[...])
        beta_all = _sigmoid(ab)
        for h in range(H):
            q_ref[h] = qkv[:, h * Dh:(h + 1) * Dh]
            k_ref[h] = qkv[:, W + h * Dh:W + (h + 1) * Dh]
            v_ref[h] = qkv[:, 2 * W + h * Dh:2 * W + (h + 1) * Dh]
            gb_ref[h] = jnp.broadcast_to(_lane_col(g_all, lane, h), (tm, Dh))
            bb_ref[h] = jnp.broadcast_to(_lane_col(beta_all, lane, H + h), (tm, Dh))

    hm = pl.BlockSpec((None, H, tm, Dh), lambda b, t: (b, 0, t, 0))
    row = pl.BlockSpec((1, LANES), lambda b, t: (0, 0))
    return pl.pallas_call(
        body, name="dn_conv_fwd", grid=(B, T // tm),
        in_specs=[pl.BlockSpec((None, tm, W3), lambda b, t: (b, t, 0)), _past_halo_spec(tm, SCONV_HALO, W3),
                  pl.BlockSpec((None, tm, LANES), lambda b, t: (b, t, 0)),
                  pl.BlockSpec((K, W3), lambda b, t: (0, 0)), row, row],
        out_specs=[hm] * 5, out_shape=[jax.ShapeDtypeStruct((B, H, T, Dh), F32)] * 5,
        scratch_shapes=[pltpu.VMEM((tm + SCONV_HALO, W3), F32)],
        compiler_params=_cparams(2),
    )(pre, pre, ab, w_sconv, alog_row, dt_row)


def _bdot(spec):
    return lambda a, b: jnp.einsum(spec, a.astype(BF16), b.astype(BF16), preferred_element_type=F32)


_NN, _NT, _TN = "gij,gjk->gik", "gik,gjk->gij", "gki,gkj->gij"


def _make_bdots():
    nn_, nt_, tn_ = _bdot(_NN), _bdot(_NT), _bdot(_TN)

    @jax.custom_vjp
    def nn(a, b):
        return nn_(a, b)

    @jax.custom_vjp
    def nt(a, b):
        return nt_(a, b)

    @jax.custom_vjp
    def tn(a, b):
        return tn_(a, b)

    nn.defvjp(lambda a, b: (nn_(a, b), (a, b)), lambda r, d: (nt_(d, r[1]), tn_(r[0], d)))
    nt.defvjp(lambda a, b: (nt_(a, b), (a, b)), lambda r, d: (nn_(d, r[1]), tn_(d, r[0])))
    tn.defvjp(lambda a, b: (tn_(a, b), (a, b)), lambda r, d: (nt_(r[1], d), nn_(r[0], d)))
    return nn, nt, tn


def _unit_lower_inverse(A):
    hdot = functools.partial(jnp.einsum, precision=lax.Precision.HIGH, preferred_element_type=F32)
    C = A.shape[-1]

    def impl(A):
        eye = (lax.broadcasted_iota(jnp.int32, A.shape, 1) == lax.broadcasted_iota(jnp.int32, A.shape, 2)).astype(F32)
        Tm = eye - A
        Ap = A
        for _ in range(max(1, (C - 1).bit_length()) - 1):
            Ap = hdot(_NN, Ap, Ap)
            Tm = Tm + hdot(_NN, Tm, Ap)
        return Tm

    @jax.custom_vjp
    def inv(A):
        return impl(A)

    def fwd(A):
        Tm = impl(A)
        return Tm, Tm

    def bwd(Tm, dT):
        return (-hdot(_NT, hdot(_TN, Tm, dT), Tm),)

    inv.defvjp(fwd, bwd)
    return inv(A)


def _chunk_fn(q, k, v, gb, bb, S):
    nn, nt, tn = _make_bdots()
    G, C, Dh = q.shape
    hdot = functools.partial(jnp.einsum, precision=HI, preferred_element_type=F32)
    q = q * lax.rsqrt(jnp.sum(q * q, axis=-1, keepdims=True) + EPS) * (Dh ** -0.5)
    k = k * lax.rsqrt(jnp.sum(k * k, axis=-1, keepdims=True) + EPS)
    row = lax.broadcasted_iota(jnp.int32, (G, C, C), 1)
    col = lax.broadcasted_iota(jnp.int32, (G, C, C), 2)
    causal = row >= col
    strict = row > col
    gc = hdot(_NN, causal.astype(F32), gb)
    spread = jnp.full((G, C, Dh), 1.0 / Dh, F32)
    gi = hdot(_NT, gc, spread)
    gj = hdot(_NT, spread, gc)
    decay = jnp.where(causal, jnp.exp(jnp.where(causal, gi - gj, 0.0)), 0.0)
    kb = k * bb
    vb = v * bb
    A = jnp.where(strict, nt(kb, k) * decay, 0.0)
    Tm = _unit_lower_inverse(A)
    eg = jnp.exp(gc)
    u = nn(Tm, vb)
    w = nn(Tm, kb * eg)
    qg = q * eg
    intra = nt(q, k) * decay
    glast = hdot(_NN, jnp.ones((G, C, C), F32), gb)
    kd = k * jnp.exp(glast - gc)
    v_new = u - nn(w, S)
    o = nn(qg, S) + nn(intra, v_new)
    egl = jnp.exp(glast)
    S_new = S * jnp.concatenate([egl] * (Dh // C), axis=1) + tn(kd, v_new)
    return o, S_new


def dn_chunk_fwd(q, k, v, gb, bb):
    B, H, T, Dh = q.shape
    NC = T // CHUNK

    def body(q_ref, k_ref, v_ref, gb_ref, bb_ref, o_ref, sp_ref, S_s):
        @pl.when(pl.program_id(1) == 0)
        def _():
            S_s[...] = jnp.zeros_like(S_s)

        S = S_s[...]
        sp_ref[...] = S
        o, S_new = _chunk_fn(q_ref[...], k_ref[...], v_ref[...], gb_ref[...], bb_ref[...], S)
        o_ref[...] = o
        S_s[...] = S_new

    hm = pl.BlockSpec((None, H, CHUNK, Dh), lambda b, n: (b, 0, n, 0))
    return pl.pallas_call(
        body, name="dn_chunk_fwd", grid=(B, NC),
        in_specs=[hm] * 5,
        out_specs=[hm, pl.BlockSpec((None, None, H, Dh, Dh), lambda b, n: (b, n, 0, 0, 0))],
        out_shape=[jax.ShapeDtypeStruct((B, H, T, Dh), F32), jax.ShapeDtypeStruct((B, NC, H, Dh, Dh), F32)],
        scratch_shapes=[pltpu.VMEM((H, Dh, Dh), F32)],
        compiler_params=_cparams(2),
    )(q, k, v, gb, bb)


def dn_chunk_bwd(q, k, v, gb, bb, s_prev, do):
    B, H, T, Dh = q.shape
    NC = T // CHUNK

    def body(q_ref, k_ref, v_ref, gb_ref, bb_ref, sp_ref, do_ref, dq_ref, dk_ref, dv_ref, dgb_ref, dbb_ref, dS_s):
        @pl.when(pl.program_id(1) == 0)
        def _():
            dS_s[...] = jnp.zeros_like(dS_s)

        _, vjp = jax.vjp(_chunk_fn, q_ref[...], k_ref[...], v_ref[...], gb_ref[...], bb_ref[...], sp_ref[...])
        dq, dk, dv, dgb, dbb, dS = vjp((do_ref[...], dS_s[...]))
        dq_ref[...] = dq
        dk_ref[...] = dk
        dv_ref[...] = dv
        dgb_ref[...] = dgb
        dbb_ref[...] = dbb
        dS_s[...] = dS

    hm = pl.BlockSpec((None, H, CHUNK, Dh), lambda b, n: (b, 0, NC - 1 - n, 0))
    return pl.pallas_call(
        body, name="dn_chunk_bwd", grid=(B, NC),
        in_specs=[hm] * 5 + [pl.BlockSpec((None, None, H, Dh, Dh), lambda b, n: (b, NC - 1 - n, 0, 0, 0)), hm],
        out_specs=[hm] * 5, out_shape=[jax.ShapeDtypeStruct((B, H, T, Dh), F32)] * 5,
        scratch_shapes=[pltpu.VMEM((H, Dh, Dh), F32)],
        compiler_params=_cparams(2),
    )(q, k, v, gb, bb, s_prev, do)


def _head_norm(o, og):
    r = lax.rsqrt(jnp.mean(o * o, axis=-1, keepdims=True) + EPS)
    return o * r, r


def dn_out_fwd(x, o, z, mod3, o_g, w_out):
    B, T, D = x.shape
    _, H, _, Dh = o.shape
    W = H * Dh
    tm = _tile(T, 512)

    def body(x_ref, o_ref, z_ref, mod_ref, og_ref, w_ref, xo_ref, y_ref):
        parts = []
        for h in range(H):
            on, _ = _head_norm(o_ref[h], og_ref[...])
            zz = z_ref[:, h * Dh:(h + 1) * Dh]
            parts.append((on * og_ref[...] * (zz * _sigmoid(zz))).astype(BF16))
        y = _mm(jnp.concatenate(parts, axis=1), w_ref[...])
        y_ref[...] = y
        xo_ref[...] = x_ref[...] + (1.0 + mod_ref[2:3, :]) * y

    tok = pl.BlockSpec((None, tm, D), lambda b, t: (b, t, 0))
    return pl.pallas_call(
        body, name="dn_out_fwd", grid=(B, T // tm),
        in_specs=[tok, pl.BlockSpec((None, H, tm, Dh), lambda b, t: (b, 0, t, 0)),
                  pl.BlockSpec((None, tm, W), lambda b, t: (b, t, 0)), pl.BlockSpec((None, 3, D), lambda b, t: (b, 0, 0)),
                  pl.BlockSpec((1, Dh), lambda b, t: (0, 0)), pl.BlockSpec((W, D), lambda b, t: (0, 0))],
        out_specs=[tok, tok], out_shape=[jax.ShapeDtypeStruct((B, T, D), F32)] * 2,
        compiler_params=_cparams(2),
    )(x, o, z, mod3, o_g, w_out)


def dn_out_bwd(dres, y, o, z, mod3, o_g, w_out):
    B, T, D = dres.shape
    _, H, _, Dh = o.shape
    W = H * Dh
    tm = _tile(T, 512)

    def body(dres_ref, y_ref, o_ref, z_ref, mod_ref, og_ref, w_ref, do_ref, dz_ref, ogb_ref, dy_ref, dgate_ref, dog_ref):
        t = pl.program_id(1)

        @pl.when(t == 0)
        def _():
            dgate_ref[...] = jnp.zeros_like(dgate_ref)
            dog_ref[...] = jnp.zeros_like(dog_ref)

        dres = dres_ref[...]
        dy = ((1.0 + mod_ref[2:3, :]) * dres).astype(BF16)
        dy_ref[...] = dy
        dgate_ref[...] += _sum0(dres * y_ref[...])
        dog = _mm_nt(dy, w_ref[...])
        og = og_ref[...]
        for h in range(H):
            ov = o_ref[h]
            xn, r = _head_norm(ov, og)
            zz = z_ref[:, h * Dh:(h + 1) * Dh]
            sg = _sigmoid(zz)
            sz = zz * sg
            d = dog[:, h * Dh:(h + 1) * Dh]
            ogb_ref[:, h * Dh:(h + 1) * Dh] = (xn * og * sz).astype(BF16)
            dz_ref[:, h * Dh:(h + 1) * Dh] = d * (xn * og) * _dsilu(zz, sg)
            don = d * sz
            dog_ref[...] += _sum0(don * xn)
            dxn = don * og
            do_ref[h] = r * (dxn - xn * jnp.mean(dxn * xn, axis=-1, keepdims=True))

    tok = pl.BlockSpec((None, tm, D), lambda b, t: (b, t, 0))
    tokw = pl.BlockSpec((None, tm, W), lambda b, t: (b, t, 0))
    hm = pl.BlockSpec((None, H, tm, Dh), lambda b, t: (b, 0, t, 0))
    return pl.pallas_call(
        body, name="dn_out_bwd", grid=(B, T // tm),
        in_specs=[tok, tok, hm, tokw, pl.BlockSpec((None, 3, D), lambda b, t: (b, 0, 0)),
                  pl.BlockSpec((1, Dh), lambda b, t: (0, 0)), pl.BlockSpec((W, D), lambda b, t: (0, 0))],
        out_specs=[hm, tokw, tokw, tok, pl.BlockSpec((None, 1, D), lambda b, t: (b, 0, 0)),
                   pl.BlockSpec((None, 1, Dh), lambda b, t: (b, 0, 0))],
        out_shape=[jax.ShapeDtypeStruct((B, H, T, Dh), F32), jax.ShapeDtypeStruct((B, T, W), F32),
                   jax.ShapeDtypeStruct((B, T, W), BF16), jax.ShapeDtypeStruct((B, T, D), BF16),
                   jax.ShapeDtypeStruct((B, 1, D), F32), jax.ShapeDtypeStruct((B, 1, Dh), F32)],
        compiler_params=_cparams(2),
    )(dres, y, o, z, mod3, o_g, w_out)


def dn_conv_bwd(dq, dk, dv, dgb, dbb, pre, ab, w_sconv, alog_row, dt_row):
    B, H, T, Dh = dq.shape
    W = H * Dh
    W3 = 3 * W
    K = w_sconv.shape[0]
    tm = _tile(T, 256)

    def body(dq_ref, dk_ref, dv_ref, dgb_ref, dbb_ref, pre_ref, halo_ref, ab_ref, w_ref, alog_ref, dt_ref,
             dc_ref, dab_ref, small_ref, ext_s):
        t = pl.program_id(1)

        @pl.when(t == 0)
        def _():
            small_ref[...] = jnp.zeros_like(small_ref)

        ext_s[0:SCONV_HALO, :] = jnp.where(t > 0, halo_ref[...], 0.0)
        ext_s[SCONV_HALO:, :] = pre_ref[...]
        cv = _sconv(ext_s, w_ref, tm, K)
        dsl = _dsilu(cv, _sigmoid(cv))
        ab = ab_ref[...]
        lane = lax.broadcasted_iota(jnp.int32, ab.shape, 1)
        dg_all = jnp.zeros_like(ab)
        db_all = jnp.zeros_like(ab)
        for h in range(H):
            dc_ref[:, h * Dh:(h + 1) * Dh] = dq_ref[h] * dsl[:, h * Dh:(h + 1) * Dh]
            dc_ref[:, W + h * Dh:W + (h + 1) * Dh] = dk_ref[h] * dsl[:, W + h * Dh:W + (h + 1) * Dh]
            dc_ref[:, 2 * W + h * Dh:2 * W + (h + 1) * Dh] = dv_ref[h] * dsl[:, 2 * W + h * Dh:2 * W + (h + 1) * Dh]
            dg_all = dg_all + jnp.where(lane == h, jnp.sum(dgb_ref[h], axis=1, keepdims=True), 0.0)
            db_all = db_all + jnp.where(lane == H + h, jnp.sum(dbb_ref[h], axis=1, keepdims=True), 0.0)
        xa = ab + dt_ref[...]
        ea = -jnp.exp(alog_ref[...])
        g_all = ea * _softplus(xa)
        da = dg_all * ea * _sigmoid(xa)
        beta = _sigmoid(ab)
        dab_ref[...] = da + db_all * beta * (1.0 - beta)
        small_ref[0:1, :] += _sum0(dg_all * g_all)
        small_ref[1:2, :] += _sum0(da)

    hm = pl.BlockSpec((None, H, tm, Dh), lambda b, t: (b, 0, t, 0))
    row = pl.BlockSpec((1, LANES), lambda b, t: (0, 0))
    return pl.pallas_call(
        body, name="dn_conv_bwd", grid=(B, T // tm),
        in_specs=[hm] * 5 + [pl.BlockSpec((None, tm, W3), lambda b, t: (b, t, 0)), _past_halo_spec(tm, SCONV_HALO, W3),
                             pl.BlockSpec((None, tm, LANES), lambda b, t: (b, t, 0)),
                             pl.BlockSpec((K, W3), lambda b, t: (0, 0)), row, row],
        out_specs=[pl.BlockSpec((None, tm, W3), lambda b, t: (b, t, 0)), pl.BlockSpec((None, tm, LANES), lambda b, t: (b, t, 0)),
                   pl.BlockSpec((None, 2, LANES), lambda b, t: (b, 0, 0))],
        out_shape=[jax.ShapeDtypeStruct((B, T, W3), F32), jax.ShapeDtypeStruct((B, T, LANES), F32),
                   jax.ShapeDtypeStruct((B, 2, LANES), F32)],
        scratch_shapes=[pltpu.VMEM((tm + SCONV_HALO, W3), F32)],
        compiler_params=_cparams(2),
    )(dq, dk, dv, dgb, dbb, pre, pre, ab, w_sconv, alog_row, dt_row)


def dn_proj_bwd(x, dres, dc, pre, dz, dab, mod3, g, w_main, w_ab, w_sconv):
    B, T, D = x.shape
    W3 = dc.shape[2]
    W = W3 // 3
    K = w_sconv.shape[0]
    tm = _tile(T, 256)
    nt = T // tm

    def body(x_ref, dres_ref, dc_ref, dch_ref, pre_ref, preh_ref, dz_ref, dab_ref, mod_ref, g_ref, wm_ref, wab_ref, ws_ref,
             dx_ref, h_ref, dproj_ref, dws_ref, dmod_ref, dg_ref, extp_s, extd_s):
        t = pl.program_id(1)

        @pl.when(t == 0)
        def _():
            dws_ref[...] = jnp.zeros_like(dws_ref)
            dmod_ref[...] = jnp.zeros_like(dmod_ref)
            dg_ref[...] = jnp.zeros_like(dg_ref)

        dc = dc_ref[...]
        extp_s[0:SCONV_HALO, :] = jnp.where(t > 0, preh_ref[...], 0.0)
        extp_s[SCONV_HALO:, :] = pre_ref[...]
        extd_s[0:tm, :] = dc
        extd_s[tm:, :] = jnp.where(t < nt - 1, dch_ref[...], 0.0)
        dpre = jnp.zeros((tm, W3), F32)
        for k in range(K):
            dpre = dpre + ws_ref[k:k + 1, :] * extd_s[pl.ds(K - 1 - k, tm), :]
            dws_ref[k:k + 1, :] += _sum0(dc * extp_s[pl.ds(SCONV_HALO - (K - 1) + k, tm), :])
        dpre = dpre.astype(BF16)
        dzb = dz_ref[...].astype(BF16)
        dproj_ref[:, 0:W3] = dpre
        dproj_ref[:, W3:] = dzb
        dh = _mm_nt(dab_ref[...], wab_ref[...]) + _mm_nt(dzb, wm_ref[:, W3:])
        for p in range(3):
            dh = dh + _mm_nt(dpre[:, p * W:(p + 1) * W], wm_ref[:, p * W:(p + 1) * W])
        xv = x_ref[...]
        h_ref[...] = _modnorm(xv, g_ref[...], mod_ref[1:2, :], mod_ref[0:1, :]).astype(BF16)
        dxn, dg, dscale, dshift = _modnorm_bwd(xv, g_ref[...], mod_ref[1:2, :], dh)
        dx_ref[...] = dres_ref[...] + dxn
        dmod_ref[0:1, :] += dshift
        dmod_ref[1:2, :] += dscale
        dg_ref[...] += dg

    tok = pl.BlockSpec((None, tm, D), lambda b, t: (b, t, 0))
    tok3 = pl.BlockSpec((None, tm, W3), lambda b, t: (b, t, 0))
    return pl.pallas_call(
        body, name="dn_proj_bwd", grid=(B, nt),
        in_specs=[tok, tok, tok3, _future_halo_spec(tm, SCONV_HALO, W3, T), tok3, _past_halo_spec(tm, SCONV_HALO, W3),
                  pl.BlockSpec((None, tm, W), lambda b, t: (b, t, 0)), pl.BlockSpec((None, tm, LANES), lambda b, t: (b, t, 0)),
                  pl.BlockSpec((None, 3, D), lambda b, t: (b, 0, 0)), pl.BlockSpec((1, D), lambda b, t: (0, 0)),
                  pl.BlockSpec((D, 4 * W), lambda b, t: (0, 0)), pl.BlockSpec((D, LANES), lambda b, t: (0, 0)),
                  pl.BlockSpec((K, W3), lambda b, t: (0, 0))],
        out_specs=[tok, tok, pl.BlockSpec((None, tm, 4 * W), lambda b, t: (b, t, 0)),
                   pl.BlockSpec((None, K, W3), lambda b, t: (b, 0, 0)), pl.BlockSpec((None, 3, D), lambda b, t: (b, 0, 0)),
                   pl.BlockSpec((None, 1, D), lambda b, t: (b, 0, 0))],
        out_shape=[jax.ShapeDtypeStruct((B, T, D), F32), jax.ShapeDtypeStruct((B, T, D), BF16),
                   jax.ShapeDtypeStruct((B, T, 4 * W), BF16), jax.ShapeDtypeStruct((B, K, W3), F32),
                   jax.ShapeDtypeStruct((B, 3, D), F32), jax.ShapeDtypeStruct((B, 1, D), F32)],
        scratch_shapes=[pltpu.VMEM((tm + SCONV_HALO, W3), F32), pltpu.VMEM((tm + SCONV_HALO, W3), F32)],
        compiler_params=_cparams(2),
    )(x, dres, dc, dc, pre, pre, dz, dab, mod3, g, w_main, w_ab, w_sconv)


def ada_fwd(c_all, w_ada, b_cols):
    L, D, Ca = w_ada.shape
    NB = c_all.shape[0]

    def body(c_ref, w_ref, b_ref, o_ref):
        cv = c_ref[...]
        o_ref[...] = _mm(cv * _sigmoid(cv), w_ref[...]) + b_ref[...]

    return pl.pallas_call(
        body, name="ada_fwd", grid=(L,),
        in_specs=[pl.BlockSpec((NB, D), lambda i: (0, 0)), pl.BlockSpec((None, D, Ca), lambda i: (i, 0, 0)),
                  pl.BlockSpec((None, 1, Ca), lambda i: (i, 0, 0))],
        out_specs=pl.BlockSpec((None, NB, Ca), lambda i: (i, 0, 0)),
        out_shape=jax.ShapeDtypeStruct((L, NB, Ca), F32),
        compiler_params=_cparams(1),
    )(c_all, w_ada, b_cols)


def ada_bwd(c_all, dmod_cols, dmod_all):
    L, NB, Ca = dmod_cols.shape
    D = c_all.shape[1]
    C9 = dmod_all.shape[2]

    def body(c_ref, dc_ref, da_ref, gw_ref, gb_ref):
        cv = c_ref[...]
        gw_ref[...] = _mm_tn(cv * _sigmoid(cv), dc_ref[...])
        gb_ref[...] = _sum0(da_ref[...])

    return pl.pallas_call(
        body, name="ada_bwd", grid=(L,),
        in_specs=[pl.BlockSpec((NB, D), lambda i: (0, 0)), pl.BlockSpec((None, NB, Ca), lambda i: (i, 0, 0)),
                  pl.BlockSpec((None, NB, C9), lambda i: (i, 0, 0))],
        out_specs=[pl.BlockSpec((None, D, Ca), lambda i: (i, 0, 0)), pl.BlockSpec((None, 1, C9), lambda i: (i, 0, 0))],
        out_shape=[jax.ShapeDtypeStruct((L, D, Ca), F32), jax.ShapeDtypeStruct((L, 1, C9), F32)],
        compiler_params=_cparams(1),
    )(c_all, dmod_cols, dmod_all)


def adamw(w, g, m, v, name):
    R, C = w.shape
    tr = _tile(R, max(8, (1 << 18) // C))

    def body(w_ref, g_ref, m_ref, v_ref, d_ref, mo_ref, vo_ref):
        gv = g_ref[...]
        mn = ADAM_B1 * m_ref[...] + (1.0 - ADAM_B1) * gv
        vn = ADAM_B2 * v_ref[...] + (1.0 - ADAM_B2) * (gv * gv)
        m_hat = mn / (1.0 - ADAM_B1 ** ADAM_STEP)
        v_hat = vn / (1.0 - ADAM_B2 ** ADAM_STEP)
        d_ref[...] = -ADAM_LR * (m_hat / (jnp.sqrt(v_hat) + ADAM_EPS) + ADAM_WD * w_ref[...])
        mo_ref[...] = mn
        vo_ref[...] = vn

    blk = pl.BlockSpec((tr, C), lambda i: (i, 0))
    return pl.pallas_call(
        body, name=name, grid=(R // tr,), in_specs=[blk] * 4, out_specs=[blk] * 3,
        out_shape=[jax.ShapeDtypeStruct((R, C), F32)] * 3, compiler_params=_cparams(1),
    )(w, g, m, v)


def sum_devices(a):
    n, R, C = a.shape

    def body(a_ref, o_ref):
        s = a_ref[0]
        for d in range(1, n):
            s = s + a_ref[d]
        o_ref[...] = s

    return pl.pallas_call(
        body, name="sum_devices", out_shape=jax.ShapeDtypeStruct((R, C), F32),
        compiler_params=pltpu.CompilerParams(vmem_limit_bytes=VMEM_LIMIT_V7X),
    )(a)


def _place():
    x, y, c = lax.axis_index("x"), lax.axis_index("y"), lax.axis_index("c")
    return x, y, c


def _other_chips(x, y):
    return [(2 * (1 - x) + y, 1 - x, y), (2 * x + (1 - y), x, 1 - y), (2 * (1 - x) + (1 - y), 1 - x, 1 - y)]


def allgather8(block):
    m_per, n = block.shape

    def body(x_ref, out_ref, send_sems, recv_sems, local_sem):
        x, y, c = _place()
        me, sibling = (x, y, c), (x, y, 1 - c)
        chips = [(1 - x, y), (x, 1 - y), (1 - x, 1 - y)]

        def rows(px, py, pc):
            return out_ref.at[pl.ds((4 * px + 2 * py + pc) * m_per, m_per), :]

        def copy(k, blk, to, src=None):
            return pltpu.make_async_remote_copy(
                src_ref=rows(*blk) if src is None else src, dst_ref=rows(*blk),
                send_sem=send_sems.at[k], recv_sem=recv_sems.at[k], device_id=to, device_id_type=MESH)

        mine = pltpu.make_async_copy(x_ref, rows(*me), local_sem)
        mine.start()
        first = [copy(0, me, sibling, src=x_ref)]
        first += [copy(1 + j, me, (*chip, c), src=x_ref) for j, chip in enumerate(chips)]
        for cp in first:
            cp.start()
        passed = [copy(4 + j, (*chip, c), sibling) for j, chip in enumerate(chips)]
        for j, chip in enumerate(chips):
            copy(1 + j, (*chip, c), me).wait_recv()
            passed[j].start()
        copy(0, sibling, me).wait_recv()
        for j, chip in enumerate(chips):
            copy(4 + j, (*chip, 1 - c), me).wait_recv()
        for cp in first + passed:
            cp.wait_send()
        mine.wait()

    return pl.pallas_call(
        body, name="allgather8", out_shape=jax.ShapeDtypeStruct((N_DEV * m_per, n), block.dtype),
        in_specs=[pl.BlockSpec(memory_space=pltpu.VMEM)], out_specs=pl.BlockSpec(memory_space=pltpu.VMEM),
        scratch_shapes=[pltpu.SemaphoreType.DMA((7,)), pltpu.SemaphoreType.DMA((7,)), pltpu.SemaphoreType.DMA],
        compiler_params=pltpu.CompilerParams(vmem_limit_bytes=VMEM_LIMIT_V7X),
    )(block)


def _half(ref, c, rh):
    return ref.at[pl.ds(pl.multiple_of(c * rh, 16), rh), :]


def gather_weights(lands):
    K = len(lands)

    def body(*refs):
        ins, outs = refs[:K], refs[K:2 * K]
        ici_send, ici_recv, d2d_send, d2d_recv = refs[2 * K:]
        x, y, c = _place()
        me = 2 * x + y
        sibling = (x, y, 1 - c)
        others = _other_chips(x, y)
        sent = []
        for k in range(K):
            rh = ins[k].shape[1] // 2
            for r, (_, px, py) in enumerate(others):
                cp = pltpu.make_async_remote_copy(
                    src_ref=_half(ins[k].at[me], c, rh), dst_ref=_half(outs[k].at[me], c, rh),
                    send_sem=ici_send.at[k, r], recv_sem=ici_recv.at[k, r], device_id=(px, py, c), device_id_type=MESH)
                cp.start()
                sent.append(cp)
        forwards = []
        for k in range(K):
            rh = ins[k].shape[1] // 2
            for r, (pchip, px, py) in enumerate(others):
                landed = _half(outs[k].at[pchip], c, rh)
                pltpu.make_async_remote_copy(
                    src_ref=landed, dst_ref=landed, send_sem=ici_send.at[k, r], recv_sem=ici_recv.at[k, r],
                    device_id=(px, py, c), device_id_type=MESH).wait_recv()
                fw = pltpu.make_async_remote_copy(
                    src_ref=landed, dst_ref=landed, send_sem=d2d_send.at[k, r], recv_sem=d2d_recv.at[k, r],
                    device_id=sibling, device_id_type=MESH)
                fw.start()
                forwards.append(fw)
        for k in range(K):
            rh = ins[k].shape[1] // 2
            for r, (pchip, _, _) in enumerate(others):
                theirs = _half(outs[k].at[pchip], 1 - c, rh)
                pltpu.make_async_remote_copy(
                    src_ref=theirs, dst_ref=theirs, send_sem=d2d_send.at[k, r], recv_sem=d2d_recv.at[k, r],
                    device_id=sibling, device_id_type=MESH).wait_recv()
        for cp in sent + forwards:
            cp.wait_send()

    return pl.pallas_call(
        body, name="gather_weights",
        out_shape=[jax.ShapeDtypeStruct(s.shape, s.dtype) for s in lands],
        in_specs=[HBM_SPEC] * K, out_specs=[HBM_SPEC] * K, input_output_aliases={k: k for k in range(K)},
        scratch_shapes=[pltpu.SemaphoreType.DMA((K, 3))] * 4,
    )(*lands)


def pair_exchange(grads):
    K = len(grads)

    def body(*refs):
        ins, outs = refs[:K], refs[K:2 * K]
        send_sems, recv_sems = refs[2 * K:]
        x, y, c = _place()
        sibling = (x, y, 1 - c)
        copies = []
        for k in range(K):
            n, r, _ = ins[k].shape
            rh = r // 2
            cp = pltpu.make_async_remote_copy(
                src_ref=ins[k].at[:, pl.ds(pl.multiple_of((1 - c) * rh, 16), rh), :], dst_ref=outs[k],
                send_sem=send_sems.at[k], recv_sem=recv_sems.at[k], device_id=sibling, device_id_type=MESH)
            cp.start()
            copies.append(cp)
        for cp in copies:
            cp.wait_recv()
        for cp in copies:
            cp.wait_send()

    return pl.pallas_call(
        body, name="pair_exchange",
        out_shape=[jax.ShapeDtypeStruct((g.shape[0], g.shape[1] // 2, g.shape[2]), g.dtype) for g in grads],
        in_specs=[HBM_SPEC] * K, out_specs=[HBM_SPEC] * K,
        scratch_shapes=[pltpu.SemaphoreType.DMA((K,))] * 2,
    )(*grads)


def pair_add(grad, recv, c_idx):
    n, r, C = grad.shape
    rh = r // 2
    tr = _tile(rh, max(16, (1 << 19) // C), 16)
    grad = grad.reshape(n, 2, rh, C)

    def body(c_ref, g_ref, r_ref, o_ref):
        o_ref[...] = (g_ref[...].astype(F32) + r_ref[...].astype(F32)).astype(BF16)

    return pl.pallas_call(
        body, name="pair_add",
        grid_spec=pltpu.PrefetchScalarGridSpec(
            num_scalar_prefetch=1, grid=(n, rh // tr),
            in_specs=[pl.BlockSpec((None, None, tr, C), lambda d, i, c_ref: (d, c_ref[0], i, 0)),
                      pl.BlockSpec((None, tr, C), lambda d, i, c_ref: (d, i, 0))],
            out_specs=pl.BlockSpec((None, tr, C), lambda d, i, c_ref: (d, i, 0))),
        out_shape=jax.ShapeDtypeStruct((n, rh, C), BF16), compiler_params=_cparams(2),
    )(c_idx, grad, recv)


def chip_exchange(parts):
    K = len(parts)

    def body(*refs):
        ins, outs = refs[:K], refs[K:2 * K]
        send_sems, recv_sems = refs[2 * K:]
        x, y, c = _place()
        others = _other_chips(x, y)
        started = []
        for k in range(K):
            for r, (pchip, px, py) in enumerate(others):
                cp = pltpu.make_async_remote_copy(
                    src_ref=ins[k].at[pchip], dst_ref=outs[k].at[r], send_sem=send_sems.at[k, r],
                    recv_sem=recv_sems.at[k, r], device_id=(px, py, c), device_id_type=MESH)
                cp.start()
                started.append(cp)
        for cp in started:
            cp.wait_recv()
        for cp in started:
            cp.wait_send()

    return pl.pallas_call(
        body, name="chip_exchange",
        out_shape=[jax.ShapeDtypeStruct((3,) + p.shape[1:], p.dtype) for p in parts],
        in_specs=[HBM_SPEC] * K, out_specs=[HBM_SPEC] * K,
        scratch_shapes=[pltpu.SemaphoreType.DMA((K, 3))] * 2,
    )(*parts)


def chip_sum(parts, got, where):
    _, rh, C = parts.shape
    tr = _tile(rh, max(16, (1 << 19) // C), 16)
    nt = rh // tr

    def body(w_ref, p_ref, g_ref, o_ref):
        s = p_ref[...].astype(F32)
        for r in range(3):
            s = s + g_ref[r].astype(F32)
        o_ref[...] = s

    return pl.pallas_call(
        body, name="chip_sum",
        grid_spec=pltpu.PrefetchScalarGridSpec(
            num_scalar_prefetch=1, grid=(nt,),
            in_specs=[pl.BlockSpec((None, tr, C), lambda i, w_ref: (w_ref[0], i, 0)),
                      pl.BlockSpec((3, tr, C), lambda i, w_ref: (0, i, 0))],
            out_specs=pl.BlockSpec((tr, C), lambda i, w_ref: (w_ref[1] * nt + i, 0))),
        out_shape=jax.ShapeDtypeStruct((2 * rh, C), F32), compiler_params=_cparams(1),
    )(where, parts, got)


def pair_share(sums):
    K = len(sums)

    def body(*refs):
        ins, outs = refs[:K], refs[K:2 * K]
        send_sems, recv_sems = refs[2 * K:]
        x, y, c = _place()
        sibling = (x, y, 1 - c)
        started = []
        for k in range(K):
            rh = ins[k].shape[0] // 2
            cp = pltpu.make_async_remote_copy(
                src_ref=_half(ins[k], c, rh), dst_ref=_half(outs[k], c, rh), send_sem=send_sems.at[k],
                recv_sem=recv_sems.at[k], device_id=sibling, device_id_type=MESH)
            cp.start()
            started.append(cp)
        for k in range(K):
            rh = ins[k].shape[0] // 2
            theirs = _half(outs[k], 1 - c, rh)
            pltpu.make_async_remote_copy(
                src_ref=theirs, dst_ref=theirs, send_sem=send_sems.at[k], recv_sem=recv_sems.at[k],
                device_id=sibling, device_id_type=MESH).wait_recv()
        for cp in started:
            cp.wait_send()

    return pl.pallas_call(
        body, name="pair_share",
        out_shape=[jax.ShapeDtypeStruct(s.shape, s.dtype) for s in sums],
        in_specs=[HBM_SPEC] * K, out_specs=[HBM_SPEC] * K, input_output_aliases={k: k for k in range(K)},
        scratch_shapes=[pltpu.SemaphoreType.DMA((K,))] * 2,
    )(*sums)


SEM_SPEC = pl.BlockSpec(memory_space=pltpu.SEMAPHORE)
ANY_SPEC = pl.BlockSpec(memory_space=pl.ANY)
DATAFLOW = pltpu.SideEffectType.DATAFLOW_SIDE_EFFECTING


def _in_hbm(a):
    return pltpu.with_memory_space_constraint(a, pltpu.HBM)


def _ici_copies(srcs, dsts, send_sems, recv_sems, src_slice, dst_slice):
    x, y, c = _place()
    out = []
    for k in range(len(srcs)):
        for r, (pchip, px, py) in enumerate(_other_chips(x, y)):
            out.append(pltpu.make_async_remote_copy(
                src_ref=src_slice(srcs[k], r, pchip), dst_ref=dst_slice(dsts[k], r, pchip),
                send_sem=send_sems.at[3 * k + r], recv_sem=recv_sems.at[3 * k + r], device_id=(px, py, c),
                device_id_type=MESH))
    return out


def _exchange_start(bufs, lands, src_slice, dst_slice, name):
    K = len(bufs)
    same = lands is None
    n_thru = K if same else 2 * K

    def body(*refs):
        ins = refs[:n_thru]
        send_sems, recv_sems = refs[n_thru], refs[n_thru + 1]
        token = refs[-1]
        srcs = ins[:K]
        dsts = srcs if same else ins[K:]
        for cp in _ici_copies(srcs, dsts, send_sems, recv_sems, src_slice, dst_slice):
            cp.start()
        token[...] = jnp.zeros_like(token)

    thru = list(bufs) + ([] if same else list(lands))
    res = pl.pallas_call(
        body, name=name,
        out_shape=[pltpu.SemaphoreType.DMA((3 * K,)), pltpu.SemaphoreType.DMA((3 * K,))]
        + [pltpu.HBM(a.shape, a.dtype) for a in thru] + [jax.ShapeDtypeStruct((8, LANES), F32)],
        in_specs=[HBM_SPEC] * n_thru,
        out_specs=[SEM_SPEC, SEM_SPEC] + [HBM_SPEC] * n_thru + [pl.BlockSpec(memory_space=pltpu.VMEM)],
        input_output_aliases={i: 2 + i for i in range(n_thru)},
        compiler_params=pltpu.CompilerParams(has_side_effects=DATAFLOW),
    )(*[_in_hbm(a) for a in thru])
    return res[0], res[1], res[2:2 + K], (res[2:2 + K] if same else res[2 + K:2 + 2 * K]), res[-1]


def _exchange_wait(send_sems, recv_sems, bufs, lands, after, src_slice, dst_slice, name):
    K = len(bufs)
    same = lands is None
    n_thru = K if same else 2 * K

    def body(*refs):
        ins = refs[:n_thru]
        ssem, rsem = refs[n_thru], refs[n_thru + 1]
        srcs = ins[:K]
        dsts = srcs if same else ins[K:]
        copies = _ici_copies(srcs, dsts, ssem, rsem, src_slice, dst_slice)
        for cp in copies:
            cp.wait_send()
        for cp in copies:
            cp.wait_recv()

    thru = list(bufs) + ([] if same else list(lands))
    res = pl.pallas_call(
        body, name=name,
        out_shape=[pltpu.HBM(a.shape, a.dtype) for a in thru],
        in_specs=[HBM_SPEC] * n_thru + [SEM_SPEC, SEM_SPEC, ANY_SPEC],
        out_specs=[HBM_SPEC] * n_thru,
        input_output_aliases={i: i for i in range(n_thru)},
        compiler_params=pltpu.CompilerParams(has_side_effects=DATAFLOW),
    )(*thru, send_sems, recv_sems, after)
    return res[:K], (res[:K] if same else res[K:])


def _own_half(ref, r, pchip):
    x, y, c = _place()
    return _half(ref.at[2 * x + y], c, ref.shape[1] // 2)


def _their_half(ref, r, pchip):
    _, _, c = _place()
    return _half(ref.at[pchip], c, ref.shape[1] // 2)


def gather_start(lands, name):
    return _exchange_start(lands, None, _own_half, _own_half, name)


def gather_wait(handle, after, name):
    ssem, rsem, lands, _, _ = handle
    return _exchange_wait(ssem, rsem, lands, None, after, _own_half, _their_half, name)[1]


def pair_forward(lands):
    K = len(lands)

    def body(*refs):
        ins, outs = refs[:K], refs[K:2 * K]
        send_sems, recv_sems = refs[2 * K:]
        x, y, c = _place()
        sibling = (x, y, 1 - c)
        started = []
        for k in range(K):
            rh = ins[k].shape[1] // 2
            for r, (pchip, _, _) in enumerate(_other_chips(x, y)):
                cp = pltpu.make_async_remote_copy(
                    src_ref=_half(ins[k].at[pchip], c, rh), dst_ref=_half(outs[k].at[pchip], c, rh),
                    send_sem=send_sems.at[k, r], recv_sem=recv_sems.at[k, r], device_id=sibling, device_id_type=MESH)
                cp.start()
                started.append(cp)
        for k in range(K):
            rh = ins[k].shape[1] // 2
            for r, (pchip, _, _) in enumerate(_other_chips(x, y)):
                theirs = _half(outs[k].at[pchip], 1 - c, rh)
                pltpu.make_async_remote_copy(
                    src_ref=theirs, dst_ref=theirs, send_sem=send_sems.at[k, r], recv_sem=recv_sems.at[k, r],
                    device_id=sibling, device_id_type=MESH).wait_recv()
        for cp in started:
            cp.wait_send()

    return pl.pallas_call(
        body, name="pair_forward",
        out_shape=[jax.ShapeDtypeStruct(s.shape, s.dtype) for s in lands],
        in_specs=[HBM_SPEC] * K, out_specs=[HBM_SPEC] * K, input_output_aliases={k: k for k in range(K)},
        scratch_shapes=[pltpu.SemaphoreType.DMA((K, 3))] * 2,
    )(*lands)


def _to_chip(ref, r, pchip):
    return ref.at[pchip]


def _from_relation(ref, r, pchip):
    return ref.at[r]


def reduce_start(grads, c_idx, name):
    recv = pair_exchange(grads)
    parts = [pair_add(g, r, c_idx) for g, r in zip(grads, recv)]
    lands = [lax.empty((3,) + p.shape[1:], p.dtype) for p in parts]
    return _exchange_start(parts, lands, _to_chip, _from_relation, name)


def reduce_finish(handle, after, where, name):
    ssem, rsem, parts, lands, _ = handle
    parts, got = _exchange_wait(ssem, rsem, parts, lands, after, _to_chip, _from_relation, name)
    return pair_share([chip_sum(p, g, where) for p, g in zip(parts, got)])


def _pack(arrs):
    flat = jnp.concatenate([a.reshape(-1).astype(F32) for a in arrs])
    pad = (-flat.shape[0]) % (8 * LANES)
    return jnp.pad(flat, (0, pad)).reshape(-1, LANES)


def _unpack(flat, shapes):
    out, off = [], 0
    for s in shapes:
        n = 1
        for d in s:
            n *= d
        out.append(flat[off:off + n].reshape(s))
        off += n
    return out


def _adamw_any(w, g, m, v, name):
    shp = w.shape
    C = shp[-1]
    d, nm, nv = adamw(w.reshape(-1, C), g.reshape(-1, C), m.reshape(-1, C), v.reshape(-1, C), name)
    return d.reshape(shp), nm.reshape(shp), nv.reshape(shp)


def kernel(x, c, norm_g, w_ada, b_ada, w_ffn_in, w_ffn_out, cm_w_glu, cm_b_glu, cm_w_dw, cm_b_dw, cm_ln_g, cm_ln_b, cm_w_pw, cm_b_pw, dn_w_in, dn_w_sconv, dn_a_log, dn_dt_bias, dn_o_g, dn_w_out, final_g, loss_target, m_norm_g, m_w_ada, m_b_ada, m_w_ffn_in, m_w_ffn_out, m_cm_w_glu, m_cm_b_glu, m_cm_w_dw, m_cm_b_dw, m_cm_ln_g, m_cm_ln_b, m_cm_w_pw, m_cm_b_pw, m_dn_w_in, m_dn_w_sconv, m_dn_a_log, m_dn_dt_bias, m_dn_o_g, m_dn_w_out, m_final_g, v_norm_g, v_w_ada, v_b_ada, v_w_ffn_in, v_w_ffn_out, v_cm_w_glu, v_cm_b_glu, v_cm_w_dw, v_cm_b_dw, v_cm_ln_g, v_cm_ln_b, v_cm_w_pw, v_cm_b_pw, v_dn_w_in, v_dn_w_sconv, v_dn_a_log, v_dn_dt_bias, v_dn_o_g, v_dn_w_out, v_final_g):
    weights = dict(norm_g=norm_g, w_ada=w_ada, b_ada=b_ada, w_ffn_in=w_ffn_in, w_ffn_out=w_ffn_out, cm_w_glu=cm_w_glu,
                   cm_b_glu=cm_b_glu, cm_w_dw=cm_w_dw, cm_b_dw=cm_b_dw, cm_ln_g=cm_ln_g, cm_ln_b=cm_ln_b, cm_w_pw=cm_w_pw,
                   cm_b_pw=cm_b_pw, dn_w_in=dn_w_in, dn_w_sconv=dn_w_sconv, dn_a_log=dn_a_log, dn_dt_bias=dn_dt_bias,
                   dn_o_g=dn_o_g, dn_w_out=dn_w_out, final_g=final_g)
    mom_m = dict(norm_g=m_norm_g, w_ada=m_w_ada, b_ada=m_b_ada, w_ffn_in=m_w_ffn_in, w_ffn_out=m_w_ffn_out,
                 cm_w_glu=m_cm_w_glu, cm_b_glu=m_cm_b_glu, cm_w_dw=m_cm_w_dw, cm_b_dw=m_cm_b_dw, cm_ln_g=m_cm_ln_g,
                 cm_ln_b=m_cm_ln_b, cm_w_pw=m_cm_w_pw, cm_b_pw=m_cm_b_pw, dn_w_in=m_dn_w_in, dn_w_sconv=m_dn_w_sconv,
                 dn_a_log=m_dn_a_log, dn_dt_bias=m_dn_dt_bias, dn_o_g=m_dn_o_g, dn_w_out=m_dn_w_out, final_g=m_final_g)
    mom_v = dict(norm_g=v_norm_g, w_ada=v_w_ada, b_ada=v_b_ada, w_ffn_in=v_w_ffn_in, w_ffn_out=v_w_ffn_out,
                 cm_w_glu=v_cm_w_glu, cm_b_glu=v_cm_b_glu, cm_w_dw=v_cm_w_dw, cm_b_dw=v_cm_b_dw, cm_ln_g=v_cm_ln_g,
                 cm_ln_b=v_cm_ln_b, cm_w_pw=v_cm_w_pw, cm_b_pw=v_cm_b_pw, dn_w_in=v_dn_w_in, dn_w_sconv=v_dn_w_sconv,
                 dn_a_log=v_dn_a_log, dn_dt_bias=v_dn_dt_bias, dn_o_g=v_dn_o_g, dn_w_out=v_dn_w_out, final_g=v_final_g)
    names = list(weights)

    BL, T, D = x.shape
    L = norm_g.shape[0]
    NB = BL * N_DEV
    Ca = w_ada.shape[2]
    C9 = b_ada.shape[1]
    H = dn_a_log.shape[1]
    Dh = dn_o_g.shape[1]
    W = H * Dh
    KC = cm_w_dw.shape[1]
    KS = dn_w_sconv.shape[1]
    n_cm, n_dn = cm_w_glu.shape[0], dn_w_in.shape[0]
    ax, ay, ac = lax.axis_index("x"), lax.axis_index("y"), lax.axis_index("c")
    chip = 2 * ax + ay
    dev = 2 * chip + ac
    c_idx = ac.astype(jnp.int32).reshape(1)
    where = jnp.stack([chip, ac]).astype(jnp.int32)

    small_in = [c, norm_g, cm_w_dw, dn_w_sconv]
    packed = _pack(small_in)
    gathered = allgather8(packed).reshape(N_DEV, -1)
    per_dev = [_unpack(gathered[d], [a.shape for a in small_in]) for d in range(N_DEV)]
    c_all = jnp.concatenate([p[0] for p in per_dev], axis=0)
    norm_g_full = jnp.concatenate([per_dev[2 * s][1] for s in range(N_CHIPS)], axis=-1)
    w_dw_full = jnp.concatenate([per_dev[2 * s][2] for s in range(N_CHIPS)], axis=-1)
    w_sconv_full = jnp.concatenate([per_dev[2 * s][3] for s in range(N_CHIPS)], axis=-1)

    b_cols = lax.dynamic_slice_in_dim(b_ada, chip * Ca, Ca, axis=1).reshape(L, 1, Ca)
    mod_part = ada_fwd(c_all, w_ada, b_cols)
    mod_g = allgather8(mod_part.reshape(-1, LANES)).reshape(N_DEV, L, NB, Ca)
    mod_all = jnp.concatenate([mod_g[2 * s] for s in range(N_CHIPS)], axis=-1)
    mod = lax.dynamic_slice_in_dim(mod_all, dev * BL, BL, axis=1).reshape(L, BL, 9, D)

    def layer_shards(i):
        sh = [w_ffn_in[i, 0], w_ffn_in[i, 1], w_ffn_out[i, 0], w_ffn_out[i, 1]]
        if i % 2 == 0:
            sh += [cm_w_glu[i // 2], cm_w_pw[i // 2]]
        else:
            sh += [dn_w_in[i // 2], dn_w_out[i // 2]]
        return [lax.dynamic_update_slice(lax.empty((N_CHIPS,) + s.shape, BF16), s.astype(BF16)[None], (chip, 0, 0))
                for s in sh]

    lands = [layer_shards(i) for i in range(L)]
    wts = [None] * L
    handle = gather_start(lands[0], "gather_start_0")

    def dn_weights(i):
        full = jnp.transpose(wts[i][4], (1, 0, 2)).reshape(D, -1)
        return full[:, :4 * W], jnp.pad(full[:, 4 * W:], ((0, 0), (0, LANES - 2 * H)))

    def row128(v):
        return jnp.pad(v.reshape(1, -1), ((0, 0), (0, LANES - v.shape[-1])))

    def pad_taps(w):
        return jnp.pad(w, ((0, 1), (0, 0)))

    saved = []
    xs = x
    after = mod
    for i in range(L):
        wl = wts[i] = pair_forward(gather_wait(handle, after, "gather_wait_%d" % i))
        tok = 0.0
        if i + 1 < L:
            handle = gather_start(lands[i + 1], "gather_start_%d" % (i + 1))
            tok = handle[4][0, 0]
        sv = {}
        m3 = [mod[i, :, 3 * j:3 * j + 3] + tok for j in range(3)]
        gs = [norm_g_full[i, j].reshape(1, D) for j in range(3)]
        sv["x0"] = xs
        xs, sv["y0"] = ffn_fwd(xs, m3[0], gs[0], wl[0], wl[2])
        sv["x1"] = xs
        if i % 2 == 0:
            a = i // 2
            sv["u"] = conv_glu_fwd(xs, m3[1], gs[1], wl[4], cm_b_glu[a].reshape(1, -1))
            xs, sv["y1"], sv["u2"] = conv_out_fwd(
                xs, sv["u"], m3[1], pad_taps(w_dw_full[a]), cm_b_dw[a].reshape(1, D), cm_ln_g[a].reshape(1, D),
                cm_ln_b[a].reshape(1, D), wl[5].reshape(D, D), cm_b_pw[a].reshape(1, D))
        else:
            a = i // 2
            w_main, w_ab = dn_weights(i)
            sv["pre"], sv["z"], sv["ab"] = dn_proj_fwd(xs, m3[1], gs[1], w_main, w_ab)
            qkvgb = dn_conv_fwd(sv["pre"], sv["ab"], w_sconv_full[a], row128(dn_a_log[a]), row128(dn_dt_bias[a]), H)
            sv["qkvgb"] = qkvgb
            sv["o"], sv["sp"] = dn_chunk_fwd(*qkvgb)
            xs, sv["y1"] = dn_out_fwd(xs, sv["o"], sv["z"], m3[1], dn_o_g[a].reshape(1, Dh), wl[5].reshape(W, D))
        sv["x2"] = xs
        xs, sv["y2"] = ffn_fwd(xs, m3[2], gs[2], wl[1], wl[3])
        saved.append(sv)
        after = xs

    dx, d_final_g, loss_part = final_loss(xs, final_g.reshape(1, D), loss_target)

    g_small = {n: None for n in names}
    d_norm_g = [[None] * 3 for _ in range(L)]
    dmod = [[None] * 3 for _ in range(L)]
    g_cm = {k: [None] * n_cm for k in ("b_glu", "w_dw", "b_dw", "ln_g", "ln_b", "b_pw")}
    g_dn = {k: [None] * n_dn for k in ("w_sconv", "a_log", "dt_bias", "o_g")}
    big = [None] * L

    def ffn_back(i, j, slot, dx, tok=0.0):
        wl, sv = wts[i], saved[i]
        m3 = mod[i, :, 3 * j:3 * j + 3] + tok
        g = norm_g_full[i, j].reshape(1, D)
        dx, hb, ab_, dgu, dyb, dm, dg = ffn_bwd(sv["x%d" % j], dx, sv["y%d" % j], m3, g, wl[slot], wl[2 + slot])
        dmod[i][j] = dm
        d_norm_g[i][j] = jnp.sum(dg, axis=(0, 1))
        Fc = wl[slot].shape[2]
        dw_in = matmul_tn(hb.reshape(-1, D), dgu.reshape(2, BL * T, 2 * Fc), Fc, "dw_ffn_in")
        dw_out = matmul_tn(ab_.reshape(-1, 2 * Fc), dyb.reshape(1, -1, D), D, "dw_ffn_out")
        return dx, dw_in, dw_out.reshape(N_CHIPS, -1, D)

    pending, tok = None, 0.0
    for i in reversed(range(L)):
        wl, sv = wts[i], saved[i]
        a = i // 2
        dx, dw_in1, dw_out1 = ffn_back(i, 2, 1, dx, tok)
        m3 = mod[i, :, 3:6]
        g = norm_g_full[i, 1].reshape(1, D)
        if i % 2 == 0:
            w_pw = wl[5].reshape(D, D)
            wdw = pad_taps(w_dw_full[a])
            du2, u3b, dyb, dgate, vec = conv_out_bwd(dx, sv["y1"], sv["u2"], m3, cm_ln_g[a].reshape(1, D),
                                                     cm_ln_b[a].reshape(1, D), w_pw)
            dx, hb, dab, dwdw, dbglu, dm, dg = conv_glu_bwd(sv["x1"], dx, du2, sv["u"], m3, g, wl[4],
                                                            cm_b_glu[a].reshape(1, -1), wdw)
            dm = dm.at[:, 2:3, :].set(dgate)
            vec = jnp.sum(vec, axis=0)
            g_cm["b_pw"][a], g_cm["ln_g"][a], g_cm["ln_b"][a], g_cm["b_dw"][a] = vec[0], vec[1], vec[2], vec[3]
            g_cm["w_dw"][a] = jnp.sum(dwdw, axis=0)[:KC]
            g_cm["b_glu"][a] = jnp.sum(dbglu, axis=(0, 1))
            dw_a = matmul_tn(hb.reshape(-1, D), dab.reshape(1, -1, 2 * D), D // 2, "dw_glu")
            dw_b = matmul_tn(u3b.reshape(-1, D), dyb.reshape(1, -1, D), D, "dw_sq").reshape(N_CHIPS, -1, D)
        else:
            w_main, w_ab = dn_weights(i)
            w_out = wl[5].reshape(W, D)
            do, dz, ogb, dyb, dgate, dog = dn_out_bwd(dx, sv["y1"], sv["o"], sv["z"], m3, dn_o_g[a].reshape(1, Dh), w_out)
            dq, dk, dv, dgb, dbb = dn_chunk_bwd(*sv["qkvgb"], sv["sp"], do)
            dc, dab, small = dn_conv_bwd(dq, dk, dv, dgb, dbb, sv["pre"], sv["ab"], w_sconv_full[a],
                                         row128(dn_a_log[a]), row128(dn_dt_bias[a]))
            dx, hb, dproj, dws, dm, dg = dn_proj_bwd(sv["x1"], dx, dc, sv["pre"], dz, dab, m3, g, w_main, w_ab,
                                                     w_sconv_full[a])
            dm = dm.at[:, 2:3, :].set(dgate)
            small = jnp.sum(small, axis=0)
            g_dn["a_log"][a], g_dn["dt_bias"][a] = small[0, :H], small[1, :H]
            g_dn["o_g"][a] = jnp.sum(dog, axis=(0, 1))
            g_dn["w_sconv"][a] = jnp.sum(dws, axis=0)
            dw_main = matmul_tn(hb.reshape(-1, D), dproj.reshape(1, -1, 4 * W), W, "dw_dn_main")
            dw_ab = matmul_tn(hb.reshape(-1, D), dab.reshape(1, -1, LANES), LANES, "dw_dn_ab")
            full = jnp.concatenate([jnp.transpose(dw_main, (1, 0, 2)).reshape(D, 4 * W), dw_ab[0][:, :2 * H]], axis=1)
            dw_a = jnp.transpose(full.reshape(D, N_CHIPS, -1), (1, 0, 2))
            dw_b = matmul_tn(ogb.reshape(-1, W), dyb.reshape(1, -1, D), D, "dw_sq").reshape(N_CHIPS, -1, D)
        dmod[i][1] = dm
        d_norm_g[i][1] = jnp.sum(dg, axis=(0, 1))
        dx, dw_in0, dw_out0 = ffn_back(i, 0, 0, dx)
        started = reduce_start([dw_in0, dw_in1, dw_out0, dw_out1, dw_a, dw_b], c_idx, "reduce_start_%d" % i)
        if pending is not None:
            big[pending[1]] = reduce_finish(pending[0], dx, where, "reduce_wait_%d" % pending[1])
        pending, tok = (started, i), started[4][0, 0]

    part = dict(
        norm_g=jnp.stack([jnp.stack(r) for r in d_norm_g]),
        cm_b_glu=jnp.stack(g_cm["b_glu"]), cm_w_dw=jnp.stack(g_cm["w_dw"]), cm_b_dw=jnp.stack(g_cm["b_dw"]),
        cm_ln_g=jnp.stack(g_cm["ln_g"]), cm_ln_b=jnp.stack(g_cm["ln_b"]), cm_b_pw=jnp.stack(g_cm["b_pw"]),
        dn_w_sconv=jnp.stack(g_dn["w_sconv"]), dn_a_log=jnp.stack(g_dn["a_log"]), dn_dt_bias=jnp.stack(g_dn["dt_bias"]),
        dn_o_g=jnp.stack(g_dn["o_g"]), final_g=jnp.sum(d_final_g, axis=(0, 1)),
        loss=jnp.sum(loss_part[:, 0, 0]).reshape(1))
    dmod_loc = jnp.stack([jnp.concatenate(r, axis=1) for r in dmod]).reshape(L, BL, C9)
    keys = list(part)
    packed = _pack([part[k] for k in keys] + [dmod_loc])
    R = packed.shape[0]
    gathered = allgather8(packed).reshape(N_DEV, R, LANES)
    summed = _unpack(sum_devices(gathered).reshape(-1), [part[k].shape for k in keys])
    tot = dict(zip(keys, summed))
    n_small = sum(int(part[k].size) for k in keys)
    dmod_all = gathered.reshape(N_DEV, -1)[:, n_small:n_small + L * BL * C9].reshape(N_DEV, L, BL, C9)
    dmod_all = jnp.transpose(dmod_all, (1, 0, 2, 3)).reshape(L, NB, C9)
    dmod_cols = lax.dynamic_slice_in_dim(dmod_all, chip * Ca, Ca, axis=2)
    g_w_ada, g_b_ada = ada_bwd(c_all, dmod_cols, dmod_all)
    delta, new_m, new_v = {}, {}, {}
    delta["w_ada"], new_m["w_ada"], new_v["w_ada"] = _adamw_any(w_ada, g_w_ada, m_w_ada, v_w_ada, "adamw_w_ada")
    big[pending[1]] = reduce_finish(pending[0], new_v["w_ada"], where, "reduce_wait_%d" % pending[1])

    def my_cols(full):
        n = full.shape[-1] // N_CHIPS
        return lax.dynamic_slice_in_dim(full, chip * n, n, axis=full.ndim - 1)

    grads = dict(
        norm_g=my_cols(tot["norm_g"]), w_ada=g_w_ada, b_ada=g_b_ada.reshape(L, C9),
        w_ffn_in=jnp.stack([jnp.stack([big[i][0], big[i][1]]) for i in range(L)]),
        w_ffn_out=jnp.stack([jnp.stack([big[i][2], big[i][3]]) for i in range(L)]),
        cm_w_glu=jnp.stack([big[i][4] for i in range(0, L, 2)]), cm_b_glu=tot["cm_b_glu"], cm_w_dw=my_cols(tot["cm_w_dw"]),
        cm_b_dw=tot["cm_b_dw"], cm_ln_g=tot["cm_ln_g"], cm_ln_b=tot["cm_ln_b"],
        cm_w_pw=jnp.stack([big[i][5] for i in range(0, L, 2)]), cm_b_pw=tot["cm_b_pw"],
        dn_w_in=jnp.stack([big[i][4] for i in range(1, L, 2)]), dn_w_sconv=my_cols(tot["dn_w_sconv"]),
        dn_a_log=tot["dn_a_log"], dn_dt_bias=tot["dn_dt_bias"], dn_o_g=tot["dn_o_g"],
        dn_w_out=jnp.stack([big[i][5] for i in range(1, L, 2)]), final_g=tot["final_g"])

    large = ("w_ada", "w_ffn_in", "w_ffn_out", "cm_w_glu", "cm_w_pw", "dn_w_in", "dn_w_out")
    for n in large[1:]:
        delta[n], new_m[n], new_v[n] = _adamw_any(weights[n], grads[n], mom_m[n], mom_v[n], "adamw_" + n)
    rest = [n for n in names if n not in large]
    shapes = [weights[n].shape for n in rest]
    pd, pm, pv = adamw(_pack([weights[n] for n in rest]), _pack([grads[n] for n in rest]),
                       _pack([mom_m[n] for n in rest]), _pack([mom_v[n] for n in rest]), "adamw_small")
    for n, d_, m_, v_ in zip(rest, _unpack(pd.reshape(-1), shapes), _unpack(pm.reshape(-1), shapes),
                             _unpack(pv.reshape(-1), shapes)):
        delta[n], new_m[n], new_v[n] = d_, m_, v_

    return (tot["loss"].reshape(()), dx, *[grads[n] for n in names], *[delta[n] for n in names],
            *[new_m[n] for n in names], *[new_v[n] for n in names])
```

```python
import functools

import jax
import jax.numpy as jnp
from jax import lax
from jax.experimental import pallas as pl
from jax.experimental.pallas import tpu as pltpu

F32 = jnp.float32
BF16 = jnp.bfloat16
EPS = 1e-6
CHUNK = 64
N_CHIPS = 4
N_DEV = 8
LANES = 128
CONV_HALO = 32
SCONV_HALO = 8
VMEM_LIMIT_V7X = 60 * 1024 * 1024
HI = lax.Precision.HIGHEST
MESH = pl.DeviceIdType.MESH
HBM_SPEC = pl.BlockSpec(memory_space=pltpu.HBM)

ADAM_LR, ADAM_B1, ADAM_B2, ADAM_EPS, ADAM_WD, ADAM_STEP = 0.001, 0.9, 0.999, 1e-08, 0.01, 10


def _cparams(n_axes):
    return pltpu.CompilerParams(dimension_semantics=("arbitrary",) * n_axes, vmem_limit_bytes=VMEM_LIMIT_V7X)


def _tile(n, pref, mult=8):
    for t in range(min(n, pref) // mult * mult, 0, -mult):
        if n % t == 0:
            return t
    return n


def _mm(a, b):
    return lax.dot_general(a.astype(BF16), b.astype(BF16), (((1,), (0,)), ((), ())), preferred_element_type=F32)


def _mm_nt(a, b):
    return lax.dot_general(a.astype(BF16), b.astype(BF16), (((1,), (1,)), ((), ())), preferred_element_type=F32)


def _mm_tn(a, b):
    return lax.dot_general(a.astype(BF16), b.astype(BF16), (((0,), (0,)), ((), ())), preferred_element_type=F32)


def _sigmoid(x):
    return jax.nn.sigmoid(x)


def _dsilu(x, s):
    return s * (1.0 + x * (1.0 - s))


def _softplus(x):
    return jnp.maximum(x, 0.0) + jnp.log(1.0 + jnp.exp(-jnp.abs(x)))


def _modnorm(x, g, scale, shift):
    r = lax.rsqrt(jnp.mean(x * x, axis=-1, keepdims=True) + EPS)
    return (x * r) * g * (1.0 + scale) + shift


def _modnorm_bwd(x, g, scale, dh):
    r = lax.rsqrt(jnp.mean(x * x, axis=-1, keepdims=True) + EPS)
    xn = x * r
    dshift = jnp.sum(dh, axis=0, keepdims=True)
    dscale = jnp.sum(dh * (xn * g), axis=0, keepdims=True)
    dhn = dh * (1.0 + scale)
    dg = jnp.sum(dhn * xn, axis=0, keepdims=True)
    dxn = dhn * g
    dx = r * (dxn - xn * jnp.mean(dxn * xn, axis=-1, keepdims=True))
    return dx, dg, dscale, dshift


def _sum0(a):
    return jnp.sum(a, axis=0, keepdims=True)


def ffn_fwd(x, mod3, g, w_in, w_out):
    B, T, D = x.shape
    Fc = w_in.shape[2]
    w_in = w_in.reshape(2, 2, D, Fc)
    w_out = w_out.reshape(2, Fc, D)
    tm = _tile(T, 512)

    def body(x_ref, mod_ref, g_ref, wi_ref, wo_ref, xo_ref, y_ref, h_s, acc_s):
        f = pl.program_id(2)

        @pl.when(f == 0)
        def _():
            h = _modnorm(x_ref[...], g_ref[...], mod_ref[1:2, :], mod_ref[0:1, :])
            h_s[...] = h.astype(BF16)
            acc_s[...] = jnp.zeros_like(acc_s)

        h = h_s[...]
        gt = _mm(h, wi_ref[0])
        up = _mm(h, wi_ref[1])
        a = gt * _sigmoid(gt) * up
        acc_s[...] += _mm(a, wo_ref[...])

        @pl.when(f == 1)
        def _():
            y = acc_s[...]
            y_ref[...] = y
            xo_ref[...] = x_ref[...] + 0.5 * (1.0 + mod_ref[2:3, :]) * y

    tok = pl.BlockSpec((None, tm, D), lambda b, t, f: (b, t, 0))
    return pl.pallas_call(
        body, name="ffn_fwd", grid=(B, T // tm, 2),
        in_specs=[tok,
                  pl.BlockSpec((None, 3, D), lambda b, t, f: (b, 0, 0)),
                  pl.BlockSpec((1, D), lambda b, t, f: (0, 0)),
                  pl.BlockSpec((2, None, D, Fc), lambda b, t, f: (0, f, 0, 0)),
                  pl.BlockSpec((None, Fc, D), lambda b, t, f: (f, 0, 0))],
        out_specs=[tok, tok],
        out_shape=[jax.ShapeDtypeStruct((B, T, D), F32)] * 2,
        scratch_shapes=[pltpu.VMEM((tm, D), BF16), pltpu.VMEM((tm, D), F32)],
        compiler_params=_cparams(3),
    )(x, mod3, g, w_in, w_out)


def ffn_bwd(x, dres, y, mod3, g, w_in, w_out):
    B, T, D = x.shape
    Fc = w_in.shape[2]
    F = 2 * Fc
    w_in = w_in.reshape(2, 2, D, Fc)
    w_out = w_out.reshape(2, Fc, D)
    tm = _tile(T, 256)

    def body(x_ref, dres_ref, y_ref, mod_ref, g_ref, wi_ref, wo_ref,
             dx_ref, h_ref, a_ref, dgu_ref, dy_ref, dmod_ref, dg_ref, h_s, dy_s, dh_s):
        t, f = pl.program_id(1), pl.program_id(2)

        @pl.when(f == 0)
        def _():
            h = _modnorm(x_ref[...], g_ref[...], mod_ref[1:2, :], mod_ref[0:1, :]).astype(BF16)
            h_s[...] = h
            h_ref[...] = h
            dres = dres_ref[...]
            dy = (0.5 * (1.0 + mod_ref[2:3, :]) * dres).astype(BF16)
            dy_s[...] = dy
            dy_ref[...] = dy
            dh_s[...] = jnp.zeros_like(dh_s)
            dgate = _sum0(dres * (0.5 * y_ref[...]))

            @pl.when(t == 0)
            def _():
                dmod_ref[...] = jnp.zeros_like(dmod_ref)
                dg_ref[...] = jnp.zeros_like(dg_ref)

            dmod_ref[2:3, :] += dgate

        h = h_s[...]
        dy = dy_s[...]
        gt = _mm(h, wi_ref[0])
        up = _mm(h, wi_ref[1])
        sg = _sigmoid(gt)
        silu = gt * sg
        a_ref[...] = (silu * up).astype(BF16)
        da = _mm_nt(dy, wo_ref[...])
        dup = (da * silu).astype(BF16)
        dgt = (da * up * _dsilu(gt, sg)).astype(BF16)
        dgu_ref[0] = dgt
        dgu_ref[1] = dup
        dh_s[...] += _mm_nt(dgt, wi_ref[0]) + _mm_nt(dup, wi_ref[1])

        @pl.when(f == 1)
        def _():
            dxn, dg, dscale, dshift = _modnorm_bwd(x_ref[...], g_ref[...], mod_ref[1:2, :], dh_s[...])
            dx_ref[...] = dres_ref[...] + dxn
            dmod_ref[0:1, :] += dshift
            dmod_ref[1:2, :] += dscale
            dg_ref[...] += dg

    tok = pl.BlockSpec((None, tm, D), lambda b, t, f: (b, t, 0))
    per_b3 = pl.BlockSpec((None, 3, D), lambda b, t, f: (b, 0, 0))
    return pl.pallas_call(
        body, name="ffn_bwd", grid=(B, T // tm, 2),
        in_specs=[tok, tok, tok, per_b3,
                  pl.BlockSpec((1, D), lambda b, t, f: (0, 0)),
                  pl.BlockSpec((2, None, D, Fc), lambda b, t, f: (0, f, 0, 0)),
                  pl.BlockSpec((None, Fc, D), lambda b, t, f: (f, 0, 0))],
        out_specs=[tok, tok,
                   pl.BlockSpec((None, tm, Fc), lambda b, t, f: (b, t, f)),
                   pl.BlockSpec((2, None, tm, Fc), lambda b, t, f: (0, b, t, f)),
                   tok, per_b3,
                   pl.BlockSpec((None, 1, D), lambda b, t, f: (b, 0, 0))],
        out_shape=[jax.ShapeDtypeStruct((B, T, D), F32), jax.ShapeDtypeStruct((B, T, D), BF16),
                   jax.ShapeDtypeStruct((B, T, F), BF16), jax.ShapeDtypeStruct((2, B, T, F), BF16),
                   jax.ShapeDtypeStruct((B, T, D), BF16), jax.ShapeDtypeStruct((B, 3, D), F32),
                   jax.ShapeDtypeStruct((B, 1, D), F32)],
        scratch_shapes=[pltpu.VMEM((tm, D), BF16), pltpu.VMEM((tm, D), BF16), pltpu.VMEM((tm, D), F32)],
        compiler_params=_cparams(3),
    )(x, dres, y, mod3, g, w_in, w_out)


def matmul_tn(xm, ym, bm, name):
    N, K = xm.shape
    GY, _, MY = ym.shape
    per = MY // bm
    nb = GY * per
    tn = _tile(N, 512)

    def body(x_ref, y_ref, o_ref, acc_s):
        n = pl.program_id(1)

        @pl.when(n == 0)
        def _():
            acc_s[...] = jnp.zeros_like(acc_s)

        acc_s[...] += _mm_tn(x_ref[...], y_ref[...])

        @pl.when(n == N // tn - 1)
        def _():
            o_ref[...] = acc_s[...].astype(BF16)

    return pl.pallas_call(
        body, name=name, grid=(nb, N // tn),
        in_specs=[pl.BlockSpec((tn, K), lambda m, n: (n, 0)),
                  pl.BlockSpec((None, tn, bm), lambda m, n: (m // per, n, m % per))],
        out_specs=pl.BlockSpec((None, K, bm), lambda m, n: (m, 0, 0)),
        out_shape=jax.ShapeDtypeStruct((nb, K, bm), BF16),
        scratch_shapes=[pltpu.VMEM((K, bm), F32)],
        compiler_params=_cparams(2),
    )(xm, ym)


def final_loss(x, fg, target):
    B, T, D = x.shape
    tm = _tile(T, 512)

    def body(x_ref, g_ref, t_ref, dx_ref, dfg_ref, loss_ref):
        t = pl.program_id(1)

        @pl.when(t == 0)
        def _():
            dfg_ref[...] = jnp.zeros_like(dfg_ref)
            loss_ref[...] = jnp.zeros_like(loss_ref)

        xv = x_ref[...]
        g = g_ref[...]
        r = lax.rsqrt(jnp.mean(xv * xv, axis=-1, keepdims=True) + EPS)
        xn = xv * r
        err = xn * g - t_ref[...]
        tok_loss = jnp.mean(err * err, axis=-1, keepdims=True)
        loss_ref[...] += 0.5 * jnp.sum(tok_loss, axis=0, keepdims=True)
        dy = err * (1.0 / D)
        dfg_ref[...] += _sum0(dy * xn)
        dxn = dy * g
        dx_ref[...] = r * (dxn - xn * jnp.mean(dxn * xn, axis=-1, keepdims=True))

    tok = pl.BlockSpec((None, tm, D), lambda b, t: (b, t, 0))
    return pl.pallas_call(
        body, name="final_loss", grid=(B, T // tm),
        in_specs=[tok, pl.BlockSpec((1, D), lambda b, t: (0, 0)), tok],
        out_specs=[tok, pl.BlockSpec((None, 1, D), lambda b, t: (b, 0, 0)),
                   pl.BlockSpec((None, 1, LANES), lambda b, t: (b, 0, 0))],
        out_shape=[jax.ShapeDtypeStruct((B, T, D), F32), jax.ShapeDtypeStruct((B, 1, D), F32),
                   jax.ShapeDtypeStruct((B, 1, LANES), F32)],
        compiler_params=_cparams(2),
    )(x, fg, target)


def _past_halo_spec(tm, halo, width):
    return pl.BlockSpec((None, halo, width), lambda b, t: (b, jnp.maximum(t * (tm // halo) - 1, 0), 0))


def _future_halo_spec(tm, halo, width, T):
    return pl.BlockSpec((None, halo, width), lambda b, t: (b, jnp.minimum((t + 1) * (tm // halo), T // halo - 1), 0))


def _glu_fwd(h, w_ref, bias):
    D = h.shape[1]
    a = jnp.concatenate([_mm(h, w_ref[0]), _mm(h, w_ref[1])], axis=1) + bias[:, :D]
    b = jnp.concatenate([_mm(h, w_ref[2]), _mm(h, w_ref[3])], axis=1) + bias[:, D:]
    return a, b


def conv_glu_fwd(x, mod3, g, w_glu, b_glu):
    B, T, D = x.shape
    tm = _tile(T, 512)

    def body(x_ref, mod_ref, g_ref, w_ref, b_ref, u_ref):
        h = _modnorm(x_ref[...], g_ref[...], mod_ref[1:2, :], mod_ref[0:1, :]).astype(BF16)
        a, b = _glu_fwd(h, w_ref, b_ref[...])
        u_ref[...] = a * _sigmoid(b)

    tok = pl.BlockSpec((None, tm, D), lambda b, t: (b, t, 0))
    return pl.pallas_call(
        body, name="conv_glu_fwd", grid=(B, T // tm),
        in_specs=[tok, pl.BlockSpec((None, 3, D), lambda b, t: (b, 0, 0)),
                  pl.BlockSpec((1, D), lambda b, t: (0, 0)),
                  pl.BlockSpec((4, D, D // 2), lambda b, t: (0, 0, 0)),
                  pl.BlockSpec((1, 2 * D), lambda b, t: (0, 0))],
        out_specs=tok, out_shape=jax.ShapeDtypeStruct((B, T, D), F32),
        compiler_params=_cparams(2),
    )(x, mod3, g, w_glu, b_glu)


def _layer_norm_parts(u2):
    mu = jnp.mean(u2, axis=-1, keepdims=True)
    xc = u2 - mu
    rs = lax.rsqrt(jnp.mean(xc * xc, axis=-1, keepdims=True) + EPS)
    return xc * rs, rs


def conv_out_fwd(x, u, mod3, w_dw, b_dw, ln_g, ln_b, w_pw, b_pw):
    B, T, D = x.shape
    K = w_dw.shape[0] - 1
    tm = _tile(T, 512)

    def body(x_ref, u_ref, halo_ref, mod_ref, wdw_ref, bdw_ref, lg_ref, lb_ref, wpw_ref, bpw_ref,
             xo_ref, y_ref, u2_ref, ext_s):
        t = pl.program_id(1)
        ext_s[0:CONV_HALO, :] = jnp.where(t > 0, halo_ref[...], 0.0)
        ext_s[CONV_HALO:, :] = u_ref[...]
        acc = jnp.broadcast_to(bdw_ref[...], (tm, D))
        for k in range(K):
            acc = acc + wdw_ref[k:k + 1, :] * ext_s[pl.ds(CONV_HALO - (K - 1) + k, tm), :]
        u2_ref[...] = acc
        xh, _ = _layer_norm_parts(acc)
        l = xh * lg_ref[...] + lb_ref[...]
        u3 = l * _sigmoid(l)
        y = _mm(u3, wpw_ref[...]) + bpw_ref[...]
        y_ref[...] = y
        xo_ref[...] = x_ref[...] + (1.0 + mod_ref[2:3, :]) * y

    tok = pl.BlockSpec((None, tm, D), lambda b, t: (b, t, 0))
    vec = pl.BlockSpec((1, D), lambda b, t: (0, 0))
    return pl.pallas_call(
        body, name="conv_out_fwd", grid=(B, T // tm),
        in_specs=[tok, tok, _past_halo_spec(tm, CONV_HALO, D), pl.BlockSpec((None, 3, D), lambda b, t: (b, 0, 0)),
                  pl.BlockSpec((K + 1, D), lambda b, t: (0, 0)), vec, vec, vec,
                  pl.BlockSpec((D, D), lambda b, t: (0, 0)), vec],
        out_specs=[tok, tok, tok], out_shape=[jax.ShapeDtypeStruct((B, T, D), F32)] * 3,
        scratch_shapes=[pltpu.VMEM((tm + CONV_HALO, D), F32)],
        compiler_params=_cparams(2),
    )(x, u, u, mod3, w_dw, b_dw, ln_g, ln_b, w_pw, b_pw)


def conv_out_bwd(dres, y, u2, mod3, ln_g, ln_b, w_pw):
    B, T, D = dres.shape
    tm = _tile(T, 512)

    def body(dres_ref, y_ref, u2_ref, mod_ref, lg_ref, lb_ref, wpw_ref, du2_ref, u3_ref, dy_ref, dgate_ref, vec_ref):
        t = pl.program_id(1)

        @pl.when(t == 0)
        def _():
            dgate_ref[...] = jnp.zeros_like(dgate_ref)
            vec_ref[...] = jnp.zeros_like(vec_ref)

        dres = dres_ref[...]
        dy = (1.0 + mod_ref[2:3, :]) * dres
        dy_ref[...] = dy.astype(BF16)
        dgate_ref[...] += _sum0(dres * y_ref[...])
        xh, rs = _layer_norm_parts(u2_ref[...])
        lg = lg_ref[...]
        l = xh * lg + lb_ref[...]
        sg = _sigmoid(l)
        u3_ref[...] = (l * sg).astype(BF16)
        du3 = _mm_nt(dy, wpw_ref[...])
        dl = du3 * _dsilu(l, sg)
        dxh = dl * lg
        du2 = rs * (dxh - jnp.mean(dxh, axis=-1, keepdims=True) - xh * jnp.mean(dxh * xh, axis=-1, keepdims=True))
        du2_ref[...] = du2
        vec_ref[0:1, :] += _sum0(dy)
        vec_ref[1:2, :] += _sum0(dl * xh)
        vec_ref[2:3, :] += _sum0(dl)
        vec_ref[3:4, :] += _sum0(du2)

    tok = pl.BlockSpec((None, tm, D), lambda b, t: (b, t, 0))
    tokb = pl.BlockSpec((None, tm, D), lambda b, t: (b, t, 0))
    vec = pl.BlockSpec((1, D), lambda b, t: (0, 0))
    return pl.pallas_call(
        body, name="conv_out_bwd", grid=(B, T // tm),
        in_specs=[tok, tok, tok, pl.BlockSpec((None, 3, D), lambda b, t: (b, 0, 0)), vec, vec,
                  pl.BlockSpec((D, D), lambda b, t: (0, 0))],
        out_specs=[tok, tokb, tokb, pl.BlockSpec((None, 1, D), lambda b, t: (b, 0, 0)),
                   pl.BlockSpec((None, 4, D), lambda b, t: (b, 0, 0))],
        out_shape=[jax.ShapeDtypeStruct((B, T, D), F32), jax.ShapeDtypeStruct((B, T, D), BF16),
                   jax.ShapeDtypeStruct((B, T, D), BF16), jax.ShapeDtypeStruct((B, 1, D), F32),
                   jax.ShapeDtypeStruct((B, 4, D), F32)],
        compiler_params=_cparams(2),
    )(dres, y, u2, mod3, ln_g, ln_b, w_pw)


def conv_glu_bwd(x, dres, du2, u, mod3, g, w_glu, b_glu, w_dw):
    B, T, D = x.shape
    K = w_dw.shape[0] - 1
    tm = _tile(T, 256)
    nt = T // tm

    def body(x_ref, dres_ref, du2_ref, du2h_ref, u_ref, uh_ref, mod_ref, g_ref, w_ref, b_ref, wdw_ref,
             dx_ref, h_ref, dab_ref, dwdw_ref, dbglu_ref, dmod_ref, dg_ref, extu_s, extd_s):
        t = pl.program_id(1)

        @pl.when(t == 0)
        def _():
            dwdw_ref[...] = jnp.zeros_like(dwdw_ref)
            dbglu_ref[...] = jnp.zeros_like(dbglu_ref)
            dmod_ref[...] = jnp.zeros_like(dmod_ref)
            dg_ref[...] = jnp.zeros_like(dg_ref)

        du2 = du2_ref[...]
        extu_s[0:CONV_HALO, :] = jnp.where(t > 0, uh_ref[...], 0.0)
        extu_s[CONV_HALO:, :] = u_ref[...]
        extd_s[0:tm, :] = du2
        extd_s[tm:, :] = jnp.where(t < nt - 1, du2h_ref[...], 0.0)
        du = jnp.zeros((tm, D), F32)
        for k in range(K):
            du = du + wdw_ref[k:k + 1, :] * extd_s[pl.ds(K - 1 - k, tm), :]
            dwdw_ref[k:k + 1, :] += _sum0(du2 * extu_s[pl.ds(CONV_HALO - (K - 1) + k, tm), :])
        xv = x_ref[...]
        h = _modnorm(xv, g_ref[...], mod_ref[1:2, :], mod_ref[0:1, :]).astype(BF16)
        h_ref[...] = h
        a, b = _glu_fwd(h, w_ref, b_ref[...])
        sb = _sigmoid(b)
        da = du * sb
        db = du * a * sb * (1.0 - sb)
        dbglu_ref[:, 0:D] += _sum0(da)
        dbglu_ref[:, D:] += _sum0(db)
        da = da.astype(BF16)
        db = db.astype(BF16)
        dab_ref[:, 0:D] = da
        dab_ref[:, D:] = db
        Dh2 = D // 2
        dh = (_mm_nt(da[:, :Dh2], w_ref[0]) + _mm_nt(da[:, Dh2:], w_ref[1])
              + _mm_nt(db[:, :Dh2], w_ref[2]) + _mm_nt(db[:, Dh2:], w_ref[3]))
        dxn, dg, dscale, dshift = _modnorm_bwd(xv, g_ref[...], mod_ref[1:2, :], dh)
        dx_ref[...] = dres_ref[...] + dxn
        dmod_ref[0:1, :] += dshift
        dmod_ref[1:2, :] += dscale
        dg_ref[...] += dg

    tok = pl.BlockSpec((None, tm, D), lambda b, t: (b, t, 0))
    return pl.pallas_call(
        body, name="conv_glu_bwd", grid=(B, nt),
        in_specs=[tok, tok, tok, _future_halo_spec(tm, CONV_HALO, D, T), tok, _past_halo_spec(tm, CONV_HALO, D),
                  pl.BlockSpec((None, 3, D), lambda b, t: (b, 0, 0)), pl.BlockSpec((1, D), lambda b, t: (0, 0)),
                  pl.BlockSpec((4, D, D // 2), lambda b, t: (0, 0, 0)), pl.BlockSpec((1, 2 * D), lambda b, t: (0, 0)),
                  pl.BlockSpec((K + 1, D), lambda b, t: (0, 0))],
        out_specs=[tok, tok, pl.BlockSpec((None, tm, 2 * D), lambda b, t: (b, t, 0)),
                   pl.BlockSpec((None, K + 1, D), lambda b, t: (b, 0, 0)),
                   pl.BlockSpec((None, 1, 2 * D), lambda b, t: (b, 0, 0)),
                   pl.BlockSpec((None, 3, D), lambda b, t: (b, 0, 0)),
                   pl.BlockSpec((None, 1, D), lambda b, t: (b, 0, 0))],
        out_shape=[jax.ShapeDtypeStruct((B, T, D), F32), jax.ShapeDtypeStruct((B, T, D), BF16),
                   jax.ShapeDtypeStruct((B, T, 2 * D), BF16), jax.ShapeDtypeStruct((B, K + 1, D), F32),
                   jax.ShapeDtypeStruct((B, 1, 2 * D), F32), jax.ShapeDtypeStruct((B, 3, D), F32),
                   jax.ShapeDtypeStruct((B, 1, D), F32)],
        scratch_shapes=[pltpu.VMEM((tm + CONV_HALO, D), F32), pltpu.VMEM((tm + CONV_HALO, D), F32)],
        compiler_params=_cparams(2),
    )(x, dres, du2, du2, u, u, mod3, g, w_glu, b_glu, w_dw)


def dn_proj_fwd(x, mod3, g, w_main, w_ab):
    B, T, D = x.shape
    W = w_main.shape[1] // 4
    tm = _tile(T, 512)

    def body(x_ref, mod_ref, g_ref, wm_ref, wab_ref, pre_ref, z_ref, ab_ref):
        h = _modnorm(x_ref[...], g_ref[...], mod_ref[1:2, :], mod_ref[0:1, :]).astype(BF16)
        for p in range(3):
            pre_ref[:, p * W:(p + 1) * W] = _mm(h, wm_ref[:, p * W:(p + 1) * W])
        z_ref[...] = _mm(h, wm_ref[:, 3 * W:])
        ab_ref[...] = _mm(h, wab_ref[...])

    return pl.pallas_call(
        body, name="dn_proj_fwd", grid=(B, T // tm),
        in_specs=[pl.BlockSpec((None, tm, D), lambda b, t: (b, t, 0)), pl.BlockSpec((None, 3, D), lambda b, t: (b, 0, 0)),
                  pl.BlockSpec((1, D), lambda b, t: (0, 0)), pl.BlockSpec((D, 4 * W), lambda b, t: (0, 0)),
                  pl.BlockSpec((D, LANES), lambda b, t: (0, 0))],
        out_specs=[pl.BlockSpec((None, tm, 3 * W), lambda b, t: (b, t, 0)),
                   pl.BlockSpec((None, tm, W), lambda b, t: (b, t, 0)),
                   pl.BlockSpec((None, tm, LANES), lambda b, t: (b, t, 0))],
        out_shape=[jax.ShapeDtypeStruct((B, T, 3 * W), F32), jax.ShapeDtypeStruct((B, T, W), F32),
                   jax.ShapeDtypeStruct((B, T, LANES), F32)],
        compiler_params=_cparams(2),
    )(x, mod3, g, w_main, w_ab)


def _sconv(ext_s, w_ref, tm, K):
    acc = w_ref[0:1, :] * ext_s[pl.ds(SCONV_HALO - (K - 1), tm), :]
    for k in range(1, K):
        acc = acc + w_ref[k:k + 1, :] * ext_s[pl.ds(SCONV_HALO - (K - 1) + k, tm), :]
    return acc


def _lane_col(val, lane, idx):
    return jnp.sum(jnp.where(lane == idx, val, 0.0), axis=1, keepdims=True)


def dn_conv_fwd(pre, ab, w_sconv, alog_row, dt_row, H):
    B, T, W3 = pre.shape
    W = W3 // 3
    Dh = W // H
    K = w_sconv.shape[0]
    tm = _tile(T, 512)

    def body(pre_ref, halo_ref, ab_ref, w_ref, alog_ref, dt_ref, q_ref, k_ref, v_ref, gb_ref, bb_ref, ext_s):
        t = pl.program_id(1)
        ext_s[0:SCONV_HALO, :] = jnp.where(t > 0, halo_ref[...], 0.0)
        ext_s[SCONV_HALO:, :] = pre_ref[...]
        cv = _sconv(ext_s, w_ref, tm, K)
        qkv = cv * _sigmoid(cv)
        ab = ab_ref[...]
        lane = lax.broadcasted_iota(jnp.int32, ab.shape, 1)
        g_all = -jnp.exp(alog_ref[...]) * _softplus(ab + dt_ref[...])
        beta_all = _sigmoid(ab)
        for h in range(H):
            q_ref[h] = qkv[:, h * Dh:(h + 1) * Dh]
            k_ref[h] = qkv[:, W + h * Dh:W + (h + 1) * Dh]
            v_ref[h] = qkv[:, 2 * W + h * Dh:2 * W + (h + 1) * Dh]
            gb_ref[h] = jnp.broadcast_to(_lane_col(g_all, lane, h), (tm, Dh))
            bb_ref[h] = jnp.broadcast_to(_lane_col(beta_all, lane, H + h), (tm, Dh))

    hm = pl.BlockSpec((None, H, tm, Dh), lambda b, t: (b, 0, t, 0))
    row = pl.BlockSpec((1, LANES), lambda b, t: (0, 0))
    return pl.pallas_call(
        body, name="dn_conv_fwd", grid=(B, T // tm),
        in_specs=[pl.BlockSpec((None, tm, W3), lambda b, t: (b, t, 0)), _past_halo_spec(tm, SCONV_HALO, W3),
                  pl.BlockSpec((None, tm, LANES), lambda b, t: (b, t, 0)),
                  pl.BlockSpec((K, W3), lambda b, t: (0, 0)), row, row],
        out_specs=[hm] * 5, out_shape=[jax.ShapeDtypeStruct((B, H, T, Dh), F32)] * 5,
        scratch_shapes=[pltpu.VMEM((tm + SCONV_HALO, W3), F32)],
        compiler_params=_cparams(2),
    )(pre, pre, ab, w_sconv, alog_row, dt_row)


def _bdot(spec):
    return lambda a, b: jnp.einsum(spec, a.astype(BF16), b.astype(BF16), preferred_element_type=F32)


_NN, _NT, _TN = "gij,gjk->gik", "gik,gjk->gij", "gki,gkj->gij"


def _make_bdots():
    nn_, nt_, tn_ = _bdot(_NN), _bdot(_NT), _bdot(_TN)

    @jax.custom_vjp
    def nn(a, b):
        return nn_(a, b)

    @jax.custom_vjp
    def nt(a, b):
        return nt_(a, b)

    @jax.custom_vjp
    def tn(a, b):
        return tn_(a, b)

    nn.defvjp(lambda a, b: (nn_(a, b), (a, b)), lambda r, d: (nt_(d, r[1]), tn_(r[0], d)))
    nt.defvjp(lambda a, b: (nt_(a, b), (a, b)), lambda r, d: (nn_(d, r[1]), tn_(d, r[0])))
    tn.defvjp(lambda a, b: (tn_(a, b), (a, b)), lambda r, d: (nt_(r[1], d), nn_(r[0], d)))
    return nn, nt, tn


def _unit_lower_inverse(A):
    hdot = functools.partial(jnp.einsum, precision=lax.Precision.HIGH, preferred_element_type=F32)
    C = A.shape[-1]

    def impl(A):
        eye = (lax.broadcasted_iota(jnp.int32, A.shape, 1) == lax.broadcasted_iota(jnp.int32, A.shape, 2)).astype(F32)
        Tm = eye - A
        Ap = A
        for _ in range(max(1, (C - 1).bit_length()) - 1):
            Ap = hdot(_NN, Ap, Ap)
            Tm = Tm + hdot(_NN, Tm, Ap)
        return Tm

    @jax.custom_vjp
    def inv(A):
        return impl(A)

    def fwd(A):
        Tm = impl(A)
        return Tm, Tm

    def bwd(Tm, dT):
        return (-hdot(_NT, hdot(_TN, Tm, dT), Tm),)

    inv.defvjp(fwd, bwd)
    return inv(A)


def _chunk_fn(q, k, v, gb, bb, S):
    nn, nt, tn = _make_bdots()
    G, C, Dh = q.shape
    hdot = functools.partial(jnp.einsum, precision=lax.Precision.HIGH, preferred_element_type=F32)
    q = q * lax.rsqrt(jnp.sum(q * q, axis=-1, keepdims=True) + EPS) * (Dh ** -0.5)
    k = k * lax.rsqrt(jnp.sum(k * k, axis=-1, keepdims=True) + EPS)
    row = lax.broadcasted_iota(jnp.int32, (G, C, C), 1)
    col = lax.broadcasted_iota(jnp.int32, (G, C, C), 2)
    causal = row >= col
    strict = row > col
    gc = hdot(_NN, causal.astype(F32), gb)
    spread = jnp.full((G, C, Dh), 1.0 / Dh, F32)
    gi = hdot(_NT, gc, spread)
    gj = hdot(_NT, spread, gc)
    decay = jnp.where(causal, jnp.exp(jnp.where(causal, gi - gj, 0.0)), 0.0)
    kb = k * bb
    vb = v * bb
    A = jnp.where(strict, nt(kb, k) * decay, 0.0)
    Tm = _unit_lower_inverse(A)
    eg = jnp.exp(gc)
    u = nn(Tm, vb)
    w = nn(Tm, kb * eg)
    qg = q * eg
    intra = nt(q, k) * decay
    glast = hdot(_NN, jnp.ones((G, C, C), F32), gb)
    kd = k * jnp.exp(glast - gc)
    v_new = u - nn(w, S)
    o = nn(qg, S) + nn(intra, v_new)
    egl = jnp.exp(glast)
    S_new = S * jnp.concatenate([egl] * (Dh // C), axis=1) + tn(kd, v_new)
    return o, S_new


def dn_chunk_fwd(q, k, v, gb, bb):
    B, H, T, Dh = q.shape
    NC = T // CHUNK

    def body(q_ref, k_ref, v_ref, gb_ref, bb_ref, o_ref, sp_ref, S_s):
        @pl.when(pl.program_id(1) == 0)
        def _():
            S_s[...] = jnp.zeros_like(S_s)

        S = S_s[...]
        sp_ref[...] = S
        o, S_new = _chunk_fn(q_ref[...], k_ref[...], v_ref[...], gb_ref[...], bb_ref[...], S)
        o_ref[...] = o
        S_s[...] = S_new

    hm = pl.BlockSpec((None, H, CHUNK, Dh), lambda b, n: (b, 0, n, 0))
    return pl.pallas_call(
        body, name="dn_chunk_fwd", grid=(B, NC),
        in_specs=[hm] * 5,
        out_specs=[hm, pl.BlockSpec((None, None, H, Dh, Dh), lambda b, n: (b, n, 0, 0, 0))],
        out_shape=[jax.ShapeDtypeStruct((B, H, T, Dh), F32), jax.ShapeDtypeStruct((B, NC, H, Dh, Dh), F32)],
        scratch_shapes=[pltpu.VMEM((H, Dh, Dh), F32)],
        compiler_params=_cparams(2),
    )(q, k, v, gb, bb)


def dn_chunk_bwd(q, k, v, gb, bb, s_prev, do):
    B, H, T, Dh = q.shape
    NC = T // CHUNK

    def body(q_ref, k_ref, v_ref, gb_ref, bb_ref, sp_ref, do_ref, dq_ref, dk_ref, dv_ref, dgb_ref, dbb_ref, dS_s):
        @pl.when(pl.program_id(1) == 0)
        def _():
            dS_s[...] = jnp.zeros_like(dS_s)

        _, vjp = jax.vjp(_chunk_fn, q_ref[...], k_ref[...], v_ref[...], gb_ref[...], bb_ref[...], sp_ref[...])
        dq, dk, dv, dgb, dbb, dS = vjp((do_ref[...], dS_s[...]))
        dq_ref[...] = dq
        dk_ref[...] = dk
        dv_ref[...] = dv
        dgb_ref[...] = dgb
        dbb_ref[...] = dbb
        dS_s[...] = dS

    hm = pl.BlockSpec((None, H, CHUNK, Dh), lambda b, n: (b, 0, NC - 1 - n, 0))
    return pl.pallas_call(
        body, name="dn_chunk_bwd", grid=(B, NC),
        in_specs=[hm] * 5 + [pl.BlockSpec((None, None, H, Dh, Dh), lambda b, n: (b, NC - 1 - n, 0, 0, 0)), hm],
        out_specs=[hm] * 5, out_shape=[jax.ShapeDtypeStruct((B, H, T, Dh), F32)] * 5,
        scratch_shapes=[pltpu.VMEM((H, Dh, Dh), F32)],
        compiler_params=_cparams(2),
    )(q, k, v, gb, bb, s_prev, do)


def _head_norm(o, og):
    r = lax.rsqrt(jnp.mean(o * o, axis=-1, keepdims=True) + EPS)
    return o * r, r


def dn_out_fwd(x, o, z, mod3, o_g, w_out):
    B, T, D = x.shape
    _, H, _, Dh = o.shape
    W = H * Dh
    tm = _tile(T, 512)

    def body(x_ref, o_ref, z_ref, mod_ref, og_ref, w_ref, xo_ref, y_ref):
        parts = []
        for h in range(H):
            on, _ = _head_norm(o_ref[h], og_ref[...])
            zz = z_ref[:, h * Dh:(h + 1) * Dh]
            parts.append((on * og_ref[...] * (zz * _sigmoid(zz))).astype(BF16))
        y = _mm(jnp.concatenate(parts, axis=1), w_ref[...])
        y_ref[...] = y
        xo_ref[...] = x_ref[...] + (1.0 + mod_ref[2:3, :]) * y

    tok = pl.BlockSpec((None, tm, D), lambda b, t: (b, t, 0))
    return pl.pallas_call(
        body, name="dn_out_fwd", grid=(B, T // tm),
        in_specs=[tok, pl.BlockSpec((None, H, tm, Dh), lambda b, t: (b, 0, t, 0)),
                  pl.BlockSpec((None, tm, W), lambda b, t: (b, t, 0)), pl.BlockSpec((None, 3, D), lambda b, t: (b, 0, 0)),
                  pl.BlockSpec((1, Dh), lambda b, t: (0, 0)), pl.BlockSpec((W, D), lambda b, t: (0, 0))],
        out_specs=[tok, tok], out_shape=[jax.ShapeDtypeStruct((B, T, D), F32)] * 2,
        compiler_params=_cparams(2),
    )(x, o, z, mod3, o_g, w_out)


def dn_out_bwd(dres, y, o, z, mod3, o_g, w_out):
    B, T, D = dres.shape
    _, H, _, Dh = o.shape
    W = H * Dh
    tm = _tile(T, 512)

    def body(dres_ref, y_ref, o_ref, z_ref, mod_ref, og_ref, w_ref, do_ref, dz_ref, ogb_ref, dy_ref, dgate_ref, dog_ref):
        t = pl.program_id(1)

        @pl.when(t == 0)
        def _():
            dgate_ref[...] = jnp.zeros_like(dgate_ref)
            dog_ref[...] = jnp.zeros_like(dog_ref)

        dres = dres_ref[...]
        dy = ((1.0 + mod_ref[2:3, :]) * dres).astype(BF16)
        dy_ref[...] = dy
        dgate_ref[...] += _sum0(dres * y_ref[...])
        dog = _mm_nt(dy, w_ref[...])
        og = og_ref[...]
        for h in range(H):
            ov = o_ref[h]
            xn, r = _head_norm(ov, og)
            zz = z_ref[:, h * Dh:(h + 1) * Dh]
            sg = _sigmoid(zz)
            sz = zz * sg
            d = dog[:, h * Dh:(h + 1) * Dh]
            ogb_ref[:, h * Dh:(h + 1) * Dh] = (xn * og * sz).astype(BF16)
            dz_ref[:, h * Dh:(h + 1) * Dh] = d * (xn * og) * _dsilu(zz, sg)
            don = d * sz
            dog_ref[...] += _sum0(don * xn)
            dxn = don * og
            do_ref[h] = r * (dxn - xn * jnp.mean(dxn * xn, axis=-1, keepdims=True))

    tok = pl.BlockSpec((None, tm, D), lambda b, t: (b, t, 0))
    tokw = pl.BlockSpec((None, tm, W), lambda b, t: (b, t, 0))
    hm = pl.BlockSpec((None, H, tm, Dh), lambda b, t: (b, 0, t, 0))
    return pl.pallas_call(
        body, name="dn_out_bwd", grid=(B, T // tm),
        in_specs=[tok, tok, hm, tokw, pl.BlockSpec((None, 3, D), lambda b, t: (b, 0, 0)),
                  pl.BlockSpec((1, Dh), lambda b, t: (0, 0)), pl.BlockSpec((W, D), lambda b, t: (0, 0))],
        out_specs=[hm, tokw, tokw, tok, pl.BlockSpec((None, 1, D), lambda b, t: (b, 0, 0)),
                   pl.BlockSpec((None, 1, Dh), lambda b, t: (b, 0, 0))],
        out_shape=[jax.ShapeDtypeStruct((B, H, T, Dh), F32), jax.ShapeDtypeStruct((B, T, W), F32),
                   jax.ShapeDtypeStruct((B, T, W), BF16), jax.ShapeDtypeStruct((B, T, D), BF16),
                   jax.ShapeDtypeStruct((B, 1, D), F32), jax.ShapeDtypeStruct((B, 1, Dh), F32)],
        compiler_params=_cparams(2),
    )(dres, y, o, z, mod3, o_g, w_out)


def dn_conv_bwd(dq, dk, dv, dgb, dbb, pre, ab, w_sconv, alog_row, dt_row):
    B, H, T, Dh = dq.shape
    W = H * Dh
    W3 = 3 * W
    K = w_sconv.shape[0]
    tm = _tile(T, 256)

    def body(dq_ref, dk_ref, dv_ref, dgb_ref, dbb_ref, pre_ref, halo_ref, ab_ref, w_ref, alog_ref, dt_ref,
             dc_ref, dab_ref, small_ref, ext_s):
        t = pl.program_id(1)

        @pl.when(t == 0)
        def _():
            small_ref[...] = jnp.zeros_like(small_ref)

        ext_s[0:SCONV_HALO, :] = jnp.where(t > 0, halo_ref[...], 0.0)
        ext_s[SCONV_HALO:, :] = pre_ref[...]
        cv = _sconv(ext_s, w_ref, tm, K)
        dsl = _dsilu(cv, _sigmoid(cv))
        ab = ab_ref[...]
        lane = lax.broadcasted_iota(jnp.int32, ab.shape, 1)
        dg_all = jnp.zeros_like(ab)
        db_all = jnp.zeros_like(ab)
        for h in range(H):
            dc_ref[:, h * Dh:(h + 1) * Dh] = dq_ref[h] * dsl[:, h * Dh:(h + 1) * Dh]
            dc_ref[:, W + h * Dh:W + (h + 1) * Dh] = dk_ref[h] * dsl[:, W + h * Dh:W + (h + 1) * Dh]
            dc_ref[:, 2 * W + h * Dh:2 * W + (h + 1) * Dh] = dv_ref[h] * dsl[:, 2 * W + h * Dh:2 * W + (h + 1) * Dh]
            dg_all = dg_all + jnp.where(lane == h, jnp.sum(dgb_ref[h], axis=1, keepdims=True), 0.0)
            db_all = db_all + jnp.where(lane == H + h, jnp.sum(dbb_ref[h], axis=1, keepdims=True), 0.0)
        xa = ab + dt_ref[...]
        ea = -jnp.exp(alog_ref[...])
        g_all = ea * _softplus(xa)
        da = dg_all * ea * _sigmoid(xa)
        beta = _sigmoid(ab)
        dab_ref[...] = da + db_all * beta * (1.0 - beta)
        small_ref[0:1, :] += _sum0(dg_all * g_all)
        small_ref[1:2, :] += _sum0(da)

    hm = pl.BlockSpec((None, H, tm, Dh), lambda b, t: (b, 0, t, 0))
    row = pl.BlockSpec((1, LANES), lambda b, t: (0, 0))
    return pl.pallas_call(
        body, name="dn_conv_bwd", grid=(B, T // tm),
        in_specs=[hm] * 5 + [pl.BlockSpec((None, tm, W3), lambda b, t: (b, t, 0)), _past_halo_spec(tm, SCONV_HALO, W3),
                             pl.BlockSpec((None, tm, LANES), lambda b, t: (b, t, 0)),
                             pl.BlockSpec((K, W3), lambda b, t: (0, 0)), row, row],
        out_specs=[pl.BlockSpec((None, tm, W3), lambda b, t: (b, t, 0)), pl.BlockSpec((None, tm, LANES), lambda b, t: (b, t, 0)),
                   pl.BlockSpec((None, 2, LANES), lambda b, t: (b, 0, 0))],
        out_shape=[jax.ShapeDtypeStruct((B, T, W3), F32), jax.ShapeDtypeStruct((B, T, LANES), F32),
                   jax.ShapeDtypeStruct((B, 2, LANES), F32)],
        scratch_shapes=[pltpu.VMEM((tm + SCONV_HALO, W3), F32)],
        compiler_params=_cparams(2),
    )(dq, dk, dv, dgb, dbb, pre, pre, ab, w_sconv, alog_row, dt_row)


def dn_proj_bwd(x, dres, dc, pre, dz, dab, mod3, g, w_main, w_ab, w_sconv):
    B, T, D = x.shape
    W3 = dc.shape[2]
    W = W3 // 3
    K = w_sconv.shape[0]
    tm = _tile(T, 256)
    nt = T // tm

    def body(x_ref, dres_ref, dc_ref, dch_ref, pre_ref, preh_ref, dz_ref, dab_ref, mod_ref, g_ref, wm_ref, wab_ref, ws_ref,
             dx_ref, h_ref, dproj_ref, dws_ref, dmod_ref, dg_ref, extp_s, extd_s):
        t = pl.program_id(1)

        @pl.when(t == 0)
        def _():
            dws_ref[...] = jnp.zeros_like(dws_ref)
            dmod_ref[...] = jnp.zeros_like(dmod_ref)
            dg_ref[...] = jnp.zeros_like(dg_ref)

        dc = dc_ref[...]
        extp_s[0:SCONV_HALO, :] = jnp.where(t > 0, preh_ref[...], 0.0)
        extp_s[SCONV_HALO:, :] = pre_ref[...]
        extd_s[0:tm, :] = dc
        extd_s[tm:, :] = jnp.where(t < nt - 1, dch_ref[...], 0.0)
        dpre = jnp.zeros((tm, W3), F32)
        for k in range(K):
            dpre = dpre + ws_ref[k:k + 1, :] * extd_s[pl.ds(K - 1 - k, tm), :]
            dws_ref[k:k + 1, :] += _sum0(dc * extp_s[pl.ds(SCONV_HALO - (K - 1) + k, tm), :])
        dpre = dpre.astype(BF16)
        dzb = dz_ref[...].astype(BF16)
        dproj_ref[:, 0:W3] = dpre
        dproj_ref[:, W3:] = dzb
        dh = _mm_nt(dab_ref[...], wab_ref[...]) + _mm_nt(dzb, wm_ref[:, W3:])
        for p in range(3):
            dh = dh + _mm_nt(dpre[:, p * W:(p + 1) * W], wm_ref[:, p * W:(p + 1) * W])
        xv = x_ref[...]
        h_ref[...] = _modnorm(xv, g_ref[...], mod_ref[1:2, :], mod_ref[0:1, :]).astype(BF16)
        dxn, dg, dscale, dshift = _modnorm_bwd(xv, g_ref[...], mod_ref[1:2, :], dh)
        dx_ref[...] = dres_ref[...] + dxn
        dmod_ref[0:1, :] += dshift
        dmod_ref[1:2, :] += dscale
        dg_ref[...] += dg

    tok = pl.BlockSpec((None, tm, D), lambda b, t: (b, t, 0))
    tok3 = pl.BlockSpec((None, tm, W3), lambda b, t: (b, t, 0))
    return pl.pallas_call(
        body, name="dn_proj_bwd", grid=(B, nt),
        in_specs=[tok, tok, tok3, _future_halo_spec(tm, SCONV_HALO, W3, T), tok3, _past_halo_spec(tm, SCONV_HALO, W3),
                  pl.BlockSpec((None, tm, W), lambda b, t: (b, t, 0)), pl.BlockSpec((None, tm, LANES), lambda b, t: (b, t, 0)),
                  pl.BlockSpec((None, 3, D), lambda b, t: (b, 0, 0)), pl.BlockSpec((1, D), lambda b, t: (0, 0)),
                  pl.BlockSpec((D, 4 * W), lambda b, t: (0, 0)), pl.BlockSpec((D, LANES), lambda b, t: (0, 0)),
                  pl.BlockSpec((K, W3), lambda b, t: (0, 0))],
        out_specs=[tok, tok, pl.BlockSpec((None, tm, 4 * W), lambda b, t: (b, t, 0)),
                   pl.BlockSpec((None, K, W3), lambda b, t: (b, 0, 0)), pl.BlockSpec((None, 3, D), lambda b, t: (b, 0, 0)),
                   pl.BlockSpec((None, 1, D), lambda b, t: (b, 0, 0))],
        out_shape=[jax.ShapeDtypeStruct((B, T, D), F32), jax.ShapeDtypeStruct((B, T, D), BF16),
                   jax.ShapeDtypeStruct((B, T, 4 * W), BF16), jax.ShapeDtypeStruct((B, K, W3), F32),
                   jax.ShapeDtypeStruct((B, 3, D), F32), jax.ShapeDtypeStruct((B, 1, D), F32)],
        scratch_shapes=[pltpu.VMEM((tm + SCONV_HALO, W3), F32), pltpu.VMEM((tm + SCONV_HALO, W3), F32)],
        compiler_params=_cparams(2),
    )(x, dres, dc, dc, pre, pre, dz, dab, mod3, g, w_main, w_ab, w_sconv)


def ada_fwd(c_all, w_ada, b_cols):
    L, D, Ca = w_ada.shape
    NB = c_all.shape[0]

    def body(c_ref, w_ref, b_ref, o_ref):
        cv = c_ref[...]
        o_ref[...] = _mm(cv * _sigmoid(cv), w_ref[...]) + b_ref[...]

    return pl.pallas_call(
        body, name="ada_fwd", grid=(L,),
        in_specs=[pl.BlockSpec((NB, D), lambda i: (0, 0)), pl.BlockSpec((None, D, Ca), lambda i: (i, 0, 0)),
                  pl.BlockSpec((None, 1, Ca), lambda i: (i, 0, 0))],
        out_specs=pl.BlockSpec((None, NB, Ca), lambda i: (i, 0, 0)),
        out_shape=jax.ShapeDtypeStruct((L, NB, Ca), F32),
        compiler_params=_cparams(1),
    )(c_all, w_ada, b_cols)


def ada_bwd(c_all, dmod_cols, dmod_all):
    L, NB, Ca = dmod_cols.shape
    D = c_all.shape[1]
    C9 = dmod_all.shape[2]

    def body(c_ref, dc_ref, da_ref, gw_ref, gb_ref):
        cv = c_ref[...]
        gw_ref[...] = _mm_tn(cv * _sigmoid(cv), dc_ref[...])
        gb_ref[...] = _sum0(da_ref[...])

    return pl.pallas_call(
        body, name="ada_bwd", grid=(L,),
        in_specs=[pl.BlockSpec((NB, D), lambda i: (0, 0)), pl.BlockSpec((None, NB, Ca), lambda i: (i, 0, 0)),
                  pl.BlockSpec((None, NB, C9), lambda i: (i, 0, 0))],
        out_specs=[pl.BlockSpec((None, D, Ca), lambda i: (i, 0, 0)), pl.BlockSpec((None, 1, C9), lambda i: (i, 0, 0))],
        out_shape=[jax.ShapeDtypeStruct((L, D, Ca), F32), jax.ShapeDtypeStruct((L, 1, C9), F32)],
        compiler_params=_cparams(1),
    )(c_all, dmod_cols, dmod_all)


def adamw(w, g, m, v, name):
    R, C = w.shape
    tr = _tile(R, max(8, (1 << 18) // C))

    def body(w_ref, g_ref, m_ref, v_ref, d_ref, mo_ref, vo_ref):
        gv = g_ref[...]
        mn = ADAM_B1 * m_ref[...] + (1.0 - ADAM_B1) * gv
        vn = ADAM_B2 * v_ref[...] + (1.0 - ADAM_B2) * (gv * gv)
        m_hat = mn / (1.0 - ADAM_B1 ** ADAM_STEP)
        v_hat = vn / (1.0 - ADAM_B2 ** ADAM_STEP)
        d_ref[...] = -ADAM_LR * (m_hat / (jnp.sqrt(v_hat) + ADAM_EPS) + ADAM_WD * w_ref[...])
        mo_ref[...] = mn
        vo_ref[...] = vn

    blk = pl.BlockSpec((tr, C), lambda i: (i, 0))
    return pl.pallas_call(
        body, name=name, grid=(R // tr,), in_specs=[blk] * 4, out_specs=[blk] * 3,
        out_shape=[jax.ShapeDtypeStruct((R, C), F32)] * 3, compiler_params=_cparams(1),
    )(w, g, m, v)


def sum_devices(a):
    n, R, C = a.shape

    def body(a_ref, o_ref):
        s = a_ref[0]
        for d in range(1, n):
            s = s + a_ref[d]
        o_ref[...] = s

    return pl.pallas_call(
        body, name="sum_devices", out_shape=jax.ShapeDtypeStruct((R, C), F32),
        compiler_params=pltpu.CompilerParams(vmem_limit_bytes=VMEM_LIMIT_V7X),
    )(a)


def _place():
    x, y, c = lax.axis_index("x"), lax.axis_index("y"), lax.axis_index("c")
    return x, y, c


def _other_chips(x, y):
    return [(2 * (1 - x) + y, 1 - x, y), (2 * x + (1 - y), x, 1 - y), (2 * (1 - x) + (1 - y), 1 - x, 1 - y)]


def allgather8(block):
    m_per, n = block.shape

    def body(x_ref, out_ref, send_sems, recv_sems, local_sem):
        x, y, c = _place()
        me, sibling = (x, y, c), (x, y, 1 - c)
        chips = [(1 - x, y), (x, 1 - y), (1 - x, 1 - y)]

        def rows(px, py, pc):
            return out_ref.at[pl.ds((4 * px + 2 * py + pc) * m_per, m_per), :]

        def copy(k, blk, to, src=None):
            return pltpu.make_async_remote_copy(
                src_ref=rows(*blk) if src is None else src, dst_ref=rows(*blk),
                send_sem=send_sems.at[k], recv_sem=recv_sems.at[k], device_id=to, device_id_type=MESH)

        mine = pltpu.make_async_copy(x_ref, rows(*me), local_sem)
        mine.start()
        first = [copy(0, me, sibling, src=x_ref)]
        first += [copy(1 + j, me, (*chip, c), src=x_ref) for j, chip in enumerate(chips)]
        for cp in first:
            cp.start()
        passed = [copy(4 + j, (*chip, c), sibling) for j, chip in enumerate(chips)]
        for j, chip in enumerate(chips):
            copy(1 + j, (*chip, c), me).wait_recv()
            passed[j].start()
        copy(0, sibling, me).wait_recv()
        for j, chip in enumerate(chips):
            copy(4 + j, (*chip, 1 - c), me).wait_recv()
        for cp in first + passed:
            cp.wait_send()
        mine.wait()

    return pl.pallas_call(
        body, name="allgather8", out_shape=jax.ShapeDtypeStruct((N_DEV * m_per, n), block.dtype),
        in_specs=[pl.BlockSpec(memory_space=pltpu.VMEM)], out_specs=pl.BlockSpec(memory_space=pltpu.VMEM),
        scratch_shapes=[pltpu.SemaphoreType.DMA((7,)), pltpu.SemaphoreType.DMA((7,)), pltpu.SemaphoreType.DMA],
        compiler_params=pltpu.CompilerParams(vmem_limit_bytes=VMEM_LIMIT_V7X),
    )(block)


def _half(ref, c, rh):
    return ref.at[pl.ds(pl.multiple_of(c * rh, 16), rh), :]


def gather_weights(lands):
    K = len(lands)

    def body(*refs):
        ins, outs = refs[:K], refs[K:2 * K]
        ici_send, ici_recv, d2d_send, d2d_recv = refs[2 * K:]
        x, y, c = _place()
        me = 2 * x + y
        sibling = (x, y, 1 - c)
        others = _other_chips(x, y)
        sent = []
        for k in range(K):
            rh = ins[k].shape[1] // 2
            for r, (_, px, py) in enumerate(others):
                cp = pltpu.make_async_remote_copy(
                    src_ref=_half(ins[k].at[me], c, rh), dst_ref=_half(outs[k].at[me], c, rh),
                    send_sem=ici_send.at[k, r], recv_sem=ici_recv.at[k, r], device_id=(px, py, c), device_id_type=MESH)
                cp.start()
                sent.append(cp)
        forwards = []
        for k in range(K):
            rh = ins[k].shape[1] // 2
            for r, (pchip, px, py) in enumerate(others):
                landed = _half(outs[k].at[pchip], c, rh)
                pltpu.make_async_remote_copy(
                    src_ref=landed, dst_ref=landed, send_sem=ici_send.at[k, r], recv_sem=ici_recv.at[k, r],
                    device_id=(px, py, c), device_id_type=MESH).wait_recv()
                fw = pltpu.make_async_remote_copy(
                    src_ref=landed, dst_ref=landed, send_sem=d2d_send.at[k, r], recv_sem=d2d_recv.at[k, r],
                    device_id=sibling, device_id_type=MESH)
                fw.start()
                forwards.append(fw)
        for k in range(K):
            rh = ins[k].shape[1] // 2
            for r, (pchip, _, _) in enumerate(others):
                theirs = _half(outs[k].at[pchip], 1 - c, rh)
                pltpu.make_async_remote_copy(
                    src_ref=theirs, dst_ref=theirs, send_sem=d2d_send.at[k, r], recv_sem=d2d_recv.at[k, r],
                    device_id=sibling, device_id_type=MESH).wait_recv()
        for cp in sent + forwards:
            cp.wait_send()

    return pl.pallas_call(
        body, name="gather_weights",
        out_shape=[jax.ShapeDtypeStruct(s.shape, s.dtype) for s in lands],
        in_specs=[HBM_SPEC] * K, out_specs=[HBM_SPEC] * K, input_output_aliases={k: k for k in range(K)},
        scratch_shapes=[pltpu.SemaphoreType.DMA((K, 3))] * 4,
    )(*lands)


def pair_exchange(grads):
    K = len(grads)

    def body(*refs):
        ins, outs = refs[:K], refs[K:2 * K]
        send_sems, recv_sems = refs[2 * K:]
        x, y, c = _place()
        sibling = (x, y, 1 - c)
        copies = []
        for k in range(K):
            n, r, _ = ins[k].shape
            rh = r // 2
            cp = pltpu.make_async_remote_copy(
                src_ref=ins[k].at[:, pl.ds(pl.multiple_of((1 - c) * rh, 16), rh), :], dst_ref=outs[k],
                send_sem=send_sems.at[k], recv_sem=recv_sems.at[k], device_id=sibling, device_id_type=MESH)
            cp.start()
            copies.append(cp)
        for cp in copies:
            cp.wait_recv()
        for cp in copies:
            cp.wait_send()

    return pl.pallas_call(
        body, name="pair_exchange",
        out_shape=[jax.ShapeDtypeStruct((g.shape[0], g.shape[1] // 2, g.shape[2]), g.dtype) for g in grads],
        in_specs=[HBM_SPEC] * K, out_specs=[HBM_SPEC] * K,
        scratch_shapes=[pltpu.SemaphoreType.DMA((K,))] * 2,
    )(*grads)


def pair_add(grad, recv, c_idx):
    n, r, C = grad.shape
    rh = r // 2
    tr = _tile(rh, max(16, (1 << 19) // C), 16)
    grad = grad.reshape(n, 2, rh, C)

    def body(c_ref, g_ref, r_ref, o_ref):
        o_ref[...] = (g_ref[...].astype(F32) + r_ref[...].astype(F32)).astype(BF16)

    return pl.pallas_call(
        body, name="pair_add",
        grid_spec=pltpu.PrefetchScalarGridSpec(
            num_scalar_prefetch=1, grid=(n, rh // tr),
            in_specs=[pl.BlockSpec((None, None, tr, C), lambda d, i, c_ref: (d, c_ref[0], i, 0)),
                      pl.BlockSpec((None, tr, C), lambda d, i, c_ref: (d, i, 0))],
            out_specs=pl.BlockSpec((None, tr, C), lambda d, i, c_ref: (d, i, 0))),
        out_shape=jax.ShapeDtypeStruct((n, rh, C), BF16), compiler_params=_cparams(2),
    )(c_idx, grad, recv)


def chip_exchange(parts):
    K = len(parts)

    def body(*refs):
        ins, outs = refs[:K], refs[K:2 * K]
        send_sems, recv_sems = refs[2 * K:]
        x, y, c = _place()
        others = _other_chips(x, y)
        started = []
        for k in range(K):
            for r, (pchip, px, py) in enumerate(others):
                cp = pltpu.make_async_remote_copy(
                    src_ref=ins[k].at[pchip], dst_ref=outs[k].at[r], send_sem=send_sems.at[k, r],
                    recv_sem=recv_sems.at[k, r], device_id=(px, py, c), device_id_type=MESH)
                cp.start()
                started.append(cp)
        for cp in started:
            cp.wait_recv()
        for cp in started:
            cp.wait_send()

    return pl.pallas_call(
        body, name="chip_exchange",
        out_shape=[jax.ShapeDtypeStruct((3,) + p.shape[1:], p.dtype) for p in parts],
        in_specs=[HBM_SPEC] * K, out_specs=[HBM_SPEC] * K,
        scratch_shapes=[pltpu.SemaphoreType.DMA((K, 3))] * 2,
    )(*parts)


def chip_sum(parts, got, where):
    _, rh, C = parts.shape
    tr = _tile(rh, max(16, (1 << 19) // C), 16)
    nt = rh // tr

    def body(w_ref, p_ref, g_ref, o_ref):
        s = p_ref[...].astype(F32)
        for r in range(3):
            s = s + g_ref[r].astype(F32)
        o_ref[...] = s

    return pl.pallas_call(
        body, name="chip_sum",
        grid_spec=pltpu.PrefetchScalarGridSpec(
            num_scalar_prefetch=1, grid=(nt,),
            in_specs=[pl.BlockSpec((None, tr, C), lambda i, w_ref: (w_ref[0], i, 0)),
                      pl.BlockSpec((3, tr, C), lambda i, w_ref: (0, i, 0))],
            out_specs=pl.BlockSpec((tr, C), lambda i, w_ref: (w_ref[1] * nt + i, 0))),
        out_shape=jax.ShapeDtypeStruct((2 * rh, C), F32), compiler_params=_cparams(1),
    )(where, parts, got)


def pair_share(sums):
    K = len(sums)

    def body(*refs):
        ins, outs = refs[:K], refs[K:2 * K]
        send_sems, recv_sems = refs[2 * K:]
        x, y, c = _place()
        sibling = (x, y, 1 - c)
        started = []
        for k in range(K):
            rh = ins[k].shape[0] // 2
            cp = pltpu.make_async_remote_copy(
                src_ref=_half(ins[k], c, rh), dst_ref=_half(outs[k], c, rh), send_sem=send_sems.at[k],
                recv_sem=recv_sems.at[k], device_id=sibling, device_id_type=MESH)
            cp.start()
            started.append(cp)
        for k in range(K):
            rh = ins[k].shape[0] // 2
            theirs = _half(outs[k], 1 - c, rh)
            pltpu.make_async_remote_copy(
                src_ref=theirs, dst_ref=theirs, send_sem=send_sems.at[k], recv_sem=recv_sems.at[k],
                device_id=sibling, device_id_type=MESH).wait_recv()
        for cp in started:
            cp.wait_send()

    return pl.pallas_call(
        body, name="pair_share",
        out_shape=[jax.ShapeDtypeStruct(s.shape, s.dtype) for s in sums],
        in_specs=[HBM_SPEC] * K, out_specs=[HBM_SPEC] * K, input_output_aliases={k: k for k in range(K)},
        scratch_shapes=[pltpu.SemaphoreType.DMA((K,))] * 2,
    )(*sums)


SEM_SPEC = pl.BlockSpec(memory_space=pltpu.SEMAPHORE)
ANY_SPEC = pl.BlockSpec(memory_space=pl.ANY)
DATAFLOW = pltpu.SideEffectType.DATAFLOW_SIDE_EFFECTING


def _in_hbm(a):
    return pltpu.with_memory_space_constraint(a, pltpu.HBM)


def _ici_copies(srcs, dsts, send_sems, recv_sems, src_slice, dst_slice):
    x, y, c = _place()
    out = []
    for k in range(len(srcs)):
        for r, (pchip, px, py) in enumerate(_other_chips(x, y)):
            out.append(pltpu.make_async_remote_copy(
                src_ref=src_slice(srcs[k], r, pchip), dst_ref=dst_slice(dsts[k], r, pchip),
                send_sem=send_sems.at[3 * k + r], recv_sem=recv_sems.at[3 * k + r], device_id=(px, py, c),
                device_id_type=MESH))
    return out


def _exchange_start(bufs, lands, src_slice, dst_slice, name, after=None):
    K = len(bufs)
    same = lands is None
    n_thru = K if same else 2 * K
    n_in = n_thru + (after is not None)

    def body(*refs):
        ins = refs[:n_thru]
        send_sems, recv_sems = refs[n_in], refs[n_in + 1]
        token = refs[-1]
        srcs = ins[:K]
        dsts = srcs if same else ins[K:]
        for cp in _ici_copies(srcs, dsts, send_sems, recv_sems, src_slice, dst_slice):
            cp.start()
        token[...] = jnp.zeros_like(token)

    thru = list(bufs) + ([] if same else list(lands))
    res = pl.pallas_call(
        body, name=name,
        out_shape=[pltpu.SemaphoreType.DMA((3 * K,)), pltpu.SemaphoreType.DMA((3 * K,))]
        + [pltpu.HBM(a.shape, a.dtype) for a in thru] + [jax.ShapeDtypeStruct((8, LANES), F32)],
        in_specs=[HBM_SPEC] * n_thru + [ANY_SPEC] * (after is not None),
        out_specs=[SEM_SPEC, SEM_SPEC] + [HBM_SPEC] * n_thru + [pl.BlockSpec(memory_space=pltpu.VMEM)],
        input_output_aliases={i: 2 + i for i in range(n_thru)},
        compiler_params=pltpu.CompilerParams(has_side_effects=DATAFLOW),
    )(*[_in_hbm(a) for a in thru], *([] if after is None else [after]))
    return res[0], res[1], res[2:2 + K], (res[2:2 + K] if same else res[2 + K:2 + 2 * K]), res[-1]


def _exchange_wait(send_sems, recv_sems, bufs, lands, after, src_slice, dst_slice, name):
    K = len(bufs)
    same = lands is None
    n_thru = K if same else 2 * K

    def body(*refs):
        ins = refs[:n_thru]
        ssem, rsem = refs[n_thru], refs[n_thru + 1]
        srcs = ins[:K]
        dsts = srcs if same else ins[K:]
        copies = _ici_copies(srcs, dsts, ssem, rsem, src_slice, dst_slice)
        for cp in copies:
            cp.wait_send()
        for cp in copies:
            cp.wait_recv()

    thru = list(bufs) + ([] if same else list(lands))
    res = pl.pallas_call(
        body, name=name,
        out_shape=[pltpu.HBM(a.shape, a.dtype) for a in thru],
        in_specs=[HBM_SPEC] * n_thru + [SEM_SPEC, SEM_SPEC, ANY_SPEC],
        out_specs=[HBM_SPEC] * n_thru,
        input_output_aliases={i: i for i in range(n_thru)},
        compiler_params=pltpu.CompilerParams(has_side_effects=DATAFLOW),
    )(*thru, send_sems, recv_sems, after)
    return res[:K], (res[:K] if same else res[K:])


def _own_half(ref, r, pchip):
    x, y, c = _place()
    return _half(ref.at[2 * x + y], c, ref.shape[1] // 2)


def _their_half(ref, r, pchip):
    _, _, c = _place()
    return _half(ref.at[pchip], c, ref.shape[1] // 2)


def gather_start(lands, name, after=None):
    return _exchange_start(lands, None, _own_half, _own_half, name, after)


def gather_wait(handle, after, name):
    ssem, rsem, lands, _, _ = handle
    return _exchange_wait(ssem, rsem, lands, None, after, _own_half, _their_half, name)[1]


def pair_forward(lands):
    K = len(lands)

    def body(*refs):
        ins, outs = refs[:K], refs[K:2 * K]
        send_sems, recv_sems = refs[2 * K:]
        x, y, c = _place()
        sibling = (x, y, 1 - c)
        started = []
        for k in range(K):
            rh = ins[k].shape[1] // 2
            for r, (pchip, _, _) in enumerate(_other_chips(x, y)):
                cp = pltpu.make_async_remote_copy(
                    src_ref=_half(ins[k].at[pchip], c, rh), dst_ref=_half(outs[k].at[pchip], c, rh),
                    send_sem=send_sems.at[k, r], recv_sem=recv_sems.at[k, r], device_id=sibling, device_id_type=MESH)
                cp.start()
                started.append(cp)
        for k in range(K):
            rh = ins[k].shape[1] // 2
            for r, (pchip, _, _) in enumerate(_other_chips(x, y)):
                theirs = _half(outs[k].at[pchip], 1 - c, rh)
                pltpu.make_async_remote_copy(
                    src_ref=theirs, dst_ref=theirs, send_sem=send_sems.at[k, r], recv_sem=recv_sems.at[k, r],
                    device_id=sibling, device_id_type=MESH).wait_recv()
        for cp in started:
            cp.wait_send()

    return pl.pallas_call(
        body, name="pair_forward",
        out_shape=[jax.ShapeDtypeStruct(s.shape, s.dtype) for s in lands],
        in_specs=[HBM_SPEC] * K, out_specs=[HBM_SPEC] * K, input_output_aliases={k: k for k in range(K)},
        scratch_shapes=[pltpu.SemaphoreType.DMA((K, 3))] * 2,
    )(*lands)


def _to_chip(ref, r, pchip):
    return ref.at[pchip]


def _from_relation(ref, r, pchip):
    return ref.at[r]


def reduce_start(grads, c_idx, name):
    recv = pair_exchange(grads)
    parts = [pair_add(g, r, c_idx) for g, r in zip(grads, recv)]
    lands = [lax.empty((3,) + p.shape[1:], p.dtype) for p in parts]
    return _exchange_start(parts, lands, _to_chip, _from_relation, name)


def reduce_finish(handle, after, where, name):
    ssem, rsem, parts, lands, _ = handle
    parts, got = _exchange_wait(ssem, rsem, parts, lands, after, _to_chip, _from_relation, name)
    return pair_share([chip_sum(p, g, where) for p, g in zip(parts, got)])


def _pack(arrs):
    flat = jnp.concatenate([a.reshape(-1).astype(F32) for a in arrs])
    pad = (-flat.shape[0]) % (8 * LANES)
    return jnp.pad(flat, (0, pad)).reshape(-1, LANES)


def _unpack(flat, shapes):
    out, off = [], 0
    for s in shapes:
        n = 1
        for d in s:
            n *= d
        out.append(flat[off:off + n].reshape(s))
        off += n
    return out


def _adamw_any(w, g, m, v, name):
    shp = w.shape
    C = shp[-1]
    d, nm, nv = adamw(w.reshape(-1, C), g.reshape(-1, C), m.reshape(-1, C), v.reshape(-1, C), name)
    return d.reshape(shp), nm.reshape(shp), nv.reshape(shp)


def kernel(x, c, norm_g, w_ada, b_ada, w_ffn_in, w_ffn_out, cm_w_glu, cm_b_glu, cm_w_dw, cm_b_dw, cm_ln_g, cm_ln_b, cm_w_pw, cm_b_pw, dn_w_in, dn_w_sconv, dn_a_log, dn_dt_bias, dn_o_g, dn_w_out, final_g, loss_target, m_norm_g, m_w_ada, m_b_ada, m_w_ffn_in, m_w_ffn_out, m_cm_w_glu, m_cm_b_glu, m_cm_w_dw, m_cm_b_dw, m_cm_ln_g, m_cm_ln_b, m_cm_w_pw, m_cm_b_pw, m_dn_w_in, m_dn_w_sconv, m_dn_a_log, m_dn_dt_bias, m_dn_o_g, m_dn_w_out, m_final_g, v_norm_g, v_w_ada, v_b_ada, v_w_ffn_in, v_w_ffn_out, v_cm_w_glu, v_cm_b_glu, v_cm_w_dw, v_cm_b_dw, v_cm_ln_g, v_cm_ln_b, v_cm_w_pw, v_cm_b_pw, v_dn_w_in, v_dn_w_sconv, v_dn_a_log, v_dn_dt_bias, v_dn_o_g, v_dn_w_out, v_final_g):
    weights = dict(norm_g=norm_g, w_ada=w_ada, b_ada=b_ada, w_ffn_in=w_ffn_in, w_ffn_out=w_ffn_out, cm_w_glu=cm_w_glu,
                   cm_b_glu=cm_b_glu, cm_w_dw=cm_w_dw, cm_b_dw=cm_b_dw, cm_ln_g=cm_ln_g, cm_ln_b=cm_ln_b, cm_w_pw=cm_w_pw,
                   cm_b_pw=cm_b_pw, dn_w_in=dn_w_in, dn_w_sconv=dn_w_sconv, dn_a_log=dn_a_log, dn_dt_bias=dn_dt_bias,
                   dn_o_g=dn_o_g, dn_w_out=dn_w_out, final_g=final_g)
    mom_m = dict(norm_g=m_norm_g, w_ada=m_w_ada, b_ada=m_b_ada, w_ffn_in=m_w_ffn_in, w_ffn_out=m_w_ffn_out,
                 cm_w_glu=m_cm_w_glu, cm_b_glu=m_cm_b_glu, cm_w_dw=m_cm_w_dw, cm_b_dw=m_cm_b_dw, cm_ln_g=m_cm_ln_g,
                 cm_ln_b=m_cm_ln_b, cm_w_pw=m_cm_w_pw, cm_b_pw=m_cm_b_pw, dn_w_in=m_dn_w_in, dn_w_sconv=m_dn_w_sconv,
                 dn_a_log=m_dn_a_log, dn_dt_bias=m_dn_dt_bias, dn_o_g=m_dn_o_g, dn_w_out=m_dn_w_out, final_g=m_final_g)
    mom_v = dict(norm_g=v_norm_g, w_ada=v_w_ada, b_ada=v_b_ada, w_ffn_in=v_w_ffn_in, w_ffn_out=v_w_ffn_out,
                 cm_w_glu=v_cm_w_glu, cm_b_glu=v_cm_b_glu, cm_w_dw=v_cm_w_dw, cm_b_dw=v_cm_b_dw, cm_ln_g=v_cm_ln_g,
                 cm_ln_b=v_cm_ln_b, cm_w_pw=v_cm_w_pw, cm_b_pw=v_cm_b_pw, dn_w_in=v_dn_w_in, dn_w_sconv=v_dn_w_sconv,
                 dn_a_log=v_dn_a_log, dn_dt_bias=v_dn_dt_bias, dn_o_g=v_dn_o_g, dn_w_out=v_dn_w_out, final_g=v_final_g)
    names = list(weights)

    BL, T, D = x.shape
    L = norm_g.shape[0]
    NB = BL * N_DEV
    Ca = w_ada.shape[2]
    C9 = b_ada.shape[1]
    H = dn_a_log.shape[1]
    Dh = dn_o_g.shape[1]
    W = H * Dh
    KC = cm_w_dw.shape[1]
    KS = dn_w_sconv.shape[1]
    n_cm, n_dn = cm_w_glu.shape[0], dn_w_in.shape[0]
    ax, ay, ac = lax.axis_index("x"), lax.axis_index("y"), lax.axis_index("c")
    chip = 2 * ax + ay
    dev = 2 * chip + ac
    c_idx = ac.astype(jnp.int32).reshape(1)
    where = jnp.stack([chip, ac]).astype(jnp.int32)

    small_in = [c, norm_g, cm_w_dw, dn_w_sconv]
    packed = _pack(small_in)
    gathered = allgather8(packed).reshape(N_DEV, -1)
    per_dev = [_unpack(gathered[d], [a.shape for a in small_in]) for d in range(N_DEV)]
    c_all = jnp.concatenate([p[0] for p in per_dev], axis=0)
    norm_g_full = jnp.concatenate([per_dev[2 * s][1] for s in range(N_CHIPS)], axis=-1)
    w_dw_full = jnp.concatenate([per_dev[2 * s][2] for s in range(N_CHIPS)], axis=-1)
    w_sconv_full = jnp.concatenate([per_dev[2 * s][3] for s in range(N_CHIPS)], axis=-1)

    b_cols = lax.dynamic_slice_in_dim(b_ada, chip * Ca, Ca, axis=1).reshape(L, 1, Ca)
    mod_part = ada_fwd(c_all, w_ada, b_cols)
    mod_g = allgather8(mod_part.reshape(-1, LANES)).reshape(N_DEV, L, NB, Ca)
    mod_all = jnp.concatenate([mod_g[2 * s] for s in range(N_CHIPS)], axis=-1)
    mod = lax.dynamic_slice_in_dim(mod_all, dev * BL, BL, axis=1).reshape(L, BL, 9, D)

    def layer_shards(i):
        sh = [w_ffn_in[i, 0], w_ffn_in[i, 1], w_ffn_out[i, 0], w_ffn_out[i, 1]]
        if i % 2 == 0:
            sh += [cm_w_glu[i // 2], cm_w_pw[i // 2]]
        else:
            sh += [dn_w_in[i // 2], dn_w_out[i // 2]]
        return [lax.dynamic_update_slice(lax.empty((N_CHIPS,) + s.shape, BF16), s.astype(BF16)[None], (chip, 0, 0))
                for s in sh]

    lands = [layer_shards(i) for i in range(L)]
    wts = [None] * L
    handle = gather_start(lands[0], "gather_start_0")

    def dn_weights(i):
        full = jnp.transpose(wts[i][4], (1, 0, 2)).reshape(D, -1)
        return full[:, :4 * W], jnp.pad(full[:, 4 * W:], ((0, 0), (0, LANES - 2 * H)))

    def row128(v):
        return jnp.pad(v.reshape(1, -1), ((0, 0), (0, LANES - v.shape[-1])))

    def pad_taps(w):
        return jnp.pad(w, ((0, 1), (0, 0)))

    saved = []
    xs = x
    after = mod
    for i in range(L):
        wl = wts[i] = pair_forward(gather_wait(handle, after, "gather_wait_%d" % i))
        tok = 0.0
        if i + 1 < L:
            handle = gather_start(lands[i + 1], "gather_start_%d" % (i + 1), wl[0])
            tok = handle[4][0, 0]
        sv = {}
        m3 = [mod[i, :, 3 * j:3 * j + 3] + tok for j in range(3)]
        gs = [norm_g_full[i, j].reshape(1, D) for j in range(3)]
        sv["x0"] = xs
        xs, sv["y0"] = ffn_fwd(xs, m3[0], gs[0], wl[0], wl[2])
        sv["x1"] = xs
        if i % 2 == 0:
            a = i // 2
            sv["u"] = conv_glu_fwd(xs, m3[1], gs[1], wl[4], cm_b_glu[a].reshape(1, -1))
            xs, sv["y1"], sv["u2"] = conv_out_fwd(
                xs, sv["u"], m3[1], pad_taps(w_dw_full[a]), cm_b_dw[a].reshape(1, D), cm_ln_g[a].reshape(1, D),
                cm_ln_b[a].reshape(1, D), wl[5].reshape(D, D), cm_b_pw[a].reshape(1, D))
        else:
            a = i // 2
            w_main, w_ab = dn_weights(i)
            sv["pre"], sv["z"], sv["ab"] = dn_proj_fwd(xs, m3[1], gs[1], w_main, w_ab)
            qkvgb = dn_conv_fwd(sv["pre"], sv["ab"], w_sconv_full[a], row128(dn_a_log[a]), row128(dn_dt_bias[a]), H)
            sv["qkvgb"] = qkvgb
            sv["o"], sv["sp"] = dn_chunk_fwd(*qkvgb)
            xs, sv["y1"] = dn_out_fwd(xs, sv["o"], sv["z"], m3[1], dn_o_g[a].reshape(1, Dh), wl[5].reshape(W, D))
        sv["x2"] = xs
        xs, sv["y2"] = ffn_fwd(xs, m3[2], gs[2], wl[1], wl[3])
        saved.append(sv)
        after = xs

    dx, d_final_g, loss_part = final_loss(xs, final_g.reshape(1, D), loss_target)

    g_small = {n: None for n in names}
    d_norm_g = [[None] * 3 for _ in range(L)]
    dmod = [[None] * 3 for _ in range(L)]
    g_cm = {k: [None] * n_cm for k in ("b_glu", "w_dw", "b_dw", "ln_g", "ln_b", "b_pw")}
    g_dn = {k: [None] * n_dn for k in ("w_sconv", "a_log", "dt_bias", "o_g")}
    big = [None] * L

    def ffn_back(i, j, slot, dx, tok=0.0):
        wl, sv = wts[i], saved[i]
        m3 = mod[i, :, 3 * j:3 * j + 3] + tok
        g = norm_g_full[i, j].reshape(1, D)
        dx, hb, ab_, dgu, dyb, dm, dg = ffn_bwd(sv["x%d" % j], dx, sv["y%d" % j], m3, g, wl[slot], wl[2 + slot])
        dmod[i][j] = dm
        d_norm_g[i][j] = jnp.sum(dg, axis=(0, 1))
        Fc = wl[slot].shape[2]
        dw_in = matmul_tn(hb.reshape(-1, D), dgu.reshape(2, BL * T, 2 * Fc), Fc, "dw_ffn_in")
        dw_out = matmul_tn(ab_.reshape(-1, 2 * Fc), dyb.reshape(1, -1, D), D, "dw_ffn_out")
        return dx, dw_in, dw_out.reshape(N_CHIPS, -1, D)

    pending, tok = None, 0.0
    for i in reversed(range(L)):
        wl, sv = wts[i], saved[i]
        a = i // 2
        dx, dw_in1, dw_out1 = ffn_back(i, 2, 1, dx, tok)
        m3 = mod[i, :, 3:6]
        g = norm_g_full[i, 1].reshape(1, D)
        if i % 2 == 0:
            w_pw = wl[5].reshape(D, D)
            wdw = pad_taps(w_dw_full[a])
            du2, u3b, dyb, dgate, vec = conv_out_bwd(dx, sv["y1"], sv["u2"], m3, cm_ln_g[a].reshape(1, D),
                                                     cm_ln_b[a].reshape(1, D), w_pw)
            dx, hb, dab, dwdw, dbglu, dm, dg = conv_glu_bwd(sv["x1"], dx, du2, sv["u"], m3, g, wl[4],
                                                            cm_b_glu[a].reshape(1, -1), wdw)
            dm = dm.at[:, 2:3, :].set(dgate)
            vec = jnp.sum(vec, axis=0)
            g_cm["b_pw"][a], g_cm["ln_g"][a], g_cm["ln_b"][a], g_cm["b_dw"][a] = vec[0], vec[1], vec[2], vec[3]
            g_cm["w_dw"][a] = jnp.sum(dwdw, axis=0)[:KC]
            g_cm["b_glu"][a] = jnp.sum(dbglu, axis=(0, 1))
            dw_a = matmul_tn(hb.reshape(-1, D), dab.reshape(1, -1, 2 * D), D // 2, "dw_glu")
            dw_b = matmul_tn(u3b.reshape(-1, D), dyb.reshape(1, -1, D), D, "dw_sq").reshape(N_CHIPS, -1, D)
        else:
            w_main, w_ab = dn_weights(i)
            w_out = wl[5].reshape(W, D)
            do, dz, ogb, dyb, dgate, dog = dn_out_bwd(dx, sv["y1"], sv["o"], sv["z"], m3, dn_o_g[a].reshape(1, Dh), w_out)
            dq, dk, dv, dgb, dbb = dn_chunk_bwd(*sv["qkvgb"], sv["sp"], do)
            dc, dab, small = dn_conv_bwd(dq, dk, dv, dgb, dbb, sv["pre"], sv["ab"], w_sconv_full[a],
                                         row128(dn_a_log[a]), row128(dn_dt_bias[a]))
            dx, hb, dproj, dws, dm, dg = dn_proj_bwd(sv["x1"], dx, dc, sv["pre"], dz, dab, m3, g, w_main, w_ab,
                                                     w_sconv_full[a])
            dm = dm.at[:, 2:3, :].set(dgate)
            small = jnp.sum(small, axis=0)
            g_dn["a_log"][a], g_dn["dt_bias"][a] = small[0, :H], small[1, :H]
            g_dn["o_g"][a] = jnp.sum(dog, axis=(0, 1))
            g_dn["w_sconv"][a] = jnp.sum(dws, axis=0)
            dw_main = matmul_tn(hb.reshape(-1, D), dproj.reshape(1, -1, 4 * W), W, "dw_dn_main")
            dw_ab = matmul_tn(hb.reshape(-1, D), dab.reshape(1, -1, LANES), LANES, "dw_dn_ab")
            full = jnp.concatenate([jnp.transpose(dw_main, (1, 0, 2)).reshape(D, 4 * W), dw_ab[0][:, :2 * H]], axis=1)
            dw_a = jnp.transpose(full.reshape(D, N_CHIPS, -1), (1, 0, 2))
            dw_b = matmul_tn(ogb.reshape(-1, W), dyb.reshape(1, -1, D), D, "dw_sq").reshape(N_CHIPS, -1, D)
        dmod[i][1] = dm
        d_norm_g[i][1] = jnp.sum(dg, axis=(0, 1))
        dx, dw_in0, dw_out0 = ffn_back(i, 0, 0, dx)
        started = reduce_start([dw_in0, dw_in1, dw_out0, dw_out1, dw_a, dw_b], c_idx, "reduce_start_%d" % i)
        if pending is not None:
            big[pending[1]] = reduce_finish(pending[0], dx, where, "reduce_wait_%d" % pending[1])
        pending, tok = (started, i), started[4][0, 0]

    part = dict(
        norm_g=jnp.stack([jnp.stack(r) for r in d_norm_g]),
        cm_b_glu=jnp.stack(g_cm["b_glu"]), cm_w_dw=jnp.stack(g_cm["w_dw"]), cm_b_dw=jnp.stack(g_cm["b_dw"]),
        cm_ln_g=jnp.stack(g_cm["ln_g"]), cm_ln_b=jnp.stack(g_cm["ln_b"]), cm_b_pw=jnp.stack(g_cm["b_pw"]),
        dn_w_sconv=jnp.stack(g_dn["w_sconv"]), dn_a_log=jnp.stack(g_dn["a_log"]), dn_dt_bias=jnp.stack(g_dn["dt_bias"]),
        dn_o_g=jnp.stack(g_dn["o_g"]), final_g=jnp.sum(d_final_g, axis=(0, 1)),
        loss=jnp.sum(loss_part[:, 0, 0]).reshape(1))
    dmod_loc = jnp.stack([jnp.concatenate(r, axis=1) for r in dmod]).reshape(L, BL, C9)
    keys = list(part)
    packed = _pack([part[k] for k in keys] + [dmod_loc]) + tok
    R = packed.shape[0]
    gathered = allgather8(packed).reshape(N_DEV, R, LANES)
    summed = _unpack(sum_devices(gathered).reshape(-1), [part[k].shape for k in keys])
    tot = dict(zip(keys, summed))
    n_small = sum(int(part[k].size) for k in keys)
    dmod_all = gathered.reshape(N_DEV, -1)[:, n_small:n_small + L * BL * C9].reshape(N_DEV, L, BL, C9)
    dmod_all = jnp.transpose(dmod_all, (1, 0, 2, 3)).reshape(L, NB, C9)
    dmod_cols = lax.dynamic_slice_in_dim(dmod_all, chip * Ca, Ca, axis=2)
    g_w_ada, g_b_ada = ada_bwd(c_all, dmod_cols, dmod_all)
    delta, new_m, new_v = {}, {}, {}
    delta["w_ada"], new_m["w_ada"], new_v["w_ada"] = _adamw_any(w_ada, g_w_ada, m_w_ada, v_w_ada, "adamw_w_ada")
    big[pending[1]] = reduce_finish(pending[0], new_v["w_ada"], where, "reduce_wait_%d" % pending[1])

    def my_cols(full):
        n = full.shape[-1] // N_CHIPS
        return lax.dynamic_slice_in_dim(full, chip * n, n, axis=full.ndim - 1)

    grads = dict(
        norm_g=my_cols(tot["norm_g"]), w_ada=g_w_ada, b_ada=g_b_ada.reshape(L, C9),
        w_ffn_in=jnp.stack([jnp.stack([big[i][0], big[i][1]]) for i in range(L)]),
        w_ffn_out=jnp.stack([jnp.stack([big[i][2], big[i][3]]) for i in range(L)]),
        cm_w_glu=jnp.stack([big[i][4] for i in range(0, L, 2)]), cm_b_glu=tot["cm_b_glu"], cm_w_dw=my_cols(tot["cm_w_dw"]),
        cm_b_dw=tot["cm_b_dw"], cm_ln_g=tot["cm_ln_g"], cm_ln_b=tot["cm_ln_b"],
        cm_w_pw=jnp.stack([big[i][5] for i in range(0, L, 2)]), cm_b_pw=tot["cm_b_pw"],
        dn_w_in=jnp.stack([big[i][4] for i in range(1, L, 2)]), dn_w_sconv=my_cols(tot["dn_w_sconv"]),
        dn_a_log=tot["dn_a_log"], dn_dt_bias=tot["dn_dt_bias"], dn_o_g=tot["dn_o_g"],
        dn_w_out=jnp.stack([big[i][5] for i in range(1, L, 2)]), final_g=tot["final_g"])

    large = ("w_ada", "w_ffn_in", "w_ffn_out", "cm_w_glu", "cm_w_pw", "dn_w_in", "dn_w_out")
    for n in large[1:]:
        delta[n], new_m[n], new_v[n] = _adamw_any(weights[n], grads[n], mom_m[n], mom_v[n], "adamw_" + n)
    rest = [n for n in names if n not in large]
    shapes = [weights[n].shape for n in rest]
    pd, pm, pv = adamw(_pack([weights[n] for n in rest]), _pack([grads[n] for n in rest]),
                       _pack([mom_m[n] for n in rest]), _pack([mom_v[n] for n in rest]), "adamw_small")
    for n, d_, m_, v_ in zip(rest, _unpack(pd.reshape(-1), shapes), _unpack(pm.reshape(-1), shapes),
                             _unpack(pv.reshape(-1), shapes)):
        delta[n], new_m[n], new_v[n] = d_, m_, v_

    return (tot["loss"].reshape(()), dx, *[grads[n] for n in names], *[delta[n] for n in names],
            *[new_m[n] for n in names], *[new_v[n] for n in names])
```

```python
import functools

import jax
import jax.numpy as jnp
from jax import lax
from jax.experimental import pallas as pl
from jax.experimental.pallas import tpu as pltpu

F32 = jnp.float32
BF16 = jnp.bfloat16
EPS = 1e-6
CHUNK = 64
CHUNKS_PER_STEP = 4
N_CHIPS = 4
N_DEV = 8
LANES = 128
CONV_HALO = 32
SCONV_HALO = 8
VMEM_LIMIT_V7X = 60 * 1024 * 1024
HI = lax.Precision.HIGHEST
MESH = pl.DeviceIdType.MESH
HBM_SPEC = pl.BlockSpec(memory_space=pltpu.HBM)

ADAM_LR, ADAM_B1, ADAM_B2, ADAM_EPS, ADAM_WD, ADAM_STEP = 0.001, 0.9, 0.999, 1e-08, 0.01, 10


def _cparams(n_axes):
    return pltpu.CompilerParams(dimension_semantics=("arbitrary",) * n_axes, vmem_limit_bytes=VMEM_LIMIT_V7X)


def _tile(n, pref, mult=8):
    for t in range(min(n, pref) // mult * mult, 0, -mult):
        if n % t == 0:
            return t
    return n


def _mm(a, b):
    return lax.dot_general(a.astype(BF16), b.astype(BF16), (((1,), (0,)), ((), ())), preferred_element_type=F32)


def _mm_nt(a, b):
    return lax.dot_general(a.astype(BF16), b.astype(BF16), (((1,), (1,)), ((), ())), preferred_element_type=F32)


def _mm_tn(a, b):
    return lax.dot_general(a.astype(BF16), b.astype(BF16), (((0,), (0,)), ((), ())), preferred_element_type=F32)


def _sigmoid(x):
    return jax.nn.sigmoid(x)


def _dsilu(x, s):
    return s * (1.0 + x * (1.0 - s))


def _softplus(x):
    return jnp.maximum(x, 0.0) + jnp.log(1.0 + jnp.exp(-jnp.abs(x)))


def _modnorm(x, g, scale, shift):
    r = lax.rsqrt(jnp.mean(x * x, axis=-1, keepdims=True) + EPS)
    return (x * r) * g * (1.0 + scale) + shift


def _modnorm_bwd(x, g, scale, dh):
    r = lax.rsqrt(jnp.mean(x * x, axis=-1, keepdims=True) + EPS)
    xn = x * r
    dshift = jnp.sum(dh, axis=0, keepdims=True)
    dscale = jnp.sum(dh * (xn * g), axis=0, keepdims=True)
    dhn = dh * (1.0 + scale)
    dg = jnp.sum(dhn * xn, axis=0, keepdims=True)
    dxn = dhn * g
    dx = r * (dxn - xn * jnp.mean(dxn * xn, axis=-1, keepdims=True))
    return dx, dg, dscale, dshift


def _sum0(a):
    return jnp.sum(a, axis=0, keepdims=True)


def ffn_fwd(x, mod3, g, w_in, w_out):
    B, T, D = x.shape
    Fc = w_in.shape[2]
    w_in = w_in.reshape(2, 2, D, Fc)
    w_out = w_out.reshape(2, Fc, D)
    tm = _tile(T, 512)

    def body(x_ref, mod_ref, g_ref, wi_ref, wo_ref, xo_ref, y_ref, h_s, acc_s):
        f = pl.program_id(2)

        @pl.when(f == 0)
        def _():
            h = _modnorm(x_ref[...], g_ref[...], mod_ref[1:2, :], mod_ref[0:1, :])
            h_s[...] = h.astype(BF16)
            acc_s[...] = jnp.zeros_like(acc_s)

        h = h_s[...]
        gt = _mm(h, wi_ref[0])
        up = _mm(h, wi_ref[1])
        a = gt * _sigmoid(gt) * up
        acc_s[...] += _mm(a, wo_ref[...])

        @pl.when(f == 1)
        def _():
            y = acc_s[...]
            y_ref[...] = y
            xo_ref[...] = x_ref[...] + 0.5 * (1.0 + mod_ref[2:3, :]) * y

    tok = pl.BlockSpec((None, tm, D), lambda b, t, f: (b, t, 0))
    return pl.pallas_call(
        body, name="ffn_fwd", grid=(B, T // tm, 2),
        in_specs=[tok,
                  pl.BlockSpec((None, 3, D), lambda b, t, f: (b, 0, 0)),
                  pl.BlockSpec((1, D), lambda b, t, f: (0, 0)),
                  pl.BlockSpec((2, None, D, Fc), lambda b, t, f: (0, f, 0, 0)),
                  pl.BlockSpec((None, Fc, D), lambda b, t, f: (f, 0, 0))],
        out_specs=[tok, tok],
        out_shape=[jax.ShapeDtypeStruct((B, T, D), F32)] * 2,
        scratch_shapes=[pltpu.VMEM((tm, D), BF16), pltpu.VMEM((tm, D), F32)],
        compiler_params=_cparams(3),
    )(x, mod3, g, w_in, w_out)


def ffn_bwd(x, dres, y, mod3, g, w_in, w_out):
    B, T, D = x.shape
    Fc = w_in.shape[2]
    F = 2 * Fc
    w_in = w_in.reshape(2, 2, D, Fc)
    w_out = w_out.reshape(2, Fc, D)
    tm = _tile(T, 256)

    def body(x_ref, dres_ref, y_ref, mod_ref, g_ref, wi_ref, wo_ref,
             dx_ref, h_ref, a_ref, dgu_ref, dy_ref, dmod_ref, dg_ref, h_s, dy_s, dh_s):
        t, f = pl.program_id(1), pl.program_id(2)

        @pl.when(f == 0)
        def _():
            h = _modnorm(x_ref[...], g_ref[...], mod_ref[1:2, :], mod_ref[0:1, :]).astype(BF16)
            h_s[...] = h
            h_ref[...] = h
            dres = dres_ref[...]
            dy = (0.5 * (1.0 + mod_ref[2:3, :]) * dres).astype(BF16)
            dy_s[...] = dy
            dy_ref[...] = dy
            dh_s[...] = jnp.zeros_like(dh_s)
            dgate = _sum0(dres * (0.5 * y_ref[...]))

            @pl.when(t == 0)
            def _():
                dmod_ref[...] = jnp.zeros_like(dmod_ref)
                dg_ref[...] = jnp.zeros_like(dg_ref)

            dmod_ref[2:3, :] += dgate

        h = h_s[...]
        dy = dy_s[...]
        gt = _mm(h, wi_ref[0])
        up = _mm(h, wi_ref[1])
        sg = _sigmoid(gt)
        silu = gt * sg
        a_ref[...] = (silu * up).astype(BF16)
        da = _mm_nt(dy, wo_ref[...])
        dup = (da * silu).astype(BF16)
        dgt = (da * up * _dsilu(gt, sg)).astype(BF16)
        dgu_ref[0] = dgt
        dgu_ref[1] = dup
        dh_s[...] += _mm_nt(dgt, wi_ref[0]) + _mm_nt(dup, wi_ref[1])

        @pl.when(f == 1)
        def _():
            dxn, dg, dscale, dshift = _modnorm_bwd(x_ref[...], g_ref[...], mod_ref[1:2, :], dh_s[...])
            dx_ref[...] = dres_ref[...] + dxn
            dmod_ref[0:1, :] += dshift
            dmod_ref[1:2, :] += dscale
            dg_ref[...] += dg

    tok = pl.BlockSpec((None, tm, D), lambda b, t, f: (b, t, 0))
    per_b3 = pl.BlockSpec((None, 3, D), lambda b, t, f: (b, 0, 0))
    return pl.pallas_call(
        body, name="ffn_bwd", grid=(B, T // tm, 2),
        in_specs=[tok, tok, tok, per_b3,
                  pl.BlockSpec((1, D), lambda b, t, f: (0, 0)),
                  pl.BlockSpec((2, None, D, Fc), lambda b, t, f: (0, f, 0, 0)),
                  pl.BlockSpec((None, Fc, D), lambda b, t, f: (f, 0, 0))],
        out_specs=[tok, tok,
                   pl.BlockSpec((None, tm, Fc), lambda b, t, f: (b, t, f)),
                   pl.BlockSpec((2, None, tm, Fc), lambda b, t, f: (0, b, t, f)),
                   tok, per_b3,
                   pl.BlockSpec((None, 1, D), lambda b, t, f: (b, 0, 0))],
        out_shape=[jax.ShapeDtypeStruct((B, T, D), F32), jax.ShapeDtypeStruct((B, T, D), BF16),
                   jax.ShapeDtypeStruct((B, T, F), BF16), jax.ShapeDtypeStruct((2, B, T, F), BF16),
                   jax.ShapeDtypeStruct((B, T, D), BF16), jax.ShapeDtypeStruct((B, 3, D), F32),
                   jax.ShapeDtypeStruct((B, 1, D), F32)],
        scratch_shapes=[pltpu.VMEM((tm, D), BF16), pltpu.VMEM((tm, D), BF16), pltpu.VMEM((tm, D), F32)],
        compiler_params=_cparams(3),
    )(x, dres, y, mod3, g, w_in, w_out)


def matmul_tn(xm, ym, bm, name):
    N, K = xm.shape
    GY, _, MY = ym.shape
    per = MY // bm
    nb = GY * per
    tn = _tile(N, 512)

    def body(x_ref, y_ref, o_ref, acc_s):
        n = pl.program_id(1)

        @pl.when(n == 0)
        def _():
            acc_s[...] = jnp.zeros_like(acc_s)

        acc_s[...] += _mm_tn(x_ref[...], y_ref[...])

        @pl.when(n == N // tn - 1)
        def _():
            o_ref[...] = acc_s[...].astype(BF16)

    return pl.pallas_call(
        body, name=name, grid=(nb, N // tn),
        in_specs=[pl.BlockSpec((tn, K), lambda m, n: (n, 0)),
                  pl.BlockSpec((None, tn, bm), lambda m, n: (m // per, n, m % per))],
        out_specs=pl.BlockSpec((None, K, bm), lambda m, n: (m, 0, 0)),
        out_shape=jax.ShapeDtypeStruct((nb, K, bm), BF16),
        scratch_shapes=[pltpu.VMEM((K, bm), F32)],
        compiler_params=_cparams(2),
    )(xm, ym)


def final_loss(x, fg, target):
    B, T, D = x.shape
    tm = _tile(T, 512)

    def body(x_ref, g_ref, t_ref, dx_ref, dfg_ref, loss_ref):
        t = pl.program_id(1)

        @pl.when(t == 0)
        def _():
            dfg_ref[...] = jnp.zeros_like(dfg_ref)
            loss_ref[...] = jnp.zeros_like(loss_ref)

        xv = x_ref[...]
        g = g_ref[...]
        r = lax.rsqrt(jnp.mean(xv * xv, axis=-1, keepdims=True) + EPS)
        xn = xv * r
        err = xn * g - t_ref[...]
        tok_loss = jnp.mean(err * err, axis=-1, keepdims=True)
        loss_ref[...] += 0.5 * jnp.sum(tok_loss, axis=0, keepdims=True)
        dy = err * (1.0 / D)
        dfg_ref[...] += _sum0(dy * xn)
        dxn = dy * g
        dx_ref[...] = r * (dxn - xn * jnp.mean(dxn * xn, axis=-1, keepdims=True))

    tok = pl.BlockSpec((None, tm, D), lambda b, t: (b, t, 0))
    return pl.pallas_call(
        body, name="final_loss", grid=(B, T // tm),
        in_specs=[tok, pl.BlockSpec((1, D), lambda b, t: (0, 0)), tok],
        out_specs=[tok, pl.BlockSpec((None, 1, D), lambda b, t: (b, 0, 0)),
                   pl.BlockSpec((None, 1, LANES), lambda b, t: (b, 0, 0))],
        out_shape=[jax.ShapeDtypeStruct((B, T, D), F32), jax.ShapeDtypeStruct((B, 1, D), F32),
                   jax.ShapeDtypeStruct((B, 1, LANES), F32)],
        compiler_params=_cparams(2),
    )(x, fg, target)


def _past_halo_spec(tm, halo, width):
    return pl.BlockSpec((None, halo, width), lambda b, t: (b, jnp.maximum(t * (tm // halo) - 1, 0), 0))


def _future_halo_spec(tm, halo, width, T):
    return pl.BlockSpec((None, halo, width), lambda b, t: (b, jnp.minimum((t + 1) * (tm // halo), T // halo - 1), 0))


def _glu_fwd(h, w_ref, bias):
    D = h.shape[1]
    a = jnp.concatenate([_mm(h, w_ref[0]), _mm(h, w_ref[1])], axis=1) + bias[:, :D]
    b = jnp.concatenate([_mm(h, w_ref[2]), _mm(h, w_ref[3])], axis=1) + bias[:, D:]
    return a, b


def conv_glu_fwd(x, mod3, g, w_glu, b_glu):
    B, T, D = x.shape
    tm = _tile(T, 512)

    def body(x_ref, mod_ref, g_ref, w_ref, b_ref, u_ref):
        h = _modnorm(x_ref[...], g_ref[...], mod_ref[1:2, :], mod_ref[0:1, :]).astype(BF16)
        a, b = _glu_fwd(h, w_ref, b_ref[...])
        u_ref[...] = a * _sigmoid(b)

    tok = pl.BlockSpec((None, tm, D), lambda b, t: (b, t, 0))
    return pl.pallas_call(
        body, name="conv_glu_fwd", grid=(B, T // tm),
        in_specs=[tok, pl.BlockSpec((None, 3, D), lambda b, t: (b, 0, 0)),
                  pl.BlockSpec((1, D), lambda b, t: (0, 0)),
                  pl.BlockSpec((4, D, D // 2), lambda b, t: (0, 0, 0)),
                  pl.BlockSpec((1, 2 * D), lambda b, t: (0, 0))],
        out_specs=tok, out_shape=jax.ShapeDtypeStruct((B, T, D), F32),
        compiler_params=_cparams(2),
    )(x, mod3, g, w_glu, b_glu)


def _layer_norm_parts(u2):
    mu = jnp.mean(u2, axis=-1, keepdims=True)
    xc = u2 - mu
    rs = lax.rsqrt(jnp.mean(xc * xc, axis=-1, keepdims=True) + EPS)
    return xc * rs, rs


def conv_out_fwd(x, u, mod3, w_dw, b_dw, ln_g, ln_b, w_pw, b_pw):
    B, T, D = x.shape
    K = w_dw.shape[0] - 1
    tm = _tile(T, 512)

    def body(x_ref, u_ref, halo_ref, mod_ref, wdw_ref, bdw_ref, lg_ref, lb_ref, wpw_ref, bpw_ref,
             xo_ref, y_ref, u2_ref, ext_s):
        t = pl.program_id(1)
        ext_s[0:CONV_HALO, :] = jnp.where(t > 0, halo_ref[...], 0.0)
        ext_s[CONV_HALO:, :] = u_ref[...]
        acc = jnp.broadcast_to(bdw_ref[...], (tm, D))
        for k in range(K):
            acc = acc + wdw_ref[k:k + 1, :] * ext_s[pl.ds(CONV_HALO - (K - 1) + k, tm), :]
        u2_ref[...] = acc
        xh, _ = _layer_norm_parts(acc)
        l = xh * lg_ref[...] + lb_ref[...]
        u3 = l * _sigmoid(l)
        y = _mm(u3, wpw_ref[...]) + bpw_ref[...]
        y_ref[...] = y
        xo_ref[...] = x_ref[...] + (1.0 + mod_ref[2:3, :]) * y

    tok = pl.BlockSpec((None, tm, D), lambda b, t: (b, t, 0))
    vec = pl.BlockSpec((1, D), lambda b, t: (0, 0))
    return pl.pallas_call(
        body, name="conv_out_fwd", grid=(B, T // tm),
        in_specs=[tok, tok, _past_halo_spec(tm, CONV_HALO, D), pl.BlockSpec((None, 3, D), lambda b, t: (b, 0, 0)),
                  pl.BlockSpec((K + 1, D), lambda b, t: (0, 0)), vec, vec, vec,
                  pl.BlockSpec((D, D), lambda b, t: (0, 0)), vec],
        out_specs=[tok, tok, tok], out_shape=[jax.ShapeDtypeStruct((B, T, D), F32)] * 3,
        scratch_shapes=[pltpu.VMEM((tm + CONV_HALO, D), F32)],
        compiler_params=_cparams(2),
    )(x, u, u, mod3, w_dw, b_dw, ln_g, ln_b, w_pw, b_pw)


def conv_out_bwd(dres, y, u2, mod3, ln_g, ln_b, w_pw):
    B, T, D = dres.shape
    tm = _tile(T, 512)

    def body(dres_ref, y_ref, u2_ref, mod_ref, lg_ref, lb_ref, wpw_ref, du2_ref, u3_ref, dy_ref, dgate_ref, vec_ref):
        t = pl.program_id(1)

        @pl.when(t == 0)
        def _():
            dgate_ref[...] = jnp.zeros_like(dgate_ref)
            vec_ref[...] = jnp.zeros_like(vec_ref)

        dres = dres_ref[...]
        dy = (1.0 + mod_ref[2:3, :]) * dres
        dy_ref[...] = dy.astype(BF16)
        dgate_ref[...] += _sum0(dres * y_ref[...])
        xh, rs = _layer_norm_parts(u2_ref[...])
        lg = lg_ref[...]
        l = xh * lg + lb_ref[...]
        sg = _sigmoid(l)
        u3_ref[...] = (l * sg).astype(BF16)
        du3 = _mm_nt(dy, wpw_ref[...])
        dl = du3 * _dsilu(l, sg)
        dxh = dl * lg
        du2 = rs * (dxh - jnp.mean(dxh, axis=-1, keepdims=True) - xh * jnp.mean(dxh * xh, axis=-1, keepdims=True))
        du2_ref[...] = du2
        vec_ref[0:1, :] += _sum0(dy)
        vec_ref[1:2, :] += _sum0(dl * xh)
        vec_ref[2:3, :] += _sum0(dl)
        vec_ref[3:4, :] += _sum0(du2)

    tok = pl.BlockSpec((None, tm, D), lambda b, t: (b, t, 0))
    tokb = pl.BlockSpec((None, tm, D), lambda b, t: (b, t, 0))
    vec = pl.BlockSpec((1, D), lambda b, t: (0, 0))
    return pl.pallas_call(
        body, name="conv_out_bwd", grid=(B, T // tm),
        in_specs=[tok, tok, tok, pl.BlockSpec((None, 3, D), lambda b, t: (b, 0, 0)), vec, vec,
                  pl.BlockSpec((D, D), lambda b, t: (0, 0))],
        out_specs=[tok, tokb, tokb, pl.BlockSpec((None, 1, D), lambda b, t: (b, 0, 0)),
                   pl.BlockSpec((None, 4, D), lambda b, t: (b, 0, 0))],
        out_shape=[jax.ShapeDtypeStruct((B, T, D), F32), jax.ShapeDtypeStruct((B, T, D), BF16),
                   jax.ShapeDtypeStruct((B, T, D), BF16), jax.ShapeDtypeStruct((B, 1, D), F32),
                   jax.ShapeDtypeStruct((B, 4, D), F32)],
        compiler_params=_cparams(2),
    )(dres, y, u2, mod3, ln_g, ln_b, w_pw)


def conv_glu_bwd(x, dres, du2, u, mod3, g, w_glu, b_glu, w_dw):
    B, T, D = x.shape
    K = w_dw.shape[0] - 1
    tm = _tile(T, 256)
    nt = T // tm

    def body(x_ref, dres_ref, du2_ref, du2h_ref, u_ref, uh_ref, mod_ref, g_ref, w_ref, b_ref, wdw_ref,
             dx_ref, h_ref, dab_ref, dwdw_ref, dbglu_ref, dmod_ref, dg_ref, extu_s, extd_s):
        t = pl.program_id(1)

        @pl.when(t == 0)
        def _():
            dwdw_ref[...] = jnp.zeros_like(dwdw_ref)
            dbglu_ref[...] = jnp.zeros_like(dbglu_ref)
            dmod_ref[...] = jnp.zeros_like(dmod_ref)
            dg_ref[...] = jnp.zeros_like(dg_ref)

        du2 = du2_ref[...]
        extu_s[0:CONV_HALO, :] = jnp.where(t > 0, uh_ref[...], 0.0)
        extu_s[CONV_HALO:, :] = u_ref[...]
        extd_s[0:tm, :] = du2
        extd_s[tm:, :] = jnp.where(t < nt - 1, du2h_ref[...], 0.0)
        du = jnp.zeros((tm, D), F32)
        for k in range(K):
            du = du + wdw_ref[k:k + 1, :] * extd_s[pl.ds(K - 1 - k, tm), :]
            dwdw_ref[k:k + 1, :] += _sum0(du2 * extu_s[pl.ds(CONV_HALO - (K - 1) + k, tm), :])
        xv = x_ref[...]
        h = _modnorm(xv, g_ref[...], mod_ref[1:2, :], mod_ref[0:1, :]).astype(BF16)
        h_ref[...] = h
        a, b = _glu_fwd(h, w_ref, b_ref[...])
        sb = _sigmoid(b)
        da = du * sb
        db = du * a * sb * (1.0 - sb)
        dbglu_ref[:, 0:D] += _sum0(da)
        dbglu_ref[:, D:] += _sum0(db)
        da = da.astype(BF16)
        db = db.astype(BF16)
        dab_ref[:, 0:D] = da
        dab_ref[:, D:] = db
        Dh2 = D // 2
        dh = (_mm_nt(da[:, :Dh2], w_ref[0]) + _mm_nt(da[:, Dh2:], w_ref[1])
              + _mm_nt(db[:, :Dh2], w_ref[2]) + _mm_nt(db[:, Dh2:], w_ref[3]))
        dxn, dg, dscale, dshift = _modnorm_bwd(xv, g_ref[...], mod_ref[1:2, :], dh)
        dx_ref[...] = dres_ref[...] + dxn
        dmod_ref[0:1, :] += dshift
        dmod_ref[1:2, :] += dscale
        dg_ref[...] += dg

    tok = pl.BlockSpec((None, tm, D), lambda b, t: (b, t, 0))
    return pl.pallas_call(
        body, name="conv_glu_bwd", grid=(B, nt),
        in_specs=[tok, tok, tok, _future_halo_spec(tm, CONV_HALO, D, T), tok, _past_halo_spec(tm, CONV_HALO, D),
                  pl.BlockSpec((None, 3, D), lambda b, t: (b, 0, 0)), pl.BlockSpec((1, D), lambda b, t: (0, 0)),
                  pl.BlockSpec((4, D, D // 2), lambda b, t: (0, 0, 0)), pl.BlockSpec((1, 2 * D), lambda b, t: (0, 0)),
                  pl.BlockSpec((K + 1, D), lambda b, t: (0, 0))],
        out_specs=[tok, tok, pl.BlockSpec((None, tm, 2 * D), lambda b, t: (b, t, 0)),
                   pl.BlockSpec((None, K + 1, D), lambda b, t: (b, 0, 0)),
                   pl.BlockSpec((None, 1, 2 * D), lambda b, t: (b, 0, 0)),
                   pl.BlockSpec((None, 3, D), lambda b, t: (b, 0, 0)),
                   pl.BlockSpec((None, 1, D), lambda b, t: (b, 0, 0))],
        out_shape=[jax.ShapeDtypeStruct((B, T, D), F32), jax.ShapeDtypeStruct((B, T, D), BF16),
                   jax.ShapeDtypeStruct((B, T, 2 * D), BF16), jax.ShapeDtypeStruct((B, K + 1, D), F32),
                   jax.ShapeDtypeStruct((B, 1, 2 * D), F32), jax.ShapeDtypeStruct((B, 3, D), F32),
                   jax.ShapeDtypeStruct((B, 1, D), F32)],
        scratch_shapes=[pltpu.VMEM((tm + CONV_HALO, D), F32), pltpu.VMEM((tm + CONV_HALO, D), F32)],
        compiler_params=_cparams(2),
    )(x, dres, du2, du2, u, u, mod3, g, w_glu, b_glu, w_dw)


def dn_proj_fwd(x, mod3, g, w_main, w_ab):
    B, T, D = x.shape
    W = w_main.shape[1] // 4
    tm = _tile(T, 512)

    def body(x_ref, mod_ref, g_ref, wm_ref, wab_ref, pre_ref, z_ref, ab_ref):
        h = _modnorm(x_ref[...], g_ref[...], mod_ref[1:2, :], mod_ref[0:1, :]).astype(BF16)
        for p in range(3):
            pre_ref[:, p * W:(p + 1) * W] = _mm(h, wm_ref[:, p * W:(p + 1) * W])
        z_ref[...] = _mm(h, wm_ref[:, 3 * W:])
        ab_ref[...] = _mm(h, wab_ref[...])

    return pl.pallas_call(
        body, name="dn_proj_fwd", grid=(B, T // tm),
        in_specs=[pl.BlockSpec((None, tm, D), lambda b, t: (b, t, 0)), pl.BlockSpec((None, 3, D), lambda b, t: (b, 0, 0)),
                  pl.BlockSpec((1, D), lambda b, t: (0, 0)), pl.BlockSpec((D, 4 * W), lambda b, t: (0, 0)),
                  pl.BlockSpec((D, LANES), lambda b, t: (0, 0))],
        out_specs=[pl.BlockSpec((None, tm, 3 * W), lambda b, t: (b, t, 0)),
                   pl.BlockSpec((None, tm, W), lambda b, t: (b, t, 0)),
                   pl.BlockSpec((None, tm, LANES), lambda b, t: (b, t, 0))],
        out_shape=[jax.ShapeDtypeStruct((B, T, 3 * W), F32), jax.ShapeDtypeStruct((B, T, W), F32),
                   jax.ShapeDtypeStruct((B, T, LANES), F32)],
        compiler_params=_cparams(2),
    )(x, mod3, g, w_main, w_ab)


def _sconv(ext_s, w_ref, tm, K):
    acc = w_ref[0:1, :] * ext_s[pl.ds(SCONV_HALO - (K - 1), tm), :]
    for k in range(1, K):
        acc = acc + w_ref[k:k + 1, :] * ext_s[pl.ds(SCONV_HALO - (K - 1) + k, tm), :]
    return acc


def _lane_col(val, lane, idx):
    return jnp.sum(jnp.where(lane == idx, val, 0.0), axis=1, keepdims=True)


def dn_conv_fwd(pre, ab, w_sconv, alog_row, dt_row, H):
    B, T, W3 = pre.shape
    W = W3 // 3
    Dh = W // H
    K = w_sconv.shape[0]
    tm = _tile(T, 512)

    def body(pre_ref, halo_ref, ab_ref, w_ref, alog_ref, dt_ref, q_ref, k_ref, v_ref, gb_ref, bb_ref, ext_s):
        t = pl.program_id(1)
        ext_s[0:SCONV_HALO, :] = jnp.where(t > 0, halo_ref[...], 0.0)
        ext_s[SCONV_HALO:, :] = pre_ref[...]
        cv = _sconv(ext_s, w_ref, tm, K)
        qkv = cv * _sigmoid(cv)
        ab = ab_ref[...]
        lane = lax.broadcasted_iota(jnp.int32, ab.shape, 1)
        g_all = -jnp.exp(alog_ref[...]) * _softplus(ab + dt_ref[...])
        beta_all = _sigmoid(ab)
        for h in range(H):
            q_ref[h] = qkv[:, h * Dh:(h + 1) * Dh]
            k_ref[h] = qkv[:, W + h * Dh:W + (h + 1) * Dh]
            v_ref[h] = qkv[:, 2 * W + h * Dh:2 * W + (h + 1) * Dh]
            gb_ref[h] = jnp.broadcast_to(_lane_col(g_all, lane, h), (tm, Dh))
            bb_ref[h] = jnp.broadcast_to(_lane_col(beta_all, lane, H + h), (tm, Dh))

    hm = pl.BlockSpec((None, H, tm, Dh), lambda b, t: (b, 0, t, 0))
    row = pl.BlockSpec((1, LANES), lambda b, t: (0, 0))
    return pl.pallas_call(
        body, name="dn_conv_fwd", grid=(B, T // tm),
        in_specs=[pl.BlockSpec((None, tm, W3), lambda b, t: (b, t, 0)), _past_halo_spec(tm, SCONV_HALO, W3),
                  pl.BlockSpec((None, tm, LANES), lambda b, t: (b, t, 0)),
                  pl.BlockSpec((K, W3), lambda b, t: (0, 0)), row, row],
        out_specs=[hm] * 5, out_shape=[jax.ShapeDtypeStruct((B, H, T, Dh), F32)] * 5,
        scratch_shapes=[pltpu.VMEM((tm + SCONV_HALO, W3), F32)],
        compiler_params=_cparams(2),
    )(pre, pre, ab, w_sconv, alog_row, dt_row)


def _bdot(spec):
    return lambda a, b: jnp.einsum(spec, a.astype(BF16), b.astype(BF16), preferred_element_type=F32)


_NN, _NT, _TN = "gij,gjk->gik", "gik,gjk->gij", "gki,gkj->gij"


def _make_bdots():
    nn_, nt_, tn_ = _bdot(_NN), _bdot(_NT), _bdot(_TN)

    @jax.custom_vjp
    def nn(a, b):
        return nn_(a, b)

    @jax.custom_vjp
    def nt(a, b):
        return nt_(a, b)

    @jax.custom_vjp
    def tn(a, b):
        return tn_(a, b)

    nn.defvjp(lambda a, b: (nn_(a, b), (a, b)), lambda r, d: (nt_(d, r[1]), tn_(r[0], d)))
    nt.defvjp(lambda a, b: (nt_(a, b), (a, b)), lambda r, d: (nn_(d, r[1]), tn_(d, r[0])))
    tn.defvjp(lambda a, b: (tn_(a, b), (a, b)), lambda r, d: (nt_(r[1], d), nn_(r[0], d)))
    return nn, nt, tn


def _unit_lower_inverse(A):
    hdot = functools.partial(jnp.einsum, precision=lax.Precision.HIGH, preferred_element_type=F32)
    C = A.shape[-1]

    def impl(A):
        eye = (lax.broadcasted_iota(jnp.int32, A.shape, 1) == lax.broadcasted_iota(jnp.int32, A.shape, 2)).astype(F32)
        Tm = eye - A
        Ap = A
        for _ in range(max(1, (C - 1).bit_length()) - 1):
            Ap = hdot(_NN, Ap, Ap)
            Tm = Tm + hdot(_NN, Tm, Ap)
        return Tm

    @jax.custom_vjp
    def inv(A):
        return impl(A)

    def fwd(A):
        Tm = impl(A)
        return Tm, Tm

    def bwd(Tm, dT):
        return (-hdot(_NT, hdot(_TN, Tm, dT), Tm),)

    inv.defvjp(fwd, bwd)
    return inv(A)


def _chunk_fn(q, k, v, gb, bb, S):
    nn, nt, tn = _make_bdots()
    G, C, Dh = q.shape
    hdot = functools.partial(jnp.einsum, precision=lax.Precision.HIGH, preferred_element_type=F32)
    q = q * lax.rsqrt(jnp.sum(q * q, axis=-1, keepdims=True) + EPS) * (Dh ** -0.5)
    k = k * lax.rsqrt(jnp.sum(k * k, axis=-1, keepdims=True) + EPS)
    row = lax.broadcasted_iota(jnp.int32, (G, C, C), 1)
    col = lax.broadcasted_iota(jnp.int32, (G, C, C), 2)
    causal = row >= col
    strict = row > col
    gc = hdot(_NN, causal.astype(F32), gb)
    spread = jnp.full((G, C, Dh), 1.0 / Dh, F32)
    gi = hdot(_NT, gc, spread)
    gj = hdot(_NT, spread, gc)
    decay = jnp.where(causal, jnp.exp(jnp.where(causal, gi - gj, 0.0)), 0.0)
    kb = k * bb
    vb = v * bb
    A = jnp.where(strict, nt(kb, k) * decay, 0.0)
    Tm = _unit_lower_inverse(A)
    eg = jnp.exp(gc)
    u = nn(Tm, vb)
    w = nn(Tm, kb * eg)
    qg = q * eg
    intra = nt(q, k) * decay
    glast = hdot(_NN, jnp.ones((G, C, C), F32), gb)
    kd = k * jnp.exp(glast - gc)
    v_new = u - nn(w, S)
    o = nn(qg, S) + nn(intra, v_new)
    egl = jnp.exp(glast)
    S_new = S * jnp.concatenate([egl] * (Dh // C), axis=1) + tn(kd, v_new)
    return o, S_new


def dn_chunk_fwd(q, k, v, gb, bb):
    B, H, T, Dh = q.shape
    NC = T // CHUNK
    NS = _tile(NC, CHUNKS_PER_STEP, 1)

    def body(q_ref, k_ref, v_ref, gb_ref, bb_ref, o_ref, sp_ref, S_s):
        @pl.when(pl.program_id(1) == 0)
        def _():
            S_s[...] = jnp.zeros_like(S_s)

        def one_chunk(j, carry):
            rows = pl.ds(pl.multiple_of(j * CHUNK, CHUNK), CHUNK)
            S = S_s[...]
            sp_ref[j] = S
            o, S_new = _chunk_fn(q_ref[:, rows, :], k_ref[:, rows, :], v_ref[:, rows, :], gb_ref[:, rows, :],
                                 bb_ref[:, rows, :], S)
            o_ref[:, rows, :] = o
            S_s[...] = S_new
            return carry

        lax.fori_loop(0, NS, one_chunk, 0)

    hm = pl.BlockSpec((None, H, NS * CHUNK, Dh), lambda b, n: (b, 0, n, 0))
    return pl.pallas_call(
        body, name="dn_chunk_fwd", grid=(B, NC // NS),
        in_specs=[hm] * 5,
        out_specs=[hm, pl.BlockSpec((None, NS, H, Dh, Dh), lambda b, n: (b, n, 0, 0, 0))],
        out_shape=[jax.ShapeDtypeStruct((B, H, T, Dh), F32), jax.ShapeDtypeStruct((B, NC, H, Dh, Dh), F32)],
        scratch_shapes=[pltpu.VMEM((H, Dh, Dh), F32)],
        compiler_params=_cparams(2),
    )(q, k, v, gb, bb)


def dn_chunk_bwd(q, k, v, gb, bb, s_prev, do):
    B, H, T, Dh = q.shape
    NC = T // CHUNK
    NS = _tile(NC, CHUNKS_PER_STEP, 1)
    NG = NC // NS

    def body(q_ref, k_ref, v_ref, gb_ref, bb_ref, sp_ref, do_ref, dq_ref, dk_ref, dv_ref, dgb_ref, dbb_ref, dS_s):
        @pl.when(pl.program_id(1) == 0)
        def _():
            dS_s[...] = jnp.zeros_like(dS_s)

        def one_chunk(jj, carry):
            j = NS - 1 - jj
            rows = pl.ds(pl.multiple_of(j * CHUNK, CHUNK), CHUNK)
            _, vjp = jax.vjp(_chunk_fn, q_ref[:, rows, :], k_ref[:, rows, :], v_ref[:, rows, :], gb_ref[:, rows, :],
                             bb_ref[:, rows, :], sp_ref[j])
            dq, dk, dv, dgb, dbb, dS = vjp((do_ref[:, rows, :], dS_s[...]))
            dq_ref[:, rows, :] = dq
            dk_ref[:, rows, :] = dk
            dv_ref[:, rows, :] = dv
            dgb_ref[:, rows, :] = dgb
            dbb_ref[:, rows, :] = dbb
            dS_s[...] = dS
            return carry

        lax.fori_loop(0, NS, one_chunk, 0)

    hm = pl.BlockSpec((None, H, NS * CHUNK, Dh), lambda b, n: (b, 0, NG - 1 - n, 0))
    return pl.pallas_call(
        body, name="dn_chunk_bwd", grid=(B, NG),
        in_specs=[hm] * 5 + [pl.BlockSpec((None, NS, H, Dh, Dh), lambda b, n: (b, NG - 1 - n, 0, 0, 0)), hm],
        out_specs=[hm] * 5, out_shape=[jax.ShapeDtypeStruct((B, H, T, Dh), F32)] * 5,
        scratch_shapes=[pltpu.VMEM((H, Dh, Dh), F32)],
        compiler_params=_cparams(2),
    )(q, k, v, gb, bb, s_prev, do)


def _head_norm(o, og):
    r = lax.rsqrt(jnp.mean(o * o, axis=-1, keepdims=True) + EPS)
    return o * r, r


def dn_out_fwd(x, o, z, mod3, o_g, w_out):
    B, T, D = x.shape
    _, H, _, Dh = o.shape
    W = H * Dh
    tm = _tile(T, 512)

    def body(x_ref, o_ref, z_ref, mod_ref, og_ref, w_ref, xo_ref, y_ref):
        parts = []
        for h in range(H):
            on, _ = _head_norm(o_ref[h], og_ref[...])
            zz = z_ref[:, h * Dh:(h + 1) * Dh]
            parts.append((on * og_ref[...] * (zz * _sigmoid(zz))).astype(BF16))
        y = _mm(jnp.concatenate(parts, axis=1), w_ref[...])
        y_ref[...] = y
        xo_ref[...] = x_ref[...] + (1.0 + mod_ref[2:3, :]) * y

    tok = pl.BlockSpec((None, tm, D), lambda b, t: (b, t, 0))
    return pl.pallas_call(
        body, name="dn_out_fwd", grid=(B, T // tm),
        in_specs=[tok, pl.BlockSpec((None, H, tm, Dh), lambda b, t: (b, 0, t, 0)),
                  pl.BlockSpec((None, tm, W), lambda b, t: (b, t, 0)), pl.BlockSpec((None, 3, D), lambda b, t: (b, 0, 0)),
                  pl.BlockSpec((1, Dh), lambda b, t: (0, 0)), pl.BlockSpec((W, D), lambda b, t: (0, 0))],
        out_specs=[tok, tok], out_shape=[jax.ShapeDtypeStruct((B, T, D), F32)] * 2,
        compiler_params=_cparams(2),
    )(x, o, z, mod3, o_g, w_out)


def dn_out_bwd(dres, y, o, z, mod3, o_g, w_out):
    B, T, D = dres.shape
    _, H, _, Dh = o.shape
    W = H * Dh
    tm = _tile(T, 512)

    def body(dres_ref, y_ref, o_ref, z_ref, mod_ref, og_ref, w_ref, do_ref, dz_ref, ogb_ref, dy_ref, dgate_ref, dog_ref):
        t = pl.program_id(1)

        @pl.when(t == 0)
        def _():
            dgate_ref[...] = jnp.zeros_like(dgate_ref)
            dog_ref[...] = jnp.zeros_like(dog_ref)

        dres = dres_ref[...]
        dy = ((1.0 + mod_ref[2:3, :]) * dres).astype(BF16)
        dy_ref[...] = dy
        dgate_ref[...] += _sum0(dres * y_ref[...])
        dog = _mm_nt(dy, w_ref[...])
        og = og_ref[...]
        for h in range(H):
            ov = o_ref[h]
            xn, r = _head_norm(ov, og)
            zz = z_ref[:, h * Dh:(h + 1) * Dh]
            sg = _sigmoid(zz)
            sz = zz * sg
            d = dog[:, h * Dh:(h + 1) * Dh]
            ogb_ref[:, h * Dh:(h + 1) * Dh] = (xn * og * sz).astype(BF16)
            dz_ref[:, h * Dh:(h + 1) * Dh] = d * (xn * og) * _dsilu(zz, sg)
            don = d * sz
            dog_ref[...] += _sum0(don * xn)
            dxn = don * og
            do_ref[h] = r * (dxn - xn * jnp.mean(dxn * xn, axis=-1, keepdims=True))

    tok = pl.BlockSpec((None, tm, D), lambda b, t: (b, t, 0))
    tokw = pl.BlockSpec((None, tm, W), lambda b, t: (b, t, 0))
    hm = pl.BlockSpec((None, H, tm, Dh), lambda b, t: (b, 0, t, 0))
    return pl.pallas_call(
        body, name="dn_out_bwd", grid=(B, T // tm),
        in_specs=[tok, tok, hm, tokw, pl.BlockSpec((None, 3, D), lambda b, t: (b, 0, 0)),
                  pl.BlockSpec((1, Dh), lambda b, t: (0, 0)), pl.BlockSpec((W, D), lambda b, t: (0, 0))],
        out_specs=[hm, tokw, tokw, tok, pl.BlockSpec((None, 1, D), lambda b, t: (b, 0, 0)),
                   pl.BlockSpec((None, 1, Dh), lambda b, t: (b, 0, 0))],
        out_shape=[jax.ShapeDtypeStruct((B, H, T, Dh), F32), jax.ShapeDtypeStruct((B, T, W), F32),
                   jax.ShapeDtypeStruct((B, T, W), BF16), jax.ShapeDtypeStruct((B, T, D), BF16),
                   jax.ShapeDtypeStruct((B, 1, D), F32), jax.ShapeDtypeStruct((B, 1, Dh), F32)],
        compiler_params=_cparams(2),
    )(dres, y, o, z, mod3, o_g, w_out)


def dn_conv_bwd(dq, dk, dv, dgb, dbb, pre, ab, w_sconv, alog_row, dt_row):
    B, H, T, Dh = dq.shape
    W = H * Dh
    W3 = 3 * W
    K = w_sconv.shape[0]
    tm = _tile(T, 256)

    def body(dq_ref, dk_ref, dv_ref, dgb_ref, dbb_ref, pre_ref, halo_ref, ab_ref, w_ref, alog_ref, dt_ref,
             dc_ref, dab_ref, small_ref, ext_s):
        t = pl.program_id(1)

        @pl.when(t == 0)
        def _():
            small_ref[...] = jnp.zeros_like(small_ref)

        ext_s[0:SCONV_HALO, :] = jnp.where(t > 0, halo_ref[...], 0.0)
        ext_s[SCONV_HALO:, :] = pre_ref[...]
        cv = _sconv(ext_s, w_ref, tm, K)
        dsl = _dsilu(cv, _sigmoid(cv))
        ab = ab_ref[...]
        lane = lax.broadcasted_iota(jnp.int32, ab.shape, 1)
        dg_all = jnp.zeros_like(ab)
        db_all = jnp.zeros_like(ab)
        for h in range(H):
            dc_ref[:, h * Dh:(h + 1) * Dh] = dq_ref[h] * dsl[:, h * Dh:(h + 1) * Dh]
            dc_ref[:, W + h * Dh:W + (h + 1) * Dh] = dk_ref[h] * dsl[:, W + h * Dh:W + (h + 1) * Dh]
            dc_ref[:, 2 * W + h * Dh:2 * W + (h + 1) * Dh] = dv_ref[h] * dsl[:, 2 * W + h * Dh:2 * W + (h + 1) * Dh]
            dg_all = dg_all + jnp.where(lane == h, jnp.sum(dgb_ref[h], axis=1, keepdims=True), 0.0)
            db_all = db_all + jnp.where(lane == H + h, jnp.sum(dbb_ref[h], axis=1, keepdims=True), 0.0)
        xa = ab + dt_ref[...]
        ea = -jnp.exp(alog_ref[...])
        g_all = ea * _softplus(xa)
        da = dg_all * ea * _sigmoid(xa)
        beta = _sigmoid(ab)
        dab_ref[...] = da + db_all * beta * (1.0 - beta)
        small_ref[0:1, :] += _sum0(dg_all * g_all)
        small_ref[1:2, :] += _sum0(da)

    hm = pl.BlockSpec((None, H, tm, Dh), lambda b, t: (b, 0, t, 0))
    row = pl.BlockSpec((1, LANES), lambda b, t: (0, 0))
    return pl.pallas_call(
        body, name="dn_conv_bwd", grid=(B, T // tm),
        in_specs=[hm] * 5 + [pl.BlockSpec((None, tm, W3), lambda b, t: (b, t, 0)), _past_halo_spec(tm, SCONV_HALO, W3),
                             pl.BlockSpec((None, tm, LANES), lambda b, t: (b, t, 0)),
                             pl.BlockSpec((K, W3), lambda b, t: (0, 0)), row, row],
        out_specs=[pl.BlockSpec((None, tm, W3), lambda b, t: (b, t, 0)), pl.BlockSpec((None, tm, LANES), lambda b, t: (b, t, 0)),
                   pl.BlockSpec((None, 2, LANES), lambda b, t: (b, 0, 0))],
        out_shape=[jax.ShapeDtypeStruct((B, T, W3), F32), jax.ShapeDtypeStruct((B, T, LANES), F32),
                   jax.ShapeDtypeStruct((B, 2, LANES), F32)],
        scratch_shapes=[pltpu.VMEM((tm + SCONV_HALO, W3), F32)],
        compiler_params=_cparams(2),
    )(dq, dk, dv, dgb, dbb, pre, pre, ab, w_sconv, alog_row, dt_row)


def dn_proj_bwd(x, dres, dc, pre, dz, dab, mod3, g, w_main, w_ab, w_sconv):
    B, T, D = x.shape
    W3 = dc.shape[2]
    W = W3 // 3
    K = w_sconv.shape[0]
    tm = _tile(T, 256)
    nt = T // tm

    def body(x_ref, dres_ref, dc_ref, dch_ref, pre_ref, preh_ref, dz_ref, dab_ref, mod_ref, g_ref, wm_ref, wab_ref, ws_ref,
             dx_ref, h_ref, dproj_ref, dws_ref, dmod_ref, dg_ref, extp_s, extd_s):
        t = pl.program_id(1)

        @pl.when(t == 0)
        def _():
            dws_ref[...] = jnp.zeros_like(dws_ref)
            dmod_ref[...] = jnp.zeros_like(dmod_ref)
            dg_ref[...] = jnp.zeros_like(dg_ref)

        dc = dc_ref[...]
        extp_s[0:SCONV_HALO, :] = jnp.where(t > 0, preh_ref[...], 0.0)
        extp_s[SCONV_HALO:, :] = pre_ref[...]
        extd_s[0:tm, :] = dc
        extd_s[tm:, :] = jnp.where(t < nt - 1, dch_ref[...], 0.0)
        dpre = jnp.zeros((tm, W3), F32)
        for k in range(K):
            dpre = dpre + ws_ref[k:k + 1, :] * extd_s[pl.ds(K - 1 - k, tm), :]
            dws_ref[k:k + 1, :] += _sum0(dc * extp_s[pl.ds(SCONV_HALO - (K - 1) + k, tm), :])
        dpre = dpre.astype(BF16)
        dzb = dz_ref[...].astype(BF16)
        dproj_ref[:, 0:W3] = dpre
        dproj_ref[:, W3:] = dzb
        dh = _mm_nt(dab_ref[...], wab_ref[...]) + _mm_nt(dzb, wm_ref[:, W3:])
        for p in range(3):
            dh = dh + _mm_nt(dpre[:, p * W:(p + 1) * W], wm_ref[:, p * W:(p + 1) * W])
        xv = x_ref[...]
        h_ref[...] = _modnorm(xv, g_ref[...], mod_ref[1:2, :], mod_ref[0:1, :]).astype(BF16)
        dxn, dg, dscale, dshift = _modnorm_bwd(xv, g_ref[...], mod_ref[1:2, :], dh)
        dx_ref[...] = dres_ref[...] + dxn
        dmod_ref[0:1, :] += dshift
        dmod_ref[1:2, :] += dscale
        dg_ref[...] += dg

    tok = pl.BlockSpec((None, tm, D), lambda b, t: (b, t, 0))
    tok3 = pl.BlockSpec((None, tm, W3), lambda b, t: (b, t, 0))
    return pl.pallas_call(
        body, name="dn_proj_bwd", grid=(B, nt),
        in_specs=[tok, tok, tok3, _future_halo_spec(tm, SCONV_HALO, W3, T), tok3, _past_halo_spec(tm, SCONV_HALO, W3),
                  pl.BlockSpec((None, tm, W), lambda b, t: (b, t, 0)), pl.BlockSpec((None, tm, LANES), lambda b, t: (b, t, 0)),
                  pl.BlockSpec((None, 3, D), lambda b, t: (b, 0, 0)), pl.BlockSpec((1, D), lambda b, t: (0, 0)),
                  pl.BlockSpec((D, 4 * W), lambda b, t: (0, 0)), pl.BlockSpec((D, LANES), lambda b, t: (0, 0)),
                  pl.BlockSpec((K, W3), lambda b, t: (0, 0))],
        out_specs=[tok, tok, pl.BlockSpec((None, tm, 4 * W), lambda b, t: (b, t, 0)),
                   pl.BlockSpec((None, K, W3), lambda b, t: (b, 0, 0)), pl.BlockSpec((None, 3, D), lambda b, t: (b, 0, 0)),
                   pl.BlockSpec((None, 1, D), lambda b, t: (b, 0, 0))],
        out_shape=[jax.ShapeDtypeStruct((B, T, D), F32), jax.ShapeDtypeStruct((B, T, D), BF16),
                   jax.ShapeDtypeStruct((B, T, 4 * W), BF16), jax.ShapeDtypeStruct((B, K, W3), F32),
                   jax.ShapeDtypeStruct((B, 3, D), F32), jax.ShapeDtypeStruct((B, 1, D), F32)],
        scratch_shapes=[pltpu.VMEM((tm + SCONV_HALO, W3), F32), pltpu.VMEM((tm + SCONV_HALO, W3), F32)],
        compiler_params=_cparams(2),
    )(x, dres, dc, dc, pre, pre, dz, dab, mod3, g, w_main, w_ab, w_sconv)


def ada_fwd(c_all, w_ada, b_cols):
    L, D, Ca = w_ada.shape
    NB = c_all.shape[0]

    def body(c_ref, w_ref, b_ref, o_ref):
        cv = c_ref[...]
        o_ref[...] = _mm(cv * _sigmoid(cv), w_ref[...]) + b_ref[...]

    return pl.pallas_call(
        body, name="ada_fwd", grid=(L,),
        in_specs=[pl.BlockSpec((NB, D), lambda i: (0, 0)), pl.BlockSpec((None, D, Ca), lambda i: (i, 0, 0)),
                  pl.BlockSpec((None, 1, Ca), lambda i: (i, 0, 0))],
        out_specs=pl.BlockSpec((None, NB, Ca), lambda i: (i, 0, 0)),
        out_shape=jax.ShapeDtypeStruct((L, NB, Ca), F32),
        compiler_params=_cparams(1),
    )(c_all, w_ada, b_cols)


def ada_bwd(c_all, dmod_cols, dmod_all):
    L, NB, Ca = dmod_cols.shape
    D = c_all.shape[1]
    C9 = dmod_all.shape[2]

    def body(c_ref, dc_ref, da_ref, gw_ref, gb_ref):
        cv = c_ref[...]
        gw_ref[...] = _mm_tn(cv * _sigmoid(cv), dc_ref[...])
        gb_ref[...] = _sum0(da_ref[...])

    return pl.pallas_call(
        body, name="ada_bwd", grid=(L,),
        in_specs=[pl.BlockSpec((NB, D), lambda i: (0, 0)), pl.BlockSpec((None, NB, Ca), lambda i: (i, 0, 0)),
                  pl.BlockSpec((None, NB, C9), lambda i: (i, 0, 0))],
        out_specs=[pl.BlockSpec((None, D, Ca), lambda i: (i, 0, 0)), pl.BlockSpec((None, 1, C9), lambda i: (i, 0, 0))],
        out_shape=[jax.ShapeDtypeStruct((L, D, Ca), F32), jax.ShapeDtypeStruct((L, 1, C9), F32)],
        compiler_params=_cparams(1),
    )(c_all, dmod_cols, dmod_all)


def adamw(w, g, m, v, name, token=None):
    R, C = w.shape
    tr = _tile(R, max(8, (1 << 18) // C))
    if token is None:
        token = jnp.zeros((8, LANES), F32)

    def body(w_ref, g_ref, m_ref, v_ref, t_ref, d_ref, mo_ref, vo_ref):
        gv = g_ref[...] + t_ref[0:1, 0:1]
        mn = ADAM_B1 * m_ref[...] + (1.0 - ADAM_B1) * gv
        vn = ADAM_B2 * v_ref[...] + (1.0 - ADAM_B2) * (gv * gv)
        m_hat = mn / (1.0 - ADAM_B1 ** ADAM_STEP)
        v_hat = vn / (1.0 - ADAM_B2 ** ADAM_STEP)
        d_ref[...] = -ADAM_LR * (m_hat / (jnp.sqrt(v_hat) + ADAM_EPS) + ADAM_WD * w_ref[...])
        mo_ref[...] = mn
        vo_ref[...] = vn

    blk = pl.BlockSpec((tr, C), lambda i: (i, 0))
    return pl.pallas_call(
        body, name=name, grid=(R // tr,), in_specs=[blk] * 4 + [pl.BlockSpec((8, LANES), lambda i: (0, 0))],
        out_specs=[blk] * 3, out_shape=[jax.ShapeDtypeStruct((R, C), F32)] * 3, compiler_params=_cparams(1),
    )(w, g, m, v, token)


def sum_devices(a):
    n, R, C = a.shape

    def body(a_ref, o_ref):
        s = a_ref[0]
        for d in range(1, n):
            s = s + a_ref[d]
        o_ref[...] = s

    return pl.pallas_call(
        body, name="sum_devices", out_shape=jax.ShapeDtypeStruct((R, C), F32),
        compiler_params=pltpu.CompilerParams(vmem_limit_bytes=VMEM_LIMIT_V7X),
    )(a)


def _place():
    x, y, c = lax.axis_index("x"), lax.axis_index("y"), lax.axis_index("c")
    return x, y, c


def _other_chips(x, y):
    return [(2 * (1 - x) + y, 1 - x, y), (2 * x + (1 - y), x, 1 - y), (2 * (1 - x) + (1 - y), 1 - x, 1 - y)]


def allgather8(block):
    m_per, n = block.shape

    def body(x_ref, out_ref, send_sems, recv_sems, local_sem):
        x, y, c = _place()
        me, sibling = (x, y, c), (x, y, 1 - c)
        chips = [(1 - x, y), (x, 1 - y), (1 - x, 1 - y)]

        def rows(px, py, pc):
            return out_ref.at[pl.ds((4 * px + 2 * py + pc) * m_per, m_per), :]

        def copy(k, blk, to, src=None):
            return pltpu.make_async_remote_copy(
                src_ref=rows(*blk) if src is None else src, dst_ref=rows(*blk),
                send_sem=send_sems.at[k], recv_sem=recv_sems.at[k], device_id=to, device_id_type=MESH)

        mine = pltpu.make_async_copy(x_ref, rows(*me), local_sem)
        mine.start()
        first = [copy(0, me, sibling, src=x_ref)]
        first += [copy(1 + j, me, (*chip, c), src=x_ref) for j, chip in enumerate(chips)]
        for cp in first:
            cp.start()
        passed = [copy(4 + j, (*chip, c), sibling) for j, chip in enumerate(chips)]
        for j, chip in enumerate(chips):
            copy(1 + j, (*chip, c), me).wait_recv()
            passed[j].start()
        copy(0, sibling, me).wait_recv()
        for j, chip in enumerate(chips):
            copy(4 + j, (*chip, 1 - c), me).wait_recv()
        for cp in first + passed:
            cp.wait_send()
        mine.wait()

    return pl.pallas_call(
        body, name="allgather8", out_shape=jax.ShapeDtypeStruct((N_DEV * m_per, n), block.dtype),
        in_specs=[pl.BlockSpec(memory_space=pltpu.VMEM)], out_specs=pl.BlockSpec(memory_space=pltpu.VMEM),
        scratch_shapes=[pltpu.SemaphoreType.DMA((7,)), pltpu.SemaphoreType.DMA((7,)), pltpu.SemaphoreType.DMA],
        compiler_params=pltpu.CompilerParams(vmem_limit_bytes=VMEM_LIMIT_V7X),
    )(block)


def _half(ref, c, rh):
    return ref.at[pl.ds(pl.multiple_of(c * rh, 16), rh), :]


def gather_weights(lands):
    K = len(lands)

    def body(*refs):
        ins, outs = refs[:K], refs[K:2 * K]
        ici_send, ici_recv, d2d_send, d2d_recv = refs[2 * K:]
        x, y, c = _place()
        me = 2 * x + y
        sibling = (x, y, 1 - c)
        others = _other_chips(x, y)
        sent = []
        for k in range(K):
            rh = ins[k].shape[1] // 2
            for r, (_, px, py) in enumerate(others):
                cp = pltpu.make_async_remote_copy(
                    src_ref=_half(ins[k].at[me], c, rh), dst_ref=_half(outs[k].at[me], c, rh),
                    send_sem=ici_send.at[k, r], recv_sem=ici_recv.at[k, r], device_id=(px, py, c), device_id_type=MESH)
                cp.start()
                sent.append(cp)
        forwards = []
        for k in range(K):
            rh = ins[k].shape[1] // 2
            for r, (pchip, px, py) in enumerate(others):
                landed = _half(outs[k].at[pchip], c, rh)
                pltpu.make_async_remote_copy(
                    src_ref=landed, dst_ref=landed, send_sem=ici_send.at[k, r], recv_sem=ici_recv.at[k, r],
                    device_id=(px, py, c), device_id_type=MESH).wait_recv()
                fw = pltpu.make_async_remote_copy(
                    src_ref=landed, dst_ref=landed, send_sem=d2d_send.at[k, r], recv_sem=d2d_recv.at[k, r],
                    device_id=sibling, device_id_type=MESH)
                fw.start()
                forwards.append(fw)
        for k in range(K):
            rh = ins[k].shape[1] // 2
            for r, (pchip, _, _) in enumerate(others):
                theirs = _half(outs[k].at[pchip], 1 - c, rh)
                pltpu.make_async_remote_copy(
                    src_ref=theirs, dst_ref=theirs, send_sem=d2d_send.at[k, r], recv_sem=d2d_recv.at[k, r],
                    device_id=sibling, device_id_type=MESH).wait_recv()
        for cp in sent + forwards:
            cp.wait_send()

    return pl.pallas_call(
        body, name="gather_weights",
        out_shape=[jax.ShapeDtypeStruct(s.shape, s.dtype) for s in lands],
        in_specs=[HBM_SPEC] * K, out_specs=[HBM_SPEC] * K, input_output_aliases={k: k for k in range(K)},
        scratch_shapes=[pltpu.SemaphoreType.DMA((K, 3))] * 4,
    )(*lands)


def pair_exchange(grads):
    K = len(grads)

    def body(*refs):
        ins, outs = refs[:K], refs[K:2 * K]
        send_sems, recv_sems = refs[2 * K:]
        x, y, c = _place()
        sibling = (x, y, 1 - c)
        copies = []
        for k in range(K):
            n, r, _ = ins[k].shape
            rh = r // 2
            cp = pltpu.make_async_remote_copy(
                src_ref=ins[k].at[:, pl.ds(pl.multiple_of((1 - c) * rh, 16), rh), :], dst_ref=outs[k],
                send_sem=send_sems.at[k], recv_sem=recv_sems.at[k], device_id=sibling, device_id_type=MESH)
            cp.start()
            copies.append(cp)
        for cp in copies:
            cp.wait_recv()
        for cp in copies:
            cp.wait_send()

    return pl.pallas_call(
        body, name="pair_exchange",
        out_shape=[jax.ShapeDtypeStruct((g.shape[0], g.shape[1] // 2, g.shape[2]), g.dtype) for g in grads],
        in_specs=[HBM_SPEC] * K, out_specs=[HBM_SPEC] * K,
        scratch_shapes=[pltpu.SemaphoreType.DMA((K,))] * 2,
    )(*grads)


def pair_add(grad, recv, c_idx):
    n, r, C = grad.shape
    rh = r // 2
    tr = _tile(rh, max(16, (1 << 19) // C), 16)
    grad = grad.reshape(n, 2, rh, C)

    def body(c_ref, g_ref, r_ref, o_ref):
        o_ref[...] = (g_ref[...].astype(F32) + r_ref[...].astype(F32)).astype(BF16)

    return pl.pallas_call(
        body, name="pair_add",
        grid_spec=pltpu.PrefetchScalarGridSpec(
            num_scalar_prefetch=1, grid=(n, rh // tr),
            in_specs=[pl.BlockSpec((None, None, tr, C), lambda d, i, c_ref: (d, c_ref[0], i, 0)),
                      pl.BlockSpec((None, tr, C), lambda d, i, c_ref: (d, i, 0))],
            out_specs=pl.BlockSpec((None, tr, C), lambda d, i, c_ref: (d, i, 0))),
        out_shape=jax.ShapeDtypeStruct((n, rh, C), BF16), compiler_params=_cparams(2),
    )(c_idx, grad, recv)


def chip_exchange(parts):
    K = len(parts)

    def body(*refs):
        ins, outs = refs[:K], refs[K:2 * K]
        send_sems, recv_sems = refs[2 * K:]
        x, y, c = _place()
        others = _other_chips(x, y)
        started = []
        for k in range(K):
            for r, (pchip, px, py) in enumerate(others):
                cp = pltpu.make_async_remote_copy(
                    src_ref=ins[k].at[pchip], dst_ref=outs[k].at[r], send_sem=send_sems.at[k, r],
                    recv_sem=recv_sems.at[k, r], device_id=(px, py, c), device_id_type=MESH)
                cp.start()
                started.append(cp)
        for cp in started:
            cp.wait_recv()
        for cp in started:
            cp.wait_send()

    return pl.pallas_call(
        body, name="chip_exchange",
        out_shape=[jax.ShapeDtypeStruct((3,) + p.shape[1:], p.dtype) for p in parts],
        in_specs=[HBM_SPEC] * K, out_specs=[HBM_SPEC] * K,
        scratch_shapes=[pltpu.SemaphoreType.DMA((K, 3))] * 2,
    )(*parts)


def chip_sum(parts, got, where):
    _, rh, C = parts.shape
    tr = _tile(rh, max(16, (1 << 19) // C), 16)
    nt = rh // tr

    def body(w_ref, p_ref, g_ref, o_ref):
        s = p_ref[...].astype(F32)
        for r in range(3):
            s = s + g_ref[r].astype(F32)
        o_ref[...] = s

    return pl.pallas_call(
        body, name="chip_sum",
        grid_spec=pltpu.PrefetchScalarGridSpec(
            num_scalar_prefetch=1, grid=(nt,),
            in_specs=[pl.BlockSpec((None, tr, C), lambda i, w_ref: (w_ref[0], i, 0)),
                      pl.BlockSpec((3, tr, C), lambda i, w_ref: (0, i, 0))],
            out_specs=pl.BlockSpec((tr, C), lambda i, w_ref: (w_ref[1] * nt + i, 0))),
        out_shape=jax.ShapeDtypeStruct((2 * rh, C), F32), compiler_params=_cparams(1),
    )(where, parts, got)


def pair_share(sums):
    K = len(sums)

    def body(*refs):
        ins, outs = refs[:K], refs[K:2 * K]
        send_sems, recv_sems = refs[2 * K:]
        x, y, c = _place()
        sibling = (x, y, 1 - c)
        started = []
        for k in range(K):
            rh = ins[k].shape[0] // 2
            cp = pltpu.make_async_remote_copy(
                src_ref=_half(ins[k], c, rh), dst_ref=_half(outs[k], c, rh), send_sem=send_sems.at[k],
                recv_sem=recv_sems.at[k], device_id=sibling, device_id_type=MESH)
            cp.start()
            started.append(cp)
        for k in range(K):
            rh = ins[k].shape[0] // 2
            theirs = _half(outs[k], 1 - c, rh)
            pltpu.make_async_remote_copy(
                src_ref=theirs, dst_ref=theirs, send_sem=send_sems.at[k], recv_sem=recv_sems.at[k],
                device_id=sibling, device_id_type=MESH).wait_recv()
        for cp in started:
            cp.wait_send()

    return pl.pallas_call(
        body, name="pair_share",
        out_shape=[jax.ShapeDtypeStruct(s.shape, s.dtype) for s in sums],
        in_specs=[HBM_SPEC] * K, out_specs=[HBM_SPEC] * K, input_output_aliases={k: k for k in range(K)},
        scratch_shapes=[pltpu.SemaphoreType.DMA((K,))] * 2,
    )(*sums)


SEM_SPEC = pl.BlockSpec(memory_space=pltpu.SEMAPHORE)
ANY_SPEC = pl.BlockSpec(memory_space=pl.ANY)
DATAFLOW = pltpu.SideEffectType.DATAFLOW_SIDE_EFFECTING


def _in_hbm(a):
    return pltpu.with_memory_space_constraint(a, pltpu.HBM)


def _ici_copies(srcs, dsts, send_sems, recv_sems, src_slice, dst_slice):
    x, y, c = _place()
    out = []
    for k in range(len(srcs)):
        for r, (pchip, px, py) in enumerate(_other_chips(x, y)):
            out.append(pltpu.make_async_remote_copy(
                src_ref=src_slice(srcs[k], r, pchip), dst_ref=dst_slice(dsts[k], r, pchip),
                send_sem=send_sems.at[3 * k + r], recv_sem=recv_sems.at[3 * k + r], device_id=(px, py, c),
                device_id_type=MESH))
    return out


def _exchange_start(bufs, lands, src_slice, dst_slice, name, after=None):
    K = len(bufs)
    same = lands is None
    n_thru = K if same else 2 * K
    n_in = n_thru + (after is not None)

    def body(*refs):
        ins = refs[:n_thru]
        send_sems, recv_sems = refs[n_in], refs[n_in + 1]
        token = refs[-1]
        srcs = ins[:K]
        dsts = srcs if same else ins[K:]
        for cp in _ici_copies(srcs, dsts, send_sems, recv_sems, src_slice, dst_slice):
            cp.start()
        token[...] = jnp.zeros_like(token)

    thru = list(bufs) + ([] if same else list(lands))
    res = pl.pallas_call(
        body, name=name,
        out_shape=[pltpu.SemaphoreType.DMA((3 * K,)), pltpu.SemaphoreType.DMA((3 * K,))]
        + [pltpu.HBM(a.shape, a.dtype) for a in thru] + [jax.ShapeDtypeStruct((8, LANES), F32)],
        in_specs=[HBM_SPEC] * n_thru + [ANY_SPEC] * (after is not None),
        out_specs=[SEM_SPEC, SEM_SPEC] + [HBM_SPEC] * n_thru + [pl.BlockSpec(memory_space=pltpu.VMEM)],
        input_output_aliases={i: 2 + i for i in range(n_thru)},
        compiler_params=pltpu.CompilerParams(has_side_effects=DATAFLOW),
    )(*[_in_hbm(a) for a in thru], *([] if after is None else [after]))
    return res[0], res[1], res[2:2 + K], (res[2:2 + K] if same else res[2 + K:2 + 2 * K]), res[-1]


def _exchange_wait(send_sems, recv_sems, bufs, lands, after, src_slice, dst_slice, name):
    K = len(bufs)
    same = lands is None
    n_thru = K if same else 2 * K

    def body(*refs):
        ins = refs[:n_thru]
        ssem, rsem = refs[n_thru], refs[n_thru + 1]
        srcs = ins[:K]
        dsts = srcs if same else ins[K:]
        copies = _ici_copies(srcs, dsts, ssem, rsem, src_slice, dst_slice)
        for cp in copies:
            cp.wait_send()
        for cp in copies:
            cp.wait_recv()

    thru = list(bufs) + ([] if same else list(lands))
    res = pl.pallas_call(
        body, name=name,
        out_shape=[pltpu.HBM(a.shape, a.dtype) for a in thru],
        in_specs=[HBM_SPEC] * n_thru + [SEM_SPEC, SEM_SPEC, ANY_SPEC],
        out_specs=[HBM_SPEC] * n_thru,
        input_output_aliases={i: i for i in range(n_thru)},
        compiler_params=pltpu.CompilerParams(has_side_effects=DATAFLOW),
    )(*thru, send_sems, recv_sems, after)
    return res[:K], (res[:K] if same else res[K:])


def _own_half(ref, r, pchip):
    x, y, c = _place()
    return _half(ref.at[2 * x + y], c, ref.shape[1] // 2)


def _their_half(ref, r, pchip):
    _, _, c = _place()
    return _half(ref.at[pchip], c, ref.shape[1] // 2)


def gather_start(lands, name, after=None):
    return _exchange_start(lands, None, _own_half, _own_half, name, after)


def gather_wait(handle, after, name):
    ssem, rsem, lands, _, _ = handle
    return _exchange_wait(ssem, rsem, lands, None, after, _own_half, _their_half, name)[1]


def pair_forward(lands):
    K = len(lands)

    def body(*refs):
        ins, outs = refs[:K], refs[K:2 * K]
        send_sems, recv_sems = refs[2 * K:]
        x, y, c = _place()
        sibling = (x, y, 1 - c)
        started = []
        for k in range(K):
            rh = ins[k].shape[1] // 2
            for r, (pchip, _, _) in enumerate(_other_chips(x, y)):
                cp = pltpu.make_async_remote_copy(
                    src_ref=_half(ins[k].at[pchip], c, rh), dst_ref=_half(outs[k].at[pchip], c, rh),
                    send_sem=send_sems.at[k, r], recv_sem=recv_sems.at[k, r], device_id=sibling, device_id_type=MESH)
                cp.start()
                started.append(cp)
        for k in range(K):
            rh = ins[k].shape[1] // 2
            for r, (pchip, _, _) in enumerate(_other_chips(x, y)):
                theirs = _half(outs[k].at[pchip], 1 - c, rh)
                pltpu.make_async_remote_copy(
                    src_ref=theirs, dst_ref=theirs, send_sem=send_sems.at[k, r], recv_sem=recv_sems.at[k, r],
                    device_id=sibling, device_id_type=MESH).wait_recv()
        for cp in started:
            cp.wait_send()

    return pl.pallas_call(
        body, name="pair_forward",
        out_shape=[jax.ShapeDtypeStruct(s.shape, s.dtype) for s in lands],
        in_specs=[HBM_SPEC] * K, out_specs=[HBM_SPEC] * K, input_output_aliases={k: k for k in range(K)},
        scratch_shapes=[pltpu.SemaphoreType.DMA((K, 3))] * 2,
    )(*lands)


def _to_chip(ref, r, pchip):
    return ref.at[pchip]


def _from_relation(ref, r, pchip):
    return ref.at[r]


def reduce_start(grads, c_idx, name, after=None):
    recv = pair_exchange(grads)
    parts = [pair_add(g, r, c_idx) for g, r in zip(grads, recv)]
    lands = [lax.empty((3,) + p.shape[1:], p.dtype) for p in parts]
    return _exchange_start(parts, lands, _to_chip, _from_relation, name, after)


def reduce_finish(handle, after, where, name):
    ssem, rsem, parts, lands, _ = handle
    parts, got = _exchange_wait(ssem, rsem, parts, lands, after, _to_chip, _from_relation, name)
    return pair_share([chip_sum(p, g, where) for p, g in zip(parts, got)])


def _pack(arrs):
    flat = jnp.concatenate([a.reshape(-1).astype(F32) for a in arrs])
    pad = (-flat.shape[0]) % (8 * LANES)
    return jnp.pad(flat, (0, pad)).reshape(-1, LANES)


def _unpack(flat, shapes):
    out, off = [], 0
    for s in shapes:
        n = 1
        for d in s:
            n *= d
        out.append(flat[off:off + n].reshape(s))
        off += n
    return out


def _adamw_any(w, g, m, v, name, token=None):
    shp = w.shape
    C = shp[-1]
    d, nm, nv = adamw(w.reshape(-1, C), g.reshape(-1, C), m.reshape(-1, C), v.reshape(-1, C), name, token)
    return d.reshape(shp), nm.reshape(shp), nv.reshape(shp)


def kernel(x, c, norm_g, w_ada, b_ada, w_ffn_in, w_ffn_out, cm_w_glu, cm_b_glu, cm_w_dw, cm_b_dw, cm_ln_g, cm_ln_b, cm_w_pw, cm_b_pw, dn_w_in, dn_w_sconv, dn_a_log, dn_dt_bias, dn_o_g, dn_w_out, final_g, loss_target, m_norm_g, m_w_ada, m_b_ada, m_w_ffn_in, m_w_ffn_out, m_cm_w_glu, m_cm_b_glu, m_cm_w_dw, m_cm_b_dw, m_cm_ln_g, m_cm_ln_b, m_cm_w_pw, m_cm_b_pw, m_dn_w_in, m_dn_w_sconv, m_dn_a_log, m_dn_dt_bias, m_dn_o_g, m_dn_w_out, m_final_g, v_norm_g, v_w_ada, v_b_ada, v_w_ffn_in, v_w_ffn_out, v_cm_w_glu, v_cm_b_glu, v_cm_w_dw, v_cm_b_dw, v_cm_ln_g, v_cm_ln_b, v_cm_w_pw, v_cm_b_pw, v_dn_w_in, v_dn_w_sconv, v_dn_a_log, v_dn_dt_bias, v_dn_o_g, v_dn_w_out, v_final_g):
    weights = dict(norm_g=norm_g, w_ada=w_ada, b_ada=b_ada, w_ffn_in=w_ffn_in, w_ffn_out=w_ffn_out, cm_w_glu=cm_w_glu,
                   cm_b_glu=cm_b_glu, cm_w_dw=cm_w_dw, cm_b_dw=cm_b_dw, cm_ln_g=cm_ln_g, cm_ln_b=cm_ln_b, cm_w_pw=cm_w_pw,
                   cm_b_pw=cm_b_pw, dn_w_in=dn_w_in, dn_w_sconv=dn_w_sconv, dn_a_log=dn_a_log, dn_dt_bias=dn_dt_bias,
                   dn_o_g=dn_o_g, dn_w_out=dn_w_out, final_g=final_g)
    mom_m = dict(norm_g=m_norm_g, w_ada=m_w_ada, b_ada=m_b_ada, w_ffn_in=m_w_ffn_in, w_ffn_out=m_w_ffn_out,
                 cm_w_glu=m_cm_w_glu, cm_b_glu=m_cm_b_glu, cm_w_dw=m_cm_w_dw, cm_b_dw=m_cm_b_dw, cm_ln_g=m_cm_ln_g,
                 cm_ln_b=m_cm_ln_b, cm_w_pw=m_cm_w_pw, cm_b_pw=m_cm_b_pw, dn_w_in=m_dn_w_in, dn_w_sconv=m_dn_w_sconv,
                 dn_a_log=m_dn_a_log, dn_dt_bias=m_dn_dt_bias, dn_o_g=m_dn_o_g, dn_w_out=m_dn_w_out, final_g=m_final_g)
    mom_v = dict(norm_g=v_norm_g, w_ada=v_w_ada, b_ada=v_b_ada, w_ffn_in=v_w_ffn_in, w_ffn_out=v_w_ffn_out,
                 cm_w_glu=v_cm_w_glu, cm_b_glu=v_cm_b_glu, cm_w_dw=v_cm_w_dw, cm_b_dw=v_cm_b_dw, cm_ln_g=v_cm_ln_g,
                 cm_ln_b=v_cm_ln_b, cm_w_pw=v_cm_w_pw, cm_b_pw=v_cm_b_pw, dn_w_in=v_dn_w_in, dn_w_sconv=v_dn_w_sconv,
                 dn_a_log=v_dn_a_log, dn_dt_bias=v_dn_dt_bias, dn_o_g=v_dn_o_g, dn_w_out=v_dn_w_out, final_g=v_final_g)
    names = list(weights)

    BL, T, D = x.shape
    L = norm_g.shape[0]
    NB = BL * N_DEV
    Ca = w_ada.shape[2]
    C9 = b_ada.shape[1]
    H = dn_a_log.shape[1]
    Dh = dn_o_g.shape[1]
    W = H * Dh
    KC = cm_w_dw.shape[1]
    KS = dn_w_sconv.shape[1]
    n_cm, n_dn = cm_w_glu.shape[0], dn_w_in.shape[0]
    ax, ay, ac = lax.axis_index("x"), lax.axis_index("y"), lax.axis_index("c")
    chip = 2 * ax + ay
    dev = 2 * chip + ac
    c_idx = ac.astype(jnp.int32).reshape(1)
    where = jnp.stack([chip, ac]).astype(jnp.int32)

    small_in = [c, norm_g, cm_w_dw, dn_w_sconv]
    packed = _pack(small_in)
    gathered = allgather8(packed).reshape(N_DEV, -1)
    per_dev = [_unpack(gathered[d], [a.shape for a in small_in]) for d in range(N_DEV)]
    c_all = jnp.concatenate([p[0] for p in per_dev], axis=0)
    norm_g_full = jnp.concatenate([per_dev[2 * s][1] for s in range(N_CHIPS)], axis=-1)
    w_dw_full = jnp.concatenate([per_dev[2 * s][2] for s in range(N_CHIPS)], axis=-1)
    w_sconv_full = jnp.concatenate([per_dev[2 * s][3] for s in range(N_CHIPS)], axis=-1)

    b_cols = lax.dynamic_slice_in_dim(b_ada, chip * Ca, Ca, axis=1).reshape(L, 1, Ca)
    mod_part = ada_fwd(c_all, w_ada, b_cols)
    mod_g = allgather8(mod_part.reshape(-1, LANES)).reshape(N_DEV, L, NB, Ca)
    mod_all = jnp.concatenate([mod_g[2 * s] for s in range(N_CHIPS)], axis=-1)
    mod = lax.dynamic_slice_in_dim(mod_all, dev * BL, BL, axis=1).reshape(L, BL, 9, D)

    def layer_shards(i):
        sh = [w_ffn_in[i, 0], w_ffn_in[i, 1], w_ffn_out[i, 0], w_ffn_out[i, 1]]
        if i % 2 == 0:
            sh += [cm_w_glu[i // 2], cm_w_pw[i // 2]]
        else:
            sh += [dn_w_in[i // 2], dn_w_out[i // 2]]
        return [lax.dynamic_update_slice(lax.empty((N_CHIPS,) + s.shape, BF16), s.astype(BF16)[None], (chip, 0, 0))
                for s in sh]

    lands = [layer_shards(i) for i in range(L)]
    wts = [None] * L
    first = gather_start([lands[0][0], lands[0][2]], "gather_start_0a")
    rest = gather_start([lands[0][k] for k in (1, 3, 4, 5)], "gather_start_0b", first[4])

    def dn_weights(i):
        full = jnp.transpose(wts[i][4], (1, 0, 2)).reshape(D, -1)
        return full[:, :4 * W], jnp.pad(full[:, 4 * W:], ((0, 0), (0, LANES - 2 * H)))

    def row128(v):
        return jnp.pad(v.reshape(1, -1), ((0, 0), (0, LANES - v.shape[-1])))

    def pad_taps(w):
        return jnp.pad(w, ((0, 1), (0, 0)))

    saved = []
    xs = x
    after = mod
    for i in range(L):
        tok = 0.0
        if i == 0:
            wl = wts[0] = [None] * 6
            wl[0], wl[2] = pair_forward(gather_wait(first, after, "gather_wait_0a"))
        else:
            wl = wts[i] = pair_forward(gather_wait(handle, after, "gather_wait_%d" % i))
            if i + 1 < L:
                handle = gather_start(lands[i + 1], "gather_start_%d" % (i + 1), wl[0])
                tok = handle[4][0, 0]
        sv = {}
        m3 = [mod[i, :, 3 * j:3 * j + 3] + tok for j in range(3)]
        gs = [norm_g_full[i, j].reshape(1, D) for j in range(3)]
        sv["x0"] = xs
        xs, sv["y0"] = ffn_fwd(xs, m3[0], gs[0], wl[0], wl[2])
        sv["x1"] = xs
        if i == 0:
            wl[1], wl[3], wl[4], wl[5] = pair_forward(gather_wait(rest, xs, "gather_wait_0b"))
            handle = gather_start(lands[1], "gather_start_1", wl[1])
            m3 = [m + handle[4][0, 0] for m in m3]
        if i % 2 == 0:
            a = i // 2
            sv["u"] = conv_glu_fwd(xs, m3[1], gs[1], wl[4], cm_b_glu[a].reshape(1, -1))
            xs, sv["y1"], sv["u2"] = conv_out_fwd(
                xs, sv["u"], m3[1], pad_taps(w_dw_full[a]), cm_b_dw[a].reshape(1, D), cm_ln_g[a].reshape(1, D),
                cm_ln_b[a].reshape(1, D), wl[5].reshape(D, D), cm_b_pw[a].reshape(1, D))
        else:
            a = i // 2
            w_main, w_ab = dn_weights(i)
            sv["pre"], sv["z"], sv["ab"] = dn_proj_fwd(xs, m3[1], gs[1], w_main, w_ab)
            qkvgb = dn_conv_fwd(sv["pre"], sv["ab"], w_sconv_full[a], row128(dn_a_log[a]), row128(dn_dt_bias[a]), H)
            sv["qkvgb"] = qkvgb
            sv["o"], sv["sp"] = dn_chunk_fwd(*qkvgb)
            xs, sv["y1"] = dn_out_fwd(xs, sv["o"], sv["z"], m3[1], dn_o_g[a].reshape(1, Dh), wl[5].reshape(W, D))
        sv["x2"] = xs
        xs, sv["y2"] = ffn_fwd(xs, m3[2], gs[2], wl[1], wl[3])
        saved.append(sv)
        after = xs

    dx, d_final_g, loss_part = final_loss(xs, final_g.reshape(1, D), loss_target)

    g_small = {n: None for n in names}
    d_norm_g = [[None] * 3 for _ in range(L)]
    dmod = [[None] * 3 for _ in range(L)]
    g_cm = {k: [None] * n_cm for k in ("b_glu", "w_dw", "b_dw", "ln_g", "ln_b", "b_pw")}
    g_dn = {k: [None] * n_dn for k in ("w_sconv", "a_log", "dt_bias", "o_g")}
    big = [None] * L

    def ffn_back(i, j, slot, dx, tok=0.0):
        wl, sv = wts[i], saved[i]
        m3 = mod[i, :, 3 * j:3 * j + 3] + tok
        g = norm_g_full[i, j].reshape(1, D)
        dx, hb, ab_, dgu, dyb, dm, dg = ffn_bwd(sv["x%d" % j], dx, sv["y%d" % j], m3, g, wl[slot], wl[2 + slot])
        dmod[i][j] = dm
        d_norm_g[i][j] = jnp.sum(dg, axis=(0, 1))
        Fc = wl[slot].shape[2]
        dw_in = matmul_tn(hb.reshape(-1, D), dgu.reshape(2, BL * T, 2 * Fc), Fc, "dw_ffn_in")
        dw_out = matmul_tn(ab_.reshape(-1, 2 * Fc), dyb.reshape(1, -1, D), D, "dw_ffn_out")
        return dx, dw_in, dw_out.reshape(N_CHIPS, -1, D)

    pending, tok = None, 0.0
    for i in reversed(range(L)):
        wl, sv = wts[i], saved[i]
        a = i // 2
        dx, dw_in1, dw_out1 = ffn_back(i, 2, 1, dx, tok)
        m3 = mod[i, :, 3:6]
        g = norm_g_full[i, 1].reshape(1, D)
        if i % 2 == 0:
            w_pw = wl[5].reshape(D, D)
            wdw = pad_taps(w_dw_full[a])
            du2, u3b, dyb, dgate, vec = conv_out_bwd(dx, sv["y1"], sv["u2"], m3, cm_ln_g[a].reshape(1, D),
                                                     cm_ln_b[a].reshape(1, D), w_pw)
            dx, hb, dab, dwdw, dbglu, dm, dg = conv_glu_bwd(sv["x1"], dx, du2, sv["u"], m3, g, wl[4],
                                                            cm_b_glu[a].reshape(1, -1), wdw)
            dm = dm.at[:, 2:3, :].set(dgate)
            vec = jnp.sum(vec, axis=0)
            g_cm["b_pw"][a], g_cm["ln_g"][a], g_cm["ln_b"][a], g_cm["b_dw"][a] = vec[0], vec[1], vec[2], vec[3]
            g_cm["w_dw"][a] = jnp.sum(dwdw, axis=0)[:KC]
            g_cm["b_glu"][a] = jnp.sum(dbglu, axis=(0, 1))
            dw_a = matmul_tn(hb.reshape(-1, D), dab.reshape(1, -1, 2 * D), D // 2, "dw_glu")
            dw_b = matmul_tn(u3b.reshape(-1, D), dyb.reshape(1, -1, D), D, "dw_sq").reshape(N_CHIPS, -1, D)
        else:
            w_main, w_ab = dn_weights(i)
            w_out = wl[5].reshape(W, D)
            do, dz, ogb, dyb, dgate, dog = dn_out_bwd(dx, sv["y1"], sv["o"], sv["z"], m3, dn_o_g[a].reshape(1, Dh), w_out)
            dq, dk, dv, dgb, dbb = dn_chunk_bwd(*sv["qkvgb"], sv["sp"], do)
            dc, dab, small = dn_conv_bwd(dq, dk, dv, dgb, dbb, sv["pre"], sv["ab"], w_sconv_full[a],
                                         row128(dn_a_log[a]), row128(dn_dt_bias[a]))
            dx, hb, dproj, dws, dm, dg = dn_proj_bwd(sv["x1"], dx, dc, sv["pre"], dz, dab, m3, g, w_main, w_ab,
                                                     w_sconv_full[a])
            dm = dm.at[:, 2:3, :].set(dgate)
            small = jnp.sum(small, axis=0)
            g_dn["a_log"][a], g_dn["dt_bias"][a] = small[0, :H], small[1, :H]
            g_dn["o_g"][a] = jnp.sum(dog, axis=(0, 1))
            g_dn["w_sconv"][a] = jnp.sum(dws, axis=0)
            dw_main = matmul_tn(hb.reshape(-1, D), dproj.reshape(1, -1, 4 * W), W, "dw_dn_main")
            dw_ab = matmul_tn(hb.reshape(-1, D), dab.reshape(1, -1, LANES), LANES, "dw_dn_ab")
            full = jnp.concatenate([jnp.transpose(dw_main, (1, 0, 2)).reshape(D, 4 * W), dw_ab[0][:, :2 * H]], axis=1)
            dw_a = jnp.transpose(full.reshape(D, N_CHIPS, -1), (1, 0, 2))
            dw_b = matmul_tn(ogb.reshape(-1, W), dyb.reshape(1, -1, D), D, "dw_sq").reshape(N_CHIPS, -1, D)
        dmod[i][1] = dm
        d_norm_g[i][1] = jnp.sum(dg, axis=(0, 1))
        if i > 0:
            dx, dw_in0, dw_out0 = ffn_back(i, 0, 0, dx)
            started = reduce_start([dw_in0, dw_in1, dw_out0, dw_out1, dw_a, dw_b], c_idx, "reduce_start_%d" % i)
            if pending is not None:
                big[pending[1]] = reduce_finish(pending[0], dx, where, "reduce_wait_%d" % pending[1])
            pending, tok = (started, i), started[4][0, 0]
        else:
            part_a = reduce_start([dw_in1, dw_out1, dw_a, dw_b], c_idx, "reduce_start_0a")
            if pending is not None:
                big[pending[1]] = reduce_finish(pending[0], dx, where, "reduce_wait_%d" % pending[1])
            dx, dw_in0, dw_out0 = ffn_back(0, 0, 0, dx, part_a[4][0, 0])
            sums_a = reduce_finish(part_a, dx, where, "reduce_wait_0a")

    part = dict(
        norm_g=jnp.stack([jnp.stack(r) for r in d_norm_g]),
        cm_b_glu=jnp.stack(g_cm["b_glu"]), cm_w_dw=jnp.stack(g_cm["w_dw"]), cm_b_dw=jnp.stack(g_cm["b_dw"]),
        cm_ln_g=jnp.stack(g_cm["ln_g"]), cm_ln_b=jnp.stack(g_cm["ln_b"]), cm_b_pw=jnp.stack(g_cm["b_pw"]),
        dn_w_sconv=jnp.stack(g_dn["w_sconv"]), dn_a_log=jnp.stack(g_dn["a_log"]), dn_dt_bias=jnp.stack(g_dn["dt_bias"]),
        dn_o_g=jnp.stack(g_dn["o_g"]), final_g=jnp.sum(d_final_g, axis=(0, 1)),
        loss=jnp.sum(loss_part[:, 0, 0]).reshape(1))
    dmod_loc = jnp.stack([jnp.concatenate(r, axis=1) for r in dmod]).reshape(L, BL, C9)
    keys = list(part)
    packed = _pack([part[k] for k in keys] + [dmod_loc])
    R = packed.shape[0]
    gathered = allgather8(packed).reshape(N_DEV, R, LANES)
    summed = _unpack(sum_devices(gathered).reshape(-1), [part[k].shape for k in keys])
    tot = dict(zip(keys, summed))
    n_small = sum(int(part[k].size) for k in keys)
    dmod_all = gathered.reshape(N_DEV, -1)[:, n_small:n_small + L * BL * C9].reshape(N_DEV, L, BL, C9)
    dmod_all = jnp.transpose(dmod_all, (1, 0, 2, 3)).reshape(L, NB, C9)
    dmod_cols = lax.dynamic_slice_in_dim(dmod_all, chip * Ca, Ca, axis=2)
    g_w_ada, g_b_ada = ada_bwd(c_all, dmod_cols, dmod_all)
    delta, new_m, new_v = {}, {}, {}
    part_b = reduce_start([dw_in0, dw_out0], c_idx, "reduce_start_0b", g_w_ada)
    delta["w_ada"], new_m["w_ada"], new_v["w_ada"] = _adamw_any(w_ada, g_w_ada, m_w_ada, v_w_ada, "adamw_w_ada",
                                                                 part_b[4])
    sums_b = reduce_finish(part_b, new_v["w_ada"], where, "reduce_wait_0b")
    big[0] = [sums_b[0], sums_a[0], sums_b[1], sums_a[1], sums_a[2], sums_a[3]]

    def my_cols(full):
        n = full.shape[-1] // N_CHIPS
        return lax.dynamic_slice_in_dim(full, chip * n, n, axis=full.ndim - 1)

    grads = dict(
        norm_g=my_cols(tot["norm_g"]), w_ada=g_w_ada, b_ada=g_b_ada.reshape(L, C9),
        w_ffn_in=jnp.stack([jnp.stack([big[i][0], big[i][1]]) for i in range(L)]),
        w_ffn_out=jnp.stack([jnp.stack([big[i][2], big[i][3]]) for i in range(L)]),
        cm_w_glu=jnp.stack([big[i][4] for i in range(0, L, 2)]), cm_b_glu=tot["cm_b_glu"], cm_w_dw=my_cols(tot["cm_w_dw"]),
        cm_b_dw=tot["cm_b_dw"], cm_ln_g=tot["cm_ln_g"], cm_ln_b=tot["cm_ln_b"],
        cm_w_pw=jnp.stack([big[i][5] for i in range(0, L, 2)]), cm_b_pw=tot["cm_b_pw"],
        dn_w_in=jnp.stack([big[i][4] for i in range(1, L, 2)]), dn_w_sconv=my_cols(tot["dn_w_sconv"]),
        dn_a_log=tot["dn_a_log"], dn_dt_bias=tot["dn_dt_bias"], dn_o_g=tot["dn_o_g"],
        dn_w_out=jnp.stack([big[i][5] for i in range(1, L, 2)]), final_g=tot["final_g"])

    large = ("w_ada", "w_ffn_in", "w_ffn_out", "cm_w_glu", "cm_w_pw", "dn_w_in", "dn_w_out")
    for n in large[1:]:
        delta[n], new_m[n], new_v[n] = _adamw_any(weights[n], grads[n], mom_m[n], mom_v[n], "adamw_" + n)
    rest = [n for n in names if n not in large]
    shapes = [weights[n].shape for n in rest]
    pd, pm, pv = adamw(_pack([weights[n] for n in rest]), _pack([grads[n] for n in rest]),
                       _pack([mom_m[n] for n in rest]), _pack([mom_v[n] for n in rest]), "adamw_small")
    for n, d_, m_, v_ in zip(rest, _unpack(pd.reshape(-1), shapes), _unpack(pm.reshape(-1), shapes),
                             _unpack(pv.reshape(-1), shapes)):
        delta[n], new_m[n], new_v[n] = d_, m_, v_

    return (tot["loss"].reshape(()), dx, *[grads[n] for n in names], *[delta[n] for n in names],
            *[new_m[n] for n in names], *[new_v[n] for n in names])
```

```python
import functools

import jax
import jax.numpy as jnp
from jax import lax
from jax.experimental import pallas as pl
from jax.experimental.pallas import tpu as pltpu

F32 = jnp.float32
BF16 = jnp.bfloat16
EPS = 1e-6
CHUNK = 64
CHUNKS_PER_STEP = 4
N_CHIPS = 4
N_DEV = 8
LANES = 128
SUBLANES = 8
CONV_HALO = 32
SCONV_HALO = 8
VMEM_LIMIT_V7X = 60 * 1024 * 1024
HI = lax.Precision.HIGHEST
MESH = pl.DeviceIdType.MESH
HBM_SPEC = pl.BlockSpec(memory_space=pltpu.HBM)

ADAM_LR, ADAM_B1, ADAM_B2, ADAM_EPS, ADAM_WD, ADAM_STEP = 0.001, 0.9, 0.999, 1e-08, 0.01, 10


def _cparams(n_axes):
    return pltpu.CompilerParams(dimension_semantics=("arbitrary",) * n_axes, vmem_limit_bytes=VMEM_LIMIT_V7X)


def _tile(n, pref, mult=8):
    for t in range(min(n, pref) // mult * mult, 0, -mult):
        if n % t == 0:
            return t
    return n


def _mm(a, b):
    return lax.dot_general(a.astype(BF16), b.astype(BF16), (((1,), (0,)), ((), ())), preferred_element_type=F32)


def _mm_nt(a, b):
    return lax.dot_general(a.astype(BF16), b.astype(BF16), (((1,), (1,)), ((), ())), preferred_element_type=F32)


def _mm_tn(a, b):
    return lax.dot_general(a.astype(BF16), b.astype(BF16), (((0,), (0,)), ((), ())), preferred_element_type=F32)


def _sigmoid(x):
    return jax.nn.sigmoid(x)


def _dsilu(x, s):
    return s * (1.0 + x * (1.0 - s))


def _softplus(x):
    return jnp.maximum(x, 0.0) + jnp.log(1.0 + jnp.exp(-jnp.abs(x)))


def _modnorm(x, g, scale, shift):
    r = lax.rsqrt(jnp.mean(x * x, axis=-1, keepdims=True) + EPS)
    return (x * r) * g * (1.0 + scale) + shift


def _modnorm_bwd(x, g, scale, dh):
    r = lax.rsqrt(jnp.mean(x * x, axis=-1, keepdims=True) + EPS)
    xn = x * r
    dshift = jnp.sum(dh, axis=0, keepdims=True)
    dscale = jnp.sum(dh * (xn * g), axis=0, keepdims=True)
    dhn = dh * (1.0 + scale)
    dg = jnp.sum(dhn * xn, axis=0, keepdims=True)
    dxn = dhn * g
    dx = r * (dxn - xn * jnp.mean(dxn * xn, axis=-1, keepdims=True))
    return dx, dg, dscale, dshift


def _sum0(a):
    return jnp.sum(a, axis=0, keepdims=True)


def ffn_fwd(x, mod3, g, w_in, w_out):
    B, T, D = x.shape
    Fc = w_in.shape[2]
    w_in = w_in.reshape(2, 2, D, Fc)
    w_out = w_out.reshape(2, Fc, D)
    tm = _tile(T, 512)

    def body(x_ref, mod_ref, g_ref, wi_ref, wo_ref, xo_ref, y_ref, h_ref, gu_ref, acc_s):
        f = pl.program_id(2)

        @pl.when(f == 0)
        def _():
            h = _modnorm(x_ref[...], g_ref[...], mod_ref[1:2, :], mod_ref[0:1, :])
            h_ref[...] = h.astype(BF16)
            acc_s[...] = jnp.zeros_like(acc_s)

        h = h_ref[...]
        gt = _mm(h, wi_ref[0])
        up = _mm(h, wi_ref[1])
        gu_ref[0] = gt.astype(BF16)
        gu_ref[1] = up.astype(BF16)
        a = gt * _sigmoid(gt) * up
        acc_s[...] += _mm(a, wo_ref[...])

        @pl.when(f == 1)
        def _():
            y = acc_s[...]
            y_ref[...] = y
            xo_ref[...] = x_ref[...] + 0.5 * (1.0 + mod_ref[2:3, :]) * y

    tok = pl.BlockSpec((None, tm, D), lambda b, t, f: (b, t, 0))
    return pl.pallas_call(
        body, name="ffn_fwd", grid=(B, T // tm, 2),
        in_specs=[tok,
                  pl.BlockSpec((None, 3, D), lambda b, t, f: (b, 0, 0)),
                  pl.BlockSpec((1, D), lambda b, t, f: (0, 0)),
                  pl.BlockSpec((2, None, D, Fc), lambda b, t, f: (0, f, 0, 0)),
                  pl.BlockSpec((None, Fc, D), lambda b, t, f: (f, 0, 0))],
        out_specs=[tok, tok, tok, pl.BlockSpec((2, None, tm, Fc), lambda b, t, f: (0, b, t, f))],
        out_shape=[jax.ShapeDtypeStruct((B, T, D), F32)] * 2
        + [jax.ShapeDtypeStruct((B, T, D), BF16), jax.ShapeDtypeStruct((2, B, T, 2 * Fc), BF16)],
        scratch_shapes=[pltpu.VMEM((tm, D), F32)],
        compiler_params=_cparams(3),
    )(x, mod3, g, w_in, w_out)


def ffn_bwd(x, dres, y, gu, mod3, g, w_in, w_out):
    B, T, D = x.shape
    Fc = w_in.shape[2]
    F = 2 * Fc
    w_in = w_in.reshape(2, 2, D, Fc)
    w_out = w_out.reshape(2, Fc, D)
    tm = _tile(T, 256)

    def body(x_ref, dres_ref, y_ref, gu_ref, mod_ref, g_ref, wi_ref, wo_ref,
             dx_ref, a_ref, dgu_ref, dy_ref, dmod_ref, dg_ref, dy_s, dh_s):
        t, f = pl.program_id(1), pl.program_id(2)

        @pl.when(f == 0)
        def _():
            dres = dres_ref[...]
            dy = (0.5 * (1.0 + mod_ref[2:3, :]) * dres).astype(BF16)
            dy_s[...] = dy
            dy_ref[...] = dy
            dh_s[...] = jnp.zeros_like(dh_s)
            dgate = _sum0(dres * (0.5 * y_ref[...]))

            @pl.when(t == 0)
            def _():
                dmod_ref[...] = jnp.zeros_like(dmod_ref)
                dg_ref[...] = jnp.zeros_like(dg_ref)

            dmod_ref[2:3, :] += dgate

        dy = dy_s[...]
        gt = gu_ref[0].astype(F32)
        up = gu_ref[1].astype(F32)
        sg = _sigmoid(gt)
        silu = gt * sg
        a_ref[...] = (silu * up).astype(BF16)
        da = _mm_nt(dy, wo_ref[...])
        dup = (da * silu).astype(BF16)
        dgt = (da * up * _dsilu(gt, sg)).astype(BF16)
        dgu_ref[0] = dgt
        dgu_ref[1] = dup
        dh_s[...] += _mm_nt(dgt, wi_ref[0]) + _mm_nt(dup, wi_ref[1])

        @pl.when(f == 1)
        def _():
            dxn, dg, dscale, dshift = _modnorm_bwd(x_ref[...], g_ref[...], mod_ref[1:2, :], dh_s[...])
            dx_ref[...] = dres_ref[...] + dxn
            dmod_ref[0:1, :] += dshift
            dmod_ref[1:2, :] += dscale
            dg_ref[...] += dg

    tok = pl.BlockSpec((None, tm, D), lambda b, t, f: (b, t, 0))
    per_b3 = pl.BlockSpec((None, 3, D), lambda b, t, f: (b, 0, 0))
    gu_spec = pl.BlockSpec((2, None, tm, Fc), lambda b, t, f: (0, b, t, f))
    return pl.pallas_call(
        body, name="ffn_bwd", grid=(B, T // tm, 2),
        in_specs=[tok, tok, tok, gu_spec, per_b3,
                  pl.BlockSpec((1, D), lambda b, t, f: (0, 0)),
                  pl.BlockSpec((2, None, D, Fc), lambda b, t, f: (0, f, 0, 0)),
                  pl.BlockSpec((None, Fc, D), lambda b, t, f: (f, 0, 0))],
        out_specs=[tok,
                   pl.BlockSpec((None, tm, Fc), lambda b, t, f: (b, t, f)),
                   gu_spec, tok, per_b3,
                   pl.BlockSpec((None, 1, D), lambda b, t, f: (b, 0, 0))],
        out_shape=[jax.ShapeDtypeStruct((B, T, D), F32),
                   jax.ShapeDtypeStruct((B, T, F), BF16), jax.ShapeDtypeStruct((2, B, T, F), BF16),
                   jax.ShapeDtypeStruct((B, T, D), BF16), jax.ShapeDtypeStruct((B, 3, D), F32),
                   jax.ShapeDtypeStruct((B, 1, D), F32)],
        scratch_shapes=[pltpu.VMEM((tm, D), BF16), pltpu.VMEM((tm, D), F32)],
        compiler_params=_cparams(3),
    )(x, dres, y, gu, mod3, g, w_in, w_out)


def matmul_tn(xm, ym, bm, name):
    N, K = xm.shape
    GY, _, MY = ym.shape
    per = MY // bm
    nb = GY * per
    tn = _tile(N, 512)

    def body(x_ref, y_ref, o_ref, acc_s):
        n = pl.program_id(1)

        @pl.when(n == 0)
        def _():
            acc_s[...] = jnp.zeros_like(acc_s)

        acc_s[...] += _mm_tn(x_ref[...], y_ref[...])

        @pl.when(n == N // tn - 1)
        def _():
            o_ref[...] = acc_s[...].astype(BF16)

    return pl.pallas_call(
        body, name=name, grid=(nb, N // tn),
        in_specs=[pl.BlockSpec((tn, K), lambda m, n: (n, 0)),
                  pl.BlockSpec((None, tn, bm), lambda m, n: (m // per, n, m % per))],
        out_specs=pl.BlockSpec((None, K, bm), lambda m, n: (m, 0, 0)),
        out_shape=jax.ShapeDtypeStruct((nb, K, bm), BF16),
        scratch_shapes=[pltpu.VMEM((K, bm), F32)],
        compiler_params=_cparams(2),
    )(xm, ym)


def final_loss(x, fg, target):
    B, T, D = x.shape
    tm = _tile(T, 512)

    def body(x_ref, g_ref, t_ref, dx_ref, dfg_ref, loss_ref):
        t = pl.program_id(1)

        @pl.when(t == 0)
        def _():
            dfg_ref[...] = jnp.zeros_like(dfg_ref)
            loss_ref[...] = jnp.zeros_like(loss_ref)

        xv = x_ref[...]
        g = g_ref[...]
        r = lax.rsqrt(jnp.mean(xv * xv, axis=-1, keepdims=True) + EPS)
        xn = xv * r
        err = xn * g - t_ref[...]
        tok_loss = jnp.mean(err * err, axis=-1, keepdims=True)
        loss_ref[...] += 0.5 * jnp.sum(tok_loss, axis=0, keepdims=True)
        dy = err * (1.0 / D)
        dfg_ref[...] += _sum0(dy * xn)
        dxn = dy * g
        dx_ref[...] = r * (dxn - xn * jnp.mean(dxn * xn, axis=-1, keepdims=True))

    tok = pl.BlockSpec((None, tm, D), lambda b, t: (b, t, 0))
    return pl.pallas_call(
        body, name="final_loss", grid=(B, T // tm),
        in_specs=[tok, pl.BlockSpec((1, D), lambda b, t: (0, 0)), tok],
        out_specs=[tok, pl.BlockSpec((None, 1, D), lambda b, t: (b, 0, 0)),
                   pl.BlockSpec((None, 1, LANES), lambda b, t: (b, 0, 0))],
        out_shape=[jax.ShapeDtypeStruct((B, T, D), F32), jax.ShapeDtypeStruct((B, 1, D), F32),
                   jax.ShapeDtypeStruct((B, 1, LANES), F32)],
        compiler_params=_cparams(2),
    )(x, fg, target)


def _past_halo_spec(tm, halo, width):
    return pl.BlockSpec((None, halo, width), lambda b, t: (b, jnp.maximum(t * (tm // halo) - 1, 0), 0))


def _future_halo_spec(tm, halo, width, T):
    return pl.BlockSpec((None, halo, width), lambda b, t: (b, jnp.minimum((t + 1) * (tm // halo), T // halo - 1), 0))


def _fill_shifted(ext_s):
    n = ext_s.shape[1]
    for b in range(1, SUBLANES):
        ext_s[b, 0:n - SUBLANES, :] = ext_s[0, pl.ds(b, n - SUBLANES), :]


def _shifted(ext_s, offset, rows):
    a, b = divmod(offset, SUBLANES)
    return ext_s[b, pl.ds(SUBLANES * a, rows), :]


def _glu_fwd(h, w_ref, bias):
    D = h.shape[1]
    a = jnp.concatenate([_mm(h, w_ref[0]), _mm(h, w_ref[1])], axis=1) + bias[:, :D]
    b = jnp.concatenate([_mm(h, w_ref[2]), _mm(h, w_ref[3])], axis=1) + bias[:, D:]
    return a, b


def conv_glu_fwd(x, mod3, g, w_glu, b_glu):
    B, T, D = x.shape
    tm = _tile(T, 512)

    def body(x_ref, mod_ref, g_ref, w_ref, b_ref, u_ref):
        h = _modnorm(x_ref[...], g_ref[...], mod_ref[1:2, :], mod_ref[0:1, :]).astype(BF16)
        a, b = _glu_fwd(h, w_ref, b_ref[...])
        u_ref[...] = a * _sigmoid(b)

    tok = pl.BlockSpec((None, tm, D), lambda b, t: (b, t, 0))
    return pl.pallas_call(
        body, name="conv_glu_fwd", grid=(B, T // tm),
        in_specs=[tok, pl.BlockSpec((None, 3, D), lambda b, t: (b, 0, 0)),
                  pl.BlockSpec((1, D), lambda b, t: (0, 0)),
                  pl.BlockSpec((4, D, D // 2), lambda b, t: (0, 0, 0)),
                  pl.BlockSpec((1, 2 * D), lambda b, t: (0, 0))],
        out_specs=tok, out_shape=jax.ShapeDtypeStruct((B, T, D), F32),
        compiler_params=_cparams(2),
    )(x, mod3, g, w_glu, b_glu)


def _layer_norm_parts(u2):
    mu = jnp.mean(u2, axis=-1, keepdims=True)
    xc = u2 - mu
    rs = lax.rsqrt(jnp.mean(xc * xc, axis=-1, keepdims=True) + EPS)
    return xc * rs, rs


def conv_out_fwd(x, u, mod3, w_dw, b_dw, ln_g, ln_b, w_pw, b_pw):
    B, T, D = x.shape
    K = w_dw.shape[0] - 1
    tm = _tile(T, 512)

    def body(x_ref, u_ref, halo_ref, mod_ref, wdw_ref, bdw_ref, lg_ref, lb_ref, wpw_ref, bpw_ref,
             xo_ref, y_ref, u2_ref, ext_s):
        t = pl.program_id(1)
        ext_s[0, 0:CONV_HALO, :] = jnp.where(t > 0, halo_ref[...], 0.0)
        ext_s[0, CONV_HALO:, :] = u_ref[...]
        _fill_shifted(ext_s)
        acc = jnp.broadcast_to(bdw_ref[...], (tm, D))
        for k in range(K):
            acc = acc + wdw_ref[k:k + 1, :] * _shifted(ext_s, CONV_HALO - (K - 1) + k, tm)
        u2_ref[...] = acc
        xh, _ = _layer_norm_parts(acc)
        l = xh * lg_ref[...] + lb_ref[...]
        u3 = l * _sigmoid(l)
        y = _mm(u3, wpw_ref[...]) + bpw_ref[...]
        y_ref[...] = y
        xo_ref[...] = x_ref[...] + (1.0 + mod_ref[2:3, :]) * y

    tok = pl.BlockSpec((None, tm, D), lambda b, t: (b, t, 0))
    vec = pl.BlockSpec((1, D), lambda b, t: (0, 0))
    return pl.pallas_call(
        body, name="conv_out_fwd", grid=(B, T // tm),
        in_specs=[tok, tok, _past_halo_spec(tm, CONV_HALO, D), pl.BlockSpec((None, 3, D), lambda b, t: (b, 0, 0)),
                  pl.BlockSpec((K + 1, D), lambda b, t: (0, 0)), vec, vec, vec,
                  pl.BlockSpec((D, D), lambda b, t: (0, 0)), vec],
        out_specs=[tok, tok, tok], out_shape=[jax.ShapeDtypeStruct((B, T, D), F32)] * 3,
        scratch_shapes=[pltpu.VMEM((SUBLANES, tm + CONV_HALO, D), F32)],
        compiler_params=_cparams(2),
    )(x, u, u, mod3, w_dw, b_dw, ln_g, ln_b, w_pw, b_pw)


def conv_out_bwd(dres, y, u2, mod3, ln_g, ln_b, w_pw):
    B, T, D = dres.shape
    tm = _tile(T, 512)

    def body(dres_ref, y_ref, u2_ref, mod_ref, lg_ref, lb_ref, wpw_ref, du2_ref, u3_ref, dy_ref, dgate_ref, vec_ref):
        t = pl.program_id(1)

        @pl.when(t == 0)
        def _():
            dgate_ref[...] = jnp.zeros_like(dgate_ref)
            vec_ref[...] = jnp.zeros_like(vec_ref)

        dres = dres_ref[...]
        dy = (1.0 + mod_ref[2:3, :]) * dres
        dy_ref[...] = dy.astype(BF16)
        dgate_ref[...] += _sum0(dres * y_ref[...])
        xh, rs = _layer_norm_parts(u2_ref[...])
        lg = lg_ref[...]
        l = xh * lg + lb_ref[...]
        sg = _sigmoid(l)
        u3_ref[...] = (l * sg).astype(BF16)
        du3 = _mm_nt(dy, wpw_ref[...])
        dl = du3 * _dsilu(l, sg)
        dxh = dl * lg
        du2 = rs * (dxh - jnp.mean(dxh, axis=-1, keepdims=True) - xh * jnp.mean(dxh * xh, axis=-1, keepdims=True))
        du2_ref[...] = du2
        vec_ref[0:1, :] += _sum0(dy)
        vec_ref[1:2, :] += _sum0(dl * xh)
        vec_ref[2:3, :] += _sum0(dl)
        vec_ref[3:4, :] += _sum0(du2)

    tok = pl.BlockSpec((None, tm, D), lambda b, t: (b, t, 0))
    tokb = pl.BlockSpec((None, tm, D), lambda b, t: (b, t, 0))
    vec = pl.BlockSpec((1, D), lambda b, t: (0, 0))
    return pl.pallas_call(
        body, name="conv_out_bwd", grid=(B, T // tm),
        in_specs=[tok, tok, tok, pl.BlockSpec((None, 3, D), lambda b, t: (b, 0, 0)), vec, vec,
                  pl.BlockSpec((D, D), lambda b, t: (0, 0))],
        out_specs=[tok, tokb, tokb, pl.BlockSpec((None, 1, D), lambda b, t: (b, 0, 0)),
                   pl.BlockSpec((None, 4, D), lambda b, t: (b, 0, 0))],
        out_shape=[jax.ShapeDtypeStruct((B, T, D), F32), jax.ShapeDtypeStruct((B, T, D), BF16),
                   jax.ShapeDtypeStruct((B, T, D), BF16), jax.ShapeDtypeStruct((B, 1, D), F32),
                   jax.ShapeDtypeStruct((B, 4, D), F32)],
        compiler_params=_cparams(2),
    )(dres, y, u2, mod3, ln_g, ln_b, w_pw)


def conv_glu_bwd(x, dres, du2, u, mod3, g, w_glu, b_glu, w_dw):
    B, T, D = x.shape
    K = w_dw.shape[0] - 1
    tm = _tile(T, 256)
    nt = T // tm

    def body(x_ref, dres_ref, du2_ref, du2h_ref, u_ref, uh_ref, mod_ref, g_ref, w_ref, b_ref, wdw_ref,
             dx_ref, h_ref, dab_ref, dwdw_ref, dbglu_ref, dmod_ref, dg_ref, extu_s, extd_s):
        t = pl.program_id(1)

        @pl.when(t == 0)
        def _():
            dwdw_ref[...] = jnp.zeros_like(dwdw_ref)
            dbglu_ref[...] = jnp.zeros_like(dbglu_ref)
            dmod_ref[...] = jnp.zeros_like(dmod_ref)
            dg_ref[...] = jnp.zeros_like(dg_ref)

        du2 = du2_ref[...]
        extu_s[0, 0:CONV_HALO, :] = jnp.where(t > 0, uh_ref[...], 0.0)
        extu_s[0, CONV_HALO:, :] = u_ref[...]
        extd_s[0, 0:tm, :] = du2
        extd_s[0, tm:, :] = jnp.where(t < nt - 1, du2h_ref[...], 0.0)
        _fill_shifted(extu_s)
        _fill_shifted(extd_s)
        du = jnp.zeros((tm, D), F32)
        for k in range(K):
            du = du + wdw_ref[k:k + 1, :] * _shifted(extd_s, K - 1 - k, tm)
            dwdw_ref[k:k + 1, :] += _sum0(du2 * _shifted(extu_s, CONV_HALO - (K - 1) + k, tm))
        xv = x_ref[...]
        h = _modnorm(xv, g_ref[...], mod_ref[1:2, :], mod_ref[0:1, :]).astype(BF16)
        h_ref[...] = h
        a, b = _glu_fwd(h, w_ref, b_ref[...])
        sb = _sigmoid(b)
        da = du * sb
        db = du * a * sb * (1.0 - sb)
        dbglu_ref[:, 0:D] += _sum0(da)
        dbglu_ref[:, D:] += _sum0(db)
        da = da.astype(BF16)
        db = db.astype(BF16)
        dab_ref[:, 0:D] = da
        dab_ref[:, D:] = db
        Dh2 = D // 2
        dh = (_mm_nt(da[:, :Dh2], w_ref[0]) + _mm_nt(da[:, Dh2:], w_ref[1])
              + _mm_nt(db[:, :Dh2], w_ref[2]) + _mm_nt(db[:, Dh2:], w_ref[3]))
        dxn, dg, dscale, dshift = _modnorm_bwd(xv, g_ref[...], mod_ref[1:2, :], dh)
        dx_ref[...] = dres_ref[...] + dxn
        dmod_ref[0:1, :] += dshift
        dmod_ref[1:2, :] += dscale
        dg_ref[...] += dg

    tok = pl.BlockSpec((None, tm, D), lambda b, t: (b, t, 0))
    return pl.pallas_call(
        body, name="conv_glu_bwd", grid=(B, nt),
        in_specs=[tok, tok, tok, _future_halo_spec(tm, CONV_HALO, D, T), tok, _past_halo_spec(tm, CONV_HALO, D),
                  pl.BlockSpec((None, 3, D), lambda b, t: (b, 0, 0)), pl.BlockSpec((1, D), lambda b, t: (0, 0)),
                  pl.BlockSpec((4, D, D // 2), lambda b, t: (0, 0, 0)), pl.BlockSpec((1, 2 * D), lambda b, t: (0, 0)),
                  pl.BlockSpec((K + 1, D), lambda b, t: (0, 0))],
        out_specs=[tok, tok, pl.BlockSpec((None, tm, 2 * D), lambda b, t: (b, t, 0)),
                   pl.BlockSpec((None, K + 1, D), lambda b, t: (b, 0, 0)),
                   pl.BlockSpec((None, 1, 2 * D), lambda b, t: (b, 0, 0)),
                   pl.BlockSpec((None, 3, D), lambda b, t: (b, 0, 0)),
                   pl.BlockSpec((None, 1, D), lambda b, t: (b, 0, 0))],
        out_shape=[jax.ShapeDtypeStruct((B, T, D), F32), jax.ShapeDtypeStruct((B, T, D), BF16),
                   jax.ShapeDtypeStruct((B, T, 2 * D), BF16), jax.ShapeDtypeStruct((B, K + 1, D), F32),
                   jax.ShapeDtypeStruct((B, 1, 2 * D), F32), jax.ShapeDtypeStruct((B, 3, D), F32),
                   jax.ShapeDtypeStruct((B, 1, D), F32)],
        scratch_shapes=[pltpu.VMEM((SUBLANES, tm + CONV_HALO, D), F32)] * 2,
        compiler_params=_cparams(2),
    )(x, dres, du2, du2, u, u, mod3, g, w_glu, b_glu, w_dw)


def dn_proj_fwd(x, mod3, g, w_main, w_ab):
    B, T, D = x.shape
    W = w_main.shape[1] // 4
    tm = _tile(T, 512)

    def body(x_ref, mod_ref, g_ref, wm_ref, wab_ref, pre_ref, z_ref, ab_ref):
        h = _modnorm(x_ref[...], g_ref[...], mod_ref[1:2, :], mod_ref[0:1, :]).astype(BF16)
        for p in range(3):
            pre_ref[:, p * W:(p + 1) * W] = _mm(h, wm_ref[:, p * W:(p + 1) * W])
        z_ref[...] = _mm(h, wm_ref[:, 3 * W:])
        ab_ref[...] = _mm(h, wab_ref[...])

    return pl.pallas_call(
        body, name="dn_proj_fwd", grid=(B, T // tm),
        in_specs=[pl.BlockSpec((None, tm, D), lambda b, t: (b, t, 0)), pl.BlockSpec((None, 3, D), lambda b, t: (b, 0, 0)),
                  pl.BlockSpec((1, D), lambda b, t: (0, 0)), pl.BlockSpec((D, 4 * W), lambda b, t: (0, 0)),
                  pl.BlockSpec((D, LANES), lambda b, t: (0, 0))],
        out_specs=[pl.BlockSpec((None, tm, 3 * W), lambda b, t: (b, t, 0)),
                   pl.BlockSpec((None, tm, W), lambda b, t: (b, t, 0)),
                   pl.BlockSpec((None, tm, LANES), lambda b, t: (b, t, 0))],
        out_shape=[jax.ShapeDtypeStruct((B, T, 3 * W), F32), jax.ShapeDtypeStruct((B, T, W), F32),
                   jax.ShapeDtypeStruct((B, T, LANES), F32)],
        compiler_params=_cparams(2),
    )(x, mod3, g, w_main, w_ab)


def _sconv(ext_s, w_ref, tm, K):
    acc = w_ref[0:1, :] * ext_s[pl.ds(SCONV_HALO - (K - 1), tm), :]
    for k in range(1, K):
        acc = acc + w_ref[k:k + 1, :] * ext_s[pl.ds(SCONV_HALO - (K - 1) + k, tm), :]
    return acc


def _lane_col(val, lane, idx):
    return jnp.sum(jnp.where(lane == idx, val, 0.0), axis=1, keepdims=True)


def dn_conv_fwd(pre, ab, w_sconv, alog_row, dt_row, H):
    B, T, W3 = pre.shape
    W = W3 // 3
    Dh = W // H
    K = w_sconv.shape[0]
    tm = _tile(T, 512)

    def body(pre_ref, halo_ref, ab_ref, w_ref, alog_ref, dt_ref, q_ref, k_ref, v_ref, gb_ref, bb_ref, ext_s):
        t = pl.program_id(1)
        ext_s[0:SCONV_HALO, :] = jnp.where(t > 0, halo_ref[...], 0.0)
        ext_s[SCONV_HALO:, :] = pre_ref[...]
        cv = _sconv(ext_s, w_ref, tm, K)
        qkv = cv * _sigmoid(cv)
        ab = ab_ref[...]
        lane = lax.broadcasted_iota(jnp.int32, ab.shape, 1)
        g_all = -jnp.exp(alog_ref[...]) * _softplus(ab + dt_ref[...])
        beta_all = _sigmoid(ab)
        for h in range(H):
            q_ref[h] = qkv[:, h * Dh:(h + 1) * Dh]
            k_ref[h] = qkv[:, W + h * Dh:W + (h + 1) * Dh]
            v_ref[h] = qkv[:, 2 * W + h * Dh:2 * W + (h + 1) * Dh]
            gb_ref[h] = jnp.broadcast_to(_lane_col(g_all, lane, h), (tm, Dh))
            bb_ref[h] = jnp.broadcast_to(_lane_col(beta_all, lane, H + h), (tm, Dh))

    hm = pl.BlockSpec((None, H, tm, Dh), lambda b, t: (b, 0, t, 0))
    row = pl.BlockSpec((1, LANES), lambda b, t: (0, 0))
    return pl.pallas_call(
        body, name="dn_conv_fwd", grid=(B, T // tm),
        in_specs=[pl.BlockSpec((None, tm, W3), lambda b, t: (b, t, 0)), _past_halo_spec(tm, SCONV_HALO, W3),
                  pl.BlockSpec((None, tm, LANES), lambda b, t: (b, t, 0)),
                  pl.BlockSpec((K, W3), lambda b, t: (0, 0)), row, row],
        out_specs=[hm] * 5, out_shape=[jax.ShapeDtypeStruct((B, H, T, Dh), F32)] * 5,
        scratch_shapes=[pltpu.VMEM((tm + SCONV_HALO, W3), F32)],
        compiler_params=_cparams(2),
    )(pre, pre, ab, w_sconv, alog_row, dt_row)


def _bdot(spec):
    return lambda a, b: jnp.einsum(spec, a.astype(BF16), b.astype(BF16), preferred_element_type=F32)


_NN, _NT, _TN = "gij,gjk->gik", "gik,gjk->gij", "gki,gkj->gij"


def _make_bdots():
    nn_, nt_, tn_ = _bdot(_NN), _bdot(_NT), _bdot(_TN)

    @jax.custom_vjp
    def nn(a, b):
        return nn_(a, b)

    @jax.custom_vjp
    def nt(a, b):
        return nt_(a, b)

    @jax.custom_vjp
    def tn(a, b):
        return tn_(a, b)

    nn.defvjp(lambda a, b: (nn_(a, b), (a, b)), lambda r, d: (nt_(d, r[1]), tn_(r[0], d)))
    nt.defvjp(lambda a, b: (nt_(a, b), (a, b)), lambda r, d: (nn_(d, r[1]), tn_(d, r[0])))
    tn.defvjp(lambda a, b: (tn_(a, b), (a, b)), lambda r, d: (nt_(r[1], d), nn_(r[0], d)))
    return nn, nt, tn


def _unit_lower_inverse(A):
    hdot = functools.partial(jnp.einsum, precision=lax.Precision.HIGH, preferred_element_type=F32)
    C = A.shape[-1]

    def impl(A):
        eye = (lax.broadcasted_iota(jnp.int32, A.shape, 1) == lax.broadcasted_iota(jnp.int32, A.shape, 2)).astype(F32)
        Tm = eye - A
        Ap = A
        for _ in range(max(1, (C - 1).bit_length()) - 1):
            Ap = hdot(_NN, Ap, Ap)
            Tm = Tm + hdot(_NN, Tm, Ap)
        return Tm

    @jax.custom_vjp
    def inv(A):
        return impl(A)

    def fwd(A):
        Tm = impl(A)
        return Tm, Tm

    def bwd(Tm, dT):
        return (-hdot(_NT, hdot(_TN, Tm, dT), Tm),)

    inv.defvjp(fwd, bwd)
    return inv(A)


def _chunk_fn(q, k, v, gb, bb, S):
    nn, nt, tn = _make_bdots()
    G, C, Dh = q.shape
    hdot = functools.partial(jnp.einsum, precision=lax.Precision.HIGH, preferred_element_type=F32)
    q = q * lax.rsqrt(jnp.sum(q * q, axis=-1, keepdims=True) + EPS) * (Dh ** -0.5)
    k = k * lax.rsqrt(jnp.sum(k * k, axis=-1, keepdims=True) + EPS)
    row = lax.broadcasted_iota(jnp.int32, (G, C, C), 1)
    col = lax.broadcasted_iota(jnp.int32, (G, C, C), 2)
    causal = row >= col
    strict = row > col
    gc = hdot(_NN, causal.astype(F32), gb)
    spread = jnp.full((G, C, Dh), 1.0 / Dh, F32)
    gi = hdot(_NT, gc, spread)
    gj = hdot(_NT, spread, gc)
    decay = jnp.where(causal, jnp.exp(jnp.where(causal, gi - gj, 0.0)), 0.0)
    kb = k * bb
    vb = v * bb
    A = jnp.where(strict, nt(kb, k) * decay, 0.0)
    Tm = _unit_lower_inverse(A)
    eg = jnp.exp(gc)
    u = nn(Tm, vb)
    w = nn(Tm, kb * eg)
    qg = q * eg
    intra = nt(q, k) * decay
    glast = hdot(_NN, jnp.ones((G, C, C), F32), gb)
    kd = k * jnp.exp(glast - gc)
    v_new = u - nn(w, S)
    o = nn(qg, S) + nn(intra, v_new)
    egl = jnp.exp(glast)
    S_new = S * jnp.concatenate([egl] * (Dh // C), axis=1) + tn(kd, v_new)
    return o, S_new


def dn_chunk_fwd(q, k, v, gb, bb):
    B, H, T, Dh = q.shape
    NC = T // CHUNK
    NS = _tile(NC, CHUNKS_PER_STEP, 1)

    def body(q_ref, k_ref, v_ref, gb_ref, bb_ref, o_ref, sp_ref, S_s):
        @pl.when(pl.program_id(1) == 0)
        def _():
            S_s[...] = jnp.zeros_like(S_s)

        def one_chunk(j, carry):
            rows = pl.ds(pl.multiple_of(j * CHUNK, CHUNK), CHUNK)
            S = S_s[...]
            sp_ref[j] = S
            o, S_new = _chunk_fn(q_ref[:, rows, :], k_ref[:, rows, :], v_ref[:, rows, :], gb_ref[:, rows, :],
                                 bb_ref[:, rows, :], S)
            o_ref[:, rows, :] = o
            S_s[...] = S_new
            return carry

        lax.fori_loop(0, NS, one_chunk, 0)

    hm = pl.BlockSpec((None, H, NS * CHUNK, Dh), lambda b, n: (b, 0, n, 0))
    return pl.pallas_call(
        body, name="dn_chunk_fwd", grid=(B, NC // NS),
        in_specs=[hm] * 5,
        out_specs=[hm, pl.BlockSpec((None, NS, H, Dh, Dh), lambda b, n: (b, n, 0, 0, 0))],
        out_shape=[jax.ShapeDtypeStruct((B, H, T, Dh), F32), jax.ShapeDtypeStruct((B, NC, H, Dh, Dh), F32)],
        scratch_shapes=[pltpu.VMEM((H, Dh, Dh), F32)],
        compiler_params=_cparams(2),
    )(q, k, v, gb, bb)


def dn_chunk_bwd(q, k, v, gb, bb, s_prev, do):
    B, H, T, Dh = q.shape
    NC = T // CHUNK
    NS = _tile(NC, CHUNKS_PER_STEP, 1)
    NG = NC // NS

    def body(q_ref, k_ref, v_ref, gb_ref, bb_ref, sp_ref, do_ref, dq_ref, dk_ref, dv_ref, dgb_ref, dbb_ref, dS_s):
        @pl.when(pl.program_id(1) == 0)
        def _():
            dS_s[...] = jnp.zeros_like(dS_s)

        def one_chunk(jj, carry):
            j = NS - 1 - jj
            rows = pl.ds(pl.multiple_of(j * CHUNK, CHUNK), CHUNK)
            _, vjp = jax.vjp(_chunk_fn, q_ref[:, rows, :], k_ref[:, rows, :], v_ref[:, rows, :], gb_ref[:, rows, :],
                             bb_ref[:, rows, :], sp_ref[j])
            dq, dk, dv, dgb, dbb, dS = vjp((do_ref[:, rows, :], dS_s[...]))
            dq_ref[:, rows, :] = dq
            dk_ref[:, rows, :] = dk
            dv_ref[:, rows, :] = dv
            dgb_ref[:, rows, :] = dgb
            dbb_ref[:, rows, :] = dbb
            dS_s[...] = dS
            return carry

        lax.fori_loop(0, NS, one_chunk, 0)

    hm = pl.BlockSpec((None, H, NS * CHUNK, Dh), lambda b, n: (b, 0, NG - 1 - n, 0))
    return pl.pallas_call(
        body, name="dn_chunk_bwd", grid=(B, NG),
        in_specs=[hm] * 5 + [pl.BlockSpec((None, NS, H, Dh, Dh), lambda b, n: (b, NG - 1 - n, 0, 0, 0)), hm],
        out_specs=[hm] * 5, out_shape=[jax.ShapeDtypeStruct((B, H, T, Dh), F32)] * 5,
        scratch_shapes=[pltpu.VMEM((H, Dh, Dh), F32)],
        compiler_params=_cparams(2),
    )(q, k, v, gb, bb, s_prev, do)


def _head_norm(o, og):
    r = lax.rsqrt(jnp.mean(o * o, axis=-1, keepdims=True) + EPS)
    return o * r, r


def dn_out_fwd(x, o, z, mod3, o_g, w_out):
    B, T, D = x.shape
    _, H, _, Dh = o.shape
    W = H * Dh
    tm = _tile(T, 512)

    def body(x_ref, o_ref, z_ref, mod_ref, og_ref, w_ref, xo_ref, y_ref):
        parts = []
        for h in range(H):
            on, _ = _head_norm(o_ref[h], og_ref[...])
            zz = z_ref[:, h * Dh:(h + 1) * Dh]
            parts.append((on * og_ref[...] * (zz * _sigmoid(zz))).astype(BF16))
        y = _mm(jnp.concatenate(parts, axis=1), w_ref[...])
        y_ref[...] = y
        xo_ref[...] = x_ref[...] + (1.0 + mod_ref[2:3, :]) * y

    tok = pl.BlockSpec((None, tm, D), lambda b, t: (b, t, 0))
    return pl.pallas_call(
        body, name="dn_out_fwd", grid=(B, T // tm),
        in_specs=[tok, pl.BlockSpec((None, H, tm, Dh), lambda b, t: (b, 0, t, 0)),
                  pl.BlockSpec((None, tm, W), lambda b, t: (b, t, 0)), pl.BlockSpec((None, 3, D), lambda b, t: (b, 0, 0)),
                  pl.BlockSpec((1, Dh), lambda b, t: (0, 0)), pl.BlockSpec((W, D), lambda b, t: (0, 0))],
        out_specs=[tok, tok], out_shape=[jax.ShapeDtypeStruct((B, T, D), F32)] * 2,
        compiler_params=_cparams(2),
    )(x, o, z, mod3, o_g, w_out)


def dn_out_bwd(dres, y, o, z, mod3, o_g, w_out):
    B, T, D = dres.shape
    _, H, _, Dh = o.shape
    W = H * Dh
    tm = _tile(T, 512)

    def body(dres_ref, y_ref, o_ref, z_ref, mod_ref, og_ref, w_ref, do_ref, dz_ref, ogb_ref, dy_ref, dgate_ref, dog_ref):
        t = pl.program_id(1)

        @pl.when(t == 0)
        def _():
            dgate_ref[...] = jnp.zeros_like(dgate_ref)
            dog_ref[...] = jnp.zeros_like(dog_ref)

        dres = dres_ref[...]
        dy = ((1.0 + mod_ref[2:3, :]) * dres).astype(BF16)
        dy_ref[...] = dy
        dgate_ref[...] += _sum0(dres * y_ref[...])
        dog = _mm_nt(dy, w_ref[...])
        og = og_ref[...]
        for h in range(H):
            ov = o_ref[h]
            xn, r = _head_norm(ov, og)
            zz = z_ref[:, h * Dh:(h + 1) * Dh]
            sg = _sigmoid(zz)
            sz = zz * sg
            d = dog[:, h * Dh:(h + 1) * Dh]
            ogb_ref[:, h * Dh:(h + 1) * Dh] = (xn * og * sz).astype(BF16)
            dz_ref[:, h * Dh:(h + 1) * Dh] = d * (xn * og) * _dsilu(zz, sg)
            don = d * sz
            dog_ref[...] += _sum0(don * xn)
            dxn = don * og
            do_ref[h] = r * (dxn - xn * jnp.mean(dxn * xn, axis=-1, keepdims=True))

    tok = pl.BlockSpec((None, tm, D), lambda b, t: (b, t, 0))
    tokw = pl.BlockSpec((None, tm, W), lambda b, t: (b, t, 0))
    hm = pl.BlockSpec((None, H, tm, Dh), lambda b, t: (b, 0, t, 0))
    return pl.pallas_call(
        body, name="dn_out_bwd", grid=(B, T // tm),
        in_specs=[tok, tok, hm, tokw, pl.BlockSpec((None, 3, D), lambda b, t: (b, 0, 0)),
                  pl.BlockSpec((1, Dh), lambda b, t: (0, 0)), pl.BlockSpec((W, D), lambda b, t: (0, 0))],
        out_specs=[hm, tokw, tokw, tok, pl.BlockSpec((None, 1, D), lambda b, t: (b, 0, 0)),
                   pl.BlockSpec((None, 1, Dh), lambda b, t: (b, 0, 0))],
        out_shape=[jax.ShapeDtypeStruct((B, H, T, Dh), F32), jax.ShapeDtypeStruct((B, T, W), F32),
                   jax.ShapeDtypeStruct((B, T, W), BF16), jax.ShapeDtypeStruct((B, T, D), BF16),
                   jax.ShapeDtypeStruct((B, 1, D), F32), jax.ShapeDtypeStruct((B, 1, Dh), F32)],
        compiler_params=_cparams(2),
    )(dres, y, o, z, mod3, o_g, w_out)


def dn_conv_bwd(dq, dk, dv, dgb, dbb, pre, ab, w_sconv, alog_row, dt_row):
    B, H, T, Dh = dq.shape
    W = H * Dh
    W3 = 3 * W
    K = w_sconv.shape[0]
    tm = _tile(T, 256)

    def body(dq_ref, dk_ref, dv_ref, dgb_ref, dbb_ref, pre_ref, halo_ref, ab_ref, w_ref, alog_ref, dt_ref,
             dc_ref, dab_ref, small_ref, ext_s):
        t = pl.program_id(1)

        @pl.when(t == 0)
        def _():
            small_ref[...] = jnp.zeros_like(small_ref)

        ext_s[0:SCONV_HALO, :] = jnp.where(t > 0, halo_ref[...], 0.0)
        ext_s[SCONV_HALO:, :] = pre_ref[...]
        cv = _sconv(ext_s, w_ref, tm, K)
        dsl = _dsilu(cv, _sigmoid(cv))
        ab = ab_ref[...]
        lane = lax.broadcasted_iota(jnp.int32, ab.shape, 1)
        dg_all = jnp.zeros_like(ab)
        db_all = jnp.zeros_like(ab)
        for h in range(H):
            dc_ref[:, h * Dh:(h + 1) * Dh] = dq_ref[h] * dsl[:, h * Dh:(h + 1) * Dh]
            dc_ref[:, W + h * Dh:W + (h + 1) * Dh] = dk_ref[h] * dsl[:, W + h * Dh:W + (h + 1) * Dh]
            dc_ref[:, 2 * W + h * Dh:2 * W + (h + 1) * Dh] = dv_ref[h] * dsl[:, 2 * W + h * Dh:2 * W + (h + 1) * Dh]
            dg_all = dg_all + jnp.where(lane == h, jnp.sum(dgb_ref[h], axis=1, keepdims=True), 0.0)
            db_all = db_all + jnp.where(lane == H + h, jnp.sum(dbb_ref[h], axis=1, keepdims=True), 0.0)
        xa = ab + dt_ref[...]
        ea = -jnp.exp(alog_ref[...])
        g_all = ea * _softplus(xa)
        da = dg_all * ea * _sigmoid(xa)
        beta = _sigmoid(ab)
        dab_ref[...] = da + db_all * beta * (1.0 - beta)
        small_ref[0:1, :] += _sum0(dg_all * g_all)
        small_ref[1:2, :] += _sum0(da)

    hm = pl.BlockSpec((None, H, tm, Dh), lambda b, t: (b, 0, t, 0))
    row = pl.BlockSpec((1, LANES), lambda b, t: (0, 0))
    return pl.pallas_call(
        body, name="dn_conv_bwd", grid=(B, T // tm),
        in_specs=[hm] * 5 + [pl.BlockSpec((None, tm, W3), lambda b, t: (b, t, 0)), _past_halo_spec(tm, SCONV_HALO, W3),
                             pl.BlockSpec((None, tm, LANES), lambda b, t: (b, t, 0)),
                             pl.BlockSpec((K, W3), lambda b, t: (0, 0)), row, row],
        out_specs=[pl.BlockSpec((None, tm, W3), lambda b, t: (b, t, 0)), pl.BlockSpec((None, tm, LANES), lambda b, t: (b, t, 0)),
                   pl.BlockSpec((None, 2, LANES), lambda b, t: (b, 0, 0))],
        out_shape=[jax.ShapeDtypeStruct((B, T, W3), F32), jax.ShapeDtypeStruct((B, T, LANES), F32),
                   jax.ShapeDtypeStruct((B, 2, LANES), F32)],
        scratch_shapes=[pltpu.VMEM((tm + SCONV_HALO, W3), F32)],
        compiler_params=_cparams(2),
    )(dq, dk, dv, dgb, dbb, pre, pre, ab, w_sconv, alog_row, dt_row)


def dn_proj_bwd(x, dres, dc, pre, dz, dab, mod3, g, w_main, w_ab, w_sconv):
    B, T, D = x.shape
    W3 = dc.shape[2]
    W = W3 // 3
    K = w_sconv.shape[0]
    tm = _tile(T, 256)
    nt = T // tm

    def body(x_ref, dres_ref, dc_ref, dch_ref, pre_ref, preh_ref, dz_ref, dab_ref, mod_ref, g_ref, wm_ref, wab_ref, ws_ref,
             dx_ref, h_ref, dproj_ref, dws_ref, dmod_ref, dg_ref, extp_s, extd_s):
        t = pl.program_id(1)

        @pl.when(t == 0)
        def _():
            dws_ref[...] = jnp.zeros_like(dws_ref)
            dmod_ref[...] = jnp.zeros_like(dmod_ref)
            dg_ref[...] = jnp.zeros_like(dg_ref)

        dc = dc_ref[...]
        extp_s[0:SCONV_HALO, :] = jnp.where(t > 0, preh_ref[...], 0.0)
        extp_s[SCONV_HALO:, :] = pre_ref[...]
        extd_s[0:tm, :] = dc
        extd_s[tm:, :] = jnp.where(t < nt - 1, dch_ref[...], 0.0)
        dpre = jnp.zeros((tm, W3), F32)
        for k in range(K):
            dpre = dpre + ws_ref[k:k + 1, :] * extd_s[pl.ds(K - 1 - k, tm), :]
            dws_ref[k:k + 1, :] += _sum0(dc * extp_s[pl.ds(SCONV_HALO - (K - 1) + k, tm), :])
        dpre = dpre.astype(BF16)
        dzb = dz_ref[...].astype(BF16)
        dproj_ref[:, 0:W3] = dpre
        dproj_ref[:, W3:] = dzb
        dh = _mm_nt(dab_ref[...], wab_ref[...]) + _mm_nt(dzb, wm_ref[:, W3:])
        for p in range(3):
            dh = dh + _mm_nt(dpre[:, p * W:(p + 1) * W], wm_ref[:, p * W:(p + 1) * W])
        xv = x_ref[...]
        h_ref[...] = _modnorm(xv, g_ref[...], mod_ref[1:2, :], mod_ref[0:1, :]).astype(BF16)
        dxn, dg, dscale, dshift = _modnorm_bwd(xv, g_ref[...], mod_ref[1:2, :], dh)
        dx_ref[...] = dres_ref[...] + dxn
        dmod_ref[0:1, :] += dshift
        dmod_ref[1:2, :] += dscale
        dg_ref[...] += dg

    tok = pl.BlockSpec((None, tm, D), lambda b, t: (b, t, 0))
    tok3 = pl.BlockSpec((None, tm, W3), lambda b, t: (b, t, 0))
    return pl.pallas_call(
        body, name="dn_proj_bwd", grid=(B, nt),
        in_specs=[tok, tok, tok3, _future_halo_spec(tm, SCONV_HALO, W3, T), tok3, _past_halo_spec(tm, SCONV_HALO, W3),
                  pl.BlockSpec((None, tm, W), lambda b, t: (b, t, 0)), pl.BlockSpec((None, tm, LANES), lambda b, t: (b, t, 0)),
                  pl.BlockSpec((None, 3, D), lambda b, t: (b, 0, 0)), pl.BlockSpec((1, D), lambda b, t: (0, 0)),
                  pl.BlockSpec((D, 4 * W), lambda b, t: (0, 0)), pl.BlockSpec((D, LANES), lambda b, t: (0, 0)),
                  pl.BlockSpec((K, W3), lambda b, t: (0, 0))],
        out_specs=[tok, tok, pl.BlockSpec((None, tm, 4 * W), lambda b, t: (b, t, 0)),
                   pl.BlockSpec((None, K, W3), lambda b, t: (b, 0, 0)), pl.BlockSpec((None, 3, D), lambda b, t: (b, 0, 0)),
                   pl.BlockSpec((None, 1, D), lambda b, t: (b, 0, 0))],
        out_shape=[jax.ShapeDtypeStruct((B, T, D), F32), jax.ShapeDtypeStruct((B, T, D), BF16),
                   jax.ShapeDtypeStruct((B, T, 4 * W), BF16), jax.ShapeDtypeStruct((B, K, W3), F32),
                   jax.ShapeDtypeStruct((B, 3, D), F32), jax.ShapeDtypeStruct((B, 1, D), F32)],
        scratch_shapes=[pltpu.VMEM((tm + SCONV_HALO, W3), F32), pltpu.VMEM((tm + SCONV_HALO, W3), F32)],
        compiler_params=_cparams(2),
    )(x, dres, dc, dc, pre, pre, dz, dab, mod3, g, w_main, w_ab, w_sconv)


def ada_fwd(c_all, w_ada, b_cols):
    L, D, Ca = w_ada.shape
    NB = c_all.shape[0]

    def body(c_ref, w_ref, b_ref, o_ref):
        cv = c_ref[...]
        o_ref[...] = _mm(cv * _sigmoid(cv), w_ref[...]) + b_ref[...]

    return pl.pallas_call(
        body, name="ada_fwd", grid=(L,),
        in_specs=[pl.BlockSpec((NB, D), lambda i: (0, 0)), pl.BlockSpec((None, D, Ca), lambda i: (i, 0, 0)),
                  pl.BlockSpec((None, 1, Ca), lambda i: (i, 0, 0))],
        out_specs=pl.BlockSpec((None, NB, Ca), lambda i: (i, 0, 0)),
        out_shape=jax.ShapeDtypeStruct((L, NB, Ca), F32),
        compiler_params=_cparams(1),
    )(c_all, w_ada, b_cols)


def ada_bwd(c_all, dmod_cols, dmod_all):
    L, NB, Ca = dmod_cols.shape
    D = c_all.shape[1]
    C9 = dmod_all.shape[2]

    def body(c_ref, dc_ref, da_ref, gw_ref, gb_ref):
        cv = c_ref[...]
        gw_ref[...] = _mm_tn(cv * _sigmoid(cv), dc_ref[...])
        gb_ref[...] = _sum0(da_ref[...])

    return pl.pallas_call(
        body, name="ada_bwd", grid=(L,),
        in_specs=[pl.BlockSpec((NB, D), lambda i: (0, 0)), pl.BlockSpec((None, NB, Ca), lambda i: (i, 0, 0)),
                  pl.BlockSpec((None, NB, C9), lambda i: (i, 0, 0))],
        out_specs=[pl.BlockSpec((None, D, Ca), lambda i: (i, 0, 0)), pl.BlockSpec((None, 1, C9), lambda i: (i, 0, 0))],
        out_shape=[jax.ShapeDtypeStruct((L, D, Ca), F32), jax.ShapeDtypeStruct((L, 1, C9), F32)],
        compiler_params=_cparams(1),
    )(c_all, dmod_cols, dmod_all)


def adamw(w, g, m, v, name, token=None):
    R, C = w.shape
    tr = _tile(R, max(8, (1 << 18) // C))
    if token is None:
        token = jnp.zeros((8, LANES), F32)

    def body(w_ref, g_ref, m_ref, v_ref, t_ref, d_ref, mo_ref, vo_ref):
        gv = g_ref[...] + t_ref[0:1, 0:1]
        mn = ADAM_B1 * m_ref[...] + (1.0 - ADAM_B1) * gv
        vn = ADAM_B2 * v_ref[...] + (1.0 - ADAM_B2) * (gv * gv)
        m_hat = mn / (1.0 - ADAM_B1 ** ADAM_STEP)
        v_hat = vn / (1.0 - ADAM_B2 ** ADAM_STEP)
        d_ref[...] = -ADAM_LR * (m_hat / (jnp.sqrt(v_hat) + ADAM_EPS) + ADAM_WD * w_ref[...])
        mo_ref[...] = mn
        vo_ref[...] = vn

    blk = pl.BlockSpec((tr, C), lambda i: (i, 0))
    return pl.pallas_call(
        body, name=name, grid=(R // tr,), in_specs=[blk] * 4 + [pl.BlockSpec((8, LANES), lambda i: (0, 0))],
        out_specs=[blk] * 3, out_shape=[jax.ShapeDtypeStruct((R, C), F32)] * 3, compiler_params=_cparams(1),
    )(w, g, m, v, token)


def sum_devices(a):
    n, R, C = a.shape

    def body(a_ref, o_ref):
        s = a_ref[0]
        for d in range(1, n):
            s = s + a_ref[d]
        o_ref[...] = s

    return pl.pallas_call(
        body, name="sum_devices", out_shape=jax.ShapeDtypeStruct((R, C), F32),
        compiler_params=pltpu.CompilerParams(vmem_limit_bytes=VMEM_LIMIT_V7X),
    )(a)


def _place():
    x, y, c = lax.axis_index("x"), lax.axis_index("y"), lax.axis_index("c")
    return x, y, c


def _other_chips(x, y):
    return [(2 * (1 - x) + y, 1 - x, y), (2 * x + (1 - y), x, 1 - y), (2 * (1 - x) + (1 - y), 1 - x, 1 - y)]


def allgather8(block):
    m_per, n = block.shape

    def body(x_ref, out_ref, send_sems, recv_sems, local_sem):
        x, y, c = _place()
        me, sibling = (x, y, c), (x, y, 1 - c)
        chips = [(1 - x, y), (x, 1 - y), (1 - x, 1 - y)]

        def rows(px, py, pc):
            return out_ref.at[pl.ds((4 * px + 2 * py + pc) * m_per, m_per), :]

        def copy(k, blk, to, src=None):
            return pltpu.make_async_remote_copy(
                src_ref=rows(*blk) if src is None else src, dst_ref=rows(*blk),
                send_sem=send_sems.at[k], recv_sem=recv_sems.at[k], device_id=to, device_id_type=MESH)

        mine = pltpu.make_async_copy(x_ref, rows(*me), local_sem)
        mine.start()
        first = [copy(0, me, sibling, src=x_ref)]
        first += [copy(1 + j, me, (*chip, c), src=x_ref) for j, chip in enumerate(chips)]
        for cp in first:
            cp.start()
        passed = [copy(4 + j, (*chip, c), sibling) for j, chip in enumerate(chips)]
        for j, chip in enumerate(chips):
            copy(1 + j, (*chip, c), me).wait_recv()
            passed[j].start()
        copy(0, sibling, me).wait_recv()
        for j, chip in enumerate(chips):
            copy(4 + j, (*chip, 1 - c), me).wait_recv()
        for cp in first + passed:
            cp.wait_send()
        mine.wait()

    return pl.pallas_call(
        body, name="allgather8", out_shape=jax.ShapeDtypeStruct((N_DEV * m_per, n), block.dtype),
        in_specs=[pl.BlockSpec(memory_space=pltpu.VMEM)], out_specs=pl.BlockSpec(memory_space=pltpu.VMEM),
        scratch_shapes=[pltpu.SemaphoreType.DMA((7,)), pltpu.SemaphoreType.DMA((7,)), pltpu.SemaphoreType.DMA],
        compiler_params=pltpu.CompilerParams(vmem_limit_bytes=VMEM_LIMIT_V7X),
    )(block)


def _half(ref, c, rh):
    return ref.at[pl.ds(pl.multiple_of(c * rh, 16), rh), :]


def gather_weights(lands):
    K = len(lands)

    def body(*refs):
        ins, outs = refs[:K], refs[K:2 * K]
        ici_send, ici_recv, d2d_send, d2d_recv = refs[2 * K:]
        x, y, c = _place()
        me = 2 * x + y
        sibling = (x, y, 1 - c)
        others = _other_chips(x, y)
        sent = []
        for k in range(K):
            rh = ins[k].shape[1] // 2
            for r, (_, px, py) in enumerate(others):
                cp = pltpu.make_async_remote_copy(
                    src_ref=_half(ins[k].at[me], c, rh), dst_ref=_half(outs[k].at[me], c, rh),
                    send_sem=ici_send.at[k, r], recv_sem=ici_recv.at[k, r], device_id=(px, py, c), device_id_type=MESH)
                cp.start()
                sent.append(cp)
        forwards = []
        for k in range(K):
            rh = ins[k].shape[1] // 2
            for r, (pchip, px, py) in enumerate(others):
                landed = _half(outs[k].at[pchip], c, rh)
                pltpu.make_async_remote_copy(
                    src_ref=landed, dst_ref=landed, send_sem=ici_send.at[k, r], recv_sem=ici_recv.at[k, r],
                    device_id=(px, py, c), device_id_type=MESH).wait_recv()
                fw = pltpu.make_async_remote_copy(
                    src_ref=landed, dst_ref=landed, send_sem=d2d_send.at[k, r], recv_sem=d2d_recv.at[k, r],
                    device_id=sibling, device_id_type=MESH)
                fw.start()
                forwards.append(fw)
        for k in range(K):
            rh = ins[k].shape[1] // 2
            for r, (pchip, _, _) in enumerate(others):
                theirs = _half(outs[k].at[pchip], 1 - c, rh)
                pltpu.make_async_remote_copy(
                    src_ref=theirs, dst_ref=theirs, send_sem=d2d_send.at[k, r], recv_sem=d2d_recv.at[k, r],
                    device_id=sibling, device_id_type=MESH).wait_recv()
        for cp in sent + forwards:
            cp.wait_send()

    return pl.pallas_call(
        body, name="gather_weights",
        out_shape=[jax.ShapeDtypeStruct(s.shape, s.dtype) for s in lands],
        in_specs=[HBM_SPEC] * K, out_specs=[HBM_SPEC] * K, input_output_aliases={k: k for k in range(K)},
        scratch_shapes=[pltpu.SemaphoreType.DMA((K, 3))] * 4,
    )(*lands)


def pair_exchange(grads):
    K = len(grads)

    def body(*refs):
        ins, outs = refs[:K], refs[K:2 * K]
        send_sems, recv_sems = refs[2 * K:]
        x, y, c = _place()
        sibling = (x, y, 1 - c)
        copies = []
        for k in range(K):
            n, r, _ = ins[k].shape
            rh = r // 2
            cp = pltpu.make_async_remote_copy(
                src_ref=ins[k].at[:, pl.ds(pl.multiple_of((1 - c) * rh, 16), rh), :], dst_ref=outs[k],
                send_sem=send_sems.at[k], recv_sem=recv_sems.at[k], device_id=sibling, device_id_type=MESH)
            cp.start()
            copies.append(cp)
        for cp in copies:
            cp.wait_recv()
        for cp in copies:
            cp.wait_send()

    return pl.pallas_call(
        body, name="pair_exchange",
        out_shape=[jax.ShapeDtypeStruct((g.shape[0], g.shape[1] // 2, g.shape[2]), g.dtype) for g in grads],
        in_specs=[HBM_SPEC] * K, out_specs=[HBM_SPEC] * K,
        scratch_shapes=[pltpu.SemaphoreType.DMA((K,))] * 2,
    )(*grads)


def pair_add(grad, recv, c_idx):
    n, r, C = grad.shape
    rh = r // 2
    tr = _tile(rh, max(16, (1 << 19) // C), 16)
    grad = grad.reshape(n, 2, rh, C)

    def body(c_ref, g_ref, r_ref, o_ref):
        o_ref[...] = (g_ref[...].astype(F32) + r_ref[...].astype(F32)).astype(BF16)

    return pl.pallas_call(
        body, name="pair_add",
        grid_spec=pltpu.PrefetchScalarGridSpec(
            num_scalar_prefetch=1, grid=(n, rh // tr),
            in_specs=[pl.BlockSpec((None, None, tr, C), lambda d, i, c_ref: (d, c_ref[0], i, 0)),
                      pl.BlockSpec((None, tr, C), lambda d, i, c_ref: (d, i, 0))],
            out_specs=pl.BlockSpec((None, tr, C), lambda d, i, c_ref: (d, i, 0))),
        out_shape=jax.ShapeDtypeStruct((n, rh, C), BF16), compiler_params=_cparams(2),
    )(c_idx, grad, recv)


def chip_exchange(parts):
    K = len(parts)

    def body(*refs):
        ins, outs = refs[:K], refs[K:2 * K]
        send_sems, recv_sems = refs[2 * K:]
        x, y, c = _place()
        others = _other_chips(x, y)
        started = []
        for k in range(K):
            for r, (pchip, px, py) in enumerate(others):
                cp = pltpu.make_async_remote_copy(
                    src_ref=ins[k].at[pchip], dst_ref=outs[k].at[r], send_sem=send_sems.at[k, r],
                    recv_sem=recv_sems.at[k, r], device_id=(px, py, c), device_id_type=MESH)
                cp.start()
                started.append(cp)
        for cp in started:
            cp.wait_recv()
        for cp in started:
            cp.wait_send()

    return pl.pallas_call(
        body, name="chip_exchange",
        out_shape=[jax.ShapeDtypeStruct((3,) + p.shape[1:], p.dtype) for p in parts],
        in_specs=[HBM_SPEC] * K, out_specs=[HBM_SPEC] * K,
        scratch_shapes=[pltpu.SemaphoreType.DMA((K, 3))] * 2,
    )(*parts)


def chip_sum(parts, got, where):
    _, rh, C = parts.shape
    tr = _tile(rh, max(16, (1 << 19) // C), 16)
    nt = rh // tr

    def body(w_ref, p_ref, g_ref, o_ref):
        s = p_ref[...].astype(F32)
        for r in range(3):
            s = s + g_ref[r].astype(F32)
        o_ref[...] = s

    return pl.pallas_call(
        body, name="chip_sum",
        grid_spec=pltpu.PrefetchScalarGridSpec(
            num_scalar_prefetch=1, grid=(nt,),
            in_specs=[pl.BlockSpec((None, tr, C), lambda i, w_ref: (w_ref[0], i, 0)),
                      pl.BlockSpec((3, tr, C), lambda i, w_ref: (0, i, 0))],
            out_specs=pl.BlockSpec((tr, C), lambda i, w_ref: (w_ref[1] * nt + i, 0))),
        out_shape=jax.ShapeDtypeStruct((2 * rh, C), F32), compiler_params=_cparams(1),
    )(where, parts, got)


def pair_share(sums):
    K = len(sums)

    def body(*refs):
        ins, outs = refs[:K], refs[K:2 * K]
        send_sems, recv_sems = refs[2 * K:]
        x, y, c = _place()
        sibling = (x, y, 1 - c)
        started = []
        for k in range(K):
            rh = ins[k].shape[0] // 2
            cp = pltpu.make_async_remote_copy(
                src_ref=_half(ins[k], c, rh), dst_ref=_half(outs[k], c, rh), send_sem=send_sems.at[k],
                recv_sem=recv_sems.at[k], device_id=sibling, device_id_type=MESH)
            cp.start()
            started.append(cp)
        for k in range(K):
            rh = ins[k].shape[0] // 2
            theirs = _half(outs[k], 1 - c, rh)
            pltpu.make_async_remote_copy(
                src_ref=theirs, dst_ref=theirs, send_sem=send_sems.at[k], recv_sem=recv_sems.at[k],
                device_id=sibling, device_id_type=MESH).wait_recv()
        for cp in started:
            cp.wait_send()

    return pl.pallas_call(
        body, name="pair_share",
        out_shape=[jax.ShapeDtypeStruct(s.shape, s.dtype) for s in sums],
        in_specs=[HBM_SPEC] * K, out_specs=[HBM_SPEC] * K, input_output_aliases={k: k for k in range(K)},
        scratch_shapes=[pltpu.SemaphoreType.DMA((K,))] * 2,
    )(*sums)


SEM_SPEC = pl.BlockSpec(memory_space=pltpu.SEMAPHORE)
ANY_SPEC = pl.BlockSpec(memory_space=pl.ANY)
DATAFLOW = pltpu.SideEffectType.DATAFLOW_SIDE_EFFECTING


def _in_hbm(a):
    return pltpu.with_memory_space_constraint(a, pltpu.HBM)


def _ici_copies(srcs, dsts, send_sems, recv_sems, src_slice, dst_slice):
    x, y, c = _place()
    out = []
    for k in range(len(srcs)):
        for r, (pchip, px, py) in enumerate(_other_chips(x, y)):
            out.append(pltpu.make_async_remote_copy(
                src_ref=src_slice(srcs[k], r, pchip), dst_ref=dst_slice(dsts[k], r, pchip),
                send_sem=send_sems.at[3 * k + r], recv_sem=recv_sems.at[3 * k + r], device_id=(px, py, c),
                device_id_type=MESH))
    return out


def _exchange_start(bufs, lands, src_slice, dst_slice, name, after=None):
    K = len(bufs)
    same = lands is None
    n_thru = K if same else 2 * K
    n_in = n_thru + (after is not None)

    def body(*refs):
        ins = refs[:n_thru]
        send_sems, recv_sems = refs[n_in], refs[n_in + 1]
        token = refs[-1]
        srcs = ins[:K]
        dsts = srcs if same else ins[K:]
        for cp in _ici_copies(srcs, dsts, send_sems, recv_sems, src_slice, dst_slice):
            cp.start()
        token[...] = jnp.zeros_like(token)

    thru = list(bufs) + ([] if same else list(lands))
    res = pl.pallas_call(
        body, name=name,
        out_shape=[pltpu.SemaphoreType.DMA((3 * K,)), pltpu.SemaphoreType.DMA((3 * K,))]
        + [pltpu.HBM(a.shape, a.dtype) for a in thru] + [jax.ShapeDtypeStruct((8, LANES), F32)],
        in_specs=[HBM_SPEC] * n_thru + [ANY_SPEC] * (after is not None),
        out_specs=[SEM_SPEC, SEM_SPEC] + [HBM_SPEC] * n_thru + [pl.BlockSpec(memory_space=pltpu.VMEM)],
        input_output_aliases={i: 2 + i for i in range(n_thru)},
        compiler_params=pltpu.CompilerParams(has_side_effects=DATAFLOW),
    )(*[_in_hbm(a) for a in thru], *([] if after is None else [after]))
    return res[0], res[1], res[2:2 + K], (res[2:2 + K] if same else res[2 + K:2 + 2 * K]), res[-1]


def _exchange_wait(send_sems, recv_sems, bufs, lands, after, src_slice, dst_slice, name):
    K = len(bufs)
    same = lands is None
    n_thru = K if same else 2 * K

    def body(*refs):
        ins = refs[:n_thru]
        ssem, rsem = refs[n_thru], refs[n_thru + 1]
        srcs = ins[:K]
        dsts = srcs if same else ins[K:]
        copies = _ici_copies(srcs, dsts, ssem, rsem, src_slice, dst_slice)
        for cp in copies:
            cp.wait_send()
        for cp in copies:
            cp.wait_recv()

    thru = list(bufs) + ([] if same else list(lands))
    res = pl.pallas_call(
        body, name=name,
        out_shape=[pltpu.HBM(a.shape, a.dtype) for a in thru],
        in_specs=[HBM_SPEC] * n_thru + [SEM_SPEC, SEM_SPEC, ANY_SPEC],
        out_specs=[HBM_SPEC] * n_thru,
        input_output_aliases={i: i for i in range(n_thru)},
        compiler_params=pltpu.CompilerParams(has_side_effects=DATAFLOW),
    )(*thru, send_sems, recv_sems, after)
    return res[:K], (res[:K] if same else res[K:])


def _own_half(ref, r, pchip):
    x, y, c = _place()
    return _half(ref.at[2 * x + y], c, ref.shape[1] // 2)


def _their_half(ref, r, pchip):
    _, _, c = _place()
    return _half(ref.at[pchip], c, ref.shape[1] // 2)


def gather_start(lands, name, after=None):
    return _exchange_start(lands, None, _own_half, _own_half, name, after)


def gather_wait(handle, after, name):
    ssem, rsem, lands, _, _ = handle
    return _exchange_wait(ssem, rsem, lands, None, after, _own_half, _their_half, name)[1]


def pair_forward(lands):
    K = len(lands)

    def body(*refs):
        ins, outs = refs[:K], refs[K:2 * K]
        send_sems, recv_sems = refs[2 * K:]
        x, y, c = _place()
        sibling = (x, y, 1 - c)
        started = []
        for k in range(K):
            rh = ins[k].shape[1] // 2
            for r, (pchip, _, _) in enumerate(_other_chips(x, y)):
                cp = pltpu.make_async_remote_copy(
                    src_ref=_half(ins[k].at[pchip], c, rh), dst_ref=_half(outs[k].at[pchip], c, rh),
                    send_sem=send_sems.at[k, r], recv_sem=recv_sems.at[k, r], device_id=sibling, device_id_type=MESH)
                cp.start()
                started.append(cp)
        for k in range(K):
            rh = ins[k].shape[1] // 2
            for r, (pchip, _, _) in enumerate(_other_chips(x, y)):
                theirs = _half(outs[k].at[pchip], 1 - c, rh)
                pltpu.make_async_remote_copy(
                    src_ref=theirs, dst_ref=theirs, send_sem=send_sems.at[k, r], recv_sem=recv_sems.at[k, r],
                    device_id=sibling, device_id_type=MESH).wait_recv()
        for cp in started:
            cp.wait_send()

    return pl.pallas_call(
        body, name="pair_forward",
        out_shape=[jax.ShapeDtypeStruct(s.shape, s.dtype) for s in lands],
        in_specs=[HBM_SPEC] * K, out_specs=[HBM_SPEC] * K, input_output_aliases={k: k for k in range(K)},
        scratch_shapes=[pltpu.SemaphoreType.DMA((K, 3))] * 2,
    )(*lands)


def _to_chip(ref, r, pchip):
    return ref.at[pchip]


def _from_relation(ref, r, pchip):
    return ref.at[r]


def reduce_start(grads, c_idx, name, after=None):
    recv = pair_exchange(grads)
    parts = [pair_add(g, r, c_idx) for g, r in zip(grads, recv)]
    lands = [lax.empty((3,) + p.shape[1:], p.dtype) for p in parts]
    return _exchange_start(parts, lands, _to_chip, _from_relation, name, after)


def reduce_finish(handle, after, where, name):
    ssem, rsem, parts, lands, _ = handle
    parts, got = _exchange_wait(ssem, rsem, parts, lands, after, _to_chip, _from_relation, name)
    return pair_share([chip_sum(p, g, where) for p, g in zip(parts, got)])


def _pack(arrs):
    flat = jnp.concatenate([a.reshape(-1).astype(F32) for a in arrs])
    pad = (-flat.shape[0]) % (8 * LANES)
    return jnp.pad(flat, (0, pad)).reshape(-1, LANES)


def _unpack(flat, shapes):
    out, off = [], 0
    for s in shapes:
        n = 1
        for d in s:
            n *= d
        out.append(flat[off:off + n].reshape(s))
        off += n
    return out


def _adamw_any(w, g, m, v, name, token=None):
    shp = w.shape
    C = shp[-1]
    d, nm, nv = adamw(w.reshape(-1, C), g.reshape(-1, C), m.reshape(-1, C), v.reshape(-1, C), name, token)
    return d.reshape(shp), nm.reshape(shp), nv.reshape(shp)


def kernel(x, c, norm_g, w_ada, b_ada, w_ffn_in, w_ffn_out, cm_w_glu, cm_b_glu, cm_w_dw, cm_b_dw, cm_ln_g, cm_ln_b, cm_w_pw, cm_b_pw, dn_w_in, dn_w_sconv, dn_a_log, dn_dt_bias, dn_o_g, dn_w_out, final_g, loss_target, m_norm_g, m_w_ada, m_b_ada, m_w_ffn_in, m_w_ffn_out, m_cm_w_glu, m_cm_b_glu, m_cm_w_dw, m_cm_b_dw, m_cm_ln_g, m_cm_ln_b, m_cm_w_pw, m_cm_b_pw, m_dn_w_in, m_dn_w_sconv, m_dn_a_log, m_dn_dt_bias, m_dn_o_g, m_dn_w_out, m_final_g, v_norm_g, v_w_ada, v_b_ada, v_w_ffn_in, v_w_ffn_out, v_cm_w_glu, v_cm_b_glu, v_cm_w_dw, v_cm_b_dw, v_cm_ln_g, v_cm_ln_b, v_cm_w_pw, v_cm_b_pw, v_dn_w_in, v_dn_w_sconv, v_dn_a_log, v_dn_dt_bias, v_dn_o_g, v_dn_w_out, v_final_g):
    weights = dict(norm_g=norm_g, w_ada=w_ada, b_ada=b_ada, w_ffn_in=w_ffn_in, w_ffn_out=w_ffn_out, cm_w_glu=cm_w_glu,
                   cm_b_glu=cm_b_glu, cm_w_dw=cm_w_dw, cm_b_dw=cm_b_dw, cm_ln_g=cm_ln_g, cm_ln_b=cm_ln_b, cm_w_pw=cm_w_pw,
                   cm_b_pw=cm_b_pw, dn_w_in=dn_w_in, dn_w_sconv=dn_w_sconv, dn_a_log=dn_a_log, dn_dt_bias=dn_dt_bias,
                   dn_o_g=dn_o_g, dn_w_out=dn_w_out, final_g=final_g)
    mom_m = dict(norm_g=m_norm_g, w_ada=m_w_ada, b_ada=m_b_ada, w_ffn_in=m_w_ffn_in, w_ffn_out=m_w_ffn_out,
                 cm_w_glu=m_cm_w_glu, cm_b_glu=m_cm_b_glu, cm_w_dw=m_cm_w_dw, cm_b_dw=m_cm_b_dw, cm_ln_g=m_cm_ln_g,
                 cm_ln_b=m_cm_ln_b, cm_w_pw=m_cm_w_pw, cm_b_pw=m_cm_b_pw, dn_w_in=m_dn_w_in, dn_w_sconv=m_dn_w_sconv,
                 dn_a_log=m_dn_a_log, dn_dt_bias=m_dn_dt_bias, dn_o_g=m_dn_o_g, dn_w_out=m_dn_w_out, final_g=m_final_g)
    mom_v = dict(norm_g=v_norm_g, w_ada=v_w_ada, b_ada=v_b_ada, w_ffn_in=v_w_ffn_in, w_ffn_out=v_w_ffn_out,
                 cm_w_glu=v_cm_w_glu, cm_b_glu=v_cm_b_glu, cm_w_dw=v_cm_w_dw, cm_b_dw=v_cm_b_dw, cm_ln_g=v_cm_ln_g,
                 cm_ln_b=v_cm_ln_b, cm_w_pw=v_cm_w_pw, cm_b_pw=v_cm_b_pw, dn_w_in=v_dn_w_in, dn_w_sconv=v_dn_w_sconv,
                 dn_a_log=v_dn_a_log, dn_dt_bias=v_dn_dt_bias, dn_o_g=v_dn_o_g, dn_w_out=v_dn_w_out, final_g=v_final_g)
    names = list(weights)

    BL, T, D = x.shape
    L = norm_g.shape[0]
    NB = BL * N_DEV
    Ca = w_ada.shape[2]
    C9 = b_ada.shape[1]
    H = dn_a_log.shape[1]
    Dh = dn_o_g.shape[1]
    W = H * Dh
    KC = cm_w_dw.shape[1]
    KS = dn_w_sconv.shape[1]
    n_cm, n_dn = cm_w_glu.shape[0], dn_w_in.shape[0]
    ax, ay, ac = lax.axis_index("x"), lax.axis_index("y"), lax.axis_index("c")
    chip = 2 * ax + ay
    dev = 2 * chip + ac
    c_idx = ac.astype(jnp.int32).reshape(1)
    where = jnp.stack([chip, ac]).astype(jnp.int32)

    def layer_shards(i):
        sh = [w_ffn_in[i, 0], w_ffn_in[i, 1], w_ffn_out[i, 0], w_ffn_out[i, 1]]
        if i % 2 == 0:
            sh += [cm_w_glu[i // 2], cm_w_pw[i // 2]]
        else:
            sh += [dn_w_in[i // 2], dn_w_out[i // 2]]
        return [lax.dynamic_update_slice(lax.empty((N_CHIPS,) + s.shape, BF16), s.astype(BF16)[None], (chip, 0, 0))
                for s in sh]

    lands = [layer_shards(i) for i in range(L)]
    wts = [None] * L
    first = gather_start([lands[0][0], lands[0][2]], "gather_start_0a")
    rest = gather_start([lands[0][k] for k in (1, 3, 4, 5)], "gather_start_0b", first[4])

    small_in = [c, norm_g, cm_w_dw, dn_w_sconv]
    packed = _pack(small_in) + rest[4][0, 0]
    gathered = allgather8(packed).reshape(N_DEV, -1)
    per_dev = [_unpack(gathered[d], [a.shape for a in small_in]) for d in range(N_DEV)]
    c_all = jnp.concatenate([p[0] for p in per_dev], axis=0)
    norm_g_full = jnp.concatenate([per_dev[2 * s][1] for s in range(N_CHIPS)], axis=-1)
    w_dw_full = jnp.concatenate([per_dev[2 * s][2] for s in range(N_CHIPS)], axis=-1)
    w_sconv_full = jnp.concatenate([per_dev[2 * s][3] for s in range(N_CHIPS)], axis=-1)

    b_cols = lax.dynamic_slice_in_dim(b_ada, chip * Ca, Ca, axis=1).reshape(L, 1, Ca)
    mod_part = ada_fwd(c_all, w_ada, b_cols)
    mod_g = allgather8(mod_part.reshape(-1, LANES)).reshape(N_DEV, L, NB, Ca)
    mod_all = jnp.concatenate([mod_g[2 * s] for s in range(N_CHIPS)], axis=-1)
    mod = lax.dynamic_slice_in_dim(mod_all, dev * BL, BL, axis=1).reshape(L, BL, 9, D)

    def dn_weights(i):
        full = jnp.transpose(wts[i][4], (1, 0, 2)).reshape(D, -1)
        return full[:, :4 * W], jnp.pad(full[:, 4 * W:], ((0, 0), (0, LANES - 2 * H)))

    def row128(v):
        return jnp.pad(v.reshape(1, -1), ((0, 0), (0, LANES - v.shape[-1])))

    def pad_taps(w):
        return jnp.pad(w, ((0, 1), (0, 0)))

    saved = []
    xs = x
    after = mod
    for i in range(L):
        tok = 0.0
        if i == 0:
            wl = wts[0] = [None] * 6
            wl[0], wl[2] = pair_forward(gather_wait(first, after, "gather_wait_0a"))
        else:
            wl = wts[i] = pair_forward(gather_wait(handle, after, "gather_wait_%d" % i))
            if i + 1 < L:
                handle = gather_start(lands[i + 1], "gather_start_%d" % (i + 1), wl[0])
                tok = handle[4][0, 0]
        sv = {}
        m3 = [mod[i, :, 3 * j:3 * j + 3] + tok for j in range(3)]
        gs = [norm_g_full[i, j].reshape(1, D) for j in range(3)]
        sv["x0"] = xs
        xs, sv["y0"], sv["h0"], sv["gu0"] = ffn_fwd(xs, m3[0], gs[0], wl[0], wl[2])
        sv["x1"] = xs
        if i == 0:
            wl[1], wl[3], wl[4], wl[5] = pair_forward(gather_wait(rest, xs, "gather_wait_0b"))
            handle = gather_start(lands[1], "gather_start_1", wl[1])
            m3 = [m + handle[4][0, 0] for m in m3]
        if i % 2 == 0:
            a = i // 2
            sv["u"] = conv_glu_fwd(xs, m3[1], gs[1], wl[4], cm_b_glu[a].reshape(1, -1))
            xs, sv["y1"], sv["u2"] = conv_out_fwd(
                xs, sv["u"], m3[1], pad_taps(w_dw_full[a]), cm_b_dw[a].reshape(1, D), cm_ln_g[a].reshape(1, D),
                cm_ln_b[a].reshape(1, D), wl[5].reshape(D, D), cm_b_pw[a].reshape(1, D))
        else:
            a = i // 2
            w_main, w_ab = dn_weights(i)
            sv["pre"], sv["z"], sv["ab"] = dn_proj_fwd(xs, m3[1], gs[1], w_main, w_ab)
            qkvgb = dn_conv_fwd(sv["pre"], sv["ab"], w_sconv_full[a], row128(dn_a_log[a]), row128(dn_dt_bias[a]), H)
            sv["qkvgb"] = qkvgb
            sv["o"], sv["sp"] = dn_chunk_fwd(*qkvgb)
            xs, sv["y1"] = dn_out_fwd(xs, sv["o"], sv["z"], m3[1], dn_o_g[a].reshape(1, Dh), wl[5].reshape(W, D))
        sv["x2"] = xs
        xs, sv["y2"], sv["h2"], sv["gu2"] = ffn_fwd(xs, m3[2], gs[2], wl[1], wl[3])
        saved.append(sv)
        after = xs

    dx, d_final_g, loss_part = final_loss(xs, final_g.reshape(1, D), loss_target)

    g_small = {n: None for n in names}
    d_norm_g = [[None] * 3 for _ in range(L)]
    dmod = [[None] * 3 for _ in range(L)]
    g_cm = {k: [None] * n_cm for k in ("b_glu", "w_dw", "b_dw", "ln_g", "ln_b", "b_pw")}
    g_dn = {k: [None] * n_dn for k in ("w_sconv", "a_log", "dt_bias", "o_g")}
    big = [None] * L

    def ffn_back(i, j, slot, dx, tok=0.0):
        wl, sv = wts[i], saved[i]
        m3 = mod[i, :, 3 * j:3 * j + 3] + tok
        g = norm_g_full[i, j].reshape(1, D)
        dx, ab_, dgu, dyb, dm, dg = ffn_bwd(sv["x%d" % j], dx, sv["y%d" % j], sv["gu%d" % j], m3, g, wl[slot],
                                            wl[2 + slot])
        hb = sv["h%d" % j]
        dmod[i][j] = dm
        d_norm_g[i][j] = jnp.sum(dg, axis=(0, 1))
        Fc = wl[slot].shape[2]
        dw_in = matmul_tn(hb.reshape(-1, D), dgu.reshape(2, BL * T, 2 * Fc), Fc, "dw_ffn_in")
        dw_out = matmul_tn(ab_.reshape(-1, 2 * Fc), dyb.reshape(1, -1, D), D, "dw_ffn_out")
        return dx, dw_in, dw_out.reshape(N_CHIPS, -1, D)

    pending, tok = None, 0.0
    for i in reversed(range(L)):
        wl, sv = wts[i], saved[i]
        a = i // 2
        dx, dw_in1, dw_out1 = ffn_back(i, 2, 1, dx, tok)
        m3 = mod[i, :, 3:6]
        g = norm_g_full[i, 1].reshape(1, D)
        if i % 2 == 0:
            w_pw = wl[5].reshape(D, D)
            wdw = pad_taps(w_dw_full[a])
            du2, u3b, dyb, dgate, vec = conv_out_bwd(dx, sv["y1"], sv["u2"], m3, cm_ln_g[a].reshape(1, D),
                                                     cm_ln_b[a].reshape(1, D), w_pw)
            dx, hb, dab, dwdw, dbglu, dm, dg = conv_glu_bwd(sv["x1"], dx, du2, sv["u"], m3, g, wl[4],
                                                            cm_b_glu[a].reshape(1, -1), wdw)
            dm = dm.at[:, 2:3, :].set(dgate)
            vec = jnp.sum(vec, axis=0)
            g_cm["b_pw"][a], g_cm["ln_g"][a], g_cm["ln_b"][a], g_cm["b_dw"][a] = vec[0], vec[1], vec[2], vec[3]
            g_cm["w_dw"][a] = jnp.sum(dwdw, axis=0)[:KC]
            g_cm["b_glu"][a] = jnp.sum(dbglu, axis=(0, 1))
            dw_a = matmul_tn(hb.reshape(-1, D), dab.reshape(1, -1, 2 * D), D // 2, "dw_glu")
            dw_b = matmul_tn(u3b.reshape(-1, D), dyb.reshape(1, -1, D), D, "dw_sq").reshape(N_CHIPS, -1, D)
        else:
            w_main, w_ab = dn_weights(i)
            w_out = wl[5].reshape(W, D)
            do, dz, ogb, dyb, dgate, dog = dn_out_bwd(dx, sv["y1"], sv["o"], sv["z"], m3, dn_o_g[a].reshape(1, Dh), w_out)
            dq, dk, dv, dgb, dbb = dn_chunk_bwd(*sv["qkvgb"], sv["sp"], do)
            dc, dab, small = dn_conv_bwd(dq, dk, dv, dgb, dbb, sv["pre"], sv["ab"], w_sconv_full[a],
                                         row128(dn_a_log[a]), row128(dn_dt_bias[a]))
            dx, hb, dproj, dws, dm, dg = dn_proj_bwd(sv["x1"], dx, dc, sv["pre"], dz, dab, m3, g, w_main, w_ab,
                                                     w_sconv_full[a])
            dm = dm.at[:, 2:3, :].set(dgate)
            small = jnp.sum(small, axis=0)
            g_dn["a_log"][a], g_dn["dt_bias"][a] = small[0, :H], small[1, :H]
            g_dn["o_g"][a] = jnp.sum(dog, axis=(0, 1))
            g_dn["w_sconv"][a] = jnp.sum(dws, axis=0)
            dw_main = matmul_tn(hb.reshape(-1, D), dproj.reshape(1, -1, 4 * W), W, "dw_dn_main")
            dw_ab = matmul_tn(hb.reshape(-1, D), dab.reshape(1, -1, LANES), LANES, "dw_dn_ab")
            full = jnp.concatenate([jnp.transpose(dw_main, (1, 0, 2)).reshape(D, 4 * W), dw_ab[0][:, :2 * H]], axis=1)
            dw_a = jnp.transpose(full.reshape(D, N_CHIPS, -1), (1, 0, 2))
            dw_b = matmul_tn(ogb.reshape(-1, W), dyb.reshape(1, -1, D), D, "dw_sq").reshape(N_CHIPS, -1, D)
        dmod[i][1] = dm
        d_norm_g[i][1] = jnp.sum(dg, axis=(0, 1))
        if i > 0:
            dx, dw_in0, dw_out0 = ffn_back(i, 0, 0, dx)
            started = reduce_start([dw_in0, dw_in1, dw_out0, dw_out1, dw_a, dw_b], c_idx, "reduce_start_%d" % i)
            if pending is not None:
                big[pending[1]] = reduce_finish(pending[0], dx, where, "reduce_wait_%d" % pending[1])
            pending, tok = (started, i), started[4][0, 0]
        else:
            part_a = reduce_start([dw_in1, dw_out1, dw_a, dw_b], c_idx, "reduce_start_0a")
            if pending is not None:
                big[pending[1]] = reduce_finish(pending[0], dx, where, "reduce_wait_%d" % pending[1])
            dx, dw_in0, dw_out0 = ffn_back(0, 0, 0, dx, part_a[4][0, 0])
            sums_a = reduce_finish(part_a, dx, where, "reduce_wait_0a")

    part = dict(
        norm_g=jnp.stack([jnp.stack(r) for r in d_norm_g]),
        cm_b_glu=jnp.stack(g_cm["b_glu"]), cm_w_dw=jnp.stack(g_cm["w_dw"]), cm_b_dw=jnp.stack(g_cm["b_dw"]),
        cm_ln_g=jnp.stack(g_cm["ln_g"]), cm_ln_b=jnp.stack(g_cm["ln_b"]), cm_b_pw=jnp.stack(g_cm["b_pw"]),
        dn_w_sconv=jnp.stack(g_dn["w_sconv"]), dn_a_log=jnp.stack(g_dn["a_log"]), dn_dt_bias=jnp.stack(g_dn["dt_bias"]),
        dn_o_g=jnp.stack(g_dn["o_g"]), final_g=jnp.sum(d_final_g, axis=(0, 1)),
        loss=jnp.sum(loss_part[:, 0, 0]).reshape(1))
    dmod_loc = jnp.stack([jnp.concatenate(r, axis=1) for r in dmod]).reshape(L, BL, C9)
    keys = list(part)
    packed = _pack([part[k] for k in keys] + [dmod_loc])
    R = packed.shape[0]
    gathered = allgather8(packed).reshape(N_DEV, R, LANES)
    summed = _unpack(sum_devices(gathered).reshape(-1), [part[k].shape for k in keys])
    tot = dict(zip(keys, summed))
    n_small = sum(int(part[k].size) for k in keys)
    dmod_all = gathered.reshape(N_DEV, -1)[:, n_small:n_small + L * BL * C9].reshape(N_DEV, L, BL, C9)
    dmod_all = jnp.transpose(dmod_all, (1, 0, 2, 3)).reshape(L, NB, C9)
    dmod_cols = lax.dynamic_slice_in_dim(dmod_all, chip * Ca, Ca, axis=2)
    g_w_ada, g_b_ada = ada_bwd(c_all, dmod_cols, dmod_all)
    delta, new_m, new_v = {}, {}, {}
    part_b = reduce_start([dw_in0, dw_out0], c_idx, "reduce_start_0b", g_w_ada)
    delta["w_ada"], new_m["w_ada"], new_v["w_ada"] = _adamw_any(w_ada, g_w_ada, m_w_ada, v_w_ada, "adamw_w_ada",
                                                                 part_b[4])
    sums_b = reduce_finish(part_b, new_v["w_ada"], where, "reduce_wait_0b")
    big[0] = [sums_b[0], sums_a[0], sums_b[1], sums_a[1], sums_a[2], sums_a[3]]

    def my_cols(full):
        n = full.shape[-1] // N_CHIPS
        return lax.dynamic_slice_in_dim(full, chip * n, n, axis=full.ndim - 1)

    grads = dict(
        norm_g=my_cols(tot["norm_g"]), w_ada=g_w_ada, b_ada=g_b_ada.reshape(L, C9),
        w_ffn_in=jnp.stack([jnp.stack([big[i][0], big[i][1]]) for i in range(L)]),
        w_ffn_out=jnp.stack([jnp.stack([big[i][2], big[i][3]]) for i in range(L)]),
        cm_w_glu=jnp.stack([big[i][4] for i in range(0, L, 2)]), cm_b_glu=tot["cm_b_glu"], cm_w_dw=my_cols(tot["cm_w_dw"]),
        cm_b_dw=tot["cm_b_dw"], cm_ln_g=tot["cm_ln_g"], cm_ln_b=tot["cm_ln_b"],
        cm_w_pw=jnp.stack([big[i][5] for i in range(0, L, 2)]), cm_b_pw=tot["cm_b_pw"],
        dn_w_in=jnp.stack([big[i][4] for i in range(1, L, 2)]), dn_w_sconv=my_cols(tot["dn_w_sconv"]),
        dn_a_log=tot["dn_a_log"], dn_dt_bias=tot["dn_dt_bias"], dn_o_g=tot["dn_o_g"],
        dn_w_out=jnp.stack([big[i][5] for i in range(1, L, 2)]), final_g=tot["final_g"])

    large = ("w_ada", "w_ffn_in", "w_ffn_out", "cm_w_glu", "cm_w_pw", "dn_w_in", "dn_w_out")
    for n in large[1:]:
        delta[n], new_m[n], new_v[n] = _adamw_any(weights[n], grads[n], mom_m[n], mom_v[n], "adamw_" + n)
    rest = [n for n in names if n not in large]
    shapes = [weights[n].shape for n in rest]
    pd, pm, pv = adamw(_pack([weights[n] for n in rest]), _pack([grads[n] for n in rest]),
                       _pack([mom_m[n] for n in rest]), _pack([mom_v[n] for n in rest]), "adamw_small")
    for n, d_, m_, v_ in zip(rest, _unpack(pd.reshape(-1), shapes), _unpack(pm.reshape(-1), shapes),
                             _unpack(pv.reshape(-1), shapes)):
        delta[n], new_m[n], new_v[n] = d_, m_, v_

    return (tot["loss"].reshape(()), dx, *[grads[n] for n in names], *[delta[n] for n in names],
            *[new_m[n] for n in names], *[new_v[n] for n in names])
```

```python
import functools

import jax
import jax.numpy as jnp
from jax import lax
from jax.experimental import pallas as pl
from jax.experimental.pallas import tpu as pltpu

F32 = jnp.float32
BF16 = jnp.bfloat16
EPS = 1e-6
CHUNK = 64
CHUNKS_PER_STEP = 4
N_CHIPS = 4
N_DEV = 8
LANES = 128
SUBLANES = 8
CONV_HALO = 32
SCONV_HALO = 8
VMEM_LIMIT_V7X = 60 * 1024 * 1024
HI = lax.Precision.HIGHEST
MESH = pl.DeviceIdType.MESH
HBM_SPEC = pl.BlockSpec(memory_space=pltpu.HBM)

ADAM_LR, ADAM_B1, ADAM_B2, ADAM_EPS, ADAM_WD, ADAM_STEP = 0.001, 0.9, 0.999, 1e-08, 0.01, 10


def _cparams(n_axes):
    return pltpu.CompilerParams(dimension_semantics=("arbitrary",) * n_axes, vmem_limit_bytes=VMEM_LIMIT_V7X)


def _tile(n, pref, mult=8):
    for t in range(min(n, pref) // mult * mult, 0, -mult):
        if n % t == 0:
            return t
    return n


def _mm(a, b):
    return lax.dot_general(a.astype(BF16), b.astype(BF16), (((1,), (0,)), ((), ())), preferred_element_type=F32)


def _mm_nt(a, b):
    return lax.dot_general(a.astype(BF16), b.astype(BF16), (((1,), (1,)), ((), ())), preferred_element_type=F32)


def _mm_tn(a, b):
    return lax.dot_general(a.astype(BF16), b.astype(BF16), (((0,), (0,)), ((), ())), preferred_element_type=F32)


def _sigmoid(x):
    return jax.nn.sigmoid(x)


def _dsilu(x, s):
    return s * (1.0 + x * (1.0 - s))


def _softplus(x):
    return jnp.maximum(x, 0.0) + jnp.log(1.0 + jnp.exp(-jnp.abs(x)))


def _modnorm(x, g, scale, shift):
    r = lax.rsqrt(jnp.mean(x * x, axis=-1, keepdims=True) + EPS)
    return (x * r) * g * (1.0 + scale) + shift


def _modnorm_bwd(x, g, scale, dh):
    r = lax.rsqrt(jnp.mean(x * x, axis=-1, keepdims=True) + EPS)
    xn = x * r
    dshift = jnp.sum(dh, axis=0, keepdims=True)
    dscale = jnp.sum(dh * (xn * g), axis=0, keepdims=True)
    dhn = dh * (1.0 + scale)
    dg = jnp.sum(dhn * xn, axis=0, keepdims=True)
    dxn = dhn * g
    dx = r * (dxn - xn * jnp.mean(dxn * xn, axis=-1, keepdims=True))
    return dx, dg, dscale, dshift


def _sum0(a):
    return jnp.sum(a, axis=0, keepdims=True)


def ffn_fwd(x, mod3, g, w_in, w_out):
    B, T, D = x.shape
    Fc = w_in.shape[2]
    w_in = w_in.reshape(2, 2, D, Fc)
    w_out = w_out.reshape(2, Fc, D)
    tm = _tile(T, 512)

    def body(x_ref, mod_ref, g_ref, wi_ref, wo_ref, xo_ref, y_ref, h_ref, gu_ref, acc_s):
        f = pl.program_id(2)

        @pl.when(f == 0)
        def _():
            h = _modnorm(x_ref[...], g_ref[...], mod_ref[1:2, :], mod_ref[0:1, :])
            h_ref[...] = h.astype(BF16)
            acc_s[...] = jnp.zeros_like(acc_s)

        h = h_ref[...]
        gt = _mm(h, wi_ref[0])
        up = _mm(h, wi_ref[1])
        gu_ref[0] = gt.astype(BF16)
        gu_ref[1] = up.astype(BF16)
        a = gt * _sigmoid(gt) * up
        acc_s[...] += _mm(a, wo_ref[...])

        @pl.when(f == 1)
        def _():
            y = acc_s[...]
            y_ref[...] = y
            xo_ref[...] = x_ref[...] + 0.5 * (1.0 + mod_ref[2:3, :]) * y

    tok = pl.BlockSpec((None, tm, D), lambda b, t, f: (b, t, 0))
    return pl.pallas_call(
        body, name="ffn_fwd", grid=(B, T // tm, 2),
        in_specs=[tok,
                  pl.BlockSpec((None, 3, D), lambda b, t, f: (b, 0, 0)),
                  pl.BlockSpec((1, D), lambda b, t, f: (0, 0)),
                  pl.BlockSpec((2, None, D, Fc), lambda b, t, f: (0, f, 0, 0)),
                  pl.BlockSpec((None, Fc, D), lambda b, t, f: (f, 0, 0))],
        out_specs=[tok, tok, tok, pl.BlockSpec((2, None, tm, Fc), lambda b, t, f: (0, b, t, f))],
        out_shape=[jax.ShapeDtypeStruct((B, T, D), F32)] * 2
        + [jax.ShapeDtypeStruct((B, T, D), BF16), jax.ShapeDtypeStruct((2, B, T, 2 * Fc), BF16)],
        scratch_shapes=[pltpu.VMEM((tm, D), F32)],
        compiler_params=_cparams(3),
    )(x, mod3, g, w_in, w_out)


def ffn_bwd_part(part, dres, gu, mod3, w_in, w_out, first=None, y=None, x=None, g=None):
    B, T, D = dres.shape
    Fc = w_in.shape[2]
    F = 2 * Fc
    w_in = w_in.reshape(2, 2, D, Fc)
    w_out = w_out.reshape(2, Fc, D)
    tm = _tile(T, 256)

    def half(dy, gu_ref, wi_ref, wo_ref, a_ref, dgu_ref):
        gt = gu_ref[0].astype(F32)
        up = gu_ref[1].astype(F32)
        sg = _sigmoid(gt)
        silu = gt * sg
        a_ref[...] = (silu * up).astype(BF16)
        da = _mm_nt(dy, wo_ref[...])
        dup = (da * silu).astype(BF16)
        dgt = (da * up * _dsilu(gt, sg)).astype(BF16)
        dgu_ref[0] = dgt
        dgu_ref[1] = dup
        return _mm_nt(dgt, wi_ref[0]) + _mm_nt(dup, wi_ref[1])

    tok = pl.BlockSpec((None, tm, D), lambda b, t: (b, t, 0))
    per_b3 = pl.BlockSpec((None, 3, D), lambda b, t: (b, 0, 0))
    per_b1 = pl.BlockSpec((None, 1, D), lambda b, t: (b, 0, 0))
    gu_spec = pl.BlockSpec((2, None, tm, Fc), lambda b, t: (0, b, t, part))
    a_spec = pl.BlockSpec((None, tm, Fc), lambda b, t: (b, t, part))
    wi_spec = pl.BlockSpec((2, None, D, Fc), lambda b, t: (0, part, 0, 0))
    wo_spec = pl.BlockSpec((None, Fc, D), lambda b, t: (part, 0, 0))
    a_shape = jax.ShapeDtypeStruct((B, T, F), BF16)
    dgu_shape = jax.ShapeDtypeStruct((2, B, T, F), BF16)

    if part == 0:
        def body(dres_ref, y_ref, gu_ref, mod_ref, wi_ref, wo_ref, a_ref, dgu_ref, dy_ref, dh_ref, dgate_ref):
            @pl.when(pl.program_id(1) == 0)
            def _():
                dgate_ref[...] = jnp.zeros_like(dgate_ref)

            dres = dres_ref[...]
            dy = (0.5 * (1.0 + mod_ref[2:3, :]) * dres).astype(BF16)
            dy_ref[...] = dy
            dgate_ref[...] += _sum0(dres * (0.5 * y_ref[...]))
            dh_ref[...] = half(dy, gu_ref, wi_ref, wo_ref, a_ref, dgu_ref)

        return pl.pallas_call(
            body, name="ffn_bwd_a", grid=(B, T // tm),
            in_specs=[tok, tok, gu_spec, per_b3, wi_spec, wo_spec],
            out_specs=[a_spec, gu_spec, tok, tok, per_b1],
            out_shape=[a_shape, dgu_shape, jax.ShapeDtypeStruct((B, T, D), BF16), jax.ShapeDtypeStruct((B, T, D), F32),
                       jax.ShapeDtypeStruct((B, 1, D), F32)],
            compiler_params=_cparams(2),
        )(dres, y, gu, mod3, w_in, w_out)

    a_full, dgu_full, dy, dh0 = first

    def body(x_ref, dres_ref, dy_ref, dh0_ref, gu_ref, mod_ref, g_ref, wi_ref, wo_ref, a_any, dgu_any,
             dx_ref, a_ref, dgu_ref, dmod_ref, dg_ref):
        @pl.when(pl.program_id(1) == 0)
        def _():
            dmod_ref[...] = jnp.zeros_like(dmod_ref)
            dg_ref[...] = jnp.zeros_like(dg_ref)

        dh = dh0_ref[...] + half(dy_ref[...], gu_ref, wi_ref, wo_ref, a_ref, dgu_ref)
        dxn, dg, dscale, dshift = _modnorm_bwd(x_ref[...], g_ref[...], mod_ref[1:2, :], dh)
        dx_ref[...] = dres_ref[...] + dxn
        dmod_ref[0:1, :] += dshift
        dmod_ref[1:2, :] += dscale
        dg_ref[...] += dg

    return pl.pallas_call(
        body, name="ffn_bwd_b", grid=(B, T // tm),
        in_specs=[tok, tok, tok, tok, gu_spec, per_b3, pl.BlockSpec((1, D), lambda b, t: (0, 0)), wi_spec, wo_spec,
                  ANY_SPEC, ANY_SPEC],
        out_specs=[tok, a_spec, gu_spec, per_b3, per_b1],
        out_shape=[jax.ShapeDtypeStruct((B, T, D), F32), a_shape, dgu_shape, jax.ShapeDtypeStruct((B, 3, D), F32),
                   jax.ShapeDtypeStruct((B, 1, D), F32)],
        input_output_aliases={9: 1, 10: 2},
        compiler_params=_cparams(2),
    )(x, dres, dy, dh0, gu, mod3, g, w_in, w_out, a_full, dgu_full)


def matmul_tn(xm, ym, bm, name):
    N, K = xm.shape
    GY, _, MY = ym.shape
    per = MY // bm
    nb = GY * per
    tn = _tile(N, 512)

    def body(x_ref, y_ref, o_ref, acc_s):
        n = pl.program_id(1)

        @pl.when(n == 0)
        def _():
            acc_s[...] = jnp.zeros_like(acc_s)

        acc_s[...] += _mm_tn(x_ref[...], y_ref[...])

        @pl.when(n == N // tn - 1)
        def _():
            o_ref[...] = acc_s[...].astype(BF16)

    return pl.pallas_call(
        body, name=name, grid=(nb, N // tn),
        in_specs=[pl.BlockSpec((tn, K), lambda m, n: (n, 0)),
                  pl.BlockSpec((None, tn, bm), lambda m, n: (m // per, n, m % per))],
        out_specs=pl.BlockSpec((None, K, bm), lambda m, n: (m, 0, 0)),
        out_shape=jax.ShapeDtypeStruct((nb, K, bm), BF16),
        scratch_shapes=[pltpu.VMEM((K, bm), F32)],
        compiler_params=_cparams(2),
    )(xm, ym)


def final_loss(x, fg, target):
    B, T, D = x.shape
    tm = _tile(T, 512)

    def body(x_ref, g_ref, t_ref, dx_ref, dfg_ref, loss_ref):
        t = pl.program_id(1)

        @pl.when(t == 0)
        def _():
            dfg_ref[...] = jnp.zeros_like(dfg_ref)
            loss_ref[...] = jnp.zeros_like(loss_ref)

        xv = x_ref[...]
        g = g_ref[...]
        r = lax.rsqrt(jnp.mean(xv * xv, axis=-1, keepdims=True) + EPS)
        xn = xv * r
        err = xn * g - t_ref[...]
        tok_loss = jnp.mean(err * err, axis=-1, keepdims=True)
        loss_ref[...] += 0.5 * jnp.sum(tok_loss, axis=0, keepdims=True)
        dy = err * (1.0 / D)
        dfg_ref[...] += _sum0(dy * xn)
        dxn = dy * g
        dx_ref[...] = r * (dxn - xn * jnp.mean(dxn * xn, axis=-1, keepdims=True))

    tok = pl.BlockSpec((None, tm, D), lambda b, t: (b, t, 0))
    return pl.pallas_call(
        body, name="final_loss", grid=(B, T // tm),
        in_specs=[tok, pl.BlockSpec((1, D), lambda b, t: (0, 0)), tok],
        out_specs=[tok, pl.BlockSpec((None, 1, D), lambda b, t: (b, 0, 0)),
                   pl.BlockSpec((None, 1, LANES), lambda b, t: (b, 0, 0))],
        out_shape=[jax.ShapeDtypeStruct((B, T, D), F32), jax.ShapeDtypeStruct((B, 1, D), F32),
                   jax.ShapeDtypeStruct((B, 1, LANES), F32)],
        compiler_params=_cparams(2),
    )(x, fg, target)


def _past_halo_spec(tm, halo, width):
    return pl.BlockSpec((None, halo, width), lambda b, t: (b, jnp.maximum(t * (tm // halo) - 1, 0), 0))


def _future_halo_spec(tm, halo, width, T):
    return pl.BlockSpec((None, halo, width), lambda b, t: (b, jnp.minimum((t + 1) * (tm // halo), T // halo - 1), 0))


def _fill_shifted(ext_s):
    n = ext_s.shape[1]
    for b in range(1, SUBLANES):
        ext_s[b, 0:n - SUBLANES, :] = ext_s[0, pl.ds(b, n - SUBLANES), :]


def _shifted(ext_s, offset, rows):
    a, b = divmod(offset, SUBLANES)
    return ext_s[b, pl.ds(SUBLANES * a, rows), :]


def _glu_fwd(h, w_ref, bias):
    D = h.shape[1]
    a = jnp.concatenate([_mm(h, w_ref[0]), _mm(h, w_ref[1])], axis=1) + bias[:, :D]
    b = jnp.concatenate([_mm(h, w_ref[2]), _mm(h, w_ref[3])], axis=1) + bias[:, D:]
    return a, b


def conv_glu_fwd(x, mod3, g, w_glu, b_glu):
    B, T, D = x.shape
    tm = _tile(T, 512)

    def body(x_ref, mod_ref, g_ref, w_ref, b_ref, u_ref):
        h = _modnorm(x_ref[...], g_ref[...], mod_ref[1:2, :], mod_ref[0:1, :]).astype(BF16)
        a, b = _glu_fwd(h, w_ref, b_ref[...])
        u_ref[...] = a * _sigmoid(b)

    tok = pl.BlockSpec((None, tm, D), lambda b, t: (b, t, 0))
    return pl.pallas_call(
        body, name="conv_glu_fwd", grid=(B, T // tm),
        in_specs=[tok, pl.BlockSpec((None, 3, D), lambda b, t: (b, 0, 0)),
                  pl.BlockSpec((1, D), lambda b, t: (0, 0)),
                  pl.BlockSpec((4, D, D // 2), lambda b, t: (0, 0, 0)),
                  pl.BlockSpec((1, 2 * D), lambda b, t: (0, 0))],
        out_specs=tok, out_shape=jax.ShapeDtypeStruct((B, T, D), F32),
        compiler_params=_cparams(2),
    )(x, mod3, g, w_glu, b_glu)


def _layer_norm_parts(u2):
    mu = jnp.mean(u2, axis=-1, keepdims=True)
    xc = u2 - mu
    rs = lax.rsqrt(jnp.mean(xc * xc, axis=-1, keepdims=True) + EPS)
    return xc * rs, rs


def conv_out_fwd(x, u, mod3, w_dw, b_dw, ln_g, ln_b, w_pw, b_pw):
    B, T, D = x.shape
    K = w_dw.shape[0] - 1
    tm = _tile(T, 512)

    def body(x_ref, u_ref, halo_ref, mod_ref, wdw_ref, bdw_ref, lg_ref, lb_ref, wpw_ref, bpw_ref,
             xo_ref, y_ref, u2_ref, ext_s):
        t = pl.program_id(1)
        ext_s[0, 0:CONV_HALO, :] = jnp.where(t > 0, halo_ref[...], 0.0)
        ext_s[0, CONV_HALO:, :] = u_ref[...]
        _fill_shifted(ext_s)
        acc = jnp.broadcast_to(bdw_ref[...], (tm, D))
        for k in range(K):
            acc = acc + wdw_ref[k:k + 1, :] * _shifted(ext_s, CONV_HALO - (K - 1) + k, tm)
        u2_ref[...] = acc
        xh, _ = _layer_norm_parts(acc)
        l = xh * lg_ref[...] + lb_ref[...]
        u3 = l * _sigmoid(l)
        y = _mm(u3, wpw_ref[...]) + bpw_ref[...]
        y_ref[...] = y
        xo_ref[...] = x_ref[...] + (1.0 + mod_ref[2:3, :]) * y

    tok = pl.BlockSpec((None, tm, D), lambda b, t: (b, t, 0))
    vec = pl.BlockSpec((1, D), lambda b, t: (0, 0))
    return pl.pallas_call(
        body, name="conv_out_fwd", grid=(B, T // tm),
        in_specs=[tok, tok, _past_halo_spec(tm, CONV_HALO, D), pl.BlockSpec((None, 3, D), lambda b, t: (b, 0, 0)),
                  pl.BlockSpec((K + 1, D), lambda b, t: (0, 0)), vec, vec, vec,
                  pl.BlockSpec((D, D), lambda b, t: (0, 0)), vec],
        out_specs=[tok, tok, tok], out_shape=[jax.ShapeDtypeStruct((B, T, D), F32)] * 3,
        scratch_shapes=[pltpu.VMEM((SUBLANES, tm + CONV_HALO, D), F32)],
        compiler_params=_cparams(2),
    )(x, u, u, mod3, w_dw, b_dw, ln_g, ln_b, w_pw, b_pw)


def conv_out_bwd(dres, y, u2, mod3, ln_g, ln_b, w_pw):
    B, T, D = dres.shape
    tm = _tile(T, 512)

    def body(dres_ref, y_ref, u2_ref, mod_ref, lg_ref, lb_ref, wpw_ref, du2_ref, u3_ref, dy_ref, dgate_ref, vec_ref):
        t = pl.program_id(1)

        @pl.when(t == 0)
        def _():
            dgate_ref[...] = jnp.zeros_like(dgate_ref)
            vec_ref[...] = jnp.zeros_like(vec_ref)

        dres = dres_ref[...]
        dy = (1.0 + mod_ref[2:3, :]) * dres
        dy_ref[...] = dy.astype(BF16)
        dgate_ref[...] += _sum0(dres * y_ref[...])
        xh, rs = _layer_norm_parts(u2_ref[...])
        lg = lg_ref[...]
        l = xh * lg + lb_ref[...]
        sg = _sigmoid(l)
        u3_ref[...] = (l * sg).astype(BF16)
        du3 = _mm_nt(dy, wpw_ref[...])
        dl = du3 * _dsilu(l, sg)
        dxh = dl * lg
        du2 = rs * (dxh - jnp.mean(dxh, axis=-1, keepdims=True) - xh * jnp.mean(dxh * xh, axis=-1, keepdims=True))
        du2_ref[...] = du2
        vec_ref[0:1, :] += _sum0(dy)
        vec_ref[1:2, :] += _sum0(dl * xh)
        vec_ref[2:3, :] += _sum0(dl)
        vec_ref[3:4, :] += _sum0(du2)

    tok = pl.BlockSpec((None, tm, D), lambda b, t: (b, t, 0))
    tokb = pl.BlockSpec((None, tm, D), lambda b, t: (b, t, 0))
    vec = pl.BlockSpec((1, D), lambda b, t: (0, 0))
    return pl.pallas_call(
        body, name="conv_out_bwd", grid=(B, T // tm),
        in_specs=[tok, tok, tok, pl.BlockSpec((None, 3, D), lambda b, t: (b, 0, 0)), vec, vec,
                  pl.BlockSpec((D, D), lambda b, t: (0, 0))],
        out_specs=[tok, tokb, tokb, pl.BlockSpec((None, 1, D), lambda b, t: (b, 0, 0)),
                   pl.BlockSpec((None, 4, D), lambda b, t: (b, 0, 0))],
        out_shape=[jax.ShapeDtypeStruct((B, T, D), F32), jax.ShapeDtypeStruct((B, T, D), BF16),
                   jax.ShapeDtypeStruct((B, T, D), BF16), jax.ShapeDtypeStruct((B, 1, D), F32),
                   jax.ShapeDtypeStruct((B, 4, D), F32)],
        compiler_params=_cparams(2),
    )(dres, y, u2, mod3, ln_g, ln_b, w_pw)


def conv_glu_bwd(x, dres, du2, u, mod3, g, w_glu, b_glu, w_dw):
    B, T, D = x.shape
    K = w_dw.shape[0] - 1
    tm = _tile(T, 256)
    nt = T // tm

    def body(x_ref, dres_ref, du2_ref, du2h_ref, u_ref, uh_ref, mod_ref, g_ref, w_ref, b_ref, wdw_ref,
             dx_ref, h_ref, dab_ref, dwdw_ref, dbglu_ref, dmod_ref, dg_ref, extu_s, extd_s):
        t = pl.program_id(1)

        @pl.when(t == 0)
        def _():
            dwdw_ref[...] = jnp.zeros_like(dwdw_ref)
            dbglu_ref[...] = jnp.zeros_like(dbglu_ref)
            dmod_ref[...] = jnp.zeros_like(dmod_ref)
            dg_ref[...] = jnp.zeros_like(dg_ref)

        du2 = du2_ref[...]
        extu_s[0, 0:CONV_HALO, :] = jnp.where(t > 0, uh_ref[...], 0.0)
        extu_s[0, CONV_HALO:, :] = u_ref[...]
        extd_s[0, 0:tm, :] = du2
        extd_s[0, tm:, :] = jnp.where(t < nt - 1, du2h_ref[...], 0.0)
        _fill_shifted(extu_s)
        _fill_shifted(extd_s)
        du = jnp.zeros((tm, D), F32)
        for k in range(K):
            du = du + wdw_ref[k:k + 1, :] * _shifted(extd_s, K - 1 - k, tm)
            dwdw_ref[k:k + 1, :] += _sum0(du2 * _shifted(extu_s, CONV_HALO - (K - 1) + k, tm))
        xv = x_ref[...]
        h = _modnorm(xv, g_ref[...], mod_ref[1:2, :], mod_ref[0:1, :]).astype(BF16)
        h_ref[...] = h
        a, b = _glu_fwd(h, w_ref, b_ref[...])
        sb = _sigmoid(b)
        da = du * sb
        db = du * a * sb * (1.0 - sb)
        dbglu_ref[:, 0:D] += _sum0(da)
        dbglu_ref[:, D:] += _sum0(db)
        da = da.astype(BF16)
        db = db.astype(BF16)
        dab_ref[:, 0:D] = da
        dab_ref[:, D:] = db
        Dh2 = D // 2
        dh = (_mm_nt(da[:, :Dh2], w_ref[0]) + _mm_nt(da[:, Dh2:], w_ref[1])
              + _mm_nt(db[:, :Dh2], w_ref[2]) + _mm_nt(db[:, Dh2:], w_ref[3]))
        dxn, dg, dscale, dshift = _modnorm_bwd(xv, g_ref[...], mod_ref[1:2, :], dh)
        dx_ref[...] = dres_ref[...] + dxn
        dmod_ref[0:1, :] += dshift
        dmod_ref[1:2, :] += dscale
        dg_ref[...] += dg

    tok = pl.BlockSpec((None, tm, D), lambda b, t: (b, t, 0))
    return pl.pallas_call(
        body, name="conv_glu_bwd", grid=(B, nt),
        in_specs=[tok, tok, tok, _future_halo_spec(tm, CONV_HALO, D, T), tok, _past_halo_spec(tm, CONV_HALO, D),
                  pl.BlockSpec((None, 3, D), lambda b, t: (b, 0, 0)), pl.BlockSpec((1, D), lambda b, t: (0, 0)),
                  pl.BlockSpec((4, D, D // 2), lambda b, t: (0, 0, 0)), pl.BlockSpec((1, 2 * D), lambda b, t: (0, 0)),
                  pl.BlockSpec((K + 1, D), lambda b, t: (0, 0))],
        out_specs=[tok, tok, pl.BlockSpec((None, tm, 2 * D), lambda b, t: (b, t, 0)),
                   pl.BlockSpec((None, K + 1, D), lambda b, t: (b, 0, 0)),
                   pl.BlockSpec((None, 1, 2 * D), lambda b, t: (b, 0, 0)),
                   pl.BlockSpec((None, 3, D), lambda b, t: (b, 0, 0)),
                   pl.BlockSpec((None, 1, D), lambda b, t: (b, 0, 0))],
        out_shape=[jax.ShapeDtypeStruct((B, T, D), F32), jax.ShapeDtypeStruct((B, T, D), BF16),
                   jax.ShapeDtypeStruct((B, T, 2 * D), BF16), jax.ShapeDtypeStruct((B, K + 1, D), F32),
                   jax.ShapeDtypeStruct((B, 1, 2 * D), F32), jax.ShapeDtypeStruct((B, 3, D), F32),
                   jax.ShapeDtypeStruct((B, 1, D), F32)],
        scratch_shapes=[pltpu.VMEM((SUBLANES, tm + CONV_HALO, D), F32)] * 2,
        compiler_params=_cparams(2),
    )(x, dres, du2, du2, u, u, mod3, g, w_glu, b_glu, w_dw)


def dn_proj_fwd(x, mod3, g, w_main, w_ab):
    B, T, D = x.shape
    W = w_main.shape[1] // 4
    tm = _tile(T, 512)

    def body(x_ref, mod_ref, g_ref, wm_ref, wab_ref, pre_ref, z_ref, ab_ref):
        h = _modnorm(x_ref[...], g_ref[...], mod_ref[1:2, :], mod_ref[0:1, :]).astype(BF16)
        for p in range(3):
            pre_ref[:, p * W:(p + 1) * W] = _mm(h, wm_ref[:, p * W:(p + 1) * W])
        z_ref[...] = _mm(h, wm_ref[:, 3 * W:])
        ab_ref[...] = _mm(h, wab_ref[...])

    return pl.pallas_call(
        body, name="dn_proj_fwd", grid=(B, T // tm),
        in_specs=[pl.BlockSpec((None, tm, D), lambda b, t: (b, t, 0)), pl.BlockSpec((None, 3, D), lambda b, t: (b, 0, 0)),
                  pl.BlockSpec((1, D), lambda b, t: (0, 0)), pl.BlockSpec((D, 4 * W), lambda b, t: (0, 0)),
                  pl.BlockSpec((D, LANES), lambda b, t: (0, 0))],
        out_specs=[pl.BlockSpec((None, tm, 3 * W), lambda b, t: (b, t, 0)),
                   pl.BlockSpec((None, tm, W), lambda b, t: (b, t, 0)),
                   pl.BlockSpec((None, tm, LANES), lambda b, t: (b, t, 0))],
        out_shape=[jax.ShapeDtypeStruct((B, T, 3 * W), F32), jax.ShapeDtypeStruct((B, T, W), F32),
                   jax.ShapeDtypeStruct((B, T, LANES), F32)],
        compiler_params=_cparams(2),
    )(x, mod3, g, w_main, w_ab)


def _sconv(ext_s, w_ref, tm, K):
    acc = w_ref[0:1, :] * ext_s[pl.ds(SCONV_HALO - (K - 1), tm), :]
    for k in range(1, K):
        acc = acc + w_ref[k:k + 1, :] * ext_s[pl.ds(SCONV_HALO - (K - 1) + k, tm), :]
    return acc


def _lane_col(val, lane, idx):
    return jnp.sum(jnp.where(lane == idx, val, 0.0), axis=1, keepdims=True)


def dn_conv_fwd(pre, ab, w_sconv, alog_row, dt_row, H):
    B, T, W3 = pre.shape
    W = W3 // 3
    Dh = W // H
    K = w_sconv.shape[0]
    tm = _tile(T, 512)

    def body(pre_ref, halo_ref, ab_ref, w_ref, alog_ref, dt_ref, q_ref, k_ref, v_ref, gb_ref, bb_ref, ext_s):
        t = pl.program_id(1)
        ext_s[0:SCONV_HALO, :] = jnp.where(t > 0, halo_ref[...], 0.0)
        ext_s[SCONV_HALO:, :] = pre_ref[...]
        cv = _sconv(ext_s, w_ref, tm, K)
        qkv = cv * _sigmoid(cv)
        ab = ab_ref[...]
        lane = lax.broadcasted_iota(jnp.int32, ab.shape, 1)
        g_all = -jnp.exp(alog_ref[...]) * _softplus(ab + dt_ref[...])
        beta_all = _sigmoid(ab)
        for h in range(H):
            q_ref[h] = qkv[:, h * Dh:(h + 1) * Dh]
            k_ref[h] = qkv[:, W + h * Dh:W + (h + 1) * Dh]
            v_ref[h] = qkv[:, 2 * W + h * Dh:2 * W + (h + 1) * Dh]
            gb_ref[h] = jnp.broadcast_to(_lane_col(g_all, lane, h), (tm, Dh))
            bb_ref[h] = jnp.broadcast_to(_lane_col(beta_all, lane, H + h), (tm, Dh))

    hm = pl.BlockSpec((None, H, tm, Dh), lambda b, t: (b, 0, t, 0))
    row = pl.BlockSpec((1, LANES), lambda b, t: (0, 0))
    return pl.pallas_call(
        body, name="dn_conv_fwd", grid=(B, T // tm),
        in_specs=[pl.BlockSpec((None, tm, W3), lambda b, t: (b, t, 0)), _past_halo_spec(tm, SCONV_HALO, W3),
                  pl.BlockSpec((None, tm, LANES), lambda b, t: (b, t, 0)),
                  pl.BlockSpec((K, W3), lambda b, t: (0, 0)), row, row],
        out_specs=[hm] * 5, out_shape=[jax.ShapeDtypeStruct((B, H, T, Dh), F32)] * 5,
        scratch_shapes=[pltpu.VMEM((tm + SCONV_HALO, W3), F32)],
        compiler_params=_cparams(2),
    )(pre, pre, ab, w_sconv, alog_row, dt_row)


def _bdot(spec):
    return lambda a, b: jnp.einsum(spec, a.astype(BF16), b.astype(BF16), preferred_element_type=F32)


_NN, _NT, _TN = "gij,gjk->gik", "gik,gjk->gij", "gki,gkj->gij"


def _make_bdots():
    nn_, nt_, tn_ = _bdot(_NN), _bdot(_NT), _bdot(_TN)

    @jax.custom_vjp
    def nn(a, b):
        return nn_(a, b)

    @jax.custom_vjp
    def nt(a, b):
        return nt_(a, b)

    @jax.custom_vjp
    def tn(a, b):
        return tn_(a, b)

    nn.defvjp(lambda a, b: (nn_(a, b), (a, b)), lambda r, d: (nt_(d, r[1]), tn_(r[0], d)))
    nt.defvjp(lambda a, b: (nt_(a, b), (a, b)), lambda r, d: (nn_(d, r[1]), tn_(d, r[0])))
    tn.defvjp(lambda a, b: (tn_(a, b), (a, b)), lambda r, d: (nt_(r[1], d), nn_(r[0], d)))
    return nn, nt, tn


def _unit_lower_inverse(A):
    hdot = functools.partial(jnp.einsum, precision=lax.Precision.HIGH, preferred_element_type=F32)
    C = A.shape[-1]

    def impl(A):
        eye = (lax.broadcasted_iota(jnp.int32, A.shape, 1) == lax.broadcasted_iota(jnp.int32, A.shape, 2)).astype(F32)
        Tm = eye - A
        Ap = A
        for _ in range(max(1, (C - 1).bit_length()) - 1):
            Ap = hdot(_NN, Ap, Ap)
            Tm = Tm + hdot(_NN, Tm, Ap)
        return Tm

    @jax.custom_vjp
    def inv(A):
        return impl(A)

    def fwd(A):
        Tm = impl(A)
        return Tm, Tm

    def bwd(Tm, dT):
        return (-hdot(_NT, hdot(_TN, Tm, dT), Tm),)

    inv.defvjp(fwd, bwd)
    return inv(A)


def _chunk_fn(q, k, v, gb, bb, S):
    nn, nt, tn = _make_bdots()
    G, C, Dh = q.shape
    hdot = functools.partial(jnp.einsum, precision=lax.Precision.HIGH, preferred_element_type=F32)
    q = q * lax.rsqrt(jnp.sum(q * q, axis=-1, keepdims=True) + EPS) * (Dh ** -0.5)
    k = k * lax.rsqrt(jnp.sum(k * k, axis=-1, keepdims=True) + EPS)
    row = lax.broadcasted_iota(jnp.int32, (G, C, C), 1)
    col = lax.broadcasted_iota(jnp.int32, (G, C, C), 2)
    causal = row >= col
    strict = row > col
    gc = hdot(_NN, causal.astype(F32), gb)
    spread = jnp.full((G, C, Dh), 1.0 / Dh, F32)
    gi = hdot(_NT, gc, spread)
    gj = hdot(_NT, spread, gc)
    decay = jnp.where(causal, jnp.exp(jnp.where(causal, gi - gj, 0.0)), 0.0)
    kb = k * bb
    vb = v * bb
    A = jnp.where(strict, nt(kb, k) * decay, 0.0)
    Tm = _unit_lower_inverse(A)
    eg = jnp.exp(gc)
    u = nn(Tm, vb)
    w = nn(Tm, kb * eg)
    qg = q * eg
    intra = nt(q, k) * decay
    glast = hdot(_NN, jnp.ones((G, C, C), F32), gb)
    kd = k * jnp.exp(glast - gc)
    v_new = u - nn(w, S)
    o = nn(qg, S) + nn(intra, v_new)
    egl = jnp.exp(glast)
    S_new = S * jnp.concatenate([egl] * (Dh // C), axis=1) + tn(kd, v_new)
    return o, S_new


def dn_chunk_fwd(q, k, v, gb, bb):
    B, H, T, Dh = q.shape
    NC = T // CHUNK
    NS = _tile(NC, CHUNKS_PER_STEP, 1)

    def body(q_ref, k_ref, v_ref, gb_ref, bb_ref, o_ref, sp_ref, S_s):
        @pl.when(pl.program_id(1) == 0)
        def _():
            S_s[...] = jnp.zeros_like(S_s)

        def one_chunk(j, carry):
            rows = pl.ds(pl.multiple_of(j * CHUNK, CHUNK), CHUNK)
            S = S_s[...]
            sp_ref[j] = S
            o, S_new = _chunk_fn(q_ref[:, rows, :], k_ref[:, rows, :], v_ref[:, rows, :], gb_ref[:, rows, :],
                                 bb_ref[:, rows, :], S)
            o_ref[:, rows, :] = o
            S_s[...] = S_new
            return carry

        lax.fori_loop(0, NS, one_chunk, 0)

    hm = pl.BlockSpec((None, H, NS * CHUNK, Dh), lambda b, n: (b, 0, n, 0))
    return pl.pallas_call(
        body, name="dn_chunk_fwd", grid=(B, NC // NS),
        in_specs=[hm] * 5,
        out_specs=[hm, pl.BlockSpec((None, NS, H, Dh, Dh), lambda b, n: (b, n, 0, 0, 0))],
        out_shape=[jax.ShapeDtypeStruct((B, H, T, Dh), F32), jax.ShapeDtypeStruct((B, NC, H, Dh, Dh), F32)],
        scratch_shapes=[pltpu.VMEM((H, Dh, Dh), F32)],
        compiler_params=_cparams(2),
    )(q, k, v, gb, bb)


def dn_chunk_bwd(q, k, v, gb, bb, s_prev, do):
    B, H, T, Dh = q.shape
    NC = T // CHUNK
    NS = _tile(NC, CHUNKS_PER_STEP, 1)
    NG = NC // NS

    def body(q_ref, k_ref, v_ref, gb_ref, bb_ref, sp_ref, do_ref, dq_ref, dk_ref, dv_ref, dgb_ref, dbb_ref, dS_s):
        @pl.when(pl.program_id(1) == 0)
        def _():
            dS_s[...] = jnp.zeros_like(dS_s)

        def one_chunk(jj, carry):
            j = NS - 1 - jj
            rows = pl.ds(pl.multiple_of(j * CHUNK, CHUNK), CHUNK)
            _, vjp = jax.vjp(_chunk_fn, q_ref[:, rows, :], k_ref[:, rows, :], v_ref[:, rows, :], gb_ref[:, rows, :],
                             bb_ref[:, rows, :], sp_ref[j])
            dq, dk, dv, dgb, dbb, dS = vjp((do_ref[:, rows, :], dS_s[...]))
            dq_ref[:, rows, :] = dq
            dk_ref[:, rows, :] = dk
            dv_ref[:, rows, :] = dv
            dgb_ref[:, rows, :] = dgb
            dbb_ref[:, rows, :] = dbb
            dS_s[...] = dS
            return carry

        lax.fori_loop(0, NS, one_chunk, 0)

    hm = pl.BlockSpec((None, H, NS * CHUNK, Dh), lambda b, n: (b, 0, NG - 1 - n, 0))
    return pl.pallas_call(
        body, name="dn_chunk_bwd", grid=(B, NG),
        in_specs=[hm] * 5 + [pl.BlockSpec((None, NS, H, Dh, Dh), lambda b, n: (b, NG - 1 - n, 0, 0, 0)), hm],
        out_specs=[hm] * 5, out_shape=[jax.ShapeDtypeStruct((B, H, T, Dh), F32)] * 5,
        scratch_shapes=[pltpu.VMEM((H, Dh, Dh), F32)],
        compiler_params=_cparams(2),
    )(q, k, v, gb, bb, s_prev, do)


def _head_norm(o, og):
    r = lax.rsqrt(jnp.mean(o * o, axis=-1, keepdims=True) + EPS)
    return o * r, r


def dn_out_fwd(x, o, z, mod3, o_g, w_out):
    B, T, D = x.shape
    _, H, _, Dh = o.shape
    W = H * Dh
    tm = _tile(T, 512)

    def body(x_ref, o_ref, z_ref, mod_ref, og_ref, w_ref, xo_ref, y_ref):
        parts = []
        for h in range(H):
            on, _ = _head_norm(o_ref[h], og_ref[...])
            zz = z_ref[:, h * Dh:(h + 1) * Dh]
            parts.append((on * og_ref[...] * (zz * _sigmoid(zz))).astype(BF16))
        y = _mm(jnp.concatenate(parts, axis=1), w_ref[...])
        y_ref[...] = y
        xo_ref[...] = x_ref[...] + (1.0 + mod_ref[2:3, :]) * y

    tok = pl.BlockSpec((None, tm, D), lambda b, t: (b, t, 0))
    return pl.pallas_call(
        body, name="dn_out_fwd", grid=(B, T // tm),
        in_specs=[tok, pl.BlockSpec((None, H, tm, Dh), lambda b, t: (b, 0, t, 0)),
                  pl.BlockSpec((None, tm, W), lambda b, t: (b, t, 0)), pl.BlockSpec((None, 3, D), lambda b, t: (b, 0, 0)),
                  pl.BlockSpec((1, Dh), lambda b, t: (0, 0)), pl.BlockSpec((W, D), lambda b, t: (0, 0))],
        out_specs=[tok, tok], out_shape=[jax.ShapeDtypeStruct((B, T, D), F32)] * 2,
        compiler_params=_cparams(2),
    )(x, o, z, mod3, o_g, w_out)


def dn_out_bwd(dres, y, o, z, mod3, o_g, w_out):
    B, T, D = dres.shape
    _, H, _, Dh = o.shape
    W = H * Dh
    tm = _tile(T, 512)

    def body(dres_ref, y_ref, o_ref, z_ref, mod_ref, og_ref, w_ref, do_ref, dz_ref, ogb_ref, dy_ref, dgate_ref, dog_ref):
        t = pl.program_id(1)

        @pl.when(t == 0)
        def _():
            dgate_ref[...] = jnp.zeros_like(dgate_ref)
            dog_ref[...] = jnp.zeros_like(dog_ref)

        dres = dres_ref[...]
        dy = ((1.0 + mod_ref[2:3, :]) * dres).astype(BF16)
        dy_ref[...] = dy
        dgate_ref[...] += _sum0(dres * y_ref[...])
        dog = _mm_nt(dy, w_ref[...])
        og = og_ref[...]
        for h in range(H):
            ov = o_ref[h]
            xn, r = _head_norm(ov, og)
            zz = z_ref[:, h * Dh:(h + 1) * Dh]
            sg = _sigmoid(zz)
            sz = zz * sg
            d = dog[:, h * Dh:(h + 1) * Dh]
            ogb_ref[:, h * Dh:(h + 1) * Dh] = (xn * og * sz).astype(BF16)
            dz_ref[:, h * Dh:(h + 1) * Dh] = d * (xn * og) * _dsilu(zz, sg)
            don = d * sz
            dog_ref[...] += _sum0(don * xn)
            dxn = don * og
            do_ref[h] = r * (dxn - xn * jnp.mean(dxn * xn, axis=-1, keepdims=True))

    tok = pl.BlockSpec((None, tm, D), lambda b, t: (b, t, 0))
    tokw = pl.BlockSpec((None, tm, W), lambda b, t: (b, t, 0))
    hm = pl.BlockSpec((None, H, tm, Dh), lambda b, t: (b, 0, t, 0))
    return pl.pallas_call(
        body, name="dn_out_bwd", grid=(B, T // tm),
        in_specs=[tok, tok, hm, tokw, pl.BlockSpec((None, 3, D), lambda b, t: (b, 0, 0)),
                  pl.BlockSpec((1, Dh), lambda b, t: (0, 0)), pl.BlockSpec((W, D), lambda b, t: (0, 0))],
        out_specs=[hm, tokw, tokw, tok, pl.BlockSpec((None, 1, D), lambda b, t: (b, 0, 0)),
                   pl.BlockSpec((None, 1, Dh), lambda b, t: (b, 0, 0))],
        out_shape=[jax.ShapeDtypeStruct((B, H, T, Dh), F32), jax.ShapeDtypeStruct((B, T, W), F32),
                   jax.ShapeDtypeStruct((B, T, W), BF16), jax.ShapeDtypeStruct((B, T, D), BF16),
                   jax.ShapeDtypeStruct((B, 1, D), F32), jax.ShapeDtypeStruct((B, 1, Dh), F32)],
        compiler_params=_cparams(2),
    )(dres, y, o, z, mod3, o_g, w_out)


def dn_conv_bwd(dq, dk, dv, dgb, dbb, pre, ab, w_sconv, alog_row, dt_row):
    B, H, T, Dh = dq.shape
    W = H * Dh
    W3 = 3 * W
    K = w_sconv.shape[0]
    tm = _tile(T, 256)

    def body(dq_ref, dk_ref, dv_ref, dgb_ref, dbb_ref, pre_ref, halo_ref, ab_ref, w_ref, alog_ref, dt_ref,
             dc_ref, dab_ref, small_ref, ext_s):
        t = pl.program_id(1)

        @pl.when(t == 0)
        def _():
            small_ref[...] = jnp.zeros_like(small_ref)

        ext_s[0:SCONV_HALO, :] = jnp.where(t > 0, halo_ref[...], 0.0)
        ext_s[SCONV_HALO:, :] = pre_ref[...]
        cv = _sconv(ext_s, w_ref, tm, K)
        dsl = _dsilu(cv, _sigmoid(cv))
        ab = ab_ref[...]
        lane = lax.broadcasted_iota(jnp.int32, ab.shape, 1)
        dg_all = jnp.zeros_like(ab)
        db_all = jnp.zeros_like(ab)
        for h in range(H):
            dc_ref[:, h * Dh:(h + 1) * Dh] = dq_ref[h] * dsl[:, h * Dh:(h + 1) * Dh]
            dc_ref[:, W + h * Dh:W + (h + 1) * Dh] = dk_ref[h] * dsl[:, W + h * Dh:W + (h + 1) * Dh]
            dc_ref[:, 2 * W + h * Dh:2 * W + (h + 1) * Dh] = dv_ref[h] * dsl[:, 2 * W + h * Dh:2 * W + (h + 1) * Dh]
            dg_all = dg_all + jnp.where(lane == h, jnp.sum(dgb_ref[h], axis=1, keepdims=True), 0.0)
            db_all = db_all + jnp.where(lane == H + h, jnp.sum(dbb_ref[h], axis=1, keepdims=True), 0.0)
        xa = ab + dt_ref[...]
        ea = -jnp.exp(alog_ref[...])
        g_all = ea * _softplus(xa)
        da = dg_all * ea * _sigmoid(xa)
        beta = _sigmoid(ab)
        dab_ref[...] = da + db_all * beta * (1.0 - beta)
        small_ref[0:1, :] += _sum0(dg_all * g_all)
        small_ref[1:2, :] += _sum0(da)

    hm = pl.BlockSpec((None, H, tm, Dh), lambda b, t: (b, 0, t, 0))
    row = pl.BlockSpec((1, LANES), lambda b, t: (0, 0))
    return pl.pallas_call(
        body, name="dn_conv_bwd", grid=(B, T // tm),
        in_specs=[hm] * 5 + [pl.BlockSpec((None, tm, W3), lambda b, t: (b, t, 0)), _past_halo_spec(tm, SCONV_HALO, W3),
                             pl.BlockSpec((None, tm, LANES), lambda b, t: (b, t, 0)),
                             pl.BlockSpec((K, W3), lambda b, t: (0, 0)), row, row],
        out_specs=[pl.BlockSpec((None, tm, W3), lambda b, t: (b, t, 0)), pl.BlockSpec((None, tm, LANES), lambda b, t: (b, t, 0)),
                   pl.BlockSpec((None, 2, LANES), lambda b, t: (b, 0, 0))],
        out_shape=[jax.ShapeDtypeStruct((B, T, W3), F32), jax.ShapeDtypeStruct((B, T, LANES), F32),
                   jax.ShapeDtypeStruct((B, 2, LANES), F32)],
        scratch_shapes=[pltpu.VMEM((tm + SCONV_HALO, W3), F32)],
        compiler_params=_cparams(2),
    )(dq, dk, dv, dgb, dbb, pre, pre, ab, w_sconv, alog_row, dt_row)


def dn_proj_bwd(x, dres, dc, pre, dz, dab, mod3, g, w_main, w_ab, w_sconv):
    B, T, D = x.shape
    W3 = dc.shape[2]
    W = W3 // 3
    K = w_sconv.shape[0]
    tm = _tile(T, 256)
    nt = T // tm

    def body(x_ref, dres_ref, dc_ref, dch_ref, pre_ref, preh_ref, dz_ref, dab_ref, mod_ref, g_ref, wm_ref, wab_ref, ws_ref,
             dx_ref, h_ref, dproj_ref, dws_ref, dmod_ref, dg_ref, extp_s, extd_s):
        t = pl.program_id(1)

        @pl.when(t == 0)
        def _():
            dws_ref[...] = jnp.zeros_like(dws_ref)
            dmod_ref[...] = jnp.zeros_like(dmod_ref)
            dg_ref[...] = jnp.zeros_like(dg_ref)

        dc = dc_ref[...]
        extp_s[0:SCONV_HALO, :] = jnp.where(t > 0, preh_ref[...], 0.0)
        extp_s[SCONV_HALO:, :] = pre_ref[...]
        extd_s[0:tm, :] = dc
        extd_s[tm:, :] = jnp.where(t < nt - 1, dch_ref[...], 0.0)
        dpre = jnp.zeros((tm, W3), F32)
        for k in range(K):
            dpre = dpre + ws_ref[k:k + 1, :] * extd_s[pl.ds(K - 1 - k, tm), :]
            dws_ref[k:k + 1, :] += _sum0(dc * extp_s[pl.ds(SCONV_HALO - (K - 1) + k, tm), :])
        dpre = dpre.astype(BF16)
        dzb = dz_ref[...].astype(BF16)
        dproj_ref[:, 0:W3] = dpre
        dproj_ref[:, W3:] = dzb
        dh = _mm_nt(dab_ref[...], wab_ref[...]) + _mm_nt(dzb, wm_ref[:, W3:])
        for p in range(3):
            dh = dh + _mm_nt(dpre[:, p * W:(p + 1) * W], wm_ref[:, p * W:(p + 1) * W])
        xv = x_ref[...]
        h_ref[...] = _modnorm(xv, g_ref[...], mod_ref[1:2, :], mod_ref[0:1, :]).astype(BF16)
        dxn, dg, dscale, dshift = _modnorm_bwd(xv, g_ref[...], mod_ref[1:2, :], dh)
        dx_ref[...] = dres_ref[...] + dxn
        dmod_ref[0:1, :] += dshift
        dmod_ref[1:2, :] += dscale
        dg_ref[...] += dg

    tok = pl.BlockSpec((None, tm, D), lambda b, t: (b, t, 0))
    tok3 = pl.BlockSpec((None, tm, W3), lambda b, t: (b, t, 0))
    return pl.pallas_call(
        body, name="dn_proj_bwd", grid=(B, nt),
        in_specs=[tok, tok, tok3, _future_halo_spec(tm, SCONV_HALO, W3, T), tok3, _past_halo_spec(tm, SCONV_HALO, W3),
                  pl.BlockSpec((None, tm, W), lambda b, t: (b, t, 0)), pl.BlockSpec((None, tm, LANES), lambda b, t: (b, t, 0)),
                  pl.BlockSpec((None, 3, D), lambda b, t: (b, 0, 0)), pl.BlockSpec((1, D), lambda b, t: (0, 0)),
                  pl.BlockSpec((D, 4 * W), lambda b, t: (0, 0)), pl.BlockSpec((D, LANES), lambda b, t: (0, 0)),
                  pl.BlockSpec((K, W3), lambda b, t: (0, 0))],
        out_specs=[tok, tok, pl.BlockSpec((None, tm, 4 * W), lambda b, t: (b, t, 0)),
                   pl.BlockSpec((None, K, W3), lambda b, t: (b, 0, 0)), pl.BlockSpec((None, 3, D), lambda b, t: (b, 0, 0)),
                   pl.BlockSpec((None, 1, D), lambda b, t: (b, 0, 0))],
        out_shape=[jax.ShapeDtypeStruct((B, T, D), F32), jax.ShapeDtypeStruct((B, T, D), BF16),
                   jax.ShapeDtypeStruct((B, T, 4 * W), BF16), jax.ShapeDtypeStruct((B, K, W3), F32),
                   jax.ShapeDtypeStruct((B, 3, D), F32), jax.ShapeDtypeStruct((B, 1, D), F32)],
        scratch_shapes=[pltpu.VMEM((tm + SCONV_HALO, W3), F32), pltpu.VMEM((tm + SCONV_HALO, W3), F32)],
        compiler_params=_cparams(2),
    )(x, dres, dc, dc, pre, pre, dz, dab, mod3, g, w_main, w_ab, w_sconv)


def ada_fwd(c_all, w_ada, b_cols):
    L, D, Ca = w_ada.shape
    NB = c_all.shape[0]

    def body(c_ref, w_ref, b_ref, o_ref):
        cv = c_ref[...]
        o_ref[...] = _mm(cv * _sigmoid(cv), w_ref[...]) + b_ref[...]

    return pl.pallas_call(
        body, name="ada_fwd", grid=(L,),
        in_specs=[pl.BlockSpec((NB, D), lambda i: (0, 0)), pl.BlockSpec((None, D, Ca), lambda i: (i, 0, 0)),
                  pl.BlockSpec((None, 1, Ca), lambda i: (i, 0, 0))],
        out_specs=pl.BlockSpec((None, NB, Ca), lambda i: (i, 0, 0)),
        out_shape=jax.ShapeDtypeStruct((L, NB, Ca), F32),
        compiler_params=_cparams(1),
    )(c_all, w_ada, b_cols)


def ada_bwd(c_all, dmod_cols, dmod_all):
    L, NB, Ca = dmod_cols.shape
    D = c_all.shape[1]
    C9 = dmod_all.shape[2]

    def body(c_ref, dc_ref, da_ref, gw_ref, gb_ref):
        cv = c_ref[...]
        gw_ref[...] = _mm_tn(cv * _sigmoid(cv), dc_ref[...])
        gb_ref[...] = _sum0(da_ref[...])

    return pl.pallas_call(
        body, name="ada_bwd", grid=(L,),
        in_specs=[pl.BlockSpec((NB, D), lambda i: (0, 0)), pl.BlockSpec((None, NB, Ca), lambda i: (i, 0, 0)),
                  pl.BlockSpec((None, NB, C9), lambda i: (i, 0, 0))],
        out_specs=[pl.BlockSpec((None, D, Ca), lambda i: (i, 0, 0)), pl.BlockSpec((None, 1, C9), lambda i: (i, 0, 0))],
        out_shape=[jax.ShapeDtypeStruct((L, D, Ca), F32), jax.ShapeDtypeStruct((L, 1, C9), F32)],
        compiler_params=_cparams(1),
    )(c_all, dmod_cols, dmod_all)


def adamw(w, g, m, v, name, token=None):
    R, C = w.shape
    tr = _tile(R, max(8, (1 << 18) // C))
    if token is None:
        token = jnp.zeros((8, LANES), F32)

    def body(w_ref, g_ref, m_ref, v_ref, t_ref, d_ref, mo_ref, vo_ref):
        gv = g_ref[...] + t_ref[0:1, 0:1]
        mn = ADAM_B1 * m_ref[...] + (1.0 - ADAM_B1) * gv
        vn = ADAM_B2 * v_ref[...] + (1.0 - ADAM_B2) * (gv * gv)
        m_hat = mn / (1.0 - ADAM_B1 ** ADAM_STEP)
        v_hat = vn / (1.0 - ADAM_B2 ** ADAM_STEP)
        d_ref[...] = -ADAM_LR * (m_hat / (jnp.sqrt(v_hat) + ADAM_EPS) + ADAM_WD * w_ref[...])
        mo_ref[...] = mn
        vo_ref[...] = vn

    blk = pl.BlockSpec((tr, C), lambda i: (i, 0))
    return pl.pallas_call(
        body, name=name, grid=(R // tr,), in_specs=[blk] * 4 + [pl.BlockSpec((8, LANES), lambda i: (0, 0))],
        out_specs=[blk] * 3, out_shape=[jax.ShapeDtypeStruct((R, C), F32)] * 3, compiler_params=_cparams(1),
    )(w, g, m, v, token)


def sum_devices(a):
    n, R, C = a.shape

    def body(a_ref, o_ref):
        s = a_ref[0]
        for d in range(1, n):
            s = s + a_ref[d]
        o_ref[...] = s

    return pl.pallas_call(
        body, name="sum_devices", out_shape=jax.ShapeDtypeStruct((R, C), F32),
        compiler_params=pltpu.CompilerParams(vmem_limit_bytes=VMEM_LIMIT_V7X),
    )(a)


def _place():
    x, y, c = lax.axis_index("x"), lax.axis_index("y"), lax.axis_index("c")
    return x, y, c


def _other_chips(x, y):
    return [(2 * (1 - x) + y, 1 - x, y), (2 * x + (1 - y), x, 1 - y), (2 * (1 - x) + (1 - y), 1 - x, 1 - y)]


def allgather8(block):
    m_per, n = block.shape

    def body(x_ref, out_ref, send_sems, recv_sems, local_sem):
        x, y, c = _place()
        me, sibling = (x, y, c), (x, y, 1 - c)
        chips = [(1 - x, y), (x, 1 - y), (1 - x, 1 - y)]

        def rows(px, py, pc):
            return out_ref.at[pl.ds((4 * px + 2 * py + pc) * m_per, m_per), :]

        def copy(k, blk, to, src=None):
            return pltpu.make_async_remote_copy(
                src_ref=rows(*blk) if src is None else src, dst_ref=rows(*blk),
                send_sem=send_sems.at[k], recv_sem=recv_sems.at[k], device_id=to, device_id_type=MESH)

        mine = pltpu.make_async_copy(x_ref, rows(*me), local_sem)
        mine.start()
        first = [copy(0, me, sibling, src=x_ref)]
        first += [copy(1 + j, me, (*chip, c), src=x_ref) for j, chip in enumerate(chips)]
        for cp in first:
            cp.start()
        passed = [copy(4 + j, (*chip, c), sibling) for j, chip in enumerate(chips)]
        for j, chip in enumerate(chips):
            copy(1 + j, (*chip, c), me).wait_recv()
            passed[j].start()
        copy(0, sibling, me).wait_recv()
        for j, chip in enumerate(chips):
            copy(4 + j, (*chip, 1 - c), me).wait_recv()
        for cp in first + passed:
            cp.wait_send()
        mine.wait()

    return pl.pallas_call(
        body, name="allgather8", out_shape=jax.ShapeDtypeStruct((N_DEV * m_per, n), block.dtype),
        in_specs=[pl.BlockSpec(memory_space=pltpu.VMEM)], out_specs=pl.BlockSpec(memory_space=pltpu.VMEM),
        scratch_shapes=[pltpu.SemaphoreType.DMA((7,)), pltpu.SemaphoreType.DMA((7,)), pltpu.SemaphoreType.DMA],
        compiler_params=pltpu.CompilerParams(vmem_limit_bytes=VMEM_LIMIT_V7X),
    )(block)


def _half(ref, c, rh):
    return ref.at[pl.ds(pl.multiple_of(c * rh, 16), rh), :]


def gather_weights(lands):
    K = len(lands)

    def body(*refs):
        ins, outs = refs[:K], refs[K:2 * K]
        ici_send, ici_recv, d2d_send, d2d_recv = refs[2 * K:]
        x, y, c = _place()
        me = 2 * x + y
        sibling = (x, y, 1 - c)
        others = _other_chips(x, y)
        sent = []
        for k in range(K):
            rh = ins[k].shape[1] // 2
            for r, (_, px, py) in enumerate(others):
                cp = pltpu.make_async_remote_copy(
                    src_ref=_half(ins[k].at[me], c, rh), dst_ref=_half(outs[k].at[me], c, rh),
                    send_sem=ici_send.at[k, r], recv_sem=ici_recv.at[k, r], device_id=(px, py, c), device_id_type=MESH)
                cp.start()
                sent.append(cp)
        forwards = []
        for k in range(K):
            rh = ins[k].shape[1] // 2
            for r, (pchip, px, py) in enumerate(others):
                landed = _half(outs[k].at[pchip], c, rh)
                pltpu.make_async_remote_copy(
                    src_ref=landed, dst_ref=landed, send_sem=ici_send.at[k, r], recv_sem=ici_recv.at[k, r],
                    device_id=(px, py, c), device_id_type=MESH).wait_recv()
                fw = pltpu.make_async_remote_copy(
                    src_ref=landed, dst_ref=landed, send_sem=d2d_send.at[k, r], recv_sem=d2d_recv.at[k, r],
                    device_id=sibling, device_id_type=MESH)
                fw.start()
                forwards.append(fw)
        for k in range(K):
            rh = ins[k].shape[1] // 2
            for r, (pchip, _, _) in enumerate(others):
                theirs = _half(outs[k].at[pchip], 1 - c, rh)
                pltpu.make_async_remote_copy(
                    src_ref=theirs, dst_ref=theirs, send_sem=d2d_send.at[k, r], recv_sem=d2d_recv.at[k, r],
                    device_id=sibling, device_id_type=MESH).wait_recv()
        for cp in sent + forwards:
            cp.wait_send()

    return pl.pallas_call(
        body, name="gather_weights",
        out_shape=[jax.ShapeDtypeStruct(s.shape, s.dtype) for s in lands],
        in_specs=[HBM_SPEC] * K, out_specs=[HBM_SPEC] * K, input_output_aliases={k: k for k in range(K)},
        scratch_shapes=[pltpu.SemaphoreType.DMA((K, 3))] * 4,
    )(*lands)


def pair_exchange(grads):
    K = len(grads)

    def body(*refs):
        ins, outs = refs[:K], refs[K:2 * K]
        send_sems, recv_sems = refs[2 * K:]
        x, y, c = _place()
        sibling = (x, y, 1 - c)
        copies = []
        for k in range(K):
            n, r, _ = ins[k].shape
            rh = r // 2
            cp = pltpu.make_async_remote_copy(
                src_ref=ins[k].at[:, pl.ds(pl.multiple_of((1 - c) * rh, 16), rh), :], dst_ref=outs[k],
                send_sem=send_sems.at[k], recv_sem=recv_sems.at[k], device_id=sibling, device_id_type=MESH)
            cp.start()
            copies.append(cp)
        for cp in copies:
            cp.wait_recv()
        for cp in copies:
            cp.wait_send()

    return pl.pallas_call(
        body, name="pair_exchange",
        out_shape=[jax.ShapeDtypeStruct((g.shape[0], g.shape[1] // 2, g.shape[2]), g.dtype) for g in grads],
        in_specs=[HBM_SPEC] * K, out_specs=[HBM_SPEC] * K,
        scratch_shapes=[pltpu.SemaphoreType.DMA((K,))] * 2,
    )(*grads)


def pair_add(grad, recv, c_idx):
    n, r, C = grad.shape
    rh = r // 2
    tr = _tile(rh, max(16, (1 << 19) // C), 16)
    grad = grad.reshape(n, 2, rh, C)

    def body(c_ref, g_ref, r_ref, o_ref):
        o_ref[...] = (g_ref[...].astype(F32) + r_ref[...].astype(F32)).astype(BF16)

    return pl.pallas_call(
        body, name="pair_add",
        grid_spec=pltpu.PrefetchScalarGridSpec(
            num_scalar_prefetch=1, grid=(n, rh // tr),
            in_specs=[pl.BlockSpec((None, None, tr, C), lambda d, i, c_ref: (d, c_ref[0], i, 0)),
                      pl.BlockSpec((None, tr, C), lambda d, i, c_ref: (d, i, 0))],
            out_specs=pl.BlockSpec((None, tr, C), lambda d, i, c_ref: (d, i, 0))),
        out_shape=jax.ShapeDtypeStruct((n, rh, C), BF16), compiler_params=_cparams(2),
    )(c_idx, grad, recv)


def chip_exchange(parts):
    K = len(parts)

    def body(*refs):
        ins, outs = refs[:K], refs[K:2 * K]
        send_sems, recv_sems = refs[2 * K:]
        x, y, c = _place()
        others = _other_chips(x, y)
        started = []
        for k in range(K):
            for r, (pchip, px, py) in enumerate(others):
                cp = pltpu.make_async_remote_copy(
                    src_ref=ins[k].at[pchip], dst_ref=outs[k].at[r], send_sem=send_sems.at[k, r],
                    recv_sem=recv_sems.at[k, r], device_id=(px, py, c), device_id_type=MESH)
                cp.start()
                started.append(cp)
        for cp in started:
            cp.wait_recv()
        for cp in started:
            cp.wait_send()

    return pl.pallas_call(
        body, name="chip_exchange",
        out_shape=[jax.ShapeDtypeStruct((3,) + p.shape[1:], p.dtype) for p in parts],
        in_specs=[HBM_SPEC] * K, out_specs=[HBM_SPEC] * K,
        scratch_shapes=[pltpu.SemaphoreType.DMA((K, 3))] * 2,
    )(*parts)


def chip_sum(parts, got, where):
    _, rh, C = parts.shape
    tr = _tile(rh, max(16, (1 << 19) // C), 16)
    nt = rh // tr

    def body(w_ref, p_ref, g_ref, o_ref):
        s = p_ref[...].astype(F32)
        for r in range(3):
            s = s + g_ref[r].astype(F32)
        o_ref[...] = s

    return pl.pallas_call(
        body, name="chip_sum",
        grid_spec=pltpu.PrefetchScalarGridSpec(
            num_scalar_prefetch=1, grid=(nt,),
            in_specs=[pl.BlockSpec((None, tr, C), lambda i, w_ref: (w_ref[0], i, 0)),
                      pl.BlockSpec((3, tr, C), lambda i, w_ref: (0, i, 0))],
            out_specs=pl.BlockSpec((tr, C), lambda i, w_ref: (w_ref[1] * nt + i, 0))),
        out_shape=jax.ShapeDtypeStruct((2 * rh, C), F32), compiler_params=_cparams(1),
    )(where, parts, got)


def pair_share(sums):
    K = len(sums)

    def body(*refs):
        ins, outs = refs[:K], refs[K:2 * K]
        send_sems, recv_sems = refs[2 * K:]
        x, y, c = _place()
        sibling = (x, y, 1 - c)
        started = []
        for k in range(K):
            rh = ins[k].shape[0] // 2
            cp = pltpu.make_async_remote_copy(
                src_ref=_half(ins[k], c, rh), dst_ref=_half(outs[k], c, rh), send_sem=send_sems.at[k],
                recv_sem=recv_sems.at[k], device_id=sibling, device_id_type=MESH)
            cp.start()
            started.append(cp)
        for k in range(K):
            rh = ins[k].shape[0] // 2
            theirs = _half(outs[k], 1 - c, rh)
            pltpu.make_async_remote_copy(
                src_ref=theirs, dst_ref=theirs, send_sem=send_sems.at[k], recv_sem=recv_sems.at[k],
                device_id=sibling, device_id_type=MESH).wait_recv()
        for cp in started:
            cp.wait_send()

    return pl.pallas_call(
        body, name="pair_share",
        out_shape=[jax.ShapeDtypeStruct(s.shape, s.dtype) for s in sums],
        in_specs=[HBM_SPEC] * K, out_specs=[HBM_SPEC] * K, input_output_aliases={k: k for k in range(K)},
        scratch_shapes=[pltpu.SemaphoreType.DMA((K,))] * 2,
    )(*sums)


SEM_SPEC = pl.BlockSpec(memory_space=pltpu.SEMAPHORE)
ANY_SPEC = pl.BlockSpec(memory_space=pl.ANY)
DATAFLOW = pltpu.SideEffectType.DATAFLOW_SIDE_EFFECTING


def _in_hbm(a):
    return pltpu.with_memory_space_constraint(a, pltpu.HBM)


def _ici_copies(srcs, dsts, send_sems, recv_sems, src_slice, dst_slice):
    x, y, c = _place()
    out = []
    for k in range(len(srcs)):
        for r, (pchip, px, py) in enumerate(_other_chips(x, y)):
            out.append(pltpu.make_async_remote_copy(
                src_ref=src_slice(srcs[k], r, pchip), dst_ref=dst_slice(dsts[k], r, pchip),
                send_sem=send_sems.at[3 * k + r], recv_sem=recv_sems.at[3 * k + r], device_id=(px, py, c),
                device_id_type=MESH))
    return out


def _exchange_start(bufs, lands, src_slice, dst_slice, name, after=None):
    K = len(bufs)
    same = lands is None
    n_thru = K if same else 2 * K
    n_in = n_thru + (after is not None)

    def body(*refs):
        ins = refs[:n_thru]
        send_sems, recv_sems = refs[n_in], refs[n_in + 1]
        token = refs[-1]
        srcs = ins[:K]
        dsts = srcs if same else ins[K:]
        for cp in _ici_copies(srcs, dsts, send_sems, recv_sems, src_slice, dst_slice):
            cp.start()
        token[...] = jnp.zeros_like(token)

    thru = list(bufs) + ([] if same else list(lands))
    res = pl.pallas_call(
        body, name=name,
        out_shape=[pltpu.SemaphoreType.DMA((3 * K,)), pltpu.SemaphoreType.DMA((3 * K,))]
        + [pltpu.HBM(a.shape, a.dtype) for a in thru] + [jax.ShapeDtypeStruct((8, LANES), F32)],
        in_specs=[HBM_SPEC] * n_thru + [ANY_SPEC] * (after is not None),
        out_specs=[SEM_SPEC, SEM_SPEC] + [HBM_SPEC] * n_thru + [pl.BlockSpec(memory_space=pltpu.VMEM)],
        input_output_aliases={i: 2 + i for i in range(n_thru)},
        compiler_params=pltpu.CompilerParams(has_side_effects=DATAFLOW),
    )(*[_in_hbm(a) for a in thru], *([] if after is None else [after]))
    return res[0], res[1], res[2:2 + K], (res[2:2 + K] if same else res[2 + K:2 + 2 * K]), res[-1]


def _exchange_wait(send_sems, recv_sems, bufs, lands, after, src_slice, dst_slice, name):
    K = len(bufs)
    same = lands is None
    n_thru = K if same else 2 * K

    def body(*refs):
        ins = refs[:n_thru]
        ssem, rsem = refs[n_thru], refs[n_thru + 1]
        srcs = ins[:K]
        dsts = srcs if same else ins[K:]
        copies = _ici_copies(srcs, dsts, ssem, rsem, src_slice, dst_slice)
        for cp in copies:
            cp.wait_send()
        for cp in copies:
            cp.wait_recv()

    thru = list(bufs) + ([] if same else list(lands))
    res = pl.pallas_call(
        body, name=name,
        out_shape=[pltpu.HBM(a.shape, a.dtype) for a in thru],
        in_specs=[HBM_SPEC] * n_thru + [SEM_SPEC, SEM_SPEC, ANY_SPEC],
        out_specs=[HBM_SPEC] * n_thru,
        input_output_aliases={i: i for i in range(n_thru)},
        compiler_params=pltpu.CompilerParams(has_side_effects=DATAFLOW),
    )(*thru, send_sems, recv_sems, after)
    return res[:K], (res[:K] if same else res[K:])


def _own_half(ref, r, pchip):
    x, y, c = _place()
    return _half(ref.at[2 * x + y], c, ref.shape[1] // 2)


def _their_half(ref, r, pchip):
    _, _, c = _place()
    return _half(ref.at[pchip], c, ref.shape[1] // 2)


def gather_start(lands, name, after=None):
    return _exchange_start(lands, None, _own_half, _own_half, name, after)


def gather_wait(handle, after, name):
    ssem, rsem, lands, _, _ = handle
    return _exchange_wait(ssem, rsem, lands, None, after, _own_half, _their_half, name)[1]


def pair_forward(lands):
    K = len(lands)

    def body(*refs):
        ins, outs = refs[:K], refs[K:2 * K]
        send_sems, recv_sems = refs[2 * K:]
        x, y, c = _place()
        sibling = (x, y, 1 - c)
        started = []
        for k in range(K):
            rh = ins[k].shape[1] // 2
            for r, (pchip, _, _) in enumerate(_other_chips(x, y)):
                cp = pltpu.make_async_remote_copy(
                    src_ref=_half(ins[k].at[pchip], c, rh), dst_ref=_half(outs[k].at[pchip], c, rh),
                    send_sem=send_sems.at[k, r], recv_sem=recv_sems.at[k, r], device_id=sibling, device_id_type=MESH)
                cp.start()
                started.append(cp)
        for k in range(K):
            rh = ins[k].shape[1] // 2
            for r, (pchip, _, _) in enumerate(_other_chips(x, y)):
                theirs = _half(outs[k].at[pchip], 1 - c, rh)
                pltpu.make_async_remote_copy(
                    src_ref=theirs, dst_ref=theirs, send_sem=send_sems.at[k, r], recv_sem=recv_sems.at[k, r],
                    device_id=sibling, device_id_type=MESH).wait_recv()
        for cp in started:
            cp.wait_send()

    return pl.pallas_call(
        body, name="pair_forward",
        out_shape=[jax.ShapeDtypeStruct(s.shape, s.dtype) for s in lands],
        in_specs=[HBM_SPEC] * K, out_specs=[HBM_SPEC] * K, input_output_aliases={k: k for k in range(K)},
        scratch_shapes=[pltpu.SemaphoreType.DMA((K, 3))] * 2,
    )(*lands)


def _to_chip(ref, r, pchip):
    return ref.at[pchip]


def _from_relation(ref, r, pchip):
    return ref.at[r]


def reduce_start(grads, c_idx, name, after=None):
    recv = pair_exchange(grads)
    parts = [pair_add(g, r, c_idx) for g, r in zip(grads, recv)]
    lands = [lax.empty((3,) + p.shape[1:], p.dtype) for p in parts]
    return _exchange_start(parts, lands, _to_chip, _from_relation, name, after)


def reduce_finish(handle, after, where, name):
    ssem, rsem, parts, lands, _ = handle
    parts, got = _exchange_wait(ssem, rsem, parts, lands, after, _to_chip, _from_relation, name)
    return pair_share([chip_sum(p, g, where) for p, g in zip(parts, got)])


def _pack(arrs):
    flat = jnp.concatenate([a.reshape(-1).astype(F32) for a in arrs])
    pad = (-flat.shape[0]) % (8 * LANES)
    return jnp.pad(flat, (0, pad)).reshape(-1, LANES)


def _unpack(flat, shapes):
    out, off = [], 0
    for s in shapes:
        n = 1
        for d in s:
            n *= d
        out.append(flat[off:off + n].reshape(s))
        off += n
    return out


def _adamw_any(w, g, m, v, name, token=None):
    shp = w.shape
    C = shp[-1]
    d, nm, nv = adamw(w.reshape(-1, C), g.reshape(-1, C), m.reshape(-1, C), v.reshape(-1, C), name, token)
    return d.reshape(shp), nm.reshape(shp), nv.reshape(shp)


def kernel(x, c, norm_g, w_ada, b_ada, w_ffn_in, w_ffn_out, cm_w_glu, cm_b_glu, cm_w_dw, cm_b_dw, cm_ln_g, cm_ln_b, cm_w_pw, cm_b_pw, dn_w_in, dn_w_sconv, dn_a_log, dn_dt_bias, dn_o_g, dn_w_out, final_g, loss_target, m_norm_g, m_w_ada, m_b_ada, m_w_ffn_in, m_w_ffn_out, m_cm_w_glu, m_cm_b_glu, m_cm_w_dw, m_cm_b_dw, m_cm_ln_g, m_cm_ln_b, m_cm_w_pw, m_cm_b_pw, m_dn_w_in, m_dn_w_sconv, m_dn_a_log, m_dn_dt_bias, m_dn_o_g, m_dn_w_out, m_final_g, v_norm_g, v_w_ada, v_b_ada, v_w_ffn_in, v_w_ffn_out, v_cm_w_glu, v_cm_b_glu, v_cm_w_dw, v_cm_b_dw, v_cm_ln_g, v_cm_ln_b, v_cm_w_pw, v_cm_b_pw, v_dn_w_in, v_dn_w_sconv, v_dn_a_log, v_dn_dt_bias, v_dn_o_g, v_dn_w_out, v_final_g):
    weights = dict(norm_g=norm_g, w_ada=w_ada, b_ada=b_ada, w_ffn_in=w_ffn_in, w_ffn_out=w_ffn_out, cm_w_glu=cm_w_glu,
                   cm_b_glu=cm_b_glu, cm_w_dw=cm_w_dw, cm_b_dw=cm_b_dw, cm_ln_g=cm_ln_g, cm_ln_b=cm_ln_b, cm_w_pw=cm_w_pw,
                   cm_b_pw=cm_b_pw, dn_w_in=dn_w_in, dn_w_sconv=dn_w_sconv, dn_a_log=dn_a_log, dn_dt_bias=dn_dt_bias,
                   dn_o_g=dn_o_g, dn_w_out=dn_w_out, final_g=final_g)
    mom_m = dict(norm_g=m_norm_g, w_ada=m_w_ada, b_ada=m_b_ada, w_ffn_in=m_w_ffn_in, w_ffn_out=m_w_ffn_out,
                 cm_w_glu=m_cm_w_glu, cm_b_glu=m_cm_b_glu, cm_w_dw=m_cm_w_dw, cm_b_dw=m_cm_b_dw, cm_ln_g=m_cm_ln_g,
                 cm_ln_b=m_cm_ln_b, cm_w_pw=m_cm_w_pw, cm_b_pw=m_cm_b_pw, dn_w_in=m_dn_w_in, dn_w_sconv=m_dn_w_sconv,
                 dn_a_log=m_dn_a_log, dn_dt_bias=m_dn_dt_bias, dn_o_g=m_dn_o_g, dn_w_out=m_dn_w_out, final_g=m_final_g)
    mom_v = dict(norm_g=v_norm_g, w_ada=v_w_ada, b_ada=v_b_ada, w_ffn_in=v_w_ffn_in, w_ffn_out=v_w_ffn_out,
                 cm_w_glu=v_cm_w_glu, cm_b_glu=v_cm_b_glu, cm_w_dw=v_cm_w_dw, cm_b_dw=v_cm_b_dw, cm_ln_g=v_cm_ln_g,
                 cm_ln_b=v_cm_ln_b, cm_w_pw=v_cm_w_pw, cm_b_pw=v_cm_b_pw, dn_w_in=v_dn_w_in, dn_w_sconv=v_dn_w_sconv,
                 dn_a_log=v_dn_a_log, dn_dt_bias=v_dn_dt_bias, dn_o_g=v_dn_o_g, dn_w_out=v_dn_w_out, final_g=v_final_g)
    names = list(weights)

    BL, T, D = x.shape
    L = norm_g.shape[0]
    NB = BL * N_DEV
    Ca = w_ada.shape[2]
    C9 = b_ada.shape[1]
    H = dn_a_log.shape[1]
    Dh = dn_o_g.shape[1]
    W = H * Dh
    KC = cm_w_dw.shape[1]
    KS = dn_w_sconv.shape[1]
    n_cm, n_dn = cm_w_glu.shape[0], dn_w_in.shape[0]
    ax, ay, ac = lax.axis_index("x"), lax.axis_index("y"), lax.axis_index("c")
    chip = 2 * ax + ay
    dev = 2 * chip + ac
    c_idx = ac.astype(jnp.int32).reshape(1)
    where = jnp.stack([chip, ac]).astype(jnp.int32)

    def layer_shards(i):
        sh = [w_ffn_in[i, 0], w_ffn_in[i, 1], w_ffn_out[i, 0], w_ffn_out[i, 1]]
        if i % 2 == 0:
            sh += [cm_w_glu[i // 2], cm_w_pw[i // 2]]
        else:
            sh += [dn_w_in[i // 2], dn_w_out[i // 2]]
        return [lax.dynamic_update_slice(lax.empty((N_CHIPS,) + s.shape, BF16), s.astype(BF16)[None], (chip, 0, 0))
                for s in sh]

    lands = [layer_shards(i) for i in range(L)]
    wts = [None] * L

    small_in = [c, norm_g, cm_w_dw, dn_w_sconv]
    gathered = allgather8(_pack(small_in))
    first = gather_start([lands[0][0], lands[0][2]], "gather_start_0a", gathered)
    gathered = gathered.reshape(N_DEV, -1)
    per_dev = [_unpack(gathered[d], [a.shape for a in small_in]) for d in range(N_DEV)]
    c_all = jnp.concatenate([p[0] for p in per_dev], axis=0)
    norm_g_full = jnp.concatenate([per_dev[2 * s][1] for s in range(N_CHIPS)], axis=-1)
    w_dw_full = jnp.concatenate([per_dev[2 * s][2] for s in range(N_CHIPS)], axis=-1)
    w_sconv_full = jnp.concatenate([per_dev[2 * s][3] for s in range(N_CHIPS)], axis=-1)

    b_cols = lax.dynamic_slice_in_dim(b_ada, chip * Ca, Ca, axis=1).reshape(L, 1, Ca)
    mod_part = ada_fwd(c_all, w_ada, b_cols)
    mod_g = allgather8((mod_part + first[4][0, 0]).reshape(-1, LANES))
    rest = gather_start([lands[0][k] for k in (1, 3, 4, 5)], "gather_start_0b", mod_g)
    mod_g = mod_g.reshape(N_DEV, L, NB, Ca)
    mod_all = jnp.concatenate([mod_g[2 * s] for s in range(N_CHIPS)], axis=-1)
    mod = lax.dynamic_slice_in_dim(mod_all, dev * BL, BL, axis=1).reshape(L, BL, 9, D)

    def dn_weights(i):
        full = jnp.transpose(wts[i][4], (1, 0, 2)).reshape(D, -1)
        return full[:, :4 * W], jnp.pad(full[:, 4 * W:], ((0, 0), (0, LANES - 2 * H)))

    def row128(v):
        return jnp.pad(v.reshape(1, -1), ((0, 0), (0, LANES - v.shape[-1])))

    def pad_taps(w):
        return jnp.pad(w, ((0, 1), (0, 0)))

    saved = []
    xs = x
    after = mod
    for i in range(L):
        tok = 0.0
        if i == 0:
            wl = wts[0] = [None] * 6
            wl[0], wl[2] = pair_forward(gather_wait(first, after, "gather_wait_0a"))
        else:
            wl = wts[i] = pair_forward(gather_wait(handle, after, "gather_wait_%d" % i))
            if i + 1 < L:
                handle = gather_start(lands[i + 1], "gather_start_%d" % (i + 1), wl[0])
                tok = handle[4][0, 0]
        sv = {}
        m3 = [mod[i, :, 3 * j:3 * j + 3] + tok for j in range(3)]
        gs = [norm_g_full[i, j].reshape(1, D) for j in range(3)]
        sv["x0"] = xs
        xs, sv["y0"], sv["h0"], sv["gu0"] = ffn_fwd(xs, m3[0], gs[0], wl[0], wl[2])
        sv["x1"] = xs
        if i == 0:
            wl[1], wl[3], wl[4], wl[5] = pair_forward(gather_wait(rest, xs, "gather_wait_0b"))
            handle = gather_start(lands[1], "gather_start_1", wl[1])
            m3 = [m + handle[4][0, 0] for m in m3]
        if i % 2 == 0:
            a = i // 2
            sv["u"] = conv_glu_fwd(xs, m3[1], gs[1], wl[4], cm_b_glu[a].reshape(1, -1))
            xs, sv["y1"], sv["u2"] = conv_out_fwd(
                xs, sv["u"], m3[1], pad_taps(w_dw_full[a]), cm_b_dw[a].reshape(1, D), cm_ln_g[a].reshape(1, D),
                cm_ln_b[a].reshape(1, D), wl[5].reshape(D, D), cm_b_pw[a].reshape(1, D))
        else:
            a = i // 2
            w_main, w_ab = dn_weights(i)
            sv["pre"], sv["z"], sv["ab"] = dn_proj_fwd(xs, m3[1], gs[1], w_main, w_ab)
            qkvgb = dn_conv_fwd(sv["pre"], sv["ab"], w_sconv_full[a], row128(dn_a_log[a]), row128(dn_dt_bias[a]), H)
            sv["qkvgb"] = qkvgb
            sv["o"], sv["sp"] = dn_chunk_fwd(*qkvgb)
            xs, sv["y1"] = dn_out_fwd(xs, sv["o"], sv["z"], m3[1], dn_o_g[a].reshape(1, Dh), wl[5].reshape(W, D))
        sv["x2"] = xs
        xs, sv["y2"], sv["h2"], sv["gu2"] = ffn_fwd(xs, m3[2], gs[2], wl[1], wl[3])
        saved.append(sv)
        after = xs

    dx, d_final_g, loss_part = final_loss(xs, final_g.reshape(1, D), loss_target)

    g_small = {n: None for n in names}
    d_norm_g = [[None] * 3 for _ in range(L)]
    dmod = [[None] * 3 for _ in range(L)]
    g_cm = {k: [None] * n_cm for k in ("b_glu", "w_dw", "b_dw", "ln_g", "ln_b", "b_pw")}
    g_dn = {k: [None] * n_dn for k in ("w_sconv", "a_log", "dt_bias", "o_g")}
    big = [None] * L

    def ffn_back(i, j, slot, dx, tok=0.0):
        wl, sv = wts[i], saved[i]
        m3 = mod[i, :, 3 * j:3 * j + 3] + tok
        g = norm_g_full[i, j].reshape(1, D)
        gu = sv["gu%d" % j]
        ab_, dgu, dyb, dh0, dgate = ffn_bwd_part(0, dx, gu, m3, wl[slot], wl[2 + slot], y=sv["y%d" % j])
        dx, ab_, dgu, dm, dg = ffn_bwd_part(1, dx, gu, m3, wl[slot], wl[2 + slot], first=(ab_, dgu, dyb, dh0),
                                            x=sv["x%d" % j], g=g)
        dm = dm.at[:, 2:3, :].set(dgate)
        hb = sv["h%d" % j]
        dmod[i][j] = dm
        d_norm_g[i][j] = jnp.sum(dg, axis=(0, 1))
        Fc = wl[slot].shape[2]
        dw_in = matmul_tn(hb.reshape(-1, D), dgu.reshape(2, BL * T, 2 * Fc), Fc, "dw_ffn_in")
        dw_out = matmul_tn(ab_.reshape(-1, 2 * Fc), dyb.reshape(1, -1, D), D, "dw_ffn_out")
        return dx, dw_in, dw_out.reshape(N_CHIPS, -1, D)

    pending, tok = None, 0.0
    for i in reversed(range(L)):
        wl, sv = wts[i], saved[i]
        a = i // 2
        dx, dw_in1, dw_out1 = ffn_back(i, 2, 1, dx, tok)
        m3 = mod[i, :, 3:6]
        g = norm_g_full[i, 1].reshape(1, D)
        if i % 2 == 0:
            w_pw = wl[5].reshape(D, D)
            wdw = pad_taps(w_dw_full[a])
            du2, u3b, dyb, dgate, vec = conv_out_bwd(dx, sv["y1"], sv["u2"], m3, cm_ln_g[a].reshape(1, D),
                                                     cm_ln_b[a].reshape(1, D), w_pw)
            dx, hb, dab, dwdw, dbglu, dm, dg = conv_glu_bwd(sv["x1"], dx, du2, sv["u"], m3, g, wl[4],
                                                            cm_b_glu[a].reshape(1, -1), wdw)
            dm = dm.at[:, 2:3, :].set(dgate)
            vec = jnp.sum(vec, axis=0)
            g_cm["b_pw"][a], g_cm["ln_g"][a], g_cm["ln_b"][a], g_cm["b_dw"][a] = vec[0], vec[1], vec[2], vec[3]
            g_cm["w_dw"][a] = jnp.sum(dwdw, axis=0)[:KC]
            g_cm["b_glu"][a] = jnp.sum(dbglu, axis=(0, 1))
            dw_a = matmul_tn(hb.reshape(-1, D), dab.reshape(1, -1, 2 * D), D // 2, "dw_glu")
            dw_b = matmul_tn(u3b.reshape(-1, D), dyb.reshape(1, -1, D), D, "dw_sq").reshape(N_CHIPS, -1, D)
        else:
            w_main, w_ab = dn_weights(i)
            w_out = wl[5].reshape(W, D)
            do, dz, ogb, dyb, dgate, dog = dn_out_bwd(dx, sv["y1"], sv["o"], sv["z"], m3, dn_o_g[a].reshape(1, Dh), w_out)
            dq, dk, dv, dgb, dbb = dn_chunk_bwd(*sv["qkvgb"], sv["sp"], do)
            dc, dab, small = dn_conv_bwd(dq, dk, dv, dgb, dbb, sv["pre"], sv["ab"], w_sconv_full[a],
                                         row128(dn_a_log[a]), row128(dn_dt_bias[a]))
            dx, hb, dproj, dws, dm, dg = dn_proj_bwd(sv["x1"], dx, dc, sv["pre"], dz, dab, m3, g, w_main, w_ab,
                                                     w_sconv_full[a])
            dm = dm.at[:, 2:3, :].set(dgate)
            small = jnp.sum(small, axis=0)
            g_dn["a_log"][a], g_dn["dt_bias"][a] = small[0, :H], small[1, :H]
            g_dn["o_g"][a] = jnp.sum(dog, axis=(0, 1))
            g_dn["w_sconv"][a] = jnp.sum(dws, axis=0)
            dw_main = matmul_tn(hb.reshape(-1, D), dproj.reshape(1, -1, 4 * W), W, "dw_dn_main")
            dw_ab = matmul_tn(hb.reshape(-1, D), dab.reshape(1, -1, LANES), LANES, "dw_dn_ab")
            full = jnp.concatenate([jnp.transpose(dw_main, (1, 0, 2)).reshape(D, 4 * W), dw_ab[0][:, :2 * H]], axis=1)
            dw_a = jnp.transpose(full.reshape(D, N_CHIPS, -1), (1, 0, 2))
            dw_b = matmul_tn(ogb.reshape(-1, W), dyb.reshape(1, -1, D), D, "dw_sq").reshape(N_CHIPS, -1, D)
        dmod[i][1] = dm
        d_norm_g[i][1] = jnp.sum(dg, axis=(0, 1))
        if i > 0:
            dx, dw_in0, dw_out0 = ffn_back(i, 0, 0, dx)
            started = reduce_start([dw_in0, dw_in1, dw_out0, dw_out1, dw_a, dw_b], c_idx, "reduce_start_%d" % i)
            if pending is not None:
                big[pending[1]] = reduce_finish(pending[0], dx, where, "reduce_wait_%d" % pending[1])
            pending, tok = (started, i), started[4][0, 0]
        else:
            part_a = reduce_start([dw_in1, dw_out1, dw_a, dw_b], c_idx, "reduce_start_0a")
            if pending is not None:
                big[pending[1]] = reduce_finish(pending[0], dx, where, "reduce_wait_%d" % pending[1])
            dx, dw_in0, dw_out0 = ffn_back(0, 0, 0, dx, part_a[4][0, 0])
            sums_a = reduce_finish(part_a, dx, where, "reduce_wait_0a")

    part = dict(
        norm_g=jnp.stack([jnp.stack(r) for r in d_norm_g]),
        cm_b_glu=jnp.stack(g_cm["b_glu"]), cm_w_dw=jnp.stack(g_cm["w_dw"]), cm_b_dw=jnp.stack(g_cm["b_dw"]),
        cm_ln_g=jnp.stack(g_cm["ln_g"]), cm_ln_b=jnp.stack(g_cm["ln_b"]), cm_b_pw=jnp.stack(g_cm["b_pw"]),
        dn_w_sconv=jnp.stack(g_dn["w_sconv"]), dn_a_log=jnp.stack(g_dn["a_log"]), dn_dt_bias=jnp.stack(g_dn["dt_bias"]),
        dn_o_g=jnp.stack(g_dn["o_g"]), final_g=jnp.sum(d_final_g, axis=(0, 1)),
        loss=jnp.sum(loss_part[:, 0, 0]).reshape(1))
    dmod_loc = jnp.stack([jnp.concatenate(r, axis=1) for r in dmod]).reshape(L, BL, C9)
    keys = list(part)
    packed = _pack([part[k] for k in keys] + [dmod_loc])
    R = packed.shape[0]
    gathered = allgather8(packed).reshape(N_DEV, R, LANES)
    summed = _unpack(sum_devices(gathered).reshape(-1), [part[k].shape for k in keys])
    tot = dict(zip(keys, summed))
    n_small = sum(int(part[k].size) for k in keys)
    dmod_all = gathered.reshape(N_DEV, -1)[:, n_small:n_small + L * BL * C9].reshape(N_DEV, L, BL, C9)
    dmod_all = jnp.transpose(dmod_all, (1, 0, 2, 3)).reshape(L, NB, C9)
    dmod_cols = lax.dynamic_slice_in_dim(dmod_all, chip * Ca, Ca, axis=2)
    g_w_ada, g_b_ada = ada_bwd(c_all, dmod_cols, dmod_all)
    delta, new_m, new_v = {}, {}, {}
    part_b = reduce_start([dw_in0, dw_out0], c_idx, "reduce_start_0b", g_w_ada)
    delta["w_ada"], new_m["w_ada"], new_v["w_ada"] = _adamw_any(w_ada, g_w_ada, m_w_ada, v_w_ada, "adamw_w_ada",
                                                                 part_b[4])
    sums_b = reduce_finish(part_b, new_v["w_ada"], where, "reduce_wait_0b")
    big[0] = [sums_b[0], sums_a[0], sums_b[1], sums_a[1], sums_a[2], sums_a[3]]

    def my_cols(full):
        n = full.shape[-1] // N_CHIPS
        return lax.dynamic_slice_in_dim(full, chip * n, n, axis=full.ndim - 1)

    grads = dict(
        norm_g=my_cols(tot["norm_g"]), w_ada=g_w_ada, b_ada=g_b_ada.reshape(L, C9),
        w_ffn_in=jnp.stack([jnp.stack([big[i][0], big[i][1]]) for i in range(L)]),
        w_ffn_out=jnp.stack([jnp.stack([big[i][2], big[i][3]]) for i in range(L)]),
        cm_w_glu=jnp.stack([big[i][4] for i in range(0, L, 2)]), cm_b_glu=tot["cm_b_glu"], cm_w_dw=my_cols(tot["cm_w_dw"]),
        cm_b_dw=tot["cm_b_dw"], cm_ln_g=tot["cm_ln_g"], cm_ln_b=tot["cm_ln_b"],
        cm_w_pw=jnp.stack([big[i][5] for i in range(0, L, 2)]), cm_b_pw=tot["cm_b_pw"],
        dn_w_in=jnp.stack([big[i][4] for i in range(1, L, 2)]), dn_w_sconv=my_cols(tot["dn_w_sconv"]),
        dn_a_log=tot["dn_a_log"], dn_dt_bias=tot["dn_dt_bias"], dn_o_g=tot["dn_o_g"],
        dn_w_out=jnp.stack([big[i][5] for i in range(1, L, 2)]), final_g=tot["final_g"])

    large = ("w_ada", "w_ffn_in", "w_ffn_out", "cm_w_glu", "cm_w_pw", "dn_w_in", "dn_w_out")
    for n in large[1:]:
        delta[n], new_m[n], new_v[n] = _adamw_any(weights[n], grads[n], mom_m[n], mom_v[n], "adamw_" + n)
    rest = [n for n in names if n not in large]
    shapes = [weights[n].shape for n in rest]
    pd, pm, pv = adamw(_pack([weights[n] for n in rest]), _pack([grads[n] for n in rest]),
                       _pack([mom_m[n] for n in rest]), _pack([mom_v[n] for n in rest]), "adamw_small")
    for n, d_, m_, v_ in zip(rest, _unpack(pd.reshape(-1), shapes), _unpack(pm.reshape(-1), shapes),
                             _unpack(pv.reshape(-1), shapes)):
        delta[n], new_m[n], new_v[n] = d_, m_, v_

    return (tot["loss"].reshape(()), dx, *[grads[n] for n in names], *[delta[n] for n in names],
            *[new_m[n] for n in names], *[new_v[n] for n in names])
```

```python
import functools

import jax
import jax.numpy as jnp
from jax import lax
from jax.experimental import pallas as pl
from jax.experimental.pallas import tpu as pltpu

F32 = jnp.float32
BF16 = jnp.bfloat16
EPS = 1e-6
CHUNK = 64
CHUNKS_PER_STEP = 4
N_CHIPS = 4
N_DEV = 8
LANES = 128
SUBLANES = 8
CONV_HALO = 32
SCONV_HALO = 8
VMEM_LIMIT_V7X = 60 * 1024 * 1024
HI = lax.Precision.HIGHEST
MESH = pl.DeviceIdType.MESH
HBM_SPEC = pl.BlockSpec(memory_space=pltpu.HBM)

ADAM_LR, ADAM_B1, ADAM_B2, ADAM_EPS, ADAM_WD, ADAM_STEP = 0.001, 0.9, 0.999, 1e-08, 0.01, 10


def _cparams(n_axes):
    return pltpu.CompilerParams(dimension_semantics=("arbitrary",) * n_axes, vmem_limit_bytes=VMEM_LIMIT_V7X)


def _tile(n, pref, mult=8):
    for t in range(min(n, pref) // mult * mult, 0, -mult):
        if n % t == 0:
            return t
    return n


def _mm(a, b):
    return lax.dot_general(a.astype(BF16), b.astype(BF16), (((1,), (0,)), ((), ())), preferred_element_type=F32)


def _mm_nt(a, b):
    return lax.dot_general(a.astype(BF16), b.astype(BF16), (((1,), (1,)), ((), ())), preferred_element_type=F32)


def _mm_tn(a, b):
    return lax.dot_general(a.astype(BF16), b.astype(BF16), (((0,), (0,)), ((), ())), preferred_element_type=F32)


def _sigmoid(x):
    return jax.nn.sigmoid(x)


def _dsilu(x, s):
    return s * (1.0 + x * (1.0 - s))


def _softplus(x):
    return jnp.maximum(x, 0.0) + jnp.log(1.0 + jnp.exp(-jnp.abs(x)))


def _modnorm(x, g, scale, shift):
    r = lax.rsqrt(jnp.mean(x * x, axis=-1, keepdims=True) + EPS)
    return (x * r) * g * (1.0 + scale) + shift


def _modnorm_bwd(x, g, scale, dh):
    r = lax.rsqrt(jnp.mean(x * x, axis=-1, keepdims=True) + EPS)
    xn = x * r
    dshift = jnp.sum(dh, axis=0, keepdims=True)
    dscale = jnp.sum(dh * (xn * g), axis=0, keepdims=True)
    dhn = dh * (1.0 + scale)
    dg = jnp.sum(dhn * xn, axis=0, keepdims=True)
    dxn = dhn * g
    dx = r * (dxn - xn * jnp.mean(dxn * xn, axis=-1, keepdims=True))
    return dx, dg, dscale, dshift


def _sum0(a):
    return jnp.sum(a, axis=0, keepdims=True)


def ffn_fwd(x, mod3, g, w_in, w_out):
    B, T, D = x.shape
    Fc = w_in.shape[2]
    w_in = w_in.reshape(2, 2, D, Fc)
    w_out = w_out.reshape(2, Fc, D)
    tm = _tile(T, 512)

    def half(h, wi_ref, wo_ref, gu_ref):
        gt = _mm(h, wi_ref[0])
        up = _mm(h, wi_ref[1])
        gu_ref[0] = gt.astype(BF16)
        gu_ref[1] = up.astype(BF16)
        return _mm(gt * _sigmoid(gt) * up, wo_ref[...])

    def body_a(x_ref, mod_ref, g_ref, wi_ref, wo_ref, h_ref, gu_ref, y0_ref):
        h = _modnorm(x_ref[...], g_ref[...], mod_ref[1:2, :], mod_ref[0:1, :]).astype(BF16)
        h_ref[...] = h
        y0_ref[...] = half(h, wi_ref, wo_ref, gu_ref)

    def body_b(x_ref, h_ref, y0_ref, mod_ref, wi_ref, wo_ref, gu_any, xo_ref, y_ref, gu_ref):
        y = y0_ref[...] + half(h_ref[...], wi_ref, wo_ref, gu_ref)
        y_ref[...] = y
        xo_ref[...] = x_ref[...] + 0.5 * (1.0 + mod_ref[2:3, :]) * y

    tok = pl.BlockSpec((None, tm, D), lambda b, t: (b, t, 0))
    per_b3 = pl.BlockSpec((None, 3, D), lambda b, t: (b, 0, 0))
    gu_shape = jax.ShapeDtypeStruct((2, B, T, 2 * Fc), BF16)

    def w_specs(part):
        return [pl.BlockSpec((2, None, D, Fc), lambda b, t: (0, part, 0, 0)),
                pl.BlockSpec((None, Fc, D), lambda b, t: (part, 0, 0))]

    def gu_spec(part):
        return pl.BlockSpec((2, None, tm, Fc), lambda b, t: (0, b, t, part))

    h, gu, y0 = pl.pallas_call(
        body_a, name="ffn_fwd_a", grid=(B, T // tm),
        in_specs=[tok, per_b3, pl.BlockSpec((1, D), lambda b, t: (0, 0))] + w_specs(0),
        out_specs=[tok, gu_spec(0), tok],
        out_shape=[jax.ShapeDtypeStruct((B, T, D), BF16), gu_shape, jax.ShapeDtypeStruct((B, T, D), F32)],
        compiler_params=_cparams(2),
    )(x, mod3, g, w_in, w_out)
    x_new, y, gu = pl.pallas_call(
        body_b, name="ffn_fwd_b", grid=(B, T // tm),
        in_specs=[tok, tok, tok, per_b3] + w_specs(1) + [pl.BlockSpec(memory_space=pl.ANY)],
        out_specs=[tok, tok, gu_spec(1)],
        out_shape=[jax.ShapeDtypeStruct((B, T, D), F32)] * 2 + [gu_shape],
        input_output_aliases={6: 2},
        compiler_params=_cparams(2),
    )(x, h, y0, mod3, w_in, w_out, gu)
    return x_new, y, h, gu


def ffn_bwd_part(part, dres, gu, mod3, w_in, w_out, first=None, y=None, x=None, g=None):
    B, T, D = dres.shape
    Fc = w_in.shape[2]
    F = 2 * Fc
    w_in = w_in.reshape(2, 2, D, Fc)
    w_out = w_out.reshape(2, Fc, D)
    tm = _tile(T, 256)

    def half(dy, gu_ref, wi_ref, wo_ref, a_ref, dgu_ref):
        gt = gu_ref[0].astype(F32)
        up = gu_ref[1].astype(F32)
        sg = _sigmoid(gt)
        silu = gt * sg
        a_ref[...] = (silu * up).astype(BF16)
        da = _mm_nt(dy, wo_ref[...])
        dup = (da * silu).astype(BF16)
        dgt = (da * up * _dsilu(gt, sg)).astype(BF16)
        dgu_ref[0] = dgt
        dgu_ref[1] = dup
        return _mm_nt(dgt, wi_ref[0]) + _mm_nt(dup, wi_ref[1])

    tok = pl.BlockSpec((None, tm, D), lambda b, t: (b, t, 0))
    per_b3 = pl.BlockSpec((None, 3, D), lambda b, t: (b, 0, 0))
    per_b1 = pl.BlockSpec((None, 1, D), lambda b, t: (b, 0, 0))
    gu_spec = pl.BlockSpec((2, None, tm, Fc), lambda b, t: (0, b, t, part))
    a_spec = pl.BlockSpec((None, tm, Fc), lambda b, t: (b, t, part))
    wi_spec = pl.BlockSpec((2, None, D, Fc), lambda b, t: (0, part, 0, 0))
    wo_spec = pl.BlockSpec((None, Fc, D), lambda b, t: (part, 0, 0))
    a_shape = jax.ShapeDtypeStruct((B, T, F), BF16)
    dgu_shape = jax.ShapeDtypeStruct((2, B, T, F), BF16)

    if part == 0:
        def body(dres_ref, y_ref, gu_ref, mod_ref, wi_ref, wo_ref, a_ref, dgu_ref, dy_ref, dh_ref, dgate_ref):
            @pl.when(pl.program_id(1) == 0)
            def _():
                dgate_ref[...] = jnp.zeros_like(dgate_ref)

            dres = dres_ref[...]
            dy = (0.5 * (1.0 + mod_ref[2:3, :]) * dres).astype(BF16)
            dy_ref[...] = dy
            dgate_ref[...] += _sum0(dres * (0.5 * y_ref[...]))
            dh_ref[...] = half(dy, gu_ref, wi_ref, wo_ref, a_ref, dgu_ref)

        return pl.pallas_call(
            body, name="ffn_bwd_a", grid=(B, T // tm),
            in_specs=[tok, tok, gu_spec, per_b3, wi_spec, wo_spec],
            out_specs=[a_spec, gu_spec, tok, tok, per_b1],
            out_shape=[a_shape, dgu_shape, jax.ShapeDtypeStruct((B, T, D), BF16), jax.ShapeDtypeStruct((B, T, D), F32),
                       jax.ShapeDtypeStruct((B, 1, D), F32)],
            compiler_params=_cparams(2),
        )(dres, y, gu, mod3, w_in, w_out)

    a_full, dgu_full, dy, dh0 = first

    def body(x_ref, dres_ref, dy_ref, dh0_ref, gu_ref, mod_ref, g_ref, wi_ref, wo_ref, a_any, dgu_any,
             dx_ref, a_ref, dgu_ref, dmod_ref, dg_ref):
        @pl.when(pl.program_id(1) == 0)
        def _():
            dmod_ref[...] = jnp.zeros_like(dmod_ref)
            dg_ref[...] = jnp.zeros_like(dg_ref)

        dh = dh0_ref[...] + half(dy_ref[...], gu_ref, wi_ref, wo_ref, a_ref, dgu_ref)
        dxn, dg, dscale, dshift = _modnorm_bwd(x_ref[...], g_ref[...], mod_ref[1:2, :], dh)
        dx_ref[...] = dres_ref[...] + dxn
        dmod_ref[0:1, :] += dshift
        dmod_ref[1:2, :] += dscale
        dg_ref[...] += dg

    return pl.pallas_call(
        body, name="ffn_bwd_b", grid=(B, T // tm),
        in_specs=[tok, tok, tok, tok, gu_spec, per_b3, pl.BlockSpec((1, D), lambda b, t: (0, 0)), wi_spec, wo_spec,
                  ANY_SPEC, ANY_SPEC],
        out_specs=[tok, a_spec, gu_spec, per_b3, per_b1],
        out_shape=[jax.ShapeDtypeStruct((B, T, D), F32), a_shape, dgu_shape, jax.ShapeDtypeStruct((B, 3, D), F32),
                   jax.ShapeDtypeStruct((B, 1, D), F32)],
        input_output_aliases={9: 1, 10: 2},
        compiler_params=_cparams(2),
    )(x, dres, dy, dh0, gu, mod3, g, w_in, w_out, a_full, dgu_full)


def matmul_tn(xm, ym, bm, name):
    N, K = xm.shape
    GY, _, MY = ym.shape
    per = MY // bm
    nb = GY * per
    tn = _tile(N, 512)

    def body(x_ref, y_ref, o_ref, acc_s):
        n = pl.program_id(1)

        @pl.when(n == 0)
        def _():
            acc_s[...] = jnp.zeros_like(acc_s)

        acc_s[...] += _mm_tn(x_ref[...], y_ref[...])

        @pl.when(n == N // tn - 1)
        def _():
            o_ref[...] = acc_s[...].astype(BF16)

    return pl.pallas_call(
        body, name=name, grid=(nb, N // tn),
        in_specs=[pl.BlockSpec((tn, K), lambda m, n: (n, 0)),
                  pl.BlockSpec((None, tn, bm), lambda m, n: (m // per, n, m % per))],
        out_specs=pl.BlockSpec((None, K, bm), lambda m, n: (m, 0, 0)),
        out_shape=jax.ShapeDtypeStruct((nb, K, bm), BF16),
        scratch_shapes=[pltpu.VMEM((K, bm), F32)],
        compiler_params=_cparams(2),
    )(xm, ym)


def final_loss(x, fg, target):
    B, T, D = x.shape
    tm = _tile(T, 512)

    def body(x_ref, g_ref, t_ref, dx_ref, dfg_ref, loss_ref):
        t = pl.program_id(1)

        @pl.when(t == 0)
        def _():
            dfg_ref[...] = jnp.zeros_like(dfg_ref)
            loss_ref[...] = jnp.zeros_like(loss_ref)

        xv = x_ref[...]
        g = g_ref[...]
        r = lax.rsqrt(jnp.mean(xv * xv, axis=-1, keepdims=True) + EPS)
        xn = xv * r
        err = xn * g - t_ref[...]
        tok_loss = jnp.mean(err * err, axis=-1, keepdims=True)
        loss_ref[...] += 0.5 * jnp.sum(tok_loss, axis=0, keepdims=True)
        dy = err * (1.0 / D)
        dfg_ref[...] += _sum0(dy * xn)
        dxn = dy * g
        dx_ref[...] = r * (dxn - xn * jnp.mean(dxn * xn, axis=-1, keepdims=True))

    tok = pl.BlockSpec((None, tm, D), lambda b, t: (b, t, 0))
    return pl.pallas_call(
        body, name="final_loss", grid=(B, T // tm),
        in_specs=[tok, pl.BlockSpec((1, D), lambda b, t: (0, 0)), tok],
        out_specs=[tok, pl.BlockSpec((None, 1, D), lambda b, t: (b, 0, 0)),
                   pl.BlockSpec((None, 1, LANES), lambda b, t: (b, 0, 0))],
        out_shape=[jax.ShapeDtypeStruct((B, T, D), F32), jax.ShapeDtypeStruct((B, 1, D), F32),
                   jax.ShapeDtypeStruct((B, 1, LANES), F32)],
        compiler_params=_cparams(2),
    )(x, fg, target)


def _past_halo_spec(tm, halo, width):
    return pl.BlockSpec((None, halo, width), lambda b, t: (b, jnp.maximum(t * (tm // halo) - 1, 0), 0))


def _future_halo_spec(tm, halo, width, T):
    return pl.BlockSpec((None, halo, width), lambda b, t: (b, jnp.minimum((t + 1) * (tm // halo), T // halo - 1), 0))


def _fill_shifted(ext_s):
    n = ext_s.shape[1]
    for b in range(1, SUBLANES):
        ext_s[b, 0:n - SUBLANES, :] = ext_s[0, pl.ds(b, n - SUBLANES), :]


def _shifted(ext_s, offset, rows):
    a, b = divmod(offset, SUBLANES)
    return ext_s[b, pl.ds(SUBLANES * a, rows), :]


def _glu_fwd(h, w_ref, bias):
    D = h.shape[1]
    a = jnp.concatenate([_mm(h, w_ref[0]), _mm(h, w_ref[1])], axis=1) + bias[:, :D]
    b = jnp.concatenate([_mm(h, w_ref[2]), _mm(h, w_ref[3])], axis=1) + bias[:, D:]
    return a, b


def conv_glu_fwd(x, mod3, g, w_glu, b_glu):
    B, T, D = x.shape
    tm = _tile(T, 512)

    def body(x_ref, mod_ref, g_ref, w_ref, b_ref, u_ref):
        h = _modnorm(x_ref[...], g_ref[...], mod_ref[1:2, :], mod_ref[0:1, :]).astype(BF16)
        a, b = _glu_fwd(h, w_ref, b_ref[...])
        u_ref[...] = a * _sigmoid(b)

    tok = pl.BlockSpec((None, tm, D), lambda b, t: (b, t, 0))
    return pl.pallas_call(
        body, name="conv_glu_fwd", grid=(B, T // tm),
        in_specs=[tok, pl.BlockSpec((None, 3, D), lambda b, t: (b, 0, 0)),
                  pl.BlockSpec((1, D), lambda b, t: (0, 0)),
                  pl.BlockSpec((4, D, D // 2), lambda b, t: (0, 0, 0)),
                  pl.BlockSpec((1, 2 * D), lambda b, t: (0, 0))],
        out_specs=tok, out_shape=jax.ShapeDtypeStruct((B, T, D), F32),
        compiler_params=_cparams(2),
    )(x, mod3, g, w_glu, b_glu)


def _layer_norm_parts(u2):
    mu = jnp.mean(u2, axis=-1, keepdims=True)
    xc = u2 - mu
    rs = lax.rsqrt(jnp.mean(xc * xc, axis=-1, keepdims=True) + EPS)
    return xc * rs, rs


def conv_out_fwd(x, u, mod3, w_dw, b_dw, ln_g, ln_b, w_pw, b_pw):
    B, T, D = x.shape
    K = w_dw.shape[0] - 1
    tm = _tile(T, 512)

    def body(x_ref, u_ref, halo_ref, mod_ref, wdw_ref, bdw_ref, lg_ref, lb_ref, wpw_ref, bpw_ref,
             xo_ref, y_ref, u2_ref, ext_s):
        t = pl.program_id(1)
        ext_s[0, 0:CONV_HALO, :] = jnp.where(t > 0, halo_ref[...], 0.0)
        ext_s[0, CONV_HALO:, :] = u_ref[...]
        _fill_shifted(ext_s)
        acc = jnp.broadcast_to(bdw_ref[...], (tm, D))
        for k in range(K):
            acc = acc + wdw_ref[k:k + 1, :] * _shifted(ext_s, CONV_HALO - (K - 1) + k, tm)
        u2_ref[...] = acc
        xh, _ = _layer_norm_parts(acc)
        l = xh * lg_ref[...] + lb_ref[...]
        u3 = l * _sigmoid(l)
        y = _mm(u3, wpw_ref[...]) + bpw_ref[...]
        y_ref[...] = y
        xo_ref[...] = x_ref[...] + (1.0 + mod_ref[2:3, :]) * y

    tok = pl.BlockSpec((None, tm, D), lambda b, t: (b, t, 0))
    vec = pl.BlockSpec((1, D), lambda b, t: (0, 0))
    return pl.pallas_call(
        body, name="conv_out_fwd", grid=(B, T // tm),
        in_specs=[tok, tok, _past_halo_spec(tm, CONV_HALO, D), pl.BlockSpec((None, 3, D), lambda b, t: (b, 0, 0)),
                  pl.BlockSpec((K + 1, D), lambda b, t: (0, 0)), vec, vec, vec,
                  pl.BlockSpec((D, D), lambda b, t: (0, 0)), vec],
        out_specs=[tok, tok, tok], out_shape=[jax.ShapeDtypeStruct((B, T, D), F32)] * 3,
        scratch_shapes=[pltpu.VMEM((SUBLANES, tm + CONV_HALO, D), F32)],
        compiler_params=_cparams(2),
    )(x, u, u, mod3, w_dw, b_dw, ln_g, ln_b, w_pw, b_pw)


def conv_out_bwd(dres, y, u2, mod3, ln_g, ln_b, w_pw):
    B, T, D = dres.shape
    tm = _tile(T, 512)

    def body(dres_ref, y_ref, u2_ref, mod_ref, lg_ref, lb_ref, wpw_ref, du2_ref, u3_ref, dy_ref, dgate_ref, vec_ref):
        t = pl.program_id(1)

        @pl.when(t == 0)
        def _():
            dgate_ref[...] = jnp.zeros_like(dgate_ref)
            vec_ref[...] = jnp.zeros_like(vec_ref)

        dres = dres_ref[...]
        dy = (1.0 + mod_ref[2:3, :]) * dres
        dy_ref[...] = dy.astype(BF16)
        dgate_ref[...] += _sum0(dres * y_ref[...])
        xh, rs = _layer_norm_parts(u2_ref[...])
        lg = lg_ref[...]
        l = xh * lg + lb_ref[...]
        sg = _sigmoid(l)
        u3_ref[...] = (l * sg).astype(BF16)
        du3 = _mm_nt(dy, wpw_ref[...])
        dl = du3 * _dsilu(l, sg)
        dxh = dl * lg
        du2 = rs * (dxh - jnp.mean(dxh, axis=-1, keepdims=True) - xh * jnp.mean(dxh * xh, axis=-1, keepdims=True))
        du2_ref[...] = du2
        vec_ref[0:1, :] += _sum0(dy)
        vec_ref[1:2, :] += _sum0(dl * xh)
        vec_ref[2:3, :] += _sum0(dl)
        vec_ref[3:4, :] += _sum0(du2)

    tok = pl.BlockSpec((None, tm, D), lambda b, t: (b, t, 0))
    tokb = pl.BlockSpec((None, tm, D), lambda b, t: (b, t, 0))
    vec = pl.BlockSpec((1, D), lambda b, t: (0, 0))
    return pl.pallas_call(
        body, name="conv_out_bwd", grid=(B, T // tm),
        in_specs=[tok, tok, tok, pl.BlockSpec((None, 3, D), lambda b, t: (b, 0, 0)), vec, vec,
                  pl.BlockSpec((D, D), lambda b, t: (0, 0))],
        out_specs=[tok, tokb, tokb, pl.BlockSpec((None, 1, D), lambda b, t: (b, 0, 0)),
                   pl.BlockSpec((None, 4, D), lambda b, t: (b, 0, 0))],
        out_shape=[jax.ShapeDtypeStruct((B, T, D), F32), jax.ShapeDtypeStruct((B, T, D), BF16),
                   jax.ShapeDtypeStruct((B, T, D), BF16), jax.ShapeDtypeStruct((B, 1, D), F32),
                   jax.ShapeDtypeStruct((B, 4, D), F32)],
        compiler_params=_cparams(2),
    )(dres, y, u2, mod3, ln_g, ln_b, w_pw)


def conv_glu_bwd(x, dres, du2, u, mod3, g, w_glu, b_glu, w_dw):
    B, T, D = x.shape
    K = w_dw.shape[0] - 1
    tm = _tile(T, 256)
    nt = T // tm

    def body(x_ref, dres_ref, du2_ref, du2h_ref, u_ref, uh_ref, mod_ref, g_ref, w_ref, b_ref, wdw_ref,
             dx_ref, h_ref, dab_ref, dwdw_ref, dbglu_ref, dmod_ref, dg_ref, extu_s, extd_s):
        t = pl.program_id(1)

        @pl.when(t == 0)
        def _():
            dwdw_ref[...] = jnp.zeros_like(dwdw_ref)
            dbglu_ref[...] = jnp.zeros_like(dbglu_ref)
            dmod_ref[...] = jnp.zeros_like(dmod_ref)
            dg_ref[...] = jnp.zeros_like(dg_ref)

        du2 = du2_ref[...]
        extu_s[0, 0:CONV_HALO, :] = jnp.where(t > 0, uh_ref[...], 0.0)
        extu_s[0, CONV_HALO:, :] = u_ref[...]
        extd_s[0, 0:tm, :] = du2
        extd_s[0, tm:, :] = jnp.where(t < nt - 1, du2h_ref[...], 0.0)
        _fill_shifted(extu_s)
        _fill_shifted(extd_s)
        du = jnp.zeros((tm, D), F32)
        for k in range(K):
            du = du + wdw_ref[k:k + 1, :] * _shifted(extd_s, K - 1 - k, tm)
            dwdw_ref[k:k + 1, :] += _sum0(du2 * _shifted(extu_s, CONV_HALO - (K - 1) + k, tm))
        xv = x_ref[...]
        h = _modnorm(xv, g_ref[...], mod_ref[1:2, :], mod_ref[0:1, :]).astype(BF16)
        h_ref[...] = h
        a, b = _glu_fwd(h, w_ref, b_ref[...])
        sb = _sigmoid(b)
        da = du * sb
        db = du * a * sb * (1.0 - sb)
        dbglu_ref[:, 0:D] += _sum0(da)
        dbglu_ref[:, D:] += _sum0(db)
        da = da.astype(BF16)
        db = db.astype(BF16)
        dab_ref[:, 0:D] = da
        dab_ref[:, D:] = db
        Dh2 = D // 2
        dh = (_mm_nt(da[:, :Dh2], w_ref[0]) + _mm_nt(da[:, Dh2:], w_ref[1])
              + _mm_nt(db[:, :Dh2], w_ref[2]) + _mm_nt(db[:, Dh2:], w_ref[3]))
        dxn, dg, dscale, dshift = _modnorm_bwd(xv, g_ref[...], mod_ref[1:2, :], dh)
        dx_ref[...] = dres_ref[...] + dxn
        dmod_ref[0:1, :] += dshift
        dmod_ref[1:2, :] += dscale
        dg_ref[...] += dg

    tok = pl.BlockSpec((None, tm, D), lambda b, t: (b, t, 0))
    return pl.pallas_call(
        body, name="conv_glu_bwd", grid=(B, nt),
        in_specs=[tok, tok, tok, _future_halo_spec(tm, CONV_HALO, D, T), tok, _past_halo_spec(tm, CONV_HALO, D),
                  pl.BlockSpec((None, 3, D), lambda b, t: (b, 0, 0)), pl.BlockSpec((1, D), lambda b, t: (0, 0)),
                  pl.BlockSpec((4, D, D // 2), lambda b, t: (0, 0, 0)), pl.BlockSpec((1, 2 * D), lambda b, t: (0, 0)),
                  pl.BlockSpec((K + 1, D), lambda b, t: (0, 0))],
        out_specs=[tok, tok, pl.BlockSpec((None, tm, 2 * D), lambda b, t: (b, t, 0)),
                   pl.BlockSpec((None, K + 1, D), lambda b, t: (b, 0, 0)),
                   pl.BlockSpec((None, 1, 2 * D), lambda b, t: (b, 0, 0)),
                   pl.BlockSpec((None, 3, D), lambda b, t: (b, 0, 0)),
                   pl.BlockSpec((None, 1, D), lambda b, t: (b, 0, 0))],
        out_shape=[jax.ShapeDtypeStruct((B, T, D), F32), jax.ShapeDtypeStruct((B, T, D), BF16),
                   jax.ShapeDtypeStruct((B, T, 2 * D), BF16), jax.ShapeDtypeStruct((B, K + 1, D), F32),
                   jax.ShapeDtypeStruct((B, 1, 2 * D), F32), jax.ShapeDtypeStruct((B, 3, D), F32),
                   jax.ShapeDtypeStruct((B, 1, D), F32)],
        scratch_shapes=[pltpu.VMEM((SUBLANES, tm + CONV_HALO, D), F32)] * 2,
        compiler_params=_cparams(2),
    )(x, dres, du2, du2, u, u, mod3, g, w_glu, b_glu, w_dw)


def dn_proj_fwd(x, mod3, g, w_main, w_ab):
    B, T, D = x.shape
    W = w_main.shape[1] // 4
    tm = _tile(T, 512)

    def body(x_ref, mod_ref, g_ref, wm_ref, wab_ref, pre_ref, z_ref, ab_ref):
        h = _modnorm(x_ref[...], g_ref[...], mod_ref[1:2, :], mod_ref[0:1, :]).astype(BF16)
        for p in range(3):
            pre_ref[:, p * W:(p + 1) * W] = _mm(h, wm_ref[:, p * W:(p + 1) * W])
        z_ref[...] = _mm(h, wm_ref[:, 3 * W:])
        ab_ref[...] = _mm(h, wab_ref[...])

    return pl.pallas_call(
        body, name="dn_proj_fwd", grid=(B, T // tm),
        in_specs=[pl.BlockSpec((None, tm, D), lambda b, t: (b, t, 0)), pl.BlockSpec((None, 3, D), lambda b, t: (b, 0, 0)),
                  pl.BlockSpec((1, D), lambda b, t: (0, 0)), pl.BlockSpec((D, 4 * W), lambda b, t: (0, 0)),
                  pl.BlockSpec((D, LANES), lambda b, t: (0, 0))],
        out_specs=[pl.BlockSpec((None, tm, 3 * W), lambda b, t: (b, t, 0)),
                   pl.BlockSpec((None, tm, W), lambda b, t: (b, t, 0)),
                   pl.BlockSpec((None, tm, LANES), lambda b, t: (b, t, 0))],
        out_shape=[jax.ShapeDtypeStruct((B, T, 3 * W), F32), jax.ShapeDtypeStruct((B, T, W), F32),
                   jax.ShapeDtypeStruct((B, T, LANES), F32)],
        compiler_params=_cparams(2),
    )(x, mod3, g, w_main, w_ab)


def _sconv(ext_s, w_ref, tm, K):
    acc = w_ref[0:1, :] * ext_s[pl.ds(SCONV_HALO - (K - 1), tm), :]
    for k in range(1, K):
        acc = acc + w_ref[k:k + 1, :] * ext_s[pl.ds(SCONV_HALO - (K - 1) + k, tm), :]
    return acc


def _lane_col(val, lane, idx):
    return jnp.sum(jnp.where(lane == idx, val, 0.0), axis=1, keepdims=True)


def dn_conv_fwd(pre, ab, w_sconv, alog_row, dt_row, H):
    B, T, W3 = pre.shape
    W = W3 // 3
    Dh = W // H
    K = w_sconv.shape[0]
    tm = _tile(T, 512)

    def body(pre_ref, halo_ref, ab_ref, w_ref, alog_ref, dt_ref, q_ref, k_ref, v_ref, gb_ref, bb_ref, ext_s):
        t = pl.program_id(1)
        ext_s[0:SCONV_HALO, :] = jnp.where(t > 0, halo_ref[...], 0.0)
        ext_s[SCONV_HALO:, :] = pre_ref[...]
        cv = _sconv(ext_s, w_ref, tm, K)
        qkv = cv * _sigmoid(cv)
        ab = ab_ref[...]
        lane = lax.broadcasted_iota(jnp.int32, ab.shape, 1)
        g_all = -jnp.exp(alog_ref[...]) * _softplus(ab + dt_ref[...])
        beta_all = _sigmoid(ab)
        for h in range(H):
            q_ref[h] = qkv[:, h * Dh:(h + 1) * Dh]
            k_ref[h] = qkv[:, W + h * Dh:W + (h + 1) * Dh]
            v_ref[h] = qkv[:, 2 * W + h * Dh:2 * W + (h + 1) * Dh]
            gb_ref[h] = jnp.broadcast_to(_lane_col(g_all, lane, h), (tm, Dh))
            bb_ref[h] = jnp.broadcast_to(_lane_col(beta_all, lane, H + h), (tm, Dh))

    hm = pl.BlockSpec((None, H, tm, Dh), lambda b, t: (b, 0, t, 0))
    row = pl.BlockSpec((1, LANES), lambda b, t: (0, 0))
    return pl.pallas_call(
        body, name="dn_conv_fwd", grid=(B, T // tm),
        in_specs=[pl.BlockSpec((None, tm, W3), lambda b, t: (b, t, 0)), _past_halo_spec(tm, SCONV_HALO, W3),
                  pl.BlockSpec((None, tm, LANES), lambda b, t: (b, t, 0)),
                  pl.BlockSpec((K, W3), lambda b, t: (0, 0)), row, row],
        out_specs=[hm] * 5, out_shape=[jax.ShapeDtypeStruct((B, H, T, Dh), F32)] * 5,
        scratch_shapes=[pltpu.VMEM((tm + SCONV_HALO, W3), F32)],
        compiler_params=_cparams(2),
    )(pre, pre, ab, w_sconv, alog_row, dt_row)


def _bdot(spec):
    return lambda a, b: jnp.einsum(spec, a.astype(BF16), b.astype(BF16), preferred_element_type=F32)


_NN, _NT, _TN = "gij,gjk->gik", "gik,gjk->gij", "gki,gkj->gij"


def _make_bdots():
    nn_, nt_, tn_ = _bdot(_NN), _bdot(_NT), _bdot(_TN)

    @jax.custom_vjp
    def nn(a, b):
        return nn_(a, b)

    @jax.custom_vjp
    def nt(a, b):
        return nt_(a, b)

    @jax.custom_vjp
    def tn(a, b):
        return tn_(a, b)

    nn.defvjp(lambda a, b: (nn_(a, b), (a, b)), lambda r, d: (nt_(d, r[1]), tn_(r[0], d)))
    nt.defvjp(lambda a, b: (nt_(a, b), (a, b)), lambda r, d: (nn_(d, r[1]), tn_(d, r[0])))
    tn.defvjp(lambda a, b: (tn_(a, b), (a, b)), lambda r, d: (nt_(r[1], d), nn_(r[0], d)))
    return nn, nt, tn


def _unit_lower_inverse(A, known=None):
    hdot = functools.partial(jnp.einsum, precision=lax.Precision.HIGH, preferred_element_type=F32)
    C = A.shape[-1]

    def impl(A):
        eye = (lax.broadcasted_iota(jnp.int32, A.shape, 1) == lax.broadcasted_iota(jnp.int32, A.shape, 2)).astype(F32)
        Tm = eye - A
        Ap = A
        for _ in range(max(1, (C - 1).bit_length()) - 1):
            Ap = hdot(_NN, Ap, Ap)
            Tm = Tm + hdot(_NN, Tm, Ap)
        return Tm

    @jax.custom_vjp
    def inv(A, given):
        return impl(A) if known is None else given

    def fwd(A, given):
        Tm = impl(A) if known is None else given
        return Tm, Tm

    def bwd(Tm, dT):
        return -hdot(_NT, hdot(_TN, Tm, dT), Tm), jnp.zeros_like(Tm)

    inv.defvjp(fwd, bwd)
    return inv(A, A if known is None else known)


def _chunk_fn(q, k, v, gb, bb, S, inverse=None, with_inverse=False):
    nn, nt, tn = _make_bdots()
    G, C, Dh = q.shape
    hdot = functools.partial(jnp.einsum, precision=lax.Precision.HIGH, preferred_element_type=F32)
    q = q * lax.rsqrt(jnp.sum(q * q, axis=-1, keepdims=True) + EPS) * (Dh ** -0.5)
    k = k * lax.rsqrt(jnp.sum(k * k, axis=-1, keepdims=True) + EPS)
    row = lax.broadcasted_iota(jnp.int32, (G, C, C), 1)
    col = lax.broadcasted_iota(jnp.int32, (G, C, C), 2)
    causal = row >= col
    strict = row > col
    gc = hdot(_NN, causal.astype(F32), gb)
    spread = jnp.full((G, C, Dh), 1.0 / Dh, F32)
    gi = hdot(_NT, gc, spread)
    gj = hdot(_NT, spread, gc)
    decay = jnp.where(causal, jnp.exp(jnp.where(causal, gi - gj, 0.0)), 0.0)
    kb = k * bb
    vb = v * bb
    A = jnp.where(strict, nt(kb, k) * decay, 0.0)
    Tm = _unit_lower_inverse(A, inverse)
    eg = jnp.exp(gc)
    u = nn(Tm, vb)
    w = nn(Tm, kb * eg)
    qg = q * eg
    intra = nt(q, k) * decay
    glast = hdot(_NN, jnp.ones((G, C, C), F32), gb)
    kd = k * jnp.exp(glast - gc)
    v_new = u - nn(w, S)
    o = nn(qg, S) + nn(intra, v_new)
    egl = jnp.exp(glast)
    S_new = S * jnp.concatenate([egl] * (Dh // C), axis=1) + tn(kd, v_new)
    return (o, S_new, Tm) if with_inverse else (o, S_new)


def dn_chunk_fwd(q, k, v, gb, bb):
    B, H, T, Dh = q.shape
    NC = T // CHUNK
    NS = _tile(NC, CHUNKS_PER_STEP, 1)

    def body(q_ref, k_ref, v_ref, gb_ref, bb_ref, o_ref, sp_ref, inv_ref, S_s):
        @pl.when(pl.program_id(1) == 0)
        def _():
            S_s[...] = jnp.zeros_like(S_s)

        def one_chunk(j, carry):
            rows = pl.ds(pl.multiple_of(j * CHUNK, CHUNK), CHUNK)
            S = S_s[...]
            sp_ref[j] = S
            o, S_new, Tm = _chunk_fn(q_ref[:, rows, :], k_ref[:, rows, :], v_ref[:, rows, :], gb_ref[:, rows, :],
                                     bb_ref[:, rows, :], S, with_inverse=True)
            o_ref[:, rows, :] = o
            inv_ref[j] = Tm
            S_s[...] = S_new
            return carry

        lax.fori_loop(0, NS, one_chunk, 0)

    hm = pl.BlockSpec((None, H, NS * CHUNK, Dh), lambda b, n: (b, 0, n, 0))
    return pl.pallas_call(
        body, name="dn_chunk_fwd", grid=(B, NC // NS),
        in_specs=[hm] * 5,
        out_specs=[hm, pl.BlockSpec((None, NS, H, Dh, Dh), lambda b, n: (b, n, 0, 0, 0)),
                   pl.BlockSpec((None, NS, H, CHUNK, CHUNK), lambda b, n: (b, n, 0, 0, 0))],
        out_shape=[jax.ShapeDtypeStruct((B, H, T, Dh), F32), jax.ShapeDtypeStruct((B, NC, H, Dh, Dh), F32),
                   jax.ShapeDtypeStruct((B, NC, H, CHUNK, CHUNK), F32)],
        scratch_shapes=[pltpu.VMEM((H, Dh, Dh), F32)],
        compiler_params=_cparams(2),
    )(q, k, v, gb, bb)


def dn_chunk_bwd(q, k, v, gb, bb, s_prev, inv, do):
    B, H, T, Dh = q.shape
    NC = T // CHUNK
    NS = _tile(NC, CHUNKS_PER_STEP, 1)
    NG = NC // NS

    def body(q_ref, k_ref, v_ref, gb_ref, bb_ref, sp_ref, inv_ref, do_ref, dq_ref, dk_ref, dv_ref, dgb_ref, dbb_ref,
             dS_s):
        @pl.when(pl.program_id(1) == 0)
        def _():
            dS_s[...] = jnp.zeros_like(dS_s)

        def one_chunk(jj, carry):
            j = NS - 1 - jj
            rows = pl.ds(pl.multiple_of(j * CHUNK, CHUNK), CHUNK)
            _, vjp = jax.vjp(functools.partial(_chunk_fn, inverse=inv_ref[j]), q_ref[:, rows, :], k_ref[:, rows, :],
                             v_ref[:, rows, :], gb_ref[:, rows, :], bb_ref[:, rows, :], sp_ref[j])
            dq, dk, dv, dgb, dbb, dS = vjp((do_ref[:, rows, :], dS_s[...]))
            dq_ref[:, rows, :] = dq
            dk_ref[:, rows, :] = dk
            dv_ref[:, rows, :] = dv
            dgb_ref[:, rows, :] = dgb
            dbb_ref[:, rows, :] = dbb
            dS_s[...] = dS
            return carry

        lax.fori_loop(0, NS, one_chunk, 0)

    hm = pl.BlockSpec((None, H, NS * CHUNK, Dh), lambda b, n: (b, 0, NG - 1 - n, 0))
    return pl.pallas_call(
        body, name="dn_chunk_bwd", grid=(B, NG),
        in_specs=[hm] * 5 + [pl.BlockSpec((None, NS, H, Dh, Dh), lambda b, n: (b, NG - 1 - n, 0, 0, 0)),
                             pl.BlockSpec((None, NS, H, CHUNK, CHUNK), lambda b, n: (b, NG - 1 - n, 0, 0, 0)), hm],
        out_specs=[hm] * 5, out_shape=[jax.ShapeDtypeStruct((B, H, T, Dh), F32)] * 5,
        scratch_shapes=[pltpu.VMEM((H, Dh, Dh), F32)],
        compiler_params=_cparams(2),
    )(q, k, v, gb, bb, s_prev, inv, do)


def _head_norm(o, og):
    r = lax.rsqrt(jnp.mean(o * o, axis=-1, keepdims=True) + EPS)
    return o * r, r


def dn_out_fwd(x, o, z, mod3, o_g, w_out):
    B, T, D = x.shape
    _, H, _, Dh = o.shape
    W = H * Dh
    tm = _tile(T, 512)

    def body(x_ref, o_ref, z_ref, mod_ref, og_ref, w_ref, xo_ref, y_ref):
        parts = []
        for h in range(H):
            on, _ = _head_norm(o_ref[h], og_ref[...])
            zz = z_ref[:, h * Dh:(h + 1) * Dh]
            parts.append((on * og_ref[...] * (zz * _sigmoid(zz))).astype(BF16))
        y = _mm(jnp.concatenate(parts, axis=1), w_ref[...])
        y_ref[...] = y
        xo_ref[...] = x_ref[...] + (1.0 + mod_ref[2:3, :]) * y

    tok = pl.BlockSpec((None, tm, D), lambda b, t: (b, t, 0))
    return pl.pallas_call(
        body, name="dn_out_fwd", grid=(B, T // tm),
        in_specs=[tok, pl.BlockSpec((None, H, tm, Dh), lambda b, t: (b, 0, t, 0)),
                  pl.BlockSpec((None, tm, W), lambda b, t: (b, t, 0)), pl.BlockSpec((None, 3, D), lambda b, t: (b, 0, 0)),
                  pl.BlockSpec((1, Dh), lambda b, t: (0, 0)), pl.BlockSpec((W, D), lambda b, t: (0, 0))],
        out_specs=[tok, tok], out_shape=[jax.ShapeDtypeStruct((B, T, D), F32)] * 2,
        compiler_params=_cparams(2),
    )(x, o, z, mod3, o_g, w_out)


def dn_out_bwd(dres, y, o, z, mod3, o_g, w_out):
    B, T, D = dres.shape
    _, H, _, Dh = o.shape
    W = H * Dh
    tm = _tile(T, 512)

    def body(dres_ref, y_ref, o_ref, z_ref, mod_ref, og_ref, w_ref, do_ref, dz_ref, ogb_ref, dy_ref, dgate_ref, dog_ref):
        t = pl.program_id(1)

        @pl.when(t == 0)
        def _():
            dgate_ref[...] = jnp.zeros_like(dgate_ref)
            dog_ref[...] = jnp.zeros_like(dog_ref)

        dres = dres_ref[...]
        dy = ((1.0 + mod_ref[2:3, :]) * dres).astype(BF16)
        dy_ref[...] = dy
        dgate_ref[...] += _sum0(dres * y_ref[...])
        dog = _mm_nt(dy, w_ref[...])
        og = og_ref[...]
        for h in range(H):
            ov = o_ref[h]
            xn, r = _head_norm(ov, og)
            zz = z_ref[:, h * Dh:(h + 1) * Dh]
            sg = _sigmoid(zz)
            sz = zz * sg
            d = dog[:, h * Dh:(h + 1) * Dh]
            ogb_ref[:, h * Dh:(h + 1) * Dh] = (xn * og * sz).astype(BF16)
            dz_ref[:, h * Dh:(h + 1) * Dh] = d * (xn * og) * _dsilu(zz, sg)
            don = d * sz
            dog_ref[...] += _sum0(don * xn)
            dxn = don * og
            do_ref[h] = r * (dxn - xn * jnp.mean(dxn * xn, axis=-1, keepdims=True))

    tok = pl.BlockSpec((None, tm, D), lambda b, t: (b, t, 0))
    tokw = pl.BlockSpec((None, tm, W), lambda b, t: (b, t, 0))
    hm = pl.BlockSpec((None, H, tm, Dh), lambda b, t: (b, 0, t, 0))
    return pl.pallas_call(
        body, name="dn_out_bwd", grid=(B, T // tm),
        in_specs=[tok, tok, hm, tokw, pl.BlockSpec((None, 3, D), lambda b, t: (b, 0, 0)),
                  pl.BlockSpec((1, Dh), lambda b, t: (0, 0)), pl.BlockSpec((W, D), lambda b, t: (0, 0))],
        out_specs=[hm, tokw, tokw, tok, pl.BlockSpec((None, 1, D), lambda b, t: (b, 0, 0)),
                   pl.BlockSpec((None, 1, Dh), lambda b, t: (b, 0, 0))],
        out_shape=[jax.ShapeDtypeStruct((B, H, T, Dh), F32), jax.ShapeDtypeStruct((B, T, W), F32),
                   jax.ShapeDtypeStruct((B, T, W), BF16), jax.ShapeDtypeStruct((B, T, D), BF16),
                   jax.ShapeDtypeStruct((B, 1, D), F32), jax.ShapeDtypeStruct((B, 1, Dh), F32)],
        compiler_params=_cparams(2),
    )(dres, y, o, z, mod3, o_g, w_out)


def dn_conv_bwd(dq, dk, dv, dgb, dbb, pre, ab, w_sconv, alog_row, dt_row):
    B, H, T, Dh = dq.shape
    W = H * Dh
    W3 = 3 * W
    K = w_sconv.shape[0]
    tm = _tile(T, 256)

    def body(dq_ref, dk_ref, dv_ref, dgb_ref, dbb_ref, pre_ref, halo_ref, ab_ref, w_ref, alog_ref, dt_ref,
             dc_ref, dab_ref, small_ref, ext_s):
        t = pl.program_id(1)

        @pl.when(t == 0)
        def _():
            small_ref[...] = jnp.zeros_like(small_ref)

        ext_s[0:SCONV_HALO, :] = jnp.where(t > 0, halo_ref[...], 0.0)
        ext_s[SCONV_HALO:, :] = pre_ref[...]
        cv = _sconv(ext_s, w_ref, tm, K)
        dsl = _dsilu(cv, _sigmoid(cv))
        ab = ab_ref[...]
        lane = lax.broadcasted_iota(jnp.int32, ab.shape, 1)
        dg_all = jnp.zeros_like(ab)
        db_all = jnp.zeros_like(ab)
        for h in range(H):
            dc_ref[:, h * Dh:(h + 1) * Dh] = dq_ref[h] * dsl[:, h * Dh:(h + 1) * Dh]
            dc_ref[:, W + h * Dh:W + (h + 1) * Dh] = dk_ref[h] * dsl[:, W + h * Dh:W + (h + 1) * Dh]
            dc_ref[:, 2 * W + h * Dh:2 * W + (h + 1) * Dh] = dv_ref[h] * dsl[:, 2 * W + h * Dh:2 * W + (h + 1) * Dh]
            dg_all = dg_all + jnp.where(lane == h, jnp.sum(dgb_ref[h], axis=1, keepdims=True), 0.0)
            db_all = db_all + jnp.where(lane == H + h, jnp.sum(dbb_ref[h], axis=1, keepdims=True), 0.0)
        xa = ab + dt_ref[...]
        ea = -jnp.exp(alog_ref[...])
        g_all = ea * _softplus(xa)
        da = dg_all * ea * _sigmoid(xa)
        beta = _sigmoid(ab)
        dab_ref[...] = da + db_all * beta * (1.0 - beta)
        small_ref[0:1, :] += _sum0(dg_all * g_all)
        small_ref[1:2, :] += _sum0(da)

    hm = pl.BlockSpec((None, H, tm, Dh), lambda b, t: (b, 0, t, 0))
    row = pl.BlockSpec((1, LANES), lambda b, t: (0, 0))
    return pl.pallas_call(
        body, name="dn_conv_bwd", grid=(B, T // tm),
        in_specs=[hm] * 5 + [pl.BlockSpec((None, tm, W3), lambda b, t: (b, t, 0)), _past_halo_spec(tm, SCONV_HALO, W3),
                             pl.BlockSpec((None, tm, LANES), lambda b, t: (b, t, 0)),
                             pl.BlockSpec((K, W3), lambda b, t: (0, 0)), row, row],
        out_specs=[pl.BlockSpec((None, tm, W3), lambda b, t: (b, t, 0)), pl.BlockSpec((None, tm, LANES), lambda b, t: (b, t, 0)),
                   pl.BlockSpec((None, 2, LANES), lambda b, t: (b, 0, 0))],
        out_shape=[jax.ShapeDtypeStruct((B, T, W3), F32), jax.ShapeDtypeStruct((B, T, LANES), F32),
                   jax.ShapeDtypeStruct((B, 2, LANES), F32)],
        scratch_shapes=[pltpu.VMEM((tm + SCONV_HALO, W3), F32)],
        compiler_params=_cparams(2),
    )(dq, dk, dv, dgb, dbb, pre, pre, ab, w_sconv, alog_row, dt_row)


def dn_proj_bwd(x, dres, dc, pre, dz, dab, mod3, g, w_main, w_ab, w_sconv):
    B, T, D = x.shape
    W3 = dc.shape[2]
    W = W3 // 3
    K = w_sconv.shape[0]
    tm = _tile(T, 256)
    nt = T // tm

    def body(x_ref, dres_ref, dc_ref, dch_ref, pre_ref, preh_ref, dz_ref, dab_ref, mod_ref, g_ref, wm_ref, wab_ref, ws_ref,
             dx_ref, h_ref, dproj_ref, dws_ref, dmod_ref, dg_ref, extp_s, extd_s):
        t = pl.program_id(1)

        @pl.when(t == 0)
        def _():
            dws_ref[...] = jnp.zeros_like(dws_ref)
            dmod_ref[...] = jnp.zeros_like(dmod_ref)
            dg_ref[...] = jnp.zeros_like(dg_ref)

        dc = dc_ref[...]
        extp_s[0:SCONV_HALO, :] = jnp.where(t > 0, preh_ref[...], 0.0)
        extp_s[SCONV_HALO:, :] = pre_ref[...]
        extd_s[0:tm, :] = dc
        extd_s[tm:, :] = jnp.where(t < nt - 1, dch_ref[...], 0.0)
        dpre = jnp.zeros((tm, W3), F32)
        for k in range(K):
            dpre = dpre + ws_ref[k:k + 1, :] * extd_s[pl.ds(K - 1 - k, tm), :]
            dws_ref[k:k + 1, :] += _sum0(dc * extp_s[pl.ds(SCONV_HALO - (K - 1) + k, tm), :])
        dpre = dpre.astype(BF16)
        dzb = dz_ref[...].astype(BF16)
        dproj_ref[:, 0:W3] = dpre
        dproj_ref[:, W3:] = dzb
        dh = _mm_nt(dab_ref[...], wab_ref[...]) + _mm_nt(dzb, wm_ref[:, W3:])
        for p in range(3):
            dh = dh + _mm_nt(dpre[:, p * W:(p + 1) * W], wm_ref[:, p * W:(p + 1) * W])
        xv = x_ref[...]
        h_ref[...] = _modnorm(xv, g_ref[...], mod_ref[1:2, :], mod_ref[0:1, :]).astype(BF16)
        dxn, dg, dscale, dshift = _modnorm_bwd(xv, g_ref[...], mod_ref[1:2, :], dh)
        dx_ref[...] = dres_ref[...] + dxn
        dmod_ref[0:1, :] += dshift
        dmod_ref[1:2, :] += dscale
        dg_ref[...] += dg

    tok = pl.BlockSpec((None, tm, D), lambda b, t: (b, t, 0))
    tok3 = pl.BlockSpec((None, tm, W3), lambda b, t: (b, t, 0))
    return pl.pallas_call(
        body, name="dn_proj_bwd", grid=(B, nt),
        in_specs=[tok, tok, tok3, _future_halo_spec(tm, SCONV_HALO, W3, T), tok3, _past_halo_spec(tm, SCONV_HALO, W3),
                  pl.BlockSpec((None, tm, W), lambda b, t: (b, t, 0)), pl.BlockSpec((None, tm, LANES), lambda b, t: (b, t, 0)),
                  pl.BlockSpec((None, 3, D), lambda b, t: (b, 0, 0)), pl.BlockSpec((1, D), lambda b, t: (0, 0)),
                  pl.BlockSpec((D, 4 * W), lambda b, t: (0, 0)), pl.BlockSpec((D, LANES), lambda b, t: (0, 0)),
                  pl.BlockSpec((K, W3), lambda b, t: (0, 0))],
        out_specs=[tok, tok, pl.BlockSpec((None, tm, 4 * W), lambda b, t: (b, t, 0)),
                   pl.BlockSpec((None, K, W3), lambda b, t: (b, 0, 0)), pl.BlockSpec((None, 3, D), lambda b, t: (b, 0, 0)),
                   pl.BlockSpec((None, 1, D), lambda b, t: (b, 0, 0))],
        out_shape=[jax.ShapeDtypeStruct((B, T, D), F32), jax.ShapeDtypeStruct((B, T, D), BF16),
                   jax.ShapeDtypeStruct((B, T, 4 * W), BF16), jax.ShapeDtypeStruct((B, K, W3), F32),
                   jax.ShapeDtypeStruct((B, 3, D), F32), jax.ShapeDtypeStruct((B, 1, D), F32)],
        scratch_shapes=[pltpu.VMEM((tm + SCONV_HALO, W3), F32), pltpu.VMEM((tm + SCONV_HALO, W3), F32)],
        compiler_params=_cparams(2),
    )(x, dres, dc, dc, pre, pre, dz, dab, mod3, g, w_main, w_ab, w_sconv)


def ada_fwd(c_all, w_ada, b_cols):
    L, D, Ca = w_ada.shape
    NB = c_all.shape[0]

    def body(c_ref, w_ref, b_ref, o_ref):
        cv = c_ref[...]
        o_ref[...] = _mm(cv * _sigmoid(cv), w_ref[...]) + b_ref[...]

    return pl.pallas_call(
        body, name="ada_fwd", grid=(L,),
        in_specs=[pl.BlockSpec((NB, D), lambda i: (0, 0)), pl.BlockSpec((None, D, Ca), lambda i: (i, 0, 0)),
                  pl.BlockSpec((None, 1, Ca), lambda i: (i, 0, 0))],
        out_specs=pl.BlockSpec((None, NB, Ca), lambda i: (i, 0, 0)),
        out_shape=jax.ShapeDtypeStruct((L, NB, Ca), F32),
        compiler_params=_cparams(1),
    )(c_all, w_ada, b_cols)


def ada_bwd(c_all, dmod_cols, dmod_all):
    L, NB, Ca = dmod_cols.shape
    D = c_all.shape[1]
    C9 = dmod_all.shape[2]

    def body(c_ref, dc_ref, da_ref, gw_ref, gb_ref):
        cv = c_ref[...]
        gw_ref[...] = _mm_tn(cv * _sigmoid(cv), dc_ref[...])
        gb_ref[...] = _sum0(da_ref[...])

    return pl.pallas_call(
        body, name="ada_bwd", grid=(L,),
        in_specs=[pl.BlockSpec((NB, D), lambda i: (0, 0)), pl.BlockSpec((None, NB, Ca), lambda i: (i, 0, 0)),
                  pl.BlockSpec((None, NB, C9), lambda i: (i, 0, 0))],
        out_specs=[pl.BlockSpec((None, D, Ca), lambda i: (i, 0, 0)), pl.BlockSpec((None, 1, C9), lambda i: (i, 0, 0))],
        out_shape=[jax.ShapeDtypeStruct((L, D, Ca), F32), jax.ShapeDtypeStruct((L, 1, C9), F32)],
        compiler_params=_cparams(1),
    )(c_all, dmod_cols, dmod_all)


def adamw(w, g, m, v, name, token=None):
    R, C = w.shape
    tr = _tile(R, max(8, (1 << 18) // C))
    if token is None:
        token = jnp.zeros((8, LANES), F32)

    def body(w_ref, g_ref, m_ref, v_ref, t_ref, d_ref, mo_ref, vo_ref):
        gv = g_ref[...] + t_ref[0:1, 0:1]
        mn = ADAM_B1 * m_ref[...] + (1.0 - ADAM_B1) * gv
        vn = ADAM_B2 * v_ref[...] + (1.0 - ADAM_B2) * (gv * gv)
        m_hat = mn / (1.0 - ADAM_B1 ** ADAM_STEP)
        v_hat = vn / (1.0 - ADAM_B2 ** ADAM_STEP)
        d_ref[...] = -ADAM_LR * (m_hat / (jnp.sqrt(v_hat) + ADAM_EPS) + ADAM_WD * w_ref[...])
        mo_ref[...] = mn
        vo_ref[...] = vn

    blk = pl.BlockSpec((tr, C), lambda i: (i, 0))
    return pl.pallas_call(
        body, name=name, grid=(R // tr,), in_specs=[blk] * 4 + [pl.BlockSpec((8, LANES), lambda i: (0, 0))],
        out_specs=[blk] * 3, out_shape=[jax.ShapeDtypeStruct((R, C), F32)] * 3, compiler_params=_cparams(1),
    )(w, g, m, v, token)


def sum_devices(a):
    n, R, C = a.shape

    def body(a_ref, o_ref):
        s = a_ref[0]
        for d in range(1, n):
            s = s + a_ref[d]
        o_ref[...] = s

    return pl.pallas_call(
        body, name="sum_devices", out_shape=jax.ShapeDtypeStruct((R, C), F32),
        compiler_params=pltpu.CompilerParams(vmem_limit_bytes=VMEM_LIMIT_V7X),
    )(a)


def _place():
    x, y, c = lax.axis_index("x"), lax.axis_index("y"), lax.axis_index("c")
    return x, y, c


def _other_chips(x, y):
    return [(2 * (1 - x) + y, 1 - x, y), (2 * x + (1 - y), x, 1 - y), (2 * (1 - x) + (1 - y), 1 - x, 1 - y)]


def allgather8(block):
    m_per, n = block.shape

    def body(x_ref, out_ref, send_sems, recv_sems, local_sem):
        x, y, c = _place()
        me, sibling = (x, y, c), (x, y, 1 - c)
        chips = [(1 - x, y), (x, 1 - y), (1 - x, 1 - y)]

        def rows(px, py, pc):
            return out_ref.at[pl.ds((4 * px + 2 * py + pc) * m_per, m_per), :]

        def copy(k, blk, to, src=None):
            return pltpu.make_async_remote_copy(
                src_ref=rows(*blk) if src is None else src, dst_ref=rows(*blk),
                send_sem=send_sems.at[k], recv_sem=recv_sems.at[k], device_id=to, device_id_type=MESH)

        mine = pltpu.make_async_copy(x_ref, rows(*me), local_sem)
        mine.start()
        first = [copy(0, me, sibling, src=x_ref)]
        first += [copy(1 + j, me, (*chip, c), src=x_ref) for j, chip in enumerate(chips)]
        for cp in first:
            cp.start()
        passed = [copy(4 + j, (*chip, c), sibling) for j, chip in enumerate(chips)]
        for j, chip in enumerate(chips):
            copy(1 + j, (*chip, c), me).wait_recv()
            passed[j].start()
        copy(0, sibling, me).wait_recv()
        for j, chip in enumerate(chips):
            copy(4 + j, (*chip, 1 - c), me).wait_recv()
        for cp in first + passed:
            cp.wait_send()
        mine.wait()

    return pl.pallas_call(
        body, name="allgather8", out_shape=jax.ShapeDtypeStruct((N_DEV * m_per, n), block.dtype),
        in_specs=[pl.BlockSpec(memory_space=pltpu.VMEM)], out_specs=pl.BlockSpec(memory_space=pltpu.VMEM),
        scratch_shapes=[pltpu.SemaphoreType.DMA((7,)), pltpu.SemaphoreType.DMA((7,)), pltpu.SemaphoreType.DMA],
        compiler_params=pltpu.CompilerParams(vmem_limit_bytes=VMEM_LIMIT_V7X),
    )(block)


def _half(ref, c, rh):
    return ref.at[pl.ds(pl.multiple_of(c * rh, 16), rh), :]


def gather_weights(lands):
    K = len(lands)

    def body(*refs):
        ins, outs = refs[:K], refs[K:2 * K]
        ici_send, ici_recv, d2d_send, d2d_recv = refs[2 * K:]
        x, y, c = _place()
        me = 2 * x + y
        sibling = (x, y, 1 - c)
        others = _other_chips(x, y)
        sent = []
        for k in range(K):
            rh = ins[k].shape[1] // 2
            for r, (_, px, py) in enumerate(others):
                cp = pltpu.make_async_remote_copy(
                    src_ref=_half(ins[k].at[me], c, rh), dst_ref=_half(outs[k].at[me], c, rh),
                    send_sem=ici_send.at[k, r], recv_sem=ici_recv.at[k, r], device_id=(px, py, c), device_id_type=MESH)
                cp.start()
                sent.append(cp)
        forwards = []
        for k in range(K):
            rh = ins[k].shape[1] // 2
            for r, (pchip, px, py) in enumerate(others):
                landed = _half(outs[k].at[pchip], c, rh)
                pltpu.make_async_remote_copy(
                    src_ref=landed, dst_ref=landed, send_sem=ici_send.at[k, r], recv_sem=ici_recv.at[k, r],
                    device_id=(px, py, c), device_id_type=MESH).wait_recv()
                fw = pltpu.make_async_remote_copy(
                    src_ref=landed, dst_ref=landed, send_sem=d2d_send.at[k, r], recv_sem=d2d_recv.at[k, r],
                    device_id=sibling, device_id_type=MESH)
                fw.start()
                forwards.append(fw)
        for k in range(K):
            rh = ins[k].shape[1] // 2
            for r, (pchip, _, _) in enumerate(others):
                theirs = _half(outs[k].at[pchip], 1 - c, rh)
                pltpu.make_async_remote_copy(
                    src_ref=theirs, dst_ref=theirs, send_sem=d2d_send.at[k, r], recv_sem=d2d_recv.at[k, r],
                    device_id=sibling, device_id_type=MESH).wait_recv()
        for cp in sent + forwards:
            cp.wait_send()

    return pl.pallas_call(
        body, name="gather_weights",
        out_shape=[jax.ShapeDtypeStruct(s.shape, s.dtype) for s in lands],
        in_specs=[HBM_SPEC] * K, out_specs=[HBM_SPEC] * K, input_output_aliases={k: k for k in range(K)},
        scratch_shapes=[pltpu.SemaphoreType.DMA((K, 3))] * 4,
    )(*lands)


def pair_exchange(grads):
    K = len(grads)

    def body(*refs):
        ins, outs = refs[:K], refs[K:2 * K]
        send_sems, recv_sems = refs[2 * K:]
        x, y, c = _place()
        sibling = (x, y, 1 - c)
        copies = []
        for k in range(K):
            n, r, _ = ins[k].shape
            rh = r // 2
            cp = pltpu.make_async_remote_copy(
                src_ref=ins[k].at[:, pl.ds(pl.multiple_of((1 - c) * rh, 16), rh), :], dst_ref=outs[k],
                send_sem=send_sems.at[k], recv_sem=recv_sems.at[k], device_id=sibling, device_id_type=MESH)
            cp.start()
            copies.append(cp)
        for cp in copies:
            cp.wait_recv()
        for cp in copies:
            cp.wait_send()

    return pl.pallas_call(
        body, name="pair_exchange",
        out_shape=[jax.ShapeDtypeStruct((g.shape[0], g.shape[1] // 2, g.shape[2]), g.dtype) for g in grads],
        in_specs=[HBM_SPEC] * K, out_specs=[HBM_SPEC] * K,
        scratch_shapes=[pltpu.SemaphoreType.DMA((K,))] * 2,
    )(*grads)


def pair_add(grad, recv, c_idx):
    n, r, C = grad.shape
    rh = r // 2
    tr = _tile(rh, max(16, (1 << 19) // C), 16)
    grad = grad.reshape(n, 2, rh, C)

    def body(c_ref, g_ref, r_ref, o_ref):
        o_ref[...] = (g_ref[...].astype(F32) + r_ref[...].astype(F32)).astype(BF16)

    return pl.pallas_call(
        body, name="pair_add",
        grid_spec=pltpu.PrefetchScalarGridSpec(
            num_scalar_prefetch=1, grid=(n, rh // tr),
            in_specs=[pl.BlockSpec((None, None, tr, C), lambda d, i, c_ref: (d, c_ref[0], i, 0)),
                      pl.BlockSpec((None, tr, C), lambda d, i, c_ref: (d, i, 0))],
            out_specs=pl.BlockSpec((None, tr, C), lambda d, i, c_ref: (d, i, 0))),
        out_shape=jax.ShapeDtypeStruct((n, rh, C), BF16), compiler_params=_cparams(2),
    )(c_idx, grad, recv)


def chip_exchange(parts):
    K = len(parts)

    def body(*refs):
        ins, outs = refs[:K], refs[K:2 * K]
        send_sems, recv_sems = refs[2 * K:]
        x, y, c = _place()
        others = _other_chips(x, y)
        started = []
        for k in range(K):
            for r, (pchip, px, py) in enumerate(others):
                cp = pltpu.make_async_remote_copy(
                    src_ref=ins[k].at[pchip], dst_ref=outs[k].at[r], send_sem=send_sems.at[k, r],
                    recv_sem=recv_sems.at[k, r], device_id=(px, py, c), device_id_type=MESH)
                cp.start()
                started.append(cp)
        for cp in started:
            cp.wait_recv()
        for cp in started:
            cp.wait_send()

    return pl.pallas_call(
        body, name="chip_exchange",
        out_shape=[jax.ShapeDtypeStruct((3,) + p.shape[1:], p.dtype) for p in parts],
        in_specs=[HBM_SPEC] * K, out_specs=[HBM_SPEC] * K,
        scratch_shapes=[pltpu.SemaphoreType.DMA((K, 3))] * 2,
    )(*parts)


def chip_sum(parts, got, where):
    _, rh, C = parts.shape
    tr = _tile(rh, max(16, (1 << 19) // C), 16)
    nt = rh // tr

    def body(w_ref, p_ref, g_ref, o_ref):
        s = p_ref[...].astype(F32)
        for r in range(3):
            s = s + g_ref[r].astype(F32)
        o_ref[...] = s

    return pl.pallas_call(
        body, name="chip_sum",
        grid_spec=pltpu.PrefetchScalarGridSpec(
            num_scalar_prefetch=1, grid=(nt,),
            in_specs=[pl.BlockSpec((None, tr, C), lambda i, w_ref: (w_ref[0], i, 0)),
                      pl.BlockSpec((3, tr, C), lambda i, w_ref: (0, i, 0))],
            out_specs=pl.BlockSpec((tr, C), lambda i, w_ref: (w_ref[1] * nt + i, 0))),
        out_shape=jax.ShapeDtypeStruct((2 * rh, C), F32), compiler_params=_cparams(1),
    )(where, parts, got)


def pair_share(sums):
    K = len(sums)

    def body(*refs):
        ins, outs = refs[:K], refs[K:2 * K]
        send_sems, recv_sems = refs[2 * K:]
        x, y, c = _place()
        sibling = (x, y, 1 - c)
        started = []
        for k in range(K):
            rh = ins[k].shape[0] // 2
            cp = pltpu.make_async_remote_copy(
                src_ref=_half(ins[k], c, rh), dst_ref=_half(outs[k], c, rh), send_sem=send_sems.at[k],
                recv_sem=recv_sems.at[k], device_id=sibling, device_id_type=MESH)
            cp.start()
            started.append(cp)
        for k in range(K):
            rh = ins[k].shape[0] // 2
            theirs = _half(outs[k], 1 - c, rh)
            pltpu.make_async_remote_copy(
                src_ref=theirs, dst_ref=theirs, send_sem=send_sems.at[k], recv_sem=recv_sems.at[k],
                device_id=sibling, device_id_type=MESH).wait_recv()
        for cp in started:
            cp.wait_send()

    return pl.pallas_call(
        body, name="pair_share",
        out_shape=[jax.ShapeDtypeStruct(s.shape, s.dtype) for s in sums],
        in_specs=[HBM_SPEC] * K, out_specs=[HBM_SPEC] * K, input_output_aliases={k: k for k in range(K)},
        scratch_shapes=[pltpu.SemaphoreType.DMA((K,))] * 2,
    )(*sums)


SEM_SPEC = pl.BlockSpec(memory_space=pltpu.SEMAPHORE)
ANY_SPEC = pl.BlockSpec(memory_space=pl.ANY)
DATAFLOW = pltpu.SideEffectType.DATAFLOW_SIDE_EFFECTING


def _in_hbm(a):
    return pltpu.with_memory_space_constraint(a, pltpu.HBM)


def _ici_copies(srcs, dsts, send_sems, recv_sems, src_slice, dst_slice):
    x, y, c = _place()
    out = []
    for k in range(len(srcs)):
        for r, (pchip, px, py) in enumerate(_other_chips(x, y)):
            out.append(pltpu.make_async_remote_copy(
                src_ref=src_slice(srcs[k], r, pchip), dst_ref=dst_slice(dsts[k], r, pchip),
                send_sem=send_sems.at[3 * k + r], recv_sem=recv_sems.at[3 * k + r], device_id=(px, py, c),
                device_id_type=MESH))
    return out


def _exchange_start(bufs, lands, src_slice, dst_slice, name, after=None):
    K = len(bufs)
    same = lands is None
    n_thru = K if same else 2 * K
    n_in = n_thru + (after is not None)

    def body(*refs):
        ins = refs[:n_thru]
        send_sems, recv_sems = refs[n_in], refs[n_in + 1]
        token = refs[-1]
        srcs = ins[:K]
        dsts = srcs if same else ins[K:]
        for cp in _ici_copies(srcs, dsts, send_sems, recv_sems, src_slice, dst_slice):
            cp.start()
        token[...] = jnp.zeros_like(token)

    thru = list(bufs) + ([] if same else list(lands))
    res = pl.pallas_call(
        body, name=name,
        out_shape=[pltpu.SemaphoreType.DMA((3 * K,)), pltpu.SemaphoreType.DMA((3 * K,))]
        + [pltpu.HBM(a.shape, a.dtype) for a in thru] + [jax.ShapeDtypeStruct((8, LANES), F32)],
        in_specs=[HBM_SPEC] * n_thru + [ANY_SPEC] * (after is not None),
        out_specs=[SEM_SPEC, SEM_SPEC] + [HBM_SPEC] * n_thru + [pl.BlockSpec(memory_space=pltpu.VMEM)],
        input_output_aliases={i: 2 + i for i in range(n_thru)},
        compiler_params=pltpu.CompilerParams(has_side_effects=DATAFLOW),
    )(*[_in_hbm(a) for a in thru], *([] if after is None else [after]))
    return res[0], res[1], res[2:2 + K], (res[2:2 + K] if same else res[2 + K:2 + 2 * K]), res[-1]


def _exchange_wait(send_sems, recv_sems, bufs, lands, after, src_slice, dst_slice, name):
    K = len(bufs)
    same = lands is None
    n_thru = K if same else 2 * K

    def body(*refs):
        ins = refs[:n_thru]
        ssem, rsem = refs[n_thru], refs[n_thru + 1]
        srcs = ins[:K]
        dsts = srcs if same else ins[K:]
        copies = _ici_copies(srcs, dsts, ssem, rsem, src_slice, dst_slice)
        for cp in copies:
            cp.wait_send()
        for cp in copies:
            cp.wait_recv()

    thru = list(bufs) + ([] if same else list(lands))
    res = pl.pallas_call(
        body, name=name,
        out_shape=[pltpu.HBM(a.shape, a.dtype) for a in thru],
        in_specs=[HBM_SPEC] * n_thru + [SEM_SPEC, SEM_SPEC, ANY_SPEC],
        out_specs=[HBM_SPEC] * n_thru,
        input_output_aliases={i: i for i in range(n_thru)},
        compiler_params=pltpu.CompilerParams(has_side_effects=DATAFLOW),
    )(*thru, send_sems, recv_sems, after)
    return res[:K], (res[:K] if same else res[K:])


def _own_half(ref, r, pchip):
    x, y, c = _place()
    return _half(ref.at[2 * x + y], c, ref.shape[1] // 2)


def _their_half(ref, r, pchip):
    _, _, c = _place()
    return _half(ref.at[pchip], c, ref.shape[1] // 2)


def gather_start(lands, name, after=None):
    return _exchange_start(lands, None, _own_half, _own_half, name, after)


def gather_wait(handle, after, name):
    ssem, rsem, lands, _, _ = handle
    return _exchange_wait(ssem, rsem, lands, None, after, _own_half, _their_half, name)[1]


def pair_forward(lands):
    K = len(lands)

    def body(*refs):
        ins, outs = refs[:K], refs[K:2 * K]
        send_sems, recv_sems = refs[2 * K:]
        x, y, c = _place()
        sibling = (x, y, 1 - c)
        started = []
        for k in range(K):
            rh = ins[k].shape[1] // 2
            for r, (pchip, _, _) in enumerate(_other_chips(x, y)):
                cp = pltpu.make_async_remote_copy(
                    src_ref=_half(ins[k].at[pchip], c, rh), dst_ref=_half(outs[k].at[pchip], c, rh),
                    send_sem=send_sems.at[k, r], recv_sem=recv_sems.at[k, r], device_id=sibling, device_id_type=MESH)
                cp.start()
                started.append(cp)
        for k in range(K):
            rh = ins[k].shape[1] // 2
            for r, (pchip, _, _) in enumerate(_other_chips(x, y)):
                theirs = _half(outs[k].at[pchip], 1 - c, rh)
                pltpu.make_async_remote_copy(
                    src_ref=theirs, dst_ref=theirs, send_sem=send_sems.at[k, r], recv_sem=recv_sems.at[k, r],
                    device_id=sibling, device_id_type=MESH).wait_recv()
        for cp in started:
            cp.wait_send()

    return pl.pallas_call(
        body, name="pair_forward",
        out_shape=[jax.ShapeDtypeStruct(s.shape, s.dtype) for s in lands],
        in_specs=[HBM_SPEC] * K, out_specs=[HBM_SPEC] * K, input_output_aliases={k: k for k in range(K)},
        scratch_shapes=[pltpu.SemaphoreType.DMA((K, 3))] * 2,
    )(*lands)


def _to_chip(ref, r, pchip):
    return ref.at[pchip]


def _from_relation(ref, r, pchip):
    return ref.at[r]


def reduce_start(grads, c_idx, name, after=None):
    recv = pair_exchange(grads)
    parts = [pair_add(g, r, c_idx) for g, r in zip(grads, recv)]
    lands = [lax.empty((3,) + p.shape[1:], p.dtype) for p in parts]
    return _exchange_start(parts, lands, _to_chip, _from_relation, name, after)


def reduce_finish(handle, after, where, name):
    ssem, rsem, parts, lands, _ = handle
    parts, got = _exchange_wait(ssem, rsem, parts, lands, after, _to_chip, _from_relation, name)
    return pair_share([chip_sum(p, g, where) for p, g in zip(parts, got)])


def _pack(arrs):
    flat = jnp.concatenate([a.reshape(-1).astype(F32) for a in arrs])
    pad = (-flat.shape[0]) % (8 * LANES)
    return jnp.pad(flat, (0, pad)).reshape(-1, LANES)


def _unpack(flat, shapes):
    out, off = [], 0
    for s in shapes:
        n = 1
        for d in s:
            n *= d
        out.append(flat[off:off + n].reshape(s))
        off += n
    return out


def _adamw_any(w, g, m, v, name, token=None):
    shp = w.shape
    C = shp[-1]
    d, nm, nv = adamw(w.reshape(-1, C), g.reshape(-1, C), m.reshape(-1, C), v.reshape(-1, C), name, token)
    return d.reshape(shp), nm.reshape(shp), nv.reshape(shp)


def kernel(x, c, norm_g, w_ada, b_ada, w_ffn_in, w_ffn_out, cm_w_glu, cm_b_glu, cm_w_dw, cm_b_dw, cm_ln_g, cm_ln_b, cm_w_pw, cm_b_pw, dn_w_in, dn_w_sconv, dn_a_log, dn_dt_bias, dn_o_g, dn_w_out, final_g, loss_target, m_norm_g, m_w_ada, m_b_ada, m_w_ffn_in, m_w_ffn_out, m_cm_w_glu, m_cm_b_glu, m_cm_w_dw, m_cm_b_dw, m_cm_ln_g, m_cm_ln_b, m_cm_w_pw, m_cm_b_pw, m_dn_w_in, m_dn_w_sconv, m_dn_a_log, m_dn_dt_bias, m_dn_o_g, m_dn_w_out, m_final_g, v_norm_g, v_w_ada, v_b_ada, v_w_ffn_in, v_w_ffn_out, v_cm_w_glu, v_cm_b_glu, v_cm_w_dw, v_cm_b_dw, v_cm_ln_g, v_cm_ln_b, v_cm_w_pw, v_cm_b_pw, v_dn_w_in, v_dn_w_sconv, v_dn_a_log, v_dn_dt_bias, v_dn_o_g, v_dn_w_out, v_final_g):
    weights = dict(norm_g=norm_g, w_ada=w_ada, b_ada=b_ada, w_ffn_in=w_ffn_in, w_ffn_out=w_ffn_out, cm_w_glu=cm_w_glu,
                   cm_b_glu=cm_b_glu, cm_w_dw=cm_w_dw, cm_b_dw=cm_b_dw, cm_ln_g=cm_ln_g, cm_ln_b=cm_ln_b, cm_w_pw=cm_w_pw,
                   cm_b_pw=cm_b_pw, dn_w_in=dn_w_in, dn_w_sconv=dn_w_sconv, dn_a_log=dn_a_log, dn_dt_bias=dn_dt_bias,
                   dn_o_g=dn_o_g, dn_w_out=dn_w_out, final_g=final_g)
    mom_m = dict(norm_g=m_norm_g, w_ada=m_w_ada, b_ada=m_b_ada, w_ffn_in=m_w_ffn_in, w_ffn_out=m_w_ffn_out,
                 cm_w_glu=m_cm_w_glu, cm_b_glu=m_cm_b_glu, cm_w_dw=m_cm_w_dw, cm_b_dw=m_cm_b_dw, cm_ln_g=m_cm_ln_g,
                 cm_ln_b=m_cm_ln_b, cm_w_pw=m_cm_w_pw, cm_b_pw=m_cm_b_pw, dn_w_in=m_dn_w_in, dn_w_sconv=m_dn_w_sconv,
                 dn_a_log=m_dn_a_log, dn_dt_bias=m_dn_dt_bias, dn_o_g=m_dn_o_g, dn_w_out=m_dn_w_out, final_g=m_final_g)
    mom_v = dict(norm_g=v_norm_g, w_ada=v_w_ada, b_ada=v_b_ada, w_ffn_in=v_w_ffn_in, w_ffn_out=v_w_ffn_out,
                 cm_w_glu=v_cm_w_glu, cm_b_glu=v_cm_b_glu, cm_w_dw=v_cm_w_dw, cm_b_dw=v_cm_b_dw, cm_ln_g=v_cm_ln_g,
                 cm_ln_b=v_cm_ln_b, cm_w_pw=v_cm_w_pw, cm_b_pw=v_cm_b_pw, dn_w_in=v_dn_w_in, dn_w_sconv=v_dn_w_sconv,
                 dn_a_log=v_dn_a_log, dn_dt_bias=v_dn_dt_bias, dn_o_g=v_dn_o_g, dn_w_out=v_dn_w_out, final_g=v_final_g)
    names = list(weights)

    BL, T, D = x.shape
    L = norm_g.shape[0]
    NB = BL * N_DEV
    Ca = w_ada.shape[2]
    C9 = b_ada.shape[1]
    H = dn_a_log.shape[1]
    Dh = dn_o_g.shape[1]
    W = H * Dh
    KC = cm_w_dw.shape[1]
    KS = dn_w_sconv.shape[1]
    n_cm, n_dn = cm_w_glu.shape[0], dn_w_in.shape[0]
    ax, ay, ac = lax.axis_index("x"), lax.axis_index("y"), lax.axis_index("c")
    chip = 2 * ax + ay
    dev = 2 * chip + ac
    c_idx = ac.astype(jnp.int32).reshape(1)
    where = jnp.stack([chip, ac]).astype(jnp.int32)

    def layer_shards(i):
        sh = [w_ffn_in[i, 0], w_ffn_in[i, 1], w_ffn_out[i, 0], w_ffn_out[i, 1]]
        if i % 2 == 0:
            sh += [cm_w_glu[i // 2], cm_w_pw[i // 2]]
        else:
            sh += [dn_w_in[i // 2], dn_w_out[i // 2]]
        return [lax.dynamic_update_slice(lax.empty((N_CHIPS,) + s.shape, BF16), s.astype(BF16)[None], (chip, 0, 0))
                for s in sh]

    lands = [layer_shards(i) for i in range(L)]
    wts = [None] * L

    small_in = [c, norm_g, cm_w_dw, dn_w_sconv]
    gathered = allgather8(_pack(small_in))
    first = gather_start([lands[0][0], lands[0][2]], "gather_start_0a", gathered)
    gathered = gathered.reshape(N_DEV, -1)
    per_dev = [_unpack(gathered[d], [a.shape for a in small_in]) for d in range(N_DEV)]
    c_all = jnp.concatenate([p[0] for p in per_dev], axis=0)
    norm_g_full = jnp.concatenate([per_dev[2 * s][1] for s in range(N_CHIPS)], axis=-1)
    w_dw_full = jnp.concatenate([per_dev[2 * s][2] for s in range(N_CHIPS)], axis=-1)
    w_sconv_full = jnp.concatenate([per_dev[2 * s][3] for s in range(N_CHIPS)], axis=-1)

    b_cols = lax.dynamic_slice_in_dim(b_ada, chip * Ca, Ca, axis=1).reshape(L, 1, Ca)
    mod_part = ada_fwd(c_all, w_ada, b_cols)
    mod_g = allgather8((mod_part + first[4][0, 0]).reshape(-1, LANES))
    rest = gather_start([lands[0][k] for k in (1, 3, 4, 5)], "gather_start_0b", mod_g)
    mod_g = mod_g.reshape(N_DEV, L, NB, Ca)
    mod_all = jnp.concatenate([mod_g[2 * s] for s in range(N_CHIPS)], axis=-1)
    mod = lax.dynamic_slice_in_dim(mod_all, dev * BL, BL, axis=1).reshape(L, BL, 9, D)

    def dn_weights(i):
        full = jnp.transpose(wts[i][4], (1, 0, 2)).reshape(D, -1)
        return full[:, :4 * W], jnp.pad(full[:, 4 * W:], ((0, 0), (0, LANES - 2 * H)))

    def row128(v):
        return jnp.pad(v.reshape(1, -1), ((0, 0), (0, LANES - v.shape[-1])))

    def pad_taps(w):
        return jnp.pad(w, ((0, 1), (0, 0)))

    saved = []
    xs = x
    after = mod
    for i in range(L):
        tok = 0.0
        if i == 0:
            wl = wts[0] = [None] * 6
            wl[0], wl[2] = pair_forward(gather_wait(first, after, "gather_wait_0a"))
        else:
            wl = wts[i] = pair_forward(gather_wait(handle, after, "gather_wait_%d" % i))
            if i + 1 < L:
                handle = gather_start(lands[i + 1], "gather_start_%d" % (i + 1), wl[0])
                tok = handle[4][0, 0]
        sv = {}
        m3 = [mod[i, :, 3 * j:3 * j + 3] + tok for j in range(3)]
        gs = [norm_g_full[i, j].reshape(1, D) for j in range(3)]
        sv["x0"] = xs
        xs, sv["y0"], sv["h0"], sv["gu0"] = ffn_fwd(xs, m3[0], gs[0], wl[0], wl[2])
        sv["x1"] = xs
        if i == 0:
            wl[1], wl[3], wl[4], wl[5] = pair_forward(gather_wait(rest, xs, "gather_wait_0b"))
            handle = gather_start(lands[1], "gather_start_1", wl[1])
            m3 = [m + handle[4][0, 0] for m in m3]
        if i % 2 == 0:
            a = i // 2
            sv["u"] = conv_glu_fwd(xs, m3[1], gs[1], wl[4], cm_b_glu[a].reshape(1, -1))
            xs, sv["y1"], sv["u2"] = conv_out_fwd(
                xs, sv["u"], m3[1], pad_taps(w_dw_full[a]), cm_b_dw[a].reshape(1, D), cm_ln_g[a].reshape(1, D),
                cm_ln_b[a].reshape(1, D), wl[5].reshape(D, D), cm_b_pw[a].reshape(1, D))
        else:
            a = i // 2
            w_main, w_ab = dn_weights(i)
            sv["pre"], sv["z"], sv["ab"] = dn_proj_fwd(xs, m3[1], gs[1], w_main, w_ab)
            qkvgb = dn_conv_fwd(sv["pre"], sv["ab"], w_sconv_full[a], row128(dn_a_log[a]), row128(dn_dt_bias[a]), H)
            sv["qkvgb"] = qkvgb
            sv["o"], sv["sp"], sv["inv"] = dn_chunk_fwd(*qkvgb)
            xs, sv["y1"] = dn_out_fwd(xs, sv["o"], sv["z"], m3[1], dn_o_g[a].reshape(1, Dh), wl[5].reshape(W, D))
        sv["x2"] = xs
        xs, sv["y2"], sv["h2"], sv["gu2"] = ffn_fwd(xs, m3[2], gs[2], wl[1], wl[3])
        saved.append(sv)
        after = xs

    dx, d_final_g, loss_part = final_loss(xs, final_g.reshape(1, D), loss_target)

    g_small = {n: None for n in names}
    d_norm_g = [[None] * 3 for _ in range(L)]
    dmod = [[None] * 3 for _ in range(L)]
    g_cm = {k: [None] * n_cm for k in ("b_glu", "w_dw", "b_dw", "ln_g", "ln_b", "b_pw")}
    g_dn = {k: [None] * n_dn for k in ("w_sconv", "a_log", "dt_bias", "o_g")}
    big = [None] * L

    def ffn_back(i, j, slot, dx, tok=0.0):
        wl, sv = wts[i], saved[i]
        m3 = mod[i, :, 3 * j:3 * j + 3] + tok
        g = norm_g_full[i, j].reshape(1, D)
        gu = sv["gu%d" % j]
        ab_, dgu, dyb, dh0, dgate = ffn_bwd_part(0, dx, gu, m3, wl[slot], wl[2 + slot], y=sv["y%d" % j])
        dx, ab_, dgu, dm, dg = ffn_bwd_part(1, dx, gu, m3, wl[slot], wl[2 + slot], first=(ab_, dgu, dyb, dh0),
                                            x=sv["x%d" % j], g=g)
        dm = dm.at[:, 2:3, :].set(dgate)
        hb = sv["h%d" % j]
        dmod[i][j] = dm
        d_norm_g[i][j] = jnp.sum(dg, axis=(0, 1))
        Fc = wl[slot].shape[2]
        dw_in = matmul_tn(hb.reshape(-1, D), dgu.reshape(2, BL * T, 2 * Fc), Fc, "dw_ffn_in")
        dw_out = matmul_tn(ab_.reshape(-1, 2 * Fc), dyb.reshape(1, -1, D), D, "dw_ffn_out")
        return dx, dw_in, dw_out.reshape(N_CHIPS, -1, D)

    pending, tok = None, 0.0
    for i in reversed(range(L)):
        wl, sv = wts[i], saved[i]
        a = i // 2
        dx, dw_in1, dw_out1 = ffn_back(i, 2, 1, dx, tok)
        m3 = mod[i, :, 3:6]
        g = norm_g_full[i, 1].reshape(1, D)
        if i % 2 == 0:
            w_pw = wl[5].reshape(D, D)
            wdw = pad_taps(w_dw_full[a])
            du2, u3b, dyb, dgate, vec = conv_out_bwd(dx, sv["y1"], sv["u2"], m3, cm_ln_g[a].reshape(1, D),
                                                     cm_ln_b[a].reshape(1, D), w_pw)
            dx, hb, dab, dwdw, dbglu, dm, dg = conv_glu_bwd(sv["x1"], dx, du2, sv["u"], m3, g, wl[4],
                                                            cm_b_glu[a].reshape(1, -1), wdw)
            dm = dm.at[:, 2:3, :].set(dgate)
            vec = jnp.sum(vec, axis=0)
            g_cm["b_pw"][a], g_cm["ln_g"][a], g_cm["ln_b"][a], g_cm["b_dw"][a] = vec[0], vec[1], vec[2], vec[3]
            g_cm["w_dw"][a] = jnp.sum(dwdw, axis=0)[:KC]
            g_cm["b_glu"][a] = jnp.sum(dbglu, axis=(0, 1))
            dw_a = matmul_tn(hb.reshape(-1, D), dab.reshape(1, -1, 2 * D), D // 2, "dw_glu")
            dw_b = matmul_tn(u3b.reshape(-1, D), dyb.reshape(1, -1, D), D, "dw_sq").reshape(N_CHIPS, -1, D)
        else:
            w_main, w_ab = dn_weights(i)
            w_out = wl[5].reshape(W, D)
            do, dz, ogb, dyb, dgate, dog = dn_out_bwd(dx, sv["y1"], sv["o"], sv["z"], m3, dn_o_g[a].reshape(1, Dh), w_out)
            dq, dk, dv, dgb, dbb = dn_chunk_bwd(*sv["qkvgb"], sv["sp"], sv["inv"], do)
            dc, dab, small = dn_conv_bwd(dq, dk, dv, dgb, dbb, sv["pre"], sv["ab"], w_sconv_full[a],
                                         row128(dn_a_log[a]), row128(dn_dt_bias[a]))
            dx, hb, dproj, dws, dm, dg = dn_proj_bwd(sv["x1"], dx, dc, sv["pre"], dz, dab, m3, g, w_main, w_ab,
                                                     w_sconv_full[a])
            dm = dm.at[:, 2:3, :].set(dgate)
            small = jnp.sum(small, axis=0)
            g_dn["a_log"][a], g_dn["dt_bias"][a] = small[0, :H], small[1, :H]
            g_dn["o_g"][a] = jnp.sum(dog, axis=(0, 1))
            g_dn["w_sconv"][a] = jnp.sum(dws, axis=0)
            dw_main = matmul_tn(hb.reshape(-1, D), dproj.reshape(1, -1, 4 * W), W, "dw_dn_main")
            dw_ab = matmul_tn(hb.reshape(-1, D), dab.reshape(1, -1, LANES), LANES, "dw_dn_ab")
            full = jnp.concatenate([jnp.transpose(dw_main, (1, 0, 2)).reshape(D, 4 * W), dw_ab[0][:, :2 * H]], axis=1)
            dw_a = jnp.transpose(full.reshape(D, N_CHIPS, -1), (1, 0, 2))
            dw_b = matmul_tn(ogb.reshape(-1, W), dyb.reshape(1, -1, D), D, "dw_sq").reshape(N_CHIPS, -1, D)
        dmod[i][1] = dm
        d_norm_g[i][1] = jnp.sum(dg, axis=(0, 1))
        if i > 0:
            dx, dw_in0, dw_out0 = ffn_back(i, 0, 0, dx)
            started = reduce_start([dw_in0, dw_in1, dw_out0, dw_out1, dw_a, dw_b], c_idx, "reduce_start_%d" % i)
            if pending is not None:
                big[pending[1]] = reduce_finish(pending[0], dx, where, "reduce_wait_%d" % pending[1])
            pending, tok = (started, i), started[4][0, 0]
        else:
            part_a = reduce_start([dw_in1, dw_out1, dw_a, dw_b], c_idx, "reduce_start_0a")
            if pending is not None:
                big[pending[1]] = reduce_finish(pending[0], dx, where, "reduce_wait_%d" % pending[1])
            dx, dw_in0, dw_out0 = ffn_back(0, 0, 0, dx, part_a[4][0, 0])
            sums_a = reduce_finish(part_a, dx, where, "reduce_wait_0a")

    part = dict(
        norm_g=jnp.stack([jnp.stack(r) for r in d_norm_g]),
        cm_b_glu=jnp.stack(g_cm["b_glu"]), cm_w_dw=jnp.stack(g_cm["w_dw"]), cm_b_dw=jnp.stack(g_cm["b_dw"]),
        cm_ln_g=jnp.stack(g_cm["ln_g"]), cm_ln_b=jnp.stack(g_cm["ln_b"]), cm_b_pw=jnp.stack(g_cm["b_pw"]),
        dn_w_sconv=jnp.stack(g_dn["w_sconv"]), dn_a_log=jnp.stack(g_dn["a_log"]), dn_dt_bias=jnp.stack(g_dn["dt_bias"]),
        dn_o_g=jnp.stack(g_dn["o_g"]), final_g=jnp.sum(d_final_g, axis=(0, 1)),
        loss=jnp.sum(loss_part[:, 0, 0]).reshape(1))
    dmod_loc = jnp.stack([jnp.concatenate(r, axis=1) for r in dmod]).reshape(L, BL, C9)
    keys = list(part)
    packed = _pack([part[k] for k in keys] + [dmod_loc])
    R = packed.shape[0]
    gathered = allgather8(packed).reshape(N_DEV, R, LANES)
    summed = _unpack(sum_devices(gathered).reshape(-1), [part[k].shape for k in keys])
    tot = dict(zip(keys, summed))
    n_small = sum(int(part[k].size) for k in keys)
    dmod_all = gathered.reshape(N_DEV, -1)[:, n_small:n_small + L * BL * C9].reshape(N_DEV, L, BL, C9)
    dmod_all = jnp.transpose(dmod_all, (1, 0, 2, 3)).reshape(L, NB, C9)
    dmod_cols = lax.dynamic_slice_in_dim(dmod_all, chip * Ca, Ca, axis=2)
    g_w_ada, g_b_ada = ada_bwd(c_all, dmod_cols, dmod_all)
    delta, new_m, new_v = {}, {}, {}
    part_b = reduce_start([dw_in0, dw_out0], c_idx, "reduce_start_0b", g_w_ada)
    delta["w_ada"], new_m["w_ada"], new_v["w_ada"] = _adamw_any(w_ada, g_w_ada, m_w_ada, v_w_ada, "adamw_w_ada",
                                                                 part_b[4])
    sums_b = reduce_finish(part_b, new_v["w_ada"], where, "reduce_wait_0b")
    big[0] = [sums_b[0], sums_a[0], sums_b[1], sums_a[1], sums_a[2], sums_a[3]]

    def my_cols(full):
        n = full.shape[-1] // N_CHIPS
        return lax.dynamic_slice_in_dim(full, chip * n, n, axis=full.ndim - 1)

    grads = dict(
        norm_g=my_cols(tot["norm_g"]), w_ada=g_w_ada, b_ada=g_b_ada.reshape(L, C9),
        w_ffn_in=jnp.stack([jnp.stack([big[i][0], big[i][1]]) for i in range(L)]),
        w_ffn_out=jnp.stack([jnp.stack([big[i][2], big[i][3]]) for i in range(L)]),
        cm_w_glu=jnp.stack([big[i][4] for i in range(0, L, 2)]), cm_b_glu=tot["cm_b_glu"], cm_w_dw=my_cols(tot["cm_w_dw"]),
        cm_b_dw=tot["cm_b_dw"], cm_ln_g=tot["cm_ln_g"], cm_ln_b=tot["cm_ln_b"],
        cm_w_pw=jnp.stack([big[i][5] for i in range(0, L, 2)]), cm_b_pw=tot["cm_b_pw"],
        dn_w_in=jnp.stack([big[i][4] for i in range(1, L, 2)]), dn_w_sconv=my_cols(tot["dn_w_sconv"]),
        dn_a_log=tot["dn_a_log"], dn_dt_bias=tot["dn_dt_bias"], dn_o_g=tot["dn_o_g"],
        dn_w_out=jnp.stack([big[i][5] for i in range(1, L, 2)]), final_g=tot["final_g"])

    large = ("w_ada", "w_ffn_in", "w_ffn_out", "cm_w_glu", "cm_w_pw", "dn_w_in", "dn_w_out")
    for n in large[1:]:
        delta[n], new_m[n], new_v[n] = _adamw_any(weights[n], grads[n], mom_m[n], mom_v[n], "adamw_" + n)
    rest = [n for n in names if n not in large]
    shapes = [weights[n].shape for n in rest]
    pd, pm, pv = adamw(_pack([weights[n] for n in rest]), _pack([grads[n] for n in rest]),
                       _pack([mom_m[n] for n in rest]), _pack([mom_v[n] for n in rest]), "adamw_small")
    for n, d_, m_, v_ in zip(rest, _unpack(pd.reshape(-1), shapes), _unpack(pm.reshape(-1), shapes),
                             _unpack(pv.reshape(-1), shapes)):
        delta[n], new_m[n], new_v[n] = d_, m_, v_

    return (tot["loss"].reshape(()), dx, *[grads[n] for n in names], *[delta[n] for n in names],
            *[new_m[n] for n in names], *[new_v[n] for n in names])
```

```python
import functools

import jax
import jax.numpy as jnp
from jax import lax
from jax.experimental import pallas as pl
from jax.experimental.pallas import tpu as pltpu

F32 = jnp.float32
BF16 = jnp.bfloat16
EPS = 1e-6
CHUNK = 64
CHUNKS_PER_STEP = 4
N_CHIPS = 4
N_DEV = 8
LANES = 128
SUBLANES = 8
CONV_HALO = 32
SCONV_HALO = 8
VMEM_LIMIT_V7X = 60 * 1024 * 1024
DW_VMEM_BUDGET = 40 * 1024 * 1024
HI = lax.Precision.HIGHEST
MESH = pl.DeviceIdType.MESH
HBM_SPEC = pl.BlockSpec(memory_space=pltpu.HBM)

ADAM_LR, ADAM_B1, ADAM_B2, ADAM_EPS, ADAM_WD, ADAM_STEP = 0.001, 0.9, 0.999, 1e-08, 0.01, 10


def _cparams(n_axes):
    return pltpu.CompilerParams(dimension_semantics=("arbitrary",) * n_axes, vmem_limit_bytes=VMEM_LIMIT_V7X)


def _tile(n, pref, mult=8):
    for t in range(min(n, pref) // mult * mult, 0, -mult):
        if n % t == 0:
            return t
    return n


def _mm(a, b):
    return lax.dot_general(a.astype(BF16), b.astype(BF16), (((1,), (0,)), ((), ())), preferred_element_type=F32)


def _mm_nt(a, b):
    return lax.dot_general(a.astype(BF16), b.astype(BF16), (((1,), (1,)), ((), ())), preferred_element_type=F32)


def _mm_tn(a, b):
    return lax.dot_general(a.astype(BF16), b.astype(BF16), (((0,), (0,)), ((), ())), preferred_element_type=F32)


def _sigmoid(x):
    return jax.nn.sigmoid(x)


def _dsilu(x, s):
    return s * (1.0 + x * (1.0 - s))


def _softplus(x):
    return jnp.maximum(x, 0.0) + jnp.log(1.0 + jnp.exp(-jnp.abs(x)))


def _modnorm(x, g, scale, shift):
    r = lax.rsqrt(jnp.mean(x * x, axis=-1, keepdims=True) + EPS)
    return (x * r) * g * (1.0 + scale) + shift


def _modnorm_bwd(x, g, scale, dh):
    r = lax.rsqrt(jnp.mean(x * x, axis=-1, keepdims=True) + EPS)
    xn = x * r
    dshift = jnp.sum(dh, axis=0, keepdims=True)
    dscale = jnp.sum(dh * (xn * g), axis=0, keepdims=True)
    dhn = dh * (1.0 + scale)
    dg = jnp.sum(dhn * xn, axis=0, keepdims=True)
    dxn = dhn * g
    dx = r * (dxn - xn * jnp.mean(dxn * xn, axis=-1, keepdims=True))
    return dx, dg, dscale, dshift


def _sum0(a):
    return jnp.sum(a, axis=0, keepdims=True)


def ffn_fwd(x, mod3, g, w_in, w_out):
    B, T, D = x.shape
    Fc = w_in.shape[2]
    w_in = w_in.reshape(2, 2, D, Fc)
    w_out = w_out.reshape(2, Fc, D)
    tm = _tile(T, 512)

    def half(h, wi_ref, wo_ref, gu_ref):
        gt = _mm(h, wi_ref[0])
        up = _mm(h, wi_ref[1])
        gu_ref[0] = gt.astype(BF16)
        gu_ref[1] = up.astype(BF16)
        return _mm(gt * _sigmoid(gt) * up, wo_ref[...])

    def body_a(x_ref, mod_ref, g_ref, wi_ref, wo_ref, h_ref, gu_ref, y0_ref):
        h = _modnorm(x_ref[...], g_ref[...], mod_ref[1:2, :], mod_ref[0:1, :]).astype(BF16)
        h_ref[...] = h
        y0_ref[...] = half(h, wi_ref, wo_ref, gu_ref)

    def body_b(x_ref, h_ref, y0_ref, mod_ref, wi_ref, wo_ref, gu_any, xo_ref, y_ref, gu_ref):
        y = y0_ref[...] + half(h_ref[...], wi_ref, wo_ref, gu_ref)
        y_ref[...] = y
        xo_ref[...] = x_ref[...] + 0.5 * (1.0 + mod_ref[2:3, :]) * y

    tok = pl.BlockSpec((None, tm, D), lambda b, t: (b, t, 0))
    per_b3 = pl.BlockSpec((None, 3, D), lambda b, t: (b, 0, 0))
    gu_shape = jax.ShapeDtypeStruct((2, B, T, 2 * Fc), BF16)

    def w_specs(part):
        return [pl.BlockSpec((2, None, D, Fc), lambda b, t: (0, part, 0, 0)),
                pl.BlockSpec((None, Fc, D), lambda b, t: (part, 0, 0))]

    def gu_spec(part):
        return pl.BlockSpec((2, None, tm, Fc), lambda b, t: (0, b, t, part))

    h, gu, y0 = pl.pallas_call(
        body_a, name="ffn_fwd_a", grid=(B, T // tm),
        in_specs=[tok, per_b3, pl.BlockSpec((1, D), lambda b, t: (0, 0))] + w_specs(0),
        out_specs=[tok, gu_spec(0), tok],
        out_shape=[jax.ShapeDtypeStruct((B, T, D), BF16), gu_shape, jax.ShapeDtypeStruct((B, T, D), F32)],
        compiler_params=_cparams(2),
    )(x, mod3, g, w_in, w_out)
    x_new, y, gu = pl.pallas_call(
        body_b, name="ffn_fwd_b", grid=(B, T // tm),
        in_specs=[tok, tok, tok, per_b3] + w_specs(1) + [pl.BlockSpec(memory_space=pl.ANY)],
        out_specs=[tok, tok, gu_spec(1)],
        out_shape=[jax.ShapeDtypeStruct((B, T, D), F32)] * 2 + [gu_shape],
        input_output_aliases={6: 2},
        compiler_params=_cparams(2),
    )(x, h, y0, mod3, w_in, w_out, gu)
    return x_new, y, h, gu


def ffn_bwd_part(part, dres, gu, mod3, w_in, w_out, first=None, y=None, x=None, g=None):
    B, T, D = dres.shape
    Fc = w_in.shape[2]
    F = 2 * Fc
    w_in = w_in.reshape(2, 2, D, Fc)
    w_out = w_out.reshape(2, Fc, D)
    tm = _tile(T, 256)

    def half(dy, gu_ref, wi_ref, wo_ref, a_ref, dgu_ref):
        gt = gu_ref[0].astype(F32)
        up = gu_ref[1].astype(F32)
        sg = _sigmoid(gt)
        silu = gt * sg
        a_ref[...] = (silu * up).astype(BF16)
        da = _mm_nt(dy, wo_ref[...])
        dup = (da * silu).astype(BF16)
        dgt = (da * up * _dsilu(gt, sg)).astype(BF16)
        dgu_ref[0] = dgt
        dgu_ref[1] = dup
        return _mm_nt(dgt, wi_ref[0]) + _mm_nt(dup, wi_ref[1])

    tok = pl.BlockSpec((None, tm, D), lambda b, t: (b, t, 0))
    per_b3 = pl.BlockSpec((None, 3, D), lambda b, t: (b, 0, 0))
    per_b1 = pl.BlockSpec((None, 1, D), lambda b, t: (b, 0, 0))
    gu_spec = pl.BlockSpec((2, None, tm, Fc), lambda b, t: (0, b, t, part))
    a_spec = pl.BlockSpec((None, tm, Fc), lambda b, t: (b, t, part))
    wi_spec = pl.BlockSpec((2, None, D, Fc), lambda b, t: (0, part, 0, 0))
    wo_spec = pl.BlockSpec((None, Fc, D), lambda b, t: (part, 0, 0))
    a_shape = jax.ShapeDtypeStruct((B, T, F), BF16)
    dgu_shape = jax.ShapeDtypeStruct((2, B, T, F), BF16)

    if part == 0:
        def body(dres_ref, y_ref, gu_ref, mod_ref, wi_ref, wo_ref, a_ref, dgu_ref, dy_ref, dh_ref, dgate_ref):
            @pl.when(pl.program_id(1) == 0)
            def _():
                dgate_ref[...] = jnp.zeros_like(dgate_ref)

            dres = dres_ref[...]
            dy = (0.5 * (1.0 + mod_ref[2:3, :]) * dres).astype(BF16)
            dy_ref[...] = dy
            dgate_ref[...] += _sum0(dres * (0.5 * y_ref[...]))
            dh_ref[...] = half(dy, gu_ref, wi_ref, wo_ref, a_ref, dgu_ref)

        return pl.pallas_call(
            body, name="ffn_bwd_a", grid=(B, T // tm),
            in_specs=[tok, tok, gu_spec, per_b3, wi_spec, wo_spec],
            out_specs=[a_spec, gu_spec, tok, tok, per_b1],
            out_shape=[a_shape, dgu_shape, jax.ShapeDtypeStruct((B, T, D), BF16), jax.ShapeDtypeStruct((B, T, D), F32),
                       jax.ShapeDtypeStruct((B, 1, D), F32)],
            compiler_params=_cparams(2),
        )(dres, y, gu, mod3, w_in, w_out)

    a_full, dgu_full, dy, dh0 = first

    def body(x_ref, dres_ref, dy_ref, dh0_ref, gu_ref, mod_ref, g_ref, wi_ref, wo_ref, a_any, dgu_any,
             dx_ref, a_ref, dgu_ref, dmod_ref, dg_ref):
        @pl.when(pl.program_id(1) == 0)
        def _():
            dmod_ref[...] = jnp.zeros_like(dmod_ref)
            dg_ref[...] = jnp.zeros_like(dg_ref)

        dh = dh0_ref[...] + half(dy_ref[...], gu_ref, wi_ref, wo_ref, a_ref, dgu_ref)
        dxn, dg, dscale, dshift = _modnorm_bwd(x_ref[...], g_ref[...], mod_ref[1:2, :], dh)
        dx_ref[...] = dres_ref[...] + dxn
        dmod_ref[0:1, :] += dshift
        dmod_ref[1:2, :] += dscale
        dg_ref[...] += dg

    return pl.pallas_call(
        body, name="ffn_bwd_b", grid=(B, T // tm),
        in_specs=[tok, tok, tok, tok, gu_spec, per_b3, pl.BlockSpec((1, D), lambda b, t: (0, 0)), wi_spec, wo_spec,
                  ANY_SPEC, ANY_SPEC],
        out_specs=[tok, a_spec, gu_spec, per_b3, per_b1],
        out_shape=[jax.ShapeDtypeStruct((B, T, D), F32), a_shape, dgu_shape, jax.ShapeDtypeStruct((B, 3, D), F32),
                   jax.ShapeDtypeStruct((B, 1, D), F32)],
        input_output_aliases={9: 1, 10: 2},
        compiler_params=_cparams(2),
    )(x, dres, dy, dh0, gu, mod3, g, w_in, w_out, a_full, dgu_full)


def matmul_tn(xm, ym, bm, name):
    N, K = xm.shape
    GY, _, MY = ym.shape
    per = MY // bm
    nb = GY * per
    fixed = K * bm * (4 + 2 * 2)
    tn = _tile(N, max(512, (DW_VMEM_BUDGET - fixed) // (2 * 2 * (K + bm))), 256)

    def body(x_ref, y_ref, o_ref, acc_s):
        n = pl.program_id(1)

        @pl.when(n == 0)
        def _():
            acc_s[...] = jnp.zeros_like(acc_s)

        acc_s[...] += _mm_tn(x_ref[...], y_ref[...])

        @pl.when(n == N // tn - 1)
        def _():
            o_ref[...] = acc_s[...].astype(BF16)

    return pl.pallas_call(
        body, name=name, grid=(nb, N // tn),
        in_specs=[pl.BlockSpec((tn, K), lambda m, n: (n, 0)),
                  pl.BlockSpec((None, tn, bm), lambda m, n: (m // per, n, m % per))],
        out_specs=pl.BlockSpec((None, K, bm), lambda m, n: (m, 0, 0)),
        out_shape=jax.ShapeDtypeStruct((nb, K, bm), BF16),
        scratch_shapes=[pltpu.VMEM((K, bm), F32)],
        compiler_params=_cparams(2),
    )(xm, ym)


def final_loss(x, fg, target):
    B, T, D = x.shape
    tm = _tile(T, 512)

    def body(x_ref, g_ref, t_ref, dx_ref, dfg_ref, loss_ref):
        t = pl.program_id(1)

        @pl.when(t == 0)
        def _():
            dfg_ref[...] = jnp.zeros_like(dfg_ref)
            loss_ref[...] = jnp.zeros_like(loss_ref)

        xv = x_ref[...]
        g = g_ref[...]
        r = lax.rsqrt(jnp.mean(xv * xv, axis=-1, keepdims=True) + EPS)
        xn = xv * r
        err = xn * g - t_ref[...]
        tok_loss = jnp.mean(err * err, axis=-1, keepdims=True)
        loss_ref[...] += 0.5 * jnp.sum(tok_loss, axis=0, keepdims=True)
        dy = err * (1.0 / D)
        dfg_ref[...] += _sum0(dy * xn)
        dxn = dy * g
        dx_ref[...] = r * (dxn - xn * jnp.mean(dxn * xn, axis=-1, keepdims=True))

    tok = pl.BlockSpec((None, tm, D), lambda b, t: (b, t, 0))
    return pl.pallas_call(
        body, name="final_loss", grid=(B, T // tm),
        in_specs=[tok, pl.BlockSpec((1, D), lambda b, t: (0, 0)), tok],
        out_specs=[tok, pl.BlockSpec((None, 1, D), lambda b, t: (b, 0, 0)),
                   pl.BlockSpec((None, 1, LANES), lambda b, t: (b, 0, 0))],
        out_shape=[jax.ShapeDtypeStruct((B, T, D), F32), jax.ShapeDtypeStruct((B, 1, D), F32),
                   jax.ShapeDtypeStruct((B, 1, LANES), F32)],
        compiler_params=_cparams(2),
    )(x, fg, target)


def _past_halo_spec(tm, halo, width):
    return pl.BlockSpec((None, halo, width), lambda b, t: (b, jnp.maximum(t * (tm // halo) - 1, 0), 0))


def _future_halo_spec(tm, halo, width, T):
    return pl.BlockSpec((None, halo, width), lambda b, t: (b, jnp.minimum((t + 1) * (tm // halo), T // halo - 1), 0))


def _fill_shifted(ext_s):
    n = ext_s.shape[1]
    for b in range(1, SUBLANES):
        ext_s[b, 0:n - SUBLANES, :] = ext_s[0, pl.ds(b, n - SUBLANES), :]


def _shifted(ext_s, offset, rows):
    a, b = divmod(offset, SUBLANES)
    return ext_s[b, pl.ds(SUBLANES * a, rows), :]


def _glu_fwd(h, w_ref, bias):
    D = h.shape[1]
    a = jnp.concatenate([_mm(h, w_ref[0]), _mm(h, w_ref[1])], axis=1) + bias[:, :D]
    b = jnp.concatenate([_mm(h, w_ref[2]), _mm(h, w_ref[3])], axis=1) + bias[:, D:]
    return a, b


def conv_glu_fwd(x, mod3, g, w_glu, b_glu):
    B, T, D = x.shape
    tm = _tile(T, 512)

    def body(x_ref, mod_ref, g_ref, w_ref, b_ref, u_ref):
        h = _modnorm(x_ref[...], g_ref[...], mod_ref[1:2, :], mod_ref[0:1, :]).astype(BF16)
        a, b = _glu_fwd(h, w_ref, b_ref[...])
        u_ref[...] = a * _sigmoid(b)

    tok = pl.BlockSpec((None, tm, D), lambda b, t: (b, t, 0))
    return pl.pallas_call(
        body, name="conv_glu_fwd", grid=(B, T // tm),
        in_specs=[tok, pl.BlockSpec((None, 3, D), lambda b, t: (b, 0, 0)),
                  pl.BlockSpec((1, D), lambda b, t: (0, 0)),
                  pl.BlockSpec((4, D, D // 2), lambda b, t: (0, 0, 0)),
                  pl.BlockSpec((1, 2 * D), lambda b, t: (0, 0))],
        out_specs=tok, out_shape=jax.ShapeDtypeStruct((B, T, D), F32),
        compiler_params=_cparams(2),
    )(x, mod3, g, w_glu, b_glu)


def _layer_norm_parts(u2):
    mu = jnp.mean(u2, axis=-1, keepdims=True)
    xc = u2 - mu
    rs = lax.rsqrt(jnp.mean(xc * xc, axis=-1, keepdims=True) + EPS)
    return xc * rs, rs


def conv_out_fwd(x, u, mod3, w_dw, b_dw, ln_g, ln_b, w_pw, b_pw):
    B, T, D = x.shape
    K = w_dw.shape[0] - 1
    tm = _tile(T, 512)

    def body(x_ref, u_ref, halo_ref, mod_ref, wdw_ref, bdw_ref, lg_ref, lb_ref, wpw_ref, bpw_ref,
             xo_ref, y_ref, u2_ref, ext_s):
        t = pl.program_id(1)
        ext_s[0, 0:CONV_HALO, :] = jnp.where(t > 0, halo_ref[...], 0.0)
        ext_s[0, CONV_HALO:, :] = u_ref[...]
        _fill_shifted(ext_s)
        acc = jnp.broadcast_to(bdw_ref[...], (tm, D))
        for k in range(K):
            acc = acc + wdw_ref[k:k + 1, :] * _shifted(ext_s, CONV_HALO - (K - 1) + k, tm)
        u2_ref[...] = acc
        xh, _ = _layer_norm_parts(acc)
        l = xh * lg_ref[...] + lb_ref[...]
        u3 = l * _sigmoid(l)
        y = _mm(u3, wpw_ref[...]) + bpw_ref[...]
        y_ref[...] = y
        xo_ref[...] = x_ref[...] + (1.0 + mod_ref[2:3, :]) * y

    tok = pl.BlockSpec((None, tm, D), lambda b, t: (b, t, 0))
    vec = pl.BlockSpec((1, D), lambda b, t: (0, 0))
    return pl.pallas_call(
        body, name="conv_out_fwd", grid=(B, T // tm),
        in_specs=[tok, tok, _past_halo_spec(tm, CONV_HALO, D), pl.BlockSpec((None, 3, D), lambda b, t: (b, 0, 0)),
                  pl.BlockSpec((K + 1, D), lambda b, t: (0, 0)), vec, vec, vec,
                  pl.BlockSpec((D, D), lambda b, t: (0, 0)), vec],
        out_specs=[tok, tok, tok], out_shape=[jax.ShapeDtypeStruct((B, T, D), F32)] * 3,
        scratch_shapes=[pltpu.VMEM((SUBLANES, tm + CONV_HALO, D), F32)],
        compiler_params=_cparams(2),
    )(x, u, u, mod3, w_dw, b_dw, ln_g, ln_b, w_pw, b_pw)


def conv_out_bwd(dres, y, u2, mod3, ln_g, ln_b, w_pw):
    B, T, D = dres.shape
    tm = _tile(T, 512)

    def body(dres_ref, y_ref, u2_ref, mod_ref, lg_ref, lb_ref, wpw_ref, du2_ref, u3_ref, dy_ref, dgate_ref, vec_ref):
        t = pl.program_id(1)

        @pl.when(t == 0)
        def _():
            dgate_ref[...] = jnp.zeros_like(dgate_ref)
            vec_ref[...] = jnp.zeros_like(vec_ref)

        dres = dres_ref[...]
        dy = (1.0 + mod_ref[2:3, :]) * dres
        dy_ref[...] = dy.astype(BF16)
        dgate_ref[...] += _sum0(dres * y_ref[...])
        xh, rs = _layer_norm_parts(u2_ref[...])
        lg = lg_ref[...]
        l = xh * lg + lb_ref[...]
        sg = _sigmoid(l)
        u3_ref[...] = (l * sg).astype(BF16)
        du3 = _mm_nt(dy, wpw_ref[...])
        dl = du3 * _dsilu(l, sg)
        dxh = dl * lg
        du2 = rs * (dxh - jnp.mean(dxh, axis=-1, keepdims=True) - xh * jnp.mean(dxh * xh, axis=-1, keepdims=True))
        du2_ref[...] = du2
        vec_ref[0:1, :] += _sum0(dy)
        vec_ref[1:2, :] += _sum0(dl * xh)
        vec_ref[2:3, :] += _sum0(dl)
        vec_ref[3:4, :] += _sum0(du2)

    tok = pl.BlockSpec((None, tm, D), lambda b, t: (b, t, 0))
    tokb = pl.BlockSpec((None, tm, D), lambda b, t: (b, t, 0))
    vec = pl.BlockSpec((1, D), lambda b, t: (0, 0))
    return pl.pallas_call(
        body, name="conv_out_bwd", grid=(B, T // tm),
        in_specs=[tok, tok, tok, pl.BlockSpec((None, 3, D), lambda b, t: (b, 0, 0)), vec, vec,
                  pl.BlockSpec((D, D), lambda b, t: (0, 0))],
        out_specs=[tok, tokb, tokb, pl.BlockSpec((None, 1, D), lambda b, t: (b, 0, 0)),
                   pl.BlockSpec((None, 4, D), lambda b, t: (b, 0, 0))],
        out_shape=[jax.ShapeDtypeStruct((B, T, D), F32), jax.ShapeDtypeStruct((B, T, D), BF16),
                   jax.ShapeDtypeStruct((B, T, D), BF16), jax.ShapeDtypeStruct((B, 1, D), F32),
                   jax.ShapeDtypeStruct((B, 4, D), F32)],
        compiler_params=_cparams(2),
    )(dres, y, u2, mod3, ln_g, ln_b, w_pw)


def conv_glu_bwd(x, dres, du2, u, mod3, g, w_glu, b_glu, w_dw):
    B, T, D = x.shape
    K = w_dw.shape[0] - 1
    tm = _tile(T, 256)
    nt = T // tm

    def body(x_ref, dres_ref, du2_ref, du2h_ref, u_ref, uh_ref, mod_ref, g_ref, w_ref, b_ref, wdw_ref,
             dx_ref, h_ref, dab_ref, dwdw_ref, dbglu_ref, dmod_ref, dg_ref, extu_s, extd_s):
        t = pl.program_id(1)

        @pl.when(t == 0)
        def _():
            dwdw_ref[...] = jnp.zeros_like(dwdw_ref)
            dbglu_ref[...] = jnp.zeros_like(dbglu_ref)
            dmod_ref[...] = jnp.zeros_like(dmod_ref)
            dg_ref[...] = jnp.zeros_like(dg_ref)

        du2 = du2_ref[...]
        extu_s[0, 0:CONV_HALO, :] = jnp.where(t > 0, uh_ref[...], 0.0)
        extu_s[0, CONV_HALO:, :] = u_ref[...]
        extd_s[0, 0:tm, :] = du2
        extd_s[0, tm:, :] = jnp.where(t < nt - 1, du2h_ref[...], 0.0)
        _fill_shifted(extu_s)
        _fill_shifted(extd_s)
        du = jnp.zeros((tm, D), F32)
        for k in range(K):
            du = du + wdw_ref[k:k + 1, :] * _shifted(extd_s, K - 1 - k, tm)
            dwdw_ref[k:k + 1, :] += _sum0(du2 * _shifted(extu_s, CONV_HALO - (K - 1) + k, tm))
        xv = x_ref[...]
        h = _modnorm(xv, g_ref[...], mod_ref[1:2, :], mod_ref[0:1, :]).astype(BF16)
        h_ref[...] = h
        a, b = _glu_fwd(h, w_ref, b_ref[...])
        sb = _sigmoid(b)
        da = du * sb
        db = du * a * sb * (1.0 - sb)
        dbglu_ref[:, 0:D] += _sum0(da)
        dbglu_ref[:, D:] += _sum0(db)
        da = da.astype(BF16)
        db = db.astype(BF16)
        dab_ref[:, 0:D] = da
        dab_ref[:, D:] = db
        Dh2 = D // 2
        dh = (_mm_nt(da[:, :Dh2], w_ref[0]) + _mm_nt(da[:, Dh2:], w_ref[1])
              + _mm_nt(db[:, :Dh2], w_ref[2]) + _mm_nt(db[:, Dh2:], w_ref[3]))
        dxn, dg, dscale, dshift = _modnorm_bwd(xv, g_ref[...], mod_ref[1:2, :], dh)
        dx_ref[...] = dres_ref[...] + dxn
        dmod_ref[0:1, :] += dshift
        dmod_ref[1:2, :] += dscale
        dg_ref[...] += dg

    tok = pl.BlockSpec((None, tm, D), lambda b, t: (b, t, 0))
    return pl.pallas_call(
        body, name="conv_glu_bwd", grid=(B, nt),
        in_specs=[tok, tok, tok, _future_halo_spec(tm, CONV_HALO, D, T), tok, _past_halo_spec(tm, CONV_HALO, D),
                  pl.BlockSpec((None, 3, D), lambda b, t: (b, 0, 0)), pl.BlockSpec((1, D), lambda b, t: (0, 0)),
                  pl.BlockSpec((4, D, D // 2), lambda b, t: (0, 0, 0)), pl.BlockSpec((1, 2 * D), lambda b, t: (0, 0)),
                  pl.BlockSpec((K + 1, D), lambda b, t: (0, 0))],
        out_specs=[tok, tok, pl.BlockSpec((None, tm, 2 * D), lambda b, t: (b, t, 0)),
                   pl.BlockSpec((None, K + 1, D), lambda b, t: (b, 0, 0)),
                   pl.BlockSpec((None, 1, 2 * D), lambda b, t: (b, 0, 0)),
                   pl.BlockSpec((None, 3, D), lambda b, t: (b, 0, 0)),
                   pl.BlockSpec((None, 1, D), lambda b, t: (b, 0, 0))],
        out_shape=[jax.ShapeDtypeStruct((B, T, D), F32), jax.ShapeDtypeStruct((B, T, D), BF16),
                   jax.ShapeDtypeStruct((B, T, 2 * D), BF16), jax.ShapeDtypeStruct((B, K + 1, D), F32),
                   jax.ShapeDtypeStruct((B, 1, 2 * D), F32), jax.ShapeDtypeStruct((B, 3, D), F32),
                   jax.ShapeDtypeStruct((B, 1, D), F32)],
        scratch_shapes=[pltpu.VMEM((SUBLANES, tm + CONV_HALO, D), F32)] * 2,
        compiler_params=_cparams(2),
    )(x, dres, du2, du2, u, u, mod3, g, w_glu, b_glu, w_dw)


def dn_proj_fwd(x, mod3, g, w_main, w_ab):
    B, T, D = x.shape
    W = w_main.shape[1] // 4
    tm = _tile(T, 512)

    def body(x_ref, mod_ref, g_ref, wm_ref, wab_ref, pre_ref, z_ref, ab_ref):
        h = _modnorm(x_ref[...], g_ref[...], mod_ref[1:2, :], mod_ref[0:1, :]).astype(BF16)
        for p in range(3):
            pre_ref[:, p * W:(p + 1) * W] = _mm(h, wm_ref[:, p * W:(p + 1) * W])
        z_ref[...] = _mm(h, wm_ref[:, 3 * W:])
        ab_ref[...] = _mm(h, wab_ref[...])

    return pl.pallas_call(
        body, name="dn_proj_fwd", grid=(B, T // tm),
        in_specs=[pl.BlockSpec((None, tm, D), lambda b, t: (b, t, 0)), pl.BlockSpec((None, 3, D), lambda b, t: (b, 0, 0)),
                  pl.BlockSpec((1, D), lambda b, t: (0, 0)), pl.BlockSpec((D, 4 * W), lambda b, t: (0, 0)),
                  pl.BlockSpec((D, LANES), lambda b, t: (0, 0))],
        out_specs=[pl.BlockSpec((None, tm, 3 * W), lambda b, t: (b, t, 0)),
                   pl.BlockSpec((None, tm, W), lambda b, t: (b, t, 0)),
                   pl.BlockSpec((None, tm, LANES), lambda b, t: (b, t, 0))],
        out_shape=[jax.ShapeDtypeStruct((B, T, 3 * W), F32), jax.ShapeDtypeStruct((B, T, W), F32),
                   jax.ShapeDtypeStruct((B, T, LANES), F32)],
        compiler_params=_cparams(2),
    )(x, mod3, g, w_main, w_ab)


def _sconv(ext_s, w_ref, tm, K):
    acc = w_ref[0:1, :] * ext_s[pl.ds(SCONV_HALO - (K - 1), tm), :]
    for k in range(1, K):
        acc = acc + w_ref[k:k + 1, :] * ext_s[pl.ds(SCONV_HALO - (K - 1) + k, tm), :]
    return acc


def _lane_col(val, lane, idx):
    return jnp.sum(jnp.where(lane == idx, val, 0.0), axis=1, keepdims=True)


def dn_conv_fwd(pre, ab, w_sconv, alog_row, dt_row, H):
    B, T, W3 = pre.shape
    W = W3 // 3
    Dh = W // H
    K = w_sconv.shape[0]
    tm = _tile(T, 512)

    def body(pre_ref, halo_ref, ab_ref, w_ref, alog_ref, dt_ref, q_ref, k_ref, v_ref, gb_ref, bb_ref, ext_s):
        t = pl.program_id(1)
        ext_s[0:SCONV_HALO, :] = jnp.where(t > 0, halo_ref[...], 0.0)
        ext_s[SCONV_HALO:, :] = pre_ref[...]
        cv = _sconv(ext_s, w_ref, tm, K)
        qkv = cv * _sigmoid(cv)
        ab = ab_ref[...]
        lane = lax.broadcasted_iota(jnp.int32, ab.shape, 1)
        g_all = -jnp.exp(alog_ref[...]) * _softplus(ab + dt_ref[...])
        beta_all = _sigmoid(ab)
        for h in range(H):
            q_ref[h] = qkv[:, h * Dh:(h + 1) * Dh]
            k_ref[h] = qkv[:, W + h * Dh:W + (h + 1) * Dh]
            v_ref[h] = qkv[:, 2 * W + h * Dh:2 * W + (h + 1) * Dh]
            gb_ref[h] = jnp.broadcast_to(_lane_col(g_all, lane, h), (tm, Dh))
            bb_ref[h] = jnp.broadcast_to(_lane_col(beta_all, lane, H + h), (tm, Dh))

    hm = pl.BlockSpec((None, H, tm, Dh), lambda b, t: (b, 0, t, 0))
    row = pl.BlockSpec((1, LANES), lambda b, t: (0, 0))
    return pl.pallas_call(
        body, name="dn_conv_fwd", grid=(B, T // tm),
        in_specs=[pl.BlockSpec((None, tm, W3), lambda b, t: (b, t, 0)), _past_halo_spec(tm, SCONV_HALO, W3),
                  pl.BlockSpec((None, tm, LANES), lambda b, t: (b, t, 0)),
                  pl.BlockSpec((K, W3), lambda b, t: (0, 0)), row, row],
        out_specs=[hm] * 5, out_shape=[jax.ShapeDtypeStruct((B, H, T, Dh), F32)] * 5,
        scratch_shapes=[pltpu.VMEM((tm + SCONV_HALO, W3), F32)],
        compiler_params=_cparams(2),
    )(pre, pre, ab, w_sconv, alog_row, dt_row)


def _bdot(spec):
    return lambda a, b: jnp.einsum(spec, a.astype(BF16), b.astype(BF16), preferred_element_type=F32)


_NN, _NT, _TN = "gij,gjk->gik", "gik,gjk->gij", "gki,gkj->gij"


def _make_bdots():
    nn_, nt_, tn_ = _bdot(_NN), _bdot(_NT), _bdot(_TN)

    @jax.custom_vjp
    def nn(a, b):
        return nn_(a, b)

    @jax.custom_vjp
    def nt(a, b):
        return nt_(a, b)

    @jax.custom_vjp
    def tn(a, b):
        return tn_(a, b)

    nn.defvjp(lambda a, b: (nn_(a, b), (a, b)), lambda r, d: (nt_(d, r[1]), tn_(r[0], d)))
    nt.defvjp(lambda a, b: (nt_(a, b), (a, b)), lambda r, d: (nn_(d, r[1]), tn_(d, r[0])))
    tn.defvjp(lambda a, b: (tn_(a, b), (a, b)), lambda r, d: (nt_(r[1], d), nn_(r[0], d)))
    return nn, nt, tn


def _unit_lower_inverse(A, known=None):
    hdot = functools.partial(jnp.einsum, precision=lax.Precision.HIGH, preferred_element_type=F32)
    C = A.shape[-1]

    def impl(A):
        eye = (lax.broadcasted_iota(jnp.int32, A.shape, 1) == lax.broadcasted_iota(jnp.int32, A.shape, 2)).astype(F32)
        Tm = eye - A
        Ap = A
        for _ in range(max(1, (C - 1).bit_length()) - 1):
            Ap = hdot(_NN, Ap, Ap)
            Tm = Tm + hdot(_NN, Tm, Ap)
        return Tm

    @jax.custom_vjp
    def inv(A, given):
        return impl(A) if known is None else given

    def fwd(A, given):
        Tm = impl(A) if known is None else given
        return Tm, Tm

    def bwd(Tm, dT):
        return -hdot(_NT, hdot(_TN, Tm, dT), Tm), jnp.zeros_like(Tm)

    inv.defvjp(fwd, bwd)
    return inv(A, A if known is None else known)


def _chunk_fn(q, k, v, gb, bb, S, inverse=None, with_inverse=False):
    nn, nt, tn = _make_bdots()
    G, C, Dh = q.shape
    hdot = functools.partial(jnp.einsum, precision=lax.Precision.HIGH, preferred_element_type=F32)
    q = q * lax.rsqrt(jnp.sum(q * q, axis=-1, keepdims=True) + EPS) * (Dh ** -0.5)
    k = k * lax.rsqrt(jnp.sum(k * k, axis=-1, keepdims=True) + EPS)
    row = lax.broadcasted_iota(jnp.int32, (G, C, C), 1)
    col = lax.broadcasted_iota(jnp.int32, (G, C, C), 2)
    causal = row >= col
    strict = row > col
    gc = hdot(_NN, causal.astype(F32), gb)
    spread = jnp.full((G, C, Dh), 1.0 / Dh, F32)
    gi = hdot(_NT, gc, spread)
    gj = hdot(_NT, spread, gc)
    decay = jnp.where(causal, jnp.exp(jnp.where(causal, gi - gj, 0.0)), 0.0)
    kb = k * bb
    vb = v * bb
    A = jnp.where(strict, nt(kb, k) * decay, 0.0)
    Tm = _unit_lower_inverse(A, inverse)
    eg = jnp.exp(gc)
    u = nn(Tm, vb)
    w = nn(Tm, kb * eg)
    qg = q * eg
    intra = nt(q, k) * decay
    glast = hdot(_NN, jnp.ones((G, C, C), F32), gb)
    kd = k * jnp.exp(glast - gc)
    v_new = u - nn(w, S)
    o = nn(qg, S) + nn(intra, v_new)
    egl = jnp.exp(glast)
    S_new = S * jnp.concatenate([egl] * (Dh // C), axis=1) + tn(kd, v_new)
    return (o, S_new, Tm) if with_inverse else (o, S_new)


def dn_chunk_fwd(q, k, v, gb, bb):
    B, H, T, Dh = q.shape
    NC = T // CHUNK
    NS = _tile(NC, CHUNKS_PER_STEP, 1)

    def body(q_ref, k_ref, v_ref, gb_ref, bb_ref, o_ref, sp_ref, inv_ref, S_s):
        @pl.when(pl.program_id(1) == 0)
        def _():
            S_s[...] = jnp.zeros_like(S_s)

        def one_chunk(j, carry):
            rows = pl.ds(pl.multiple_of(j * CHUNK, CHUNK), CHUNK)
            S = S_s[...]
            sp_ref[j] = S
            o, S_new, Tm = _chunk_fn(q_ref[:, rows, :], k_ref[:, rows, :], v_ref[:, rows, :], gb_ref[:, rows, :],
                                     bb_ref[:, rows, :], S, with_inverse=True)
            o_ref[:, rows, :] = o
            inv_ref[j] = Tm
            S_s[...] = S_new
            return carry

        lax.fori_loop(0, NS, one_chunk, 0)

    hm = pl.BlockSpec((None, H, NS * CHUNK, Dh), lambda b, n: (b, 0, n, 0))
    return pl.pallas_call(
        body, name="dn_chunk_fwd", grid=(B, NC // NS),
        in_specs=[hm] * 5,
        out_specs=[hm, pl.BlockSpec((None, NS, H, Dh, Dh), lambda b, n: (b, n, 0, 0, 0)),
                   pl.BlockSpec((None, NS, H, CHUNK, CHUNK), lambda b, n: (b, n, 0, 0, 0))],
        out_shape=[jax.ShapeDtypeStruct((B, H, T, Dh), F32), jax.ShapeDtypeStruct((B, NC, H, Dh, Dh), F32),
                   jax.ShapeDtypeStruct((B, NC, H, CHUNK, CHUNK), F32)],
        scratch_shapes=[pltpu.VMEM((H, Dh, Dh), F32)],
        compiler_params=_cparams(2),
    )(q, k, v, gb, bb)


def dn_chunk_bwd(q, k, v, gb, bb, s_prev, inv, do):
    B, H, T, Dh = q.shape
    NC = T // CHUNK
    NS = _tile(NC, CHUNKS_PER_STEP, 1)
    NG = NC // NS

    def body(q_ref, k_ref, v_ref, gb_ref, bb_ref, sp_ref, inv_ref, do_ref, dq_ref, dk_ref, dv_ref, dgb_ref, dbb_ref,
             dS_s):
        @pl.when(pl.program_id(1) == 0)
        def _():
            dS_s[...] = jnp.zeros_like(dS_s)

        def one_chunk(jj, carry):
            j = NS - 1 - jj
            rows = pl.ds(pl.multiple_of(j * CHUNK, CHUNK), CHUNK)
            _, vjp = jax.vjp(functools.partial(_chunk_fn, inverse=inv_ref[j]), q_ref[:, rows, :], k_ref[:, rows, :],
                             v_ref[:, rows, :], gb_ref[:, rows, :], bb_ref[:, rows, :], sp_ref[j])
            dq, dk, dv, dgb, dbb, dS = vjp((do_ref[:, rows, :], dS_s[...]))
            dq_ref[:, rows, :] = dq
            dk_ref[:, rows, :] = dk
            dv_ref[:, rows, :] = dv
            dgb_ref[:, rows, :] = dgb
            dbb_ref[:, rows, :] = dbb
            dS_s[...] = dS
            return carry

        lax.fori_loop(0, NS, one_chunk, 0)

    hm = pl.BlockSpec((None, H, NS * CHUNK, Dh), lambda b, n: (b, 0, NG - 1 - n, 0))
    return pl.pallas_call(
        body, name="dn_chunk_bwd", grid=(B, NG),
        in_specs=[hm] * 5 + [pl.BlockSpec((None, NS, H, Dh, Dh), lambda b, n: (b, NG - 1 - n, 0, 0, 0)),
                             pl.BlockSpec((None, NS, H, CHUNK, CHUNK), lambda b, n: (b, NG - 1 - n, 0, 0, 0)), hm],
        out_specs=[hm] * 5, out_shape=[jax.ShapeDtypeStruct((B, H, T, Dh), F32)] * 5,
        scratch_shapes=[pltpu.VMEM((H, Dh, Dh), F32)],
        compiler_params=_cparams(2),
    )(q, k, v, gb, bb, s_prev, inv, do)


def _head_norm(o, og):
    r = lax.rsqrt(jnp.mean(o * o, axis=-1, keepdims=True) + EPS)
    return o * r, r


def dn_out_fwd(x, o, z, mod3, o_g, w_out):
    B, T, D = x.shape
    _, H, _, Dh = o.shape
    W = H * Dh
    tm = _tile(T, 512)

    def body(x_ref, o_ref, z_ref, mod_ref, og_ref, w_ref, xo_ref, y_ref):
        parts = []
        for h in range(H):
            on, _ = _head_norm(o_ref[h], og_ref[...])
            zz = z_ref[:, h * Dh:(h + 1) * Dh]
            parts.append((on * og_ref[...] * (zz * _sigmoid(zz))).astype(BF16))
        y = _mm(jnp.concatenate(parts, axis=1), w_ref[...])
        y_ref[...] = y
        xo_ref[...] = x_ref[...] + (1.0 + mod_ref[2:3, :]) * y

    tok = pl.BlockSpec((None, tm, D), lambda b, t: (b, t, 0))
    return pl.pallas_call(
        body, name="dn_out_fwd", grid=(B, T // tm),
        in_specs=[tok, pl.BlockSpec((None, H, tm, Dh), lambda b, t: (b, 0, t, 0)),
                  pl.BlockSpec((None, tm, W), lambda b, t: (b, t, 0)), pl.BlockSpec((None, 3, D), lambda b, t: (b, 0, 0)),
                  pl.BlockSpec((1, Dh), lambda b, t: (0, 0)), pl.BlockSpec((W, D), lambda b, t: (0, 0))],
        out_specs=[tok, tok], out_shape=[jax.ShapeDtypeStruct((B, T, D), F32)] * 2,
        compiler_params=_cparams(2),
    )(x, o, z, mod3, o_g, w_out)


def dn_out_bwd(dres, y, o, z, mod3, o_g, w_out):
    B, T, D = dres.shape
    _, H, _, Dh = o.shape
    W = H * Dh
    tm = _tile(T, 512)

    def body(dres_ref, y_ref, o_ref, z_ref, mod_ref, og_ref, w_ref, do_ref, dz_ref, ogb_ref, dy_ref, dgate_ref, dog_ref):
        t = pl.program_id(1)

        @pl.when(t == 0)
        def _():
            dgate_ref[...] = jnp.zeros_like(dgate_ref)
            dog_ref[...] = jnp.zeros_like(dog_ref)

        dres = dres_ref[...]
        dy = ((1.0 + mod_ref[2:3, :]) * dres).astype(BF16)
        dy_ref[...] = dy
        dgate_ref[...] += _sum0(dres * y_ref[...])
        dog = _mm_nt(dy, w_ref[...])
        og = og_ref[...]
        for h in range(H):
            ov = o_ref[h]
            xn, r = _head_norm(ov, og)
            zz = z_ref[:, h * Dh:(h + 1) * Dh]
            sg = _sigmoid(zz)
            sz = zz * sg
            d = dog[:, h * Dh:(h + 1) * Dh]
            ogb_ref[:, h * Dh:(h + 1) * Dh] = (xn * og * sz).astype(BF16)
            dz_ref[:, h * Dh:(h + 1) * Dh] = d * (xn * og) * _dsilu(zz, sg)
            don = d * sz
            dog_ref[...] += _sum0(don * xn)
            dxn = don * og
            do_ref[h] = r * (dxn - xn * jnp.mean(dxn * xn, axis=-1, keepdims=True))

    tok = pl.BlockSpec((None, tm, D), lambda b, t: (b, t, 0))
    tokw = pl.BlockSpec((None, tm, W), lambda b, t: (b, t, 0))
    hm = pl.BlockSpec((None, H, tm, Dh), lambda b, t: (b, 0, t, 0))
    return pl.pallas_call(
        body, name="dn_out_bwd", grid=(B, T // tm),
        in_specs=[tok, tok, hm, tokw, pl.BlockSpec((None, 3, D), lambda b, t: (b, 0, 0)),
                  pl.BlockSpec((1, Dh), lambda b, t: (0, 0)), pl.BlockSpec((W, D), lambda b, t: (0, 0))],
        out_specs=[hm, tokw, tokw, tok, pl.BlockSpec((None, 1, D), lambda b, t: (b, 0, 0)),
                   pl.BlockSpec((None, 1, Dh), lambda b, t: (b, 0, 0))],
        out_shape=[jax.ShapeDtypeStruct((B, H, T, Dh), F32), jax.ShapeDtypeStruct((B, T, W), F32),
                   jax.ShapeDtypeStruct((B, T, W), BF16), jax.ShapeDtypeStruct((B, T, D), BF16),
                   jax.ShapeDtypeStruct((B, 1, D), F32), jax.ShapeDtypeStruct((B, 1, Dh), F32)],
        compiler_params=_cparams(2),
    )(dres, y, o, z, mod3, o_g, w_out)


def dn_conv_bwd(dq, dk, dv, dgb, dbb, pre, ab, w_sconv, alog_row, dt_row):
    B, H, T, Dh = dq.shape
    W = H * Dh
    W3 = 3 * W
    K = w_sconv.shape[0]
    tm = _tile(T, 256)

    def body(dq_ref, dk_ref, dv_ref, dgb_ref, dbb_ref, pre_ref, halo_ref, ab_ref, w_ref, alog_ref, dt_ref,
             dc_ref, dab_ref, small_ref, ext_s):
        t = pl.program_id(1)

        @pl.when(t == 0)
        def _():
            small_ref[...] = jnp.zeros_like(small_ref)

        ext_s[0:SCONV_HALO, :] = jnp.where(t > 0, halo_ref[...], 0.0)
        ext_s[SCONV_HALO:, :] = pre_ref[...]
        cv = _sconv(ext_s, w_ref, tm, K)
        dsl = _dsilu(cv, _sigmoid(cv))
        ab = ab_ref[...]
        lane = lax.broadcasted_iota(jnp.int32, ab.shape, 1)
        dg_all = jnp.zeros_like(ab)
        db_all = jnp.zeros_like(ab)
        for h in range(H):
            dc_ref[:, h * Dh:(h + 1) * Dh] = dq_ref[h] * dsl[:, h * Dh:(h + 1) * Dh]
            dc_ref[:, W + h * Dh:W + (h + 1) * Dh] = dk_ref[h] * dsl[:, W + h * Dh:W + (h + 1) * Dh]
            dc_ref[:, 2 * W + h * Dh:2 * W + (h + 1) * Dh] = dv_ref[h] * dsl[:, 2 * W + h * Dh:2 * W + (h + 1) * Dh]
            dg_all = dg_all + jnp.where(lane == h, jnp.sum(dgb_ref[h], axis=1, keepdims=True), 0.0)
            db_all = db_all + jnp.where(lane == H + h, jnp.sum(dbb_ref[h], axis=1, keepdims=True), 0.0)
        xa = ab + dt_ref[...]
        ea = -jnp.exp(alog_ref[...])
        g_all = ea * _softplus(xa)
        da = dg_all * ea * _sigmoid(xa)
        beta = _sigmoid(ab)
        dab_ref[...] = da + db_all * beta * (1.0 - beta)
        small_ref[0:1, :] += _sum0(dg_all * g_all)
        small_ref[1:2, :] += _sum0(da)

    hm = pl.BlockSpec((None, H, tm, Dh), lambda b, t: (b, 0, t, 0))
    row = pl.BlockSpec((1, LANES), lambda b, t: (0, 0))
    return pl.pallas_call(
        body, name="dn_conv_bwd", grid=(B, T // tm),
        in_specs=[hm] * 5 + [pl.BlockSpec((None, tm, W3), lambda b, t: (b, t, 0)), _past_halo_spec(tm, SCONV_HALO, W3),
                             pl.BlockSpec((None, tm, LANES), lambda b, t: (b, t, 0)),
                             pl.BlockSpec((K, W3), lambda b, t: (0, 0)), row, row],
        out_specs=[pl.BlockSpec((None, tm, W3), lambda b, t: (b, t, 0)), pl.BlockSpec((None, tm, LANES), lambda b, t: (b, t, 0)),
                   pl.BlockSpec((None, 2, LANES), lambda b, t: (b, 0, 0))],
        out_shape=[jax.ShapeDtypeStruct((B, T, W3), F32), jax.ShapeDtypeStruct((B, T, LANES), F32),
                   jax.ShapeDtypeStruct((B, 2, LANES), F32)],
        scratch_shapes=[pltpu.VMEM((tm + SCONV_HALO, W3), F32)],
        compiler_params=_cparams(2),
    )(dq, dk, dv, dgb, dbb, pre, pre, ab, w_sconv, alog_row, dt_row)


def dn_proj_bwd(x, dres, dc, pre, dz, dab, mod3, g, w_main, w_ab, w_sconv):
    B, T, D = x.shape
    W3 = dc.shape[2]
    W = W3 // 3
    K = w_sconv.shape[0]
    tm = _tile(T, 256)
    nt = T // tm

    def body(x_ref, dres_ref, dc_ref, dch_ref, pre_ref, preh_ref, dz_ref, dab_ref, mod_ref, g_ref, wm_ref, wab_ref, ws_ref,
             dx_ref, h_ref, dproj_ref, dws_ref, dmod_ref, dg_ref, extp_s, extd_s):
        t = pl.program_id(1)

        @pl.when(t == 0)
        def _():
            dws_ref[...] = jnp.zeros_like(dws_ref)
            dmod_ref[...] = jnp.zeros_like(dmod_ref)
            dg_ref[...] = jnp.zeros_like(dg_ref)

        dc = dc_ref[...]
        extp_s[0:SCONV_HALO, :] = jnp.where(t > 0, preh_ref[...], 0.0)
        extp_s[SCONV_HALO:, :] = pre_ref[...]
        extd_s[0:tm, :] = dc
        extd_s[tm:, :] = jnp.where(t < nt - 1, dch_ref[...], 0.0)
        dpre = jnp.zeros((tm, W3), F32)
        for k in range(K):
            dpre = dpre + ws_ref[k:k + 1, :] * extd_s[pl.ds(K - 1 - k, tm), :]
            dws_ref[k:k + 1, :] += _sum0(dc * extp_s[pl.ds(SCONV_HALO - (K - 1) + k, tm), :])
        dpre = dpre.astype(BF16)
        dzb = dz_ref[...].astype(BF16)
        dproj_ref[:, 0:W3] = dpre
        dproj_ref[:, W3:] = dzb
        dh = _mm_nt(dab_ref[...], wab_ref[...]) + _mm_nt(dzb, wm_ref[:, W3:])
        for p in range(3):
            dh = dh + _mm_nt(dpre[:, p * W:(p + 1) * W], wm_ref[:, p * W:(p + 1) * W])
        xv = x_ref[...]
        h_ref[...] = _modnorm(xv, g_ref[...], mod_ref[1:2, :], mod_ref[0:1, :]).astype(BF16)
        dxn, dg, dscale, dshift = _modnorm_bwd(xv, g_ref[...], mod_ref[1:2, :], dh)
        dx_ref[...] = dres_ref[...] + dxn
        dmod_ref[0:1, :] += dshift
        dmod_ref[1:2, :] += dscale
        dg_ref[...] += dg

    tok = pl.BlockSpec((None, tm, D), lambda b, t: (b, t, 0))
    tok3 = pl.BlockSpec((None, tm, W3), lambda b, t: (b, t, 0))
    return pl.pallas_call(
        body, name="dn_proj_bwd", grid=(B, nt),
        in_specs=[tok, tok, tok3, _future_halo_spec(tm, SCONV_HALO, W3, T), tok3, _past_halo_spec(tm, SCONV_HALO, W3),
                  pl.BlockSpec((None, tm, W), lambda b, t: (b, t, 0)), pl.BlockSpec((None, tm, LANES), lambda b, t: (b, t, 0)),
                  pl.BlockSpec((None, 3, D), lambda b, t: (b, 0, 0)), pl.BlockSpec((1, D), lambda b, t: (0, 0)),
                  pl.BlockSpec((D, 4 * W), lambda b, t: (0, 0)), pl.BlockSpec((D, LANES), lambda b, t: (0, 0)),
                  pl.BlockSpec((K, W3), lambda b, t: (0, 0))],
        out_specs=[tok, tok, pl.BlockSpec((None, tm, 4 * W), lambda b, t: (b, t, 0)),
                   pl.BlockSpec((None, K, W3), lambda b, t: (b, 0, 0)), pl.BlockSpec((None, 3, D), lambda b, t: (b, 0, 0)),
                   pl.BlockSpec((None, 1, D), lambda b, t: (b, 0, 0))],
        out_shape=[jax.ShapeDtypeStruct((B, T, D), F32), jax.ShapeDtypeStruct((B, T, D), BF16),
                   jax.ShapeDtypeStruct((B, T, 4 * W), BF16), jax.ShapeDtypeStruct((B, K, W3), F32),
                   jax.ShapeDtypeStruct((B, 3, D), F32), jax.ShapeDtypeStruct((B, 1, D), F32)],
        scratch_shapes=[pltpu.VMEM((tm + SCONV_HALO, W3), F32), pltpu.VMEM((tm + SCONV_HALO, W3), F32)],
        compiler_params=_cparams(2),
    )(x, dres, dc, dc, pre, pre, dz, dab, mod3, g, w_main, w_ab, w_sconv)


def ada_fwd(c_all, w_ada, b_cols):
    L, D, Ca = w_ada.shape
    NB = c_all.shape[0]

    def body(c_ref, w_ref, b_ref, o_ref):
        cv = c_ref[...]
        o_ref[...] = _mm(cv * _sigmoid(cv), w_ref[...]) + b_ref[...]

    return pl.pallas_call(
        body, name="ada_fwd", grid=(L,),
        in_specs=[pl.BlockSpec((NB, D), lambda i: (0, 0)), pl.BlockSpec((None, D, Ca), lambda i: (i, 0, 0)),
                  pl.BlockSpec((None, 1, Ca), lambda i: (i, 0, 0))],
        out_specs=pl.BlockSpec((None, NB, Ca), lambda i: (i, 0, 0)),
        out_shape=jax.ShapeDtypeStruct((L, NB, Ca), F32),
        compiler_params=_cparams(1),
    )(c_all, w_ada, b_cols)


def ada_bwd(c_all, dmod_cols, dmod_all):
    L, NB, Ca = dmod_cols.shape
    D = c_all.shape[1]
    C9 = dmod_all.shape[2]

    def body(c_ref, dc_ref, da_ref, gw_ref, gb_ref):
        cv = c_ref[...]
        gw_ref[...] = _mm_tn(cv * _sigmoid(cv), dc_ref[...])
        gb_ref[...] = _sum0(da_ref[...])

    return pl.pallas_call(
        body, name="ada_bwd", grid=(L,),
        in_specs=[pl.BlockSpec((NB, D), lambda i: (0, 0)), pl.BlockSpec((None, NB, Ca), lambda i: (i, 0, 0)),
                  pl.BlockSpec((None, NB, C9), lambda i: (i, 0, 0))],
        out_specs=[pl.BlockSpec((None, D, Ca), lambda i: (i, 0, 0)), pl.BlockSpec((None, 1, C9), lambda i: (i, 0, 0))],
        out_shape=[jax.ShapeDtypeStruct((L, D, Ca), F32), jax.ShapeDtypeStruct((L, 1, C9), F32)],
        compiler_params=_cparams(1),
    )(c_all, dmod_cols, dmod_all)


def adamw(w, g, m, v, name, token=None):
    R, C = w.shape
    tr = _tile(R, max(8, (1 << 18) // C))
    if token is None:
        token = jnp.zeros((8, LANES), F32)

    def body(w_ref, g_ref, m_ref, v_ref, t_ref, d_ref, mo_ref, vo_ref):
        gv = g_ref[...] + t_ref[0:1, 0:1]
        mn = ADAM_B1 * m_ref[...] + (1.0 - ADAM_B1) * gv
        vn = ADAM_B2 * v_ref[...] + (1.0 - ADAM_B2) * (gv * gv)
        m_hat = mn / (1.0 - ADAM_B1 ** ADAM_STEP)
        v_hat = vn / (1.0 - ADAM_B2 ** ADAM_STEP)
        d_ref[...] = -ADAM_LR * (m_hat / (jnp.sqrt(v_hat) + ADAM_EPS) + ADAM_WD * w_ref[...])
        mo_ref[...] = mn
        vo_ref[...] = vn

    blk = pl.BlockSpec((tr, C), lambda i: (i, 0))
    return pl.pallas_call(
        body, name=name, grid=(R // tr,), in_specs=[blk] * 4 + [pl.BlockSpec((8, LANES), lambda i: (0, 0))],
        out_specs=[blk] * 3, out_shape=[jax.ShapeDtypeStruct((R, C), F32)] * 3, compiler_params=_cparams(1),
    )(w, g, m, v, token)


def sum_devices(a):
    n, R, C = a.shape

    def body(a_ref, o_ref):
        s = a_ref[0]
        for d in range(1, n):
            s = s + a_ref[d]
        o_ref[...] = s

    return pl.pallas_call(
        body, name="sum_devices", out_shape=jax.ShapeDtypeStruct((R, C), F32),
        compiler_params=pltpu.CompilerParams(vmem_limit_bytes=VMEM_LIMIT_V7X),
    )(a)


def _place():
    x, y, c = lax.axis_index("x"), lax.axis_index("y"), lax.axis_index("c")
    return x, y, c


def _other_chips(x, y):
    return [(2 * (1 - x) + y, 1 - x, y), (2 * x + (1 - y), x, 1 - y), (2 * (1 - x) + (1 - y), 1 - x, 1 - y)]


def allgather8(block):
    m_per, n = block.shape

    def body(x_ref, out_ref, send_sems, recv_sems, local_sem):
        x, y, c = _place()
        me, sibling = (x, y, c), (x, y, 1 - c)
        chips = [(1 - x, y), (x, 1 - y), (1 - x, 1 - y)]

        def rows(px, py, pc):
            return out_ref.at[pl.ds((4 * px + 2 * py + pc) * m_per, m_per), :]

        def copy(k, blk, to, src=None):
            return pltpu.make_async_remote_copy(
                src_ref=rows(*blk) if src is None else src, dst_ref=rows(*blk),
                send_sem=send_sems.at[k], recv_sem=recv_sems.at[k], device_id=to, device_id_type=MESH)

        mine = pltpu.make_async_copy(x_ref, rows(*me), local_sem)
        mine.start()
        first = [copy(0, me, sibling, src=x_ref)]
        first += [copy(1 + j, me, (*chip, c), src=x_ref) for j, chip in enumerate(chips)]
        for cp in first:
            cp.start()
        passed = [copy(4 + j, (*chip, c), sibling) for j, chip in enumerate(chips)]
        for j, chip in enumerate(chips):
            copy(1 + j, (*chip, c), me).wait_recv()
            passed[j].start()
        copy(0, sibling, me).wait_recv()
        for j, chip in enumerate(chips):
            copy(4 + j, (*chip, 1 - c), me).wait_recv()
        for cp in first + passed:
            cp.wait_send()
        mine.wait()

    return pl.pallas_call(
        body, name="allgather8", out_shape=jax.ShapeDtypeStruct((N_DEV * m_per, n), block.dtype),
        in_specs=[pl.BlockSpec(memory_space=pltpu.VMEM)], out_specs=pl.BlockSpec(memory_space=pltpu.VMEM),
        scratch_shapes=[pltpu.SemaphoreType.DMA((7,)), pltpu.SemaphoreType.DMA((7,)), pltpu.SemaphoreType.DMA],
        compiler_params=pltpu.CompilerParams(vmem_limit_bytes=VMEM_LIMIT_V7X),
    )(block)


def _half(ref, c, rh):
    return ref.at[pl.ds(pl.multiple_of(c * rh, 16), rh), :]


def pair_exchange(grads):
    K = len(grads)

    def body(*refs):
        ins, outs = refs[:K], refs[K:2 * K]
        send_sems, recv_sems = refs[2 * K:]
        x, y, c = _place()
        sibling = (x, y, 1 - c)
        copies = []
        for k in range(K):
            n, r, _ = ins[k].shape
            rh = r // 2
            cp = pltpu.make_async_remote_copy(
                src_ref=ins[k].at[:, pl.ds(pl.multiple_of((1 - c) * rh, 16), rh), :], dst_ref=outs[k],
                send_sem=send_sems.at[k], recv_sem=recv_sems.at[k], device_id=sibling, device_id_type=MESH)
            cp.start()
            copies.append(cp)
        for cp in copies:
            cp.wait_recv()
        for cp in copies:
            cp.wait_send()

    return pl.pallas_call(
        body, name="pair_exchange",
        out_shape=[jax.ShapeDtypeStruct((g.shape[0], g.shape[1] // 2, g.shape[2]), g.dtype) for g in grads],
        in_specs=[HBM_SPEC] * K, out_specs=[HBM_SPEC] * K,
        scratch_shapes=[pltpu.SemaphoreType.DMA((K,))] * 2,
    )(*grads)


def pair_add(grad, recv, c_idx):
    n, r, C = grad.shape
    rh = r // 2
    tr = _tile(rh, max(16, (1 << 19) // C), 16)
    grad = grad.reshape(n, 2, rh, C)

    def body(c_ref, g_ref, r_ref, o_ref):
        o_ref[...] = (g_ref[...].astype(F32) + r_ref[...].astype(F32)).astype(BF16)

    return pl.pallas_call(
        body, name="pair_add",
        grid_spec=pltpu.PrefetchScalarGridSpec(
            num_scalar_prefetch=1, grid=(n, rh // tr),
            in_specs=[pl.BlockSpec((None, None, tr, C), lambda d, i, c_ref: (d, c_ref[0], i, 0)),
                      pl.BlockSpec((None, tr, C), lambda d, i, c_ref: (d, i, 0))],
            out_specs=pl.BlockSpec((None, tr, C), lambda d, i, c_ref: (d, i, 0))),
        out_shape=jax.ShapeDtypeStruct((n, rh, C), BF16), compiler_params=_cparams(2),
    )(c_idx, grad, recv)


def chip_sum(parts, got, where, stack, slot):
    _, rh, C = parts.shape
    tr = _tile(rh, max(16, (1 << 19) // C), 16)
    nt = rh // tr

    def body(w_ref, p_ref, g_ref, stack_any, o_ref):
        s = p_ref[...].astype(F32)
        for r in range(3):
            s = s + g_ref[r].astype(F32)
        o_ref[...] = s

    return pl.pallas_call(
        body, name="chip_sum",
        grid_spec=pltpu.PrefetchScalarGridSpec(
            num_scalar_prefetch=1, grid=(nt,),
            in_specs=[pl.BlockSpec((None, tr, C), lambda i, w_ref: (w_ref[0], i, 0)),
                      pl.BlockSpec((3, tr, C), lambda i, w_ref: (0, i, 0)),
                      pl.BlockSpec(memory_space=pl.ANY)],
            out_specs=pl.BlockSpec((None, tr, C), lambda i, w_ref: (slot, w_ref[1] * nt + i, 0))),
        out_shape=jax.ShapeDtypeStruct(stack.shape, F32), input_output_aliases={3: 0},
        compiler_params=_cparams(1),
    )(where, parts, got, stack)


def pair_share(stacks, slots):
    K = len(stacks)
    jobs = [(k, s) for k in range(K) for s in slots[k]]

    def body(*refs):
        ins, outs = refs[:K], refs[K:2 * K]
        send_sems, recv_sems = refs[2 * K:]
        x, y, c = _place()
        sibling = (x, y, 1 - c)
        started = []
        for n, (k, s) in enumerate(jobs):
            rh = ins[k].shape[1] // 2
            cp = pltpu.make_async_remote_copy(
                src_ref=_half(ins[k].at[s], c, rh), dst_ref=_half(outs[k].at[s], c, rh), send_sem=send_sems.at[n],
                recv_sem=recv_sems.at[n], device_id=sibling, device_id_type=MESH)
            cp.start()
            started.append(cp)
        for n, (k, s) in enumerate(jobs):
            rh = ins[k].shape[1] // 2
            theirs = _half(outs[k].at[s], 1 - c, rh)
            pltpu.make_async_remote_copy(
                src_ref=theirs, dst_ref=theirs, send_sem=send_sems.at[n], recv_sem=recv_sems.at[n],
                device_id=sibling, device_id_type=MESH).wait_recv()
        for cp in started:
            cp.wait_send()

    return pl.pallas_call(
        body, name="pair_share",
        out_shape=[jax.ShapeDtypeStruct(s.shape, s.dtype) for s in stacks],
        in_specs=[HBM_SPEC] * K, out_specs=[HBM_SPEC] * K, input_output_aliases={k: k for k in range(K)},
        scratch_shapes=[pltpu.SemaphoreType.DMA((len(jobs),))] * 2,
    )(*stacks)


SEM_SPEC = pl.BlockSpec(memory_space=pltpu.SEMAPHORE)
ANY_SPEC = pl.BlockSpec(memory_space=pl.ANY)
DATAFLOW = pltpu.SideEffectType.DATAFLOW_SIDE_EFFECTING


def _in_hbm(a):
    return pltpu.with_memory_space_constraint(a, pltpu.HBM)


def _ici_copies(srcs, dsts, send_sems, recv_sems, src_slice, dst_slice):
    x, y, c = _place()
    out = []
    for k in range(len(srcs)):
        for r, (pchip, px, py) in enumerate(_other_chips(x, y)):
            out.append(pltpu.make_async_remote_copy(
                src_ref=src_slice(srcs[k], r, pchip), dst_ref=dst_slice(dsts[k], r, pchip),
                send_sem=send_sems.at[3 * k + r], recv_sem=recv_sems.at[3 * k + r], device_id=(px, py, c),
                device_id_type=MESH))
    return out


def _exchange_start(bufs, lands, src_slice, dst_slice, name, after=None):
    K = len(bufs)
    same = lands is None
    n_thru = K if same else 2 * K
    n_in = n_thru + (after is not None)

    def body(*refs):
        ins = refs[:n_thru]
        send_sems, recv_sems = refs[n_in], refs[n_in + 1]
        token = refs[-1]
        srcs = ins[:K]
        dsts = srcs if same else ins[K:]
        for cp in _ici_copies(srcs, dsts, send_sems, recv_sems, src_slice, dst_slice):
            cp.start()
        token[...] = jnp.zeros_like(token)

    thru = list(bufs) + ([] if same else list(lands))
    res = pl.pallas_call(
        body, name=name,
        out_shape=[pltpu.SemaphoreType.DMA((3 * K,)), pltpu.SemaphoreType.DMA((3 * K,))]
        + [pltpu.HBM(a.shape, a.dtype) for a in thru] + [jax.ShapeDtypeStruct((8, LANES), F32)],
        in_specs=[HBM_SPEC] * n_thru + [ANY_SPEC] * (after is not None),
        out_specs=[SEM_SPEC, SEM_SPEC] + [HBM_SPEC] * n_thru + [pl.BlockSpec(memory_space=pltpu.VMEM)],
        input_output_aliases={i: 2 + i for i in range(n_thru)},
        compiler_params=pltpu.CompilerParams(has_side_effects=DATAFLOW),
    )(*[_in_hbm(a) for a in thru], *([] if after is None else [after]))
    return res[0], res[1], res[2:2 + K], (res[2:2 + K] if same else res[2 + K:2 + 2 * K]), res[-1]


def _exchange_wait(send_sems, recv_sems, bufs, lands, after, src_slice, dst_slice, name):
    K = len(bufs)
    same = lands is None
    n_thru = K if same else 2 * K

    def body(*refs):
        ins = refs[:n_thru]
        ssem, rsem = refs[n_thru], refs[n_thru + 1]
        srcs = ins[:K]
        dsts = srcs if same else ins[K:]
        copies = _ici_copies(srcs, dsts, ssem, rsem, src_slice, dst_slice)
        for cp in copies:
            cp.wait_send()
        for cp in copies:
            cp.wait_recv()

    thru = list(bufs) + ([] if same else list(lands))
    res = pl.pallas_call(
        body, name=name,
        out_shape=[pltpu.HBM(a.shape, a.dtype) for a in thru],
        in_specs=[HBM_SPEC] * n_thru + [SEM_SPEC, SEM_SPEC, ANY_SPEC],
        out_specs=[HBM_SPEC] * n_thru,
        input_output_aliases={i: i for i in range(n_thru)},
        compiler_params=pltpu.CompilerParams(has_side_effects=DATAFLOW),
    )(*thru, send_sems, recv_sems, after)
    return res[:K], (res[:K] if same else res[K:])


def _own_half(ref, r, pchip):
    x, y, c = _place()
    return _half(ref.at[2 * x + y], c, ref.shape[1] // 2)


def _their_half(ref, r, pchip):
    _, _, c = _place()
    return _half(ref.at[pchip], c, ref.shape[1] // 2)


def gather_start(lands, name, after=None):
    return _exchange_start(lands, None, _own_half, _own_half, name, after)


def gather_wait(handle, after, name):
    ssem, rsem, lands, _, _ = handle
    return _exchange_wait(ssem, rsem, lands, None, after, _own_half, _their_half, name)[1]


def pair_forward(lands):
    K = len(lands)

    def body(*refs):
        ins, outs = refs[:K], refs[K:2 * K]
        send_sems, recv_sems = refs[2 * K:]
        x, y, c = _place()
        sibling = (x, y, 1 - c)
        started = []
        for k in range(K):
            rh = ins[k].shape[1] // 2
            for r, (pchip, _, _) in enumerate(_other_chips(x, y)):
                cp = pltpu.make_async_remote_copy(
                    src_ref=_half(ins[k].at[pchip], c, rh), dst_ref=_half(outs[k].at[pchip], c, rh),
                    send_sem=send_sems.at[k, r], recv_sem=recv_sems.at[k, r], device_id=sibling, device_id_type=MESH)
                cp.start()
                started.append(cp)
        for k in range(K):
            rh = ins[k].shape[1] // 2
            for r, (pchip, _, _) in enumerate(_other_chips(x, y)):
                theirs = _half(outs[k].at[pchip], 1 - c, rh)
                pltpu.make_async_remote_copy(
                    src_ref=theirs, dst_ref=theirs, send_sem=send_sems.at[k, r], recv_sem=recv_sems.at[k, r],
                    device_id=sibling, device_id_type=MESH).wait_recv()
        for cp in started:
            cp.wait_send()

    return pl.pallas_call(
        body, name="pair_forward",
        out_shape=[jax.ShapeDtypeStruct(s.shape, s.dtype) for s in lands],
        in_specs=[HBM_SPEC] * K, out_specs=[HBM_SPEC] * K, input_output_aliases={k: k for k in range(K)},
        scratch_shapes=[pltpu.SemaphoreType.DMA((K, 3))] * 2,
    )(*lands)


def _to_chip(ref, r, pchip):
    return ref.at[pchip]


def _from_relation(ref, r, pchip):
    return ref.at[r]


def reduce_start(grads, c_idx, name, after=None):
    recv = pair_exchange(grads)
    parts = [pair_add(g, r, c_idx) for g, r in zip(grads, recv)]
    lands = [lax.empty((3,) + p.shape[1:], p.dtype) for p in parts]
    return _exchange_start(parts, lands, _to_chip, _from_relation, name, after)


def reduce_finish(handle, after, where, name, stacks, targets):
    ssem, rsem, parts, lands, _ = handle
    parts, got = _exchange_wait(ssem, rsem, parts, lands, after, _to_chip, _from_relation, name)
    stacks = dict(stacks)
    for p, g, (key, slot) in zip(parts, got, targets):
        stacks[key] = chip_sum(p, g, where, stacks[key], slot)
    keys = list(dict.fromkeys(key for key, _ in targets))
    shared = pair_share([stacks[k] for k in keys], [[s for key, s in targets if key == k] for k in keys])
    stacks.update(zip(keys, shared))
    return stacks


def _pack(arrs):
    flat = jnp.concatenate([a.reshape(-1).astype(F32) for a in arrs])
    pad = (-flat.shape[0]) % (8 * LANES)
    return jnp.pad(flat, (0, pad)).reshape(-1, LANES)


def _unpack(flat, shapes):
    out, off = [], 0
    for s in shapes:
        n = 1
        for d in s:
            n *= d
        out.append(flat[off:off + n].reshape(s))
        off += n
    return out


def _adamw_any(w, g, m, v, name, token=None):
    shp = w.shape
    C = shp[-1]
    d, nm, nv = adamw(w.reshape(-1, C), g.reshape(-1, C), m.reshape(-1, C), v.reshape(-1, C), name, token)
    return d.reshape(shp), nm.reshape(shp), nv.reshape(shp)


def kernel(x, c, norm_g, w_ada, b_ada, w_ffn_in, w_ffn_out, cm_w_glu, cm_b_glu, cm_w_dw, cm_b_dw, cm_ln_g, cm_ln_b, cm_w_pw, cm_b_pw, dn_w_in, dn_w_sconv, dn_a_log, dn_dt_bias, dn_o_g, dn_w_out, final_g, loss_target, m_norm_g, m_w_ada, m_b_ada, m_w_ffn_in, m_w_ffn_out, m_cm_w_glu, m_cm_b_glu, m_cm_w_dw, m_cm_b_dw, m_cm_ln_g, m_cm_ln_b, m_cm_w_pw, m_cm_b_pw, m_dn_w_in, m_dn_w_sconv, m_dn_a_log, m_dn_dt_bias, m_dn_o_g, m_dn_w_out, m_final_g, v_norm_g, v_w_ada, v_b_ada, v_w_ffn_in, v_w_ffn_out, v_cm_w_glu, v_cm_b_glu, v_cm_w_dw, v_cm_b_dw, v_cm_ln_g, v_cm_ln_b, v_cm_w_pw, v_cm_b_pw, v_dn_w_in, v_dn_w_sconv, v_dn_a_log, v_dn_dt_bias, v_dn_o_g, v_dn_w_out, v_final_g):
    weights = dict(norm_g=norm_g, w_ada=w_ada, b_ada=b_ada, w_ffn_in=w_ffn_in, w_ffn_out=w_ffn_out, cm_w_glu=cm_w_glu,
                   cm_b_glu=cm_b_glu, cm_w_dw=cm_w_dw, cm_b_dw=cm_b_dw, cm_ln_g=cm_ln_g, cm_ln_b=cm_ln_b, cm_w_pw=cm_w_pw,
                   cm_b_pw=cm_b_pw, dn_w_in=dn_w_in, dn_w_sconv=dn_w_sconv, dn_a_log=dn_a_log, dn_dt_bias=dn_dt_bias,
                   dn_o_g=dn_o_g, dn_w_out=dn_w_out, final_g=final_g)
    mom_m = dict(norm_g=m_norm_g, w_ada=m_w_ada, b_ada=m_b_ada, w_ffn_in=m_w_ffn_in, w_ffn_out=m_w_ffn_out,
                 cm_w_glu=m_cm_w_glu, cm_b_glu=m_cm_b_glu, cm_w_dw=m_cm_w_dw, cm_b_dw=m_cm_b_dw, cm_ln_g=m_cm_ln_g,
                 cm_ln_b=m_cm_ln_b, cm_w_pw=m_cm_w_pw, cm_b_pw=m_cm_b_pw, dn_w_in=m_dn_w_in, dn_w_sconv=m_dn_w_sconv,
                 dn_a_log=m_dn_a_log, dn_dt_bias=m_dn_dt_bias, dn_o_g=m_dn_o_g, dn_w_out=m_dn_w_out, final_g=m_final_g)
    mom_v = dict(norm_g=v_norm_g, w_ada=v_w_ada, b_ada=v_b_ada, w_ffn_in=v_w_ffn_in, w_ffn_out=v_w_ffn_out,
                 cm_w_glu=v_cm_w_glu, cm_b_glu=v_cm_b_glu, cm_w_dw=v_cm_w_dw, cm_b_dw=v_cm_b_dw, cm_ln_g=v_cm_ln_g,
                 cm_ln_b=v_cm_ln_b, cm_w_pw=v_cm_w_pw, cm_b_pw=v_cm_b_pw, dn_w_in=v_dn_w_in, dn_w_sconv=v_dn_w_sconv,
                 dn_a_log=v_dn_a_log, dn_dt_bias=v_dn_dt_bias, dn_o_g=v_dn_o_g, dn_w_out=v_dn_w_out, final_g=v_final_g)
    names = list(weights)

    BL, T, D = x.shape
    L = norm_g.shape[0]
    NB = BL * N_DEV
    Ca = w_ada.shape[2]
    C9 = b_ada.shape[1]
    H = dn_a_log.shape[1]
    Dh = dn_o_g.shape[1]
    W = H * Dh
    KC = cm_w_dw.shape[1]
    KS = dn_w_sconv.shape[1]
    n_cm, n_dn = cm_w_glu.shape[0], dn_w_in.shape[0]
    ax, ay, ac = lax.axis_index("x"), lax.axis_index("y"), lax.axis_index("c")
    chip = 2 * ax + ay
    dev = 2 * chip + ac
    c_idx = ac.astype(jnp.int32).reshape(1)
    where = jnp.stack([chip, ac]).astype(jnp.int32)

    def layer_shards(i):
        sh = [w_ffn_in[i, 0], w_ffn_in[i, 1], w_ffn_out[i, 0], w_ffn_out[i, 1]]
        if i % 2 == 0:
            sh += [cm_w_glu[i // 2], cm_w_pw[i // 2]]
        else:
            sh += [dn_w_in[i // 2], dn_w_out[i // 2]]
        return [lax.dynamic_update_slice(lax.empty((N_CHIPS,) + s.shape, BF16), s.astype(BF16)[None], (chip, 0, 0))
                for s in sh]

    lands = [layer_shards(i) for i in range(L)]
    wts = [None] * L

    small_in = [c, norm_g, cm_w_dw, dn_w_sconv]
    gathered = allgather8(_pack(small_in))
    first = gather_start([lands[0][0], lands[0][2]], "gather_start_0a", gathered)
    gathered = gathered.reshape(N_DEV, -1)
    per_dev = [_unpack(gathered[d], [a.shape for a in small_in]) for d in range(N_DEV)]
    c_all = jnp.concatenate([p[0] for p in per_dev], axis=0)
    norm_g_full = jnp.concatenate([per_dev[2 * s][1] for s in range(N_CHIPS)], axis=-1)
    w_dw_full = jnp.concatenate([per_dev[2 * s][2] for s in range(N_CHIPS)], axis=-1)
    w_sconv_full = jnp.concatenate([per_dev[2 * s][3] for s in range(N_CHIPS)], axis=-1)

    b_cols = lax.dynamic_slice_in_dim(b_ada, chip * Ca, Ca, axis=1).reshape(L, 1, Ca)
    mod_part = ada_fwd(c_all, w_ada, b_cols)
    mod_g = allgather8((mod_part + first[4][0, 0]).reshape(-1, LANES))
    rest = gather_start([lands[0][k] for k in (1, 3, 4, 5)], "gather_start_0b", mod_g)
    mod_g = mod_g.reshape(N_DEV, L, NB, Ca)
    mod_all = jnp.concatenate([mod_g[2 * s] for s in range(N_CHIPS)], axis=-1)
    mod = lax.dynamic_slice_in_dim(mod_all, dev * BL, BL, axis=1).reshape(L, BL, 9, D)

    def dn_weights(i):
        full = jnp.transpose(wts[i][4], (1, 0, 2)).reshape(D, -1)
        return full[:, :4 * W], jnp.pad(full[:, 4 * W:], ((0, 0), (0, LANES - 2 * H)))

    def row128(v):
        return jnp.pad(v.reshape(1, -1), ((0, 0), (0, LANES - v.shape[-1])))

    def pad_taps(w):
        return jnp.pad(w, ((0, 1), (0, 0)))

    saved = []
    xs = x
    after = mod
    for i in range(L):
        tok = 0.0
        if i == 0:
            wl = wts[0] = [None] * 6
            wl[0], wl[2] = pair_forward(gather_wait(first, after, "gather_wait_0a"))
        else:
            wl = wts[i] = pair_forward(gather_wait(handle, after, "gather_wait_%d" % i))
            if i + 1 < L:
                handle = gather_start(lands[i + 1], "gather_start_%d" % (i + 1), wl[0])
                tok = handle[4][0, 0]
        sv = {}
        m3 = [mod[i, :, 3 * j:3 * j + 3] + tok for j in range(3)]
        gs = [norm_g_full[i, j].reshape(1, D) for j in range(3)]
        sv["x0"] = xs
        xs, sv["y0"], sv["h0"], sv["gu0"] = ffn_fwd(xs, m3[0], gs[0], wl[0], wl[2])
        sv["x1"] = xs
        if i == 0:
            wl[1], wl[3], wl[4], wl[5] = pair_forward(gather_wait(rest, xs, "gather_wait_0b"))
            handle = gather_start(lands[1], "gather_start_1", wl[1])
            m3 = [m + handle[4][0, 0] for m in m3]
        if i % 2 == 0:
            a = i // 2
            sv["u"] = conv_glu_fwd(xs, m3[1], gs[1], wl[4], cm_b_glu[a].reshape(1, -1))
            xs, sv["y1"], sv["u2"] = conv_out_fwd(
                xs, sv["u"], m3[1], pad_taps(w_dw_full[a]), cm_b_dw[a].reshape(1, D), cm_ln_g[a].reshape(1, D),
                cm_ln_b[a].reshape(1, D), wl[5].reshape(D, D), cm_b_pw[a].reshape(1, D))
        else:
            a = i // 2
            w_main, w_ab = dn_weights(i)
            sv["pre"], sv["z"], sv["ab"] = dn_proj_fwd(xs, m3[1], gs[1], w_main, w_ab)
            qkvgb = dn_conv_fwd(sv["pre"], sv["ab"], w_sconv_full[a], row128(dn_a_log[a]), row128(dn_dt_bias[a]), H)
            sv["qkvgb"] = qkvgb
            sv["o"], sv["sp"], sv["inv"] = dn_chunk_fwd(*qkvgb)
            xs, sv["y1"] = dn_out_fwd(xs, sv["o"], sv["z"], m3[1], dn_o_g[a].reshape(1, Dh), wl[5].reshape(W, D))
        sv["x2"] = xs
        xs, sv["y2"], sv["h2"], sv["gu2"] = ffn_fwd(xs, m3[2], gs[2], wl[1], wl[3])
        saved.append(sv)
        after = xs

    dx, d_final_g, loss_part = final_loss(xs, final_g.reshape(1, D), loss_target)

    g_small = {n: None for n in names}
    d_norm_g = [[None] * 3 for _ in range(L)]
    dmod = [[None] * 3 for _ in range(L)]
    g_cm = {k: [None] * n_cm for k in ("b_glu", "w_dw", "b_dw", "ln_g", "ln_b", "b_pw")}
    g_dn = {k: [None] * n_dn for k in ("w_sconv", "a_log", "dt_bias", "o_g")}
    big_names = ("w_ffn_in", "w_ffn_out", "cm_w_glu", "cm_w_pw", "dn_w_in", "dn_w_out")
    stacks = {n: lax.empty((weights[n].size // (weights[n].shape[-2] * weights[n].shape[-1]),) + weights[n].shape[-2:], F32)
              for n in big_names}

    def targets(i, which):
        mix = ("cm_w_glu", "cm_w_pw") if i % 2 == 0 else ("dn_w_in", "dn_w_out")
        full = [("w_ffn_in", 2 * i), ("w_ffn_in", 2 * i + 1), ("w_ffn_out", 2 * i), ("w_ffn_out", 2 * i + 1),
                (mix[0], i // 2), (mix[1], i // 2)]
        return [full[k] for k in which]

    def ffn_back(i, j, slot, dx, tok=0.0):
        wl, sv = wts[i], saved[i]
        m3 = mod[i, :, 3 * j:3 * j + 3] + tok
        g = norm_g_full[i, j].reshape(1, D)
        gu = sv["gu%d" % j]
        ab_, dgu, dyb, dh0, dgate = ffn_bwd_part(0, dx, gu, m3, wl[slot], wl[2 + slot], y=sv["y%d" % j])
        dx, ab_, dgu, dm, dg = ffn_bwd_part(1, dx, gu, m3, wl[slot], wl[2 + slot], first=(ab_, dgu, dyb, dh0),
                                            x=sv["x%d" % j], g=g)
        dm = dm.at[:, 2:3, :].set(dgate)
        hb = sv["h%d" % j]
        dmod[i][j] = dm
        d_norm_g[i][j] = jnp.sum(dg, axis=(0, 1))
        Fc = wl[slot].shape[2]
        dw_in = matmul_tn(hb.reshape(-1, D), dgu.reshape(2, BL * T, 2 * Fc), Fc, "dw_ffn_in")
        dw_out = matmul_tn(ab_.reshape(-1, 2 * Fc), dyb.reshape(1, -1, D), D, "dw_ffn_out")
        return dx, dw_in, dw_out.reshape(N_CHIPS, -1, D)

    pending, tok = None, 0.0
    for i in reversed(range(L)):
        wl, sv = wts[i], saved[i]
        a = i // 2
        dx, dw_in1, dw_out1 = ffn_back(i, 2, 1, dx, tok)
        m3 = mod[i, :, 3:6]
        g = norm_g_full[i, 1].reshape(1, D)
        if i % 2 == 0:
            w_pw = wl[5].reshape(D, D)
            wdw = pad_taps(w_dw_full[a])
            du2, u3b, dyb, dgate, vec = conv_out_bwd(dx, sv["y1"], sv["u2"], m3, cm_ln_g[a].reshape(1, D),
                                                     cm_ln_b[a].reshape(1, D), w_pw)
            dx, hb, dab, dwdw, dbglu, dm, dg = conv_glu_bwd(sv["x1"], dx, du2, sv["u"], m3, g, wl[4],
                                                            cm_b_glu[a].reshape(1, -1), wdw)
            dm = dm.at[:, 2:3, :].set(dgate)
            vec = jnp.sum(vec, axis=0)
            g_cm["b_pw"][a], g_cm["ln_g"][a], g_cm["ln_b"][a], g_cm["b_dw"][a] = vec[0], vec[1], vec[2], vec[3]
            g_cm["w_dw"][a] = jnp.sum(dwdw, axis=0)[:KC]
            g_cm["b_glu"][a] = jnp.sum(dbglu, axis=(0, 1))
            dw_a = matmul_tn(hb.reshape(-1, D), dab.reshape(1, -1, 2 * D), D // 2, "dw_glu")
            dw_b = matmul_tn(u3b.reshape(-1, D), dyb.reshape(1, -1, D), D, "dw_sq").reshape(N_CHIPS, -1, D)
        else:
            w_main, w_ab = dn_weights(i)
            w_out = wl[5].reshape(W, D)
            do, dz, ogb, dyb, dgate, dog = dn_out_bwd(dx, sv["y1"], sv["o"], sv["z"], m3, dn_o_g[a].reshape(1, Dh), w_out)
            dq, dk, dv, dgb, dbb = dn_chunk_bwd(*sv["qkvgb"], sv["sp"], sv["inv"], do)
            dc, dab, small = dn_conv_bwd(dq, dk, dv, dgb, dbb, sv["pre"], sv["ab"], w_sconv_full[a],
                                         row128(dn_a_log[a]), row128(dn_dt_bias[a]))
            dx, hb, dproj, dws, dm, dg = dn_proj_bwd(sv["x1"], dx, dc, sv["pre"], dz, dab, m3, g, w_main, w_ab,
                                                     w_sconv_full[a])
            dm = dm.at[:, 2:3, :].set(dgate)
            small = jnp.sum(small, axis=0)
            g_dn["a_log"][a], g_dn["dt_bias"][a] = small[0, :H], small[1, :H]
            g_dn["o_g"][a] = jnp.sum(dog, axis=(0, 1))
            g_dn["w_sconv"][a] = jnp.sum(dws, axis=0)
            dw_main = matmul_tn(hb.reshape(-1, D), dproj.reshape(1, -1, 4 * W), W, "dw_dn_main")
            dw_ab = matmul_tn(hb.reshape(-1, D), dab.reshape(1, -1, LANES), LANES, "dw_dn_ab")
            full = jnp.concatenate([jnp.transpose(dw_main, (1, 0, 2)).reshape(D, 4 * W), dw_ab[0][:, :2 * H]], axis=1)
            dw_a = jnp.transpose(full.reshape(D, N_CHIPS, -1), (1, 0, 2))
            dw_b = matmul_tn(ogb.reshape(-1, W), dyb.reshape(1, -1, D), D, "dw_sq").reshape(N_CHIPS, -1, D)
        dmod[i][1] = dm
        d_norm_g[i][1] = jnp.sum(dg, axis=(0, 1))
        if i > 0:
            dx, dw_in0, dw_out0 = ffn_back(i, 0, 0, dx)
            started = reduce_start([dw_in0, dw_in1, dw_out0, dw_out1, dw_a, dw_b], c_idx, "reduce_start_%d" % i)
            if pending is not None:
                stacks = reduce_finish(pending[0], dx, where, "reduce_wait_%d" % pending[1], stacks,
                                       targets(pending[1], range(6)))
            pending, tok = (started, i), started[4][0, 0]
        else:
            part_a = reduce_start([dw_in1, dw_out1, dw_a, dw_b], c_idx, "reduce_start_0a")
            if pending is not None:
                stacks = reduce_finish(pending[0], dx, where, "reduce_wait_%d" % pending[1], stacks,
                                       targets(pending[1], range(6)))
            dx, dw_in0, dw_out0 = ffn_back(0, 0, 0, dx, part_a[4][0, 0])
            stacks = reduce_finish(part_a, dx, where, "reduce_wait_0a", stacks, targets(0, (1, 3, 4, 5)))

    part = dict(
        norm_g=jnp.stack([jnp.stack(r) for r in d_norm_g]),
        cm_b_glu=jnp.stack(g_cm["b_glu"]), cm_w_dw=jnp.stack(g_cm["w_dw"]), cm_b_dw=jnp.stack(g_cm["b_dw"]),
        cm_ln_g=jnp.stack(g_cm["ln_g"]), cm_ln_b=jnp.stack(g_cm["ln_b"]), cm_b_pw=jnp.stack(g_cm["b_pw"]),
        dn_w_sconv=jnp.stack(g_dn["w_sconv"]), dn_a_log=jnp.stack(g_dn["a_log"]), dn_dt_bias=jnp.stack(g_dn["dt_bias"]),
        dn_o_g=jnp.stack(g_dn["o_g"]), final_g=jnp.sum(d_final_g, axis=(0, 1)),
        loss=jnp.sum(loss_part[:, 0, 0]).reshape(1))
    dmod_loc = jnp.stack([jnp.concatenate(r, axis=1) for r in dmod]).reshape(L, BL, C9)
    keys = list(part)
    packed = _pack([part[k] for k in keys] + [dmod_loc])
    R = packed.shape[0]
    gathered = allgather8(packed).reshape(N_DEV, R, LANES)
    summed = _unpack(sum_devices(gathered).reshape(-1), [part[k].shape for k in keys])
    tot = dict(zip(keys, summed))
    n_small = sum(int(part[k].size) for k in keys)
    dmod_all = gathered.reshape(N_DEV, -1)[:, n_small:n_small + L * BL * C9].reshape(N_DEV, L, BL, C9)
    dmod_all = jnp.transpose(dmod_all, (1, 0, 2, 3)).reshape(L, NB, C9)
    dmod_cols = lax.dynamic_slice_in_dim(dmod_all, chip * Ca, Ca, axis=2)
    g_w_ada, g_b_ada = ada_bwd(c_all, dmod_cols, dmod_all)
    delta, new_m, new_v = {}, {}, {}
    part_b = reduce_start([dw_in0, dw_out0], c_idx, "reduce_start_0b", g_w_ada)
    delta["w_ada"], new_m["w_ada"], new_v["w_ada"] = _adamw_any(w_ada, g_w_ada, m_w_ada, v_w_ada, "adamw_w_ada",
                                                                 part_b[4])
    stacks = reduce_finish(part_b, new_v["w_ada"], where, "reduce_wait_0b", stacks, targets(0, (0, 2)))

    def my_cols(full):
        n = full.shape[-1] // N_CHIPS
        return lax.dynamic_slice_in_dim(full, chip * n, n, axis=full.ndim - 1)

    grads = dict(
        norm_g=my_cols(tot["norm_g"]), w_ada=g_w_ada, b_ada=g_b_ada.reshape(L, C9),
        cm_b_glu=tot["cm_b_glu"], cm_w_dw=my_cols(tot["cm_w_dw"]), cm_b_dw=tot["cm_b_dw"], cm_ln_g=tot["cm_ln_g"],
        cm_ln_b=tot["cm_ln_b"], cm_b_pw=tot["cm_b_pw"], dn_w_sconv=my_cols(tot["dn_w_sconv"]),
        dn_a_log=tot["dn_a_log"], dn_dt_bias=tot["dn_dt_bias"], dn_o_g=tot["dn_o_g"], final_g=tot["final_g"],
        **{n: stacks[n].reshape(weights[n].shape) for n in big_names})

    large = ("w_ada", "w_ffn_in", "w_ffn_out", "cm_w_glu", "cm_w_pw", "dn_w_in", "dn_w_out")
    for n in large[1:]:
        delta[n], new_m[n], new_v[n] = _adamw_any(weights[n], grads[n], mom_m[n], mom_v[n], "adamw_" + n)
    rest = [n for n in names if n not in large]
    shapes = [weights[n].shape for n in rest]
    pd, pm, pv = adamw(_pack([weights[n] for n in rest]), _pack([grads[n] for n in rest]),
                       _pack([mom_m[n] for n in rest]), _pack([mom_v[n] for n in rest]), "adamw_small")
    for n, d_, m_, v_ in zip(rest, _unpack(pd.reshape(-1), shapes), _unpack(pm.reshape(-1), shapes),
                             _unpack(pv.reshape(-1), shapes)):
        delta[n], new_m[n], new_v[n] = d_, m_, v_

    return (tot["loss"].reshape(()), dx, *[grads[n] for n in names], *[delta[n] for n in names],
            *[new_m[n] for n in names], *[new_v[n] for n in names])
```

```python
import functools

import jax
import jax.numpy as jnp
from jax import lax
from jax.experimental import pallas as pl
from jax.experimental.pallas import tpu as pltpu

F32 = jnp.float32
BF16 = jnp.bfloat16
EPS = 1e-6
CHUNK = 64
CHUNKS_PER_STEP = 4
N_CHIPS = 4
N_DEV = 8
LANES = 128
SUBLANES = 8
CONV_HALO = 32
SCONV_HALO = 8
VMEM_LIMIT_V7X = 60 * 1024 * 1024
DW_VMEM_BUDGET = 40 * 1024 * 1024
HI = lax.Precision.HIGHEST
MESH = pl.DeviceIdType.MESH
HBM_SPEC = pl.BlockSpec(memory_space=pltpu.HBM)

ADAM_LR, ADAM_B1, ADAM_B2, ADAM_EPS, ADAM_WD, ADAM_STEP = 0.001, 0.9, 0.999, 1e-08, 0.01, 10


def _cparams(n_axes):
    return pltpu.CompilerParams(dimension_semantics=("arbitrary",) * n_axes, vmem_limit_bytes=VMEM_LIMIT_V7X)


def _tile(n, pref, mult=8):
    for t in range(min(n, pref) // mult * mult, 0, -mult):
        if n % t == 0:
            return t
    return n


def _mm(a, b):
    return lax.dot_general(a.astype(BF16), b.astype(BF16), (((1,), (0,)), ((), ())), preferred_element_type=F32)


def _mm_nt(a, b):
    return lax.dot_general(a.astype(BF16), b.astype(BF16), (((1,), (1,)), ((), ())), preferred_element_type=F32)


def _mm_tn(a, b):
    return lax.dot_general(a.astype(BF16), b.astype(BF16), (((0,), (0,)), ((), ())), preferred_element_type=F32)


def _sigmoid(x):
    return jax.nn.sigmoid(x)


def _dsilu(x, s):
    return s * (1.0 + x * (1.0 - s))


def _softplus(x):
    return jnp.maximum(x, 0.0) + jnp.log(1.0 + jnp.exp(-jnp.abs(x)))


def _modnorm(x, g, scale, shift):
    r = lax.rsqrt(jnp.mean(x * x, axis=-1, keepdims=True) + EPS)
    return (x * r) * g * (1.0 + scale) + shift


def _modnorm_bwd(x, g, scale, dh):
    r = lax.rsqrt(jnp.mean(x * x, axis=-1, keepdims=True) + EPS)
    xn = x * r
    dshift = jnp.sum(dh, axis=0, keepdims=True)
    dscale = jnp.sum(dh * (xn * g), axis=0, keepdims=True)
    dhn = dh * (1.0 + scale)
    dg = jnp.sum(dhn * xn, axis=0, keepdims=True)
    dxn = dhn * g
    dx = r * (dxn - xn * jnp.mean(dxn * xn, axis=-1, keepdims=True))
    return dx, dg, dscale, dshift


def _sum0(a):
    return jnp.sum(a, axis=0, keepdims=True)


def ffn_fwd(x, mod3, g, w_in, w_out):
    B, T, D = x.shape
    Fc = w_in.shape[2]
    w_in = w_in.reshape(2, 2, D, Fc)
    w_out = w_out.reshape(2, Fc, D)
    tm = _tile(T, 512)

    def half(h, wi_ref, wo_ref, gu_ref):
        gt = _mm(h, wi_ref[0])
        up = _mm(h, wi_ref[1])
        gu_ref[0] = gt.astype(BF16)
        gu_ref[1] = up.astype(BF16)
        return _mm(gt * _sigmoid(gt) * up, wo_ref[...])

    def body_a(x_ref, mod_ref, g_ref, wi_ref, wo_ref, h_ref, gu_ref, y0_ref):
        h = _modnorm(x_ref[...], g_ref[...], mod_ref[1:2, :], mod_ref[0:1, :]).astype(BF16)
        h_ref[...] = h
        y0_ref[...] = half(h, wi_ref, wo_ref, gu_ref)

    def body_b(x_ref, h_ref, y0_ref, mod_ref, wi_ref, wo_ref, gu_any, xo_ref, y_ref, gu_ref):
        y = y0_ref[...] + half(h_ref[...], wi_ref, wo_ref, gu_ref)
        y_ref[...] = y
        xo_ref[...] = x_ref[...] + 0.5 * (1.0 + mod_ref[2:3, :]) * y

    tok = pl.BlockSpec((None, tm, D), lambda b, t: (b, t, 0))
    per_b3 = pl.BlockSpec((None, 3, D), lambda b, t: (b, 0, 0))
    gu_shape = jax.ShapeDtypeStruct((2, B, T, 2 * Fc), BF16)

    def w_specs(part):
        return [pl.BlockSpec((2, None, D, Fc), lambda b, t: (0, part, 0, 0)),
                pl.BlockSpec((None, Fc, D), lambda b, t: (part, 0, 0))]

    def gu_spec(part):
        return pl.BlockSpec((2, None, tm, Fc), lambda b, t: (0, b, t, part))

    h, gu, y0 = pl.pallas_call(
        body_a, name="ffn_fwd_a", grid=(B, T // tm),
        in_specs=[tok, per_b3, pl.BlockSpec((1, D), lambda b, t: (0, 0))] + w_specs(0),
        out_specs=[tok, gu_spec(0), tok],
        out_shape=[jax.ShapeDtypeStruct((B, T, D), BF16), gu_shape, jax.ShapeDtypeStruct((B, T, D), F32)],
        compiler_params=_cparams(2),
    )(x, mod3, g, w_in, w_out)
    x_new, y, gu = pl.pallas_call(
        body_b, name="ffn_fwd_b", grid=(B, T // tm),
        in_specs=[tok, tok, tok, per_b3] + w_specs(1) + [pl.BlockSpec(memory_space=pl.ANY)],
        out_specs=[tok, tok, gu_spec(1)],
        out_shape=[jax.ShapeDtypeStruct((B, T, D), F32)] * 2 + [gu_shape],
        input_output_aliases={6: 2},
        compiler_params=_cparams(2),
    )(x, h, y0, mod3, w_in, w_out, gu)
    return x_new, y, h, gu


def ffn_bwd_part(part, dres, gu, mod3, w_in, w_out, first=None, y=None, x=None, g=None):
    B, T, D = dres.shape
    Fc = w_in.shape[2]
    F = 2 * Fc
    w_in = w_in.reshape(2, 2, D, Fc)
    w_out = w_out.reshape(2, Fc, D)
    tm = _tile(T, 256)

    def half(dy, gu_ref, wi_ref, wo_ref, a_ref, dgu_ref):
        gt = gu_ref[0].astype(F32)
        up = gu_ref[1].astype(F32)
        sg = _sigmoid(gt)
        silu = gt * sg
        a_ref[...] = (silu * up).astype(BF16)
        da = _mm_nt(dy, wo_ref[...])
        dup = (da * silu).astype(BF16)
        dgt = (da * up * _dsilu(gt, sg)).astype(BF16)
        dgu_ref[0] = dgt
        dgu_ref[1] = dup
        return _mm_nt(dgt, wi_ref[0]) + _mm_nt(dup, wi_ref[1])

    tok = pl.BlockSpec((None, tm, D), lambda b, t: (b, t, 0))
    per_b3 = pl.BlockSpec((None, 3, D), lambda b, t: (b, 0, 0))
    per_b1 = pl.BlockSpec((None, 1, D), lambda b, t: (b, 0, 0))
    gu_spec = pl.BlockSpec((2, None, tm, Fc), lambda b, t: (0, b, t, part))
    a_spec = pl.BlockSpec((None, tm, Fc), lambda b, t: (b, t, part))
    wi_spec = pl.BlockSpec((2, None, D, Fc), lambda b, t: (0, part, 0, 0))
    wo_spec = pl.BlockSpec((None, Fc, D), lambda b, t: (part, 0, 0))
    a_shape = jax.ShapeDtypeStruct((B, T, F), BF16)
    dgu_shape = jax.ShapeDtypeStruct((2, B, T, F), BF16)

    if part == 0:
        def body(dres_ref, y_ref, gu_ref, mod_ref, wi_ref, wo_ref, a_ref, dgu_ref, dy_ref, dh_ref, dgate_ref):
            @pl.when(pl.program_id(1) == 0)
            def _():
                dgate_ref[...] = jnp.zeros_like(dgate_ref)

            dres = dres_ref[...]
            dy = (0.5 * (1.0 + mod_ref[2:3, :]) * dres).astype(BF16)
            dy_ref[...] = dy
            dgate_ref[...] += _sum0(dres * (0.5 * y_ref[...]))
            dh_ref[...] = half(dy, gu_ref, wi_ref, wo_ref, a_ref, dgu_ref)

        return pl.pallas_call(
            body, name="ffn_bwd_a", grid=(B, T // tm),
            in_specs=[tok, tok, gu_spec, per_b3, wi_spec, wo_spec],
            out_specs=[a_spec, gu_spec, tok, tok, per_b1],
            out_shape=[a_shape, dgu_shape, jax.ShapeDtypeStruct((B, T, D), BF16), jax.ShapeDtypeStruct((B, T, D), F32),
                       jax.ShapeDtypeStruct((B, 1, D), F32)],
            compiler_params=_cparams(2),
        )(dres, y, gu, mod3, w_in, w_out)

    a_full, dgu_full, dy, dh0 = first

    def body(x_ref, dres_ref, dy_ref, dh0_ref, gu_ref, mod_ref, g_ref, wi_ref, wo_ref, a_any, dgu_any,
             dx_ref, a_ref, dgu_ref, dmod_ref, dg_ref):
        @pl.when(pl.program_id(1) == 0)
        def _():
            dmod_ref[...] = jnp.zeros_like(dmod_ref)
            dg_ref[...] = jnp.zeros_like(dg_ref)

        dh = dh0_ref[...] + half(dy_ref[...], gu_ref, wi_ref, wo_ref, a_ref, dgu_ref)
        dxn, dg, dscale, dshift = _modnorm_bwd(x_ref[...], g_ref[...], mod_ref[1:2, :], dh)
        dx_ref[...] = dres_ref[...] + dxn
        dmod_ref[0:1, :] += dshift
        dmod_ref[1:2, :] += dscale
        dg_ref[...] += dg

    return pl.pallas_call(
        body, name="ffn_bwd_b", grid=(B, T // tm),
        in_specs=[tok, tok, tok, tok, gu_spec, per_b3, pl.BlockSpec((1, D), lambda b, t: (0, 0)), wi_spec, wo_spec,
                  ANY_SPEC, ANY_SPEC],
        out_specs=[tok, a_spec, gu_spec, per_b3, per_b1],
        out_shape=[jax.ShapeDtypeStruct((B, T, D), F32), a_shape, dgu_shape, jax.ShapeDtypeStruct((B, 3, D), F32),
                   jax.ShapeDtypeStruct((B, 1, D), F32)],
        input_output_aliases={9: 1, 10: 2},
        compiler_params=_cparams(2),
    )(x, dres, dy, dh0, gu, mod3, g, w_in, w_out, a_full, dgu_full)


def matmul_tn(xm, ym, bm, name):
    N, K = xm.shape
    GY, _, MY = ym.shape
    per = MY // bm
    nb = GY * per
    fixed = K * bm * (4 + 2 * 2)
    tn = _tile(N, max(512, (DW_VMEM_BUDGET - fixed) // (2 * 2 * (K + bm))), 256)

    def body(x_ref, y_ref, o_ref, acc_s):
        n = pl.program_id(1)

        @pl.when(n == 0)
        def _():
            acc_s[...] = jnp.zeros_like(acc_s)

        acc_s[...] += _mm_tn(x_ref[...], y_ref[...])

        @pl.when(n == N // tn - 1)
        def _():
            o_ref[...] = acc_s[...].astype(BF16)

    return pl.pallas_call(
        body, name=name, grid=(nb, N // tn),
        in_specs=[pl.BlockSpec((tn, K), lambda m, n: (n, 0)),
                  pl.BlockSpec((None, tn, bm), lambda m, n: (m // per, n, m % per))],
        out_specs=pl.BlockSpec((None, K, bm), lambda m, n: (m, 0, 0)),
        out_shape=jax.ShapeDtypeStruct((nb, K, bm), BF16),
        scratch_shapes=[pltpu.VMEM((K, bm), F32)],
        compiler_params=_cparams(2),
    )(xm, ym)


def final_loss(x, fg, target):
    B, T, D = x.shape
    tm = _tile(T, 512)

    def body(x_ref, g_ref, t_ref, dx_ref, dfg_ref, loss_ref):
        t = pl.program_id(1)

        @pl.when(t == 0)
        def _():
            dfg_ref[...] = jnp.zeros_like(dfg_ref)
            loss_ref[...] = jnp.zeros_like(loss_ref)

        xv = x_ref[...]
        g = g_ref[...]
        r = lax.rsqrt(jnp.mean(xv * xv, axis=-1, keepdims=True) + EPS)
        xn = xv * r
        err = xn * g - t_ref[...]
        tok_loss = jnp.mean(err * err, axis=-1, keepdims=True)
        loss_ref[...] += 0.5 * jnp.sum(tok_loss, axis=0, keepdims=True)
        dy = err * (1.0 / D)
        dfg_ref[...] += _sum0(dy * xn)
        dxn = dy * g
        dx_ref[...] = r * (dxn - xn * jnp.mean(dxn * xn, axis=-1, keepdims=True))

    tok = pl.BlockSpec((None, tm, D), lambda b, t: (b, t, 0))
    return pl.pallas_call(
        body, name="final_loss", grid=(B, T // tm),
        in_specs=[tok, pl.BlockSpec((1, D), lambda b, t: (0, 0)), tok],
        out_specs=[tok, pl.BlockSpec((None, 1, D), lambda b, t: (b, 0, 0)),
                   pl.BlockSpec((None, 1, LANES), lambda b, t: (b, 0, 0))],
        out_shape=[jax.ShapeDtypeStruct((B, T, D), F32), jax.ShapeDtypeStruct((B, 1, D), F32),
                   jax.ShapeDtypeStruct((B, 1, LANES), F32)],
        compiler_params=_cparams(2),
    )(x, fg, target)


def _past_halo_spec(tm, halo, width):
    return pl.BlockSpec((None, halo, width), lambda b, t: (b, jnp.maximum(t * (tm // halo) - 1, 0), 0))


def _future_halo_spec(tm, halo, width, T):
    return pl.BlockSpec((None, halo, width), lambda b, t: (b, jnp.minimum((t + 1) * (tm // halo), T // halo - 1), 0))


def _fill_shifted(ext_s):
    n = ext_s.shape[1]
    for b in range(1, SUBLANES):
        ext_s[b, 0:n - SUBLANES, :] = ext_s[0, pl.ds(b, n - SUBLANES), :]


def _shifted(ext_s, offset, rows):
    a, b = divmod(offset, SUBLANES)
    return ext_s[b, pl.ds(SUBLANES * a, rows), :]


def _glu_fwd(h, w_ref, bias):
    D = h.shape[1]
    a = jnp.concatenate([_mm(h, w_ref[0]), _mm(h, w_ref[1])], axis=1) + bias[:, :D]
    b = jnp.concatenate([_mm(h, w_ref[2]), _mm(h, w_ref[3])], axis=1) + bias[:, D:]
    return a, b


def conv_glu_fwd(x, mod3, g, w_glu, b_glu):
    B, T, D = x.shape
    tm = _tile(T, 512)

    def body(x_ref, mod_ref, g_ref, w_ref, b_ref, u_ref):
        h = _modnorm(x_ref[...], g_ref[...], mod_ref[1:2, :], mod_ref[0:1, :]).astype(BF16)
        a, b = _glu_fwd(h, w_ref, b_ref[...])
        u_ref[...] = a * _sigmoid(b)

    tok = pl.BlockSpec((None, tm, D), lambda b, t: (b, t, 0))
    return pl.pallas_call(
        body, name="conv_glu_fwd", grid=(B, T // tm),
        in_specs=[tok, pl.BlockSpec((None, 3, D), lambda b, t: (b, 0, 0)),
                  pl.BlockSpec((1, D), lambda b, t: (0, 0)),
                  pl.BlockSpec((4, D, D // 2), lambda b, t: (0, 0, 0)),
                  pl.BlockSpec((1, 2 * D), lambda b, t: (0, 0))],
        out_specs=tok, out_shape=jax.ShapeDtypeStruct((B, T, D), F32),
        compiler_params=_cparams(2),
    )(x, mod3, g, w_glu, b_glu)


def _layer_norm_parts(u2):
    mu = jnp.mean(u2, axis=-1, keepdims=True)
    xc = u2 - mu
    rs = lax.rsqrt(jnp.mean(xc * xc, axis=-1, keepdims=True) + EPS)
    return xc * rs, rs


def conv_out_fwd(x, u, mod3, w_dw, b_dw, ln_g, ln_b, w_pw, b_pw):
    B, T, D = x.shape
    K = w_dw.shape[0] - 1
    tm = _tile(T, 512)

    def body(x_ref, u_ref, halo_ref, mod_ref, wdw_ref, bdw_ref, lg_ref, lb_ref, wpw_ref, bpw_ref,
             xo_ref, y_ref, u2_ref, ext_s):
        t = pl.program_id(1)
        ext_s[0, 0:CONV_HALO, :] = jnp.where(t > 0, halo_ref[...], 0.0)
        ext_s[0, CONV_HALO:, :] = u_ref[...]
        _fill_shifted(ext_s)
        acc = jnp.broadcast_to(bdw_ref[...], (tm, D))
        for k in range(K):
            acc = acc + wdw_ref[k:k + 1, :] * _shifted(ext_s, CONV_HALO - (K - 1) + k, tm)
        u2_ref[...] = acc
        xh, _ = _layer_norm_parts(acc)
        l = xh * lg_ref[...] + lb_ref[...]
        u3 = l * _sigmoid(l)
        y = _mm(u3, wpw_ref[...]) + bpw_ref[...]
        y_ref[...] = y
        xo_ref[...] = x_ref[...] + (1.0 + mod_ref[2:3, :]) * y

    tok = pl.BlockSpec((None, tm, D), lambda b, t: (b, t, 0))
    vec = pl.BlockSpec((1, D), lambda b, t: (0, 0))
    return pl.pallas_call(
        body, name="conv_out_fwd", grid=(B, T // tm),
        in_specs=[tok, tok, _past_halo_spec(tm, CONV_HALO, D), pl.BlockSpec((None, 3, D), lambda b, t: (b, 0, 0)),
                  pl.BlockSpec((K + 1, D), lambda b, t: (0, 0)), vec, vec, vec,
                  pl.BlockSpec((D, D), lambda b, t: (0, 0)), vec],
        out_specs=[tok, tok, tok], out_shape=[jax.ShapeDtypeStruct((B, T, D), F32)] * 3,
        scratch_shapes=[pltpu.VMEM((SUBLANES, tm + CONV_HALO, D), F32)],
        compiler_params=_cparams(2),
    )(x, u, u, mod3, w_dw, b_dw, ln_g, ln_b, w_pw, b_pw)


def conv_out_bwd(dres, y, u2, mod3, ln_g, ln_b, w_pw):
    B, T, D = dres.shape
    tm = _tile(T, 512)

    def body(dres_ref, y_ref, u2_ref, mod_ref, lg_ref, lb_ref, wpw_ref, du2_ref, u3_ref, dy_ref, dgate_ref, vec_ref):
        t = pl.program_id(1)

        @pl.when(t == 0)
        def _():
            dgate_ref[...] = jnp.zeros_like(dgate_ref)
            vec_ref[...] = jnp.zeros_like(vec_ref)

        dres = dres_ref[...]
        dy = (1.0 + mod_ref[2:3, :]) * dres
        dy_ref[...] = dy.astype(BF16)
        dgate_ref[...] += _sum0(dres * y_ref[...])
        xh, rs = _layer_norm_parts(u2_ref[...])
        lg = lg_ref[...]
        l = xh * lg + lb_ref[...]
        sg = _sigmoid(l)
        u3_ref[...] = (l * sg).astype(BF16)
        du3 = _mm_nt(dy, wpw_ref[...])
        dl = du3 * _dsilu(l, sg)
        dxh = dl * lg
        du2 = rs * (dxh - jnp.mean(dxh, axis=-1, keepdims=True) - xh * jnp.mean(dxh * xh, axis=-1, keepdims=True))
        du2_ref[...] = du2
        vec_ref[0:1, :] += _sum0(dy)
        vec_ref[1:2, :] += _sum0(dl * xh)
        vec_ref[2:3, :] += _sum0(dl)
        vec_ref[3:4, :] += _sum0(du2)

    tok = pl.BlockSpec((None, tm, D), lambda b, t: (b, t, 0))
    tokb = pl.BlockSpec((None, tm, D), lambda b, t: (b, t, 0))
    vec = pl.BlockSpec((1, D), lambda b, t: (0, 0))
    return pl.pallas_call(
        body, name="conv_out_bwd", grid=(B, T // tm),
        in_specs=[tok, tok, tok, pl.BlockSpec((None, 3, D), lambda b, t: (b, 0, 0)), vec, vec,
                  pl.BlockSpec((D, D), lambda b, t: (0, 0))],
        out_specs=[tok, tokb, tokb, pl.BlockSpec((None, 1, D), lambda b, t: (b, 0, 0)),
                   pl.BlockSpec((None, 4, D), lambda b, t: (b, 0, 0))],
        out_shape=[jax.ShapeDtypeStruct((B, T, D), F32), jax.ShapeDtypeStruct((B, T, D), BF16),
                   jax.ShapeDtypeStruct((B, T, D), BF16), jax.ShapeDtypeStruct((B, 1, D), F32),
                   jax.ShapeDtypeStruct((B, 4, D), F32)],
        compiler_params=_cparams(2),
    )(dres, y, u2, mod3, ln_g, ln_b, w_pw)


def conv_glu_bwd(x, dres, du2, u, mod3, g, w_glu, b_glu, w_dw):
    B, T, D = x.shape
    K = w_dw.shape[0] - 1
    tm = _tile(T, 256)
    nt = T // tm

    def body(x_ref, dres_ref, du2_ref, du2h_ref, u_ref, uh_ref, mod_ref, g_ref, w_ref, b_ref, wdw_ref,
             dx_ref, h_ref, dab_ref, dwdw_ref, dbglu_ref, dmod_ref, dg_ref, extu_s, extd_s):
        t = pl.program_id(1)

        @pl.when(t == 0)
        def _():
            dwdw_ref[...] = jnp.zeros_like(dwdw_ref)
            dbglu_ref[...] = jnp.zeros_like(dbglu_ref)
            dmod_ref[...] = jnp.zeros_like(dmod_ref)
            dg_ref[...] = jnp.zeros_like(dg_ref)

        du2 = du2_ref[...]
        extu_s[0, 0:CONV_HALO, :] = jnp.where(t > 0, uh_ref[...], 0.0)
        extu_s[0, CONV_HALO:, :] = u_ref[...]
        extd_s[0, 0:tm, :] = du2
        extd_s[0, tm:, :] = jnp.where(t < nt - 1, du2h_ref[...], 0.0)
        _fill_shifted(extu_s)
        _fill_shifted(extd_s)
        du = jnp.zeros((tm, D), F32)
        for k in range(K):
            du = du + wdw_ref[k:k + 1, :] * _shifted(extd_s, K - 1 - k, tm)
            dwdw_ref[k:k + 1, :] += _sum0(du2 * _shifted(extu_s, CONV_HALO - (K - 1) + k, tm))
        xv = x_ref[...]
        h = _modnorm(xv, g_ref[...], mod_ref[1:2, :], mod_ref[0:1, :]).astype(BF16)
        h_ref[...] = h
        a, b = _glu_fwd(h, w_ref, b_ref[...])
        sb = _sigmoid(b)
        da = du * sb
        db = du * a * sb * (1.0 - sb)
        dbglu_ref[:, 0:D] += _sum0(da)
        dbglu_ref[:, D:] += _sum0(db)
        da = da.astype(BF16)
        db = db.astype(BF16)
        dab_ref[:, 0:D] = da
        dab_ref[:, D:] = db
        Dh2 = D // 2
        dh = (_mm_nt(da[:, :Dh2], w_ref[0]) + _mm_nt(da[:, Dh2:], w_ref[1])
              + _mm_nt(db[:, :Dh2], w_ref[2]) + _mm_nt(db[:, Dh2:], w_ref[3]))
        dxn, dg, dscale, dshift = _modnorm_bwd(xv, g_ref[...], mod_ref[1:2, :], dh)
        dx_ref[...] = dres_ref[...] + dxn
        dmod_ref[0:1, :] += dshift
        dmod_ref[1:2, :] += dscale
        dg_ref[...] += dg

    tok = pl.BlockSpec((None, tm, D), lambda b, t: (b, t, 0))
    return pl.pallas_call(
        body, name="conv_glu_bwd", grid=(B, nt),
        in_specs=[tok, tok, tok, _future_halo_spec(tm, CONV_HALO, D, T), tok, _past_halo_spec(tm, CONV_HALO, D),
                  pl.BlockSpec((None, 3, D), lambda b, t: (b, 0, 0)), pl.BlockSpec((1, D), lambda b, t: (0, 0)),
                  pl.BlockSpec((4, D, D // 2), lambda b, t: (0, 0, 0)), pl.BlockSpec((1, 2 * D), lambda b, t: (0, 0)),
                  pl.BlockSpec((K + 1, D), lambda b, t: (0, 0))],
        out_specs=[tok, tok, pl.BlockSpec((None, tm, 2 * D), lambda b, t: (b, t, 0)),
                   pl.BlockSpec((None, K + 1, D), lambda b, t: (b, 0, 0)),
                   pl.BlockSpec((None, 1, 2 * D), lambda b, t: (b, 0, 0)),
                   pl.BlockSpec((None, 3, D), lambda b, t: (b, 0, 0)),
                   pl.BlockSpec((None, 1, D), lambda b, t: (b, 0, 0))],
        out_shape=[jax.ShapeDtypeStruct((B, T, D), F32), jax.ShapeDtypeStruct((B, T, D), BF16),
                   jax.ShapeDtypeStruct((B, T, 2 * D), BF16), jax.ShapeDtypeStruct((B, K + 1, D), F32),
                   jax.ShapeDtypeStruct((B, 1, 2 * D), F32), jax.ShapeDtypeStruct((B, 3, D), F32),
                   jax.ShapeDtypeStruct((B, 1, D), F32)],
        scratch_shapes=[pltpu.VMEM((SUBLANES, tm + CONV_HALO, D), F32)] * 2,
        compiler_params=_cparams(2),
    )(x, dres, du2, du2, u, u, mod3, g, w_glu, b_glu, w_dw)


def dn_proj_fwd(x, mod3, g, w_main, w_ab):
    B, T, D = x.shape
    W = w_main.shape[1] // 4
    tm = _tile(T, 512)

    def body(x_ref, mod_ref, g_ref, wm_ref, wab_ref, pre_ref, z_ref, ab_ref):
        h = _modnorm(x_ref[...], g_ref[...], mod_ref[1:2, :], mod_ref[0:1, :]).astype(BF16)
        for p in range(3):
            pre_ref[:, p * W:(p + 1) * W] = _mm(h, wm_ref[:, p * W:(p + 1) * W])
        z_ref[...] = _mm(h, wm_ref[:, 3 * W:])
        ab_ref[...] = _mm(h, wab_ref[...])

    return pl.pallas_call(
        body, name="dn_proj_fwd", grid=(B, T // tm),
        in_specs=[pl.BlockSpec((None, tm, D), lambda b, t: (b, t, 0)), pl.BlockSpec((None, 3, D), lambda b, t: (b, 0, 0)),
                  pl.BlockSpec((1, D), lambda b, t: (0, 0)), pl.BlockSpec((D, 4 * W), lambda b, t: (0, 0)),
                  pl.BlockSpec((D, LANES), lambda b, t: (0, 0))],
        out_specs=[pl.BlockSpec((None, tm, 3 * W), lambda b, t: (b, t, 0)),
                   pl.BlockSpec((None, tm, W), lambda b, t: (b, t, 0)),
                   pl.BlockSpec((None, tm, LANES), lambda b, t: (b, t, 0))],
        out_shape=[jax.ShapeDtypeStruct((B, T, 3 * W), F32), jax.ShapeDtypeStruct((B, T, W), F32),
                   jax.ShapeDtypeStruct((B, T, LANES), F32)],
        compiler_params=_cparams(2),
    )(x, mod3, g, w_main, w_ab)


def _sconv(ext_s, w_ref, tm, K):
    acc = w_ref[0:1, :] * ext_s[pl.ds(SCONV_HALO - (K - 1), tm), :]
    for k in range(1, K):
        acc = acc + w_ref[k:k + 1, :] * ext_s[pl.ds(SCONV_HALO - (K - 1) + k, tm), :]
    return acc


def _lane_col(val, lane, idx):
    return jnp.sum(jnp.where(lane == idx, val, 0.0), axis=1, keepdims=True)


def dn_conv_fwd(pre, ab, w_sconv, alog_row, dt_row, H):
    B, T, W3 = pre.shape
    W = W3 // 3
    Dh = W // H
    K = w_sconv.shape[0]
    tm = _tile(T, 512)

    def body(pre_ref, halo_ref, ab_ref, w_ref, alog_ref, dt_ref, q_ref, k_ref, v_ref, gb_ref, bb_ref, ext_s):
        t = pl.program_id(1)
        ext_s[0:SCONV_HALO, :] = jnp.where(t > 0, halo_ref[...], 0.0)
        ext_s[SCONV_HALO:, :] = pre_ref[...]
        cv = _sconv(ext_s, w_ref, tm, K)
        qkv = cv * _sigmoid(cv)
        ab = ab_ref[...]
        lane = lax.broadcasted_iota(jnp.int32, ab.shape, 1)
        g_all = -jnp.exp(alog_ref[...]) * _softplus(ab + dt_ref[...])
        beta_all = _sigmoid(ab)
        for h in range(H):
            q_ref[h] = qkv[:, h * Dh:(h + 1) * Dh]
            k_ref[h] = qkv[:, W + h * Dh:W + (h + 1) * Dh]
            v_ref[h] = qkv[:, 2 * W + h * Dh:2 * W + (h + 1) * Dh]
            gb_ref[h] = jnp.broadcast_to(_lane_col(g_all, lane, h), (tm, Dh))
            bb_ref[h] = jnp.broadcast_to(_lane_col(beta_all, lane, H + h), (tm, Dh))

    hm = pl.BlockSpec((None, H, tm, Dh), lambda b, t: (b, 0, t, 0))
    row = pl.BlockSpec((1, LANES), lambda b, t: (0, 0))
    return pl.pallas_call(
        body, name="dn_conv_fwd", grid=(B, T // tm),
        in_specs=[pl.BlockSpec((None, tm, W3), lambda b, t: (b, t, 0)), _past_halo_spec(tm, SCONV_HALO, W3),
                  pl.BlockSpec((None, tm, LANES), lambda b, t: (b, t, 0)),
                  pl.BlockSpec((K, W3), lambda b, t: (0, 0)), row, row],
        out_specs=[hm] * 5, out_shape=[jax.ShapeDtypeStruct((B, H, T, Dh), F32)] * 5,
        scratch_shapes=[pltpu.VMEM((tm + SCONV_HALO, W3), F32)],
        compiler_params=_cparams(2),
    )(pre, pre, ab, w_sconv, alog_row, dt_row)


def _bdot(spec):
    return lambda a, b: jnp.einsum(spec, a.astype(BF16), b.astype(BF16), preferred_element_type=F32)


_NN, _NT, _TN = "gij,gjk->gik", "gik,gjk->gij", "gki,gkj->gij"


def _make_bdots():
    nn_, nt_, tn_ = _bdot(_NN), _bdot(_NT), _bdot(_TN)

    @jax.custom_vjp
    def nn(a, b):
        return nn_(a, b)

    @jax.custom_vjp
    def nt(a, b):
        return nt_(a, b)

    @jax.custom_vjp
    def tn(a, b):
        return tn_(a, b)

    nn.defvjp(lambda a, b: (nn_(a, b), (a, b)), lambda r, d: (nt_(d, r[1]), tn_(r[0], d)))
    nt.defvjp(lambda a, b: (nt_(a, b), (a, b)), lambda r, d: (nn_(d, r[1]), tn_(d, r[0])))
    tn.defvjp(lambda a, b: (tn_(a, b), (a, b)), lambda r, d: (nt_(r[1], d), nn_(r[0], d)))
    return nn, nt, tn


def _unit_lower_inverse(A, known=None):
    hdot = functools.partial(jnp.einsum, precision=lax.Precision.HIGH, preferred_element_type=F32)
    C = A.shape[-1]

    def impl(A):
        eye = (lax.broadcasted_iota(jnp.int32, A.shape, 1) == lax.broadcasted_iota(jnp.int32, A.shape, 2)).astype(F32)
        Tm = eye - A
        Ap = A
        for _ in range(max(1, (C - 1).bit_length()) - 1):
            Ap = hdot(_NN, Ap, Ap)
            Tm = Tm + hdot(_NN, Tm, Ap)
        return Tm

    @jax.custom_vjp
    def inv(A, given):
        return impl(A) if known is None else given

    def fwd(A, given):
        Tm = impl(A) if known is None else given
        return Tm, Tm

    def bwd(Tm, dT):
        return -hdot(_NT, hdot(_TN, Tm, dT), Tm), jnp.zeros_like(Tm)

    inv.defvjp(fwd, bwd)
    return inv(A, A if known is None else known)


def _chunk_fn(q, k, v, gb, bb, S, inverse=None, with_inverse=False):
    nn, nt, tn = _make_bdots()
    G, C, Dh = q.shape
    hdot = functools.partial(jnp.einsum, precision=lax.Precision.HIGH, preferred_element_type=F32)
    q = q * lax.rsqrt(jnp.sum(q * q, axis=-1, keepdims=True) + EPS) * (Dh ** -0.5)
    k = k * lax.rsqrt(jnp.sum(k * k, axis=-1, keepdims=True) + EPS)
    row = lax.broadcasted_iota(jnp.int32, (G, C, C), 1)
    col = lax.broadcasted_iota(jnp.int32, (G, C, C), 2)
    causal = row >= col
    strict = row > col
    gc = hdot(_NN, causal.astype(F32), gb)
    spread = jnp.full((G, C, Dh), 1.0 / Dh, F32)
    gi = hdot(_NT, gc, spread)
    gj = hdot(_NT, spread, gc)
    decay = jnp.where(causal, jnp.exp(jnp.where(causal, gi - gj, 0.0)), 0.0)
    kb = k * bb
    vb = v * bb
    A = jnp.where(strict, nt(kb, k) * decay, 0.0)
    Tm = _unit_lower_inverse(A, inverse)
    eg = jnp.exp(gc)
    u = nn(Tm, vb)
    w = nn(Tm, kb * eg)
    qg = q * eg
    intra = nt(q, k) * decay
    glast = hdot(_NN, jnp.ones((G, C, C), F32), gb)
    kd = k * jnp.exp(glast - gc)
    v_new = u - nn(w, S)
    o = nn(qg, S) + nn(intra, v_new)
    egl = jnp.exp(glast)
    S_new = S * jnp.concatenate([egl] * (Dh // C), axis=1) + tn(kd, v_new)
    return (o, S_new, Tm) if with_inverse else (o, S_new)


def dn_chunk_fwd(q, k, v, gb, bb):
    B, H, T, Dh = q.shape
    NC = T // CHUNK
    NS = _tile(NC, CHUNKS_PER_STEP, 1)

    def body(q_ref, k_ref, v_ref, gb_ref, bb_ref, o_ref, sp_ref, inv_ref, S_s):
        @pl.when(pl.program_id(1) == 0)
        def _():
            S_s[...] = jnp.zeros_like(S_s)

        def one_chunk(j, carry):
            rows = pl.ds(pl.multiple_of(j * CHUNK, CHUNK), CHUNK)
            S = S_s[...]
            sp_ref[j] = S
            o, S_new, Tm = _chunk_fn(q_ref[:, rows, :], k_ref[:, rows, :], v_ref[:, rows, :], gb_ref[:, rows, :],
                                     bb_ref[:, rows, :], S, with_inverse=True)
            o_ref[:, rows, :] = o
            inv_ref[j] = Tm
            S_s[...] = S_new
            return carry

        lax.fori_loop(0, NS, one_chunk, 0)

    hm = pl.BlockSpec((None, H, NS * CHUNK, Dh), lambda b, n: (b, 0, n, 0))
    return pl.pallas_call(
        body, name="dn_chunk_fwd", grid=(B, NC // NS),
        in_specs=[hm] * 5,
        out_specs=[hm, pl.BlockSpec((None, NS, H, Dh, Dh), lambda b, n: (b, n, 0, 0, 0)),
                   pl.BlockSpec((None, NS, H, CHUNK, CHUNK), lambda b, n: (b, n, 0, 0, 0))],
        out_shape=[jax.ShapeDtypeStruct((B, H, T, Dh), F32), jax.ShapeDtypeStruct((B, NC, H, Dh, Dh), F32),
                   jax.ShapeDtypeStruct((B, NC, H, CHUNK, CHUNK), F32)],
        scratch_shapes=[pltpu.VMEM((H, Dh, Dh), F32)],
        compiler_params=_cparams(2),
    )(q, k, v, gb, bb)


def dn_chunk_bwd(q, k, v, gb, bb, s_prev, inv, do):
    B, H, T, Dh = q.shape
    NC = T // CHUNK
    NS = _tile(NC, CHUNKS_PER_STEP, 1)
    NG = NC // NS

    def body(q_ref, k_ref, v_ref, gb_ref, bb_ref, sp_ref, inv_ref, do_ref, dq_ref, dk_ref, dv_ref, dgb_ref, dbb_ref,
             dS_s):
        @pl.when(pl.program_id(1) == 0)
        def _():
            dS_s[...] = jnp.zeros_like(dS_s)

        def one_chunk(jj, carry):
            j = NS - 1 - jj
            rows = pl.ds(pl.multiple_of(j * CHUNK, CHUNK), CHUNK)
            _, vjp = jax.vjp(functools.partial(_chunk_fn, inverse=inv_ref[j]), q_ref[:, rows, :], k_ref[:, rows, :],
                             v_ref[:, rows, :], gb_ref[:, rows, :], bb_ref[:, rows, :], sp_ref[j])
            dq, dk, dv, dgb, dbb, dS = vjp((do_ref[:, rows, :], dS_s[...]))
            dq_ref[:, rows, :] = dq
            dk_ref[:, rows, :] = dk
            dv_ref[:, rows, :] = dv
            dgb_ref[:, rows, :] = dgb
            dbb_ref[:, rows, :] = dbb
            dS_s[...] = dS
            return carry

        lax.fori_loop(0, NS, one_chunk, 0)

    hm = pl.BlockSpec((None, H, NS * CHUNK, Dh), lambda b, n: (b, 0, NG - 1 - n, 0))
    return pl.pallas_call(
        body, name="dn_chunk_bwd", grid=(B, NG),
        in_specs=[hm] * 5 + [pl.BlockSpec((None, NS, H, Dh, Dh), lambda b, n: (b, NG - 1 - n, 0, 0, 0)),
                             pl.BlockSpec((None, NS, H, CHUNK, CHUNK), lambda b, n: (b, NG - 1 - n, 0, 0, 0)), hm],
        out_specs=[hm] * 5, out_shape=[jax.ShapeDtypeStruct((B, H, T, Dh), F32)] * 5,
        scratch_shapes=[pltpu.VMEM((H, Dh, Dh), F32)],
        compiler_params=_cparams(2),
    )(q, k, v, gb, bb, s_prev, inv, do)


def _head_norm(o, og):
    r = lax.rsqrt(jnp.mean(o * o, axis=-1, keepdims=True) + EPS)
    return o * r, r


def dn_out_fwd(x, o, z, mod3, o_g, w_out):
    B, T, D = x.shape
    _, H, _, Dh = o.shape
    W = H * Dh
    tm = _tile(T, 512)

    def body(x_ref, o_ref, z_ref, mod_ref, og_ref, w_ref, xo_ref, y_ref):
        parts = []
        for h in range(H):
            on, _ = _head_norm(o_ref[h], og_ref[...])
            zz = z_ref[:, h * Dh:(h + 1) * Dh]
            parts.append((on * og_ref[...] * (zz * _sigmoid(zz))).astype(BF16))
        y = _mm(jnp.concatenate(parts, axis=1), w_ref[...])
        y_ref[...] = y
        xo_ref[...] = x_ref[...] + (1.0 + mod_ref[2:3, :]) * y

    tok = pl.BlockSpec((None, tm, D), lambda b, t: (b, t, 0))
    return pl.pallas_call(
        body, name="dn_out_fwd", grid=(B, T // tm),
        in_specs=[tok, pl.BlockSpec((None, H, tm, Dh), lambda b, t: (b, 0, t, 0)),
                  pl.BlockSpec((None, tm, W), lambda b, t: (b, t, 0)), pl.BlockSpec((None, 3, D), lambda b, t: (b, 0, 0)),
                  pl.BlockSpec((1, Dh), lambda b, t: (0, 0)), pl.BlockSpec((W, D), lambda b, t: (0, 0))],
        out_specs=[tok, tok], out_shape=[jax.ShapeDtypeStruct((B, T, D), F32)] * 2,
        compiler_params=_cparams(2),
    )(x, o, z, mod3, o_g, w_out)


def dn_out_bwd(dres, y, o, z, mod3, o_g, w_out):
    B, T, D = dres.shape
    _, H, _, Dh = o.shape
    W = H * Dh
    tm = _tile(T, 512)

    def body(dres_ref, y_ref, o_ref, z_ref, mod_ref, og_ref, w_ref, do_ref, dz_ref, ogb_ref, dy_ref, dgate_ref, dog_ref):
        t = pl.program_id(1)

        @pl.when(t == 0)
        def _():
            dgate_ref[...] = jnp.zeros_like(dgate_ref)
            dog_ref[...] = jnp.zeros_like(dog_ref)

        dres = dres_ref[...]
        dy = ((1.0 + mod_ref[2:3, :]) * dres).astype(BF16)
        dy_ref[...] = dy
        dgate_ref[...] += _sum0(dres * y_ref[...])
        dog = _mm_nt(dy, w_ref[...])
        og = og_ref[...]
        for h in range(H):
            ov = o_ref[h]
            xn, r = _head_norm(ov, og)
            zz = z_ref[:, h * Dh:(h + 1) * Dh]
            sg = _sigmoid(zz)
            sz = zz * sg
            d = dog[:, h * Dh:(h + 1) * Dh]
            ogb_ref[:, h * Dh:(h + 1) * Dh] = (xn * og * sz).astype(BF16)
            dz_ref[:, h * Dh:(h + 1) * Dh] = d * (xn * og) * _dsilu(zz, sg)
            don = d * sz
            dog_ref[...] += _sum0(don * xn)
            dxn = don * og
            do_ref[h] = r * (dxn - xn * jnp.mean(dxn * xn, axis=-1, keepdims=True))

    tok = pl.BlockSpec((None, tm, D), lambda b, t: (b, t, 0))
    tokw = pl.BlockSpec((None, tm, W), lambda b, t: (b, t, 0))
    hm = pl.BlockSpec((None, H, tm, Dh), lambda b, t: (b, 0, t, 0))
    return pl.pallas_call(
        body, name="dn_out_bwd", grid=(B, T // tm),
        in_specs=[tok, tok, hm, tokw, pl.BlockSpec((None, 3, D), lambda b, t: (b, 0, 0)),
                  pl.BlockSpec((1, Dh), lambda b, t: (0, 0)), pl.BlockSpec((W, D), lambda b, t: (0, 0))],
        out_specs=[hm, tokw, tokw, tok, pl.BlockSpec((None, 1, D), lambda b, t: (b, 0, 0)),
                   pl.BlockSpec((None, 1, Dh), lambda b, t: (b, 0, 0))],
        out_shape=[jax.ShapeDtypeStruct((B, H, T, Dh), F32), jax.ShapeDtypeStruct((B, T, W), F32),
                   jax.ShapeDtypeStruct((B, T, W), BF16), jax.ShapeDtypeStruct((B, T, D), BF16),
                   jax.ShapeDtypeStruct((B, 1, D), F32), jax.ShapeDtypeStruct((B, 1, Dh), F32)],
        compiler_params=_cparams(2),
    )(dres, y, o, z, mod3, o_g, w_out)


def dn_conv_bwd(dq, dk, dv, dgb, dbb, pre, ab, w_sconv, alog_row, dt_row):
    B, H, T, Dh = dq.shape
    W = H * Dh
    W3 = 3 * W
    K = w_sconv.shape[0]
    tm = _tile(T, 256)

    def body(dq_ref, dk_ref, dv_ref, dgb_ref, dbb_ref, pre_ref, halo_ref, ab_ref, w_ref, alog_ref, dt_ref,
             dc_ref, dab_ref, small_ref, ext_s):
        t = pl.program_id(1)

        @pl.when(t == 0)
        def _():
            small_ref[...] = jnp.zeros_like(small_ref)

        ext_s[0:SCONV_HALO, :] = jnp.where(t > 0, halo_ref[...], 0.0)
        ext_s[SCONV_HALO:, :] = pre_ref[...]
        cv = _sconv(ext_s, w_ref, tm, K)
        dsl = _dsilu(cv, _sigmoid(cv))
        ab = ab_ref[...]
        lane = lax.broadcasted_iota(jnp.int32, ab.shape, 1)
        dg_all = jnp.zeros_like(ab)
        db_all = jnp.zeros_like(ab)
        for h in range(H):
            dc_ref[:, h * Dh:(h + 1) * Dh] = dq_ref[h] * dsl[:, h * Dh:(h + 1) * Dh]
            dc_ref[:, W + h * Dh:W + (h + 1) * Dh] = dk_ref[h] * dsl[:, W + h * Dh:W + (h + 1) * Dh]
            dc_ref[:, 2 * W + h * Dh:2 * W + (h + 1) * Dh] = dv_ref[h] * dsl[:, 2 * W + h * Dh:2 * W + (h + 1) * Dh]
            dg_all = dg_all + jnp.where(lane == h, jnp.sum(dgb_ref[h], axis=1, keepdims=True), 0.0)
            db_all = db_all + jnp.where(lane == H + h, jnp.sum(dbb_ref[h], axis=1, keepdims=True), 0.0)
        xa = ab + dt_ref[...]
        ea = -jnp.exp(alog_ref[...])
        g_all = ea * _softplus(xa)
        da = dg_all * ea * _sigmoid(xa)
        beta = _sigmoid(ab)
        dab_ref[...] = da + db_all * beta * (1.0 - beta)
        small_ref[0:1, :] += _sum0(dg_all * g_all)
        small_ref[1:2, :] += _sum0(da)

    hm = pl.BlockSpec((None, H, tm, Dh), lambda b, t: (b, 0, t, 0))
    row = pl.BlockSpec((1, LANES), lambda b, t: (0, 0))
    return pl.pallas_call(
        body, name="dn_conv_bwd", grid=(B, T // tm),
        in_specs=[hm] * 5 + [pl.BlockSpec((None, tm, W3), lambda b, t: (b, t, 0)), _past_halo_spec(tm, SCONV_HALO, W3),
                             pl.BlockSpec((None, tm, LANES), lambda b, t: (b, t, 0)),
                             pl.BlockSpec((K, W3), lambda b, t: (0, 0)), row, row],
        out_specs=[pl.BlockSpec((None, tm, W3), lambda b, t: (b, t, 0)), pl.BlockSpec((None, tm, LANES), lambda b, t: (b, t, 0)),
                   pl.BlockSpec((None, 2, LANES), lambda b, t: (b, 0, 0))],
        out_shape=[jax.ShapeDtypeStruct((B, T, W3), F32), jax.ShapeDtypeStruct((B, T, LANES), F32),
                   jax.ShapeDtypeStruct((B, 2, LANES), F32)],
        scratch_shapes=[pltpu.VMEM((tm + SCONV_HALO, W3), F32)],
        compiler_params=_cparams(2),
    )(dq, dk, dv, dgb, dbb, pre, pre, ab, w_sconv, alog_row, dt_row)


def dn_proj_bwd(x, dres, dc, pre, dz, dab, mod3, g, w_main, w_ab, w_sconv):
    B, T, D = x.shape
    W3 = dc.shape[2]
    W = W3 // 3
    K = w_sconv.shape[0]
    tm = _tile(T, 256)
    nt = T // tm

    def body(x_ref, dres_ref, dc_ref, dch_ref, pre_ref, preh_ref, dz_ref, dab_ref, mod_ref, g_ref, wm_ref, wab_ref, ws_ref,
             dx_ref, h_ref, dproj_ref, dws_ref, dmod_ref, dg_ref, extp_s, extd_s):
        t = pl.program_id(1)

        @pl.when(t == 0)
        def _():
            dws_ref[...] = jnp.zeros_like(dws_ref)
            dmod_ref[...] = jnp.zeros_like(dmod_ref)
            dg_ref[...] = jnp.zeros_like(dg_ref)

        dc = dc_ref[...]
        extp_s[0:SCONV_HALO, :] = jnp.where(t > 0, preh_ref[...], 0.0)
        extp_s[SCONV_HALO:, :] = pre_ref[...]
        extd_s[0:tm, :] = dc
        extd_s[tm:, :] = jnp.where(t < nt - 1, dch_ref[...], 0.0)
        dpre = jnp.zeros((tm, W3), F32)
        for k in range(K):
            dpre = dpre + ws_ref[k:k + 1, :] * extd_s[pl.ds(K - 1 - k, tm), :]
            dws_ref[k:k + 1, :] += _sum0(dc * extp_s[pl.ds(SCONV_HALO - (K - 1) + k, tm), :])
        dpre = dpre.astype(BF16)
        dzb = dz_ref[...].astype(BF16)
        dproj_ref[:, 0:W3] = dpre
        dproj_ref[:, W3:] = dzb
        dh = _mm_nt(dab_ref[...], wab_ref[...]) + _mm_nt(dzb, wm_ref[:, W3:])
        for p in range(3):
            dh = dh + _mm_nt(dpre[:, p * W:(p + 1) * W], wm_ref[:, p * W:(p + 1) * W])
        xv = x_ref[...]
        h_ref[...] = _modnorm(xv, g_ref[...], mod_ref[1:2, :], mod_ref[0:1, :]).astype(BF16)
        dxn, dg, dscale, dshift = _modnorm_bwd(xv, g_ref[...], mod_ref[1:2, :], dh)
        dx_ref[...] = dres_ref[...] + dxn
        dmod_ref[0:1, :] += dshift
        dmod_ref[1:2, :] += dscale
        dg_ref[...] += dg

    tok = pl.BlockSpec((None, tm, D), lambda b, t: (b, t, 0))
    tok3 = pl.BlockSpec((None, tm, W3), lambda b, t: (b, t, 0))
    return pl.pallas_call(
        body, name="dn_proj_bwd", grid=(B, nt),
        in_specs=[tok, tok, tok3, _future_halo_spec(tm, SCONV_HALO, W3, T), tok3, _past_halo_spec(tm, SCONV_HALO, W3),
                  pl.BlockSpec((None, tm, W), lambda b, t: (b, t, 0)), pl.BlockSpec((None, tm, LANES), lambda b, t: (b, t, 0)),
                  pl.BlockSpec((None, 3, D), lambda b, t: (b, 0, 0)), pl.BlockSpec((1, D), lambda b, t: (0, 0)),
                  pl.BlockSpec((D, 4 * W), lambda b, t: (0, 0)), pl.BlockSpec((D, LANES), lambda b, t: (0, 0)),
                  pl.BlockSpec((K, W3), lambda b, t: (0, 0))],
        out_specs=[tok, tok, pl.BlockSpec((None, tm, 4 * W), lambda b, t: (b, t, 0)),
                   pl.BlockSpec((None, K, W3), lambda b, t: (b, 0, 0)), pl.BlockSpec((None, 3, D), lambda b, t: (b, 0, 0)),
                   pl.BlockSpec((None, 1, D), lambda b, t: (b, 0, 0))],
        out_shape=[jax.ShapeDtypeStruct((B, T, D), F32), jax.ShapeDtypeStruct((B, T, D), BF16),
                   jax.ShapeDtypeStruct((B, T, 4 * W), BF16), jax.ShapeDtypeStruct((B, K, W3), F32),
                   jax.ShapeDtypeStruct((B, 3, D), F32), jax.ShapeDtypeStruct((B, 1, D), F32)],
        scratch_shapes=[pltpu.VMEM((tm + SCONV_HALO, W3), F32), pltpu.VMEM((tm + SCONV_HALO, W3), F32)],
        compiler_params=_cparams(2),
    )(x, dres, dc, dc, pre, pre, dz, dab, mod3, g, w_main, w_ab, w_sconv)


def ada_fwd(c_all, w_ada, b_cols):
    L, D, Ca = w_ada.shape
    NB = c_all.shape[0]

    def body(c_ref, w_ref, b_ref, o_ref):
        cv = c_ref[...]
        o_ref[...] = _mm(cv * _sigmoid(cv), w_ref[...]) + b_ref[...]

    return pl.pallas_call(
        body, name="ada_fwd", grid=(L,),
        in_specs=[pl.BlockSpec((NB, D), lambda i: (0, 0)), pl.BlockSpec((None, D, Ca), lambda i: (i, 0, 0)),
                  pl.BlockSpec((None, 1, Ca), lambda i: (i, 0, 0))],
        out_specs=pl.BlockSpec((None, NB, Ca), lambda i: (i, 0, 0)),
        out_shape=jax.ShapeDtypeStruct((L, NB, Ca), F32),
        compiler_params=_cparams(1),
    )(c_all, w_ada, b_cols)


def ada_bwd(c_all, dmod_cols, dmod_all):
    L, NB, Ca = dmod_cols.shape
    D = c_all.shape[1]
    C9 = dmod_all.shape[2]

    def body(c_ref, dc_ref, da_ref, gw_ref, gb_ref):
        cv = c_ref[...]
        gw_ref[...] = _mm_tn(cv * _sigmoid(cv), dc_ref[...])
        gb_ref[...] = _sum0(da_ref[...])

    return pl.pallas_call(
        body, name="ada_bwd", grid=(L,),
        in_specs=[pl.BlockSpec((NB, D), lambda i: (0, 0)), pl.BlockSpec((None, NB, Ca), lambda i: (i, 0, 0)),
                  pl.BlockSpec((None, NB, C9), lambda i: (i, 0, 0))],
        out_specs=[pl.BlockSpec((None, D, Ca), lambda i: (i, 0, 0)), pl.BlockSpec((None, 1, C9), lambda i: (i, 0, 0))],
        out_shape=[jax.ShapeDtypeStruct((L, D, Ca), F32), jax.ShapeDtypeStruct((L, 1, C9), F32)],
        compiler_params=_cparams(1),
    )(c_all, dmod_cols, dmod_all)


def adamw(w, g, m, v, name, token=None):
    R, C = w.shape
    tr = _tile(R, max(8, (1 << 18) // C))
    if token is None:
        token = jnp.zeros((8, LANES), F32)

    def body(w_ref, g_ref, m_ref, v_ref, t_ref, d_ref, mo_ref, vo_ref):
        gv = g_ref[...] + t_ref[0:1, 0:1]
        mn = ADAM_B1 * m_ref[...] + (1.0 - ADAM_B1) * gv
        vn = ADAM_B2 * v_ref[...] + (1.0 - ADAM_B2) * (gv * gv)
        m_hat = mn / (1.0 - ADAM_B1 ** ADAM_STEP)
        v_hat = vn / (1.0 - ADAM_B2 ** ADAM_STEP)
        d_ref[...] = -ADAM_LR * (m_hat / (jnp.sqrt(v_hat) + ADAM_EPS) + ADAM_WD * w_ref[...])
        mo_ref[...] = mn
        vo_ref[...] = vn

    blk = pl.BlockSpec((tr, C), lambda i: (i, 0))
    return pl.pallas_call(
        body, name=name, grid=(R // tr,), in_specs=[blk] * 4 + [pl.BlockSpec((8, LANES), lambda i: (0, 0))],
        out_specs=[blk] * 3, out_shape=[jax.ShapeDtypeStruct((R, C), F32)] * 3, compiler_params=_cparams(1),
    )(w, g, m, v, token)


def sum_devices(a):
    n, R, C = a.shape

    def body(a_ref, o_ref):
        s = a_ref[0]
        for d in range(1, n):
            s = s + a_ref[d]
        o_ref[...] = s

    return pl.pallas_call(
        body, name="sum_devices", out_shape=jax.ShapeDtypeStruct((R, C), F32),
        compiler_params=pltpu.CompilerParams(vmem_limit_bytes=VMEM_LIMIT_V7X),
    )(a)


def _place():
    x, y, c = lax.axis_index("x"), lax.axis_index("y"), lax.axis_index("c")
    return x, y, c


def _other_chips(x, y):
    return [(2 * (1 - x) + y, 1 - x, y), (2 * x + (1 - y), x, 1 - y), (2 * (1 - x) + (1 - y), 1 - x, 1 - y)]


def allgather8(block):
    m_per, n = block.shape

    def body(x_ref, out_ref, send_sems, recv_sems, local_sem):
        x, y, c = _place()
        me, sibling = (x, y, c), (x, y, 1 - c)
        chips = [(1 - x, y), (x, 1 - y), (1 - x, 1 - y)]

        def rows(px, py, pc):
            return out_ref.at[pl.ds((4 * px + 2 * py + pc) * m_per, m_per), :]

        def copy(k, blk, to, src=None):
            return pltpu.make_async_remote_copy(
                src_ref=rows(*blk) if src is None else src, dst_ref=rows(*blk),
                send_sem=send_sems.at[k], recv_sem=recv_sems.at[k], device_id=to, device_id_type=MESH)

        mine = pltpu.make_async_copy(x_ref, rows(*me), local_sem)
        mine.start()
        first = [copy(0, me, sibling, src=x_ref)]
        first += [copy(1 + j, me, (*chip, c), src=x_ref) for j, chip in enumerate(chips)]
        for cp in first:
            cp.start()
        passed = [copy(4 + j, (*chip, c), sibling) for j, chip in enumerate(chips)]
        for j, chip in enumerate(chips):
            copy(1 + j, (*chip, c), me).wait_recv()
            passed[j].start()
        copy(0, sibling, me).wait_recv()
        for j, chip in enumerate(chips):
            copy(4 + j, (*chip, 1 - c), me).wait_recv()
        for cp in first + passed:
            cp.wait_send()
        mine.wait()

    return pl.pallas_call(
        body, name="allgather8", out_shape=jax.ShapeDtypeStruct((N_DEV * m_per, n), block.dtype),
        in_specs=[pl.BlockSpec(memory_space=pltpu.VMEM)], out_specs=pl.BlockSpec(memory_space=pltpu.VMEM),
        scratch_shapes=[pltpu.SemaphoreType.DMA((7,)), pltpu.SemaphoreType.DMA((7,)), pltpu.SemaphoreType.DMA],
        compiler_params=pltpu.CompilerParams(vmem_limit_bytes=VMEM_LIMIT_V7X),
    )(block)


def _half(ref, c, rh):
    return ref.at[pl.ds(pl.multiple_of(c * rh, 16), rh), :]


def pair_exchange(grads):
    K = len(grads)

    def body(*refs):
        ins, outs = refs[:K], refs[K:2 * K]
        send_sems, recv_sems = refs[2 * K:]
        x, y, c = _place()
        sibling = (x, y, 1 - c)
        copies = []
        for k in range(K):
            n, r, _ = ins[k].shape
            rh = r // 2
            cp = pltpu.make_async_remote_copy(
                src_ref=ins[k].at[:, pl.ds(pl.multiple_of((1 - c) * rh, 16), rh), :], dst_ref=outs[k],
                send_sem=send_sems.at[k], recv_sem=recv_sems.at[k], device_id=sibling, device_id_type=MESH)
            cp.start()
            copies.append(cp)
        for cp in copies:
            cp.wait_recv()
        for cp in copies:
            cp.wait_send()

    return pl.pallas_call(
        body, name="pair_exchange",
        out_shape=[jax.ShapeDtypeStruct((g.shape[0], g.shape[1] // 2, g.shape[2]), g.dtype) for g in grads],
        in_specs=[HBM_SPEC] * K, out_specs=[HBM_SPEC] * K,
        scratch_shapes=[pltpu.SemaphoreType.DMA((K,))] * 2,
    )(*grads)


def pair_add(grad, recv, c_idx):
    n, r, C = grad.shape
    rh = r // 2
    tr = _tile(rh, max(16, (1 << 19) // C), 16)
    grad = grad.reshape(n, 2, rh, C)

    def body(c_ref, g_ref, r_ref, o_ref):
        o_ref[...] = (g_ref[...].astype(F32) + r_ref[...].astype(F32)).astype(BF16)

    return pl.pallas_call(
        body, name="pair_add",
        grid_spec=pltpu.PrefetchScalarGridSpec(
            num_scalar_prefetch=1, grid=(n, rh // tr),
            in_specs=[pl.BlockSpec((None, None, tr, C), lambda d, i, c_ref: (d, c_ref[0], i, 0)),
                      pl.BlockSpec((None, tr, C), lambda d, i, c_ref: (d, i, 0))],
            out_specs=pl.BlockSpec((None, tr, C), lambda d, i, c_ref: (d, i, 0))),
        out_shape=jax.ShapeDtypeStruct((n, rh, C), BF16), compiler_params=_cparams(2),
    )(c_idx, grad, recv)


def chip_sum(parts, got, where, stack, slot):
    _, rh, C = parts.shape
    tr = _tile(rh, max(16, (1 << 19) // C), 16)
    nt = rh // tr

    def body(w_ref, p_ref, g_ref, stack_any, o_ref):
        s = p_ref[...].astype(F32)
        for r in range(3):
            s = s + g_ref[r].astype(F32)
        o_ref[...] = s

    return pl.pallas_call(
        body, name="chip_sum",
        grid_spec=pltpu.PrefetchScalarGridSpec(
            num_scalar_prefetch=1, grid=(nt,),
            in_specs=[pl.BlockSpec((None, tr, C), lambda i, w_ref: (w_ref[0], i, 0)),
                      pl.BlockSpec((3, tr, C), lambda i, w_ref: (0, i, 0)),
                      pl.BlockSpec(memory_space=pl.ANY)],
            out_specs=pl.BlockSpec((None, tr, C), lambda i, w_ref: (slot, w_ref[1] * nt + i, 0))),
        out_shape=jax.ShapeDtypeStruct(stack.shape, F32), input_output_aliases={3: 0},
        compiler_params=_cparams(1),
    )(where, parts, got, stack)


def pair_share(stacks, slots):
    K = len(stacks)
    jobs = [(k, s) for k in range(K) for s in slots[k]]

    def body(*refs):
        ins, outs = refs[:K], refs[K:2 * K]
        send_sems, recv_sems = refs[2 * K:]
        x, y, c = _place()
        sibling = (x, y, 1 - c)
        started = []
        for n, (k, s) in enumerate(jobs):
            rh = ins[k].shape[1] // 2
            cp = pltpu.make_async_remote_copy(
                src_ref=_half(ins[k].at[s], c, rh), dst_ref=_half(outs[k].at[s], c, rh), send_sem=send_sems.at[n],
                recv_sem=recv_sems.at[n], device_id=sibling, device_id_type=MESH)
            cp.start()
            started.append(cp)
        for n, (k, s) in enumerate(jobs):
            rh = ins[k].shape[1] // 2
            theirs = _half(outs[k].at[s], 1 - c, rh)
            pltpu.make_async_remote_copy(
                src_ref=theirs, dst_ref=theirs, send_sem=send_sems.at[n], recv_sem=recv_sems.at[n],
                device_id=sibling, device_id_type=MESH).wait_recv()
        for cp in started:
            cp.wait_send()

    return pl.pallas_call(
        body, name="pair_share",
        out_shape=[jax.ShapeDtypeStruct(s.shape, s.dtype) for s in stacks],
        in_specs=[HBM_SPEC] * K, out_specs=[HBM_SPEC] * K, input_output_aliases={k: k for k in range(K)},
        scratch_shapes=[pltpu.SemaphoreType.DMA((len(jobs),))] * 2,
    )(*stacks)


SEM_SPEC = pl.BlockSpec(memory_space=pltpu.SEMAPHORE)
ANY_SPEC = pl.BlockSpec(memory_space=pl.ANY)
DATAFLOW = pltpu.SideEffectType.DATAFLOW_SIDE_EFFECTING


def _in_hbm(a):
    return pltpu.with_memory_space_constraint(a, pltpu.HBM)


def _ici_copies(srcs, dsts, send_sems, recv_sems, src_slice, dst_slice):
    x, y, c = _place()
    out = []
    for k in range(len(srcs)):
        for r, (pchip, px, py) in enumerate(_other_chips(x, y)):
            out.append(pltpu.make_async_remote_copy(
                src_ref=src_slice(srcs[k], r, pchip), dst_ref=dst_slice(dsts[k], r, pchip),
                send_sem=send_sems.at[3 * k + r], recv_sem=recv_sems.at[3 * k + r], device_id=(px, py, c),
                device_id_type=MESH))
    return out


def _exchange_start(bufs, lands, src_slice, dst_slice, name, after=None):
    K = len(bufs)
    same = lands is None
    n_thru = K if same else 2 * K
    n_in = n_thru + (after is not None)

    def body(*refs):
        ins = refs[:n_thru]
        send_sems, recv_sems = refs[n_in], refs[n_in + 1]
        token = refs[-1]
        srcs = ins[:K]
        dsts = srcs if same else ins[K:]
        for cp in _ici_copies(srcs, dsts, send_sems, recv_sems, src_slice, dst_slice):
            cp.start()
        token[...] = jnp.zeros_like(token)

    thru = list(bufs) + ([] if same else list(lands))
    res = pl.pallas_call(
        body, name=name,
        out_shape=[pltpu.SemaphoreType.DMA((3 * K,)), pltpu.SemaphoreType.DMA((3 * K,))]
        + [pltpu.HBM(a.shape, a.dtype) for a in thru] + [jax.ShapeDtypeStruct((8, LANES), F32)],
        in_specs=[HBM_SPEC] * n_thru + [ANY_SPEC] * (after is not None),
        out_specs=[SEM_SPEC, SEM_SPEC] + [HBM_SPEC] * n_thru + [pl.BlockSpec(memory_space=pltpu.VMEM)],
        input_output_aliases={i: 2 + i for i in range(n_thru)},
        compiler_params=pltpu.CompilerParams(has_side_effects=DATAFLOW),
    )(*[_in_hbm(a) for a in thru], *([] if after is None else [after]))
    return res[0], res[1], res[2:2 + K], (res[2:2 + K] if same else res[2 + K:2 + 2 * K]), res[-1]


def _exchange_wait(send_sems, recv_sems, bufs, lands, after, src_slice, dst_slice, name):
    K = len(bufs)
    same = lands is None
    n_thru = K if same else 2 * K

    def body(*refs):
        ins = refs[:n_thru]
        ssem, rsem = refs[n_thru], refs[n_thru + 1]
        srcs = ins[:K]
        dsts = srcs if same else ins[K:]
        copies = _ici_copies(srcs, dsts, ssem, rsem, src_slice, dst_slice)
        for cp in copies:
            cp.wait_send()
        for cp in copies:
            cp.wait_recv()

    thru = list(bufs) + ([] if same else list(lands))
    res = pl.pallas_call(
        body, name=name,
        out_shape=[pltpu.HBM(a.shape, a.dtype) for a in thru],
        in_specs=[HBM_SPEC] * n_thru + [SEM_SPEC, SEM_SPEC, ANY_SPEC],
        out_specs=[HBM_SPEC] * n_thru,
        input_output_aliases={i: i for i in range(n_thru)},
        compiler_params=pltpu.CompilerParams(has_side_effects=DATAFLOW),
    )(*thru, send_sems, recv_sems, after)
    return res[:K], (res[:K] if same else res[K:])


def _own_half(ref, r, pchip):
    x, y, c = _place()
    return _half(ref.at[2 * x + y], c, ref.shape[1] // 2)


def _their_half(ref, r, pchip):
    _, _, c = _place()
    return _half(ref.at[pchip], c, ref.shape[1] // 2)


def gather_start(lands, name, after=None):
    return _exchange_start(lands, None, _own_half, _own_half, name, after)


def gather_wait(handle, after, name):
    ssem, rsem, lands, _, _ = handle
    return _exchange_wait(ssem, rsem, lands, None, after, _own_half, _their_half, name)[1]


def pair_forward(lands):
    K = len(lands)

    def body(*refs):
        ins, outs = refs[:K], refs[K:2 * K]
        send_sems, recv_sems = refs[2 * K:]
        x, y, c = _place()
        sibling = (x, y, 1 - c)
        started = []
        for k in range(K):
            rh = ins[k].shape[1] // 2
            for r, (pchip, _, _) in enumerate(_other_chips(x, y)):
                cp = pltpu.make_async_remote_copy(
                    src_ref=_half(ins[k].at[pchip], c, rh), dst_ref=_half(outs[k].at[pchip], c, rh),
                    send_sem=send_sems.at[k, r], recv_sem=recv_sems.at[k, r], device_id=sibling, device_id_type=MESH)
                cp.start()
                started.append(cp)
        for k in range(K):
            rh = ins[k].shape[1] // 2
            for r, (pchip, _, _) in enumerate(_other_chips(x, y)):
                theirs = _half(outs[k].at[pchip], 1 - c, rh)
                pltpu.make_async_remote_copy(
                    src_ref=theirs, dst_ref=theirs, send_sem=send_sems.at[k, r], recv_sem=recv_sems.at[k, r],
                    device_id=sibling, device_id_type=MESH).wait_recv()
        for cp in started:
            cp.wait_send()

    return pl.pallas_call(
        body, name="pair_forward",
        out_shape=[jax.ShapeDtypeStruct(s.shape, s.dtype) for s in lands],
        in_specs=[HBM_SPEC] * K, out_specs=[HBM_SPEC] * K, input_output_aliases={k: k for k in range(K)},
        scratch_shapes=[pltpu.SemaphoreType.DMA((K, 3))] * 2,
    )(*lands)


def _to_chip(ref, r, pchip):
    return ref.at[pchip]


def _from_relation(ref, r, pchip):
    return ref.at[r]


def reduce_start(grads, c_idx, name, after=None):
    recv = pair_exchange(grads)
    parts = [pair_add(g, r, c_idx) for g, r in zip(grads, recv)]
    lands = [lax.empty((3,) + p.shape[1:], p.dtype) for p in parts]
    return _exchange_start(parts, lands, _to_chip, _from_relation, name, after)


def reduce_finish(handle, after, where, name, stacks, targets):
    ssem, rsem, parts, lands, _ = handle
    parts, got = _exchange_wait(ssem, rsem, parts, lands, after, _to_chip, _from_relation, name)
    stacks = dict(stacks)
    for p, g, (key, slot) in zip(parts, got, targets):
        stacks[key] = chip_sum(p, g, where, stacks[key], slot)
    keys = list(dict.fromkeys(key for key, _ in targets))
    shared = pair_share([stacks[k] for k in keys], [[s for key, s in targets if key == k] for k in keys])
    stacks.update(zip(keys, shared))
    return stacks


def _pack(arrs):
    flat = jnp.concatenate([a.reshape(-1).astype(F32) for a in arrs])
    pad = (-flat.shape[0]) % (8 * LANES)
    return jnp.pad(flat, (0, pad)).reshape(-1, LANES)


def _unpack(flat, shapes):
    out, off = [], 0
    for s in shapes:
        n = 1
        for d in s:
            n *= d
        out.append(flat[off:off + n].reshape(s))
        off += n
    return out


def _adamw_any(w, g, m, v, name, token=None):
    shp = w.shape
    C = shp[-1]
    d, nm, nv = adamw(w.reshape(-1, C), g.reshape(-1, C), m.reshape(-1, C), v.reshape(-1, C), name, token)
    return d.reshape(shp), nm.reshape(shp), nv.reshape(shp)


def kernel(x, c, norm_g, w_ada, b_ada, w_ffn_in, w_ffn_out, cm_w_glu, cm_b_glu, cm_w_dw, cm_b_dw, cm_ln_g, cm_ln_b, cm_w_pw, cm_b_pw, dn_w_in, dn_w_sconv, dn_a_log, dn_dt_bias, dn_o_g, dn_w_out, final_g, loss_target, m_norm_g, m_w_ada, m_b_ada, m_w_ffn_in, m_w_ffn_out, m_cm_w_glu, m_cm_b_glu, m_cm_w_dw, m_cm_b_dw, m_cm_ln_g, m_cm_ln_b, m_cm_w_pw, m_cm_b_pw, m_dn_w_in, m_dn_w_sconv, m_dn_a_log, m_dn_dt_bias, m_dn_o_g, m_dn_w_out, m_final_g, v_norm_g, v_w_ada, v_b_ada, v_w_ffn_in, v_w_ffn_out, v_cm_w_glu, v_cm_b_glu, v_cm_w_dw, v_cm_b_dw, v_cm_ln_g, v_cm_ln_b, v_cm_w_pw, v_cm_b_pw, v_dn_w_in, v_dn_w_sconv, v_dn_a_log, v_dn_dt_bias, v_dn_o_g, v_dn_w_out, v_final_g):
    weights = dict(norm_g=norm_g, w_ada=w_ada, b_ada=b_ada, w_ffn_in=w_ffn_in, w_ffn_out=w_ffn_out, cm_w_glu=cm_w_glu,
                   cm_b_glu=cm_b_glu, cm_w_dw=cm_w_dw, cm_b_dw=cm_b_dw, cm_ln_g=cm_ln_g, cm_ln_b=cm_ln_b, cm_w_pw=cm_w_pw,
                   cm_b_pw=cm_b_pw, dn_w_in=dn_w_in, dn_w_sconv=dn_w_sconv, dn_a_log=dn_a_log, dn_dt_bias=dn_dt_bias,
                   dn_o_g=dn_o_g, dn_w_out=dn_w_out, final_g=final_g)
    mom_m = dict(norm_g=m_norm_g, w_ada=m_w_ada, b_ada=m_b_ada, w_ffn_in=m_w_ffn_in, w_ffn_out=m_w_ffn_out,
                 cm_w_glu=m_cm_w_glu, cm_b_glu=m_cm_b_glu, cm_w_dw=m_cm_w_dw, cm_b_dw=m_cm_b_dw, cm_ln_g=m_cm_ln_g,
                 cm_ln_b=m_cm_ln_b, cm_w_pw=m_cm_w_pw, cm_b_pw=m_cm_b_pw, dn_w_in=m_dn_w_in, dn_w_sconv=m_dn_w_sconv,
                 dn_a_log=m_dn_a_log, dn_dt_bias=m_dn_dt_bias, dn_o_g=m_dn_o_g, dn_w_out=m_dn_w_out, final_g=m_final_g)
    mom_v = dict(norm_g=v_norm_g, w_ada=v_w_ada, b_ada=v_b_ada, w_ffn_in=v_w_ffn_in, w_ffn_out=v_w_ffn_out,
                 cm_w_glu=v_cm_w_glu, cm_b_glu=v_cm_b_glu, cm_w_dw=v_cm_w_dw, cm_b_dw=v_cm_b_dw, cm_ln_g=v_cm_ln_g,
                 cm_ln_b=v_cm_ln_b, cm_w_pw=v_cm_w_pw, cm_b_pw=v_cm_b_pw, dn_w_in=v_dn_w_in, dn_w_sconv=v_dn_w_sconv,
                 dn_a_log=v_dn_a_log, dn_dt_bias=v_dn_dt_bias, dn_o_g=v_dn_o_g, dn_w_out=v_dn_w_out, final_g=v_final_g)
    names = list(weights)

    BL, T, D = x.shape
    L = norm_g.shape[0]
    NB = BL * N_DEV
    Ca = w_ada.shape[2]
    C9 = b_ada.shape[1]
    H = dn_a_log.shape[1]
    Dh = dn_o_g.shape[1]
    W = H * Dh
    KC = cm_w_dw.shape[1]
    KS = dn_w_sconv.shape[1]
    n_cm, n_dn = cm_w_glu.shape[0], dn_w_in.shape[0]
    ax, ay, ac = lax.axis_index("x"), lax.axis_index("y"), lax.axis_index("c")
    chip = 2 * ax + ay
    dev = 2 * chip + ac
    c_idx = ac.astype(jnp.int32).reshape(1)
    where = jnp.stack([chip, ac]).astype(jnp.int32)

    def landing(s, tok=None):
        s = s if tok is None else s + tok
        return lax.dynamic_update_slice(lax.empty((N_CHIPS,) + s.shape, BF16), s.astype(BF16)[None], (chip, 0, 0))

    def layer_shards(i):
        sh = [w_ffn_in[i, 0], w_ffn_in[i, 1], w_ffn_out[i, 0], w_ffn_out[i, 1]]
        if i % 2 == 0:
            sh += [cm_w_glu[i // 2], cm_w_pw[i // 2]]
        else:
            sh += [dn_w_in[i // 2], dn_w_out[i // 2]]
        return sh

    wts = [None] * L

    small_in = [c, norm_g, cm_w_dw, dn_w_sconv]
    gathered = allgather8(_pack(small_in)).reshape(N_DEV, -1)
    per_dev = [_unpack(gathered[d], [a.shape for a in small_in]) for d in range(N_DEV)]
    c_all = jnp.concatenate([p[0] for p in per_dev], axis=0)
    norm_g_full = jnp.concatenate([per_dev[2 * s][1] for s in range(N_CHIPS)], axis=-1)
    w_dw_full = jnp.concatenate([per_dev[2 * s][2] for s in range(N_CHIPS)], axis=-1)
    w_sconv_full = jnp.concatenate([per_dev[2 * s][3] for s in range(N_CHIPS)], axis=-1)

    b_cols = lax.dynamic_slice_in_dim(b_ada, chip * Ca, Ca, axis=1).reshape(L, 1, Ca)
    mod_part = ada_fwd(c_all, w_ada, b_cols)
    mod_g = allgather8(mod_part.reshape(-1, LANES))
    shards0 = layer_shards(0)
    first = gather_start([landing(shards0[0]), landing(shards0[2])], "gather_start_0a", mod_g)
    tok0 = first[4][0, 0]
    rest = gather_start([landing(shards0[k], tok0) for k in (1, 3, 4, 5)], "gather_start_0b", first[4])
    lands = [None] + [[landing(s, tok0) for s in layer_shards(i)] for i in range(1, L)]
    mod_g = mod_g.reshape(N_DEV, L, NB, Ca)
    mod_all = jnp.concatenate([mod_g[2 * s] for s in range(N_CHIPS)], axis=-1)
    mod = lax.dynamic_slice_in_dim(mod_all, dev * BL, BL, axis=1).reshape(L, BL, 9, D)

    def dn_weights(i):
        full = jnp.transpose(wts[i][4], (1, 0, 2)).reshape(D, -1)
        return full[:, :4 * W], jnp.pad(full[:, 4 * W:], ((0, 0), (0, LANES - 2 * H)))

    def row128(v):
        return jnp.pad(v.reshape(1, -1), ((0, 0), (0, LANES - v.shape[-1])))

    def pad_taps(w):
        return jnp.pad(w, ((0, 1), (0, 0)))

    saved = []
    xs = x
    after = mod
    for i in range(L):
        tok = 0.0
        if i == 0:
            wl = wts[0] = [None] * 6
            wl[0], wl[2] = pair_forward(gather_wait(first, after, "gather_wait_0a"))
        else:
            wl = wts[i] = pair_forward(gather_wait(handle, after, "gather_wait_%d" % i))
            if i + 1 < L:
                handle = gather_start(lands[i + 1], "gather_start_%d" % (i + 1), wl[0])
                tok = handle[4][0, 0]
        sv = {}
        m3 = [mod[i, :, 3 * j:3 * j + 3] + tok for j in range(3)]
        gs = [norm_g_full[i, j].reshape(1, D) for j in range(3)]
        sv["x0"] = xs
        xs, sv["y0"], sv["h0"], sv["gu0"] = ffn_fwd(xs, m3[0], gs[0], wl[0], wl[2])
        sv["x1"] = xs
        if i == 0:
            wl[1], wl[3], wl[4], wl[5] = pair_forward(gather_wait(rest, xs, "gather_wait_0b"))
            handle = gather_start(lands[1], "gather_start_1", wl[1])
            m3 = [m + handle[4][0, 0] for m in m3]
        if i % 2 == 0:
            a = i // 2
            sv["u"] = conv_glu_fwd(xs, m3[1], gs[1], wl[4], cm_b_glu[a].reshape(1, -1))
            xs, sv["y1"], sv["u2"] = conv_out_fwd(
                xs, sv["u"], m3[1], pad_taps(w_dw_full[a]), cm_b_dw[a].reshape(1, D), cm_ln_g[a].reshape(1, D),
                cm_ln_b[a].reshape(1, D), wl[5].reshape(D, D), cm_b_pw[a].reshape(1, D))
        else:
            a = i // 2
            w_main, w_ab = dn_weights(i)
            sv["pre"], sv["z"], sv["ab"] = dn_proj_fwd(xs, m3[1], gs[1], w_main, w_ab)
            qkvgb = dn_conv_fwd(sv["pre"], sv["ab"], w_sconv_full[a], row128(dn_a_log[a]), row128(dn_dt_bias[a]), H)
            sv["qkvgb"] = qkvgb
            sv["o"], sv["sp"], sv["inv"] = dn_chunk_fwd(*qkvgb)
            xs, sv["y1"] = dn_out_fwd(xs, sv["o"], sv["z"], m3[1], dn_o_g[a].reshape(1, Dh), wl[5].reshape(W, D))
        sv["x2"] = xs
        xs, sv["y2"], sv["h2"], sv["gu2"] = ffn_fwd(xs, m3[2], gs[2], wl[1], wl[3])
        saved.append(sv)
        after = xs

    dx, d_final_g, loss_part = final_loss(xs, final_g.reshape(1, D), loss_target)

    g_small = {n: None for n in names}
    d_norm_g = [[None] * 3 for _ in range(L)]
    dmod = [[None] * 3 for _ in range(L)]
    g_cm = {k: [None] * n_cm for k in ("b_glu", "w_dw", "b_dw", "ln_g", "ln_b", "b_pw")}
    g_dn = {k: [None] * n_dn for k in ("w_sconv", "a_log", "dt_bias", "o_g")}
    big_names = ("w_ffn_in", "w_ffn_out", "cm_w_glu", "cm_w_pw", "dn_w_in", "dn_w_out")
    stacks = {n: lax.empty((weights[n].size // (weights[n].shape[-2] * weights[n].shape[-1]),) + weights[n].shape[-2:], F32)
              for n in big_names}

    def targets(i, which):
        mix = ("cm_w_glu", "cm_w_pw") if i % 2 == 0 else ("dn_w_in", "dn_w_out")
        full = [("w_ffn_in", 2 * i), ("w_ffn_in", 2 * i + 1), ("w_ffn_out", 2 * i), ("w_ffn_out", 2 * i + 1),
                (mix[0], i // 2), (mix[1], i // 2)]
        return [full[k] for k in which]

    def ffn_back(i, j, slot, dx, tok=0.0):
        wl, sv = wts[i], saved[i]
        m3 = mod[i, :, 3 * j:3 * j + 3] + tok
        g = norm_g_full[i, j].reshape(1, D)
        gu = sv["gu%d" % j]
        ab_, dgu, dyb, dh0, dgate = ffn_bwd_part(0, dx, gu, m3, wl[slot], wl[2 + slot], y=sv["y%d" % j])
        dx, ab_, dgu, dm, dg = ffn_bwd_part(1, dx, gu, m3, wl[slot], wl[2 + slot], first=(ab_, dgu, dyb, dh0),
                                            x=sv["x%d" % j], g=g)
        dm = dm.at[:, 2:3, :].set(dgate)
        hb = sv["h%d" % j]
        dmod[i][j] = dm
        d_norm_g[i][j] = jnp.sum(dg, axis=(0, 1))
        Fc = wl[slot].shape[2]
        dw_in = matmul_tn(hb.reshape(-1, D), dgu.reshape(2, BL * T, 2 * Fc), Fc, "dw_ffn_in")
        dw_out = matmul_tn(ab_.reshape(-1, 2 * Fc), dyb.reshape(1, -1, D), D, "dw_ffn_out")
        return dx, dw_in, dw_out.reshape(N_CHIPS, -1, D)

    pending, tok = None, 0.0
    for i in reversed(range(L)):
        wl, sv = wts[i], saved[i]
        a = i // 2
        dx, dw_in1, dw_out1 = ffn_back(i, 2, 1, dx, tok)
        m3 = mod[i, :, 3:6]
        g = norm_g_full[i, 1].reshape(1, D)
        if i % 2 == 0:
            w_pw = wl[5].reshape(D, D)
            wdw = pad_taps(w_dw_full[a])
            du2, u3b, dyb, dgate, vec = conv_out_bwd(dx, sv["y1"], sv["u2"], m3, cm_ln_g[a].reshape(1, D),
                                                     cm_ln_b[a].reshape(1, D), w_pw)
            dx, hb, dab, dwdw, dbglu, dm, dg = conv_glu_bwd(sv["x1"], dx, du2, sv["u"], m3, g, wl[4],
                                                            cm_b_glu[a].reshape(1, -1), wdw)
            dm = dm.at[:, 2:3, :].set(dgate)
            vec = jnp.sum(vec, axis=0)
            g_cm["b_pw"][a], g_cm["ln_g"][a], g_cm["ln_b"][a], g_cm["b_dw"][a] = vec[0], vec[1], vec[2], vec[3]
            g_cm["w_dw"][a] = jnp.sum(dwdw, axis=0)[:KC]
            g_cm["b_glu"][a] = jnp.sum(dbglu, axis=(0, 1))
            dw_a = matmul_tn(hb.reshape(-1, D), dab.reshape(1, -1, 2 * D), D // 2, "dw_glu")
            dw_b = matmul_tn(u3b.reshape(-1, D), dyb.reshape(1, -1, D), D, "dw_sq").reshape(N_CHIPS, -1, D)
        else:
            w_main, w_ab = dn_weights(i)
            w_out = wl[5].reshape(W, D)
            do, dz, ogb, dyb, dgate, dog = dn_out_bwd(dx, sv["y1"], sv["o"], sv["z"], m3, dn_o_g[a].reshape(1, Dh), w_out)
            dq, dk, dv, dgb, dbb = dn_chunk_bwd(*sv["qkvgb"], sv["sp"], sv["inv"], do)
            dc, dab, small = dn_conv_bwd(dq, dk, dv, dgb, dbb, sv["pre"], sv["ab"], w_sconv_full[a],
                                         row128(dn_a_log[a]), row128(dn_dt_bias[a]))
            dx, hb, dproj, dws, dm, dg = dn_proj_bwd(sv["x1"], dx, dc, sv["pre"], dz, dab, m3, g, w_main, w_ab,
                                                     w_sconv_full[a])
            dm = dm.at[:, 2:3, :].set(dgate)
            small = jnp.sum(small, axis=0)
            g_dn["a_log"][a], g_dn["dt_bias"][a] = small[0, :H], small[1, :H]
            g_dn["o_g"][a] = jnp.sum(dog, axis=(0, 1))
            g_dn["w_sconv"][a] = jnp.sum(dws, axis=0)
            dw_main = matmul_tn(hb.reshape(-1, D), dproj.reshape(1, -1, 4 * W), W, "dw_dn_main")
            dw_ab = matmul_tn(hb.reshape(-1, D), dab.reshape(1, -1, LANES), LANES, "dw_dn_ab")
            full = jnp.concatenate([jnp.transpose(dw_main, (1, 0, 2)).reshape(D, 4 * W), dw_ab[0][:, :2 * H]], axis=1)
            dw_a = jnp.transpose(full.reshape(D, N_CHIPS, -1), (1, 0, 2))
            dw_b = matmul_tn(ogb.reshape(-1, W), dyb.reshape(1, -1, D), D, "dw_sq").reshape(N_CHIPS, -1, D)
        dmod[i][1] = dm
        d_norm_g[i][1] = jnp.sum(dg, axis=(0, 1))
        if i > 0:
            dx, dw_in0, dw_out0 = ffn_back(i, 0, 0, dx)
            started = reduce_start([dw_in0, dw_in1, dw_out0, dw_out1, dw_a, dw_b], c_idx, "reduce_start_%d" % i)
            if pending is not None:
                stacks = reduce_finish(pending[0], dx, where, "reduce_wait_%d" % pending[1], stacks,
                                       targets(pending[1], range(6)))
            pending, tok = (started, i), started[4][0, 0]
        else:
            part_a = reduce_start([dw_in1, dw_out1, dw_a, dw_b], c_idx, "reduce_start_0a")
            if pending is not None:
                stacks = reduce_finish(pending[0], dx, where, "reduce_wait_%d" % pending[1], stacks,
                                       targets(pending[1], range(6)))
            dx, dw_in0, dw_out0 = ffn_back(0, 0, 0, dx, part_a[4][0, 0])
            stacks = reduce_finish(part_a, dx, where, "reduce_wait_0a", stacks, targets(0, (1, 3, 4, 5)))

    part = dict(
        norm_g=jnp.stack([jnp.stack(r) for r in d_norm_g]),
        cm_b_glu=jnp.stack(g_cm["b_glu"]), cm_w_dw=jnp.stack(g_cm["w_dw"]), cm_b_dw=jnp.stack(g_cm["b_dw"]),
        cm_ln_g=jnp.stack(g_cm["ln_g"]), cm_ln_b=jnp.stack(g_cm["ln_b"]), cm_b_pw=jnp.stack(g_cm["b_pw"]),
        dn_w_sconv=jnp.stack(g_dn["w_sconv"]), dn_a_log=jnp.stack(g_dn["a_log"]), dn_dt_bias=jnp.stack(g_dn["dt_bias"]),
        dn_o_g=jnp.stack(g_dn["o_g"]), final_g=jnp.sum(d_final_g, axis=(0, 1)),
        loss=jnp.sum(loss_part[:, 0, 0]).reshape(1))
    dmod_loc = jnp.stack([jnp.concatenate(r, axis=1) for r in dmod]).reshape(L, BL, C9)
    keys = list(part)
    packed = _pack([part[k] for k in keys] + [dmod_loc])
    R = packed.shape[0]
    gathered = allgather8(packed).reshape(N_DEV, R, LANES)
    summed = _unpack(sum_devices(gathered).reshape(-1), [part[k].shape for k in keys])
    tot = dict(zip(keys, summed))
    n_small = sum(int(part[k].size) for k in keys)
    dmod_all = gathered.reshape(N_DEV, -1)[:, n_small:n_small + L * BL * C9].reshape(N_DEV, L, BL, C9)
    dmod_all = jnp.transpose(dmod_all, (1, 0, 2, 3)).reshape(L, NB, C9)
    dmod_cols = lax.dynamic_slice_in_dim(dmod_all, chip * Ca, Ca, axis=2)
    g_w_ada, g_b_ada = ada_bwd(c_all, dmod_cols, dmod_all)
    delta, new_m, new_v = {}, {}, {}
    part_b = reduce_start([dw_in0, dw_out0], c_idx, "reduce_start_0b", g_w_ada)
    delta["w_ada"], new_m["w_ada"], new_v["w_ada"] = _adamw_any(w_ada, g_w_ada, m_w_ada, v_w_ada, "adamw_w_ada",
                                                                 part_b[4])
    stacks = reduce_finish(part_b, new_v["w_ada"], where, "reduce_wait_0b", stacks, targets(0, (0, 2)))

    def my_cols(full):
        n = full.shape[-1] // N_CHIPS
        return lax.dynamic_slice_in_dim(full, chip * n, n, axis=full.ndim - 1)

    grads = dict(
        norm_g=my_cols(tot["norm_g"]), w_ada=g_w_ada, b_ada=g_b_ada.reshape(L, C9),
        cm_b_glu=tot["cm_b_glu"], cm_w_dw=my_cols(tot["cm_w_dw"]), cm_b_dw=tot["cm_b_dw"], cm_ln_g=tot["cm_ln_g"],
        cm_ln_b=tot["cm_ln_b"], cm_b_pw=tot["cm_b_pw"], dn_w_sconv=my_cols(tot["dn_w_sconv"]),
        dn_a_log=tot["dn_a_log"], dn_dt_bias=tot["dn_dt_bias"], dn_o_g=tot["dn_o_g"], final_g=tot["final_g"],
        **{n: stacks[n].reshape(weights[n].shape) for n in big_names})

    large = ("w_ada", "w_ffn_in", "w_ffn_out", "cm_w_glu", "cm_w_pw", "dn_w_in", "dn_w_out")
    for n in large[1:]:
        delta[n], new_m[n], new_v[n] = _adamw_any(weights[n], grads[n], mom_m[n], mom_v[n], "adamw_" + n)
    rest = [n for n in names if n not in large]
    shapes = [weights[n].shape for n in rest]
    pd, pm, pv = adamw(_pack([weights[n] for n in rest]), _pack([grads[n] for n in rest]),
                       _pack([mom_m[n] for n in rest]), _pack([mom_v[n] for n in rest]), "adamw_small")
    for n, d_, m_, v_ in zip(rest, _unpack(pd.reshape(-1), shapes), _unpack(pm.reshape(-1), shapes),
                             _unpack(pv.reshape(-1), shapes)):
        delta[n], new_m[n], new_v[n] = d_, m_, v_

    return (tot["loss"].reshape(()), dx, *[grads[n] for n in names], *[delta[n] for n in names],
            *[new_m[n] for n in names], *[new_v[n] for n in names])
```

```python
import functools

import jax
import jax.numpy as jnp
from jax import lax
from jax.experimental import pallas as pl
from jax.experimental.pallas import tpu as pltpu

F32 = jnp.float32
BF16 = jnp.bfloat16
EPS = 1e-6
CHUNK = 64
CHUNKS_PER_STEP = 4
N_CHIPS = 4
N_DEV = 8
LANES = 128
SUBLANES = 8
CONV_HALO = 32
SCONV_HALO = 8
VMEM_LIMIT_V7X = 60 * 1024 * 1024
DW_VMEM_BUDGET = 40 * 1024 * 1024
HI = lax.Precision.HIGHEST
MESH = pl.DeviceIdType.MESH
HBM_SPEC = pl.BlockSpec(memory_space=pltpu.HBM)

ADAM_LR, ADAM_B1, ADAM_B2, ADAM_EPS, ADAM_WD, ADAM_STEP = 0.001, 0.9, 0.999, 1e-08, 0.01, 10


def _cparams(n_axes):
    return pltpu.CompilerParams(dimension_semantics=("arbitrary",) * n_axes, vmem_limit_bytes=VMEM_LIMIT_V7X)


def _tile(n, pref, mult=8):
    for t in range(min(n, pref) // mult * mult, 0, -mult):
        if n % t == 0:
            return t
    return n


def _mm(a, b):
    return lax.dot_general(a.astype(BF16), b.astype(BF16), (((1,), (0,)), ((), ())), preferred_element_type=F32)


def _mm_nt(a, b):
    return lax.dot_general(a.astype(BF16), b.astype(BF16), (((1,), (1,)), ((), ())), preferred_element_type=F32)


def _mm_tn(a, b):
    return lax.dot_general(a.astype(BF16), b.astype(BF16), (((0,), (0,)), ((), ())), preferred_element_type=F32)


def _sigmoid(x):
    return jax.nn.sigmoid(x)


def _dsilu(x, s):
    return s * (1.0 + x * (1.0 - s))


def _softplus(x):
    return jnp.maximum(x, 0.0) + jnp.log(1.0 + jnp.exp(-jnp.abs(x)))


def _modnorm(x, g, scale, shift):
    r = lax.rsqrt(jnp.mean(x * x, axis=-1, keepdims=True) + EPS)
    return (x * r) * g * (1.0 + scale) + shift


def _modnorm_bwd(x, g, scale, dh):
    r = lax.rsqrt(jnp.mean(x * x, axis=-1, keepdims=True) + EPS)
    xn = x * r
    dshift = jnp.sum(dh, axis=0, keepdims=True)
    dscale = jnp.sum(dh * (xn * g), axis=0, keepdims=True)
    dhn = dh * (1.0 + scale)
    dg = jnp.sum(dhn * xn, axis=0, keepdims=True)
    dxn = dhn * g
    dx = r * (dxn - xn * jnp.mean(dxn * xn, axis=-1, keepdims=True))
    return dx, dg, dscale, dshift


def _sum0(a):
    return jnp.sum(a, axis=0, keepdims=True)


def ffn_fwd(x, mod3, g, w_in, w_out):
    B, T, D = x.shape
    Fc = w_in.shape[2]
    w_in = w_in.reshape(2, 2, D, Fc)
    w_out = w_out.reshape(2, Fc, D)
    tm = _tile(T, 512)

    def half(h, wi_ref, wo_ref, gu_ref):
        gt = _mm(h, wi_ref[0])
        up = _mm(h, wi_ref[1])
        gu_ref[0] = gt.astype(BF16)
        gu_ref[1] = up.astype(BF16)
        return _mm(gt * _sigmoid(gt) * up, wo_ref[...])

    def body_a(x_ref, mod_ref, g_ref, wi_ref, wo_ref, h_ref, gu_ref, y0_ref):
        h = _modnorm(x_ref[...], g_ref[...], mod_ref[1:2, :], mod_ref[0:1, :]).astype(BF16)
        h_ref[...] = h
        y0_ref[...] = half(h, wi_ref, wo_ref, gu_ref)

    def body_b(x_ref, h_ref, y0_ref, mod_ref, wi_ref, wo_ref, gu_any, xo_ref, y_ref, gu_ref):
        y = y0_ref[...] + half(h_ref[...], wi_ref, wo_ref, gu_ref)
        y_ref[...] = y
        xo_ref[...] = x_ref[...] + 0.5 * (1.0 + mod_ref[2:3, :]) * y

    tok = pl.BlockSpec((None, tm, D), lambda b, t: (b, t, 0))
    per_b3 = pl.BlockSpec((None, 3, D), lambda b, t: (b, 0, 0))
    gu_shape = jax.ShapeDtypeStruct((2, B, T, 2 * Fc), BF16)

    def w_specs(part):
        return [pl.BlockSpec((2, None, D, Fc), lambda b, t: (0, part, 0, 0)),
                pl.BlockSpec((None, Fc, D), lambda b, t: (part, 0, 0))]

    def gu_spec(part):
        return pl.BlockSpec((2, None, tm, Fc), lambda b, t: (0, b, t, part))

    h, gu, y0 = pl.pallas_call(
        body_a, name="ffn_fwd_a", grid=(B, T // tm),
        in_specs=[tok, per_b3, pl.BlockSpec((1, D), lambda b, t: (0, 0))] + w_specs(0),
        out_specs=[tok, gu_spec(0), tok],
        out_shape=[jax.ShapeDtypeStruct((B, T, D), BF16), gu_shape, jax.ShapeDtypeStruct((B, T, D), F32)],
        compiler_params=_cparams(2),
    )(x, mod3, g, w_in, w_out)
    x_new, y, gu = pl.pallas_call(
        body_b, name="ffn_fwd_b", grid=(B, T // tm),
        in_specs=[tok, tok, tok, per_b3] + w_specs(1) + [pl.BlockSpec(memory_space=pl.ANY)],
        out_specs=[tok, tok, gu_spec(1)],
        out_shape=[jax.ShapeDtypeStruct((B, T, D), F32)] * 2 + [gu_shape],
        input_output_aliases={6: 2},
        compiler_params=_cparams(2),
    )(x, h, y0, mod3, w_in, w_out, gu)
    return x_new, y, h, gu


def ffn_bwd_part(part, dres, gu, mod3, w_in, w_out, first=None, y=None, x=None, g=None):
    B, T, D = dres.shape
    Fc = w_in.shape[2]
    F = 2 * Fc
    w_in = w_in.reshape(2, 2, D, Fc)
    w_out = w_out.reshape(2, Fc, D)
    tm = _tile(T, 256)

    def half(dy, gu_ref, wi_ref, wo_ref, a_ref, dgu_ref):
        gt = gu_ref[0].astype(F32)
        up = gu_ref[1].astype(F32)
        sg = _sigmoid(gt)
        silu = gt * sg
        a_ref[...] = (silu * up).astype(BF16)
        da = _mm_nt(dy, wo_ref[...])
        dup = (da * silu).astype(BF16)
        dgt = (da * up * _dsilu(gt, sg)).astype(BF16)
        dgu_ref[0] = dgt
        dgu_ref[1] = dup
        return _mm_nt(dgt, wi_ref[0]) + _mm_nt(dup, wi_ref[1])

    tok = pl.BlockSpec((None, tm, D), lambda b, t: (b, t, 0))
    per_b3 = pl.BlockSpec((None, 3, D), lambda b, t: (b, 0, 0))
    per_b1 = pl.BlockSpec((None, 1, D), lambda b, t: (b, 0, 0))
    gu_spec = pl.BlockSpec((2, None, tm, Fc), lambda b, t: (0, b, t, part))
    a_spec = pl.BlockSpec((None, tm, Fc), lambda b, t: (b, t, part))
    wi_spec = pl.BlockSpec((2, None, D, Fc), lambda b, t: (0, part, 0, 0))
    wo_spec = pl.BlockSpec((None, Fc, D), lambda b, t: (part, 0, 0))
    a_shape = jax.ShapeDtypeStruct((B, T, F), BF16)
    dgu_shape = jax.ShapeDtypeStruct((2, B, T, F), BF16)

    if part == 0:
        def body(dres_ref, y_ref, gu_ref, mod_ref, wi_ref, wo_ref, a_ref, dgu_ref, dy_ref, dh_ref, dgate_ref):
            @pl.when(pl.program_id(1) == 0)
            def _():
                dgate_ref[...] = jnp.zeros_like(dgate_ref)

            dres = dres_ref[...]
            dy = (0.5 * (1.0 + mod_ref[2:3, :]) * dres).astype(BF16)
            dy_ref[...] = dy
            dgate_ref[...] += _sum0(dres * (0.5 * y_ref[...]))
            dh_ref[...] = half(dy, gu_ref, wi_ref, wo_ref, a_ref, dgu_ref)

        return pl.pallas_call(
            body, name="ffn_bwd_a", grid=(B, T // tm),
            in_specs=[tok, tok, gu_spec, per_b3, wi_spec, wo_spec],
            out_specs=[a_spec, gu_spec, tok, tok, per_b1],
            out_shape=[a_shape, dgu_shape, jax.ShapeDtypeStruct((B, T, D), BF16), jax.ShapeDtypeStruct((B, T, D), F32),
                       jax.ShapeDtypeStruct((B, 1, D), F32)],
            compiler_params=_cparams(2),
        )(dres, y, gu, mod3, w_in, w_out)

    a_full, dgu_full, dy, dh0 = first

    def body(x_ref, dres_ref, dy_ref, dh0_ref, gu_ref, mod_ref, g_ref, wi_ref, wo_ref, a_any, dgu_any,
             dx_ref, a_ref, dgu_ref, dmod_ref, dg_ref):
        @pl.when(pl.program_id(1) == 0)
        def _():
            dmod_ref[...] = jnp.zeros_like(dmod_ref)
            dg_ref[...] = jnp.zeros_like(dg_ref)

        dh = dh0_ref[...] + half(dy_ref[...], gu_ref, wi_ref, wo_ref, a_ref, dgu_ref)
        dxn, dg, dscale, dshift = _modnorm_bwd(x_ref[...], g_ref[...], mod_ref[1:2, :], dh)
        dx_ref[...] = dres_ref[...] + dxn
        dmod_ref[0:1, :] += dshift
        dmod_ref[1:2, :] += dscale
        dg_ref[...] += dg

    return pl.pallas_call(
        body, name="ffn_bwd_b", grid=(B, T // tm),
        in_specs=[tok, tok, tok, tok, gu_spec, per_b3, pl.BlockSpec((1, D), lambda b, t: (0, 0)), wi_spec, wo_spec,
                  ANY_SPEC, ANY_SPEC],
        out_specs=[tok, a_spec, gu_spec, per_b3, per_b1],
        out_shape=[jax.ShapeDtypeStruct((B, T, D), F32), a_shape, dgu_shape, jax.ShapeDtypeStruct((B, 3, D), F32),
                   jax.ShapeDtypeStruct((B, 1, D), F32)],
        input_output_aliases={9: 1, 10: 2},
        compiler_params=_cparams(2),
    )(x, dres, dy, dh0, gu, mod3, g, w_in, w_out, a_full, dgu_full)


def matmul_tn(xm, ym, bm, name):
    N, K = xm.shape
    GY, _, MY = ym.shape
    per = MY // bm
    nb = GY * per
    fixed = K * bm * (4 + 2 * 2)
    tn = _tile(N, max(512, (DW_VMEM_BUDGET - fixed) // (2 * 2 * (K + bm))), 256)

    def body(x_ref, y_ref, o_ref, acc_s):
        n = pl.program_id(1)

        @pl.when(n == 0)
        def _():
            acc_s[...] = jnp.zeros_like(acc_s)

        acc_s[...] += _mm_tn(x_ref[...], y_ref[...])

        @pl.when(n == N // tn - 1)
        def _():
            o_ref[...] = acc_s[...].astype(BF16)

    return pl.pallas_call(
        body, name=name, grid=(nb, N // tn),
        in_specs=[pl.BlockSpec((tn, K), lambda m, n: (n, 0)),
                  pl.BlockSpec((None, tn, bm), lambda m, n: (m // per, n, m % per))],
        out_specs=pl.BlockSpec((None, K, bm), lambda m, n: (m, 0, 0)),
        out_shape=jax.ShapeDtypeStruct((nb, K, bm), BF16),
        scratch_shapes=[pltpu.VMEM((K, bm), F32)],
        compiler_params=_cparams(2),
    )(xm, ym)


def final_loss(x, fg, target):
    B, T, D = x.shape
    tm = _tile(T, 512)

    def body(x_ref, g_ref, t_ref, dx_ref, dfg_ref, loss_ref):
        t = pl.program_id(1)

        @pl.when(t == 0)
        def _():
            dfg_ref[...] = jnp.zeros_like(dfg_ref)
            loss_ref[...] = jnp.zeros_like(loss_ref)

        xv = x_ref[...]
        g = g_ref[...]
        r = lax.rsqrt(jnp.mean(xv * xv, axis=-1, keepdims=True) + EPS)
        xn = xv * r
        err = xn * g - t_ref[...]
        tok_loss = jnp.mean(err * err, axis=-1, keepdims=True)
        loss_ref[...] += 0.5 * jnp.sum(tok_loss, axis=0, keepdims=True)
        dy = err * (1.0 / D)
        dfg_ref[...] += _sum0(dy * xn)
        dxn = dy * g
        dx_ref[...] = r * (dxn - xn * jnp.mean(dxn * xn, axis=-1, keepdims=True))

    tok = pl.BlockSpec((None, tm, D), lambda b, t: (b, t, 0))
    return pl.pallas_call(
        body, name="final_loss", grid=(B, T // tm),
        in_specs=[tok, pl.BlockSpec((1, D), lambda b, t: (0, 0)), tok],
        out_specs=[tok, pl.BlockSpec((None, 1, D), lambda b, t: (b, 0, 0)),
                   pl.BlockSpec((None, 1, LANES), lambda b, t: (b, 0, 0))],
        out_shape=[jax.ShapeDtypeStruct((B, T, D), F32), jax.ShapeDtypeStruct((B, 1, D), F32),
                   jax.ShapeDtypeStruct((B, 1, LANES), F32)],
        compiler_params=_cparams(2),
    )(x, fg, target)


def _past_halo_spec(tm, halo, width):
    return pl.BlockSpec((None, halo, width), lambda b, t: (b, jnp.maximum(t * (tm // halo) - 1, 0), 0))


def _future_halo_spec(tm, halo, width, T):
    return pl.BlockSpec((None, halo, width), lambda b, t: (b, jnp.minimum((t + 1) * (tm // halo), T // halo - 1), 0))


def _fill_shifted(ext_s):
    n = ext_s.shape[1]
    for b in range(1, SUBLANES):
        ext_s[b, 0:n - SUBLANES, :] = ext_s[0, pl.ds(b, n - SUBLANES), :]


def _shifted(ext_s, offset, rows):
    a, b = divmod(offset, SUBLANES)
    return ext_s[b, pl.ds(SUBLANES * a, rows), :]


def _glu_fwd(h, w_ref, bias):
    D = h.shape[1]
    a = jnp.concatenate([_mm(h, w_ref[0]), _mm(h, w_ref[1])], axis=1) + bias[:, :D]
    b = jnp.concatenate([_mm(h, w_ref[2]), _mm(h, w_ref[3])], axis=1) + bias[:, D:]
    return a, b


def conv_glu_fwd(x, mod3, g, w_glu, b_glu):
    B, T, D = x.shape
    tm = _tile(T, 512)

    def body(x_ref, mod_ref, g_ref, w_ref, b_ref, u_ref):
        h = _modnorm(x_ref[...], g_ref[...], mod_ref[1:2, :], mod_ref[0:1, :]).astype(BF16)
        a, b = _glu_fwd(h, w_ref, b_ref[...])
        u_ref[...] = a * _sigmoid(b)

    tok = pl.BlockSpec((None, tm, D), lambda b, t: (b, t, 0))
    return pl.pallas_call(
        body, name="conv_glu_fwd", grid=(B, T // tm),
        in_specs=[tok, pl.BlockSpec((None, 3, D), lambda b, t: (b, 0, 0)),
                  pl.BlockSpec((1, D), lambda b, t: (0, 0)),
                  pl.BlockSpec((4, D, D // 2), lambda b, t: (0, 0, 0)),
                  pl.BlockSpec((1, 2 * D), lambda b, t: (0, 0))],
        out_specs=tok, out_shape=jax.ShapeDtypeStruct((B, T, D), F32),
        compiler_params=_cparams(2),
    )(x, mod3, g, w_glu, b_glu)


def _layer_norm_parts(u2):
    mu = jnp.mean(u2, axis=-1, keepdims=True)
    xc = u2 - mu
    rs = lax.rsqrt(jnp.mean(xc * xc, axis=-1, keepdims=True) + EPS)
    return xc * rs, rs


def conv_out_fwd(x, u, mod3, w_dw, b_dw, ln_g, ln_b, w_pw, b_pw):
    B, T, D = x.shape
    K = w_dw.shape[0] - 1
    tm = _tile(T, 512)

    def body(x_ref, u_ref, halo_ref, mod_ref, wdw_ref, bdw_ref, lg_ref, lb_ref, wpw_ref, bpw_ref,
             xo_ref, y_ref, u2_ref, ext_s):
        t = pl.program_id(1)
        ext_s[0, 0:CONV_HALO, :] = jnp.where(t > 0, halo_ref[...], 0.0)
        ext_s[0, CONV_HALO:, :] = u_ref[...]
        _fill_shifted(ext_s)
        acc = jnp.broadcast_to(bdw_ref[...], (tm, D))
        for k in range(K):
            acc = acc + wdw_ref[k:k + 1, :] * _shifted(ext_s, CONV_HALO - (K - 1) + k, tm)
        u2_ref[...] = acc
        xh, _ = _layer_norm_parts(acc)
        l = xh * lg_ref[...] + lb_ref[...]
        u3 = l * _sigmoid(l)
        y = _mm(u3, wpw_ref[...]) + bpw_ref[...]
        y_ref[...] = y
        xo_ref[...] = x_ref[...] + (1.0 + mod_ref[2:3, :]) * y

    tok = pl.BlockSpec((None, tm, D), lambda b, t: (b, t, 0))
    vec = pl.BlockSpec((1, D), lambda b, t: (0, 0))
    return pl.pallas_call(
        body, name="conv_out_fwd", grid=(B, T // tm),
        in_specs=[tok, tok, _past_halo_spec(tm, CONV_HALO, D), pl.BlockSpec((None, 3, D), lambda b, t: (b, 0, 0)),
                  pl.BlockSpec((K + 1, D), lambda b, t: (0, 0)), vec, vec, vec,
                  pl.BlockSpec((D, D), lambda b, t: (0, 0)), vec],
        out_specs=[tok, tok, tok], out_shape=[jax.ShapeDtypeStruct((B, T, D), F32)] * 3,
        scratch_shapes=[pltpu.VMEM((SUBLANES, tm + CONV_HALO, D), F32)],
        compiler_params=_cparams(2),
    )(x, u, u, mod3, w_dw, b_dw, ln_g, ln_b, w_pw, b_pw)


def conv_out_bwd(dres, y, u2, mod3, ln_g, ln_b, w_pw):
    B, T, D = dres.shape
    tm = _tile(T, 512)

    def body(dres_ref, y_ref, u2_ref, mod_ref, lg_ref, lb_ref, wpw_ref, du2_ref, u3_ref, dy_ref, dgate_ref, vec_ref):
        t = pl.program_id(1)

        @pl.when(t == 0)
        def _():
            dgate_ref[...] = jnp.zeros_like(dgate_ref)
            vec_ref[...] = jnp.zeros_like(vec_ref)

        dres = dres_ref[...]
        dy = (1.0 + mod_ref[2:3, :]) * dres
        dy_ref[...] = dy.astype(BF16)
        dgate_ref[...] += _sum0(dres * y_ref[...])
        xh, rs = _layer_norm_parts(u2_ref[...])
        lg = lg_ref[...]
        l = xh * lg + lb_ref[...]
        sg = _sigmoid(l)
        u3_ref[...] = (l * sg).astype(BF16)
        du3 = _mm_nt(dy, wpw_ref[...])
        dl = du3 * _dsilu(l, sg)
        dxh = dl * lg
        du2 = rs * (dxh - jnp.mean(dxh, axis=-1, keepdims=True) - xh * jnp.mean(dxh * xh, axis=-1, keepdims=True))
        du2_ref[...] = du2
        vec_ref[0:1, :] += _sum0(dy)
        vec_ref[1:2, :] += _sum0(dl * xh)
        vec_ref[2:3, :] += _sum0(dl)
        vec_ref[3:4, :] += _sum0(du2)

    tok = pl.BlockSpec((None, tm, D), lambda b, t: (b, t, 0))
    tokb = pl.BlockSpec((None, tm, D), lambda b, t: (b, t, 0))
    vec = pl.BlockSpec((1, D), lambda b, t: (0, 0))
    return pl.pallas_call(
        body, name="conv_out_bwd", grid=(B, T // tm),
        in_specs=[tok, tok, tok, pl.BlockSpec((None, 3, D), lambda b, t: (b, 0, 0)), vec, vec,
                  pl.BlockSpec((D, D), lambda b, t: (0, 0))],
        out_specs=[tok, tokb, tokb, pl.BlockSpec((None, 1, D), lambda b, t: (b, 0, 0)),
                   pl.BlockSpec((None, 4, D), lambda b, t: (b, 0, 0))],
        out_shape=[jax.ShapeDtypeStruct((B, T, D), F32), jax.ShapeDtypeStruct((B, T, D), BF16),
                   jax.ShapeDtypeStruct((B, T, D), BF16), jax.ShapeDtypeStruct((B, 1, D), F32),
                   jax.ShapeDtypeStruct((B, 4, D), F32)],
        compiler_params=_cparams(2),
    )(dres, y, u2, mod3, ln_g, ln_b, w_pw)


def conv_glu_bwd(x, dres, du2, u, mod3, g, w_glu, b_glu, w_dw):
    B, T, D = x.shape
    K = w_dw.shape[0] - 1
    tm = _tile(T, 256)
    nt = T // tm

    def body(x_ref, dres_ref, du2_ref, du2h_ref, u_ref, uh_ref, mod_ref, g_ref, w_ref, b_ref, wdw_ref,
             dx_ref, h_ref, dab_ref, dwdw_ref, dbglu_ref, dmod_ref, dg_ref, extu_s, extd_s):
        t = pl.program_id(1)

        @pl.when(t == 0)
        def _():
            dwdw_ref[...] = jnp.zeros_like(dwdw_ref)
            dbglu_ref[...] = jnp.zeros_like(dbglu_ref)
            dmod_ref[...] = jnp.zeros_like(dmod_ref)
            dg_ref[...] = jnp.zeros_like(dg_ref)

        du2 = du2_ref[...]
        extu_s[0, 0:CONV_HALO, :] = jnp.where(t > 0, uh_ref[...], 0.0)
        extu_s[0, CONV_HALO:, :] = u_ref[...]
        extd_s[0, 0:tm, :] = du2
        extd_s[0, tm:, :] = jnp.where(t < nt - 1, du2h_ref[...], 0.0)
        _fill_shifted(extu_s)
        _fill_shifted(extd_s)
        du = jnp.zeros((tm, D), F32)
        for k in range(K):
            du = du + wdw_ref[k:k + 1, :] * _shifted(extd_s, K - 1 - k, tm)
            dwdw_ref[k:k + 1, :] += _sum0(du2 * _shifted(extu_s, CONV_HALO - (K - 1) + k, tm))
        xv = x_ref[...]
        h = _modnorm(xv, g_ref[...], mod_ref[1:2, :], mod_ref[0:1, :]).astype(BF16)
        h_ref[...] = h
        a, b = _glu_fwd(h, w_ref, b_ref[...])
        sb = _sigmoid(b)
        da = du * sb
        db = du * a * sb * (1.0 - sb)
        dbglu_ref[:, 0:D] += _sum0(da)
        dbglu_ref[:, D:] += _sum0(db)
        da = da.astype(BF16)
        db = db.astype(BF16)
        dab_ref[:, 0:D] = da
        dab_ref[:, D:] = db
        Dh2 = D // 2
        dh = (_mm_nt(da[:, :Dh2], w_ref[0]) + _mm_nt(da[:, Dh2:], w_ref[1])
              + _mm_nt(db[:, :Dh2], w_ref[2]) + _mm_nt(db[:, Dh2:], w_ref[3]))
        dxn, dg, dscale, dshift = _modnorm_bwd(xv, g_ref[...], mod_ref[1:2, :], dh)
        dx_ref[...] = dres_ref[...] + dxn
        dmod_ref[0:1, :] += dshift
        dmod_ref[1:2, :] += dscale
        dg_ref[...] += dg

    tok = pl.BlockSpec((None, tm, D), lambda b, t: (b, t, 0))
    return pl.pallas_call(
        body, name="conv_glu_bwd", grid=(B, nt),
        in_specs=[tok, tok, tok, _future_halo_spec(tm, CONV_HALO, D, T), tok, _past_halo_spec(tm, CONV_HALO, D),
                  pl.BlockSpec((None, 3, D), lambda b, t: (b, 0, 0)), pl.BlockSpec((1, D), lambda b, t: (0, 0)),
                  pl.BlockSpec((4, D, D // 2), lambda b, t: (0, 0, 0)), pl.BlockSpec((1, 2 * D), lambda b, t: (0, 0)),
                  pl.BlockSpec((K + 1, D), lambda b, t: (0, 0))],
        out_specs=[tok, tok, pl.BlockSpec((None, tm, 2 * D), lambda b, t: (b, t, 0)),
                   pl.BlockSpec((None, K + 1, D), lambda b, t: (b, 0, 0)),
                   pl.BlockSpec((None, 1, 2 * D), lambda b, t: (b, 0, 0)),
                   pl.BlockSpec((None, 3, D), lambda b, t: (b, 0, 0)),
                   pl.BlockSpec((None, 1, D), lambda b, t: (b, 0, 0))],
        out_shape=[jax.ShapeDtypeStruct((B, T, D), F32), jax.ShapeDtypeStruct((B, T, D), BF16),
                   jax.ShapeDtypeStruct((B, T, 2 * D), BF16), jax.ShapeDtypeStruct((B, K + 1, D), F32),
                   jax.ShapeDtypeStruct((B, 1, 2 * D), F32), jax.ShapeDtypeStruct((B, 3, D), F32),
                   jax.ShapeDtypeStruct((B, 1, D), F32)],
        scratch_shapes=[pltpu.VMEM((SUBLANES, tm + CONV_HALO, D), F32)] * 2,
        compiler_params=_cparams(2),
    )(x, dres, du2, du2, u, u, mod3, g, w_glu, b_glu, w_dw)


def dn_proj_fwd(x, mod3, g, w_main, w_ab):
    B, T, D = x.shape
    W = w_main.shape[1] // 4
    tm = _tile(T, 512)

    def body(x_ref, mod_ref, g_ref, wm_ref, wab_ref, pre_ref, z_ref, ab_ref):
        h = _modnorm(x_ref[...], g_ref[...], mod_ref[1:2, :], mod_ref[0:1, :]).astype(BF16)
        for p in range(3):
            pre_ref[:, p * W:(p + 1) * W] = _mm(h, wm_ref[:, p * W:(p + 1) * W])
        z_ref[...] = _mm(h, wm_ref[:, 3 * W:])
        ab_ref[...] = _mm(h, wab_ref[...])

    return pl.pallas_call(
        body, name="dn_proj_fwd", grid=(B, T // tm),
        in_specs=[pl.BlockSpec((None, tm, D), lambda b, t: (b, t, 0)), pl.BlockSpec((None, 3, D), lambda b, t: (b, 0, 0)),
                  pl.BlockSpec((1, D), lambda b, t: (0, 0)), pl.BlockSpec((D, 4 * W), lambda b, t: (0, 0)),
                  pl.BlockSpec((D, LANES), lambda b, t: (0, 0))],
        out_specs=[pl.BlockSpec((None, tm, 3 * W), lambda b, t: (b, t, 0)),
                   pl.BlockSpec((None, tm, W), lambda b, t: (b, t, 0)),
                   pl.BlockSpec((None, tm, LANES), lambda b, t: (b, t, 0))],
        out_shape=[jax.ShapeDtypeStruct((B, T, 3 * W), F32), jax.ShapeDtypeStruct((B, T, W), F32),
                   jax.ShapeDtypeStruct((B, T, LANES), F32)],
        compiler_params=_cparams(2),
    )(x, mod3, g, w_main, w_ab)


def _sconv(ext_s, w_ref, tm, K):
    acc = w_ref[0:1, :] * ext_s[pl.ds(SCONV_HALO - (K - 1), tm), :]
    for k in range(1, K):
        acc = acc + w_ref[k:k + 1, :] * ext_s[pl.ds(SCONV_HALO - (K - 1) + k, tm), :]
    return acc


def _lane_col(val, lane, idx):
    return jnp.sum(jnp.where(lane == idx, val, 0.0), axis=1, keepdims=True)


def dn_conv_fwd(pre, ab, w_sconv, alog_row, dt_row, H):
    B, T, W3 = pre.shape
    W = W3 // 3
    Dh = W // H
    K = w_sconv.shape[0]
    tm = _tile(T, 512)

    def body(pre_ref, halo_ref, ab_ref, w_ref, alog_ref, dt_ref, q_ref, k_ref, v_ref, gb_ref, bb_ref, ext_s):
        t = pl.program_id(1)
        ext_s[0:SCONV_HALO, :] = jnp.where(t > 0, halo_ref[...], 0.0)
        ext_s[SCONV_HALO:, :] = pre_ref[...]
        cv = _sconv(ext_s, w_ref, tm, K)
        qkv = cv * _sigmoid(cv)
        ab = ab_ref[...]
        lane = lax.broadcasted_iota(jnp.int32, ab.shape, 1)
        g_all = -jnp.exp(alog_ref[...]) * _softplus(ab + dt_ref[...])
        beta_all = _sigmoid(ab)
        for h in range(H):
            q_ref[h] = qkv[:, h * Dh:(h + 1) * Dh]
            k_ref[h] = qkv[:, W + h * Dh:W + (h + 1) * Dh]
            v_ref[h] = qkv[:, 2 * W + h * Dh:2 * W + (h + 1) * Dh]
            gb_ref[h] = jnp.broadcast_to(_lane_col(g_all, lane, h), (tm, Dh))
            bb_ref[h] = jnp.broadcast_to(_lane_col(beta_all, lane, H + h), (tm, Dh))

    hm = pl.BlockSpec((None, H, tm, Dh), lambda b, t: (b, 0, t, 0))
    row = pl.BlockSpec((1, LANES), lambda b, t: (0, 0))
    return pl.pallas_call(
        body, name="dn_conv_fwd", grid=(B, T // tm),
        in_specs=[pl.BlockSpec((None, tm, W3), lambda b, t: (b, t, 0)), _past_halo_spec(tm, SCONV_HALO, W3),
                  pl.BlockSpec((None, tm, LANES), lambda b, t: (b, t, 0)),
                  pl.BlockSpec((K, W3), lambda b, t: (0, 0)), row, row],
        out_specs=[hm] * 5, out_shape=[jax.ShapeDtypeStruct((B, H, T, Dh), F32)] * 5,
        scratch_shapes=[pltpu.VMEM((tm + SCONV_HALO, W3), F32)],
        compiler_params=_cparams(2),
    )(pre, pre, ab, w_sconv, alog_row, dt_row)


def _bdot(spec):
    return lambda a, b: jnp.einsum(spec, a.astype(BF16), b.astype(BF16), preferred_element_type=F32)


_NN, _NT, _TN = "gij,gjk->gik", "gik,gjk->gij", "gki,gkj->gij"


def _make_bdots():
    nn_, nt_, tn_ = _bdot(_NN), _bdot(_NT), _bdot(_TN)

    @jax.custom_vjp
    def nn(a, b):
        return nn_(a, b)

    @jax.custom_vjp
    def nt(a, b):
        return nt_(a, b)

    @jax.custom_vjp
    def tn(a, b):
        return tn_(a, b)

    nn.defvjp(lambda a, b: (nn_(a, b), (a, b)), lambda r, d: (nt_(d, r[1]), tn_(r[0], d)))
    nt.defvjp(lambda a, b: (nt_(a, b), (a, b)), lambda r, d: (nn_(d, r[1]), tn_(d, r[0])))
    tn.defvjp(lambda a, b: (tn_(a, b), (a, b)), lambda r, d: (nt_(r[1], d), nn_(r[0], d)))
    return nn, nt, tn


def _unit_lower_inverse(A, known=None):
    hdot = functools.partial(jnp.einsum, precision=lax.Precision.HIGH, preferred_element_type=F32)
    C = A.shape[-1]

    def impl(A):
        eye = (lax.broadcasted_iota(jnp.int32, A.shape, 1) == lax.broadcasted_iota(jnp.int32, A.shape, 2)).astype(F32)
        Tm = eye - A
        Ap = A
        for _ in range(max(1, (C - 1).bit_length()) - 1):
            Ap = hdot(_NN, Ap, Ap)
            Tm = Tm + hdot(_NN, Tm, Ap)
        return Tm

    @jax.custom_vjp
    def inv(A, given):
        return impl(A) if known is None else given

    def fwd(A, given):
        Tm = impl(A) if known is None else given
        return Tm, Tm

    def bwd(Tm, dT):
        return -hdot(_NT, hdot(_TN, Tm, dT), Tm), jnp.zeros_like(Tm)

    inv.defvjp(fwd, bwd)
    return inv(A, A if known is None else known)


def _chunk_fn(q, k, v, gb, bb, S, inverse=None, with_inverse=False):
    nn, nt, tn = _make_bdots()
    G, C, Dh = q.shape
    hdot = functools.partial(jnp.einsum, precision=lax.Precision.HIGH, preferred_element_type=F32)
    q = q * lax.rsqrt(jnp.sum(q * q, axis=-1, keepdims=True) + EPS) * (Dh ** -0.5)
    k = k * lax.rsqrt(jnp.sum(k * k, axis=-1, keepdims=True) + EPS)
    row = lax.broadcasted_iota(jnp.int32, (G, C, C), 1)
    col = lax.broadcasted_iota(jnp.int32, (G, C, C), 2)
    causal = row >= col
    strict = row > col
    gc = hdot(_NN, causal.astype(F32), gb)
    spread = jnp.full((G, C, Dh), 1.0 / Dh, F32)
    gi = hdot(_NT, gc, spread)
    gj = hdot(_NT, spread, gc)
    decay = jnp.where(causal, jnp.exp(jnp.where(causal, gi - gj, 0.0)), 0.0)
    kb = k * bb
    vb = v * bb
    A = jnp.where(strict, nt(kb, k) * decay, 0.0)
    Tm = _unit_lower_inverse(A, inverse)
    eg = jnp.exp(gc)
    u = nn(Tm, vb)
    w = nn(Tm, kb * eg)
    qg = q * eg
    intra = nt(q, k) * decay
    glast = hdot(_NN, jnp.ones((G, C, C), F32), gb)
    kd = k * jnp.exp(glast - gc)
    v_new = u - nn(w, S)
    o = nn(qg, S) + nn(intra, v_new)
    egl = jnp.exp(glast)
    S_new = S * jnp.concatenate([egl] * (Dh // C), axis=1) + tn(kd, v_new)
    return (o, S_new, Tm) if with_inverse else (o, S_new)


def dn_chunk_fwd(q, k, v, gb, bb):
    B, H, T, Dh = q.shape
    NC = T // CHUNK
    NS = _tile(NC, CHUNKS_PER_STEP, 1)

    def body(q_ref, k_ref, v_ref, gb_ref, bb_ref, o_ref, sp_ref, inv_ref, S_s):
        @pl.when(pl.program_id(1) == 0)
        def _():
            S_s[...] = jnp.zeros_like(S_s)

        def one_chunk(j, carry):
            rows = pl.ds(pl.multiple_of(j * CHUNK, CHUNK), CHUNK)
            S = S_s[...]
            sp_ref[j] = S
            o, S_new, Tm = _chunk_fn(q_ref[:, rows, :], k_ref[:, rows, :], v_ref[:, rows, :], gb_ref[:, rows, :],
                                     bb_ref[:, rows, :], S, with_inverse=True)
            o_ref[:, rows, :] = o
            inv_ref[j] = Tm
            S_s[...] = S_new
            return carry

        lax.fori_loop(0, NS, one_chunk, 0)

    hm = pl.BlockSpec((None, H, NS * CHUNK, Dh), lambda b, n: (b, 0, n, 0))
    return pl.pallas_call(
        body, name="dn_chunk_fwd", grid=(B, NC // NS),
        in_specs=[hm] * 5,
        out_specs=[hm, pl.BlockSpec((None, NS, H, Dh, Dh), lambda b, n: (b, n, 0, 0, 0)),
                   pl.BlockSpec((None, NS, H, CHUNK, CHUNK), lambda b, n: (b, n, 0, 0, 0))],
        out_shape=[jax.ShapeDtypeStruct((B, H, T, Dh), F32), jax.ShapeDtypeStruct((B, NC, H, Dh, Dh), F32),
                   jax.ShapeDtypeStruct((B, NC, H, CHUNK, CHUNK), F32)],
        scratch_shapes=[pltpu.VMEM((H, Dh, Dh), F32)],
        compiler_params=_cparams(2),
    )(q, k, v, gb, bb)


def dn_chunk_bwd(q, k, v, gb, bb, s_prev, inv, do):
    B, H, T, Dh = q.shape
    NC = T // CHUNK
    NS = _tile(NC, CHUNKS_PER_STEP, 1)
    NG = NC // NS

    def body(q_ref, k_ref, v_ref, gb_ref, bb_ref, sp_ref, inv_ref, do_ref, dq_ref, dk_ref, dv_ref, dgb_ref, dbb_ref,
             dS_s):
        @pl.when(pl.program_id(1) == 0)
        def _():
            dS_s[...] = jnp.zeros_like(dS_s)

        def one_chunk(jj, carry):
            j = NS - 1 - jj
            rows = pl.ds(pl.multiple_of(j * CHUNK, CHUNK), CHUNK)
            _, vjp = jax.vjp(functools.partial(_chunk_fn, inverse=inv_ref[j]), q_ref[:, rows, :], k_ref[:, rows, :],
                             v_ref[:, rows, :], gb_ref[:, rows, :], bb_ref[:, rows, :], sp_ref[j])
            dq, dk, dv, dgb, dbb, dS = vjp((do_ref[:, rows, :], dS_s[...]))
            dq_ref[:, rows, :] = dq
            dk_ref[:, rows, :] = dk
            dv_ref[:, rows, :] = dv
            dgb_ref[:, rows, :] = dgb
            dbb_ref[:, rows, :] = dbb
            dS_s[...] = dS
            return carry

        lax.fori_loop(0, NS, one_chunk, 0)

    hm = pl.BlockSpec((None, H, NS * CHUNK, Dh), lambda b, n: (b, 0, NG - 1 - n, 0))
    return pl.pallas_call(
        body, name="dn_chunk_bwd", grid=(B, NG),
        in_specs=[hm] * 5 + [pl.BlockSpec((None, NS, H, Dh, Dh), lambda b, n: (b, NG - 1 - n, 0, 0, 0)),
                             pl.BlockSpec((None, NS, H, CHUNK, CHUNK), lambda b, n: (b, NG - 1 - n, 0, 0, 0)), hm],
        out_specs=[hm] * 5, out_shape=[jax.ShapeDtypeStruct((B, H, T, Dh), F32)] * 5,
        scratch_shapes=[pltpu.VMEM((H, Dh, Dh), F32)],
        compiler_params=_cparams(2),
    )(q, k, v, gb, bb, s_prev, inv, do)


def _head_norm(o, og):
    r = lax.rsqrt(jnp.mean(o * o, axis=-1, keepdims=True) + EPS)
    return o * r, r


def dn_out_fwd(x, o, z, mod3, o_g, w_out):
    B, T, D = x.shape
    _, H, _, Dh = o.shape
    W = H * Dh
    tm = _tile(T, 512)

    def body(x_ref, o_ref, z_ref, mod_ref, og_ref, w_ref, xo_ref, y_ref):
        parts = []
        for h in range(H):
            on, _ = _head_norm(o_ref[h], og_ref[...])
            zz = z_ref[:, h * Dh:(h + 1) * Dh]
            parts.append((on * og_ref[...] * (zz * _sigmoid(zz))).astype(BF16))
        y = _mm(jnp.concatenate(parts, axis=1), w_ref[...])
        y_ref[...] = y
        xo_ref[...] = x_ref[...] + (1.0 + mod_ref[2:3, :]) * y

    tok = pl.BlockSpec((None, tm, D), lambda b, t: (b, t, 0))
    return pl.pallas_call(
        body, name="dn_out_fwd", grid=(B, T // tm),
        in_specs=[tok, pl.BlockSpec((None, H, tm, Dh), lambda b, t: (b, 0, t, 0)),
                  pl.BlockSpec((None, tm, W), lambda b, t: (b, t, 0)), pl.BlockSpec((None, 3, D), lambda b, t: (b, 0, 0)),
                  pl.BlockSpec((1, Dh), lambda b, t: (0, 0)), pl.BlockSpec((W, D), lambda b, t: (0, 0))],
        out_specs=[tok, tok], out_shape=[jax.ShapeDtypeStruct((B, T, D), F32)] * 2,
        compiler_params=_cparams(2),
    )(x, o, z, mod3, o_g, w_out)


def dn_out_bwd(dres, y, o, z, mod3, o_g, w_out):
    B, T, D = dres.shape
    _, H, _, Dh = o.shape
    W = H * Dh
    tm = _tile(T, 512)

    def body(dres_ref, y_ref, o_ref, z_ref, mod_ref, og_ref, w_ref, do_ref, dz_ref, ogb_ref, dy_ref, dgate_ref, dog_ref):
        t = pl.program_id(1)

        @pl.when(t == 0)
        def _():
            dgate_ref[...] = jnp.zeros_like(dgate_ref)
            dog_ref[...] = jnp.zeros_like(dog_ref)

        dres = dres_ref[...]
        dy = ((1.0 + mod_ref[2:3, :]) * dres).astype(BF16)
        dy_ref[...] = dy
        dgate_ref[...] += _sum0(dres * y_ref[...])
        dog = _mm_nt(dy, w_ref[...])
        og = og_ref[...]
        for h in range(H):
            ov = o_ref[h]
            xn, r = _head_norm(ov, og)
            zz = z_ref[:, h * Dh:(h + 1) * Dh]
            sg = _sigmoid(zz)
            sz = zz * sg
            d = dog[:, h * Dh:(h + 1) * Dh]
            ogb_ref[:, h * Dh:(h + 1) * Dh] = (xn * og * sz).astype(BF16)
            dz_ref[:, h * Dh:(h + 1) * Dh] = d * (xn * og) * _dsilu(zz, sg)
            don = d * sz
            dog_ref[...] += _sum0(don * xn)
            dxn = don * og
            do_ref[h] = r * (dxn - xn * jnp.mean(dxn * xn, axis=-1, keepdims=True))

    tok = pl.BlockSpec((None, tm, D), lambda b, t: (b, t, 0))
    tokw = pl.BlockSpec((None, tm, W), lambda b, t: (b, t, 0))
    hm = pl.BlockSpec((None, H, tm, Dh), lambda b, t: (b, 0, t, 0))
    return pl.pallas_call(
        body, name="dn_out_bwd", grid=(B, T // tm),
        in_specs=[tok, tok, hm, tokw, pl.BlockSpec((None, 3, D), lambda b, t: (b, 0, 0)),
                  pl.BlockSpec((1, Dh), lambda b, t: (0, 0)), pl.BlockSpec((W, D), lambda b, t: (0, 0))],
        out_specs=[hm, tokw, tokw, tok, pl.BlockSpec((None, 1, D), lambda b, t: (b, 0, 0)),
                   pl.BlockSpec((None, 1, Dh), lambda b, t: (b, 0, 0))],
        out_shape=[jax.ShapeDtypeStruct((B, H, T, Dh), F32), jax.ShapeDtypeStruct((B, T, W), F32),
                   jax.ShapeDtypeStruct((B, T, W), BF16), jax.ShapeDtypeStruct((B, T, D), BF16),
                   jax.ShapeDtypeStruct((B, 1, D), F32), jax.ShapeDtypeStruct((B, 1, Dh), F32)],
        compiler_params=_cparams(2),
    )(dres, y, o, z, mod3, o_g, w_out)


def dn_conv_bwd(dq, dk, dv, dgb, dbb, pre, ab, w_sconv, alog_row, dt_row):
    B, H, T, Dh = dq.shape
    W = H * Dh
    W3 = 3 * W
    K = w_sconv.shape[0]
    tm = _tile(T, 256)

    def body(dq_ref, dk_ref, dv_ref, dgb_ref, dbb_ref, pre_ref, halo_ref, ab_ref, w_ref, alog_ref, dt_ref,
             dc_ref, dab_ref, small_ref, ext_s):
        t = pl.program_id(1)

        @pl.when(t == 0)
        def _():
            small_ref[...] = jnp.zeros_like(small_ref)

        ext_s[0:SCONV_HALO, :] = jnp.where(t > 0, halo_ref[...], 0.0)
        ext_s[SCONV_HALO:, :] = pre_ref[...]
        cv = _sconv(ext_s, w_ref, tm, K)
        dsl = _dsilu(cv, _sigmoid(cv))
        ab = ab_ref[...]
        lane = lax.broadcasted_iota(jnp.int32, ab.shape, 1)
        dg_all = jnp.zeros_like(ab)
        db_all = jnp.zeros_like(ab)
        for h in range(H):
            dc_ref[:, h * Dh:(h + 1) * Dh] = dq_ref[h] * dsl[:, h * Dh:(h + 1) * Dh]
            dc_ref[:, W + h * Dh:W + (h + 1) * Dh] = dk_ref[h] * dsl[:, W + h * Dh:W + (h + 1) * Dh]
            dc_ref[:, 2 * W + h * Dh:2 * W + (h + 1) * Dh] = dv_ref[h] * dsl[:, 2 * W + h * Dh:2 * W + (h + 1) * Dh]
            dg_all = dg_all + jnp.where(lane == h, jnp.sum(dgb_ref[h], axis=1, keepdims=True), 0.0)
            db_all = db_all + jnp.where(lane == H + h, jnp.sum(dbb_ref[h], axis=1, keepdims=True), 0.0)
        xa = ab + dt_ref[...]
        ea = -jnp.exp(alog_ref[...])
        g_all = ea * _softplus(xa)
        da = dg_all * ea * _sigmoid(xa)
        beta = _sigmoid(ab)
        dab_ref[...] = da + db_all * beta * (1.0 - beta)
        small_ref[0:1, :] += _sum0(dg_all * g_all)
        small_ref[1:2, :] += _sum0(da)

    hm = pl.BlockSpec((None, H, tm, Dh), lambda b, t: (b, 0, t, 0))
    row = pl.BlockSpec((1, LANES), lambda b, t: (0, 0))
    return pl.pallas_call(
        body, name="dn_conv_bwd", grid=(B, T // tm),
        in_specs=[hm] * 5 + [pl.BlockSpec((None, tm, W3), lambda b, t: (b, t, 0)), _past_halo_spec(tm, SCONV_HALO, W3),
                             pl.BlockSpec((None, tm, LANES), lambda b, t: (b, t, 0)),
                             pl.BlockSpec((K, W3), lambda b, t: (0, 0)), row, row],
        out_specs=[pl.BlockSpec((None, tm, W3), lambda b, t: (b, t, 0)), pl.BlockSpec((None, tm, LANES), lambda b, t: (b, t, 0)),
                   pl.BlockSpec((None, 2, LANES), lambda b, t: (b, 0, 0))],
        out_shape=[jax.ShapeDtypeStruct((B, T, W3), F32), jax.ShapeDtypeStruct((B, T, LANES), F32),
                   jax.ShapeDtypeStruct((B, 2, LANES), F32)],
        scratch_shapes=[pltpu.VMEM((tm + SCONV_HALO, W3), F32)],
        compiler_params=_cparams(2),
    )(dq, dk, dv, dgb, dbb, pre, pre, ab, w_sconv, alog_row, dt_row)


def dn_proj_bwd(x, dres, dc, pre, dz, dab, mod3, g, w_main, w_ab, w_sconv):
    B, T, D = x.shape
    W3 = dc.shape[2]
    W = W3 // 3
    K = w_sconv.shape[0]
    tm = _tile(T, 256)
    nt = T // tm

    def body(x_ref, dres_ref, dc_ref, dch_ref, pre_ref, preh_ref, dz_ref, dab_ref, mod_ref, g_ref, wm_ref, wab_ref, ws_ref,
             dx_ref, h_ref, dproj_ref, dws_ref, dmod_ref, dg_ref, extp_s, extd_s):
        t = pl.program_id(1)

        @pl.when(t == 0)
        def _():
            dws_ref[...] = jnp.zeros_like(dws_ref)
            dmod_ref[...] = jnp.zeros_like(dmod_ref)
            dg_ref[...] = jnp.zeros_like(dg_ref)

        dc = dc_ref[...]
        extp_s[0:SCONV_HALO, :] = jnp.where(t > 0, preh_ref[...], 0.0)
        extp_s[SCONV_HALO:, :] = pre_ref[...]
        extd_s[0:tm, :] = dc
        extd_s[tm:, :] = jnp.where(t < nt - 1, dch_ref[...], 0.0)
        dpre = jnp.zeros((tm, W3), F32)
        for k in range(K):
            dpre = dpre + ws_ref[k:k + 1, :] * extd_s[pl.ds(K - 1 - k, tm), :]
            dws_ref[k:k + 1, :] += _sum0(dc * extp_s[pl.ds(SCONV_HALO - (K - 1) + k, tm), :])
        dpre = dpre.astype(BF16)
        dzb = dz_ref[...].astype(BF16)
        dproj_ref[:, 0:W3] = dpre
        dproj_ref[:, W3:] = dzb
        dh = _mm_nt(dab_ref[...], wab_ref[...]) + _mm_nt(dzb, wm_ref[:, W3:])
        for p in range(3):
            dh = dh + _mm_nt(dpre[:, p * W:(p + 1) * W], wm_ref[:, p * W:(p + 1) * W])
        xv = x_ref[...]
        h_ref[...] = _modnorm(xv, g_ref[...], mod_ref[1:2, :], mod_ref[0:1, :]).astype(BF16)
        dxn, dg, dscale, dshift = _modnorm_bwd(xv, g_ref[...], mod_ref[1:2, :], dh)
        dx_ref[...] = dres_ref[...] + dxn
        dmod_ref[0:1, :] += dshift
        dmod_ref[1:2, :] += dscale
        dg_ref[...] += dg

    tok = pl.BlockSpec((None, tm, D), lambda b, t: (b, t, 0))
    tok3 = pl.BlockSpec((None, tm, W3), lambda b, t: (b, t, 0))
    return pl.pallas_call(
        body, name="dn_proj_bwd", grid=(B, nt),
        in_specs=[tok, tok, tok3, _future_halo_spec(tm, SCONV_HALO, W3, T), tok3, _past_halo_spec(tm, SCONV_HALO, W3),
                  pl.BlockSpec((None, tm, W), lambda b, t: (b, t, 0)), pl.BlockSpec((None, tm, LANES), lambda b, t: (b, t, 0)),
                  pl.BlockSpec((None, 3, D), lambda b, t: (b, 0, 0)), pl.BlockSpec((1, D), lambda b, t: (0, 0)),
                  pl.BlockSpec((D, 4 * W), lambda b, t: (0, 0)), pl.BlockSpec((D, LANES), lambda b, t: (0, 0)),
                  pl.BlockSpec((K, W3), lambda b, t: (0, 0))],
        out_specs=[tok, tok, pl.BlockSpec((None, tm, 4 * W), lambda b, t: (b, t, 0)),
                   pl.BlockSpec((None, K, W3), lambda b, t: (b, 0, 0)), pl.BlockSpec((None, 3, D), lambda b, t: (b, 0, 0)),
                   pl.BlockSpec((None, 1, D), lambda b, t: (b, 0, 0))],
        out_shape=[jax.ShapeDtypeStruct((B, T, D), F32), jax.ShapeDtypeStruct((B, T, D), BF16),
                   jax.ShapeDtypeStruct((B, T, 4 * W), BF16), jax.ShapeDtypeStruct((B, K, W3), F32),
                   jax.ShapeDtypeStruct((B, 3, D), F32), jax.ShapeDtypeStruct((B, 1, D), F32)],
        scratch_shapes=[pltpu.VMEM((tm + SCONV_HALO, W3), F32), pltpu.VMEM((tm + SCONV_HALO, W3), F32)],
        compiler_params=_cparams(2),
    )(x, dres, dc, dc, pre, pre, dz, dab, mod3, g, w_main, w_ab, w_sconv)


def ada_fwd(c_all, w_ada, b_cols):
    L, D, Ca = w_ada.shape
    NB = c_all.shape[0]

    def body(c_ref, w_ref, b_ref, o_ref):
        cv = c_ref[...]
        o_ref[...] = _mm(cv * _sigmoid(cv), w_ref[...]) + b_ref[...]

    return pl.pallas_call(
        body, name="ada_fwd", grid=(L,),
        in_specs=[pl.BlockSpec((NB, D), lambda i: (0, 0)), pl.BlockSpec((None, D, Ca), lambda i: (i, 0, 0)),
                  pl.BlockSpec((None, 1, Ca), lambda i: (i, 0, 0))],
        out_specs=pl.BlockSpec((None, NB, Ca), lambda i: (i, 0, 0)),
        out_shape=jax.ShapeDtypeStruct((L, NB, Ca), F32),
        compiler_params=_cparams(1),
    )(c_all, w_ada, b_cols)


def ada_bwd(c_all, dmod_cols, dmod_all):
    L, NB, Ca = dmod_cols.shape
    D = c_all.shape[1]
    C9 = dmod_all.shape[2]

    def body(c_ref, dc_ref, da_ref, gw_ref, gb_ref):
        cv = c_ref[...]
        gw_ref[...] = _mm_tn(cv * _sigmoid(cv), dc_ref[...])
        gb_ref[...] = _sum0(da_ref[...])

    return pl.pallas_call(
        body, name="ada_bwd", grid=(L,),
        in_specs=[pl.BlockSpec((NB, D), lambda i: (0, 0)), pl.BlockSpec((None, NB, Ca), lambda i: (i, 0, 0)),
                  pl.BlockSpec((None, NB, C9), lambda i: (i, 0, 0))],
        out_specs=[pl.BlockSpec((None, D, Ca), lambda i: (i, 0, 0)), pl.BlockSpec((None, 1, C9), lambda i: (i, 0, 0))],
        out_shape=[jax.ShapeDtypeStruct((L, D, Ca), F32), jax.ShapeDtypeStruct((L, 1, C9), F32)],
        compiler_params=_cparams(1),
    )(c_all, dmod_cols, dmod_all)


def adamw(w, g, m, v, name, token=None):
    R, C = w.shape
    tr = _tile(R, max(8, (1 << 19) // C))
    if token is None:
        token = jnp.zeros((8, LANES), F32)

    def body(w_ref, g_ref, m_ref, v_ref, t_ref, d_ref, mo_ref, vo_ref):
        gv = g_ref[...] + t_ref[0:1, 0:1]
        mn = ADAM_B1 * m_ref[...] + (1.0 - ADAM_B1) * gv
        vn = ADAM_B2 * v_ref[...] + (1.0 - ADAM_B2) * (gv * gv)
        m_hat = mn / (1.0 - ADAM_B1 ** ADAM_STEP)
        v_hat = vn / (1.0 - ADAM_B2 ** ADAM_STEP)
        d_ref[...] = -ADAM_LR * (m_hat / (jnp.sqrt(v_hat) + ADAM_EPS) + ADAM_WD * w_ref[...])
        mo_ref[...] = mn
        vo_ref[...] = vn

    blk = pl.BlockSpec((tr, C), lambda i: (i, 0))
    return pl.pallas_call(
        body, name=name, grid=(R // tr,), in_specs=[blk] * 4 + [pl.BlockSpec((8, LANES), lambda i: (0, 0))],
        out_specs=[blk] * 3, out_shape=[jax.ShapeDtypeStruct((R, C), F32)] * 3, compiler_params=_cparams(1),
    )(w, g, m, v, token)


def sum_devices(a):
    n, R, C = a.shape

    def body(a_ref, o_ref):
        s = a_ref[0]
        for d in range(1, n):
            s = s + a_ref[d]
        o_ref[...] = s

    return pl.pallas_call(
        body, name="sum_devices", out_shape=jax.ShapeDtypeStruct((R, C), F32),
        compiler_params=pltpu.CompilerParams(vmem_limit_bytes=VMEM_LIMIT_V7X),
    )(a)


def _place():
    x, y, c = lax.axis_index("x"), lax.axis_index("y"), lax.axis_index("c")
    return x, y, c


def _other_chips(x, y):
    return [(2 * (1 - x) + y, 1 - x, y), (2 * x + (1 - y), x, 1 - y), (2 * (1 - x) + (1 - y), 1 - x, 1 - y)]


def allgather8(block):
    m_per, n = block.shape

    def body(x_ref, out_ref, send_sems, recv_sems, local_sem):
        x, y, c = _place()
        me, sibling = (x, y, c), (x, y, 1 - c)
        chips = [(1 - x, y), (x, 1 - y), (1 - x, 1 - y)]

        def rows(px, py, pc):
            return out_ref.at[pl.ds((4 * px + 2 * py + pc) * m_per, m_per), :]

        def copy(k, blk, to, src=None):
            return pltpu.make_async_remote_copy(
                src_ref=rows(*blk) if src is None else src, dst_ref=rows(*blk),
                send_sem=send_sems.at[k], recv_sem=recv_sems.at[k], device_id=to, device_id_type=MESH)

        mine = pltpu.make_async_copy(x_ref, rows(*me), local_sem)
        mine.start()
        first = [copy(0, me, sibling, src=x_ref)]
        first += [copy(1 + j, me, (*chip, c), src=x_ref) for j, chip in enumerate(chips)]
        for cp in first:
            cp.start()
        passed = [copy(4 + j, (*chip, c), sibling) for j, chip in enumerate(chips)]
        for j, chip in enumerate(chips):
            copy(1 + j, (*chip, c), me).wait_recv()
            passed[j].start()
        copy(0, sibling, me).wait_recv()
        for j, chip in enumerate(chips):
            copy(4 + j, (*chip, 1 - c), me).wait_recv()
        for cp in first + passed:
            cp.wait_send()
        mine.wait()

    return pl.pallas_call(
        body, name="allgather8", out_shape=jax.ShapeDtypeStruct((N_DEV * m_per, n), block.dtype),
        in_specs=[pl.BlockSpec(memory_space=pltpu.VMEM)], out_specs=pl.BlockSpec(memory_space=pltpu.VMEM),
        scratch_shapes=[pltpu.SemaphoreType.DMA((7,)), pltpu.SemaphoreType.DMA((7,)), pltpu.SemaphoreType.DMA],
        compiler_params=pltpu.CompilerParams(vmem_limit_bytes=VMEM_LIMIT_V7X),
    )(block)


def _half(ref, c, rh):
    return ref.at[pl.ds(pl.multiple_of(c * rh, 16), rh), :]


def pair_exchange(grads):
    K = len(grads)

    def body(*refs):
        ins, outs = refs[:K], refs[K:2 * K]
        send_sems, recv_sems = refs[2 * K:]
        x, y, c = _place()
        sibling = (x, y, 1 - c)
        copies = []
        for k in range(K):
            n, r, _ = ins[k].shape
            rh = r // 2
            cp = pltpu.make_async_remote_copy(
                src_ref=ins[k].at[:, pl.ds(pl.multiple_of((1 - c) * rh, 16), rh), :], dst_ref=outs[k],
                send_sem=send_sems.at[k], recv_sem=recv_sems.at[k], device_id=sibling, device_id_type=MESH)
            cp.start()
            copies.append(cp)
        for cp in copies:
            cp.wait_recv()
        for cp in copies:
            cp.wait_send()

    return pl.pallas_call(
        body, name="pair_exchange",
        out_shape=[jax.ShapeDtypeStruct((g.shape[0], g.shape[1] // 2, g.shape[2]), g.dtype) for g in grads],
        in_specs=[HBM_SPEC] * K, out_specs=[HBM_SPEC] * K,
        scratch_shapes=[pltpu.SemaphoreType.DMA((K,))] * 2,
    )(*grads)


def pair_add(grad, recv, c_idx):
    n, r, C = grad.shape
    rh = r // 2
    tr = _tile(rh, max(16, (1 << 19) // C), 16)
    grad = grad.reshape(n, 2, rh, C)

    def body(c_ref, g_ref, r_ref, o_ref):
        o_ref[...] = (g_ref[...].astype(F32) + r_ref[...].astype(F32)).astype(BF16)

    return pl.pallas_call(
        body, name="pair_add",
        grid_spec=pltpu.PrefetchScalarGridSpec(
            num_scalar_prefetch=1, grid=(n, rh // tr),
            in_specs=[pl.BlockSpec((None, None, tr, C), lambda d, i, c_ref: (d, c_ref[0], i, 0)),
                      pl.BlockSpec((None, tr, C), lambda d, i, c_ref: (d, i, 0))],
            out_specs=pl.BlockSpec((None, tr, C), lambda d, i, c_ref: (d, i, 0))),
        out_shape=jax.ShapeDtypeStruct((n, rh, C), BF16), compiler_params=_cparams(2),
    )(c_idx, grad, recv)


def chip_sum(parts, got, where, stack, slot):
    _, rh, C = parts.shape
    tr = _tile(rh, max(16, (1 << 19) // C), 16)
    nt = rh // tr

    def body(w_ref, p_ref, g_ref, stack_any, o_ref):
        s = p_ref[...].astype(F32)
        for r in range(3):
            s = s + g_ref[r].astype(F32)
        o_ref[...] = s

    return pl.pallas_call(
        body, name="chip_sum",
        grid_spec=pltpu.PrefetchScalarGridSpec(
            num_scalar_prefetch=1, grid=(nt,),
            in_specs=[pl.BlockSpec((None, tr, C), lambda i, w_ref: (w_ref[0], i, 0)),
                      pl.BlockSpec((3, tr, C), lambda i, w_ref: (0, i, 0)),
                      pl.BlockSpec(memory_space=pl.ANY)],
            out_specs=pl.BlockSpec((None, tr, C), lambda i, w_ref: (slot, w_ref[1] * nt + i, 0))),
        out_shape=jax.ShapeDtypeStruct(stack.shape, F32), input_output_aliases={3: 0},
        compiler_params=_cparams(1),
    )(where, parts, got, stack)


def pair_share(stacks, slots):
    K = len(stacks)
    jobs = [(k, s) for k in range(K) for s in slots[k]]

    def body(*refs):
        ins, outs = refs[:K], refs[K:2 * K]
        send_sems, recv_sems = refs[2 * K:]
        x, y, c = _place()
        sibling = (x, y, 1 - c)
        started = []
        for n, (k, s) in enumerate(jobs):
            rh = ins[k].shape[1] // 2
            cp = pltpu.make_async_remote_copy(
                src_ref=_half(ins[k].at[s], c, rh), dst_ref=_half(outs[k].at[s], c, rh), send_sem=send_sems.at[n],
                recv_sem=recv_sems.at[n], device_id=sibling, device_id_type=MESH)
            cp.start()
            started.append(cp)
        for n, (k, s) in enumerate(jobs):
            rh = ins[k].shape[1] // 2
            theirs = _half(outs[k].at[s], 1 - c, rh)
            pltpu.make_async_remote_copy(
                src_ref=theirs, dst_ref=theirs, send_sem=send_sems.at[n], recv_sem=recv_sems.at[n],
                device_id=sibling, device_id_type=MESH).wait_recv()
        for cp in started:
            cp.wait_send()

    return pl.pallas_call(
        body, name="pair_share",
        out_shape=[jax.ShapeDtypeStruct(s.shape, s.dtype) for s in stacks],
        in_specs=[HBM_SPEC] * K, out_specs=[HBM_SPEC] * K, input_output_aliases={k: k for k in range(K)},
        scratch_shapes=[pltpu.SemaphoreType.DMA((len(jobs),))] * 2,
    )(*stacks)


SEM_SPEC = pl.BlockSpec(memory_space=pltpu.SEMAPHORE)
ANY_SPEC = pl.BlockSpec(memory_space=pl.ANY)
DATAFLOW = pltpu.SideEffectType.DATAFLOW_SIDE_EFFECTING


def _in_hbm(a):
    return pltpu.with_memory_space_constraint(a, pltpu.HBM)


def _ici_copies(srcs, dsts, send_sems, recv_sems, src_slice, dst_slice):
    x, y, c = _place()
    out = []
    for k in range(len(srcs)):
        for r, (pchip, px, py) in enumerate(_other_chips(x, y)):
            out.append(pltpu.make_async_remote_copy(
                src_ref=src_slice(srcs[k], r, pchip), dst_ref=dst_slice(dsts[k], r, pchip),
                send_sem=send_sems.at[3 * k + r], recv_sem=recv_sems.at[3 * k + r], device_id=(px, py, c),
                device_id_type=MESH))
    return out


def _pair_copies(srcs, dsts, send_sems, recv_sems, src_slice, dst_slice):
    x, y, c = _place()
    return [pltpu.make_async_remote_copy(
        src_ref=src_slice(srcs[k]), dst_ref=dst_slice(dsts[k]), send_sem=send_sems.at[k], recv_sem=recv_sems.at[k],
        device_id=(x, y, 1 - c), device_id_type=MESH) for k in range(len(srcs))]


def _exchange_start(bufs, lands, src_slice, dst_slice, name, after=None, copies=_ici_copies, per=3):
    K = len(bufs)
    same = lands is None
    n_thru = K if same else 2 * K
    n_in = n_thru + (after is not None)

    def body(*refs):
        ins = refs[:n_thru]
        send_sems, recv_sems = refs[n_in], refs[n_in + 1]
        token = refs[-1]
        srcs = ins[:K]
        dsts = srcs if same else ins[K:]
        for cp in copies(srcs, dsts, send_sems, recv_sems, src_slice, dst_slice):
            cp.start()
        token[...] = jnp.zeros_like(token)

    thru = list(bufs) + ([] if same else list(lands))
    res = pl.pallas_call(
        body, name=name,
        out_shape=[pltpu.SemaphoreType.DMA((per * K,)), pltpu.SemaphoreType.DMA((per * K,))]
        + [pltpu.HBM(a.shape, a.dtype) for a in thru] + [jax.ShapeDtypeStruct((8, LANES), F32)],
        in_specs=[HBM_SPEC] * n_thru + [ANY_SPEC] * (after is not None),
        out_specs=[SEM_SPEC, SEM_SPEC] + [HBM_SPEC] * n_thru + [pl.BlockSpec(memory_space=pltpu.VMEM)],
        input_output_aliases={i: 2 + i for i in range(n_thru)},
        compiler_params=pltpu.CompilerParams(has_side_effects=DATAFLOW),
    )(*[_in_hbm(a) for a in thru], *([] if after is None else [after]))
    return res[0], res[1], res[2:2 + K], (res[2:2 + K] if same else res[2 + K:2 + 2 * K]), res[-1]


def _exchange_wait(send_sems, recv_sems, bufs, lands, after, src_slice, dst_slice, name, copies=_ici_copies):
    K = len(bufs)
    same = lands is None
    n_thru = K if same else 2 * K

    def body(*refs):
        ins = refs[:n_thru]
        ssem, rsem = refs[n_thru], refs[n_thru + 1]
        srcs = ins[:K]
        dsts = srcs if same else ins[K:]
        started = copies(srcs, dsts, ssem, rsem, src_slice, dst_slice)
        for cp in started:
            cp.wait_send()
        for cp in started:
            cp.wait_recv()

    thru = list(bufs) + ([] if same else list(lands))
    res = pl.pallas_call(
        body, name=name,
        out_shape=[pltpu.HBM(a.shape, a.dtype) for a in thru],
        in_specs=[HBM_SPEC] * n_thru + [SEM_SPEC, SEM_SPEC, ANY_SPEC],
        out_specs=[HBM_SPEC] * n_thru,
        input_output_aliases={i: i for i in range(n_thru)},
        compiler_params=pltpu.CompilerParams(has_side_effects=DATAFLOW),
    )(*thru, send_sems, recv_sems, after)
    return res[:K], (res[:K] if same else res[K:])


def _own_half(ref, r, pchip):
    x, y, c = _place()
    return _half(ref.at[2 * x + y], c, ref.shape[1] // 2)


def _their_half(ref, r, pchip):
    _, _, c = _place()
    return _half(ref.at[pchip], c, ref.shape[1] // 2)


def gather_start(lands, name, after=None):
    return _exchange_start(lands, None, _own_half, _own_half, name, after)


def gather_wait(handle, after, name):
    ssem, rsem, lands, _, _ = handle
    return _exchange_wait(ssem, rsem, lands, None, after, _own_half, _their_half, name)[1]


def pair_forward(lands):
    K = len(lands)

    def body(*refs):
        ins, outs = refs[:K], refs[K:2 * K]
        send_sems, recv_sems = refs[2 * K:]
        x, y, c = _place()
        sibling = (x, y, 1 - c)
        started = []
        for k in range(K):
            rh = ins[k].shape[1] // 2
            for r, (pchip, _, _) in enumerate(_other_chips(x, y)):
                cp = pltpu.make_async_remote_copy(
                    src_ref=_half(ins[k].at[pchip], c, rh), dst_ref=_half(outs[k].at[pchip], c, rh),
                    send_sem=send_sems.at[k, r], recv_sem=recv_sems.at[k, r], device_id=sibling, device_id_type=MESH)
                cp.start()
                started.append(cp)
        for k in range(K):
            rh = ins[k].shape[1] // 2
            for r, (pchip, _, _) in enumerate(_other_chips(x, y)):
                theirs = _half(outs[k].at[pchip], 1 - c, rh)
                pltpu.make_async_remote_copy(
                    src_ref=theirs, dst_ref=theirs, send_sem=send_sems.at[k, r], recv_sem=recv_sems.at[k, r],
                    device_id=sibling, device_id_type=MESH).wait_recv()
        for cp in started:
            cp.wait_send()

    return pl.pallas_call(
        body, name="pair_forward",
        out_shape=[jax.ShapeDtypeStruct(s.shape, s.dtype) for s in lands],
        in_specs=[HBM_SPEC] * K, out_specs=[HBM_SPEC] * K, input_output_aliases={k: k for k in range(K)},
        scratch_shapes=[pltpu.SemaphoreType.DMA((K, 3))] * 2,
    )(*lands)


def _to_chip(ref, r, pchip):
    return ref.at[pchip]


def _from_relation(ref, r, pchip):
    return ref.at[r]


def _other_rows(ref):
    _, _, c = _place()
    rh = ref.shape[1] // 2
    return ref.at[:, pl.ds(pl.multiple_of((1 - c) * rh, 16), rh), :]


def _whole(ref):
    return ref


def pair_start(grads, name):
    lands = [lax.empty((g.shape[0], g.shape[1] // 2, g.shape[2]), g.dtype) for g in grads]
    return _exchange_start(grads, lands, _other_rows, _whole, name, copies=_pair_copies, per=1)


def pair_finish(handle, after, name):
    ssem, rsem, grads, lands, _ = handle
    return _exchange_wait(ssem, rsem, grads, lands, after, _other_rows, _whole, name, copies=_pair_copies)


def reduce_start(grads, c_idx, name, after=None, recv=None):
    if recv is None:
        recv = pair_exchange(grads)
    parts = [pair_add(g, r, c_idx) for g, r in zip(grads, recv)]
    lands = [lax.empty((3,) + p.shape[1:], p.dtype) for p in parts]
    return _exchange_start(parts, lands, _to_chip, _from_relation, name, after)


def reduce_finish(handle, after, where, name, stacks, targets):
    ssem, rsem, parts, lands, _ = handle
    parts, got = _exchange_wait(ssem, rsem, parts, lands, after, _to_chip, _from_relation, name)
    stacks = dict(stacks)
    for p, g, (key, slot) in zip(parts, got, targets):
        stacks[key] = chip_sum(p, g, where, stacks[key], slot)
    keys = list(dict.fromkeys(key for key, _ in targets))
    shared = pair_share([stacks[k] for k in keys], [[s for key, s in targets if key == k] for k in keys])
    stacks.update(zip(keys, shared))
    return stacks


def _pack(arrs):
    flat = jnp.concatenate([a.reshape(-1).astype(F32) for a in arrs])
    pad = (-flat.shape[0]) % (8 * LANES)
    return jnp.pad(flat, (0, pad)).reshape(-1, LANES)


def _unpack(flat, shapes):
    out, off = [], 0
    for s in shapes:
        n = 1
        for d in s:
            n *= d
        out.append(flat[off:off + n].reshape(s))
        off += n
    return out


def _adamw_any(w, g, m, v, name, token=None):
    shp = w.shape
    C = shp[-1]
    d, nm, nv = adamw(w.reshape(-1, C), g.reshape(-1, C), m.reshape(-1, C), v.reshape(-1, C), name, token)
    return d.reshape(shp), nm.reshape(shp), nv.reshape(shp)


def kernel(x, c, norm_g, w_ada, b_ada, w_ffn_in, w_ffn_out, cm_w_glu, cm_b_glu, cm_w_dw, cm_b_dw, cm_ln_g, cm_ln_b, cm_w_pw, cm_b_pw, dn_w_in, dn_w_sconv, dn_a_log, dn_dt_bias, dn_o_g, dn_w_out, final_g, loss_target, m_norm_g, m_w_ada, m_b_ada, m_w_ffn_in, m_w_ffn_out, m_cm_w_glu, m_cm_b_glu, m_cm_w_dw, m_cm_b_dw, m_cm_ln_g, m_cm_ln_b, m_cm_w_pw, m_cm_b_pw, m_dn_w_in, m_dn_w_sconv, m_dn_a_log, m_dn_dt_bias, m_dn_o_g, m_dn_w_out, m_final_g, v_norm_g, v_w_ada, v_b_ada, v_w_ffn_in, v_w_ffn_out, v_cm_w_glu, v_cm_b_glu, v_cm_w_dw, v_cm_b_dw, v_cm_ln_g, v_cm_ln_b, v_cm_w_pw, v_cm_b_pw, v_dn_w_in, v_dn_w_sconv, v_dn_a_log, v_dn_dt_bias, v_dn_o_g, v_dn_w_out, v_final_g):
    weights = dict(norm_g=norm_g, w_ada=w_ada, b_ada=b_ada, w_ffn_in=w_ffn_in, w_ffn_out=w_ffn_out, cm_w_glu=cm_w_glu,
                   cm_b_glu=cm_b_glu, cm_w_dw=cm_w_dw, cm_b_dw=cm_b_dw, cm_ln_g=cm_ln_g, cm_ln_b=cm_ln_b, cm_w_pw=cm_w_pw,
                   cm_b_pw=cm_b_pw, dn_w_in=dn_w_in, dn_w_sconv=dn_w_sconv, dn_a_log=dn_a_log, dn_dt_bias=dn_dt_bias,
                   dn_o_g=dn_o_g, dn_w_out=dn_w_out, final_g=final_g)
    mom_m = dict(norm_g=m_norm_g, w_ada=m_w_ada, b_ada=m_b_ada, w_ffn_in=m_w_ffn_in, w_ffn_out=m_w_ffn_out,
                 cm_w_glu=m_cm_w_glu, cm_b_glu=m_cm_b_glu, cm_w_dw=m_cm_w_dw, cm_b_dw=m_cm_b_dw, cm_ln_g=m_cm_ln_g,
                 cm_ln_b=m_cm_ln_b, cm_w_pw=m_cm_w_pw, cm_b_pw=m_cm_b_pw, dn_w_in=m_dn_w_in, dn_w_sconv=m_dn_w_sconv,
                 dn_a_log=m_dn_a_log, dn_dt_bias=m_dn_dt_bias, dn_o_g=m_dn_o_g, dn_w_out=m_dn_w_out, final_g=m_final_g)
    mom_v = dict(norm_g=v_norm_g, w_ada=v_w_ada, b_ada=v_b_ada, w_ffn_in=v_w_ffn_in, w_ffn_out=v_w_ffn_out,
                 cm_w_glu=v_cm_w_glu, cm_b_glu=v_cm_b_glu, cm_w_dw=v_cm_w_dw, cm_b_dw=v_cm_b_dw, cm_ln_g=v_cm_ln_g,
                 cm_ln_b=v_cm_ln_b, cm_w_pw=v_cm_w_pw, cm_b_pw=v_cm_b_pw, dn_w_in=v_dn_w_in, dn_w_sconv=v_dn_w_sconv,
                 dn_a_log=v_dn_a_log, dn_dt_bias=v_dn_dt_bias, dn_o_g=v_dn_o_g, dn_w_out=v_dn_w_out, final_g=v_final_g)
    names = list(weights)

    BL, T, D = x.shape
    L = norm_g.shape[0]
    NB = BL * N_DEV
    Ca = w_ada.shape[2]
    C9 = b_ada.shape[1]
    H = dn_a_log.shape[1]
    Dh = dn_o_g.shape[1]
    W = H * Dh
    KC = cm_w_dw.shape[1]
    KS = dn_w_sconv.shape[1]
    n_cm, n_dn = cm_w_glu.shape[0], dn_w_in.shape[0]
    ax, ay, ac = lax.axis_index("x"), lax.axis_index("y"), lax.axis_index("c")
    chip = 2 * ax + ay
    dev = 2 * chip + ac
    c_idx = ac.astype(jnp.int32).reshape(1)
    where = jnp.stack([chip, ac]).astype(jnp.int32)

    def landing(s, tok=None):
        s = s if tok is None else s + tok
        return lax.dynamic_update_slice(lax.empty((N_CHIPS,) + s.shape, BF16), s.astype(BF16)[None], (chip, 0, 0))

    def layer_shards(i):
        sh = [w_ffn_in[i, 0], w_ffn_in[i, 1], w_ffn_out[i, 0], w_ffn_out[i, 1]]
        if i % 2 == 0:
            sh += [cm_w_glu[i // 2], cm_w_pw[i // 2]]
        else:
            sh += [dn_w_in[i // 2], dn_w_out[i // 2]]
        return sh

    wts = [None] * L

    small_in = [c, norm_g, cm_w_dw, dn_w_sconv]
    gathered = allgather8(_pack(small_in)).reshape(N_DEV, -1)
    per_dev = [_unpack(gathered[d], [a.shape for a in small_in]) for d in range(N_DEV)]
    c_all = jnp.concatenate([p[0] for p in per_dev], axis=0)
    norm_g_full = jnp.concatenate([per_dev[2 * s][1] for s in range(N_CHIPS)], axis=-1)
    w_dw_full = jnp.concatenate([per_dev[2 * s][2] for s in range(N_CHIPS)], axis=-1)
    w_sconv_full = jnp.concatenate([per_dev[2 * s][3] for s in range(N_CHIPS)], axis=-1)

    b_cols = lax.dynamic_slice_in_dim(b_ada, chip * Ca, Ca, axis=1).reshape(L, 1, Ca)
    mod_part = ada_fwd(c_all, w_ada, b_cols)
    mod_g = allgather8(mod_part.reshape(-1, LANES))
    shards0 = layer_shards(0)
    first = gather_start([landing(shards0[0]), landing(shards0[2])], "gather_start_0a", mod_g)
    tok0 = first[4][0, 0]
    rest = gather_start([landing(shards0[k], tok0) for k in (1, 3, 4, 5)], "gather_start_0b", first[4])
    lands = [None] + [[landing(s, tok0) for s in layer_shards(i)] for i in range(1, L)]
    mod_g = mod_g.reshape(N_DEV, L, NB, Ca)
    mod_all = jnp.concatenate([mod_g[2 * s] for s in range(N_CHIPS)], axis=-1)
    mod = lax.dynamic_slice_in_dim(mod_all, dev * BL, BL, axis=1).reshape(L, BL, 9, D)

    def dn_weights(i):
        full = jnp.transpose(wts[i][4], (1, 0, 2)).reshape(D, -1)
        return full[:, :4 * W], jnp.pad(full[:, 4 * W:], ((0, 0), (0, LANES - 2 * H)))

    def row128(v):
        return jnp.pad(v.reshape(1, -1), ((0, 0), (0, LANES - v.shape[-1])))

    def pad_taps(w):
        return jnp.pad(w, ((0, 1), (0, 0)))

    saved = []
    xs = x
    after = mod
    for i in range(L):
        tok = 0.0
        if i == 0:
            wl = wts[0] = [None] * 6
            wl[0], wl[2] = pair_forward(gather_wait(first, after, "gather_wait_0a"))
        else:
            wl = wts[i] = pair_forward(gather_wait(handle, after, "gather_wait_%d" % i))
            if i + 1 < L:
                handle = gather_start(lands[i + 1], "gather_start_%d" % (i + 1), wl[0])
                tok = handle[4][0, 0]
        sv = {}
        m3 = [mod[i, :, 3 * j:3 * j + 3] + tok for j in range(3)]
        gs = [norm_g_full[i, j].reshape(1, D) for j in range(3)]
        sv["x0"] = xs
        xs, sv["y0"], sv["h0"], sv["gu0"] = ffn_fwd(xs, m3[0], gs[0], wl[0], wl[2])
        sv["x1"] = xs
        if i == 0:
            wl[1], wl[3], wl[4], wl[5] = pair_forward(gather_wait(rest, xs, "gather_wait_0b"))
            handle = gather_start(lands[1], "gather_start_1", wl[1])
            m3 = [m + handle[4][0, 0] for m in m3]
        if i % 2 == 0:
            a = i // 2
            sv["u"] = conv_glu_fwd(xs, m3[1], gs[1], wl[4], cm_b_glu[a].reshape(1, -1))
            xs, sv["y1"], sv["u2"] = conv_out_fwd(
                xs, sv["u"], m3[1], pad_taps(w_dw_full[a]), cm_b_dw[a].reshape(1, D), cm_ln_g[a].reshape(1, D),
                cm_ln_b[a].reshape(1, D), wl[5].reshape(D, D), cm_b_pw[a].reshape(1, D))
        else:
            a = i // 2
            w_main, w_ab = dn_weights(i)
            sv["pre"], sv["z"], sv["ab"] = dn_proj_fwd(xs, m3[1], gs[1], w_main, w_ab)
            qkvgb = dn_conv_fwd(sv["pre"], sv["ab"], w_sconv_full[a], row128(dn_a_log[a]), row128(dn_dt_bias[a]), H)
            sv["qkvgb"] = qkvgb
            sv["o"], sv["sp"], sv["inv"] = dn_chunk_fwd(*qkvgb)
            xs, sv["y1"] = dn_out_fwd(xs, sv["o"], sv["z"], m3[1], dn_o_g[a].reshape(1, Dh), wl[5].reshape(W, D))
        sv["x2"] = xs
        xs, sv["y2"], sv["h2"], sv["gu2"] = ffn_fwd(xs, m3[2], gs[2], wl[1], wl[3])
        saved.append(sv)
        after = xs

    dx, d_final_g, loss_part = final_loss(xs, final_g.reshape(1, D), loss_target)

    g_small = {n: None for n in names}
    d_norm_g = [[None] * 3 for _ in range(L)]
    dmod = [[None] * 3 for _ in range(L)]
    g_cm = {k: [None] * n_cm for k in ("b_glu", "w_dw", "b_dw", "ln_g", "ln_b", "b_pw")}
    g_dn = {k: [None] * n_dn for k in ("w_sconv", "a_log", "dt_bias", "o_g")}
    big_names = ("w_ffn_in", "w_ffn_out", "cm_w_glu", "cm_w_pw", "dn_w_in", "dn_w_out")
    stacks = {n: lax.empty((weights[n].size // (weights[n].shape[-2] * weights[n].shape[-1]),) + weights[n].shape[-2:], F32)
              for n in big_names}

    def targets(i, which):
        mix = ("cm_w_glu", "cm_w_pw") if i % 2 == 0 else ("dn_w_in", "dn_w_out")
        full = [("w_ffn_in", 2 * i), ("w_ffn_in", 2 * i + 1), ("w_ffn_out", 2 * i), ("w_ffn_out", 2 * i + 1),
                (mix[0], i // 2), (mix[1], i // 2)]
        return [full[k] for k in which]

    def ffn_back(i, j, slot, dx, tok=0.0):
        wl, sv = wts[i], saved[i]
        m3 = mod[i, :, 3 * j:3 * j + 3] + tok
        g = norm_g_full[i, j].reshape(1, D)
        gu = sv["gu%d" % j]
        ab_, dgu, dyb, dh0, dgate = ffn_bwd_part(0, dx, gu, m3, wl[slot], wl[2 + slot], y=sv["y%d" % j])
        dx, ab_, dgu, dm, dg = ffn_bwd_part(1, dx, gu, m3, wl[slot], wl[2 + slot], first=(ab_, dgu, dyb, dh0),
                                            x=sv["x%d" % j], g=g)
        dm = dm.at[:, 2:3, :].set(dgate)
        hb = sv["h%d" % j]
        dmod[i][j] = dm
        d_norm_g[i][j] = jnp.sum(dg, axis=(0, 1))
        Fc = wl[slot].shape[2]
        dw_in = matmul_tn(hb.reshape(-1, D), dgu.reshape(2, BL * T, 2 * Fc), Fc, "dw_ffn_in")
        dw_out = matmul_tn(ab_.reshape(-1, 2 * Fc), dyb.reshape(1, -1, D), D, "dw_ffn_out")
        return dx, dw_in, dw_out.reshape(N_CHIPS, -1, D)

    pending, paired, tok = None, None, 0.0
    for i in reversed(range(L)):
        wl, sv = wts[i], saved[i]
        a = i // 2
        dx, dw_in1, dw_out1 = ffn_back(i, 2, 1, dx, tok)
        m3 = mod[i, :, 3:6]
        if paired is not None:
            theirs, recv = pair_finish(paired[0], dx, "pair_wait_%d" % paired[1])
            started = reduce_start(theirs, c_idx, "reduce_start_%d" % paired[1], recv=recv)
            pending, paired = (started, paired[1]), None
            m3 = m3 + started[4][0, 0]
        g = norm_g_full[i, 1].reshape(1, D)
        if i % 2 == 0:
            w_pw = wl[5].reshape(D, D)
            wdw = pad_taps(w_dw_full[a])
            du2, u3b, dyb, dgate, vec = conv_out_bwd(dx, sv["y1"], sv["u2"], m3, cm_ln_g[a].reshape(1, D),
                                                     cm_ln_b[a].reshape(1, D), w_pw)
            dx, hb, dab, dwdw, dbglu, dm, dg = conv_glu_bwd(sv["x1"], dx, du2, sv["u"], m3, g, wl[4],
                                                            cm_b_glu[a].reshape(1, -1), wdw)
            dm = dm.at[:, 2:3, :].set(dgate)
            vec = jnp.sum(vec, axis=0)
            g_cm["b_pw"][a], g_cm["ln_g"][a], g_cm["ln_b"][a], g_cm["b_dw"][a] = vec[0], vec[1], vec[2], vec[3]
            g_cm["w_dw"][a] = jnp.sum(dwdw, axis=0)[:KC]
            g_cm["b_glu"][a] = jnp.sum(dbglu, axis=(0, 1))
            dw_a = matmul_tn(hb.reshape(-1, D), dab.reshape(1, -1, 2 * D), D // 2, "dw_glu")
            dw_b = matmul_tn(u3b.reshape(-1, D), dyb.reshape(1, -1, D), D, "dw_sq").reshape(N_CHIPS, -1, D)
        else:
            w_main, w_ab = dn_weights(i)
            w_out = wl[5].reshape(W, D)
            do, dz, ogb, dyb, dgate, dog = dn_out_bwd(dx, sv["y1"], sv["o"], sv["z"], m3, dn_o_g[a].reshape(1, Dh), w_out)
            dq, dk, dv, dgb, dbb = dn_chunk_bwd(*sv["qkvgb"], sv["sp"], sv["inv"], do)
            dc, dab, small = dn_conv_bwd(dq, dk, dv, dgb, dbb, sv["pre"], sv["ab"], w_sconv_full[a],
                                         row128(dn_a_log[a]), row128(dn_dt_bias[a]))
            dx, hb, dproj, dws, dm, dg = dn_proj_bwd(sv["x1"], dx, dc, sv["pre"], dz, dab, m3, g, w_main, w_ab,
                                                     w_sconv_full[a])
            dm = dm.at[:, 2:3, :].set(dgate)
            small = jnp.sum(small, axis=0)
            g_dn["a_log"][a], g_dn["dt_bias"][a] = small[0, :H], small[1, :H]
            g_dn["o_g"][a] = jnp.sum(dog, axis=(0, 1))
            g_dn["w_sconv"][a] = jnp.sum(dws, axis=0)
            dw_main = matmul_tn(hb.reshape(-1, D), dproj.reshape(1, -1, 4 * W), W, "dw_dn_main")
            dw_ab = matmul_tn(hb.reshape(-1, D), dab.reshape(1, -1, LANES), LANES, "dw_dn_ab")
            full = jnp.concatenate([jnp.transpose(dw_main, (1, 0, 2)).reshape(D, 4 * W), dw_ab[0][:, :2 * H]], axis=1)
            dw_a = jnp.transpose(full.reshape(D, N_CHIPS, -1), (1, 0, 2))
            dw_b = matmul_tn(ogb.reshape(-1, W), dyb.reshape(1, -1, D), D, "dw_sq").reshape(N_CHIPS, -1, D)
        dmod[i][1] = dm
        d_norm_g[i][1] = jnp.sum(dg, axis=(0, 1))
        if i > 0:
            dx, dw_in0, dw_out0 = ffn_back(i, 0, 0, dx)
            if pending is not None:
                stacks = reduce_finish(pending[0], dx, where, "reduce_wait_%d" % pending[1], stacks,
                                       targets(pending[1], range(6)))
                pending = None
            handed = pair_start([dw_in0, dw_in1, dw_out0, dw_out1, dw_a, dw_b], "pair_start_%d" % i)
            paired, tok = (handed, i), handed[4][0, 0]
        else:
            part_a = reduce_start([dw_in1, dw_out1, dw_a, dw_b], c_idx, "reduce_start_0a")
            dx, dw_in0, dw_out0 = ffn_back(0, 0, 0, dx, part_a[4][0, 0])
            if pending is not None:
                stacks = reduce_finish(pending[0], dx, where, "reduce_wait_%d" % pending[1], stacks,
                                       targets(pending[1], range(6)))
            stacks = reduce_finish(part_a, dx, where, "reduce_wait_0a", stacks, targets(0, (1, 3, 4, 5)))

    part = dict(
        norm_g=jnp.stack([jnp.stack(r) for r in d_norm_g]),
        cm_b_glu=jnp.stack(g_cm["b_glu"]), cm_w_dw=jnp.stack(g_cm["w_dw"]), cm_b_dw=jnp.stack(g_cm["b_dw"]),
        cm_ln_g=jnp.stack(g_cm["ln_g"]), cm_ln_b=jnp.stack(g_cm["ln_b"]), cm_b_pw=jnp.stack(g_cm["b_pw"]),
        dn_w_sconv=jnp.stack(g_dn["w_sconv"]), dn_a_log=jnp.stack(g_dn["a_log"]), dn_dt_bias=jnp.stack(g_dn["dt_bias"]),
        dn_o_g=jnp.stack(g_dn["o_g"]), final_g=jnp.sum(d_final_g, axis=(0, 1)),
        loss=jnp.sum(loss_part[:, 0, 0]).reshape(1))
    dmod_loc = jnp.stack([jnp.concatenate(r, axis=1) for r in dmod]).reshape(L, BL, C9)
    keys = list(part)
    packed = _pack([part[k] for k in keys] + [dmod_loc])
    R = packed.shape[0]
    gathered = allgather8(packed).reshape(N_DEV, R, LANES)
    summed = _unpack(sum_devices(gathered).reshape(-1), [part[k].shape for k in keys])
    tot = dict(zip(keys, summed))
    n_small = sum(int(part[k].size) for k in keys)
    dmod_all = gathered.reshape(N_DEV, -1)[:, n_small:n_small + L * BL * C9].reshape(N_DEV, L, BL, C9)
    dmod_all = jnp.transpose(dmod_all, (1, 0, 2, 3)).reshape(L, NB, C9)
    dmod_cols = lax.dynamic_slice_in_dim(dmod_all, chip * Ca, Ca, axis=2)
    g_w_ada, g_b_ada = ada_bwd(c_all, dmod_cols, dmod_all)
    delta, new_m, new_v = {}, {}, {}
    part_b = reduce_start([dw_in0, dw_out0], c_idx, "reduce_start_0b", g_w_ada)
    delta["w_ada"], new_m["w_ada"], new_v["w_ada"] = _adamw_any(w_ada, g_w_ada, m_w_ada, v_w_ada, "adamw_w_ada",
                                                                 part_b[4])
    stacks = reduce_finish(part_b, new_v["w_ada"], where, "reduce_wait_0b", stacks, targets(0, (0, 2)))

    def my_cols(full):
        n = full.shape[-1] // N_CHIPS
        return lax.dynamic_slice_in_dim(full, chip * n, n, axis=full.ndim - 1)

    grads = dict(
        norm_g=my_cols(tot["norm_g"]), w_ada=g_w_ada, b_ada=g_b_ada.reshape(L, C9),
        cm_b_glu=tot["cm_b_glu"], cm_w_dw=my_cols(tot["cm_w_dw"]), cm_b_dw=tot["cm_b_dw"], cm_ln_g=tot["cm_ln_g"],
        cm_ln_b=tot["cm_ln_b"], cm_b_pw=tot["cm_b_pw"], dn_w_sconv=my_cols(tot["dn_w_sconv"]),
        dn_a_log=tot["dn_a_log"], dn_dt_bias=tot["dn_dt_bias"], dn_o_g=tot["dn_o_g"], final_g=tot["final_g"],
        **{n: stacks[n].reshape(weights[n].shape) for n in big_names})

    large = ("w_ada", "w_ffn_in", "w_ffn_out", "cm_w_glu", "cm_w_pw", "dn_w_in", "dn_w_out")
    for n in large[1:]:
        delta[n], new_m[n], new_v[n] = _adamw_any(weights[n], grads[n], mom_m[n], mom_v[n], "adamw_" + n)
    rest = [n for n in names if n not in large]
    shapes = [weights[n].shape for n in rest]
    pd, pm, pv = adamw(_pack([weights[n] for n in rest]), _pack([grads[n] for n in rest]),
                       _pack([mom_m[n] for n in rest]), _pack([mom_v[n] for n in rest]), "adamw_small")
    for n, d_, m_, v_ in zip(rest, _unpack(pd.reshape(-1), shapes), _unpack(pm.reshape(-1), shapes),
                             _unpack(pv.reshape(-1), shapes)):
        delta[n], new_m[n], new_v[n] = d_, m_, v_

    return (tot["loss"].reshape(()), dx, *[grads[n] for n in names], *[delta[n] for n in names],
            *[new_m[n] for n in names], *[new_v[n] for n in names])
```

```python
import functools

import jax
import jax.numpy as jnp
from jax import lax
from jax.experimental import pallas as pl
from jax.experimental.pallas import tpu as pltpu

F32 = jnp.float32
BF16 = jnp.bfloat16
EPS = 1e-6
CHUNK = 64
CHUNKS_PER_STEP = 4
N_CHIPS = 4
N_DEV = 8
LANES = 128
SUBLANES = 8
CONV_HALO = 32
SCONV_HALO = 8
VMEM_LIMIT_V7X = 60 * 1024 * 1024
DW_VMEM_BUDGET = 40 * 1024 * 1024
TOKENS_PER_STEP = 512
TOKENS_PER_STEP_WIDE = 256
ELEMENTWISE_BLOCK = 1 << 19
MESH = pl.DeviceIdType.MESH
HBM_SPEC = pl.BlockSpec(memory_space=pltpu.HBM)

ADAM_LR, ADAM_B1, ADAM_B2, ADAM_EPS, ADAM_WD, ADAM_STEP = 0.001, 0.9, 0.999, 1e-08, 0.01, 10


def _cparams(n_axes):
    return pltpu.CompilerParams(dimension_semantics=("arbitrary",) * n_axes, vmem_limit_bytes=VMEM_LIMIT_V7X)


def _tile(n, pref, mult=8):
    for t in range(min(n, pref) // mult * mult, 0, -mult):
        if n % t == 0:
            return t
    return n


def _mm(a, b):
    return lax.dot_general(a.astype(BF16), b.astype(BF16), (((1,), (0,)), ((), ())), preferred_element_type=F32)


def _mm_nt(a, b):
    return lax.dot_general(a.astype(BF16), b.astype(BF16), (((1,), (1,)), ((), ())), preferred_element_type=F32)


def _mm_tn(a, b):
    return lax.dot_general(a.astype(BF16), b.astype(BF16), (((0,), (0,)), ((), ())), preferred_element_type=F32)


def _sigmoid(x):
    return jax.nn.sigmoid(x)


def _dsilu(x, s):
    return s * (1.0 + x * (1.0 - s))


def _softplus(x):
    return jnp.maximum(x, 0.0) + jnp.log(1.0 + jnp.exp(-jnp.abs(x)))


def _modnorm(x, g, scale, shift):
    r = lax.rsqrt(jnp.mean(x * x, axis=-1, keepdims=True) + EPS)
    return (x * r) * g * (1.0 + scale) + shift


def _modnorm_bwd(x, g, scale, dh):
    r = lax.rsqrt(jnp.mean(x * x, axis=-1, keepdims=True) + EPS)
    xn = x * r
    dshift = jnp.sum(dh, axis=0, keepdims=True)
    dscale = jnp.sum(dh * (xn * g), axis=0, keepdims=True)
    dhn = dh * (1.0 + scale)
    dg = jnp.sum(dhn * xn, axis=0, keepdims=True)
    dxn = dhn * g
    dx = r * (dxn - xn * jnp.mean(dxn * xn, axis=-1, keepdims=True))
    return dx, dg, dscale, dshift


def _sum0(a):
    return jnp.sum(a, axis=0, keepdims=True)


def ffn_fwd(x, mod3, g, w_in, w_out):
    B, T, D = x.shape
    Fc = w_in.shape[2]
    w_in = w_in.reshape(2, 2, D, Fc)
    w_out = w_out.reshape(2, Fc, D)
    tm = _tile(T, TOKENS_PER_STEP)

    def half(h, wi_ref, wo_ref, gu_ref):
        gt = _mm(h, wi_ref[0])
        up = _mm(h, wi_ref[1])
        gu_ref[0] = gt.astype(BF16)
        gu_ref[1] = up.astype(BF16)
        return _mm(gt * _sigmoid(gt) * up, wo_ref[...])

    def body_a(x_ref, mod_ref, g_ref, wi_ref, wo_ref, h_ref, gu_ref, y0_ref):
        h = _modnorm(x_ref[...], g_ref[...], mod_ref[1:2, :], mod_ref[0:1, :]).astype(BF16)
        h_ref[...] = h
        y0_ref[...] = half(h, wi_ref, wo_ref, gu_ref)

    def body_b(x_ref, h_ref, y0_ref, mod_ref, wi_ref, wo_ref, gu_any, xo_ref, y_ref, gu_ref):
        y = y0_ref[...] + half(h_ref[...], wi_ref, wo_ref, gu_ref)
        y_ref[...] = y
        xo_ref[...] = x_ref[...] + 0.5 * (1.0 + mod_ref[2:3, :]) * y

    tok = pl.BlockSpec((None, tm, D), lambda b, t: (b, t, 0))
    per_b3 = pl.BlockSpec((None, 3, D), lambda b, t: (b, 0, 0))
    gu_shape = jax.ShapeDtypeStruct((2, B, T, 2 * Fc), BF16)

    def w_specs(part):
        return [pl.BlockSpec((2, None, D, Fc), lambda b, t: (0, part, 0, 0)),
                pl.BlockSpec((None, Fc, D), lambda b, t: (part, 0, 0))]

    def gu_spec(part):
        return pl.BlockSpec((2, None, tm, Fc), lambda b, t: (0, b, t, part))

    h, gu, y0 = pl.pallas_call(
        body_a, name="ffn_fwd_a", grid=(B, T // tm),
        in_specs=[tok, per_b3, pl.BlockSpec((1, D), lambda b, t: (0, 0))] + w_specs(0),
        out_specs=[tok, gu_spec(0), tok],
        out_shape=[jax.ShapeDtypeStruct((B, T, D), BF16), gu_shape, jax.ShapeDtypeStruct((B, T, D), F32)],
        compiler_params=_cparams(2),
    )(x, mod3, g, w_in, w_out)
    x_new, y, gu = pl.pallas_call(
        body_b, name="ffn_fwd_b", grid=(B, T // tm),
        in_specs=[tok, tok, tok, per_b3] + w_specs(1) + [pl.BlockSpec(memory_space=pl.ANY)],
        out_specs=[tok, tok, gu_spec(1)],
        out_shape=[jax.ShapeDtypeStruct((B, T, D), F32)] * 2 + [gu_shape],
        input_output_aliases={6: 2},
        compiler_params=_cparams(2),
    )(x, h, y0, mod3, w_in, w_out, gu)
    return x_new, y, h, gu


def ffn_bwd_part(part, dres, gu, mod3, w_in, w_out, first=None, y=None, x=None, g=None):
    B, T, D = dres.shape
    Fc = w_in.shape[2]
    F = 2 * Fc
    w_in = w_in.reshape(2, 2, D, Fc)
    w_out = w_out.reshape(2, Fc, D)
    tm = _tile(T, TOKENS_PER_STEP_WIDE)

    def half(dy, gu_ref, wi_ref, wo_ref, a_ref, dgu_ref):
        gt = gu_ref[0].astype(F32)
        up = gu_ref[1].astype(F32)
        sg = _sigmoid(gt)
        silu = gt * sg
        a_ref[...] = (silu * up).astype(BF16)
        da = _mm_nt(dy, wo_ref[...])
        dup = (da * silu).astype(BF16)
        dgt = (da * up * _dsilu(gt, sg)).astype(BF16)
        dgu_ref[0] = dgt
        dgu_ref[1] = dup
        return _mm_nt(dgt, wi_ref[0]) + _mm_nt(dup, wi_ref[1])

    tok = pl.BlockSpec((None, tm, D), lambda b, t: (b, t, 0))
    per_b3 = pl.BlockSpec((None, 3, D), lambda b, t: (b, 0, 0))
    per_b1 = pl.BlockSpec((None, 1, D), lambda b, t: (b, 0, 0))
    gu_spec = pl.BlockSpec((2, None, tm, Fc), lambda b, t: (0, b, t, part))
    a_spec = pl.BlockSpec((None, tm, Fc), lambda b, t: (b, t, part))
    wi_spec = pl.BlockSpec((2, None, D, Fc), lambda b, t: (0, part, 0, 0))
    wo_spec = pl.BlockSpec((None, Fc, D), lambda b, t: (part, 0, 0))
    a_shape = jax.ShapeDtypeStruct((B, T, F), BF16)
    dgu_shape = jax.ShapeDtypeStruct((2, B, T, F), BF16)

    if part == 0:
        def body(dres_ref, y_ref, gu_ref, mod_ref, wi_ref, wo_ref, a_ref, dgu_ref, dy_ref, dh_ref, dgate_ref):
            @pl.when(pl.program_id(1) == 0)
            def _():
                dgate_ref[...] = jnp.zeros_like(dgate_ref)

            dres = dres_ref[...]
            dy = (0.5 * (1.0 + mod_ref[2:3, :]) * dres).astype(BF16)
            dy_ref[...] = dy
            dgate_ref[...] += _sum0(dres * (0.5 * y_ref[...]))
            dh_ref[...] = half(dy, gu_ref, wi_ref, wo_ref, a_ref, dgu_ref)

        return pl.pallas_call(
            body, name="ffn_bwd_a", grid=(B, T // tm),
            in_specs=[tok, tok, gu_spec, per_b3, wi_spec, wo_spec],
            out_specs=[a_spec, gu_spec, tok, tok, per_b1],
            out_shape=[a_shape, dgu_shape, jax.ShapeDtypeStruct((B, T, D), BF16), jax.ShapeDtypeStruct((B, T, D), F32),
                       jax.ShapeDtypeStruct((B, 1, D), F32)],
            compiler_params=_cparams(2),
        )(dres, y, gu, mod3, w_in, w_out)

    a_full, dgu_full, dy, dh0 = first

    def body(x_ref, dres_ref, dy_ref, dh0_ref, gu_ref, mod_ref, g_ref, wi_ref, wo_ref, a_any, dgu_any,
             dx_ref, a_ref, dgu_ref, dmod_ref, dg_ref):
        @pl.when(pl.program_id(1) == 0)
        def _():
            dmod_ref[...] = jnp.zeros_like(dmod_ref)
            dg_ref[...] = jnp.zeros_like(dg_ref)

        dh = dh0_ref[...] + half(dy_ref[...], gu_ref, wi_ref, wo_ref, a_ref, dgu_ref)
        dxn, dg, dscale, dshift = _modnorm_bwd(x_ref[...], g_ref[...], mod_ref[1:2, :], dh)
        dx_ref[...] = dres_ref[...] + dxn
        dmod_ref[0:1, :] += dshift
        dmod_ref[1:2, :] += dscale
        dg_ref[...] += dg

    return pl.pallas_call(
        body, name="ffn_bwd_b", grid=(B, T // tm),
        in_specs=[tok, tok, tok, tok, gu_spec, per_b3, pl.BlockSpec((1, D), lambda b, t: (0, 0)), wi_spec, wo_spec,
                  ANY_SPEC, ANY_SPEC],
        out_specs=[tok, a_spec, gu_spec, per_b3, per_b1],
        out_shape=[jax.ShapeDtypeStruct((B, T, D), F32), a_shape, dgu_shape, jax.ShapeDtypeStruct((B, 3, D), F32),
                   jax.ShapeDtypeStruct((B, 1, D), F32)],
        input_output_aliases={9: 1, 10: 2},
        compiler_params=_cparams(2),
    )(x, dres, dy, dh0, gu, mod3, g, w_in, w_out, a_full, dgu_full)


def matmul_tn(xm, ym, bm, name):
    N, K = xm.shape
    GY, _, MY = ym.shape
    per = MY // bm
    nb = GY * per
    fixed = K * bm * (4 + 2 * 2)
    tn = _tile(N, max(512, (DW_VMEM_BUDGET - fixed) // (2 * 2 * (K + bm))), 256)

    def body(x_ref, y_ref, o_ref, acc_s):
        n = pl.program_id(1)

        @pl.when(n == 0)
        def _():
            acc_s[...] = jnp.zeros_like(acc_s)

        acc_s[...] += _mm_tn(x_ref[...], y_ref[...])

        @pl.when(n == N // tn - 1)
        def _():
            o_ref[...] = acc_s[...].astype(BF16)

    return pl.pallas_call(
        body, name=name, grid=(nb, N // tn),
        in_specs=[pl.BlockSpec((tn, K), lambda m, n: (n, 0)),
                  pl.BlockSpec((None, tn, bm), lambda m, n: (m // per, n, m % per))],
        out_specs=pl.BlockSpec((None, K, bm), lambda m, n: (m, 0, 0)),
        out_shape=jax.ShapeDtypeStruct((nb, K, bm), BF16),
        scratch_shapes=[pltpu.VMEM((K, bm), F32)],
        compiler_params=_cparams(2),
    )(xm, ym)


def final_loss(x, fg, target):
    B, T, D = x.shape
    tm = _tile(T, TOKENS_PER_STEP)

    def body(x_ref, g_ref, t_ref, dx_ref, dfg_ref, loss_ref):
        t = pl.program_id(1)

        @pl.when(t == 0)
        def _():
            dfg_ref[...] = jnp.zeros_like(dfg_ref)
            loss_ref[...] = jnp.zeros_like(loss_ref)

        xv = x_ref[...]
        g = g_ref[...]
        r = lax.rsqrt(jnp.mean(xv * xv, axis=-1, keepdims=True) + EPS)
        xn = xv * r
        err = xn * g - t_ref[...]
        tok_loss = jnp.mean(err * err, axis=-1, keepdims=True)
        loss_ref[...] += 0.5 * jnp.sum(tok_loss, axis=0, keepdims=True)
        dy = err * (1.0 / D)
        dfg_ref[...] += _sum0(dy * xn)
        dxn = dy * g
        dx_ref[...] = r * (dxn - xn * jnp.mean(dxn * xn, axis=-1, keepdims=True))

    tok = pl.BlockSpec((None, tm, D), lambda b, t: (b, t, 0))
    return pl.pallas_call(
        body, name="final_loss", grid=(B, T // tm),
        in_specs=[tok, pl.BlockSpec((1, D), lambda b, t: (0, 0)), tok],
        out_specs=[tok, pl.BlockSpec((None, 1, D), lambda b, t: (b, 0, 0)),
                   pl.BlockSpec((None, 1, LANES), lambda b, t: (b, 0, 0))],
        out_shape=[jax.ShapeDtypeStruct((B, T, D), F32), jax.ShapeDtypeStruct((B, 1, D), F32),
                   jax.ShapeDtypeStruct((B, 1, LANES), F32)],
        compiler_params=_cparams(2),
    )(x, fg, target)


def _past_halo_spec(tm, halo, width):
    return pl.BlockSpec((None, halo, width), lambda b, t: (b, jnp.maximum(t * (tm // halo) - 1, 0), 0))


def _future_halo_spec(tm, halo, width, T):
    return pl.BlockSpec((None, halo, width), lambda b, t: (b, jnp.minimum((t + 1) * (tm // halo), T // halo - 1), 0))


def _fill_shifted(ext_s):
    n = ext_s.shape[1]
    for b in range(1, SUBLANES):
        ext_s[b, 0:n - SUBLANES, :] = ext_s[0, pl.ds(b, n - SUBLANES), :]


def _shifted(ext_s, offset, rows):
    a, b = divmod(offset, SUBLANES)
    return ext_s[b, pl.ds(SUBLANES * a, rows), :]


def _glu_fwd(h, w_ref, bias):
    D = h.shape[1]
    a = jnp.concatenate([_mm(h, w_ref[0]), _mm(h, w_ref[1])], axis=1) + bias[:, :D]
    b = jnp.concatenate([_mm(h, w_ref[2]), _mm(h, w_ref[3])], axis=1) + bias[:, D:]
    return a, b


def conv_glu_fwd(x, mod3, g, w_glu, b_glu):
    B, T, D = x.shape
    tm = _tile(T, TOKENS_PER_STEP)

    def body(x_ref, mod_ref, g_ref, w_ref, b_ref, u_ref):
        h = _modnorm(x_ref[...], g_ref[...], mod_ref[1:2, :], mod_ref[0:1, :]).astype(BF16)
        a, b = _glu_fwd(h, w_ref, b_ref[...])
        u_ref[...] = a * _sigmoid(b)

    tok = pl.BlockSpec((None, tm, D), lambda b, t: (b, t, 0))
    return pl.pallas_call(
        body, name="conv_glu_fwd", grid=(B, T // tm),
        in_specs=[tok, pl.BlockSpec((None, 3, D), lambda b, t: (b, 0, 0)),
                  pl.BlockSpec((1, D), lambda b, t: (0, 0)),
                  pl.BlockSpec((4, D, D // 2), lambda b, t: (0, 0, 0)),
                  pl.BlockSpec((1, 2 * D), lambda b, t: (0, 0))],
        out_specs=tok, out_shape=jax.ShapeDtypeStruct((B, T, D), F32),
        compiler_params=_cparams(2),
    )(x, mod3, g, w_glu, b_glu)


def _layer_norm_parts(u2):
    mu = jnp.mean(u2, axis=-1, keepdims=True)
    xc = u2 - mu
    rs = lax.rsqrt(jnp.mean(xc * xc, axis=-1, keepdims=True) + EPS)
    return xc * rs, rs


def conv_out_fwd(x, u, mod3, w_dw, b_dw, ln_g, ln_b, w_pw, b_pw):
    B, T, D = x.shape
    K = w_dw.shape[0] - 1
    tm = _tile(T, TOKENS_PER_STEP)

    def body(x_ref, u_ref, halo_ref, mod_ref, wdw_ref, bdw_ref, lg_ref, lb_ref, wpw_ref, bpw_ref,
             xo_ref, y_ref, u2_ref, ext_s):
        t = pl.program_id(1)
        ext_s[0, 0:CONV_HALO, :] = jnp.where(t > 0, halo_ref[...], 0.0)
        ext_s[0, CONV_HALO:, :] = u_ref[...]
        _fill_shifted(ext_s)
        acc = jnp.broadcast_to(bdw_ref[...], (tm, D))
        for k in range(K):
            acc = acc + wdw_ref[k:k + 1, :] * _shifted(ext_s, CONV_HALO - (K - 1) + k, tm)
        u2_ref[...] = acc
        xh, _ = _layer_norm_parts(acc)
        l = xh * lg_ref[...] + lb_ref[...]
        u3 = l * _sigmoid(l)
        y = _mm(u3, wpw_ref[...]) + bpw_ref[...]
        y_ref[...] = y
        xo_ref[...] = x_ref[...] + (1.0 + mod_ref[2:3, :]) * y

    tok = pl.BlockSpec((None, tm, D), lambda b, t: (b, t, 0))
    vec = pl.BlockSpec((1, D), lambda b, t: (0, 0))
    return pl.pallas_call(
        body, name="conv_out_fwd", grid=(B, T // tm),
        in_specs=[tok, tok, _past_halo_spec(tm, CONV_HALO, D), pl.BlockSpec((None, 3, D), lambda b, t: (b, 0, 0)),
                  pl.BlockSpec((K + 1, D), lambda b, t: (0, 0)), vec, vec, vec,
                  pl.BlockSpec((D, D), lambda b, t: (0, 0)), vec],
        out_specs=[tok, tok, tok], out_shape=[jax.ShapeDtypeStruct((B, T, D), F32)] * 3,
        scratch_shapes=[pltpu.VMEM((SUBLANES, tm + CONV_HALO, D), F32)],
        compiler_params=_cparams(2),
    )(x, u, u, mod3, w_dw, b_dw, ln_g, ln_b, w_pw, b_pw)


def conv_out_bwd(dres, y, u2, mod3, ln_g, ln_b, w_pw):
    B, T, D = dres.shape
    tm = _tile(T, TOKENS_PER_STEP)

    def body(dres_ref, y_ref, u2_ref, mod_ref, lg_ref, lb_ref, wpw_ref, du2_ref, u3_ref, dy_ref, dgate_ref, vec_ref):
        t = pl.program_id(1)

        @pl.when(t == 0)
        def _():
            dgate_ref[...] = jnp.zeros_like(dgate_ref)
            vec_ref[...] = jnp.zeros_like(vec_ref)

        dres = dres_ref[...]
        dy = (1.0 + mod_ref[2:3, :]) * dres
        dy_ref[...] = dy.astype(BF16)
        dgate_ref[...] += _sum0(dres * y_ref[...])
        xh, rs = _layer_norm_parts(u2_ref[...])
        lg = lg_ref[...]
        l = xh * lg + lb_ref[...]
        sg = _sigmoid(l)
        u3_ref[...] = (l * sg).astype(BF16)
        du3 = _mm_nt(dy, wpw_ref[...])
        dl = du3 * _dsilu(l, sg)
        dxh = dl * lg
        du2 = rs * (dxh - jnp.mean(dxh, axis=-1, keepdims=True) - xh * jnp.mean(dxh * xh, axis=-1, keepdims=True))
        du2_ref[...] = du2
        vec_ref[0:1, :] += _sum0(dy)
        vec_ref[1:2, :] += _sum0(dl * xh)
        vec_ref[2:3, :] += _sum0(dl)
        vec_ref[3:4, :] += _sum0(du2)

    tok = pl.BlockSpec((None, tm, D), lambda b, t: (b, t, 0))
    tokb = pl.BlockSpec((None, tm, D), lambda b, t: (b, t, 0))
    vec = pl.BlockSpec((1, D), lambda b, t: (0, 0))
    return pl.pallas_call(
        body, name="conv_out_bwd", grid=(B, T // tm),
        in_specs=[tok, tok, tok, pl.BlockSpec((None, 3, D), lambda b, t: (b, 0, 0)), vec, vec,
                  pl.BlockSpec((D, D), lambda b, t: (0, 0))],
        out_specs=[tok, tokb, tokb, pl.BlockSpec((None, 1, D), lambda b, t: (b, 0, 0)),
                   pl.BlockSpec((None, 4, D), lambda b, t: (b, 0, 0))],
        out_shape=[jax.ShapeDtypeStruct((B, T, D), F32), jax.ShapeDtypeStruct((B, T, D), BF16),
                   jax.ShapeDtypeStruct((B, T, D), BF16), jax.ShapeDtypeStruct((B, 1, D), F32),
                   jax.ShapeDtypeStruct((B, 4, D), F32)],
        compiler_params=_cparams(2),
    )(dres, y, u2, mod3, ln_g, ln_b, w_pw)


def conv_glu_bwd(x, dres, du2, u, mod3, g, w_glu, b_glu, w_dw):
    B, T, D = x.shape
    K = w_dw.shape[0] - 1
    tm = _tile(T, TOKENS_PER_STEP_WIDE)
    nt = T // tm

    def body(x_ref, dres_ref, du2_ref, du2h_ref, u_ref, uh_ref, mod_ref, g_ref, w_ref, b_ref, wdw_ref,
             dx_ref, h_ref, dab_ref, dwdw_ref, dbglu_ref, dmod_ref, dg_ref, extu_s, extd_s):
        t = pl.program_id(1)

        @pl.when(t == 0)
        def _():
            dwdw_ref[...] = jnp.zeros_like(dwdw_ref)
            dbglu_ref[...] = jnp.zeros_like(dbglu_ref)
            dmod_ref[...] = jnp.zeros_like(dmod_ref)
            dg_ref[...] = jnp.zeros_like(dg_ref)

        du2 = du2_ref[...]
        extu_s[0, 0:CONV_HALO, :] = jnp.where(t > 0, uh_ref[...], 0.0)
        extu_s[0, CONV_HALO:, :] = u_ref[...]
        extd_s[0, 0:tm, :] = du2
        extd_s[0, tm:, :] = jnp.where(t < nt - 1, du2h_ref[...], 0.0)
        _fill_shifted(extu_s)
        _fill_shifted(extd_s)
        du = jnp.zeros((tm, D), F32)
        for k in range(K):
            du = du + wdw_ref[k:k + 1, :] * _shifted(extd_s, K - 1 - k, tm)
            dwdw_ref[k:k + 1, :] += _sum0(du2 * _shifted(extu_s, CONV_HALO - (K - 1) + k, tm))
        xv = x_ref[...]
        h = _modnorm(xv, g_ref[...], mod_ref[1:2, :], mod_ref[0:1, :]).astype(BF16)
        h_ref[...] = h
        a, b = _glu_fwd(h, w_ref, b_ref[...])
        sb = _sigmoid(b)
        da = du * sb
        db = du * a * sb * (1.0 - sb)
        dbglu_ref[:, 0:D] += _sum0(da)
        dbglu_ref[:, D:] += _sum0(db)
        da = da.astype(BF16)
        db = db.astype(BF16)
        dab_ref[:, 0:D] = da
        dab_ref[:, D:] = db
        Dh2 = D // 2
        dh = (_mm_nt(da[:, :Dh2], w_ref[0]) + _mm_nt(da[:, Dh2:], w_ref[1])
              + _mm_nt(db[:, :Dh2], w_ref[2]) + _mm_nt(db[:, Dh2:], w_ref[3]))
        dxn, dg, dscale, dshift = _modnorm_bwd(xv, g_ref[...], mod_ref[1:2, :], dh)
        dx_ref[...] = dres_ref[...] + dxn
        dmod_ref[0:1, :] += dshift
        dmod_ref[1:2, :] += dscale
        dg_ref[...] += dg

    tok = pl.BlockSpec((None, tm, D), lambda b, t: (b, t, 0))
    return pl.pallas_call(
        body, name="conv_glu_bwd", grid=(B, nt),
        in_specs=[tok, tok, tok, _future_halo_spec(tm, CONV_HALO, D, T), tok, _past_halo_spec(tm, CONV_HALO, D),
                  pl.BlockSpec((None, 3, D), lambda b, t: (b, 0, 0)), pl.BlockSpec((1, D), lambda b, t: (0, 0)),
                  pl.BlockSpec((4, D, D // 2), lambda b, t: (0, 0, 0)), pl.BlockSpec((1, 2 * D), lambda b, t: (0, 0)),
                  pl.BlockSpec((K + 1, D), lambda b, t: (0, 0))],
        out_specs=[tok, tok, pl.BlockSpec((None, tm, 2 * D), lambda b, t: (b, t, 0)),
                   pl.BlockSpec((None, K + 1, D), lambda b, t: (b, 0, 0)),
                   pl.BlockSpec((None, 1, 2 * D), lambda b, t: (b, 0, 0)),
                   pl.BlockSpec((None, 3, D), lambda b, t: (b, 0, 0)),
                   pl.BlockSpec((None, 1, D), lambda b, t: (b, 0, 0))],
        out_shape=[jax.ShapeDtypeStruct((B, T, D), F32), jax.ShapeDtypeStruct((B, T, D), BF16),
                   jax.ShapeDtypeStruct((B, T, 2 * D), BF16), jax.ShapeDtypeStruct((B, K + 1, D), F32),
                   jax.ShapeDtypeStruct((B, 1, 2 * D), F32), jax.ShapeDtypeStruct((B, 3, D), F32),
                   jax.ShapeDtypeStruct((B, 1, D), F32)],
        scratch_shapes=[pltpu.VMEM((SUBLANES, tm + CONV_HALO, D), F32)] * 2,
        compiler_params=_cparams(2),
    )(x, dres, du2, du2, u, u, mod3, g, w_glu, b_glu, w_dw)


def dn_proj_fwd(x, mod3, g, w_main, w_ab):
    B, T, D = x.shape
    W = w_main.shape[1] // 4
    tm = _tile(T, TOKENS_PER_STEP)

    def body(x_ref, mod_ref, g_ref, wm_ref, wab_ref, pre_ref, z_ref, ab_ref):
        h = _modnorm(x_ref[...], g_ref[...], mod_ref[1:2, :], mod_ref[0:1, :]).astype(BF16)
        for p in range(3):
            pre_ref[:, p * W:(p + 1) * W] = _mm(h, wm_ref[:, p * W:(p + 1) * W])
        z_ref[...] = _mm(h, wm_ref[:, 3 * W:])
        ab_ref[...] = _mm(h, wab_ref[...])

    return pl.pallas_call(
        body, name="dn_proj_fwd", grid=(B, T // tm),
        in_specs=[pl.BlockSpec((None, tm, D), lambda b, t: (b, t, 0)), pl.BlockSpec((None, 3, D), lambda b, t: (b, 0, 0)),
                  pl.BlockSpec((1, D), lambda b, t: (0, 0)), pl.BlockSpec((D, 4 * W), lambda b, t: (0, 0)),
                  pl.BlockSpec((D, LANES), lambda b, t: (0, 0))],
        out_specs=[pl.BlockSpec((None, tm, 3 * W), lambda b, t: (b, t, 0)),
                   pl.BlockSpec((None, tm, W), lambda b, t: (b, t, 0)),
                   pl.BlockSpec((None, tm, LANES), lambda b, t: (b, t, 0))],
        out_shape=[jax.ShapeDtypeStruct((B, T, 3 * W), F32), jax.ShapeDtypeStruct((B, T, W), F32),
                   jax.ShapeDtypeStruct((B, T, LANES), F32)],
        compiler_params=_cparams(2),
    )(x, mod3, g, w_main, w_ab)


def _sconv(ext_s, w_ref, tm, K):
    acc = w_ref[0:1, :] * ext_s[pl.ds(SCONV_HALO - (K - 1), tm), :]
    for k in range(1, K):
        acc = acc + w_ref[k:k + 1, :] * ext_s[pl.ds(SCONV_HALO - (K - 1) + k, tm), :]
    return acc


def _lane_col(val, lane, idx):
    return jnp.sum(jnp.where(lane == idx, val, 0.0), axis=1, keepdims=True)


def dn_conv_fwd(pre, ab, w_sconv, alog_row, dt_row, H):
    B, T, W3 = pre.shape
    W = W3 // 3
    Dh = W // H
    K = w_sconv.shape[0]
    tm = _tile(T, TOKENS_PER_STEP)

    def body(pre_ref, halo_ref, ab_ref, w_ref, alog_ref, dt_ref, q_ref, k_ref, v_ref, gb_ref, bb_ref, ext_s):
        t = pl.program_id(1)
        ext_s[0:SCONV_HALO, :] = jnp.where(t > 0, halo_ref[...], 0.0)
        ext_s[SCONV_HALO:, :] = pre_ref[...]
        cv = _sconv(ext_s, w_ref, tm, K)
        qkv = cv * _sigmoid(cv)
        ab = ab_ref[...]
        lane = lax.broadcasted_iota(jnp.int32, ab.shape, 1)
        g_all = -jnp.exp(alog_ref[...]) * _softplus(ab + dt_ref[...])
        beta_all = _sigmoid(ab)
        for h in range(H):
            q_ref[h] = qkv[:, h * Dh:(h + 1) * Dh]
            k_ref[h] = qkv[:, W + h * Dh:W + (h + 1) * Dh]
            v_ref[h] = qkv[:, 2 * W + h * Dh:2 * W + (h + 1) * Dh]
            gb_ref[h] = jnp.broadcast_to(_lane_col(g_all, lane, h), (tm, Dh))
            bb_ref[h] = jnp.broadcast_to(_lane_col(beta_all, lane, H + h), (tm, Dh))

    hm = pl.BlockSpec((None, H, tm, Dh), lambda b, t: (b, 0, t, 0))
    row = pl.BlockSpec((1, LANES), lambda b, t: (0, 0))
    return pl.pallas_call(
        body, name="dn_conv_fwd", grid=(B, T // tm),
        in_specs=[pl.BlockSpec((None, tm, W3), lambda b, t: (b, t, 0)), _past_halo_spec(tm, SCONV_HALO, W3),
                  pl.BlockSpec((None, tm, LANES), lambda b, t: (b, t, 0)),
                  pl.BlockSpec((K, W3), lambda b, t: (0, 0)), row, row],
        out_specs=[hm] * 5, out_shape=[jax.ShapeDtypeStruct((B, H, T, Dh), F32)] * 5,
        scratch_shapes=[pltpu.VMEM((tm + SCONV_HALO, W3), F32)],
        compiler_params=_cparams(2),
    )(pre, pre, ab, w_sconv, alog_row, dt_row)


def _bdot(spec):
    return lambda a, b: jnp.einsum(spec, a.astype(BF16), b.astype(BF16), preferred_element_type=F32)


_NN, _NT, _TN = "gij,gjk->gik", "gik,gjk->gij", "gki,gkj->gij"


def _make_bdots():
    nn_, nt_, tn_ = _bdot(_NN), _bdot(_NT), _bdot(_TN)

    @jax.custom_vjp
    def nn(a, b):
        return nn_(a, b)

    @jax.custom_vjp
    def nt(a, b):
        return nt_(a, b)

    @jax.custom_vjp
    def tn(a, b):
        return tn_(a, b)

    nn.defvjp(lambda a, b: (nn_(a, b), (a, b)), lambda r, d: (nt_(d, r[1]), tn_(r[0], d)))
    nt.defvjp(lambda a, b: (nt_(a, b), (a, b)), lambda r, d: (nn_(d, r[1]), tn_(d, r[0])))
    tn.defvjp(lambda a, b: (tn_(a, b), (a, b)), lambda r, d: (nt_(r[1], d), nn_(r[0], d)))
    return nn, nt, tn


def _unit_lower_inverse(A, known=None):
    hdot = functools.partial(jnp.einsum, precision=lax.Precision.HIGH, preferred_element_type=F32)
    C = A.shape[-1]

    def impl(A):
        eye = (lax.broadcasted_iota(jnp.int32, A.shape, 1) == lax.broadcasted_iota(jnp.int32, A.shape, 2)).astype(F32)
        Tm = eye - A
        Ap = A
        for _ in range(max(1, (C - 1).bit_length()) - 1):
            Ap = hdot(_NN, Ap, Ap)
            Tm = Tm + hdot(_NN, Tm, Ap)
        return Tm

    @jax.custom_vjp
    def inv(A, given):
        return impl(A) if known is None else given

    def fwd(A, given):
        Tm = impl(A) if known is None else given
        return Tm, Tm

    def bwd(Tm, dT):
        return -hdot(_NT, hdot(_TN, Tm, dT), Tm), jnp.zeros_like(Tm)

    inv.defvjp(fwd, bwd)
    return inv(A, A if known is None else known)


def _chunk_fn(q, k, v, gb, bb, S, inverse=None, with_inverse=False):
    nn, nt, tn = _make_bdots()
    G, C, Dh = q.shape
    hdot = functools.partial(jnp.einsum, precision=lax.Precision.HIGH, preferred_element_type=F32)
    q = q * lax.rsqrt(jnp.sum(q * q, axis=-1, keepdims=True) + EPS) * (Dh ** -0.5)
    k = k * lax.rsqrt(jnp.sum(k * k, axis=-1, keepdims=True) + EPS)
    row = lax.broadcasted_iota(jnp.int32, (G, C, C), 1)
    col = lax.broadcasted_iota(jnp.int32, (G, C, C), 2)
    causal = row >= col
    strict = row > col
    gc = hdot(_NN, causal.astype(F32), gb)
    spread = jnp.full((G, C, Dh), 1.0 / Dh, F32)
    gi = hdot(_NT, gc, spread)
    gj = hdot(_NT, spread, gc)
    decay = jnp.where(causal, jnp.exp(jnp.where(causal, gi - gj, 0.0)), 0.0)
    kb = k * bb
    vb = v * bb
    A = jnp.where(strict, nt(kb, k) * decay, 0.0)
    Tm = _unit_lower_inverse(A, inverse)
    eg = jnp.exp(gc)
    u = nn(Tm, vb)
    w = nn(Tm, kb * eg)
    qg = q * eg
    intra = nt(q, k) * decay
    glast = hdot(_NN, jnp.ones((G, C, C), F32), gb)
    kd = k * jnp.exp(glast - gc)
    v_new = u - nn(w, S)
    o = nn(qg, S) + nn(intra, v_new)
    egl = jnp.exp(glast)
    S_new = S * jnp.concatenate([egl] * (Dh // C), axis=1) + tn(kd, v_new)
    return (o, S_new, Tm) if with_inverse else (o, S_new)


def dn_chunk_fwd(q, k, v, gb, bb):
    B, H, T, Dh = q.shape
    NC = T // CHUNK
    NS = _tile(NC, CHUNKS_PER_STEP, 1)

    def body(q_ref, k_ref, v_ref, gb_ref, bb_ref, o_ref, sp_ref, inv_ref, S_s):
        @pl.when(pl.program_id(1) == 0)
        def _():
            S_s[...] = jnp.zeros_like(S_s)

        def one_chunk(j, carry):
            rows = pl.ds(pl.multiple_of(j * CHUNK, CHUNK), CHUNK)
            S = S_s[...]
            sp_ref[j] = S
            o, S_new, Tm = _chunk_fn(q_ref[:, rows, :], k_ref[:, rows, :], v_ref[:, rows, :], gb_ref[:, rows, :],
                                     bb_ref[:, rows, :], S, with_inverse=True)
            o_ref[:, rows, :] = o
            inv_ref[j] = Tm
            S_s[...] = S_new
            return carry

        lax.fori_loop(0, NS, one_chunk, 0)

    hm = pl.BlockSpec((None, H, NS * CHUNK, Dh), lambda b, n: (b, 0, n, 0))
    return pl.pallas_call(
        body, name="dn_chunk_fwd", grid=(B, NC // NS),
        in_specs=[hm] * 5,
        out_specs=[hm, pl.BlockSpec((None, NS, H, Dh, Dh), lambda b, n: (b, n, 0, 0, 0)),
                   pl.BlockSpec((None, NS, H, CHUNK, CHUNK), lambda b, n: (b, n, 0, 0, 0))],
        out_shape=[jax.ShapeDtypeStruct((B, H, T, Dh), F32), jax.ShapeDtypeStruct((B, NC, H, Dh, Dh), F32),
                   jax.ShapeDtypeStruct((B, NC, H, CHUNK, CHUNK), F32)],
        scratch_shapes=[pltpu.VMEM((H, Dh, Dh), F32)],
        compiler_params=_cparams(2),
    )(q, k, v, gb, bb)


def dn_chunk_bwd(q, k, v, gb, bb, s_prev, inv, do):
    B, H, T, Dh = q.shape
    NC = T // CHUNK
    NS = _tile(NC, CHUNKS_PER_STEP, 1)
    NG = NC // NS

    def body(q_ref, k_ref, v_ref, gb_ref, bb_ref, sp_ref, inv_ref, do_ref, dq_ref, dk_ref, dv_ref, dgb_ref, dbb_ref,
             dS_s):
        @pl.when(pl.program_id(1) == 0)
        def _():
            dS_s[...] = jnp.zeros_like(dS_s)

        def one_chunk(jj, carry):
            j = NS - 1 - jj
            rows = pl.ds(pl.multiple_of(j * CHUNK, CHUNK), CHUNK)
            _, vjp = jax.vjp(functools.partial(_chunk_fn, inverse=inv_ref[j]), q_ref[:, rows, :], k_ref[:, rows, :],
                             v_ref[:, rows, :], gb_ref[:, rows, :], bb_ref[:, rows, :], sp_ref[j])
            dq, dk, dv, dgb, dbb, dS = vjp((do_ref[:, rows, :], dS_s[...]))
            dq_ref[:, rows, :] = dq
            dk_ref[:, rows, :] = dk
            dv_ref[:, rows, :] = dv
            dgb_ref[:, rows, :] = dgb
            dbb_ref[:, rows, :] = dbb
            dS_s[...] = dS
            return carry

        lax.fori_loop(0, NS, one_chunk, 0)

    hm = pl.BlockSpec((None, H, NS * CHUNK, Dh), lambda b, n: (b, 0, NG - 1 - n, 0))
    return pl.pallas_call(
        body, name="dn_chunk_bwd", grid=(B, NG),
        in_specs=[hm] * 5 + [pl.BlockSpec((None, NS, H, Dh, Dh), lambda b, n: (b, NG - 1 - n, 0, 0, 0)),
                             pl.BlockSpec((None, NS, H, CHUNK, CHUNK), lambda b, n: (b, NG - 1 - n, 0, 0, 0)), hm],
        out_specs=[hm] * 5, out_shape=[jax.ShapeDtypeStruct((B, H, T, Dh), F32)] * 5,
        scratch_shapes=[pltpu.VMEM((H, Dh, Dh), F32)],
        compiler_params=_cparams(2),
    )(q, k, v, gb, bb, s_prev, inv, do)


def _head_norm(o, og):
    r = lax.rsqrt(jnp.mean(o * o, axis=-1, keepdims=True) + EPS)
    return o * r, r


def dn_out_fwd(x, o, z, mod3, o_g, w_out):
    B, T, D = x.shape
    _, H, _, Dh = o.shape
    W = H * Dh
    tm = _tile(T, TOKENS_PER_STEP)

    def body(x_ref, o_ref, z_ref, mod_ref, og_ref, w_ref, xo_ref, y_ref):
        parts = []
        for h in range(H):
            on, _ = _head_norm(o_ref[h], og_ref[...])
            zz = z_ref[:, h * Dh:(h + 1) * Dh]
            parts.append((on * og_ref[...] * (zz * _sigmoid(zz))).astype(BF16))
        y = _mm(jnp.concatenate(parts, axis=1), w_ref[...])
        y_ref[...] = y
        xo_ref[...] = x_ref[...] + (1.0 + mod_ref[2:3, :]) * y

    tok = pl.BlockSpec((None, tm, D), lambda b, t: (b, t, 0))
    return pl.pallas_call(
        body, name="dn_out_fwd", grid=(B, T // tm),
        in_specs=[tok, pl.BlockSpec((None, H, tm, Dh), lambda b, t: (b, 0, t, 0)),
                  pl.BlockSpec((None, tm, W), lambda b, t: (b, t, 0)), pl.BlockSpec((None, 3, D), lambda b, t: (b, 0, 0)),
                  pl.BlockSpec((1, Dh), lambda b, t: (0, 0)), pl.BlockSpec((W, D), lambda b, t: (0, 0))],
        out_specs=[tok, tok], out_shape=[jax.ShapeDtypeStruct((B, T, D), F32)] * 2,
        compiler_params=_cparams(2),
    )(x, o, z, mod3, o_g, w_out)


def dn_out_bwd(dres, y, o, z, mod3, o_g, w_out):
    B, T, D = dres.shape
    _, H, _, Dh = o.shape
    W = H * Dh
    tm = _tile(T, TOKENS_PER_STEP)

    def body(dres_ref, y_ref, o_ref, z_ref, mod_ref, og_ref, w_ref, do_ref, dz_ref, ogb_ref, dy_ref, dgate_ref, dog_ref):
        t = pl.program_id(1)

        @pl.when(t == 0)
        def _():
            dgate_ref[...] = jnp.zeros_like(dgate_ref)
            dog_ref[...] = jnp.zeros_like(dog_ref)

        dres = dres_ref[...]
        dy = ((1.0 + mod_ref[2:3, :]) * dres).astype(BF16)
        dy_ref[...] = dy
        dgate_ref[...] += _sum0(dres * y_ref[...])
        dog = _mm_nt(dy, w_ref[...])
        og = og_ref[...]
        for h in range(H):
            ov = o_ref[h]
            xn, r = _head_norm(ov, og)
            zz = z_ref[:, h * Dh:(h + 1) * Dh]
            sg = _sigmoid(zz)
            sz = zz * sg
            d = dog[:, h * Dh:(h + 1) * Dh]
            ogb_ref[:, h * Dh:(h + 1) * Dh] = (xn * og * sz).astype(BF16)
            dz_ref[:, h * Dh:(h + 1) * Dh] = d * (xn * og) * _dsilu(zz, sg)
            don = d * sz
            dog_ref[...] += _sum0(don * xn)
            dxn = don * og
            do_ref[h] = r * (dxn - xn * jnp.mean(dxn * xn, axis=-1, keepdims=True))

    tok = pl.BlockSpec((None, tm, D), lambda b, t: (b, t, 0))
    tokw = pl.BlockSpec((None, tm, W), lambda b, t: (b, t, 0))
    hm = pl.BlockSpec((None, H, tm, Dh), lambda b, t: (b, 0, t, 0))
    return pl.pallas_call(
        body, name="dn_out_bwd", grid=(B, T // tm),
        in_specs=[tok, tok, hm, tokw, pl.BlockSpec((None, 3, D), lambda b, t: (b, 0, 0)),
                  pl.BlockSpec((1, Dh), lambda b, t: (0, 0)), pl.BlockSpec((W, D), lambda b, t: (0, 0))],
        out_specs=[hm, tokw, tokw, tok, pl.BlockSpec((None, 1, D), lambda b, t: (b, 0, 0)),
                   pl.BlockSpec((None, 1, Dh), lambda b, t: (b, 0, 0))],
        out_shape=[jax.ShapeDtypeStruct((B, H, T, Dh), F32), jax.ShapeDtypeStruct((B, T, W), F32),
                   jax.ShapeDtypeStruct((B, T, W), BF16), jax.ShapeDtypeStruct((B, T, D), BF16),
                   jax.ShapeDtypeStruct((B, 1, D), F32), jax.ShapeDtypeStruct((B, 1, Dh), F32)],
        compiler_params=_cparams(2),
    )(dres, y, o, z, mod3, o_g, w_out)


def dn_conv_bwd(dq, dk, dv, dgb, dbb, pre, ab, w_sconv, alog_row, dt_row):
    B, H, T, Dh = dq.shape
    W = H * Dh
    W3 = 3 * W
    K = w_sconv.shape[0]
    tm = _tile(T, TOKENS_PER_STEP_WIDE)

    def body(dq_ref, dk_ref, dv_ref, dgb_ref, dbb_ref, pre_ref, halo_ref, ab_ref, w_ref, alog_ref, dt_ref,
             dc_ref, dab_ref, small_ref, ext_s):
        t = pl.program_id(1)

        @pl.when(t == 0)
        def _():
            small_ref[...] = jnp.zeros_like(small_ref)

        ext_s[0:SCONV_HALO, :] = jnp.where(t > 0, halo_ref[...], 0.0)
        ext_s[SCONV_HALO:, :] = pre_ref[...]
        cv = _sconv(ext_s, w_ref, tm, K)
        dsl = _dsilu(cv, _sigmoid(cv))
        ab = ab_ref[...]
        lane = lax.broadcasted_iota(jnp.int32, ab.shape, 1)
        dg_all = jnp.zeros_like(ab)
        db_all = jnp.zeros_like(ab)
        for h in range(H):
            dc_ref[:, h * Dh:(h + 1) * Dh] = dq_ref[h] * dsl[:, h * Dh:(h + 1) * Dh]
            dc_ref[:, W + h * Dh:W + (h + 1) * Dh] = dk_ref[h] * dsl[:, W + h * Dh:W + (h + 1) * Dh]
            dc_ref[:, 2 * W + h * Dh:2 * W + (h + 1) * Dh] = dv_ref[h] * dsl[:, 2 * W + h * Dh:2 * W + (h + 1) * Dh]
            dg_all = dg_all + jnp.where(lane == h, jnp.sum(dgb_ref[h], axis=1, keepdims=True), 0.0)
            db_all = db_all + jnp.where(lane == H + h, jnp.sum(dbb_ref[h], axis=1, keepdims=True), 0.0)
        xa = ab + dt_ref[...]
        ea = -jnp.exp(alog_ref[...])
        g_all = ea * _softplus(xa)
        da = dg_all * ea * _sigmoid(xa)
        beta = _sigmoid(ab)
        dab_ref[...] = da + db_all * beta * (1.0 - beta)
        small_ref[0:1, :] += _sum0(dg_all * g_all)
        small_ref[1:2, :] += _sum0(da)

    hm = pl.BlockSpec((None, H, tm, Dh), lambda b, t: (b, 0, t, 0))
    row = pl.BlockSpec((1, LANES), lambda b, t: (0, 0))
    return pl.pallas_call(
        body, name="dn_conv_bwd", grid=(B, T // tm),
        in_specs=[hm] * 5 + [pl.BlockSpec((None, tm, W3), lambda b, t: (b, t, 0)), _past_halo_spec(tm, SCONV_HALO, W3),
                             pl.BlockSpec((None, tm, LANES), lambda b, t: (b, t, 0)),
                             pl.BlockSpec((K, W3), lambda b, t: (0, 0)), row, row],
        out_specs=[pl.BlockSpec((None, tm, W3), lambda b, t: (b, t, 0)), pl.BlockSpec((None, tm, LANES), lambda b, t: (b, t, 0)),
                   pl.BlockSpec((None, 2, LANES), lambda b, t: (b, 0, 0))],
        out_shape=[jax.ShapeDtypeStruct((B, T, W3), F32), jax.ShapeDtypeStruct((B, T, LANES), F32),
                   jax.ShapeDtypeStruct((B, 2, LANES), F32)],
        scratch_shapes=[pltpu.VMEM((tm + SCONV_HALO, W3), F32)],
        compiler_params=_cparams(2),
    )(dq, dk, dv, dgb, dbb, pre, pre, ab, w_sconv, alog_row, dt_row)


def dn_proj_bwd(x, dres, dc, pre, dz, dab, mod3, g, w_main, w_ab, w_sconv):
    B, T, D = x.shape
    W3 = dc.shape[2]
    W = W3 // 3
    K = w_sconv.shape[0]
    tm = _tile(T, TOKENS_PER_STEP_WIDE)
    nt = T // tm

    def body(x_ref, dres_ref, dc_ref, dch_ref, pre_ref, preh_ref, dz_ref, dab_ref, mod_ref, g_ref, wm_ref, wab_ref, ws_ref,
             dx_ref, h_ref, dproj_ref, dws_ref, dmod_ref, dg_ref, extp_s, extd_s):
        t = pl.program_id(1)

        @pl.when(t == 0)
        def _():
            dws_ref[...] = jnp.zeros_like(dws_ref)
            dmod_ref[...] = jnp.zeros_like(dmod_ref)
            dg_ref[...] = jnp.zeros_like(dg_ref)

        dc = dc_ref[...]
        extp_s[0:SCONV_HALO, :] = jnp.where(t > 0, preh_ref[...], 0.0)
        extp_s[SCONV_HALO:, :] = pre_ref[...]
        extd_s[0:tm, :] = dc
        extd_s[tm:, :] = jnp.where(t < nt - 1, dch_ref[...], 0.0)
        dpre = jnp.zeros((tm, W3), F32)
        for k in range(K):
            dpre = dpre + ws_ref[k:k + 1, :] * extd_s[pl.ds(K - 1 - k, tm), :]
            dws_ref[k:k + 1, :] += _sum0(dc * extp_s[pl.ds(SCONV_HALO - (K - 1) + k, tm), :])
        dpre = dpre.astype(BF16)
        dzb = dz_ref[...].astype(BF16)
        dproj_ref[:, 0:W3] = dpre
        dproj_ref[:, W3:] = dzb
        dh = _mm_nt(dab_ref[...], wab_ref[...]) + _mm_nt(dzb, wm_ref[:, W3:])
        for p in range(3):
            dh = dh + _mm_nt(dpre[:, p * W:(p + 1) * W], wm_ref[:, p * W:(p + 1) * W])
        xv = x_ref[...]
        h_ref[...] = _modnorm(xv, g_ref[...], mod_ref[1:2, :], mod_ref[0:1, :]).astype(BF16)
        dxn, dg, dscale, dshift = _modnorm_bwd(xv, g_ref[...], mod_ref[1:2, :], dh)
        dx_ref[...] = dres_ref[...] + dxn
        dmod_ref[0:1, :] += dshift
        dmod_ref[1:2, :] += dscale
        dg_ref[...] += dg

    tok = pl.BlockSpec((None, tm, D), lambda b, t: (b, t, 0))
    tok3 = pl.BlockSpec((None, tm, W3), lambda b, t: (b, t, 0))
    return pl.pallas_call(
        body, name="dn_proj_bwd", grid=(B, nt),
        in_specs=[tok, tok, tok3, _future_halo_spec(tm, SCONV_HALO, W3, T), tok3, _past_halo_spec(tm, SCONV_HALO, W3),
                  pl.BlockSpec((None, tm, W), lambda b, t: (b, t, 0)), pl.BlockSpec((None, tm, LANES), lambda b, t: (b, t, 0)),
                  pl.BlockSpec((None, 3, D), lambda b, t: (b, 0, 0)), pl.BlockSpec((1, D), lambda b, t: (0, 0)),
                  pl.BlockSpec((D, 4 * W), lambda b, t: (0, 0)), pl.BlockSpec((D, LANES), lambda b, t: (0, 0)),
                  pl.BlockSpec((K, W3), lambda b, t: (0, 0))],
        out_specs=[tok, tok, pl.BlockSpec((None, tm, 4 * W), lambda b, t: (b, t, 0)),
                   pl.BlockSpec((None, K, W3), lambda b, t: (b, 0, 0)), pl.BlockSpec((None, 3, D), lambda b, t: (b, 0, 0)),
                   pl.BlockSpec((None, 1, D), lambda b, t: (b, 0, 0))],
        out_shape=[jax.ShapeDtypeStruct((B, T, D), F32), jax.ShapeDtypeStruct((B, T, D), BF16),
                   jax.ShapeDtypeStruct((B, T, 4 * W), BF16), jax.ShapeDtypeStruct((B, K, W3), F32),
                   jax.ShapeDtypeStruct((B, 3, D), F32), jax.ShapeDtypeStruct((B, 1, D), F32)],
        scratch_shapes=[pltpu.VMEM((tm + SCONV_HALO, W3), F32), pltpu.VMEM((tm + SCONV_HALO, W3), F32)],
        compiler_params=_cparams(2),
    )(x, dres, dc, dc, pre, pre, dz, dab, mod3, g, w_main, w_ab, w_sconv)


def ada_fwd(c_all, w_ada, b_cols):
    L, D, Ca = w_ada.shape
    NB = c_all.shape[0]

    def body(c_ref, w_ref, b_ref, o_ref):
        cv = c_ref[...]
        o_ref[...] = _mm(cv * _sigmoid(cv), w_ref[...]) + b_ref[...]

    return pl.pallas_call(
        body, name="ada_fwd", grid=(L,),
        in_specs=[pl.BlockSpec((NB, D), lambda i: (0, 0)), pl.BlockSpec((None, D, Ca), lambda i: (i, 0, 0)),
                  pl.BlockSpec((None, 1, Ca), lambda i: (i, 0, 0))],
        out_specs=pl.BlockSpec((None, NB, Ca), lambda i: (i, 0, 0)),
        out_shape=jax.ShapeDtypeStruct((L, NB, Ca), F32),
        compiler_params=_cparams(1),
    )(c_all, w_ada, b_cols)


def ada_bwd(c_all, dmod_cols, dmod_all):
    L, NB, Ca = dmod_cols.shape
    D = c_all.shape[1]
    C9 = dmod_all.shape[2]

    def body(c_ref, dc_ref, da_ref, gw_ref, gb_ref):
        cv = c_ref[...]
        gw_ref[...] = _mm_tn(cv * _sigmoid(cv), dc_ref[...])
        gb_ref[...] = _sum0(da_ref[...])

    return pl.pallas_call(
        body, name="ada_bwd", grid=(L,),
        in_specs=[pl.BlockSpec((NB, D), lambda i: (0, 0)), pl.BlockSpec((None, NB, Ca), lambda i: (i, 0, 0)),
                  pl.BlockSpec((None, NB, C9), lambda i: (i, 0, 0))],
        out_specs=[pl.BlockSpec((None, D, Ca), lambda i: (i, 0, 0)), pl.BlockSpec((None, 1, C9), lambda i: (i, 0, 0))],
        out_shape=[jax.ShapeDtypeStruct((L, D, Ca), F32), jax.ShapeDtypeStruct((L, 1, C9), F32)],
        compiler_params=_cparams(1),
    )(c_all, dmod_cols, dmod_all)


def adamw(w, g, m, v, name, token=None):
    R, C = w.shape
    tr = _tile(R, max(8, ELEMENTWISE_BLOCK // C))
    if token is None:
        token = jnp.zeros((8, LANES), F32)

    def body(w_ref, g_ref, m_ref, v_ref, t_ref, d_ref, mo_ref, vo_ref):
        gv = g_ref[...] + t_ref[0:1, 0:1]
        mn = ADAM_B1 * m_ref[...] + (1.0 - ADAM_B1) * gv
        vn = ADAM_B2 * v_ref[...] + (1.0 - ADAM_B2) * (gv * gv)
        m_hat = mn / (1.0 - ADAM_B1 ** ADAM_STEP)
        v_hat = vn / (1.0 - ADAM_B2 ** ADAM_STEP)
        d_ref[...] = -ADAM_LR * (m_hat / (jnp.sqrt(v_hat) + ADAM_EPS) + ADAM_WD * w_ref[...])
        mo_ref[...] = mn
        vo_ref[...] = vn

    blk = pl.BlockSpec((tr, C), lambda i: (i, 0))
    return pl.pallas_call(
        body, name=name, grid=(R // tr,), in_specs=[blk] * 4 + [pl.BlockSpec((8, LANES), lambda i: (0, 0))],
        out_specs=[blk] * 3, out_shape=[jax.ShapeDtypeStruct((R, C), F32)] * 3, compiler_params=_cparams(1),
    )(w, g, m, v, token)


def sum_devices(a):
    n, R, C = a.shape

    def body(a_ref, o_ref):
        s = a_ref[0]
        for d in range(1, n):
            s = s + a_ref[d]
        o_ref[...] = s

    return pl.pallas_call(
        body, name="sum_devices", out_shape=jax.ShapeDtypeStruct((R, C), F32),
        compiler_params=pltpu.CompilerParams(vmem_limit_bytes=VMEM_LIMIT_V7X),
    )(a)


def _place():
    x, y, c = lax.axis_index("x"), lax.axis_index("y"), lax.axis_index("c")
    return x, y, c


def _other_chips(x, y):
    return [(2 * (1 - x) + y, 1 - x, y), (2 * x + (1 - y), x, 1 - y), (2 * (1 - x) + (1 - y), 1 - x, 1 - y)]


def allgather8(block):
    m_per, n = block.shape

    def body(x_ref, out_ref, send_sems, recv_sems, local_sem):
        x, y, c = _place()
        me, sibling = (x, y, c), (x, y, 1 - c)
        chips = [(1 - x, y), (x, 1 - y), (1 - x, 1 - y)]

        def rows(px, py, pc):
            return out_ref.at[pl.ds((4 * px + 2 * py + pc) * m_per, m_per), :]

        def copy(k, blk, to, src=None):
            return pltpu.make_async_remote_copy(
                src_ref=rows(*blk) if src is None else src, dst_ref=rows(*blk),
                send_sem=send_sems.at[k], recv_sem=recv_sems.at[k], device_id=to, device_id_type=MESH)

        mine = pltpu.make_async_copy(x_ref, rows(*me), local_sem)
        mine.start()
        first = [copy(0, me, sibling, src=x_ref)]
        first += [copy(1 + j, me, (*chip, c), src=x_ref) for j, chip in enumerate(chips)]
        for cp in first:
            cp.start()
        passed = [copy(4 + j, (*chip, c), sibling) for j, chip in enumerate(chips)]
        for j, chip in enumerate(chips):
            copy(1 + j, (*chip, c), me).wait_recv()
            passed[j].start()
        copy(0, sibling, me).wait_recv()
        for j, chip in enumerate(chips):
            copy(4 + j, (*chip, 1 - c), me).wait_recv()
        for cp in first + passed:
            cp.wait_send()
        mine.wait()

    return pl.pallas_call(
        body, name="allgather8", out_shape=jax.ShapeDtypeStruct((N_DEV * m_per, n), block.dtype),
        in_specs=[pl.BlockSpec(memory_space=pltpu.VMEM)], out_specs=pl.BlockSpec(memory_space=pltpu.VMEM),
        scratch_shapes=[pltpu.SemaphoreType.DMA((7,)), pltpu.SemaphoreType.DMA((7,)), pltpu.SemaphoreType.DMA],
        compiler_params=pltpu.CompilerParams(vmem_limit_bytes=VMEM_LIMIT_V7X),
    )(block)


def _half(ref, c, rh):
    return ref.at[pl.ds(pl.multiple_of(c * rh, 16), rh), :]


def pair_exchange(grads):
    K = len(grads)

    def body(*refs):
        ins, outs = refs[:K], refs[K:2 * K]
        send_sems, recv_sems = refs[2 * K:]
        x, y, c = _place()
        sibling = (x, y, 1 - c)
        copies = []
        for k in range(K):
            n, r, _ = ins[k].shape
            rh = r // 2
            cp = pltpu.make_async_remote_copy(
                src_ref=ins[k].at[:, pl.ds(pl.multiple_of((1 - c) * rh, 16), rh), :], dst_ref=outs[k],
                send_sem=send_sems.at[k], recv_sem=recv_sems.at[k], device_id=sibling, device_id_type=MESH)
            cp.start()
            copies.append(cp)
        for cp in copies:
            cp.wait_recv()
        for cp in copies:
            cp.wait_send()

    return pl.pallas_call(
        body, name="pair_exchange",
        out_shape=[jax.ShapeDtypeStruct((g.shape[0], g.shape[1] // 2, g.shape[2]), g.dtype) for g in grads],
        in_specs=[HBM_SPEC] * K, out_specs=[HBM_SPEC] * K,
        scratch_shapes=[pltpu.SemaphoreType.DMA((K,))] * 2,
    )(*grads)


def pair_add(grad, recv, c_idx):
    n, r, C = grad.shape
    rh = r // 2
    tr = _tile(rh, max(16, ELEMENTWISE_BLOCK // C), 16)
    grad = grad.reshape(n, 2, rh, C)

    def body(c_ref, g_ref, r_ref, o_ref):
        o_ref[...] = (g_ref[...].astype(F32) + r_ref[...].astype(F32)).astype(BF16)

    return pl.pallas_call(
        body, name="pair_add",
        grid_spec=pltpu.PrefetchScalarGridSpec(
            num_scalar_prefetch=1, grid=(n, rh // tr),
            in_specs=[pl.BlockSpec((None, None, tr, C), lambda d, i, c_ref: (d, c_ref[0], i, 0)),
                      pl.BlockSpec((None, tr, C), lambda d, i, c_ref: (d, i, 0))],
            out_specs=pl.BlockSpec((None, tr, C), lambda d, i, c_ref: (d, i, 0))),
        out_shape=jax.ShapeDtypeStruct((n, rh, C), BF16), compiler_params=_cparams(2),
    )(c_idx, grad, recv)


def chip_sum(parts, got, where, stack, slot):
    _, rh, C = parts.shape
    tr = _tile(rh, max(16, ELEMENTWISE_BLOCK // C), 16)
    nt = rh // tr

    def body(w_ref, p_ref, g_ref, stack_any, o_ref):
        s = p_ref[...].astype(F32)
        for r in range(3):
            s = s + g_ref[r].astype(F32)
        o_ref[...] = s

    return pl.pallas_call(
        body, name="chip_sum",
        grid_spec=pltpu.PrefetchScalarGridSpec(
            num_scalar_prefetch=1, grid=(nt,),
            in_specs=[pl.BlockSpec((None, tr, C), lambda i, w_ref: (w_ref[0], i, 0)),
                      pl.BlockSpec((3, tr, C), lambda i, w_ref: (0, i, 0)),
                      pl.BlockSpec(memory_space=pl.ANY)],
            out_specs=pl.BlockSpec((None, tr, C), lambda i, w_ref: (slot, w_ref[1] * nt + i, 0))),
        out_shape=jax.ShapeDtypeStruct(stack.shape, F32), input_output_aliases={3: 0},
        compiler_params=_cparams(1),
    )(where, parts, got, stack)


def pair_share(stacks, slots):
    K = len(stacks)
    jobs = [(k, s) for k in range(K) for s in slots[k]]

    def body(*refs):
        ins, outs = refs[:K], refs[K:2 * K]
        send_sems, recv_sems = refs[2 * K:]
        x, y, c = _place()
        sibling = (x, y, 1 - c)
        started = []
        for n, (k, s) in enumerate(jobs):
            rh = ins[k].shape[1] // 2
            cp = pltpu.make_async_remote_copy(
                src_ref=_half(ins[k].at[s], c, rh), dst_ref=_half(outs[k].at[s], c, rh), send_sem=send_sems.at[n],
                recv_sem=recv_sems.at[n], device_id=sibling, device_id_type=MESH)
            cp.start()
            started.append(cp)
        for n, (k, s) in enumerate(jobs):
            rh = ins[k].shape[1] // 2
            theirs = _half(outs[k].at[s], 1 - c, rh)
            pltpu.make_async_remote_copy(
                src_ref=theirs, dst_ref=theirs, send_sem=send_sems.at[n], recv_sem=recv_sems.at[n],
                device_id=sibling, device_id_type=MESH).wait_recv()
        for cp in started:
            cp.wait_send()

    return pl.pallas_call(
        body, name="pair_share",
        out_shape=[jax.ShapeDtypeStruct(s.shape, s.dtype) for s in stacks],
        in_specs=[HBM_SPEC] * K, out_specs=[HBM_SPEC] * K, input_output_aliases={k: k for k in range(K)},
        scratch_shapes=[pltpu.SemaphoreType.DMA((len(jobs),))] * 2,
    )(*stacks)


SEM_SPEC = pl.BlockSpec(memory_space=pltpu.SEMAPHORE)
ANY_SPEC = pl.BlockSpec(memory_space=pl.ANY)
DATAFLOW = pltpu.SideEffectType.DATAFLOW_SIDE_EFFECTING


def _in_hbm(a):
    return pltpu.with_memory_space_constraint(a, pltpu.HBM)


def _ici_copies(srcs, dsts, send_sems, recv_sems, src_slice, dst_slice):
    x, y, c = _place()
    out = []
    for k in range(len(srcs)):
        for r, (pchip, px, py) in enumerate(_other_chips(x, y)):
            out.append(pltpu.make_async_remote_copy(
                src_ref=src_slice(srcs[k], r, pchip), dst_ref=dst_slice(dsts[k], r, pchip),
                send_sem=send_sems.at[3 * k + r], recv_sem=recv_sems.at[3 * k + r], device_id=(px, py, c),
                device_id_type=MESH))
    return out


def _pair_copies(srcs, dsts, send_sems, recv_sems, src_slice, dst_slice):
    x, y, c = _place()
    return [pltpu.make_async_remote_copy(
        src_ref=src_slice(srcs[k]), dst_ref=dst_slice(dsts[k]), send_sem=send_sems.at[k], recv_sem=recv_sems.at[k],
        device_id=(x, y, 1 - c), device_id_type=MESH) for k in range(len(srcs))]


def _exchange_start(bufs, lands, src_slice, dst_slice, name, after=None, copies=_ici_copies, per=3):
    K = len(bufs)
    same = lands is None
    n_thru = K if same else 2 * K
    n_in = n_thru + (after is not None)

    def body(*refs):
        ins = refs[:n_thru]
        send_sems, recv_sems = refs[n_in], refs[n_in + 1]
        token = refs[-1]
        srcs = ins[:K]
        dsts = srcs if same else ins[K:]
        for cp in copies(srcs, dsts, send_sems, recv_sems, src_slice, dst_slice):
            cp.start()
        token[...] = jnp.zeros_like(token)

    thru = list(bufs) + ([] if same else list(lands))
    res = pl.pallas_call(
        body, name=name,
        out_shape=[pltpu.SemaphoreType.DMA((per * K,)), pltpu.SemaphoreType.DMA((per * K,))]
        + [pltpu.HBM(a.shape, a.dtype) for a in thru] + [jax.ShapeDtypeStruct((8, LANES), F32)],
        in_specs=[HBM_SPEC] * n_thru + [ANY_SPEC] * (after is not None),
        out_specs=[SEM_SPEC, SEM_SPEC] + [HBM_SPEC] * n_thru + [pl.BlockSpec(memory_space=pltpu.VMEM)],
        input_output_aliases={i: 2 + i for i in range(n_thru)},
        compiler_params=pltpu.CompilerParams(has_side_effects=DATAFLOW),
    )(*[_in_hbm(a) for a in thru], *([] if after is None else [after]))
    return res[0], res[1], res[2:2 + K], (res[2:2 + K] if same else res[2 + K:2 + 2 * K]), res[-1]


def _exchange_wait(send_sems, recv_sems, bufs, lands, after, src_slice, dst_slice, name, copies=_ici_copies):
    K = len(bufs)
    same = lands is None
    n_thru = K if same else 2 * K

    def body(*refs):
        ins = refs[:n_thru]
        ssem, rsem = refs[n_thru], refs[n_thru + 1]
        srcs = ins[:K]
        dsts = srcs if same else ins[K:]
        started = copies(srcs, dsts, ssem, rsem, src_slice, dst_slice)
        for cp in started:
            cp.wait_send()
        for cp in started:
            cp.wait_recv()

    thru = list(bufs) + ([] if same else list(lands))
    res = pl.pallas_call(
        body, name=name,
        out_shape=[pltpu.HBM(a.shape, a.dtype) for a in thru],
        in_specs=[HBM_SPEC] * n_thru + [SEM_SPEC, SEM_SPEC, ANY_SPEC],
        out_specs=[HBM_SPEC] * n_thru,
        input_output_aliases={i: i for i in range(n_thru)},
        compiler_params=pltpu.CompilerParams(has_side_effects=DATAFLOW),
    )(*thru, send_sems, recv_sems, after)
    return res[:K], (res[:K] if same else res[K:])


def _own_half(ref, r, pchip):
    x, y, c = _place()
    return _half(ref.at[2 * x + y], c, ref.shape[1] // 2)


def _their_half(ref, r, pchip):
    _, _, c = _place()
    return _half(ref.at[pchip], c, ref.shape[1] // 2)


def gather_start(lands, name, after=None):
    return _exchange_start(lands, None, _own_half, _own_half, name, after)


def gather_wait(handle, after, name):
    ssem, rsem, lands, _, _ = handle
    return _exchange_wait(ssem, rsem, lands, None, after, _own_half, _their_half, name)[1]


def pair_forward(lands):
    K = len(lands)

    def body(*refs):
        ins, outs = refs[:K], refs[K:2 * K]
        send_sems, recv_sems = refs[2 * K:]
        x, y, c = _place()
        sibling = (x, y, 1 - c)
        started = []
        for k in range(K):
            rh = ins[k].shape[1] // 2
            for r, (pchip, _, _) in enumerate(_other_chips(x, y)):
                cp = pltpu.make_async_remote_copy(
                    src_ref=_half(ins[k].at[pchip], c, rh), dst_ref=_half(outs[k].at[pchip], c, rh),
                    send_sem=send_sems.at[k, r], recv_sem=recv_sems.at[k, r], device_id=sibling, device_id_type=MESH)
                cp.start()
                started.append(cp)
        for k in range(K):
            rh = ins[k].shape[1] // 2
            for r, (pchip, _, _) in enumerate(_other_chips(x, y)):
                theirs = _half(outs[k].at[pchip], 1 - c, rh)
                pltpu.make_async_remote_copy(
                    src_ref=theirs, dst_ref=theirs, send_sem=send_sems.at[k, r], recv_sem=recv_sems.at[k, r],
                    device_id=sibling, device_id_type=MESH).wait_recv()
        for cp in started:
            cp.wait_send()

    return pl.pallas_call(
        body, name="pair_forward",
        out_shape=[jax.ShapeDtypeStruct(s.shape, s.dtype) for s in lands],
        in_specs=[HBM_SPEC] * K, out_specs=[HBM_SPEC] * K, input_output_aliases={k: k for k in range(K)},
        scratch_shapes=[pltpu.SemaphoreType.DMA((K, 3))] * 2,
    )(*lands)


def _to_chip(ref, r, pchip):
    return ref.at[pchip]


def _from_relation(ref, r, pchip):
    return ref.at[r]


def _other_rows(ref):
    _, _, c = _place()
    rh = ref.shape[1] // 2
    return ref.at[:, pl.ds(pl.multiple_of((1 - c) * rh, 16), rh), :]


def _whole(ref):
    return ref


def pair_start(grads, name):
    lands = [lax.empty((g.shape[0], g.shape[1] // 2, g.shape[2]), g.dtype) for g in grads]
    return _exchange_start(grads, lands, _other_rows, _whole, name, copies=_pair_copies, per=1)


def pair_finish(handle, after, name):
    ssem, rsem, grads, lands, _ = handle
    return _exchange_wait(ssem, rsem, grads, lands, after, _other_rows, _whole, name, copies=_pair_copies)


def reduce_start(grads, c_idx, name, after=None, recv=None):
    if recv is None:
        recv = pair_exchange(grads)
    parts = [pair_add(g, r, c_idx) for g, r in zip(grads, recv)]
    lands = [lax.empty((3,) + p.shape[1:], p.dtype) for p in parts]
    return _exchange_start(parts, lands, _to_chip, _from_relation, name, after)


def reduce_finish(handle, after, where, name, stacks, targets):
    ssem, rsem, parts, lands, _ = handle
    parts, got = _exchange_wait(ssem, rsem, parts, lands, after, _to_chip, _from_relation, name)
    stacks = dict(stacks)
    for p, g, (key, slot) in zip(parts, got, targets):
        stacks[key] = chip_sum(p, g, where, stacks[key], slot)
    keys = list(dict.fromkeys(key for key, _ in targets))
    shared = pair_share([stacks[k] for k in keys], [[s for key, s in targets if key == k] for k in keys])
    stacks.update(zip(keys, shared))
    return stacks


def _pack(arrs):
    flat = jnp.concatenate([a.reshape(-1).astype(F32) for a in arrs])
    pad = (-flat.shape[0]) % (8 * LANES)
    return jnp.pad(flat, (0, pad)).reshape(-1, LANES)


def _unpack(flat, shapes):
    out, off = [], 0
    for s in shapes:
        n = 1
        for d in s:
            n *= d
        out.append(flat[off:off + n].reshape(s))
        off += n
    return out


def _adamw_any(w, g, m, v, name, token=None):
    shp = w.shape
    C = shp[-1]
    d, nm, nv = adamw(w.reshape(-1, C), g.reshape(-1, C), m.reshape(-1, C), v.reshape(-1, C), name, token)
    return d.reshape(shp), nm.reshape(shp), nv.reshape(shp)


def kernel(x, c, norm_g, w_ada, b_ada, w_ffn_in, w_ffn_out, cm_w_glu, cm_b_glu, cm_w_dw, cm_b_dw, cm_ln_g, cm_ln_b, cm_w_pw, cm_b_pw, dn_w_in, dn_w_sconv, dn_a_log, dn_dt_bias, dn_o_g, dn_w_out, final_g, loss_target, m_norm_g, m_w_ada, m_b_ada, m_w_ffn_in, m_w_ffn_out, m_cm_w_glu, m_cm_b_glu, m_cm_w_dw, m_cm_b_dw, m_cm_ln_g, m_cm_ln_b, m_cm_w_pw, m_cm_b_pw, m_dn_w_in, m_dn_w_sconv, m_dn_a_log, m_dn_dt_bias, m_dn_o_g, m_dn_w_out, m_final_g, v_norm_g, v_w_ada, v_b_ada, v_w_ffn_in, v_w_ffn_out, v_cm_w_glu, v_cm_b_glu, v_cm_w_dw, v_cm_b_dw, v_cm_ln_g, v_cm_ln_b, v_cm_w_pw, v_cm_b_pw, v_dn_w_in, v_dn_w_sconv, v_dn_a_log, v_dn_dt_bias, v_dn_o_g, v_dn_w_out, v_final_g):
    weights = dict(norm_g=norm_g, w_ada=w_ada, b_ada=b_ada, w_ffn_in=w_ffn_in, w_ffn_out=w_ffn_out, cm_w_glu=cm_w_glu,
                   cm_b_glu=cm_b_glu, cm_w_dw=cm_w_dw, cm_b_dw=cm_b_dw, cm_ln_g=cm_ln_g, cm_ln_b=cm_ln_b, cm_w_pw=cm_w_pw,
                   cm_b_pw=cm_b_pw, dn_w_in=dn_w_in, dn_w_sconv=dn_w_sconv, dn_a_log=dn_a_log, dn_dt_bias=dn_dt_bias,
                   dn_o_g=dn_o_g, dn_w_out=dn_w_out, final_g=final_g)
    mom_m = dict(norm_g=m_norm_g, w_ada=m_w_ada, b_ada=m_b_ada, w_ffn_in=m_w_ffn_in, w_ffn_out=m_w_ffn_out,
                 cm_w_glu=m_cm_w_glu, cm_b_glu=m_cm_b_glu, cm_w_dw=m_cm_w_dw, cm_b_dw=m_cm_b_dw, cm_ln_g=m_cm_ln_g,
                 cm_ln_b=m_cm_ln_b, cm_w_pw=m_cm_w_pw, cm_b_pw=m_cm_b_pw, dn_w_in=m_dn_w_in, dn_w_sconv=m_dn_w_sconv,
                 dn_a_log=m_dn_a_log, dn_dt_bias=m_dn_dt_bias, dn_o_g=m_dn_o_g, dn_w_out=m_dn_w_out, final_g=m_final_g)
    mom_v = dict(norm_g=v_norm_g, w_ada=v_w_ada, b_ada=v_b_ada, w_ffn_in=v_w_ffn_in, w_ffn_out=v_w_ffn_out,
                 cm_w_glu=v_cm_w_glu, cm_b_glu=v_cm_b_glu, cm_w_dw=v_cm_w_dw, cm_b_dw=v_cm_b_dw, cm_ln_g=v_cm_ln_g,
                 cm_ln_b=v_cm_ln_b, cm_w_pw=v_cm_w_pw, cm_b_pw=v_cm_b_pw, dn_w_in=v_dn_w_in, dn_w_sconv=v_dn_w_sconv,
                 dn_a_log=v_dn_a_log, dn_dt_bias=v_dn_dt_bias, dn_o_g=v_dn_o_g, dn_w_out=v_dn_w_out, final_g=v_final_g)
    names = list(weights)

    BL, T, D = x.shape
    L = norm_g.shape[0]
    NB = BL * N_DEV
    Ca = w_ada.shape[2]
    C9 = b_ada.shape[1]
    H = dn_a_log.shape[1]
    Dh = dn_o_g.shape[1]
    W = H * Dh
    KC = cm_w_dw.shape[1]
    n_cm, n_dn = cm_w_glu.shape[0], dn_w_in.shape[0]
    ax, ay, ac = lax.axis_index("x"), lax.axis_index("y"), lax.axis_index("c")
    chip = 2 * ax + ay
    dev = 2 * chip + ac
    c_idx = ac.astype(jnp.int32).reshape(1)
    where = jnp.stack([chip, ac]).astype(jnp.int32)

    def landing(s, tok=None):
        s = s if tok is None else s + tok
        return lax.dynamic_update_slice(lax.empty((N_CHIPS,) + s.shape, BF16), s.astype(BF16)[None], (chip, 0, 0))

    def layer_shards(i):
        sh = [w_ffn_in[i, 0], w_ffn_in[i, 1], w_ffn_out[i, 0], w_ffn_out[i, 1]]
        if i % 2 == 0:
            sh += [cm_w_glu[i // 2], cm_w_pw[i // 2]]
        else:
            sh += [dn_w_in[i // 2], dn_w_out[i // 2]]
        return sh

    wts = [None] * L

    small_in = [c, norm_g, cm_w_dw, dn_w_sconv]
    gathered = allgather8(_pack(small_in)).reshape(N_DEV, -1)
    per_dev = [_unpack(gathered[d], [a.shape for a in small_in]) for d in range(N_DEV)]
    c_all = jnp.concatenate([p[0] for p in per_dev], axis=0)
    norm_g_full = jnp.concatenate([per_dev[2 * s][1] for s in range(N_CHIPS)], axis=-1)
    w_dw_full = jnp.concatenate([per_dev[2 * s][2] for s in range(N_CHIPS)], axis=-1)
    w_sconv_full = jnp.concatenate([per_dev[2 * s][3] for s in range(N_CHIPS)], axis=-1)

    b_cols = lax.dynamic_slice_in_dim(b_ada, chip * Ca, Ca, axis=1).reshape(L, 1, Ca)
    mod_part = ada_fwd(c_all, w_ada, b_cols)
    mod_g = allgather8(mod_part.reshape(-1, LANES))
    shards0 = layer_shards(0)
    first = gather_start([landing(shards0[0]), landing(shards0[2])], "gather_start_0a", mod_g)
    tok0 = first[4][0, 0]
    second = gather_start([landing(shards0[k], tok0) for k in (4, 5)], "gather_start_0b", first[4])
    third = gather_start([landing(shards0[k], tok0) for k in (1, 3)], "gather_start_0c", second[4])
    lands = [None] + [[landing(s, tok0) for s in layer_shards(i)] for i in range(1, L)]
    mod_g = mod_g.reshape(N_DEV, L, NB, Ca)
    mod_all = jnp.concatenate([mod_g[2 * s] for s in range(N_CHIPS)], axis=-1)
    mod = lax.dynamic_slice_in_dim(mod_all, dev * BL, BL, axis=1).reshape(L, BL, 9, D)

    def dn_weights(i):
        full = jnp.transpose(wts[i][4], (1, 0, 2)).reshape(D, -1)
        return full[:, :4 * W], jnp.pad(full[:, 4 * W:], ((0, 0), (0, LANES - 2 * H)))

    def row128(v):
        return jnp.pad(v.reshape(1, -1), ((0, 0), (0, LANES - v.shape[-1])))

    def pad_taps(w):
        return jnp.pad(w, ((0, 1), (0, 0)))

    saved = []
    xs = x
    after = mod
    for i in range(L):
        tok = 0.0
        if i == 0:
            wl = wts[0] = [None] * 6
            wl[0], wl[2] = pair_forward(gather_wait(first, after, "gather_wait_0a"))
        else:
            wl = wts[i] = pair_forward(gather_wait(handle, after, "gather_wait_%d" % i))
            if i + 1 < L:
                handle = gather_start(lands[i + 1], "gather_start_%d" % (i + 1), wl[0])
                tok = handle[4][0, 0]
        sv = {}
        m3 = [mod[i, :, 3 * j:3 * j + 3] + tok for j in range(3)]
        gs = [norm_g_full[i, j].reshape(1, D) for j in range(3)]
        sv["x0"] = xs
        xs, sv["y0"], sv["h0"], sv["gu0"] = ffn_fwd(xs, m3[0], gs[0], wl[0], wl[2])
        sv["x1"] = xs
        if i == 0:
            wl[4], wl[5] = pair_forward(gather_wait(second, xs, "gather_wait_0b"))
            handle = gather_start(lands[1], "gather_start_1", wl[4])
            m3 = [m + handle[4][0, 0] for m in m3]
        if i % 2 == 0:
            a = i // 2
            sv["u"] = conv_glu_fwd(xs, m3[1], gs[1], wl[4], cm_b_glu[a].reshape(1, -1))
            xs, sv["y1"], sv["u2"] = conv_out_fwd(
                xs, sv["u"], m3[1], pad_taps(w_dw_full[a]), cm_b_dw[a].reshape(1, D), cm_ln_g[a].reshape(1, D),
                cm_ln_b[a].reshape(1, D), wl[5].reshape(D, D), cm_b_pw[a].reshape(1, D))
        else:
            a = i // 2
            w_main, w_ab = dn_weights(i)
            sv["pre"], sv["z"], sv["ab"] = dn_proj_fwd(xs, m3[1], gs[1], w_main, w_ab)
            qkvgb = dn_conv_fwd(sv["pre"], sv["ab"], w_sconv_full[a], row128(dn_a_log[a]), row128(dn_dt_bias[a]), H)
            sv["qkvgb"] = qkvgb
            sv["o"], sv["sp"], sv["inv"] = dn_chunk_fwd(*qkvgb)
            xs, sv["y1"] = dn_out_fwd(xs, sv["o"], sv["z"], m3[1], dn_o_g[a].reshape(1, Dh), wl[5].reshape(W, D))
        sv["x2"] = xs
        if i == 0:
            wl[1], wl[3] = pair_forward(gather_wait(third, xs, "gather_wait_0c"))
        xs, sv["y2"], sv["h2"], sv["gu2"] = ffn_fwd(xs, m3[2], gs[2], wl[1], wl[3])
        saved.append(sv)
        after = xs

    dx, d_final_g, loss_part = final_loss(xs, final_g.reshape(1, D), loss_target)

    d_norm_g = [[None] * 3 for _ in range(L)]
    dmod = [[None] * 3 for _ in range(L)]
    g_cm = {k: [None] * n_cm for k in ("b_glu", "w_dw", "b_dw", "ln_g", "ln_b", "b_pw")}
    g_dn = {k: [None] * n_dn for k in ("w_sconv", "a_log", "dt_bias", "o_g")}
    big_names = ("w_ffn_in", "w_ffn_out", "cm_w_glu", "cm_w_pw", "dn_w_in", "dn_w_out")
    stacks = {n: lax.empty((weights[n].size // (weights[n].shape[-2] * weights[n].shape[-1]),) + weights[n].shape[-2:], F32)
              for n in big_names}

    def targets(i, which):
        mix = ("cm_w_glu", "cm_w_pw") if i % 2 == 0 else ("dn_w_in", "dn_w_out")
        full = [("w_ffn_in", 2 * i), ("w_ffn_in", 2 * i + 1), ("w_ffn_out", 2 * i), ("w_ffn_out", 2 * i + 1),
                (mix[0], i // 2), (mix[1], i // 2)]
        return [full[k] for k in which]

    def ffn_back(i, j, slot, dx, tok=0.0):
        wl, sv = wts[i], saved[i]
        m3 = mod[i, :, 3 * j:3 * j + 3] + tok
        g = norm_g_full[i, j].reshape(1, D)
        gu = sv["gu%d" % j]
        ab_, dgu, dyb, dh0, dgate = ffn_bwd_part(0, dx, gu, m3, wl[slot], wl[2 + slot], y=sv["y%d" % j])
        dx, ab_, dgu, dm, dg = ffn_bwd_part(1, dx, gu, m3, wl[slot], wl[2 + slot], first=(ab_, dgu, dyb, dh0),
                                            x=sv["x%d" % j], g=g)
        dm = dm.at[:, 2:3, :].set(dgate)
        hb = sv["h%d" % j]
        dmod[i][j] = dm
        d_norm_g[i][j] = jnp.sum(dg, axis=(0, 1))
        Fc = wl[slot].shape[2]
        dw_in = matmul_tn(hb.reshape(-1, D), dgu.reshape(2, BL * T, 2 * Fc), Fc, "dw_ffn_in")
        dw_out = matmul_tn(ab_.reshape(-1, 2 * Fc), dyb.reshape(1, -1, D), D, "dw_ffn_out")
        return dx, dw_in, dw_out.reshape(N_CHIPS, -1, D)

    pending, paired, tok = None, None, 0.0
    for i in reversed(range(L)):
        wl, sv = wts[i], saved[i]
        a = i // 2
        dx, dw_in1, dw_out1 = ffn_back(i, 2, 1, dx, tok)
        m3 = mod[i, :, 3:6]
        if paired is not None:
            theirs, recv = pair_finish(paired[0], dx, "pair_wait_%d" % paired[1])
            started = reduce_start(theirs, c_idx, "reduce_start_%d" % paired[1], recv=recv)
            pending, paired = (started, paired[1]), None
            m3 = m3 + started[4][0, 0]
        g = norm_g_full[i, 1].reshape(1, D)
        if i % 2 == 0:
            w_pw = wl[5].reshape(D, D)
            wdw = pad_taps(w_dw_full[a])
            du2, u3b, dyb, dgate, vec = conv_out_bwd(dx, sv["y1"], sv["u2"], m3, cm_ln_g[a].reshape(1, D),
                                                     cm_ln_b[a].reshape(1, D), w_pw)
            dx, hb, dab, dwdw, dbglu, dm, dg = conv_glu_bwd(sv["x1"], dx, du2, sv["u"], m3, g, wl[4],
                                                            cm_b_glu[a].reshape(1, -1), wdw)
            dm = dm.at[:, 2:3, :].set(dgate)
            vec = jnp.sum(vec, axis=0)
            g_cm["b_pw"][a], g_cm["ln_g"][a], g_cm["ln_b"][a], g_cm["b_dw"][a] = vec[0], vec[1], vec[2], vec[3]
            g_cm["w_dw"][a] = jnp.sum(dwdw, axis=0)[:KC]
            g_cm["b_glu"][a] = jnp.sum(dbglu, axis=(0, 1))
            dw_a = matmul_tn(hb.reshape(-1, D), dab.reshape(1, -1, 2 * D), D // 2, "dw_glu")
            dw_b = matmul_tn(u3b.reshape(-1, D), dyb.reshape(1, -1, D), D, "dw_sq").reshape(N_CHIPS, -1, D)
        else:
            w_main, w_ab = dn_weights(i)
            w_out = wl[5].reshape(W, D)
            do, dz, ogb, dyb, dgate, dog = dn_out_bwd(dx, sv["y1"], sv["o"], sv["z"], m3, dn_o_g[a].reshape(1, Dh), w_out)
            dq, dk, dv, dgb, dbb = dn_chunk_bwd(*sv["qkvgb"], sv["sp"], sv["inv"], do)
            dc, dab, small = dn_conv_bwd(dq, dk, dv, dgb, dbb, sv["pre"], sv["ab"], w_sconv_full[a],
                                         row128(dn_a_log[a]), row128(dn_dt_bias[a]))
            dx, hb, dproj, dws, dm, dg = dn_proj_bwd(sv["x1"], dx, dc, sv["pre"], dz, dab, m3, g, w_main, w_ab,
                                                     w_sconv_full[a])
            dm = dm.at[:, 2:3, :].set(dgate)
            small = jnp.sum(small, axis=0)
            g_dn["a_log"][a], g_dn["dt_bias"][a] = small[0, :H], small[1, :H]
            g_dn["o_g"][a] = jnp.sum(dog, axis=(0, 1))
            g_dn["w_sconv"][a] = jnp.sum(dws, axis=0)
            dw_main = matmul_tn(hb.reshape(-1, D), dproj.reshape(1, -1, 4 * W), W, "dw_dn_main")
            dw_ab = matmul_tn(hb.reshape(-1, D), dab.reshape(1, -1, LANES), LANES, "dw_dn_ab")
            full = jnp.concatenate([jnp.transpose(dw_main, (1, 0, 2)).reshape(D, 4 * W), dw_ab[0][:, :2 * H]], axis=1)
            dw_a = jnp.transpose(full.reshape(D, N_CHIPS, -1), (1, 0, 2))
            dw_b = matmul_tn(ogb.reshape(-1, W), dyb.reshape(1, -1, D), D, "dw_sq").reshape(N_CHIPS, -1, D)
        dmod[i][1] = dm
        d_norm_g[i][1] = jnp.sum(dg, axis=(0, 1))
        if i > 0:
            dx, dw_in0, dw_out0 = ffn_back(i, 0, 0, dx)
            if pending is not None:
                stacks = reduce_finish(pending[0], dx, where, "reduce_wait_%d" % pending[1], stacks,
                                       targets(pending[1], range(6)))
                pending = None
            handed = pair_start([dw_in0, dw_in1, dw_out0, dw_out1, dw_a, dw_b], "pair_start_%d" % i)
            paired, tok = (handed, i), handed[4][0, 0]
        else:
            part_a = reduce_start([dw_in1, dw_out1, dw_a, dw_b], c_idx, "reduce_start_0a")
            dx, dw_in0, dw_out0 = ffn_back(0, 0, 0, dx, part_a[4][0, 0])
            if pending is not None:
                stacks = reduce_finish(pending[0], dx, where, "reduce_wait_%d" % pending[1], stacks,
                                       targets(pending[1], range(6)))
            stacks = reduce_finish(part_a, dx, where, "reduce_wait_0a", stacks, targets(0, (1, 3, 4, 5)))

    part = dict(
        norm_g=jnp.stack([jnp.stack(r) for r in d_norm_g]),
        cm_b_glu=jnp.stack(g_cm["b_glu"]), cm_w_dw=jnp.stack(g_cm["w_dw"]), cm_b_dw=jnp.stack(g_cm["b_dw"]),
        cm_ln_g=jnp.stack(g_cm["ln_g"]), cm_ln_b=jnp.stack(g_cm["ln_b"]), cm_b_pw=jnp.stack(g_cm["b_pw"]),
        dn_w_sconv=jnp.stack(g_dn["w_sconv"]), dn_a_log=jnp.stack(g_dn["a_log"]), dn_dt_bias=jnp.stack(g_dn["dt_bias"]),
        dn_o_g=jnp.stack(g_dn["o_g"]), final_g=jnp.sum(d_final_g, axis=(0, 1)),
        loss=jnp.sum(loss_part[:, 0, 0]).reshape(1))
    dmod_loc = jnp.stack([jnp.concatenate(r, axis=1) for r in dmod]).reshape(L, BL, C9)
    keys = list(part)
    packed = _pack([part[k] for k in keys] + [dmod_loc])
    R = packed.shape[0]
    gathered = allgather8(packed).reshape(N_DEV, R, LANES)
    summed = _unpack(sum_devices(gathered).reshape(-1), [part[k].shape for k in keys])
    tot = dict(zip(keys, summed))
    n_small = sum(int(part[k].size) for k in keys)
    dmod_all = gathered.reshape(N_DEV, -1)[:, n_small:n_small + L * BL * C9].reshape(N_DEV, L, BL, C9)
    dmod_all = jnp.transpose(dmod_all, (1, 0, 2, 3)).reshape(L, NB, C9)
    dmod_cols = lax.dynamic_slice_in_dim(dmod_all, chip * Ca, Ca, axis=2)
    g_w_ada, g_b_ada = ada_bwd(c_all, dmod_cols, dmod_all)
    delta, new_m, new_v = {}, {}, {}
    part_b = reduce_start([dw_in0, dw_out0], c_idx, "reduce_start_0b", g_w_ada)
    delta["w_ada"], new_m["w_ada"], new_v["w_ada"] = _adamw_any(w_ada, g_w_ada, m_w_ada, v_w_ada, "adamw_w_ada",
                                                                 part_b[4])
    stacks = reduce_finish(part_b, new_v["w_ada"], where, "reduce_wait_0b", stacks, targets(0, (0, 2)))

    def my_cols(full):
        n = full.shape[-1] // N_CHIPS
        return lax.dynamic_slice_in_dim(full, chip * n, n, axis=full.ndim - 1)

    grads = dict(
        norm_g=my_cols(tot["norm_g"]), w_ada=g_w_ada, b_ada=g_b_ada.reshape(L, C9),
        cm_b_glu=tot["cm_b_glu"], cm_w_dw=my_cols(tot["cm_w_dw"]), cm_b_dw=tot["cm_b_dw"], cm_ln_g=tot["cm_ln_g"],
        cm_ln_b=tot["cm_ln_b"], cm_b_pw=tot["cm_b_pw"], dn_w_sconv=my_cols(tot["dn_w_sconv"]),
        dn_a_log=tot["dn_a_log"], dn_dt_bias=tot["dn_dt_bias"], dn_o_g=tot["dn_o_g"], final_g=tot["final_g"],
        **{n: stacks[n].reshape(weights[n].shape) for n in big_names})

    large = ("w_ada", "w_ffn_in", "w_ffn_out", "cm_w_glu", "cm_w_pw", "dn_w_in", "dn_w_out")
    for n in large[1:]:
        delta[n], new_m[n], new_v[n] = _adamw_any(weights[n], grads[n], mom_m[n], mom_v[n], "adamw_" + n)
    rest = [n for n in names if n not in large]
    shapes = [weights[n].shape for n in rest]
    pd, pm, pv = adamw(_pack([weights[n] for n in rest]), _pack([grads[n] for n in rest]),
                       _pack([mom_m[n] for n in rest]), _pack([mom_v[n] for n in rest]), "adamw_small")
    for n, d_, m_, v_ in zip(rest, _unpack(pd.reshape(-1), shapes), _unpack(pm.reshape(-1), shapes),
                             _unpack(pv.reshape(-1), shapes)):
        delta[n], new_m[n], new_v[n] = d_, m_, v_

    return (tot["loss"].reshape(()), dx, *[grads[n] for n in names], *[delta[n] for n in names],
            *[new_m[n] for n in names], *[new_v[n] for n in names])
```

```python
import functools

import jax
import jax.numpy as jnp
from jax import lax
from jax.experimental import pallas as pl
from jax.experimental.pallas import tpu as pltpu

F32 = jnp.float32
BF16 = jnp.bfloat16
EPS = 1e-6
CHUNK = 64
CHUNKS_PER_STEP = 4
N_CHIPS = 4
N_DEV = 8
LANES = 128
SUBLANES = 8
CONV_HALO = 32
SCONV_HALO = 8
VMEM_LIMIT_V7X = 60 * 1024 * 1024
DW_VMEM_BUDGET = 40 * 1024 * 1024
TOKENS_PER_STEP = 512
TOKENS_PER_STEP_WIDE = 256
ELEMENTWISE_BLOCK = 1 << 19
MESH = pl.DeviceIdType.MESH
HBM_SPEC = pl.BlockSpec(memory_space=pltpu.HBM)

ADAM_LR, ADAM_B1, ADAM_B2, ADAM_EPS, ADAM_WD, ADAM_STEP = 0.001, 0.9, 0.999, 1e-08, 0.01, 10


def _cparams(n_axes):
    return pltpu.CompilerParams(dimension_semantics=("arbitrary",) * n_axes, vmem_limit_bytes=VMEM_LIMIT_V7X)


def _tile(n, pref, mult=8):
    for t in range(min(n, pref) // mult * mult, 0, -mult):
        if n % t == 0:
            return t
    return n


def _mm(a, b):
    return lax.dot_general(a.astype(BF16), b.astype(BF16), (((1,), (0,)), ((), ())), preferred_element_type=F32)


def _mm_nt(a, b):
    return lax.dot_general(a.astype(BF16), b.astype(BF16), (((1,), (1,)), ((), ())), preferred_element_type=F32)


def _mm_tn(a, b):
    return lax.dot_general(a.astype(BF16), b.astype(BF16), (((0,), (0,)), ((), ())), preferred_element_type=F32)


def _sigmoid(x):
    return jax.nn.sigmoid(x)


def _dsilu(x, s):
    return s * (1.0 + x * (1.0 - s))


def _softplus(x):
    return jnp.maximum(x, 0.0) + jnp.log(1.0 + jnp.exp(-jnp.abs(x)))


def _modnorm(x, g, scale, shift):
    r = lax.rsqrt(jnp.mean(x * x, axis=-1, keepdims=True) + EPS)
    return (x * r) * g * (1.0 + scale) + shift


def _modnorm_bwd(x, g, scale, dh):
    r = lax.rsqrt(jnp.mean(x * x, axis=-1, keepdims=True) + EPS)
    xn = x * r
    dshift = jnp.sum(dh, axis=0, keepdims=True)
    dscale = jnp.sum(dh * (xn * g), axis=0, keepdims=True)
    dhn = dh * (1.0 + scale)
    dg = jnp.sum(dhn * xn, axis=0, keepdims=True)
    dxn = dhn * g
    dx = r * (dxn - xn * jnp.mean(dxn * xn, axis=-1, keepdims=True))
    return dx, dg, dscale, dshift


def _sum0(a):
    return jnp.sum(a, axis=0, keepdims=True)


def ffn_fwd(x, mod3, g, w_in, w_out):
    B, T, D = x.shape
    Fc = w_in.shape[2]
    w_in = w_in.reshape(2, 2, D, Fc)
    w_out = w_out.reshape(2, Fc, D)
    tm = _tile(T, TOKENS_PER_STEP)

    def half(h, wi_ref, wo_ref, gu_ref):
        gt = _mm(h, wi_ref[0])
        up = _mm(h, wi_ref[1])
        gu_ref[0] = gt.astype(BF16)
        gu_ref[1] = up.astype(BF16)
        return _mm(gt * _sigmoid(gt) * up, wo_ref[...])

    def body_a(x_ref, mod_ref, g_ref, wi_ref, wo_ref, h_ref, gu_ref, y0_ref):
        h = _modnorm(x_ref[...], g_ref[...], mod_ref[1:2, :], mod_ref[0:1, :]).astype(BF16)
        h_ref[...] = h
        y0_ref[...] = half(h, wi_ref, wo_ref, gu_ref)

    def body_b(x_ref, h_ref, y0_ref, mod_ref, wi_ref, wo_ref, gu_any, xo_ref, y_ref, gu_ref):
        y = y0_ref[...] + half(h_ref[...], wi_ref, wo_ref, gu_ref)
        y_ref[...] = y.astype(BF16)
        xo_ref[...] = x_ref[...] + 0.5 * (1.0 + mod_ref[2:3, :]) * y

    tok = pl.BlockSpec((None, tm, D), lambda b, t: (b, t, 0))
    per_b3 = pl.BlockSpec((None, 3, D), lambda b, t: (b, 0, 0))
    gu_shape = jax.ShapeDtypeStruct((2, B, T, 2 * Fc), BF16)

    def w_specs(part):
        return [pl.BlockSpec((2, None, D, Fc), lambda b, t: (0, part, 0, 0)),
                pl.BlockSpec((None, Fc, D), lambda b, t: (part, 0, 0))]

    def gu_spec(part):
        return pl.BlockSpec((2, None, tm, Fc), lambda b, t: (0, b, t, part))

    h, gu, y0 = pl.pallas_call(
        body_a, name="ffn_fwd_a", grid=(B, T // tm),
        in_specs=[tok, per_b3, pl.BlockSpec((1, D), lambda b, t: (0, 0))] + w_specs(0),
        out_specs=[tok, gu_spec(0), tok],
        out_shape=[jax.ShapeDtypeStruct((B, T, D), BF16), gu_shape, jax.ShapeDtypeStruct((B, T, D), F32)],
        compiler_params=_cparams(2),
    )(x, mod3, g, w_in, w_out)
    x_new, y, gu = pl.pallas_call(
        body_b, name="ffn_fwd_b", grid=(B, T // tm),
        in_specs=[tok, tok, tok, per_b3] + w_specs(1) + [pl.BlockSpec(memory_space=pl.ANY)],
        out_specs=[tok, tok, gu_spec(1)],
        out_shape=[jax.ShapeDtypeStruct((B, T, D), F32), jax.ShapeDtypeStruct((B, T, D), BF16), gu_shape],
        input_output_aliases={6: 2},
        compiler_params=_cparams(2),
    )(x, h, y0, mod3, w_in, w_out, gu)
    return x_new, y, h, gu


def ffn_bwd_part(part, dres, gu, mod3, w_in, w_out, first=None, y=None, x=None, g=None):
    B, T, D = dres.shape
    Fc = w_in.shape[2]
    F = 2 * Fc
    w_in = w_in.reshape(2, 2, D, Fc)
    w_out = w_out.reshape(2, Fc, D)
    tm = _tile(T, TOKENS_PER_STEP_WIDE)

    def half(dy, gu_ref, wi_ref, wo_ref, a_ref, dgu_ref):
        gt = gu_ref[0].astype(F32)
        up = gu_ref[1].astype(F32)
        sg = _sigmoid(gt)
        silu = gt * sg
        a_ref[...] = (silu * up).astype(BF16)
        da = _mm_nt(dy, wo_ref[...])
        dup = (da * silu).astype(BF16)
        dgt = (da * up * _dsilu(gt, sg)).astype(BF16)
        dgu_ref[0] = dgt
        dgu_ref[1] = dup
        return _mm_nt(dgt, wi_ref[0]) + _mm_nt(dup, wi_ref[1])

    tok = pl.BlockSpec((None, tm, D), lambda b, t: (b, t, 0))
    per_b3 = pl.BlockSpec((None, 3, D), lambda b, t: (b, 0, 0))
    per_b1 = pl.BlockSpec((None, 1, D), lambda b, t: (b, 0, 0))
    gu_spec = pl.BlockSpec((2, None, tm, Fc), lambda b, t: (0, b, t, part))
    a_spec = pl.BlockSpec((None, tm, Fc), lambda b, t: (b, t, part))
    wi_spec = pl.BlockSpec((2, None, D, Fc), lambda b, t: (0, part, 0, 0))
    wo_spec = pl.BlockSpec((None, Fc, D), lambda b, t: (part, 0, 0))
    a_shape = jax.ShapeDtypeStruct((B, T, F), BF16)
    dgu_shape = jax.ShapeDtypeStruct((2, B, T, F), BF16)

    if part == 0:
        def body(dres_ref, y_ref, gu_ref, mod_ref, wi_ref, wo_ref, a_ref, dgu_ref, dy_ref, dh_ref, dgate_ref):
            @pl.when(pl.program_id(1) == 0)
            def _():
                dgate_ref[...] = jnp.zeros_like(dgate_ref)

            dres = dres_ref[...]
            dy = (0.5 * (1.0 + mod_ref[2:3, :]) * dres).astype(BF16)
            dy_ref[...] = dy
            dgate_ref[...] += _sum0(dres * (0.5 * y_ref[...]))
            dh_ref[...] = half(dy, gu_ref, wi_ref, wo_ref, a_ref, dgu_ref)

        return pl.pallas_call(
            body, name="ffn_bwd_a", grid=(B, T // tm),
            in_specs=[tok, tok, gu_spec, per_b3, wi_spec, wo_spec],
            out_specs=[a_spec, gu_spec, tok, tok, per_b1],
            out_shape=[a_shape, dgu_shape, jax.ShapeDtypeStruct((B, T, D), BF16), jax.ShapeDtypeStruct((B, T, D), F32),
                       jax.ShapeDtypeStruct((B, 1, D), F32)],
            compiler_params=_cparams(2),
        )(dres, y, gu, mod3, w_in, w_out)

    a_full, dgu_full, dy, dh0 = first

    def body(x_ref, dres_ref, dy_ref, dh0_ref, gu_ref, mod_ref, g_ref, wi_ref, wo_ref, a_any, dgu_any,
             dx_ref, a_ref, dgu_ref, dmod_ref, dg_ref):
        @pl.when(pl.program_id(1) == 0)
        def _():
            dmod_ref[...] = jnp.zeros_like(dmod_ref)
            dg_ref[...] = jnp.zeros_like(dg_ref)

        dh = dh0_ref[...] + half(dy_ref[...], gu_ref, wi_ref, wo_ref, a_ref, dgu_ref)
        dxn, dg, dscale, dshift = _modnorm_bwd(x_ref[...], g_ref[...], mod_ref[1:2, :], dh)
        dx_ref[...] = dres_ref[...] + dxn
        dmod_ref[0:1, :] += dshift
        dmod_ref[1:2, :] += dscale
        dg_ref[...] += dg

    return pl.pallas_call(
        body, name="ffn_bwd_b", grid=(B, T // tm),
        in_specs=[tok, tok, tok, tok, gu_spec, per_b3, pl.BlockSpec((1, D), lambda b, t: (0, 0)), wi_spec, wo_spec,
                  ANY_SPEC, ANY_SPEC],
        out_specs=[tok, a_spec, gu_spec, per_b3, per_b1],
        out_shape=[jax.ShapeDtypeStruct((B, T, D), F32), a_shape, dgu_shape, jax.ShapeDtypeStruct((B, 3, D), F32),
                   jax.ShapeDtypeStruct((B, 1, D), F32)],
        input_output_aliases={9: 1, 10: 2},
        compiler_params=_cparams(2),
    )(x, dres, dy, dh0, gu, mod3, g, w_in, w_out, a_full, dgu_full)


def matmul_tn(xm, ym, bm, name):
    N, K = xm.shape
    GY, _, MY = ym.shape
    per = MY // bm
    nb = GY * per
    fixed = K * bm * (4 + 2 * 2)
    tn = _tile(N, max(512, (DW_VMEM_BUDGET - fixed) // (2 * 2 * (K + bm))), 256)

    def body(x_ref, y_ref, o_ref, acc_s):
        n = pl.program_id(1)

        @pl.when(n == 0)
        def _():
            acc_s[...] = jnp.zeros_like(acc_s)

        acc_s[...] += _mm_tn(x_ref[...], y_ref[...])

        @pl.when(n == N // tn - 1)
        def _():
            o_ref[...] = acc_s[...].astype(BF16)

    return pl.pallas_call(
        body, name=name, grid=(nb, N // tn),
        in_specs=[pl.BlockSpec((tn, K), lambda m, n: (n, 0)),
                  pl.BlockSpec((None, tn, bm), lambda m, n: (m // per, n, m % per))],
        out_specs=pl.BlockSpec((None, K, bm), lambda m, n: (m, 0, 0)),
        out_shape=jax.ShapeDtypeStruct((nb, K, bm), BF16),
        scratch_shapes=[pltpu.VMEM((K, bm), F32)],
        compiler_params=_cparams(2),
    )(xm, ym)


def final_loss(x, fg, target):
    B, T, D = x.shape
    tm = _tile(T, TOKENS_PER_STEP)

    def body(x_ref, g_ref, t_ref, dx_ref, dfg_ref, loss_ref):
        t = pl.program_id(1)

        @pl.when(t == 0)
        def _():
            dfg_ref[...] = jnp.zeros_like(dfg_ref)
            loss_ref[...] = jnp.zeros_like(loss_ref)

        xv = x_ref[...]
        g = g_ref[...]
        r = lax.rsqrt(jnp.mean(xv * xv, axis=-1, keepdims=True) + EPS)
        xn = xv * r
        err = xn * g - t_ref[...]
        tok_loss = jnp.mean(err * err, axis=-1, keepdims=True)
        loss_ref[...] += 0.5 * jnp.sum(tok_loss, axis=0, keepdims=True)
        dy = err * (1.0 / D)
        dfg_ref[...] += _sum0(dy * xn)
        dxn = dy * g
        dx_ref[...] = r * (dxn - xn * jnp.mean(dxn * xn, axis=-1, keepdims=True))

    tok = pl.BlockSpec((None, tm, D), lambda b, t: (b, t, 0))
    return pl.pallas_call(
        body, name="final_loss", grid=(B, T // tm),
        in_specs=[tok, pl.BlockSpec((1, D), lambda b, t: (0, 0)), tok],
        out_specs=[tok, pl.BlockSpec((None, 1, D), lambda b, t: (b, 0, 0)),
                   pl.BlockSpec((None, 1, LANES), lambda b, t: (b, 0, 0))],
        out_shape=[jax.ShapeDtypeStruct((B, T, D), F32), jax.ShapeDtypeStruct((B, 1, D), F32),
                   jax.ShapeDtypeStruct((B, 1, LANES), F32)],
        compiler_params=_cparams(2),
    )(x, fg, target)


def _past_halo_spec(tm, halo, width):
    return pl.BlockSpec((None, halo, width), lambda b, t: (b, jnp.maximum(t * (tm // halo) - 1, 0), 0))


def _future_halo_spec(tm, halo, width, T):
    return pl.BlockSpec((None, halo, width), lambda b, t: (b, jnp.minimum((t + 1) * (tm // halo), T // halo - 1), 0))


def _fill_shifted(ext_s):
    n = ext_s.shape[1]
    for b in range(1, SUBLANES):
        ext_s[b, 0:n - SUBLANES, :] = ext_s[0, pl.ds(b, n - SUBLANES), :]


def _shifted(ext_s, offset, rows):
    a, b = divmod(offset, SUBLANES)
    return ext_s[b, pl.ds(SUBLANES * a, rows), :]


def _glu_fwd(h, w_ref, bias):
    D = h.shape[1]
    a = jnp.concatenate([_mm(h, w_ref[0]), _mm(h, w_ref[1])], axis=1) + bias[:, :D]
    b = jnp.concatenate([_mm(h, w_ref[2]), _mm(h, w_ref[3])], axis=1) + bias[:, D:]
    return a, b


def conv_glu_fwd(x, mod3, g, w_glu, b_glu):
    B, T, D = x.shape
    tm = _tile(T, TOKENS_PER_STEP)

    def body(x_ref, mod_ref, g_ref, w_ref, b_ref, u_ref):
        h = _modnorm(x_ref[...], g_ref[...], mod_ref[1:2, :], mod_ref[0:1, :]).astype(BF16)
        a, b = _glu_fwd(h, w_ref, b_ref[...])
        u_ref[...] = a * _sigmoid(b)

    tok = pl.BlockSpec((None, tm, D), lambda b, t: (b, t, 0))
    return pl.pallas_call(
        body, name="conv_glu_fwd", grid=(B, T // tm),
        in_specs=[tok, pl.BlockSpec((None, 3, D), lambda b, t: (b, 0, 0)),
                  pl.BlockSpec((1, D), lambda b, t: (0, 0)),
                  pl.BlockSpec((4, D, D // 2), lambda b, t: (0, 0, 0)),
                  pl.BlockSpec((1, 2 * D), lambda b, t: (0, 0))],
        out_specs=tok, out_shape=jax.ShapeDtypeStruct((B, T, D), F32),
        compiler_params=_cparams(2),
    )(x, mod3, g, w_glu, b_glu)


def _layer_norm_parts(u2):
    mu = jnp.mean(u2, axis=-1, keepdims=True)
    xc = u2 - mu
    rs = lax.rsqrt(jnp.mean(xc * xc, axis=-1, keepdims=True) + EPS)
    return xc * rs, rs


def conv_out_fwd(x, u, mod3, w_dw, b_dw, ln_g, ln_b, w_pw, b_pw):
    B, T, D = x.shape
    K = w_dw.shape[0] - 1
    tm = _tile(T, TOKENS_PER_STEP)

    def body(x_ref, u_ref, halo_ref, mod_ref, wdw_ref, bdw_ref, lg_ref, lb_ref, wpw_ref, bpw_ref,
             xo_ref, y_ref, u2_ref, ext_s):
        t = pl.program_id(1)
        ext_s[0, 0:CONV_HALO, :] = jnp.where(t > 0, halo_ref[...], 0.0)
        ext_s[0, CONV_HALO:, :] = u_ref[...]
        _fill_shifted(ext_s)
        acc = jnp.broadcast_to(bdw_ref[...], (tm, D))
        for k in range(K):
            acc = acc + wdw_ref[k:k + 1, :] * _shifted(ext_s, CONV_HALO - (K - 1) + k, tm)
        u2_ref[...] = acc
        xh, _ = _layer_norm_parts(acc)
        l = xh * lg_ref[...] + lb_ref[...]
        u3 = l * _sigmoid(l)
        y = _mm(u3, wpw_ref[...]) + bpw_ref[...]
        y_ref[...] = y
        xo_ref[...] = x_ref[...] + (1.0 + mod_ref[2:3, :]) * y

    tok = pl.BlockSpec((None, tm, D), lambda b, t: (b, t, 0))
    vec = pl.BlockSpec((1, D), lambda b, t: (0, 0))
    return pl.pallas_call(
        body, name="conv_out_fwd", grid=(B, T // tm),
        in_specs=[tok, tok, _past_halo_spec(tm, CONV_HALO, D), pl.BlockSpec((None, 3, D), lambda b, t: (b, 0, 0)),
                  pl.BlockSpec((K + 1, D), lambda b, t: (0, 0)), vec, vec, vec,
                  pl.BlockSpec((D, D), lambda b, t: (0, 0)), vec],
        out_specs=[tok, tok, tok], out_shape=[jax.ShapeDtypeStruct((B, T, D), F32)] * 3,
        scratch_shapes=[pltpu.VMEM((SUBLANES, tm + CONV_HALO, D), F32)],
        compiler_params=_cparams(2),
    )(x, u, u, mod3, w_dw, b_dw, ln_g, ln_b, w_pw, b_pw)


def conv_out_bwd(dres, y, u2, mod3, ln_g, ln_b, w_pw):
    B, T, D = dres.shape
    tm = _tile(T, TOKENS_PER_STEP)

    def body(dres_ref, y_ref, u2_ref, mod_ref, lg_ref, lb_ref, wpw_ref, du2_ref, u3_ref, dy_ref, dgate_ref, vec_ref):
        t = pl.program_id(1)

        @pl.when(t == 0)
        def _():
            dgate_ref[...] = jnp.zeros_like(dgate_ref)
            vec_ref[...] = jnp.zeros_like(vec_ref)

        dres = dres_ref[...]
        dy = (1.0 + mod_ref[2:3, :]) * dres
        dy_ref[...] = dy.astype(BF16)
        dgate_ref[...] += _sum0(dres * y_ref[...])
        xh, rs = _layer_norm_parts(u2_ref[...])
        lg = lg_ref[...]
        l = xh * lg + lb_ref[...]
        sg = _sigmoid(l)
        u3_ref[...] = (l * sg).astype(BF16)
        du3 = _mm_nt(dy, wpw_ref[...])
        dl = du3 * _dsilu(l, sg)
        dxh = dl * lg
        du2 = rs * (dxh - jnp.mean(dxh, axis=-1, keepdims=True) - xh * jnp.mean(dxh * xh, axis=-1, keepdims=True))
        du2_ref[...] = du2
        vec_ref[0:1, :] += _sum0(dy)
        vec_ref[1:2, :] += _sum0(dl * xh)
        vec_ref[2:3, :] += _sum0(dl)
        vec_ref[3:4, :] += _sum0(du2)

    tok = pl.BlockSpec((None, tm, D), lambda b, t: (b, t, 0))
    tokb = pl.BlockSpec((None, tm, D), lambda b, t: (b, t, 0))
    vec = pl.BlockSpec((1, D), lambda b, t: (0, 0))
    return pl.pallas_call(
        body, name="conv_out_bwd", grid=(B, T // tm),
        in_specs=[tok, tok, tok, pl.BlockSpec((None, 3, D), lambda b, t: (b, 0, 0)), vec, vec,
                  pl.BlockSpec((D, D), lambda b, t: (0, 0))],
        out_specs=[tok, tokb, tokb, pl.BlockSpec((None, 1, D), lambda b, t: (b, 0, 0)),
                   pl.BlockSpec((None, 4, D), lambda b, t: (b, 0, 0))],
        out_shape=[jax.ShapeDtypeStruct((B, T, D), F32), jax.ShapeDtypeStruct((B, T, D), BF16),
                   jax.ShapeDtypeStruct((B, T, D), BF16), jax.ShapeDtypeStruct((B, 1, D), F32),
                   jax.ShapeDtypeStruct((B, 4, D), F32)],
        compiler_params=_cparams(2),
    )(dres, y, u2, mod3, ln_g, ln_b, w_pw)


def conv_glu_bwd(x, dres, du2, u, mod3, g, w_glu, b_glu, w_dw):
    B, T, D = x.shape
    K = w_dw.shape[0] - 1
    tm = _tile(T, TOKENS_PER_STEP_WIDE)
    nt = T // tm

    def body(x_ref, dres_ref, du2_ref, du2h_ref, u_ref, uh_ref, mod_ref, g_ref, w_ref, b_ref, wdw_ref,
             dx_ref, h_ref, dab_ref, dwdw_ref, dbglu_ref, dmod_ref, dg_ref, extu_s, extd_s):
        t = pl.program_id(1)

        @pl.when(t == 0)
        def _():
            dwdw_ref[...] = jnp.zeros_like(dwdw_ref)
            dbglu_ref[...] = jnp.zeros_like(dbglu_ref)
            dmod_ref[...] = jnp.zeros_like(dmod_ref)
            dg_ref[...] = jnp.zeros_like(dg_ref)

        du2 = du2_ref[...]
        extu_s[0, 0:CONV_HALO, :] = jnp.where(t > 0, uh_ref[...], 0.0)
        extu_s[0, CONV_HALO:, :] = u_ref[...]
        extd_s[0, 0:tm, :] = du2
        extd_s[0, tm:, :] = jnp.where(t < nt - 1, du2h_ref[...], 0.0)
        _fill_shifted(extu_s)
        _fill_shifted(extd_s)
        du = jnp.zeros((tm, D), F32)
        for k in range(K):
            du = du + wdw_ref[k:k + 1, :] * _shifted(extd_s, K - 1 - k, tm)
            dwdw_ref[k:k + 1, :] += _sum0(du2 * _shifted(extu_s, CONV_HALO - (K - 1) + k, tm))
        xv = x_ref[...]
        h = _modnorm(xv, g_ref[...], mod_ref[1:2, :], mod_ref[0:1, :]).astype(BF16)
        h_ref[...] = h
        a, b = _glu_fwd(h, w_ref, b_ref[...])
        sb = _sigmoid(b)
        da = du * sb
        db = du * a * sb * (1.0 - sb)
        dbglu_ref[:, 0:D] += _sum0(da)
        dbglu_ref[:, D:] += _sum0(db)
        da = da.astype(BF16)
        db = db.astype(BF16)
        dab_ref[:, 0:D] = da
        dab_ref[:, D:] = db
        Dh2 = D // 2
        dh = (_mm_nt(da[:, :Dh2], w_ref[0]) + _mm_nt(da[:, Dh2:], w_ref[1])
              + _mm_nt(db[:, :Dh2], w_ref[2]) + _mm_nt(db[:, Dh2:], w_ref[3]))
        dxn, dg, dscale, dshift = _modnorm_bwd(xv, g_ref[...], mod_ref[1:2, :], dh)
        dx_ref[...] = dres_ref[...] + dxn
        dmod_ref[0:1, :] += dshift
        dmod_ref[1:2, :] += dscale
        dg_ref[...] += dg

    tok = pl.BlockSpec((None, tm, D), lambda b, t: (b, t, 0))
    return pl.pallas_call(
        body, name="conv_glu_bwd", grid=(B, nt),
        in_specs=[tok, tok, tok, _future_halo_spec(tm, CONV_HALO, D, T), tok, _past_halo_spec(tm, CONV_HALO, D),
                  pl.BlockSpec((None, 3, D), lambda b, t: (b, 0, 0)), pl.BlockSpec((1, D), lambda b, t: (0, 0)),
                  pl.BlockSpec((4, D, D // 2), lambda b, t: (0, 0, 0)), pl.BlockSpec((1, 2 * D), lambda b, t: (0, 0)),
                  pl.BlockSpec((K + 1, D), lambda b, t: (0, 0))],
        out_specs=[tok, tok, pl.BlockSpec((None, tm, 2 * D), lambda b, t: (b, t, 0)),
                   pl.BlockSpec((None, K + 1, D), lambda b, t: (b, 0, 0)),
                   pl.BlockSpec((None, 1, 2 * D), lambda b, t: (b, 0, 0)),
                   pl.BlockSpec((None, 3, D), lambda b, t: (b, 0, 0)),
                   pl.BlockSpec((None, 1, D), lambda b, t: (b, 0, 0))],
        out_shape=[jax.ShapeDtypeStruct((B, T, D), F32), jax.ShapeDtypeStruct((B, T, D), BF16),
                   jax.ShapeDtypeStruct((B, T, 2 * D), BF16), jax.ShapeDtypeStruct((B, K + 1, D), F32),
                   jax.ShapeDtypeStruct((B, 1, 2 * D), F32), jax.ShapeDtypeStruct((B, 3, D), F32),
                   jax.ShapeDtypeStruct((B, 1, D), F32)],
        scratch_shapes=[pltpu.VMEM((SUBLANES, tm + CONV_HALO, D), F32)] * 2,
        compiler_params=_cparams(2),
    )(x, dres, du2, du2, u, u, mod3, g, w_glu, b_glu, w_dw)


def dn_proj_fwd(x, mod3, g, w_main, w_ab):
    B, T, D = x.shape
    W = w_main.shape[1] // 4
    tm = _tile(T, TOKENS_PER_STEP)

    def body(x_ref, mod_ref, g_ref, wm_ref, wab_ref, pre_ref, z_ref, ab_ref):
        h = _modnorm(x_ref[...], g_ref[...], mod_ref[1:2, :], mod_ref[0:1, :]).astype(BF16)
        for p in range(3):
            pre_ref[:, p * W:(p + 1) * W] = _mm(h, wm_ref[:, p * W:(p + 1) * W])
        z_ref[...] = _mm(h, wm_ref[:, 3 * W:])
        ab_ref[...] = _mm(h, wab_ref[...])

    return pl.pallas_call(
        body, name="dn_proj_fwd", grid=(B, T // tm),
        in_specs=[pl.BlockSpec((None, tm, D), lambda b, t: (b, t, 0)), pl.BlockSpec((None, 3, D), lambda b, t: (b, 0, 0)),
                  pl.BlockSpec((1, D), lambda b, t: (0, 0)), pl.BlockSpec((D, 4 * W), lambda b, t: (0, 0)),
                  pl.BlockSpec((D, LANES), lambda b, t: (0, 0))],
        out_specs=[pl.BlockSpec((None, tm, 3 * W), lambda b, t: (b, t, 0)),
                   pl.BlockSpec((None, tm, W), lambda b, t: (b, t, 0)),
                   pl.BlockSpec((None, tm, LANES), lambda b, t: (b, t, 0))],
        out_shape=[jax.ShapeDtypeStruct((B, T, 3 * W), F32), jax.ShapeDtypeStruct((B, T, W), F32),
                   jax.ShapeDtypeStruct((B, T, LANES), F32)],
        compiler_params=_cparams(2),
    )(x, mod3, g, w_main, w_ab)


def _sconv(ext_s, w_ref, tm, K):
    acc = w_ref[0:1, :] * ext_s[pl.ds(SCONV_HALO - (K - 1), tm), :]
    for k in range(1, K):
        acc = acc + w_ref[k:k + 1, :] * ext_s[pl.ds(SCONV_HALO - (K - 1) + k, tm), :]
    return acc


def _lane_col(val, lane, idx):
    return jnp.sum(jnp.where(lane == idx, val, 0.0), axis=1, keepdims=True)


def dn_conv_fwd(pre, ab, w_sconv, alog_row, dt_row, H):
    B, T, W3 = pre.shape
    W = W3 // 3
    Dh = W // H
    K = w_sconv.shape[0]
    tm = _tile(T, TOKENS_PER_STEP)

    def body(pre_ref, halo_ref, ab_ref, w_ref, alog_ref, dt_ref, q_ref, k_ref, v_ref, gb_ref, bb_ref, ext_s):
        t = pl.program_id(1)
        ext_s[0:SCONV_HALO, :] = jnp.where(t > 0, halo_ref[...], 0.0)
        ext_s[SCONV_HALO:, :] = pre_ref[...]
        cv = _sconv(ext_s, w_ref, tm, K)
        qkv = cv * _sigmoid(cv)
        ab = ab_ref[...]
        lane = lax.broadcasted_iota(jnp.int32, ab.shape, 1)
        g_all = -jnp.exp(alog_ref[...]) * _softplus(ab + dt_ref[...])
        beta_all = _sigmoid(ab)
        for h in range(H):
            q_ref[h] = qkv[:, h * Dh:(h + 1) * Dh]
            k_ref[h] = qkv[:, W + h * Dh:W + (h + 1) * Dh]
            v_ref[h] = qkv[:, 2 * W + h * Dh:2 * W + (h + 1) * Dh]
            gb_ref[h] = jnp.broadcast_to(_lane_col(g_all, lane, h), (tm, Dh))
            bb_ref[h] = jnp.broadcast_to(_lane_col(beta_all, lane, H + h), (tm, Dh))

    hm = pl.BlockSpec((None, H, tm, Dh), lambda b, t: (b, 0, t, 0))
    row = pl.BlockSpec((1, LANES), lambda b, t: (0, 0))
    return pl.pallas_call(
        body, name="dn_conv_fwd", grid=(B, T // tm),
        in_specs=[pl.BlockSpec((None, tm, W3), lambda b, t: (b, t, 0)), _past_halo_spec(tm, SCONV_HALO, W3),
                  pl.BlockSpec((None, tm, LANES), lambda b, t: (b, t, 0)),
                  pl.BlockSpec((K, W3), lambda b, t: (0, 0)), row, row],
        out_specs=[hm] * 5, out_shape=[jax.ShapeDtypeStruct((B, H, T, Dh), F32)] * 5,
        scratch_shapes=[pltpu.VMEM((tm + SCONV_HALO, W3), F32)],
        compiler_params=_cparams(2),
    )(pre, pre, ab, w_sconv, alog_row, dt_row)


def _bdot(spec):
    return lambda a, b: jnp.einsum(spec, a.astype(BF16), b.astype(BF16), preferred_element_type=F32)


_NN, _NT, _TN = "gij,gjk->gik", "gik,gjk->gij", "gki,gkj->gij"


def _make_bdots():
    nn_, nt_, tn_ = _bdot(_NN), _bdot(_NT), _bdot(_TN)

    @jax.custom_vjp
    def nn(a, b):
        return nn_(a, b)

    @jax.custom_vjp
    def nt(a, b):
        return nt_(a, b)

    @jax.custom_vjp
    def tn(a, b):
        return tn_(a, b)

    nn.defvjp(lambda a, b: (nn_(a, b), (a, b)), lambda r, d: (nt_(d, r[1]), tn_(r[0], d)))
    nt.defvjp(lambda a, b: (nt_(a, b), (a, b)), lambda r, d: (nn_(d, r[1]), tn_(d, r[0])))
    tn.defvjp(lambda a, b: (tn_(a, b), (a, b)), lambda r, d: (nt_(r[1], d), nn_(r[0], d)))
    return nn, nt, tn


def _unit_lower_inverse(A, known=None):
    hdot = functools.partial(jnp.einsum, precision=lax.Precision.HIGH, preferred_element_type=F32)
    C = A.shape[-1]

    def impl(A):
        eye = (lax.broadcasted_iota(jnp.int32, A.shape, 1) == lax.broadcasted_iota(jnp.int32, A.shape, 2)).astype(F32)
        Tm = eye - A
        Ap = A
        for _ in range(max(1, (C - 1).bit_length()) - 1):
            Ap = hdot(_NN, Ap, Ap)
            Tm = Tm + hdot(_NN, Tm, Ap)
        return Tm

    @jax.custom_vjp
    def inv(A, given):
        return impl(A) if known is None else given

    def fwd(A, given):
        Tm = impl(A) if known is None else given
        return Tm, Tm

    def bwd(Tm, dT):
        return -hdot(_NT, hdot(_TN, Tm, dT), Tm), jnp.zeros_like(Tm)

    inv.defvjp(fwd, bwd)
    return inv(A, A if known is None else known)


def _chunk_fn(q, k, v, gb, bb, S, inverse=None, with_inverse=False):
    nn, nt, tn = _make_bdots()
    G, C, Dh = q.shape
    hdot = functools.partial(jnp.einsum, precision=lax.Precision.HIGH, preferred_element_type=F32)
    q = q * lax.rsqrt(jnp.sum(q * q, axis=-1, keepdims=True) + EPS) * (Dh ** -0.5)
    k = k * lax.rsqrt(jnp.sum(k * k, axis=-1, keepdims=True) + EPS)
    row = lax.broadcasted_iota(jnp.int32, (G, C, C), 1)
    col = lax.broadcasted_iota(jnp.int32, (G, C, C), 2)
    causal = row >= col
    strict = row > col
    gc = hdot(_NN, causal.astype(F32), gb)
    spread = jnp.full((G, C, Dh), 1.0 / Dh, F32)
    gi = hdot(_NT, gc, spread)
    gj = hdot(_NT, spread, gc)
    decay = jnp.where(causal, jnp.exp(jnp.where(causal, gi - gj, 0.0)), 0.0)
    kb = k * bb
    vb = v * bb
    A = jnp.where(strict, nt(kb, k) * decay, 0.0)
    Tm = _unit_lower_inverse(A, inverse)
    eg = jnp.exp(gc)
    u = nn(Tm, vb)
    w = nn(Tm, kb * eg)
    qg = q * eg
    intra = nt(q, k) * decay
    glast = hdot(_NN, jnp.ones((G, C, C), F32), gb)
    kd = k * jnp.exp(glast - gc)
    v_new = u - nn(w, S)
    o = nn(qg, S) + nn(intra, v_new)
    egl = jnp.exp(glast)
    S_new = S * jnp.concatenate([egl] * (Dh // C), axis=1) + tn(kd, v_new)
    return (o, S_new, Tm) if with_inverse else (o, S_new)


def dn_chunk_fwd(q, k, v, gb, bb):
    B, H, T, Dh = q.shape
    NC = T // CHUNK
    NS = _tile(NC, CHUNKS_PER_STEP, 1)

    def body(q_ref, k_ref, v_ref, gb_ref, bb_ref, o_ref, sp_ref, inv_ref, S_s):
        @pl.when(pl.program_id(1) == 0)
        def _():
            S_s[...] = jnp.zeros_like(S_s)

        def one_chunk(j, carry):
            rows = pl.ds(pl.multiple_of(j * CHUNK, CHUNK), CHUNK)
            S = S_s[...]
            sp_ref[j] = S
            o, S_new, Tm = _chunk_fn(q_ref[:, rows, :], k_ref[:, rows, :], v_ref[:, rows, :], gb_ref[:, rows, :],
                                     bb_ref[:, rows, :], S, with_inverse=True)
            o_ref[:, rows, :] = o
            inv_ref[j] = Tm
            S_s[...] = S_new
            return carry

        lax.fori_loop(0, NS, one_chunk, 0)

    hm = pl.BlockSpec((None, H, NS * CHUNK, Dh), lambda b, n: (b, 0, n, 0))
    return pl.pallas_call(
        body, name="dn_chunk_fwd", grid=(B, NC // NS),
        in_specs=[hm] * 5,
        out_specs=[hm, pl.BlockSpec((None, NS, H, Dh, Dh), lambda b, n: (b, n, 0, 0, 0)),
                   pl.BlockSpec((None, NS, H, CHUNK, CHUNK), lambda b, n: (b, n, 0, 0, 0))],
        out_shape=[jax.ShapeDtypeStruct((B, H, T, Dh), F32), jax.ShapeDtypeStruct((B, NC, H, Dh, Dh), F32),
                   jax.ShapeDtypeStruct((B, NC, H, CHUNK, CHUNK), F32)],
        scratch_shapes=[pltpu.VMEM((H, Dh, Dh), F32)],
        compiler_params=_cparams(2),
    )(q, k, v, gb, bb)


def dn_chunk_bwd(q, k, v, gb, bb, s_prev, inv, do):
    B, H, T, Dh = q.shape
    NC = T // CHUNK
    NS = _tile(NC, CHUNKS_PER_STEP, 1)
    NG = NC // NS

    def body(q_ref, k_ref, v_ref, gb_ref, bb_ref, sp_ref, inv_ref, do_ref, dq_ref, dk_ref, dv_ref, dgb_ref, dbb_ref,
             dS_s):
        @pl.when(pl.program_id(1) == 0)
        def _():
            dS_s[...] = jnp.zeros_like(dS_s)

        def one_chunk(jj, carry):
            j = NS - 1 - jj
            rows = pl.ds(pl.multiple_of(j * CHUNK, CHUNK), CHUNK)
            _, vjp = jax.vjp(functools.partial(_chunk_fn, inverse=inv_ref[j]), q_ref[:, rows, :], k_ref[:, rows, :],
                             v_ref[:, rows, :], gb_ref[:, rows, :], bb_ref[:, rows, :], sp_ref[j])
            dq, dk, dv, dgb, dbb, dS = vjp((do_ref[:, rows, :], dS_s[...]))
            dq_ref[:, rows, :] = dq
            dk_ref[:, rows, :] = dk
            dv_ref[:, rows, :] = dv
            dgb_ref[:, rows, :] = dgb
            dbb_ref[:, rows, :] = dbb
            dS_s[...] = dS
            return carry

        lax.fori_loop(0, NS, one_chunk, 0)

    hm = pl.BlockSpec((None, H, NS * CHUNK, Dh), lambda b, n: (b, 0, NG - 1 - n, 0))
    return pl.pallas_call(
        body, name="dn_chunk_bwd", grid=(B, NG),
        in_specs=[hm] * 5 + [pl.BlockSpec((None, NS, H, Dh, Dh), lambda b, n: (b, NG - 1 - n, 0, 0, 0)),
                             pl.BlockSpec((None, NS, H, CHUNK, CHUNK), lambda b, n: (b, NG - 1 - n, 0, 0, 0)), hm],
        out_specs=[hm] * 5, out_shape=[jax.ShapeDtypeStruct((B, H, T, Dh), F32)] * 5,
        scratch_shapes=[pltpu.VMEM((H, Dh, Dh), F32)],
        compiler_params=_cparams(2),
    )(q, k, v, gb, bb, s_prev, inv, do)


def _head_norm(o, og):
    r = lax.rsqrt(jnp.mean(o * o, axis=-1, keepdims=True) + EPS)
    return o * r, r


def dn_out_fwd(x, o, z, mod3, o_g, w_out):
    B, T, D = x.shape
    _, H, _, Dh = o.shape
    W = H * Dh
    tm = _tile(T, TOKENS_PER_STEP)

    def body(x_ref, o_ref, z_ref, mod_ref, og_ref, w_ref, xo_ref, y_ref):
        parts = []
        for h in range(H):
            on, _ = _head_norm(o_ref[h], og_ref[...])
            zz = z_ref[:, h * Dh:(h + 1) * Dh]
            parts.append((on * og_ref[...] * (zz * _sigmoid(zz))).astype(BF16))
        y = _mm(jnp.concatenate(parts, axis=1), w_ref[...])
        y_ref[...] = y
        xo_ref[...] = x_ref[...] + (1.0 + mod_ref[2:3, :]) * y

    tok = pl.BlockSpec((None, tm, D), lambda b, t: (b, t, 0))
    return pl.pallas_call(
        body, name="dn_out_fwd", grid=(B, T // tm),
        in_specs=[tok, pl.BlockSpec((None, H, tm, Dh), lambda b, t: (b, 0, t, 0)),
                  pl.BlockSpec((None, tm, W), lambda b, t: (b, t, 0)), pl.BlockSpec((None, 3, D), lambda b, t: (b, 0, 0)),
                  pl.BlockSpec((1, Dh), lambda b, t: (0, 0)), pl.BlockSpec((W, D), lambda b, t: (0, 0))],
        out_specs=[tok, tok], out_shape=[jax.ShapeDtypeStruct((B, T, D), F32)] * 2,
        compiler_params=_cparams(2),
    )(x, o, z, mod3, o_g, w_out)


def dn_out_bwd(dres, y, o, z, mod3, o_g, w_out):
    B, T, D = dres.shape
    _, H, _, Dh = o.shape
    W = H * Dh
    tm = _tile(T, TOKENS_PER_STEP)

    def body(dres_ref, y_ref, o_ref, z_ref, mod_ref, og_ref, w_ref, do_ref, dz_ref, ogb_ref, dy_ref, dgate_ref, dog_ref):
        t = pl.program_id(1)

        @pl.when(t == 0)
        def _():
            dgate_ref[...] = jnp.zeros_like(dgate_ref)
            dog_ref[...] = jnp.zeros_like(dog_ref)

        dres = dres_ref[...]
        dy = ((1.0 + mod_ref[2:3, :]) * dres).astype(BF16)
        dy_ref[...] = dy
        dgate_ref[...] += _sum0(dres * y_ref[...])
        dog = _mm_nt(dy, w_ref[...])
        og = og_ref[...]
        for h in range(H):
            ov = o_ref[h]
            xn, r = _head_norm(ov, og)
            zz = z_ref[:, h * Dh:(h + 1) * Dh]
            sg = _sigmoid(zz)
            sz = zz * sg
            d = dog[:, h * Dh:(h + 1) * Dh]
            ogb_ref[:, h * Dh:(h + 1) * Dh] = (xn * og * sz).astype(BF16)
            dz_ref[:, h * Dh:(h + 1) * Dh] = d * (xn * og) * _dsilu(zz, sg)
            don = d * sz
            dog_ref[...] += _sum0(don * xn)
            dxn = don * og
            do_ref[h] = r * (dxn - xn * jnp.mean(dxn * xn, axis=-1, keepdims=True))

    tok = pl.BlockSpec((None, tm, D), lambda b, t: (b, t, 0))
    tokw = pl.BlockSpec((None, tm, W), lambda b, t: (b, t, 0))
    hm = pl.BlockSpec((None, H, tm, Dh), lambda b, t: (b, 0, t, 0))
    return pl.pallas_call(
        body, name="dn_out_bwd", grid=(B, T // tm),
        in_specs=[tok, tok, hm, tokw, pl.BlockSpec((None, 3, D), lambda b, t: (b, 0, 0)),
                  pl.BlockSpec((1, Dh), lambda b, t: (0, 0)), pl.BlockSpec((W, D), lambda b, t: (0, 0))],
        out_specs=[hm, tokw, tokw, tok, pl.BlockSpec((None, 1, D), lambda b, t: (b, 0, 0)),
                   pl.BlockSpec((None, 1, Dh), lambda b, t: (b, 0, 0))],
        out_shape=[jax.ShapeDtypeStruct((B, H, T, Dh), F32), jax.ShapeDtypeStruct((B, T, W), F32),
                   jax.ShapeDtypeStruct((B, T, W), BF16), jax.ShapeDtypeStruct((B, T, D), BF16),
                   jax.ShapeDtypeStruct((B, 1, D), F32), jax.ShapeDtypeStruct((B, 1, Dh), F32)],
        compiler_params=_cparams(2),
    )(dres, y, o, z, mod3, o_g, w_out)


def dn_conv_bwd(dq, dk, dv, dgb, dbb, pre, ab, w_sconv, alog_row, dt_row):
    B, H, T, Dh = dq.shape
    W = H * Dh
    W3 = 3 * W
    K = w_sconv.shape[0]
    tm = _tile(T, TOKENS_PER_STEP_WIDE)

    def body(dq_ref, dk_ref, dv_ref, dgb_ref, dbb_ref, pre_ref, halo_ref, ab_ref, w_ref, alog_ref, dt_ref,
             dc_ref, dab_ref, small_ref, ext_s):
        t = pl.program_id(1)

        @pl.when(t == 0)
        def _():
            small_ref[...] = jnp.zeros_like(small_ref)

        ext_s[0:SCONV_HALO, :] = jnp.where(t > 0, halo_ref[...], 0.0)
        ext_s[SCONV_HALO:, :] = pre_ref[...]
        cv = _sconv(ext_s, w_ref, tm, K)
        dsl = _dsilu(cv, _sigmoid(cv))
        ab = ab_ref[...]
        lane = lax.broadcasted_iota(jnp.int32, ab.shape, 1)
        dg_all = jnp.zeros_like(ab)
        db_all = jnp.zeros_like(ab)
        for h in range(H):
            dc_ref[:, h * Dh:(h + 1) * Dh] = dq_ref[h] * dsl[:, h * Dh:(h + 1) * Dh]
            dc_ref[:, W + h * Dh:W + (h + 1) * Dh] = dk_ref[h] * dsl[:, W + h * Dh:W + (h + 1) * Dh]
            dc_ref[:, 2 * W + h * Dh:2 * W + (h + 1) * Dh] = dv_ref[h] * dsl[:, 2 * W + h * Dh:2 * W + (h + 1) * Dh]
            dg_all = dg_all + jnp.where(lane == h, jnp.sum(dgb_ref[h], axis=1, keepdims=True), 0.0)
            db_all = db_all + jnp.where(lane == H + h, jnp.sum(dbb_ref[h], axis=1, keepdims=True), 0.0)
        xa = ab + dt_ref[...]
        ea = -jnp.exp(alog_ref[...])
        g_all = ea * _softplus(xa)
        da = dg_all * ea * _sigmoid(xa)
        beta = _sigmoid(ab)
        dab_ref[...] = da + db_all * beta * (1.0 - beta)
        small_ref[0:1, :] += _sum0(dg_all * g_all)
        small_ref[1:2, :] += _sum0(da)

    hm = pl.BlockSpec((None, H, tm, Dh), lambda b, t: (b, 0, t, 0))
    row = pl.BlockSpec((1, LANES), lambda b, t: (0, 0))
    return pl.pallas_call(
        body, name="dn_conv_bwd", grid=(B, T // tm),
        in_specs=[hm] * 5 + [pl.BlockSpec((None, tm, W3), lambda b, t: (b, t, 0)), _past_halo_spec(tm, SCONV_HALO, W3),
                             pl.BlockSpec((None, tm, LANES), lambda b, t: (b, t, 0)),
                             pl.BlockSpec((K, W3), lambda b, t: (0, 0)), row, row],
        out_specs=[pl.BlockSpec((None, tm, W3), lambda b, t: (b, t, 0)), pl.BlockSpec((None, tm, LANES), lambda b, t: (b, t, 0)),
                   pl.BlockSpec((None, 2, LANES), lambda b, t: (b, 0, 0))],
        out_shape=[jax.ShapeDtypeStruct((B, T, W3), F32), jax.ShapeDtypeStruct((B, T, LANES), F32),
                   jax.ShapeDtypeStruct((B, 2, LANES), F32)],
        scratch_shapes=[pltpu.VMEM((tm + SCONV_HALO, W3), F32)],
        compiler_params=_cparams(2),
    )(dq, dk, dv, dgb, dbb, pre, pre, ab, w_sconv, alog_row, dt_row)


def dn_proj_bwd(x, dres, dc, pre, dz, dab, mod3, g, w_main, w_ab, w_sconv):
    B, T, D = x.shape
    W3 = dc.shape[2]
    W = W3 // 3
    K = w_sconv.shape[0]
    tm = _tile(T, TOKENS_PER_STEP_WIDE)
    nt = T // tm

    def body(x_ref, dres_ref, dc_ref, dch_ref, pre_ref, preh_ref, dz_ref, dab_ref, mod_ref, g_ref, wm_ref, wab_ref, ws_ref,
             dx_ref, h_ref, dproj_ref, dws_ref, dmod_ref, dg_ref, extp_s, extd_s):
        t = pl.program_id(1)

        @pl.when(t == 0)
        def _():
            dws_ref[...] = jnp.zeros_like(dws_ref)
            dmod_ref[...] = jnp.zeros_like(dmod_ref)
            dg_ref[...] = jnp.zeros_like(dg_ref)

        dc = dc_ref[...]
        extp_s[0:SCONV_HALO, :] = jnp.where(t > 0, preh_ref[...], 0.0)
        extp_s[SCONV_HALO:, :] = pre_ref[...]
        extd_s[0:tm, :] = dc
        extd_s[tm:, :] = jnp.where(t < nt - 1, dch_ref[...], 0.0)
        dpre = jnp.zeros((tm, W3), F32)
        for k in range(K):
            dpre = dpre + ws_ref[k:k + 1, :] * extd_s[pl.ds(K - 1 - k, tm), :]
            dws_ref[k:k + 1, :] += _sum0(dc * extp_s[pl.ds(SCONV_HALO - (K - 1) + k, tm), :])
        dpre = dpre.astype(BF16)
        dzb = dz_ref[...].astype(BF16)
        dproj_ref[:, 0:W3] = dpre
        dproj_ref[:, W3:] = dzb
        dh = _mm_nt(dab_ref[...], wab_ref[...]) + _mm_nt(dzb, wm_ref[:, W3:])
        for p in range(3):
            dh = dh + _mm_nt(dpre[:, p * W:(p + 1) * W], wm_ref[:, p * W:(p + 1) * W])
        xv = x_ref[...]
        h_ref[...] = _modnorm(xv, g_ref[...], mod_ref[1:2, :], mod_ref[0:1, :]).astype(BF16)
        dxn, dg, dscale, dshift = _modnorm_bwd(xv, g_ref[...], mod_ref[1:2, :], dh)
        dx_ref[...] = dres_ref[...] + dxn
        dmod_ref[0:1, :] += dshift
        dmod_ref[1:2, :] += dscale
        dg_ref[...] += dg

    tok = pl.BlockSpec((None, tm, D), lambda b, t: (b, t, 0))
    tok3 = pl.BlockSpec((None, tm, W3), lambda b, t: (b, t, 0))
    return pl.pallas_call(
        body, name="dn_proj_bwd", grid=(B, nt),
        in_specs=[tok, tok, tok3, _future_halo_spec(tm, SCONV_HALO, W3, T), tok3, _past_halo_spec(tm, SCONV_HALO, W3),
                  pl.BlockSpec((None, tm, W), lambda b, t: (b, t, 0)), pl.BlockSpec((None, tm, LANES), lambda b, t: (b, t, 0)),
                  pl.BlockSpec((None, 3, D), lambda b, t: (b, 0, 0)), pl.BlockSpec((1, D), lambda b, t: (0, 0)),
                  pl.BlockSpec((D, 4 * W), lambda b, t: (0, 0)), pl.BlockSpec((D, LANES), lambda b, t: (0, 0)),
                  pl.BlockSpec((K, W3), lambda b, t: (0, 0))],
        out_specs=[tok, tok, pl.BlockSpec((None, tm, 4 * W), lambda b, t: (b, t, 0)),
                   pl.BlockSpec((None, K, W3), lambda b, t: (b, 0, 0)), pl.BlockSpec((None, 3, D), lambda b, t: (b, 0, 0)),
                   pl.BlockSpec((None, 1, D), lambda b, t: (b, 0, 0))],
        out_shape=[jax.ShapeDtypeStruct((B, T, D), F32), jax.ShapeDtypeStruct((B, T, D), BF16),
                   jax.ShapeDtypeStruct((B, T, 4 * W), BF16), jax.ShapeDtypeStruct((B, K, W3), F32),
                   jax.ShapeDtypeStruct((B, 3, D), F32), jax.ShapeDtypeStruct((B, 1, D), F32)],
        scratch_shapes=[pltpu.VMEM((tm + SCONV_HALO, W3), F32), pltpu.VMEM((tm + SCONV_HALO, W3), F32)],
        compiler_params=_cparams(2),
    )(x, dres, dc, dc, pre, pre, dz, dab, mod3, g, w_main, w_ab, w_sconv)


def ada_fwd(c_all, w_ada, b_cols):
    L, D, Ca = w_ada.shape
    NB = c_all.shape[0]

    def body(c_ref, w_ref, b_ref, o_ref):
        cv = c_ref[...]
        o_ref[...] = _mm(cv * _sigmoid(cv), w_ref[...]) + b_ref[...]

    return pl.pallas_call(
        body, name="ada_fwd", grid=(L,),
        in_specs=[pl.BlockSpec((NB, D), lambda i: (0, 0)), pl.BlockSpec((None, D, Ca), lambda i: (i, 0, 0)),
                  pl.BlockSpec((None, 1, Ca), lambda i: (i, 0, 0))],
        out_specs=pl.BlockSpec((None, NB, Ca), lambda i: (i, 0, 0)),
        out_shape=jax.ShapeDtypeStruct((L, NB, Ca), F32),
        compiler_params=_cparams(1),
    )(c_all, w_ada, b_cols)


def ada_bwd(c_all, dmod_cols, dmod_all):
    L, NB, Ca = dmod_cols.shape
    D = c_all.shape[1]
    C9 = dmod_all.shape[2]

    def body(c_ref, dc_ref, da_ref, gw_ref, gb_ref):
        cv = c_ref[...]
        gw_ref[...] = _mm_tn(cv * _sigmoid(cv), dc_ref[...])
        gb_ref[...] = _sum0(da_ref[...])

    return pl.pallas_call(
        body, name="ada_bwd", grid=(L,),
        in_specs=[pl.BlockSpec((NB, D), lambda i: (0, 0)), pl.BlockSpec((None, NB, Ca), lambda i: (i, 0, 0)),
                  pl.BlockSpec((None, NB, C9), lambda i: (i, 0, 0))],
        out_specs=[pl.BlockSpec((None, D, Ca), lambda i: (i, 0, 0)), pl.BlockSpec((None, 1, C9), lambda i: (i, 0, 0))],
        out_shape=[jax.ShapeDtypeStruct((L, D, Ca), F32), jax.ShapeDtypeStruct((L, 1, C9), F32)],
        compiler_params=_cparams(1),
    )(c_all, dmod_cols, dmod_all)


def adamw(w, g, m, v, name, token=None):
    R, C = w.shape
    tr = _tile(R, max(8, ELEMENTWISE_BLOCK // C))
    if token is None:
        token = jnp.zeros((8, LANES), F32)

    def body(w_ref, g_ref, m_ref, v_ref, t_ref, d_ref, mo_ref, vo_ref):
        gv = g_ref[...] + t_ref[0:1, 0:1]
        mn = ADAM_B1 * m_ref[...] + (1.0 - ADAM_B1) * gv
        vn = ADAM_B2 * v_ref[...] + (1.0 - ADAM_B2) * (gv * gv)
        m_hat = mn / (1.0 - ADAM_B1 ** ADAM_STEP)
        v_hat = vn / (1.0 - ADAM_B2 ** ADAM_STEP)
        d_ref[...] = -ADAM_LR * (m_hat / (jnp.sqrt(v_hat) + ADAM_EPS) + ADAM_WD * w_ref[...])
        mo_ref[...] = mn
        vo_ref[...] = vn

    blk = pl.BlockSpec((tr, C), lambda i: (i, 0))
    return pl.pallas_call(
        body, name=name, grid=(R // tr,), in_specs=[blk] * 4 + [pl.BlockSpec((8, LANES), lambda i: (0, 0))],
        out_specs=[blk] * 3, out_shape=[jax.ShapeDtypeStruct((R, C), F32)] * 3, compiler_params=_cparams(1),
    )(w, g, m, v, token)


def sum_devices(a):
    n, R, C = a.shape

    def body(a_ref, o_ref):
        s = a_ref[0]
        for d in range(1, n):
            s = s + a_ref[d]
        o_ref[...] = s

    return pl.pallas_call(
        body, name="sum_devices", out_shape=jax.ShapeDtypeStruct((R, C), F32),
        compiler_params=pltpu.CompilerParams(vmem_limit_bytes=VMEM_LIMIT_V7X),
    )(a)


def _place():
    x, y, c = lax.axis_index("x"), lax.axis_index("y"), lax.axis_index("c")
    return x, y, c


def _other_chips(x, y):
    return [(2 * (1 - x) + y, 1 - x, y), (2 * x + (1 - y), x, 1 - y), (2 * (1 - x) + (1 - y), 1 - x, 1 - y)]


def allgather8(block):
    m_per, n = block.shape

    def body(x_ref, out_ref, send_sems, recv_sems, local_sem):
        x, y, c = _place()
        me, sibling = (x, y, c), (x, y, 1 - c)
        chips = [(1 - x, y), (x, 1 - y), (1 - x, 1 - y)]

        def rows(px, py, pc):
            return out_ref.at[pl.ds((4 * px + 2 * py + pc) * m_per, m_per), :]

        def copy(k, blk, to, src=None):
            return pltpu.make_async_remote_copy(
                src_ref=rows(*blk) if src is None else src, dst_ref=rows(*blk),
                send_sem=send_sems.at[k], recv_sem=recv_sems.at[k], device_id=to, device_id_type=MESH)

        mine = pltpu.make_async_copy(x_ref, rows(*me), local_sem)
        mine.start()
        first = [copy(0, me, sibling, src=x_ref)]
        first += [copy(1 + j, me, (*chip, c), src=x_ref) for j, chip in enumerate(chips)]
        for cp in first:
            cp.start()
        passed = [copy(4 + j, (*chip, c), sibling) for j, chip in enumerate(chips)]
        for j, chip in enumerate(chips):
            copy(1 + j, (*chip, c), me).wait_recv()
            passed[j].start()
        copy(0, sibling, me).wait_recv()
        for j, chip in enumerate(chips):
            copy(4 + j, (*chip, 1 - c), me).wait_recv()
        for cp in first + passed:
            cp.wait_send()
        mine.wait()

    return pl.pallas_call(
        body, name="allgather8", out_shape=jax.ShapeDtypeStruct((N_DEV * m_per, n), block.dtype),
        in_specs=[pl.BlockSpec(memory_space=pltpu.VMEM)], out_specs=pl.BlockSpec(memory_space=pltpu.VMEM),
        scratch_shapes=[pltpu.SemaphoreType.DMA((7,)), pltpu.SemaphoreType.DMA((7,)), pltpu.SemaphoreType.DMA],
        compiler_params=pltpu.CompilerParams(vmem_limit_bytes=VMEM_LIMIT_V7X),
    )(block)


def _half(ref, c, rh):
    return ref.at[pl.ds(pl.multiple_of(c * rh, 16), rh), :]


def pair_exchange(grads):
    K = len(grads)

    def body(*refs):
        ins, outs = refs[:K], refs[K:2 * K]
        send_sems, recv_sems = refs[2 * K:]
        x, y, c = _place()
        sibling = (x, y, 1 - c)
        copies = []
        for k in range(K):
            n, r, _ = ins[k].shape
            rh = r // 2
            cp = pltpu.make_async_remote_copy(
                src_ref=ins[k].at[:, pl.ds(pl.multiple_of((1 - c) * rh, 16), rh), :], dst_ref=outs[k],
                send_sem=send_sems.at[k], recv_sem=recv_sems.at[k], device_id=sibling, device_id_type=MESH)
            cp.start()
            copies.append(cp)
        for cp in copies:
            cp.wait_recv()
        for cp in copies:
            cp.wait_send()

    return pl.pallas_call(
        body, name="pair_exchange",
        out_shape=[jax.ShapeDtypeStruct((g.shape[0], g.shape[1] // 2, g.shape[2]), g.dtype) for g in grads],
        in_specs=[HBM_SPEC] * K, out_specs=[HBM_SPEC] * K,
        scratch_shapes=[pltpu.SemaphoreType.DMA((K,))] * 2,
    )(*grads)


def pair_add(grad, recv, c_idx):
    n, r, C = grad.shape
    rh = r // 2
    tr = _tile(rh, max(16, ELEMENTWISE_BLOCK // C), 16)
    grad = grad.reshape(n, 2, rh, C)

    def body(c_ref, g_ref, r_ref, o_ref):
        o_ref[...] = (g_ref[...].astype(F32) + r_ref[...].astype(F32)).astype(BF16)

    return pl.pallas_call(
        body, name="pair_add",
        grid_spec=pltpu.PrefetchScalarGridSpec(
            num_scalar_prefetch=1, grid=(n, rh // tr),
            in_specs=[pl.BlockSpec((None, None, tr, C), lambda d, i, c_ref: (d, c_ref[0], i, 0)),
                      pl.BlockSpec((None, tr, C), lambda d, i, c_ref: (d, i, 0))],
            out_specs=pl.BlockSpec((None, tr, C), lambda d, i, c_ref: (d, i, 0))),
        out_shape=jax.ShapeDtypeStruct((n, rh, C), BF16), compiler_params=_cparams(2),
    )(c_idx, grad, recv)


def chip_sum(parts, got, where, stack, slot):
    _, rh, C = parts.shape
    tr = _tile(rh, max(16, ELEMENTWISE_BLOCK // C), 16)
    nt = rh // tr

    def body(w_ref, p_ref, g_ref, stack_any, o_ref):
        s = p_ref[...].astype(F32)
        for r in range(3):
            s = s + g_ref[r].astype(F32)
        o_ref[...] = s

    return pl.pallas_call(
        body, name="chip_sum",
        grid_spec=pltpu.PrefetchScalarGridSpec(
            num_scalar_prefetch=1, grid=(nt,),
            in_specs=[pl.BlockSpec((None, tr, C), lambda i, w_ref: (w_ref[0], i, 0)),
                      pl.BlockSpec((3, tr, C), lambda i, w_ref: (0, i, 0)),
                      pl.BlockSpec(memory_space=pl.ANY)],
            out_specs=pl.BlockSpec((None, tr, C), lambda i, w_ref: (slot, w_ref[1] * nt + i, 0))),
        out_shape=jax.ShapeDtypeStruct(stack.shape, F32), input_output_aliases={3: 0},
        compiler_params=_cparams(1),
    )(where, parts, got, stack)


def pair_share(stacks, slots):
    K = len(stacks)
    jobs = [(k, s) for k in range(K) for s in slots[k]]

    def body(*refs):
        ins, outs = refs[:K], refs[K:2 * K]
        send_sems, recv_sems = refs[2 * K:]
        x, y, c = _place()
        sibling = (x, y, 1 - c)
        started = []
        for n, (k, s) in enumerate(jobs):
            rh = ins[k].shape[1] // 2
            cp = pltpu.make_async_remote_copy(
                src_ref=_half(ins[k].at[s], c, rh), dst_ref=_half(outs[k].at[s], c, rh), send_sem=send_sems.at[n],
                recv_sem=recv_sems.at[n], device_id=sibling, device_id_type=MESH)
            cp.start()
            started.append(cp)
        for n, (k, s) in enumerate(jobs):
            rh = ins[k].shape[1] // 2
            theirs = _half(outs[k].at[s], 1 - c, rh)
            pltpu.make_async_remote_copy(
                src_ref=theirs, dst_ref=theirs, send_sem=send_sems.at[n], recv_sem=recv_sems.at[n],
                device_id=sibling, device_id_type=MESH).wait_recv()
        for cp in started:
            cp.wait_send()

    return pl.pallas_call(
        body, name="pair_share",
        out_shape=[jax.ShapeDtypeStruct(s.shape, s.dtype) for s in stacks],
        in_specs=[HBM_SPEC] * K, out_specs=[HBM_SPEC] * K, input_output_aliases={k: k for k in range(K)},
        scratch_shapes=[pltpu.SemaphoreType.DMA((len(jobs),))] * 2,
    )(*stacks)


SEM_SPEC = pl.BlockSpec(memory_space=pltpu.SEMAPHORE)
ANY_SPEC = pl.BlockSpec(memory_space=pl.ANY)
DATAFLOW = pltpu.SideEffectType.DATAFLOW_SIDE_EFFECTING


def _in_hbm(a):
    return pltpu.with_memory_space_constraint(a, pltpu.HBM)


def _ici_copies(srcs, dsts, send_sems, recv_sems, src_slice, dst_slice):
    x, y, c = _place()
    out = []
    for k in range(len(srcs)):
        for r, (pchip, px, py) in enumerate(_other_chips(x, y)):
            out.append(pltpu.make_async_remote_copy(
                src_ref=src_slice(srcs[k], r, pchip), dst_ref=dst_slice(dsts[k], r, pchip),
                send_sem=send_sems.at[3 * k + r], recv_sem=recv_sems.at[3 * k + r], device_id=(px, py, c),
                device_id_type=MESH))
    return out


def _pair_copies(srcs, dsts, send_sems, recv_sems, src_slice, dst_slice):
    x, y, c = _place()
    return [pltpu.make_async_remote_copy(
        src_ref=src_slice(srcs[k]), dst_ref=dst_slice(dsts[k]), send_sem=send_sems.at[k], recv_sem=recv_sems.at[k],
        device_id=(x, y, 1 - c), device_id_type=MESH) for k in range(len(srcs))]


def _exchange_start(bufs, lands, src_slice, dst_slice, name, after=None, copies=_ici_copies, per=3):
    K = len(bufs)
    same = lands is None
    n_thru = K if same else 2 * K
    n_in = n_thru + (after is not None)

    def body(*refs):
        ins = refs[:n_thru]
        send_sems, recv_sems = refs[n_in], refs[n_in + 1]
        token = refs[-1]
        srcs = ins[:K]
        dsts = srcs if same else ins[K:]
        for cp in copies(srcs, dsts, send_sems, recv_sems, src_slice, dst_slice):
            cp.start()
        token[...] = jnp.zeros_like(token)

    thru = list(bufs) + ([] if same else list(lands))
    res = pl.pallas_call(
        body, name=name,
        out_shape=[pltpu.SemaphoreType.DMA((per * K,)), pltpu.SemaphoreType.DMA((per * K,))]
        + [pltpu.HBM(a.shape, a.dtype) for a in thru] + [jax.ShapeDtypeStruct((8, LANES), F32)],
        in_specs=[HBM_SPEC] * n_thru + [ANY_SPEC] * (after is not None),
        out_specs=[SEM_SPEC, SEM_SPEC] + [HBM_SPEC] * n_thru + [pl.BlockSpec(memory_space=pltpu.VMEM)],
        input_output_aliases={i: 2 + i for i in range(n_thru)},
        compiler_params=pltpu.CompilerParams(has_side_effects=DATAFLOW),
    )(*[_in_hbm(a) for a in thru], *([] if after is None else [after]))
    return res[0], res[1], res[2:2 + K], (res[2:2 + K] if same else res[2 + K:2 + 2 * K]), res[-1]


def _exchange_wait(send_sems, recv_sems, bufs, lands, after, src_slice, dst_slice, name, copies=_ici_copies):
    K = len(bufs)
    same = lands is None
    n_thru = K if same else 2 * K

    def body(*refs):
        ins = refs[:n_thru]
        ssem, rsem = refs[n_thru], refs[n_thru + 1]
        srcs = ins[:K]
        dsts = srcs if same else ins[K:]
        started = copies(srcs, dsts, ssem, rsem, src_slice, dst_slice)
        for cp in started:
            cp.wait_send()
        for cp in started:
            cp.wait_recv()

    thru = list(bufs) + ([] if same else list(lands))
    res = pl.pallas_call(
        body, name=name,
        out_shape=[pltpu.HBM(a.shape, a.dtype) for a in thru],
        in_specs=[HBM_SPEC] * n_thru + [SEM_SPEC, SEM_SPEC, ANY_SPEC],
        out_specs=[HBM_SPEC] * n_thru,
        input_output_aliases={i: i for i in range(n_thru)},
        compiler_params=pltpu.CompilerParams(has_side_effects=DATAFLOW),
    )(*thru, send_sems, recv_sems, after)
    return res[:K], (res[:K] if same else res[K:])


def _own_half(ref, r, pchip):
    x, y, c = _place()
    return _half(ref.at[2 * x + y], c, ref.shape[1] // 2)


def _their_half(ref, r, pchip):
    _, _, c = _place()
    return _half(ref.at[pchip], c, ref.shape[1] // 2)


def gather_start(lands, name, after=None):
    return _exchange_start(lands, None, _own_half, _own_half, name, after)


def gather_wait(handle, after, name):
    ssem, rsem, lands, _, _ = handle
    return _exchange_wait(ssem, rsem, lands, None, after, _own_half, _their_half, name)[1]


def pair_forward(lands):
    K = len(lands)

    def body(*refs):
        ins, outs = refs[:K], refs[K:2 * K]
        send_sems, recv_sems = refs[2 * K:]
        x, y, c = _place()
        sibling = (x, y, 1 - c)
        started = []
        for k in range(K):
            rh = ins[k].shape[1] // 2
            for r, (pchip, _, _) in enumerate(_other_chips(x, y)):
                cp = pltpu.make_async_remote_copy(
                    src_ref=_half(ins[k].at[pchip], c, rh), dst_ref=_half(outs[k].at[pchip], c, rh),
                    send_sem=send_sems.at[k, r], recv_sem=recv_sems.at[k, r], device_id=sibling, device_id_type=MESH)
                cp.start()
                started.append(cp)
        for k in range(K):
            rh = ins[k].shape[1] // 2
            for r, (pchip, _, _) in enumerate(_other_chips(x, y)):
                theirs = _half(outs[k].at[pchip], 1 - c, rh)
                pltpu.make_async_remote_copy(
                    src_ref=theirs, dst_ref=theirs, send_sem=send_sems.at[k, r], recv_sem=recv_sems.at[k, r],
                    device_id=sibling, device_id_type=MESH).wait_recv()
        for cp in started:
            cp.wait_send()

    return pl.pallas_call(
        body, name="pair_forward",
        out_shape=[jax.ShapeDtypeStruct(s.shape, s.dtype) for s in lands],
        in_specs=[HBM_SPEC] * K, out_specs=[HBM_SPEC] * K, input_output_aliases={k: k for k in range(K)},
        scratch_shapes=[pltpu.SemaphoreType.DMA((K, 3))] * 2,
    )(*lands)


def _to_chip(ref, r, pchip):
    return ref.at[pchip]


def _from_relation(ref, r, pchip):
    return ref.at[r]


def _other_rows(ref):
    _, _, c = _place()
    rh = ref.shape[1] // 2
    return ref.at[:, pl.ds(pl.multiple_of((1 - c) * rh, 16), rh), :]


def _whole(ref):
    return ref


def pair_start(grads, name):
    lands = [lax.empty((g.shape[0], g.shape[1] // 2, g.shape[2]), g.dtype) for g in grads]
    return _exchange_start(grads, lands, _other_rows, _whole, name, copies=_pair_copies, per=1)


def pair_finish(handle, after, name):
    ssem, rsem, grads, lands, _ = handle
    return _exchange_wait(ssem, rsem, grads, lands, after, _other_rows, _whole, name, copies=_pair_copies)


def reduce_start(grads, c_idx, name, after=None, recv=None):
    if recv is None:
        recv = pair_exchange(grads)
    parts = [pair_add(g, r, c_idx) for g, r in zip(grads, recv)]
    lands = [lax.empty((3,) + p.shape[1:], p.dtype) for p in parts]
    return _exchange_start(parts, lands, _to_chip, _from_relation, name, after)


def reduce_finish(handle, after, where, name, stacks, targets):
    ssem, rsem, parts, lands, _ = handle
    parts, got = _exchange_wait(ssem, rsem, parts, lands, after, _to_chip, _from_relation, name)
    stacks = dict(stacks)
    for p, g, (key, slot) in zip(parts, got, targets):
        stacks[key] = chip_sum(p, g, where, stacks[key], slot)
    keys = list(dict.fromkeys(key for key, _ in targets))
    shared = pair_share([stacks[k] for k in keys], [[s for key, s in targets if key == k] for k in keys])
    stacks.update(zip(keys, shared))
    return stacks


def _pack(arrs):
    flat = jnp.concatenate([a.reshape(-1).astype(F32) for a in arrs])
    pad = (-flat.shape[0]) % (8 * LANES)
    return jnp.pad(flat, (0, pad)).reshape(-1, LANES)


def _unpack(flat, shapes):
    out, off = [], 0
    for s in shapes:
        n = 1
        for d in s:
            n *= d
        out.append(flat[off:off + n].reshape(s))
        off += n
    return out


def _adamw_any(w, g, m, v, name, token=None):
    shp = w.shape
    C = shp[-1]
    d, nm, nv = adamw(w.reshape(-1, C), g.reshape(-1, C), m.reshape(-1, C), v.reshape(-1, C), name, token)
    return d.reshape(shp), nm.reshape(shp), nv.reshape(shp)


def kernel(x, c, norm_g, w_ada, b_ada, w_ffn_in, w_ffn_out, cm_w_glu, cm_b_glu, cm_w_dw, cm_b_dw, cm_ln_g, cm_ln_b, cm_w_pw, cm_b_pw, dn_w_in, dn_w_sconv, dn_a_log, dn_dt_bias, dn_o_g, dn_w_out, final_g, loss_target, m_norm_g, m_w_ada, m_b_ada, m_w_ffn_in, m_w_ffn_out, m_cm_w_glu, m_cm_b_glu, m_cm_w_dw, m_cm_b_dw, m_cm_ln_g, m_cm_ln_b, m_cm_w_pw, m_cm_b_pw, m_dn_w_in, m_dn_w_sconv, m_dn_a_log, m_dn_dt_bias, m_dn_o_g, m_dn_w_out, m_final_g, v_norm_g, v_w_ada, v_b_ada, v_w_ffn_in, v_w_ffn_out, v_cm_w_glu, v_cm_b_glu, v_cm_w_dw, v_cm_b_dw, v_cm_ln_g, v_cm_ln_b, v_cm_w_pw, v_cm_b_pw, v_dn_w_in, v_dn_w_sconv, v_dn_a_log, v_dn_dt_bias, v_dn_o_g, v_dn_w_out, v_final_g):
    weights = dict(norm_g=norm_g, w_ada=w_ada, b_ada=b_ada, w_ffn_in=w_ffn_in, w_ffn_out=w_ffn_out, cm_w_glu=cm_w_glu,
                   cm_b_glu=cm_b_glu, cm_w_dw=cm_w_dw, cm_b_dw=cm_b_dw, cm_ln_g=cm_ln_g, cm_ln_b=cm_ln_b, cm_w_pw=cm_w_pw,
                   cm_b_pw=cm_b_pw, dn_w_in=dn_w_in, dn_w_sconv=dn_w_sconv, dn_a_log=dn_a_log, dn_dt_bias=dn_dt_bias,
                   dn_o_g=dn_o_g, dn_w_out=dn_w_out, final_g=final_g)
    mom_m = dict(norm_g=m_norm_g, w_ada=m_w_ada, b_ada=m_b_ada, w_ffn_in=m_w_ffn_in, w_ffn_out=m_w_ffn_out,
                 cm_w_glu=m_cm_w_glu, cm_b_glu=m_cm_b_glu, cm_w_dw=m_cm_w_dw, cm_b_dw=m_cm_b_dw, cm_ln_g=m_cm_ln_g,
                 cm_ln_b=m_cm_ln_b, cm_w_pw=m_cm_w_pw, cm_b_pw=m_cm_b_pw, dn_w_in=m_dn_w_in, dn_w_sconv=m_dn_w_sconv,
                 dn_a_log=m_dn_a_log, dn_dt_bias=m_dn_dt_bias, dn_o_g=m_dn_o_g, dn_w_out=m_dn_w_out, final_g=m_final_g)
    mom_v = dict(norm_g=v_norm_g, w_ada=v_w_ada, b_ada=v_b_ada, w_ffn_in=v_w_ffn_in, w_ffn_out=v_w_ffn_out,
                 cm_w_glu=v_cm_w_glu, cm_b_glu=v_cm_b_glu, cm_w_dw=v_cm_w_dw, cm_b_dw=v_cm_b_dw, cm_ln_g=v_cm_ln_g,
                 cm_ln_b=v_cm_ln_b, cm_w_pw=v_cm_w_pw, cm_b_pw=v_cm_b_pw, dn_w_in=v_dn_w_in, dn_w_sconv=v_dn_w_sconv,
                 dn_a_log=v_dn_a_log, dn_dt_bias=v_dn_dt_bias, dn_o_g=v_dn_o_g, dn_w_out=v_dn_w_out, final_g=v_final_g)
    names = list(weights)

    BL, T, D = x.shape
    L = norm_g.shape[0]
    NB = BL * N_DEV
    Ca = w_ada.shape[2]
    C9 = b_ada.shape[1]
    H = dn_a_log.shape[1]
    Dh = dn_o_g.shape[1]
    W = H * Dh
    KC = cm_w_dw.shape[1]
    n_cm, n_dn = cm_w_glu.shape[0], dn_w_in.shape[0]
    ax, ay, ac = lax.axis_index("x"), lax.axis_index("y"), lax.axis_index("c")
    chip = 2 * ax + ay
    dev = 2 * chip + ac
    c_idx = ac.astype(jnp.int32).reshape(1)
    where = jnp.stack([chip, ac]).astype(jnp.int32)

    def landing(s, tok=None):
        s = s if tok is None else s + tok
        return lax.dynamic_update_slice(lax.empty((N_CHIPS,) + s.shape, BF16), s.astype(BF16)[None], (chip, 0, 0))

    def layer_shards(i):
        sh = [w_ffn_in[i, 0], w_ffn_in[i, 1], w_ffn_out[i, 0], w_ffn_out[i, 1]]
        if i % 2 == 0:
            sh += [cm_w_glu[i // 2], cm_w_pw[i // 2]]
        else:
            sh += [dn_w_in[i // 2], dn_w_out[i // 2]]
        return sh

    wts = [None] * L

    small_in = [c, norm_g, cm_w_dw, dn_w_sconv]
    gathered = allgather8(_pack(small_in)).reshape(N_DEV, -1)
    per_dev = [_unpack(gathered[d], [a.shape for a in small_in]) for d in range(N_DEV)]
    c_all = jnp.concatenate([p[0] for p in per_dev], axis=0)
    norm_g_full = jnp.concatenate([per_dev[2 * s][1] for s in range(N_CHIPS)], axis=-1)
    w_dw_full = jnp.concatenate([per_dev[2 * s][2] for s in range(N_CHIPS)], axis=-1)
    w_sconv_full = jnp.concatenate([per_dev[2 * s][3] for s in range(N_CHIPS)], axis=-1)

    b_cols = lax.dynamic_slice_in_dim(b_ada, chip * Ca, Ca, axis=1).reshape(L, 1, Ca)
    mod_part = ada_fwd(c_all, w_ada, b_cols)
    mod_g = allgather8(mod_part.reshape(-1, LANES))
    shards0 = layer_shards(0)
    first = gather_start([landing(shards0[0]), landing(shards0[2])], "gather_start_0a", mod_g)
    tok0 = first[4][0, 0]
    rest = gather_start([landing(shards0[k], tok0) for k in (1, 3, 4, 5)], "gather_start_0b", first[4])
    lands = [None] + [[landing(s, tok0) for s in layer_shards(i)] for i in range(1, L)]
    mod_g = mod_g.reshape(N_DEV, L, NB, Ca)
    mod_all = jnp.concatenate([mod_g[2 * s] for s in range(N_CHIPS)], axis=-1)
    mod = lax.dynamic_slice_in_dim(mod_all, dev * BL, BL, axis=1).reshape(L, BL, 9, D)

    def dn_weights(i):
        full = jnp.transpose(wts[i][4], (1, 0, 2)).reshape(D, -1)
        return full[:, :4 * W], jnp.pad(full[:, 4 * W:], ((0, 0), (0, LANES - 2 * H)))

    def row128(v):
        return jnp.pad(v.reshape(1, -1), ((0, 0), (0, LANES - v.shape[-1])))

    def pad_taps(w):
        return jnp.pad(w, ((0, 1), (0, 0)))

    saved = []
    xs = x
    after = mod
    for i in range(L):
        tok = 0.0
        if i == 0:
            wl = wts[0] = [None] * 6
            wl[0], wl[2] = pair_forward(gather_wait(first, after, "gather_wait_0a"))
        else:
            wl = wts[i] = pair_forward(gather_wait(handle, after, "gather_wait_%d" % i))
            if i + 1 < L:
                handle = gather_start(lands[i + 1], "gather_start_%d" % (i + 1), wl[0])
                tok = handle[4][0, 0]
        sv = {}
        m3 = [mod[i, :, 3 * j:3 * j + 3] + tok for j in range(3)]
        gs = [norm_g_full[i, j].reshape(1, D) for j in range(3)]
        sv["x0"] = xs
        xs, sv["y0"], sv["h0"], sv["gu0"] = ffn_fwd(xs, m3[0], gs[0], wl[0], wl[2])
        sv["x1"] = xs
        if i == 0:
            wl[1], wl[3], wl[4], wl[5] = pair_forward(gather_wait(rest, xs, "gather_wait_0b"))
            handle = gather_start(lands[1], "gather_start_1", wl[1])
            m3 = [m + handle[4][0, 0] for m in m3]
        if i % 2 == 0:
            a = i // 2
            sv["u"] = conv_glu_fwd(xs, m3[1], gs[1], wl[4], cm_b_glu[a].reshape(1, -1))
            xs, sv["y1"], sv["u2"] = conv_out_fwd(
                xs, sv["u"], m3[1], pad_taps(w_dw_full[a]), cm_b_dw[a].reshape(1, D), cm_ln_g[a].reshape(1, D),
                cm_ln_b[a].reshape(1, D), wl[5].reshape(D, D), cm_b_pw[a].reshape(1, D))
        else:
            a = i // 2
            w_main, w_ab = dn_weights(i)
            sv["pre"], sv["z"], sv["ab"] = dn_proj_fwd(xs, m3[1], gs[1], w_main, w_ab)
            qkvgb = dn_conv_fwd(sv["pre"], sv["ab"], w_sconv_full[a], row128(dn_a_log[a]), row128(dn_dt_bias[a]), H)
            sv["qkvgb"] = qkvgb
            sv["o"], sv["sp"], sv["inv"] = dn_chunk_fwd(*qkvgb)
            xs, sv["y1"] = dn_out_fwd(xs, sv["o"], sv["z"], m3[1], dn_o_g[a].reshape(1, Dh), wl[5].reshape(W, D))
        sv["x2"] = xs
        xs, sv["y2"], sv["h2"], sv["gu2"] = ffn_fwd(xs, m3[2], gs[2], wl[1], wl[3])
        saved.append(sv)
        after = xs

    dx, d_final_g, loss_part = final_loss(xs, final_g.reshape(1, D), loss_target)

    d_norm_g = [[None] * 3 for _ in range(L)]
    dmod = [[None] * 3 for _ in range(L)]
    g_cm = {k: [None] * n_cm for k in ("b_glu", "w_dw", "b_dw", "ln_g", "ln_b", "b_pw")}
    g_dn = {k: [None] * n_dn for k in ("w_sconv", "a_log", "dt_bias", "o_g")}
    big_names = ("w_ffn_in", "w_ffn_out", "cm_w_glu", "cm_w_pw", "dn_w_in", "dn_w_out")
    stacks = {n: lax.empty((weights[n].size // (weights[n].shape[-2] * weights[n].shape[-1]),) + weights[n].shape[-2:], F32)
              for n in big_names}

    def targets(i, which):
        mix = ("cm_w_glu", "cm_w_pw") if i % 2 == 0 else ("dn_w_in", "dn_w_out")
        full = [("w_ffn_in", 2 * i), ("w_ffn_in", 2 * i + 1), ("w_ffn_out", 2 * i), ("w_ffn_out", 2 * i + 1),
                (mix[0], i // 2), (mix[1], i // 2)]
        return [full[k] for k in which]

    def ffn_back(i, j, slot, dx, tok=0.0):
        wl, sv = wts[i], saved[i]
        m3 = mod[i, :, 3 * j:3 * j + 3] + tok
        g = norm_g_full[i, j].reshape(1, D)
        gu = sv["gu%d" % j]
        ab_, dgu, dyb, dh0, dgate = ffn_bwd_part(0, dx, gu, m3, wl[slot], wl[2 + slot], y=sv["y%d" % j])
        dx, ab_, dgu, dm, dg = ffn_bwd_part(1, dx, gu, m3, wl[slot], wl[2 + slot], first=(ab_, dgu, dyb, dh0),
                                            x=sv["x%d" % j], g=g)
        dm = dm.at[:, 2:3, :].set(dgate)
        hb = sv["h%d" % j]
        dmod[i][j] = dm
        d_norm_g[i][j] = jnp.sum(dg, axis=(0, 1))
        Fc = wl[slot].shape[2]
        dw_in = matmul_tn(hb.reshape(-1, D), dgu.reshape(2, BL * T, 2 * Fc), Fc, "dw_ffn_in")
        dw_out = matmul_tn(ab_.reshape(-1, 2 * Fc), dyb.reshape(1, -1, D), D, "dw_ffn_out")
        return dx, dw_in, dw_out.reshape(N_CHIPS, -1, D)

    pending, paired, tok = None, None, 0.0
    for i in reversed(range(L)):
        wl, sv = wts[i], saved[i]
        a = i // 2
        dx, dw_in1, dw_out1 = ffn_back(i, 2, 1, dx, tok)
        m3 = mod[i, :, 3:6]
        if paired is not None:
            theirs, recv = pair_finish(paired[0], dx, "pair_wait_%d" % paired[1])
            started = reduce_start(theirs, c_idx, "reduce_start_%d" % paired[1], recv=recv)
            pending, paired = (started, paired[1]), None
            m3 = m3 + started[4][0, 0]
        g = norm_g_full[i, 1].reshape(1, D)
        if i % 2 == 0:
            w_pw = wl[5].reshape(D, D)
            wdw = pad_taps(w_dw_full[a])
            du2, u3b, dyb, dgate, vec = conv_out_bwd(dx, sv["y1"], sv["u2"], m3, cm_ln_g[a].reshape(1, D),
                                                     cm_ln_b[a].reshape(1, D), w_pw)
            dx, hb, dab, dwdw, dbglu, dm, dg = conv_glu_bwd(sv["x1"], dx, du2, sv["u"], m3, g, wl[4],
                                                            cm_b_glu[a].reshape(1, -1), wdw)
            dm = dm.at[:, 2:3, :].set(dgate)
            vec = jnp.sum(vec, axis=0)
            g_cm["b_pw"][a], g_cm["ln_g"][a], g_cm["ln_b"][a], g_cm["b_dw"][a] = vec[0], vec[1], vec[2], vec[3]
            g_cm["w_dw"][a] = jnp.sum(dwdw, axis=0)[:KC]
            g_cm["b_glu"][a] = jnp.sum(dbglu, axis=(0, 1))
            dw_a = matmul_tn(hb.reshape(-1, D), dab.reshape(1, -1, 2 * D), D // 2, "dw_glu")
            dw_b = matmul_tn(u3b.reshape(-1, D), dyb.reshape(1, -1, D), D, "dw_sq").reshape(N_CHIPS, -1, D)
        else:
            w_main, w_ab = dn_weights(i)
            w_out = wl[5].reshape(W, D)
            do, dz, ogb, dyb, dgate, dog = dn_out_bwd(dx, sv["y1"], sv["o"], sv["z"], m3, dn_o_g[a].reshape(1, Dh), w_out)
            dq, dk, dv, dgb, dbb = dn_chunk_bwd(*sv["qkvgb"], sv["sp"], sv["inv"], do)
            dc, dab, small = dn_conv_bwd(dq, dk, dv, dgb, dbb, sv["pre"], sv["ab"], w_sconv_full[a],
                                         row128(dn_a_log[a]), row128(dn_dt_bias[a]))
            dx, hb, dproj, dws, dm, dg = dn_proj_bwd(sv["x1"], dx, dc, sv["pre"], dz, dab, m3, g, w_main, w_ab,
                                                     w_sconv_full[a])
            dm = dm.at[:, 2:3, :].set(dgate)
            small = jnp.sum(small, axis=0)
            g_dn["a_log"][a], g_dn["dt_bias"][a] = small[0, :H], small[1, :H]
            g_dn["o_g"][a] = jnp.sum(dog, axis=(0, 1))
            g_dn["w_sconv"][a] = jnp.sum(dws, axis=0)
            dw_main = matmul_tn(hb.reshape(-1, D), dproj.reshape(1, -1, 4 * W), W, "dw_dn_main")
            dw_ab = matmul_tn(hb.reshape(-1, D), dab.reshape(1, -1, LANES), LANES, "dw_dn_ab")
            full = jnp.concatenate([jnp.transpose(dw_main, (1, 0, 2)).reshape(D, 4 * W), dw_ab[0][:, :2 * H]], axis=1)
            dw_a = jnp.transpose(full.reshape(D, N_CHIPS, -1), (1, 0, 2))
            dw_b = matmul_tn(ogb.reshape(-1, W), dyb.reshape(1, -1, D), D, "dw_sq").reshape(N_CHIPS, -1, D)
        dmod[i][1] = dm
        d_norm_g[i][1] = jnp.sum(dg, axis=(0, 1))
        if i > 0:
            dx, dw_in0, dw_out0 = ffn_back(i, 0, 0, dx)
            if pending is not None:
                stacks = reduce_finish(pending[0], dx, where, "reduce_wait_%d" % pending[1], stacks,
                                       targets(pending[1], range(6)))
                pending = None
            handed = pair_start([dw_in0, dw_in1, dw_out0, dw_out1, dw_a, dw_b], "pair_start_%d" % i)
            paired, tok = (handed, i), handed[4][0, 0]
        else:
            part_a = reduce_start([dw_in1, dw_out1, dw_a, dw_b], c_idx, "reduce_start_0a")
            dx, dw_in0, dw_out0 = ffn_back(0, 0, 0, dx, part_a[4][0, 0])
            if pending is not None:
                stacks = reduce_finish(pending[0], dx, where, "reduce_wait_%d" % pending[1], stacks,
                                       targets(pending[1], range(6)))
            stacks = reduce_finish(part_a, dx, where, "reduce_wait_0a", stacks, targets(0, (1, 3, 4, 5)))

    part = dict(
        norm_g=jnp.stack([jnp.stack(r) for r in d_norm_g]),
        cm_b_glu=jnp.stack(g_cm["b_glu"]), cm_w_dw=jnp.stack(g_cm["w_dw"]), cm_b_dw=jnp.stack(g_cm["b_dw"]),
        cm_ln_g=jnp.stack(g_cm["ln_g"]), cm_ln_b=jnp.stack(g_cm["ln_b"]), cm_b_pw=jnp.stack(g_cm["b_pw"]),
        dn_w_sconv=jnp.stack(g_dn["w_sconv"]), dn_a_log=jnp.stack(g_dn["a_log"]), dn_dt_bias=jnp.stack(g_dn["dt_bias"]),
        dn_o_g=jnp.stack(g_dn["o_g"]), final_g=jnp.sum(d_final_g, axis=(0, 1)),
        loss=jnp.sum(loss_part[:, 0, 0]).reshape(1))
    dmod_loc = jnp.stack([jnp.concatenate(r, axis=1) for r in dmod]).reshape(L, BL, C9)
    keys = list(part)
    packed = _pack([part[k] for k in keys] + [dmod_loc])
    R = packed.shape[0]
    gathered = allgather8(packed).reshape(N_DEV, R, LANES)
    summed = _unpack(sum_devices(gathered).reshape(-1), [part[k].shape for k in keys])
    tot = dict(zip(keys, summed))
    n_small = sum(int(part[k].size) for k in keys)
    dmod_all = gathered.reshape(N_DEV, -1)[:, n_small:n_small + L * BL * C9].reshape(N_DEV, L, BL, C9)
    dmod_all = jnp.transpose(dmod_all, (1, 0, 2, 3)).reshape(L, NB, C9)
    dmod_cols = lax.dynamic_slice_in_dim(dmod_all, chip * Ca, Ca, axis=2)
    g_w_ada, g_b_ada = ada_bwd(c_all, dmod_cols, dmod_all)
    delta, new_m, new_v = {}, {}, {}
    part_b = reduce_start([dw_in0, dw_out0], c_idx, "reduce_start_0b", g_w_ada)
    delta["w_ada"], new_m["w_ada"], new_v["w_ada"] = _adamw_any(w_ada, g_w_ada, m_w_ada, v_w_ada, "adamw_w_ada",
                                                                 part_b[4])
    stacks = reduce_finish(part_b, new_v["w_ada"], where, "reduce_wait_0b", stacks, targets(0, (0, 2)))

    def my_cols(full):
        n = full.shape[-1] // N_CHIPS
        return lax.dynamic_slice_in_dim(full, chip * n, n, axis=full.ndim - 1)

    grads = dict(
        norm_g=my_cols(tot["norm_g"]), w_ada=g_w_ada, b_ada=g_b_ada.reshape(L, C9),
        cm_b_glu=tot["cm_b_glu"], cm_w_dw=my_cols(tot["cm_w_dw"]), cm_b_dw=tot["cm_b_dw"], cm_ln_g=tot["cm_ln_g"],
        cm_ln_b=tot["cm_ln_b"], cm_b_pw=tot["cm_b_pw"], dn_w_sconv=my_cols(tot["dn_w_sconv"]),
        dn_a_log=tot["dn_a_log"], dn_dt_bias=tot["dn_dt_bias"], dn_o_g=tot["dn_o_g"], final_g=tot["final_g"],
        **{n: stacks[n].reshape(weights[n].shape) for n in big_names})

    large = ("w_ada", "w_ffn_in", "w_ffn_out", "cm_w_glu", "cm_w_pw", "dn_w_in", "dn_w_out")
    for n in large[1:]:
        delta[n], new_m[n], new_v[n] = _adamw_any(weights[n], grads[n], mom_m[n], mom_v[n], "adamw_" + n)
    rest = [n for n in names if n not in large]
    shapes = [weights[n].shape for n in rest]
    pd, pm, pv = adamw(_pack([weights[n] for n in rest]), _pack([grads[n] for n in rest]),
                       _pack([mom_m[n] for n in rest]), _pack([mom_v[n] for n in rest]), "adamw_small")
    for n, d_, m_, v_ in zip(rest, _unpack(pd.reshape(-1), shapes), _unpack(pm.reshape(-1), shapes),
                             _unpack(pv.reshape(-1), shapes)):
        delta[n], new_m[n], new_v[n] = d_, m_, v_

    return (tot["loss"].reshape(()), dx, *[grads[n] for n in names], *[delta[n] for n in names],
            *[new_m[n] for n in names], *[new_v[n] for n in names])
```

```python
import functools

import jax
import jax.numpy as jnp
from jax import lax
from jax.experimental import pallas as pl
from jax.experimental.pallas import tpu as pltpu

F32 = jnp.float32
BF16 = jnp.bfloat16
EPS = 1e-6
CHUNK = 64
CHUNKS_PER_STEP = 4
N_CHIPS = 4
N_DEV = 8
LANES = 128
SUBLANES = 8
CONV_HALO = 32
SCONV_HALO = 8
VMEM_LIMIT_V7X = 60 * 1024 * 1024
DW_VMEM_BUDGET = 40 * 1024 * 1024
TOKENS_PER_STEP = 512
TOKENS_PER_STEP_WIDE = 256
ELEMENTWISE_BLOCK = 1 << 19
MESH = pl.DeviceIdType.MESH
HBM_SPEC = pl.BlockSpec(memory_space=pltpu.HBM)

ADAM_LR, ADAM_B1, ADAM_B2, ADAM_EPS, ADAM_WD, ADAM_STEP = 0.001, 0.9, 0.999, 1e-08, 0.01, 10


def _cparams(n_axes):
    return pltpu.CompilerParams(dimension_semantics=("arbitrary",) * n_axes, vmem_limit_bytes=VMEM_LIMIT_V7X)


def _tile(n, pref, mult=8):
    for t in range(min(n, pref) // mult * mult, 0, -mult):
        if n % t == 0:
            return t
    return n


def _mm(a, b):
    return lax.dot_general(a.astype(BF16), b.astype(BF16), (((1,), (0,)), ((), ())), preferred_element_type=F32)


def _mm_nt(a, b):
    return lax.dot_general(a.astype(BF16), b.astype(BF16), (((1,), (1,)), ((), ())), preferred_element_type=F32)


def _mm_tn(a, b):
    return lax.dot_general(a.astype(BF16), b.astype(BF16), (((0,), (0,)), ((), ())), preferred_element_type=F32)


def _sigmoid(x):
    return jax.nn.sigmoid(x)


def _dsilu(x, s):
    return s * (1.0 + x * (1.0 - s))


def _softplus(x):
    return jnp.maximum(x, 0.0) + jnp.log(1.0 + jnp.exp(-jnp.abs(x)))


def _modnorm(x, g, scale, shift):
    r = lax.rsqrt(jnp.mean(x * x, axis=-1, keepdims=True) + EPS)
    return (x * r) * g * (1.0 + scale) + shift


def _modnorm_bwd(x, g, scale, dh):
    r = lax.rsqrt(jnp.mean(x * x, axis=-1, keepdims=True) + EPS)
    xn = x * r
    dshift = jnp.sum(dh, axis=0, keepdims=True)
    dscale = jnp.sum(dh * (xn * g), axis=0, keepdims=True)
    dhn = dh * (1.0 + scale)
    dg = jnp.sum(dhn * xn, axis=0, keepdims=True)
    dxn = dhn * g
    dx = r * (dxn - xn * jnp.mean(dxn * xn, axis=-1, keepdims=True))
    return dx, dg, dscale, dshift


def _sum0(a):
    return jnp.sum(a, axis=0, keepdims=True)


def ffn_fwd(x, mod3, g, w_in, w_out):
    B, T, D = x.shape
    Fc = w_in.shape[2]
    w_in = w_in.reshape(2, 2, D, Fc)
    w_out = w_out.reshape(2, Fc, D)
    tm = _tile(T, TOKENS_PER_STEP)

    def half(h, wi_ref, wo_ref, gu_ref):
        gt = _mm(h, wi_ref[0])
        up = _mm(h, wi_ref[1])
        gu_ref[0] = gt.astype(BF16)
        gu_ref[1] = up.astype(BF16)
        return _mm(gt * _sigmoid(gt) * up, wo_ref[...])

    def body_a(x_ref, mod_ref, g_ref, wi_ref, wo_ref, h_ref, gu_ref, y0_ref):
        h = _modnorm(x_ref[...], g_ref[...], mod_ref[1:2, :], mod_ref[0:1, :]).astype(BF16)
        h_ref[...] = h
        y0_ref[...] = half(h, wi_ref, wo_ref, gu_ref)

    def body_b(x_ref, h_ref, y0_ref, mod_ref, wi_ref, wo_ref, gu_any, xo_ref, y_ref, gu_ref):
        y = y0_ref[...] + half(h_ref[...], wi_ref, wo_ref, gu_ref)
        y_ref[...] = y.astype(BF16)
        xo_ref[...] = x_ref[...] + 0.5 * (1.0 + mod_ref[2:3, :]) * y

    tok = pl.BlockSpec((None, tm, D), lambda b, t: (b, t, 0))
    per_b3 = pl.BlockSpec((None, 3, D), lambda b, t: (b, 0, 0))
    gu_shape = jax.ShapeDtypeStruct((2, B, T, 2 * Fc), BF16)

    def w_specs(part):
        return [pl.BlockSpec((2, None, D, Fc), lambda b, t: (0, part, 0, 0)),
                pl.BlockSpec((None, Fc, D), lambda b, t: (part, 0, 0))]

    def gu_spec(part):
        return pl.BlockSpec((2, None, tm, Fc), lambda b, t: (0, b, t, part))

    h, gu, y0 = pl.pallas_call(
        body_a, name="ffn_fwd_a", grid=(B, T // tm),
        in_specs=[tok, per_b3, pl.BlockSpec((1, D), lambda b, t: (0, 0))] + w_specs(0),
        out_specs=[tok, gu_spec(0), tok],
        out_shape=[jax.ShapeDtypeStruct((B, T, D), BF16), gu_shape, jax.ShapeDtypeStruct((B, T, D), F32)],
        compiler_params=_cparams(2),
    )(x, mod3, g, w_in, w_out)
    x_new, y, gu = pl.pallas_call(
        body_b, name="ffn_fwd_b", grid=(B, T // tm),
        in_specs=[tok, tok, tok, per_b3] + w_specs(1) + [pl.BlockSpec(memory_space=pl.ANY)],
        out_specs=[tok, tok, gu_spec(1)],
        out_shape=[jax.ShapeDtypeStruct((B, T, D), F32), jax.ShapeDtypeStruct((B, T, D), BF16), gu_shape],
        input_output_aliases={6: 2},
        compiler_params=_cparams(2),
    )(x, h, y0, mod3, w_in, w_out, gu)
    return x_new, y, h, gu


def ffn_bwd_part(part, dres, gu, mod3, w_in, w_out, first=None, y=None, x=None, g=None):
    B, T, D = dres.shape
    Fc = w_in.shape[2]
    F = 2 * Fc
    w_in = w_in.reshape(2, 2, D, Fc)
    w_out = w_out.reshape(2, Fc, D)
    tm = _tile(T, TOKENS_PER_STEP_WIDE)

    def half(dy, gu_ref, wi_ref, wo_ref, a_ref, dgu_ref):
        gt = gu_ref[0].astype(F32)
        up = gu_ref[1].astype(F32)
        sg = _sigmoid(gt)
        silu = gt * sg
        a_ref[...] = (silu * up).astype(BF16)
        da = _mm_nt(dy, wo_ref[...])
        dup = (da * silu).astype(BF16)
        dgt = (da * up * _dsilu(gt, sg)).astype(BF16)
        dgu_ref[0] = dgt
        dgu_ref[1] = dup
        return _mm_nt(dgt, wi_ref[0]) + _mm_nt(dup, wi_ref[1])

    tok = pl.BlockSpec((None, tm, D), lambda b, t: (b, t, 0))
    per_b3 = pl.BlockSpec((None, 3, D), lambda b, t: (b, 0, 0))
    per_b1 = pl.BlockSpec((None, 1, D), lambda b, t: (b, 0, 0))
    gu_spec = pl.BlockSpec((2, None, tm, Fc), lambda b, t: (0, b, t, part))
    a_spec = pl.BlockSpec((None, tm, Fc), lambda b, t: (b, t, part))
    wi_spec = pl.BlockSpec((2, None, D, Fc), lambda b, t: (0, part, 0, 0))
    wo_spec = pl.BlockSpec((None, Fc, D), lambda b, t: (part, 0, 0))
    a_shape = jax.ShapeDtypeStruct((B, T, F), BF16)
    dgu_shape = jax.ShapeDtypeStruct((2, B, T, F), BF16)

    if part == 0:
        def body(dres_ref, y_ref, gu_ref, mod_ref, wi_ref, wo_ref, a_ref, dgu_ref, dy_ref, dh_ref, dgate_ref):
            @pl.when(pl.program_id(1) == 0)
            def _():
                dgate_ref[...] = jnp.zeros_like(dgate_ref)

            dres = dres_ref[...]
            dy = (0.5 * (1.0 + mod_ref[2:3, :]) * dres).astype(BF16)
            dy_ref[...] = dy
            dgate_ref[...] += _sum0(dres * (0.5 * y_ref[...]))
            dh_ref[...] = half(dy, gu_ref, wi_ref, wo_ref, a_ref, dgu_ref).astype(BF16)

        return pl.pallas_call(
            body, name="ffn_bwd_a", grid=(B, T // tm),
            in_specs=[tok, tok, gu_spec, per_b3, wi_spec, wo_spec],
            out_specs=[a_spec, gu_spec, tok, tok, per_b1],
            out_shape=[a_shape, dgu_shape, jax.ShapeDtypeStruct((B, T, D), BF16), jax.ShapeDtypeStruct((B, T, D), BF16),
                       jax.ShapeDtypeStruct((B, 1, D), F32)],
            compiler_params=_cparams(2),
        )(dres, y, gu, mod3, w_in, w_out)

    a_full, dgu_full, dy, dh0 = first

    def body(x_ref, dres_ref, dy_ref, dh0_ref, gu_ref, mod_ref, g_ref, wi_ref, wo_ref, a_any, dgu_any,
             dx_ref, a_ref, dgu_ref, dmod_ref, dg_ref):
        @pl.when(pl.program_id(1) == 0)
        def _():
            dmod_ref[...] = jnp.zeros_like(dmod_ref)
            dg_ref[...] = jnp.zeros_like(dg_ref)

        dh = dh0_ref[...] + half(dy_ref[...], gu_ref, wi_ref, wo_ref, a_ref, dgu_ref)
        dxn, dg, dscale, dshift = _modnorm_bwd(x_ref[...], g_ref[...], mod_ref[1:2, :], dh)
        dx_ref[...] = dres_ref[...] + dxn
        dmod_ref[0:1, :] += dshift
        dmod_ref[1:2, :] += dscale
        dg_ref[...] += dg

    return pl.pallas_call(
        body, name="ffn_bwd_b", grid=(B, T // tm),
        in_specs=[tok, tok, tok, tok, gu_spec, per_b3, pl.BlockSpec((1, D), lambda b, t: (0, 0)), wi_spec, wo_spec,
                  ANY_SPEC, ANY_SPEC],
        out_specs=[tok, a_spec, gu_spec, per_b3, per_b1],
        out_shape=[jax.ShapeDtypeStruct((B, T, D), F32), a_shape, dgu_shape, jax.ShapeDtypeStruct((B, 3, D), F32),
                   jax.ShapeDtypeStruct((B, 1, D), F32)],
        input_output_aliases={9: 1, 10: 2},
        compiler_params=_cparams(2),
    )(x, dres, dy, dh0, gu, mod3, g, w_in, w_out, a_full, dgu_full)


def matmul_tn(xm, ym, bm, name):
    N, K = xm.shape
    GY, _, MY = ym.shape
    per = MY // bm
    nb = GY * per
    fixed = K * bm * (4 + 2 * 2)
    tn = _tile(N, max(512, (DW_VMEM_BUDGET - fixed) // (2 * 2 * (K + bm))), 256)

    def body(x_ref, y_ref, o_ref, acc_s):
        n = pl.program_id(1)

        @pl.when(n == 0)
        def _():
            acc_s[...] = jnp.zeros_like(acc_s)

        acc_s[...] += _mm_tn(x_ref[...], y_ref[...])

        @pl.when(n == N // tn - 1)
        def _():
            o_ref[...] = acc_s[...].astype(BF16)

    return pl.pallas_call(
        body, name=name, grid=(nb, N // tn),
        in_specs=[pl.BlockSpec((tn, K), lambda m, n: (n, 0)),
                  pl.BlockSpec((None, tn, bm), lambda m, n: (m // per, n, m % per))],
        out_specs=pl.BlockSpec((None, K, bm), lambda m, n: (m, 0, 0)),
        out_shape=jax.ShapeDtypeStruct((nb, K, bm), BF16),
        scratch_shapes=[pltpu.VMEM((K, bm), F32)],
        compiler_params=_cparams(2),
    )(xm, ym)


def final_loss(x, fg, target):
    B, T, D = x.shape
    tm = _tile(T, TOKENS_PER_STEP)

    def body(x_ref, g_ref, t_ref, dx_ref, dfg_ref, loss_ref):
        t = pl.program_id(1)

        @pl.when(t == 0)
        def _():
            dfg_ref[...] = jnp.zeros_like(dfg_ref)
            loss_ref[...] = jnp.zeros_like(loss_ref)

        xv = x_ref[...]
        g = g_ref[...]
        r = lax.rsqrt(jnp.mean(xv * xv, axis=-1, keepdims=True) + EPS)
        xn = xv * r
        err = xn * g - t_ref[...]
        tok_loss = jnp.mean(err * err, axis=-1, keepdims=True)
        loss_ref[...] += 0.5 * jnp.sum(tok_loss, axis=0, keepdims=True)
        dy = err * (1.0 / D)
        dfg_ref[...] += _sum0(dy * xn)
        dxn = dy * g
        dx_ref[...] = r * (dxn - xn * jnp.mean(dxn * xn, axis=-1, keepdims=True))

    tok = pl.BlockSpec((None, tm, D), lambda b, t: (b, t, 0))
    return pl.pallas_call(
        body, name="final_loss", grid=(B, T // tm),
        in_specs=[tok, pl.BlockSpec((1, D), lambda b, t: (0, 0)), tok],
        out_specs=[tok, pl.BlockSpec((None, 1, D), lambda b, t: (b, 0, 0)),
                   pl.BlockSpec((None, 1, LANES), lambda b, t: (b, 0, 0))],
        out_shape=[jax.ShapeDtypeStruct((B, T, D), F32), jax.ShapeDtypeStruct((B, 1, D), F32),
                   jax.ShapeDtypeStruct((B, 1, LANES), F32)],
        compiler_params=_cparams(2),
    )(x, fg, target)


def _past_halo_spec(tm, halo, width):
    return pl.BlockSpec((None, halo, width), lambda b, t: (b, jnp.maximum(t * (tm // halo) - 1, 0), 0))


def _future_halo_spec(tm, halo, width, T):
    return pl.BlockSpec((None, halo, width), lambda b, t: (b, jnp.minimum((t + 1) * (tm // halo), T // halo - 1), 0))


def _fill_shifted(ext_s):
    n = ext_s.shape[1]
    for b in range(1, SUBLANES):
        ext_s[b, 0:n - SUBLANES, :] = ext_s[0, pl.ds(b, n - SUBLANES), :]


def _shifted(ext_s, offset, rows):
    a, b = divmod(offset, SUBLANES)
    return ext_s[b, pl.ds(SUBLANES * a, rows), :]


def _glu_fwd(h, w_ref, bias):
    D = h.shape[1]
    a = jnp.concatenate([_mm(h, w_ref[0]), _mm(h, w_ref[1])], axis=1) + bias[:, :D]
    b = jnp.concatenate([_mm(h, w_ref[2]), _mm(h, w_ref[3])], axis=1) + bias[:, D:]
    return a, b


def conv_glu_fwd(x, mod3, g, w_glu, b_glu):
    B, T, D = x.shape
    tm = _tile(T, TOKENS_PER_STEP)

    def body(x_ref, mod_ref, g_ref, w_ref, b_ref, u_ref):
        h = _modnorm(x_ref[...], g_ref[...], mod_ref[1:2, :], mod_ref[0:1, :]).astype(BF16)
        a, b = _glu_fwd(h, w_ref, b_ref[...])
        u_ref[...] = a * _sigmoid(b)

    tok = pl.BlockSpec((None, tm, D), lambda b, t: (b, t, 0))
    return pl.pallas_call(
        body, name="conv_glu_fwd", grid=(B, T // tm),
        in_specs=[tok, pl.BlockSpec((None, 3, D), lambda b, t: (b, 0, 0)),
                  pl.BlockSpec((1, D), lambda b, t: (0, 0)),
                  pl.BlockSpec((4, D, D // 2), lambda b, t: (0, 0, 0)),
                  pl.BlockSpec((1, 2 * D), lambda b, t: (0, 0))],
        out_specs=tok, out_shape=jax.ShapeDtypeStruct((B, T, D), F32),
        compiler_params=_cparams(2),
    )(x, mod3, g, w_glu, b_glu)


def _layer_norm_parts(u2):
    mu = jnp.mean(u2, axis=-1, keepdims=True)
    xc = u2 - mu
    rs = lax.rsqrt(jnp.mean(xc * xc, axis=-1, keepdims=True) + EPS)
    return xc * rs, rs


def conv_out_fwd(x, u, mod3, w_dw, b_dw, ln_g, ln_b, w_pw, b_pw):
    B, T, D = x.shape
    K = w_dw.shape[0] - 1
    tm = _tile(T, TOKENS_PER_STEP)

    def body(x_ref, u_ref, halo_ref, mod_ref, wdw_ref, bdw_ref, lg_ref, lb_ref, wpw_ref, bpw_ref,
             xo_ref, y_ref, u2_ref, ext_s):
        t = pl.program_id(1)
        ext_s[0, 0:CONV_HALO, :] = jnp.where(t > 0, halo_ref[...], 0.0)
        ext_s[0, CONV_HALO:, :] = u_ref[...]
        _fill_shifted(ext_s)
        acc = jnp.broadcast_to(bdw_ref[...], (tm, D))
        for k in range(K):
            acc = acc + wdw_ref[k:k + 1, :] * _shifted(ext_s, CONV_HALO - (K - 1) + k, tm)
        u2_ref[...] = acc
        xh, _ = _layer_norm_parts(acc)
        l = xh * lg_ref[...] + lb_ref[...]
        u3 = l * _sigmoid(l)
        y = _mm(u3, wpw_ref[...]) + bpw_ref[...]
        y_ref[...] = y
        xo_ref[...] = x_ref[...] + (1.0 + mod_ref[2:3, :]) * y

    tok = pl.BlockSpec((None, tm, D), lambda b, t: (b, t, 0))
    vec = pl.BlockSpec((1, D), lambda b, t: (0, 0))
    return pl.pallas_call(
        body, name="conv_out_fwd", grid=(B, T // tm),
        in_specs=[tok, tok, _past_halo_spec(tm, CONV_HALO, D), pl.BlockSpec((None, 3, D), lambda b, t: (b, 0, 0)),
                  pl.BlockSpec((K + 1, D), lambda b, t: (0, 0)), vec, vec, vec,
                  pl.BlockSpec((D, D), lambda b, t: (0, 0)), vec],
        out_specs=[tok, tok, tok], out_shape=[jax.ShapeDtypeStruct((B, T, D), F32)] * 3,
        scratch_shapes=[pltpu.VMEM((SUBLANES, tm + CONV_HALO, D), F32)],
        compiler_params=_cparams(2),
    )(x, u, u, mod3, w_dw, b_dw, ln_g, ln_b, w_pw, b_pw)


def conv_out_bwd(dres, y, u2, mod3, ln_g, ln_b, w_pw):
    B, T, D = dres.shape
    tm = _tile(T, TOKENS_PER_STEP)

    def body(dres_ref, y_ref, u2_ref, mod_ref, lg_ref, lb_ref, wpw_ref, du2_ref, u3_ref, dy_ref, dgate_ref, vec_ref):
        t = pl.program_id(1)

        @pl.when(t == 0)
        def _():
            dgate_ref[...] = jnp.zeros_like(dgate_ref)
            vec_ref[...] = jnp.zeros_like(vec_ref)

        dres = dres_ref[...]
        dy = (1.0 + mod_ref[2:3, :]) * dres
        dy_ref[...] = dy.astype(BF16)
        dgate_ref[...] += _sum0(dres * y_ref[...])
        xh, rs = _layer_norm_parts(u2_ref[...])
        lg = lg_ref[...]
        l = xh * lg + lb_ref[...]
        sg = _sigmoid(l)
        u3_ref[...] = (l * sg).astype(BF16)
        du3 = _mm_nt(dy, wpw_ref[...])
        dl = du3 * _dsilu(l, sg)
        dxh = dl * lg
        du2 = rs * (dxh - jnp.mean(dxh, axis=-1, keepdims=True) - xh * jnp.mean(dxh * xh, axis=-1, keepdims=True))
        du2_ref[...] = du2
        vec_ref[0:1, :] += _sum0(dy)
        vec_ref[1:2, :] += _sum0(dl * xh)
        vec_ref[2:3, :] += _sum0(dl)
        vec_ref[3:4, :] += _sum0(du2)

    tok = pl.BlockSpec((None, tm, D), lambda b, t: (b, t, 0))
    tokb = pl.BlockSpec((None, tm, D), lambda b, t: (b, t, 0))
    vec = pl.BlockSpec((1, D), lambda b, t: (0, 0))
    return pl.pallas_call(
        body, name="conv_out_bwd", grid=(B, T // tm),
        in_specs=[tok, tok, tok, pl.BlockSpec((None, 3, D), lambda b, t: (b, 0, 0)), vec, vec,
                  pl.BlockSpec((D, D), lambda b, t: (0, 0))],
        out_specs=[tok, tokb, tokb, pl.BlockSpec((None, 1, D), lambda b, t: (b, 0, 0)),
                   pl.BlockSpec((None, 4, D), lambda b, t: (b, 0, 0))],
        out_shape=[jax.ShapeDtypeStruct((B, T, D), F32), jax.ShapeDtypeStruct((B, T, D), BF16),
                   jax.ShapeDtypeStruct((B, T, D), BF16), jax.ShapeDtypeStruct((B, 1, D), F32),
                   jax.ShapeDtypeStruct((B, 4, D), F32)],
        compiler_params=_cparams(2),
    )(dres, y, u2, mod3, ln_g, ln_b, w_pw)


def conv_glu_bwd(x, dres, du2, u, mod3, g, w_glu, b_glu, w_dw):
    B, T, D = x.shape
    K = w_dw.shape[0] - 1
    tm = _tile(T, TOKENS_PER_STEP_WIDE)
    nt = T // tm

    def body(x_ref, dres_ref, du2_ref, du2h_ref, u_ref, uh_ref, mod_ref, g_ref, w_ref, b_ref, wdw_ref,
             dx_ref, h_ref, dab_ref, dwdw_ref, dbglu_ref, dmod_ref, dg_ref, extu_s, extd_s):
        t = pl.program_id(1)

        @pl.when(t == 0)
        def _():
            dwdw_ref[...] = jnp.zeros_like(dwdw_ref)
            dbglu_ref[...] = jnp.zeros_like(dbglu_ref)
            dmod_ref[...] = jnp.zeros_like(dmod_ref)
            dg_ref[...] = jnp.zeros_like(dg_ref)

        du2 = du2_ref[...]
        extu_s[0, 0:CONV_HALO, :] = jnp.where(t > 0, uh_ref[...], 0.0)
        extu_s[0, CONV_HALO:, :] = u_ref[...]
        extd_s[0, 0:tm, :] = du2
        extd_s[0, tm:, :] = jnp.where(t < nt - 1, du2h_ref[...], 0.0)
        _fill_shifted(extu_s)
        _fill_shifted(extd_s)
        du = jnp.zeros((tm, D), F32)
        for k in range(K):
            du = du + wdw_ref[k:k + 1, :] * _shifted(extd_s, K - 1 - k, tm)
            dwdw_ref[k:k + 1, :] += _sum0(du2 * _shifted(extu_s, CONV_HALO - (K - 1) + k, tm))
        xv = x_ref[...]
        h = _modnorm(xv, g_ref[...], mod_ref[1:2, :], mod_ref[0:1, :]).astype(BF16)
        h_ref[...] = h
        a, b = _glu_fwd(h, w_ref, b_ref[...])
        sb = _sigmoid(b)
        da = du * sb
        db = du * a * sb * (1.0 - sb)
        dbglu_ref[:, 0:D] += _sum0(da)
        dbglu_ref[:, D:] += _sum0(db)
        da = da.astype(BF16)
        db = db.astype(BF16)
        dab_ref[:, 0:D] = da
        dab_ref[:, D:] = db
        Dh2 = D // 2
        dh = (_mm_nt(da[:, :Dh2], w_ref[0]) + _mm_nt(da[:, Dh2:], w_ref[1])
              + _mm_nt(db[:, :Dh2], w_ref[2]) + _mm_nt(db[:, Dh2:], w_ref[3]))
        dxn, dg, dscale, dshift = _modnorm_bwd(xv, g_ref[...], mod_ref[1:2, :], dh)
        dx_ref[...] = dres_ref[...] + dxn
        dmod_ref[0:1, :] += dshift
        dmod_ref[1:2, :] += dscale
        dg_ref[...] += dg

    tok = pl.BlockSpec((None, tm, D), lambda b, t: (b, t, 0))
    return pl.pallas_call(
        body, name="conv_glu_bwd", grid=(B, nt),
        in_specs=[tok, tok, tok, _future_halo_spec(tm, CONV_HALO, D, T), tok, _past_halo_spec(tm, CONV_HALO, D),
                  pl.BlockSpec((None, 3, D), lambda b, t: (b, 0, 0)), pl.BlockSpec((1, D), lambda b, t: (0, 0)),
                  pl.BlockSpec((4, D, D // 2), lambda b, t: (0, 0, 0)), pl.BlockSpec((1, 2 * D), lambda b, t: (0, 0)),
                  pl.BlockSpec((K + 1, D), lambda b, t: (0, 0))],
        out_specs=[tok, tok, pl.BlockSpec((None, tm, 2 * D), lambda b, t: (b, t, 0)),
                   pl.BlockSpec((None, K + 1, D), lambda b, t: (b, 0, 0)),
                   pl.BlockSpec((None, 1, 2 * D), lambda b, t: (b, 0, 0)),
                   pl.BlockSpec((None, 3, D), lambda b, t: (b, 0, 0)),
                   pl.BlockSpec((None, 1, D), lambda b, t: (b, 0, 0))],
        out_shape=[jax.ShapeDtypeStruct((B, T, D), F32), jax.ShapeDtypeStruct((B, T, D), BF16),
                   jax.ShapeDtypeStruct((B, T, 2 * D), BF16), jax.ShapeDtypeStruct((B, K + 1, D), F32),
                   jax.ShapeDtypeStruct((B, 1, 2 * D), F32), jax.ShapeDtypeStruct((B, 3, D), F32),
                   jax.ShapeDtypeStruct((B, 1, D), F32)],
        scratch_shapes=[pltpu.VMEM((SUBLANES, tm + CONV_HALO, D), F32)] * 2,
        compiler_params=_cparams(2),
    )(x, dres, du2, du2, u, u, mod3, g, w_glu, b_glu, w_dw)


def dn_proj_fwd(x, mod3, g, w_main, w_ab):
    B, T, D = x.shape
    W = w_main.shape[1] // 4
    tm = _tile(T, TOKENS_PER_STEP)

    def body(x_ref, mod_ref, g_ref, wm_ref, wab_ref, pre_ref, z_ref, ab_ref):
        h = _modnorm(x_ref[...], g_ref[...], mod_ref[1:2, :], mod_ref[0:1, :]).astype(BF16)
        for p in range(3):
            pre_ref[:, p * W:(p + 1) * W] = _mm(h, wm_ref[:, p * W:(p + 1) * W])
        z_ref[...] = _mm(h, wm_ref[:, 3 * W:])
        ab_ref[...] = _mm(h, wab_ref[...])

    return pl.pallas_call(
        body, name="dn_proj_fwd", grid=(B, T // tm),
        in_specs=[pl.BlockSpec((None, tm, D), lambda b, t: (b, t, 0)), pl.BlockSpec((None, 3, D), lambda b, t: (b, 0, 0)),
                  pl.BlockSpec((1, D), lambda b, t: (0, 0)), pl.BlockSpec((D, 4 * W), lambda b, t: (0, 0)),
                  pl.BlockSpec((D, LANES), lambda b, t: (0, 0))],
        out_specs=[pl.BlockSpec((None, tm, 3 * W), lambda b, t: (b, t, 0)),
                   pl.BlockSpec((None, tm, W), lambda b, t: (b, t, 0)),
                   pl.BlockSpec((None, tm, LANES), lambda b, t: (b, t, 0))],
        out_shape=[jax.ShapeDtypeStruct((B, T, 3 * W), F32), jax.ShapeDtypeStruct((B, T, W), F32),
                   jax.ShapeDtypeStruct((B, T, LANES), F32)],
        compiler_params=_cparams(2),
    )(x, mod3, g, w_main, w_ab)


def _sconv(ext_s, w_ref, tm, K):
    acc = w_ref[0:1, :] * ext_s[pl.ds(SCONV_HALO - (K - 1), tm), :]
    for k in range(1, K):
        acc = acc + w_ref[k:k + 1, :] * ext_s[pl.ds(SCONV_HALO - (K - 1) + k, tm), :]
    return acc


def _lane_col(val, lane, idx):
    return jnp.sum(jnp.where(lane == idx, val, 0.0), axis=1, keepdims=True)


def dn_conv_fwd(pre, ab, w_sconv, alog_row, dt_row, H):
    B, T, W3 = pre.shape
    W = W3 // 3
    Dh = W // H
    K = w_sconv.shape[0]
    tm = _tile(T, TOKENS_PER_STEP)

    def body(pre_ref, halo_ref, ab_ref, w_ref, alog_ref, dt_ref, q_ref, k_ref, v_ref, gb_ref, bb_ref, ext_s):
        t = pl.program_id(1)
        ext_s[0:SCONV_HALO, :] = jnp.where(t > 0, halo_ref[...], 0.0)
        ext_s[SCONV_HALO:, :] = pre_ref[...]
        cv = _sconv(ext_s, w_ref, tm, K)
        qkv = cv * _sigmoid(cv)
        ab = ab_ref[...]
        lane = lax.broadcasted_iota(jnp.int32, ab.shape, 1)
        g_all = -jnp.exp(alog_ref[...]) * _softplus(ab + dt_ref[...])
        beta_all = _sigmoid(ab)
        for h in range(H):
            q_ref[h] = qkv[:, h * Dh:(h + 1) * Dh]
            k_ref[h] = qkv[:, W + h * Dh:W + (h + 1) * Dh]
            v_ref[h] = qkv[:, 2 * W + h * Dh:2 * W + (h + 1) * Dh]
            gb_ref[h] = jnp.broadcast_to(_lane_col(g_all, lane, h), (tm, Dh))
            bb_ref[h] = jnp.broadcast_to(_lane_col(beta_all, lane, H + h), (tm, Dh))

    hm = pl.BlockSpec((None, H, tm, Dh), lambda b, t: (b, 0, t, 0))
    row = pl.BlockSpec((1, LANES), lambda b, t: (0, 0))
    return pl.pallas_call(
        body, name="dn_conv_fwd", grid=(B, T // tm),
        in_specs=[pl.BlockSpec((None, tm, W3), lambda b, t: (b, t, 0)), _past_halo_spec(tm, SCONV_HALO, W3),
                  pl.BlockSpec((None, tm, LANES), lambda b, t: (b, t, 0)),
                  pl.BlockSpec((K, W3), lambda b, t: (0, 0)), row, row],
        out_specs=[hm] * 5, out_shape=[jax.ShapeDtypeStruct((B, H, T, Dh), F32)] * 5,
        scratch_shapes=[pltpu.VMEM((tm + SCONV_HALO, W3), F32)],
        compiler_params=_cparams(2),
    )(pre, pre, ab, w_sconv, alog_row, dt_row)


def _bdot(spec):
    return lambda a, b: jnp.einsum(spec, a.astype(BF16), b.astype(BF16), preferred_element_type=F32)


_NN, _NT, _TN = "gij,gjk->gik", "gik,gjk->gij", "gki,gkj->gij"


def _make_bdots():
    nn_, nt_, tn_ = _bdot(_NN), _bdot(_NT), _bdot(_TN)

    @jax.custom_vjp
    def nn(a, b):
        return nn_(a, b)

    @jax.custom_vjp
    def nt(a, b):
        return nt_(a, b)

    @jax.custom_vjp
    def tn(a, b):
        return tn_(a, b)

    nn.defvjp(lambda a, b: (nn_(a, b), (a, b)), lambda r, d: (nt_(d, r[1]), tn_(r[0], d)))
    nt.defvjp(lambda a, b: (nt_(a, b), (a, b)), lambda r, d: (nn_(d, r[1]), tn_(d, r[0])))
    tn.defvjp(lambda a, b: (tn_(a, b), (a, b)), lambda r, d: (nt_(r[1], d), nn_(r[0], d)))
    return nn, nt, tn


def _unit_lower_inverse(A, known=None):
    hdot = functools.partial(jnp.einsum, precision=lax.Precision.HIGH, preferred_element_type=F32)
    C = A.shape[-1]

    def impl(A):
        eye = (lax.broadcasted_iota(jnp.int32, A.shape, 1) == lax.broadcasted_iota(jnp.int32, A.shape, 2)).astype(F32)
        Tm = eye - A
        Ap = A
        for _ in range(max(1, (C - 1).bit_length()) - 1):
            Ap = hdot(_NN, Ap, Ap)
            Tm = Tm + hdot(_NN, Tm, Ap)
        return Tm

    @jax.custom_vjp
    def inv(A, given):
        return impl(A) if known is None else given

    def fwd(A, given):
        Tm = impl(A) if known is None else given
        return Tm, Tm

    def bwd(Tm, dT):
        return -hdot(_NT, hdot(_TN, Tm, dT), Tm), jnp.zeros_like(Tm)

    inv.defvjp(fwd, bwd)
    return inv(A, A if known is None else known)


def _chunk_fn(q, k, v, gb, bb, S, inverse=None, with_inverse=False):
    nn, nt, tn = _make_bdots()
    G, C, Dh = q.shape
    hdot = functools.partial(jnp.einsum, precision=lax.Precision.HIGH, preferred_element_type=F32)
    q = q * lax.rsqrt(jnp.sum(q * q, axis=-1, keepdims=True) + EPS) * (Dh ** -0.5)
    k = k * lax.rsqrt(jnp.sum(k * k, axis=-1, keepdims=True) + EPS)
    row = lax.broadcasted_iota(jnp.int32, (G, C, C), 1)
    col = lax.broadcasted_iota(jnp.int32, (G, C, C), 2)
    causal = row >= col
    strict = row > col
    gc = hdot(_NN, causal.astype(F32), gb)
    spread = jnp.full((G, C, Dh), 1.0 / Dh, F32)
    gi = hdot(_NT, gc, spread)
    gj = hdot(_NT, spread, gc)
    decay = jnp.where(causal, jnp.exp(jnp.where(causal, gi - gj, 0.0)), 0.0)
    kb = k * bb
    vb = v * bb
    A = jnp.where(strict, nt(kb, k) * decay, 0.0)
    Tm = _unit_lower_inverse(A, inverse)
    eg = jnp.exp(gc)
    u = nn(Tm, vb)
    w = nn(Tm, kb * eg)
    qg = q * eg
    intra = nt(q, k) * decay
    glast = hdot(_NN, jnp.ones((G, C, C), F32), gb)
    kd = k * jnp.exp(glast - gc)
    v_new = u - nn(w, S)
    o = nn(qg, S) + nn(intra, v_new)
    egl = jnp.exp(glast)
    S_new = S * jnp.concatenate([egl] * (Dh // C), axis=1) + tn(kd, v_new)
    return (o, S_new, Tm) if with_inverse else (o, S_new)


def dn_chunk_fwd(q, k, v, gb, bb):
    B, H, T, Dh = q.shape
    NC = T // CHUNK
    NS = _tile(NC, CHUNKS_PER_STEP, 1)

    def body(q_ref, k_ref, v_ref, gb_ref, bb_ref, o_ref, sp_ref, inv_ref, S_s):
        @pl.when(pl.program_id(1) == 0)
        def _():
            S_s[...] = jnp.zeros_like(S_s)

        def one_chunk(j, carry):
            rows = pl.ds(pl.multiple_of(j * CHUNK, CHUNK), CHUNK)
            S = S_s[...]
            sp_ref[j] = S
            o, S_new, Tm = _chunk_fn(q_ref[:, rows, :], k_ref[:, rows, :], v_ref[:, rows, :], gb_ref[:, rows, :],
                                     bb_ref[:, rows, :], S, with_inverse=True)
            o_ref[:, rows, :] = o
            inv_ref[j] = Tm
            S_s[...] = S_new
            return carry

        lax.fori_loop(0, NS, one_chunk, 0)

    hm = pl.BlockSpec((None, H, NS * CHUNK, Dh), lambda b, n: (b, 0, n, 0))
    return pl.pallas_call(
        body, name="dn_chunk_fwd", grid=(B, NC // NS),
        in_specs=[hm] * 5,
        out_specs=[hm, pl.BlockSpec((None, NS, H, Dh, Dh), lambda b, n: (b, n, 0, 0, 0)),
                   pl.BlockSpec((None, NS, H, CHUNK, CHUNK), lambda b, n: (b, n, 0, 0, 0))],
        out_shape=[jax.ShapeDtypeStruct((B, H, T, Dh), F32), jax.ShapeDtypeStruct((B, NC, H, Dh, Dh), F32),
                   jax.ShapeDtypeStruct((B, NC, H, CHUNK, CHUNK), F32)],
        scratch_shapes=[pltpu.VMEM((H, Dh, Dh), F32)],
        compiler_params=_cparams(2),
    )(q, k, v, gb, bb)


def dn_chunk_bwd(q, k, v, gb, bb, s_prev, inv, do):
    B, H, T, Dh = q.shape
    NC = T // CHUNK
    NS = _tile(NC, CHUNKS_PER_STEP, 1)
    NG = NC // NS

    def body(q_ref, k_ref, v_ref, gb_ref, bb_ref, sp_ref, inv_ref, do_ref, dq_ref, dk_ref, dv_ref, dgb_ref, dbb_ref,
             dS_s):
        @pl.when(pl.program_id(1) == 0)
        def _():
            dS_s[...] = jnp.zeros_like(dS_s)

        def one_chunk(jj, carry):
            j = NS - 1 - jj
            rows = pl.ds(pl.multiple_of(j * CHUNK, CHUNK), CHUNK)
            _, vjp = jax.vjp(functools.partial(_chunk_fn, inverse=inv_ref[j]), q_ref[:, rows, :], k_ref[:, rows, :],
                             v_ref[:, rows, :], gb_ref[:, rows, :], bb_ref[:, rows, :], sp_ref[j])
            dq, dk, dv, dgb, dbb, dS = vjp((do_ref[:, rows, :], dS_s[...]))
            dq_ref[:, rows, :] = dq
            dk_ref[:, rows, :] = dk
            dv_ref[:, rows, :] = dv
            dgb_ref[:, rows, :] = dgb
            dbb_ref[:, rows, :] = dbb
            dS_s[...] = dS
            return carry

        lax.fori_loop(0, NS, one_chunk, 0)

    hm = pl.BlockSpec((None, H, NS * CHUNK, Dh), lambda b, n: (b, 0, NG - 1 - n, 0))
    return pl.pallas_call(
        body, name="dn_chunk_bwd", grid=(B, NG),
        in_specs=[hm] * 5 + [pl.BlockSpec((None, NS, H, Dh, Dh), lambda b, n: (b, NG - 1 - n, 0, 0, 0)),
                             pl.BlockSpec((None, NS, H, CHUNK, CHUNK), lambda b, n: (b, NG - 1 - n, 0, 0, 0)), hm],
        out_specs=[hm] * 5, out_shape=[jax.ShapeDtypeStruct((B, H, T, Dh), F32)] * 5,
        scratch_shapes=[pltpu.VMEM((H, Dh, Dh), F32)],
        compiler_params=_cparams(2),
    )(q, k, v, gb, bb, s_prev, inv, do)


def _head_norm(o, og):
    r = lax.rsqrt(jnp.mean(o * o, axis=-1, keepdims=True) + EPS)
    return o * r, r


def dn_out_fwd(x, o, z, mod3, o_g, w_out):
    B, T, D = x.shape
    _, H, _, Dh = o.shape
    W = H * Dh
    tm = _tile(T, TOKENS_PER_STEP)

    def body(x_ref, o_ref, z_ref, mod_ref, og_ref, w_ref, xo_ref, y_ref):
        parts = []
        for h in range(H):
            on, _ = _head_norm(o_ref[h], og_ref[...])
            zz = z_ref[:, h * Dh:(h + 1) * Dh]
            parts.append((on * og_ref[...] * (zz * _sigmoid(zz))).astype(BF16))
        y = _mm(jnp.concatenate(parts, axis=1), w_ref[...])
        y_ref[...] = y
        xo_ref[...] = x_ref[...] + (1.0 + mod_ref[2:3, :]) * y

    tok = pl.BlockSpec((None, tm, D), lambda b, t: (b, t, 0))
    return pl.pallas_call(
        body, name="dn_out_fwd", grid=(B, T // tm),
        in_specs=[tok, pl.BlockSpec((None, H, tm, Dh), lambda b, t: (b, 0, t, 0)),
                  pl.BlockSpec((None, tm, W), lambda b, t: (b, t, 0)), pl.BlockSpec((None, 3, D), lambda b, t: (b, 0, 0)),
                  pl.BlockSpec((1, Dh), lambda b, t: (0, 0)), pl.BlockSpec((W, D), lambda b, t: (0, 0))],
        out_specs=[tok, tok], out_shape=[jax.ShapeDtypeStruct((B, T, D), F32)] * 2,
        compiler_params=_cparams(2),
    )(x, o, z, mod3, o_g, w_out)


def dn_out_bwd(dres, y, o, z, mod3, o_g, w_out):
    B, T, D = dres.shape
    _, H, _, Dh = o.shape
    W = H * Dh
    tm = _tile(T, TOKENS_PER_STEP)

    def body(dres_ref, y_ref, o_ref, z_ref, mod_ref, og_ref, w_ref, do_ref, dz_ref, ogb_ref, dy_ref, dgate_ref, dog_ref):
        t = pl.program_id(1)

        @pl.when(t == 0)
        def _():
            dgate_ref[...] = jnp.zeros_like(dgate_ref)
            dog_ref[...] = jnp.zeros_like(dog_ref)

        dres = dres_ref[...]
        dy = ((1.0 + mod_ref[2:3, :]) * dres).astype(BF16)
        dy_ref[...] = dy
        dgate_ref[...] += _sum0(dres * y_ref[...])
        dog = _mm_nt(dy, w_ref[...])
        og = og_ref[...]
        for h in range(H):
            ov = o_ref[h]
            xn, r = _head_norm(ov, og)
            zz = z_ref[:, h * Dh:(h + 1) * Dh]
            sg = _sigmoid(zz)
            sz = zz * sg
            d = dog[:, h * Dh:(h + 1) * Dh]
            ogb_ref[:, h * Dh:(h + 1) * Dh] = (xn * og * sz).astype(BF16)
            dz_ref[:, h * Dh:(h + 1) * Dh] = d * (xn * og) * _dsilu(zz, sg)
            don = d * sz
            dog_ref[...] += _sum0(don * xn)
            dxn = don * og
            do_ref[h] = r * (dxn - xn * jnp.mean(dxn * xn, axis=-1, keepdims=True))

    tok = pl.BlockSpec((None, tm, D), lambda b, t: (b, t, 0))
    tokw = pl.BlockSpec((None, tm, W), lambda b, t: (b, t, 0))
    hm = pl.BlockSpec((None, H, tm, Dh), lambda b, t: (b, 0, t, 0))
    return pl.pallas_call(
        body, name="dn_out_bwd", grid=(B, T // tm),
        in_specs=[tok, tok, hm, tokw, pl.BlockSpec((None, 3, D), lambda b, t: (b, 0, 0)),
                  pl.BlockSpec((1, Dh), lambda b, t: (0, 0)), pl.BlockSpec((W, D), lambda b, t: (0, 0))],
        out_specs=[hm, tokw, tokw, tok, pl.BlockSpec((None, 1, D), lambda b, t: (b, 0, 0)),
                   pl.BlockSpec((None, 1, Dh), lambda b, t: (b, 0, 0))],
        out_shape=[jax.ShapeDtypeStruct((B, H, T, Dh), F32), jax.ShapeDtypeStruct((B, T, W), F32),
                   jax.ShapeDtypeStruct((B, T, W), BF16), jax.ShapeDtypeStruct((B, T, D), BF16),
                   jax.ShapeDtypeStruct((B, 1, D), F32), jax.ShapeDtypeStruct((B, 1, Dh), F32)],
        compiler_params=_cparams(2),
    )(dres, y, o, z, mod3, o_g, w_out)


def dn_conv_bwd(dq, dk, dv, dgb, dbb, pre, ab, w_sconv, alog_row, dt_row):
    B, H, T, Dh = dq.shape
    W = H * Dh
    W3 = 3 * W
    K = w_sconv.shape[0]
    tm = _tile(T, TOKENS_PER_STEP_WIDE)

    def body(dq_ref, dk_ref, dv_ref, dgb_ref, dbb_ref, pre_ref, halo_ref, ab_ref, w_ref, alog_ref, dt_ref,
             dc_ref, dab_ref, small_ref, ext_s):
        t = pl.program_id(1)

        @pl.when(t == 0)
        def _():
            small_ref[...] = jnp.zeros_like(small_ref)

        ext_s[0:SCONV_HALO, :] = jnp.where(t > 0, halo_ref[...], 0.0)
        ext_s[SCONV_HALO:, :] = pre_ref[...]
        cv = _sconv(ext_s, w_ref, tm, K)
        dsl = _dsilu(cv, _sigmoid(cv))
        ab = ab_ref[...]
        lane = lax.broadcasted_iota(jnp.int32, ab.shape, 1)
        dg_all = jnp.zeros_like(ab)
        db_all = jnp.zeros_like(ab)
        for h in range(H):
            dc_ref[:, h * Dh:(h + 1) * Dh] = dq_ref[h] * dsl[:, h * Dh:(h + 1) * Dh]
            dc_ref[:, W + h * Dh:W + (h + 1) * Dh] = dk_ref[h] * dsl[:, W + h * Dh:W + (h + 1) * Dh]
            dc_ref[:, 2 * W + h * Dh:2 * W + (h + 1) * Dh] = dv_ref[h] * dsl[:, 2 * W + h * Dh:2 * W + (h + 1) * Dh]
            dg_all = dg_all + jnp.where(lane == h, jnp.sum(dgb_ref[h], axis=1, keepdims=True), 0.0)
            db_all = db_all + jnp.where(lane == H + h, jnp.sum(dbb_ref[h], axis=1, keepdims=True), 0.0)
        xa = ab + dt_ref[...]
        ea = -jnp.exp(alog_ref[...])
        g_all = ea * _softplus(xa)
        da = dg_all * ea * _sigmoid(xa)
        beta = _sigmoid(ab)
        dab_ref[...] = da + db_all * beta * (1.0 - beta)
        small_ref[0:1, :] += _sum0(dg_all * g_all)
        small_ref[1:2, :] += _sum0(da)

    hm = pl.BlockSpec((None, H, tm, Dh), lambda b, t: (b, 0, t, 0))
    row = pl.BlockSpec((1, LANES), lambda b, t: (0, 0))
    return pl.pallas_call(
        body, name="dn_conv_bwd", grid=(B, T // tm),
        in_specs=[hm] * 5 + [pl.BlockSpec((None, tm, W3), lambda b, t: (b, t, 0)), _past_halo_spec(tm, SCONV_HALO, W3),
                             pl.BlockSpec((None, tm, LANES), lambda b, t: (b, t, 0)),
                             pl.BlockSpec((K, W3), lambda b, t: (0, 0)), row, row],
        out_specs=[pl.BlockSpec((None, tm, W3), lambda b, t: (b, t, 0)), pl.BlockSpec((None, tm, LANES), lambda b, t: (b, t, 0)),
                   pl.BlockSpec((None, 2, LANES), lambda b, t: (b, 0, 0))],
        out_shape=[jax.ShapeDtypeStruct((B, T, W3), F32), jax.ShapeDtypeStruct((B, T, LANES), F32),
                   jax.ShapeDtypeStruct((B, 2, LANES), F32)],
        scratch_shapes=[pltpu.VMEM((tm + SCONV_HALO, W3), F32)],
        compiler_params=_cparams(2),
    )(dq, dk, dv, dgb, dbb, pre, pre, ab, w_sconv, alog_row, dt_row)


def dn_proj_bwd(x, dres, dc, pre, dz, dab, mod3, g, w_main, w_ab, w_sconv):
    B, T, D = x.shape
    W3 = dc.shape[2]
    W = W3 // 3
    K = w_sconv.shape[0]
    tm = _tile(T, TOKENS_PER_STEP_WIDE)
    nt = T // tm

    def body(x_ref, dres_ref, dc_ref, dch_ref, pre_ref, preh_ref, dz_ref, dab_ref, mod_ref, g_ref, wm_ref, wab_ref, ws_ref,
             dx_ref, h_ref, dproj_ref, dws_ref, dmod_ref, dg_ref, extp_s, extd_s):
        t = pl.program_id(1)

        @pl.when(t == 0)
        def _():
            dws_ref[...] = jnp.zeros_like(dws_ref)
            dmod_ref[...] = jnp.zeros_like(dmod_ref)
            dg_ref[...] = jnp.zeros_like(dg_ref)

        dc = dc_ref[...]
        extp_s[0:SCONV_HALO, :] = jnp.where(t > 0, preh_ref[...], 0.0)
        extp_s[SCONV_HALO:, :] = pre_ref[...]
        extd_s[0:tm, :] = dc
        extd_s[tm:, :] = jnp.where(t < nt - 1, dch_ref[...], 0.0)
        dpre = jnp.zeros((tm, W3), F32)
        for k in range(K):
            dpre = dpre + ws_ref[k:k + 1, :] * extd_s[pl.ds(K - 1 - k, tm), :]
            dws_ref[k:k + 1, :] += _sum0(dc * extp_s[pl.ds(SCONV_HALO - (K - 1) + k, tm), :])
        dpre = dpre.astype(BF16)
        dzb = dz_ref[...].astype(BF16)
        dproj_ref[:, 0:W3] = dpre
        dproj_ref[:, W3:] = dzb
        dh = _mm_nt(dab_ref[...], wab_ref[...]) + _mm_nt(dzb, wm_ref[:, W3:])
        for p in range(3):
            dh = dh + _mm_nt(dpre[:, p * W:(p + 1) * W], wm_ref[:, p * W:(p + 1) * W])
        xv = x_ref[...]
        h_ref[...] = _modnorm(xv, g_ref[...], mod_ref[1:2, :], mod_ref[0:1, :]).astype(BF16)
        dxn, dg, dscale, dshift = _modnorm_bwd(xv, g_ref[...], mod_ref[1:2, :], dh)
        dx_ref[...] = dres_ref[...] + dxn
        dmod_ref[0:1, :] += dshift
        dmod_ref[1:2, :] += dscale
        dg_ref[...] += dg

    tok = pl.BlockSpec((None, tm, D), lambda b, t: (b, t, 0))
    tok3 = pl.BlockSpec((None, tm, W3), lambda b, t: (b, t, 0))
    return pl.pallas_call(
        body, name="dn_proj_bwd", grid=(B, nt),
        in_specs=[tok, tok, tok3, _future_halo_spec(tm, SCONV_HALO, W3, T), tok3, _past_halo_spec(tm, SCONV_HALO, W3),
                  pl.BlockSpec((None, tm, W), lambda b, t: (b, t, 0)), pl.BlockSpec((None, tm, LANES), lambda b, t: (b, t, 0)),
                  pl.BlockSpec((None, 3, D), lambda b, t: (b, 0, 0)), pl.BlockSpec((1, D), lambda b, t: (0, 0)),
                  pl.BlockSpec((D, 4 * W), lambda b, t: (0, 0)), pl.BlockSpec((D, LANES), lambda b, t: (0, 0)),
                  pl.BlockSpec((K, W3), lambda b, t: (0, 0))],
        out_specs=[tok, tok, pl.BlockSpec((None, tm, 4 * W), lambda b, t: (b, t, 0)),
                   pl.BlockSpec((None, K, W3), lambda b, t: (b, 0, 0)), pl.BlockSpec((None, 3, D), lambda b, t: (b, 0, 0)),
                   pl.BlockSpec((None, 1, D), lambda b, t: (b, 0, 0))],
        out_shape=[jax.ShapeDtypeStruct((B, T, D), F32), jax.ShapeDtypeStruct((B, T, D), BF16),
                   jax.ShapeDtypeStruct((B, T, 4 * W), BF16), jax.ShapeDtypeStruct((B, K, W3), F32),
                   jax.ShapeDtypeStruct((B, 3, D), F32), jax.ShapeDtypeStruct((B, 1, D), F32)],
        scratch_shapes=[pltpu.VMEM((tm + SCONV_HALO, W3), F32), pltpu.VMEM((tm + SCONV_HALO, W3), F32)],
        compiler_params=_cparams(2),
    )(x, dres, dc, dc, pre, pre, dz, dab, mod3, g, w_main, w_ab, w_sconv)


def ada_fwd(c_all, w_ada, b_cols):
    L, D, Ca = w_ada.shape
    NB = c_all.shape[0]

    def body(c_ref, w_ref, b_ref, o_ref):
        cv = c_ref[...]
        o_ref[...] = _mm(cv * _sigmoid(cv), w_ref[...]) + b_ref[...]

    return pl.pallas_call(
        body, name="ada_fwd", grid=(L,),
        in_specs=[pl.BlockSpec((NB, D), lambda i: (0, 0)), pl.BlockSpec((None, D, Ca), lambda i: (i, 0, 0)),
                  pl.BlockSpec((None, 1, Ca), lambda i: (i, 0, 0))],
        out_specs=pl.BlockSpec((None, NB, Ca), lambda i: (i, 0, 0)),
        out_shape=jax.ShapeDtypeStruct((L, NB, Ca), F32),
        compiler_params=_cparams(1),
    )(c_all, w_ada, b_cols)


def ada_bwd(c_all, dmod_cols, dmod_all):
    L, NB, Ca = dmod_cols.shape
    D = c_all.shape[1]
    C9 = dmod_all.shape[2]

    def body(c_ref, dc_ref, da_ref, gw_ref, gb_ref):
        cv = c_ref[...]
        gw_ref[...] = _mm_tn(cv * _sigmoid(cv), dc_ref[...])
        gb_ref[...] = _sum0(da_ref[...])

    return pl.pallas_call(
        body, name="ada_bwd", grid=(L,),
        in_specs=[pl.BlockSpec((NB, D), lambda i: (0, 0)), pl.BlockSpec((None, NB, Ca), lambda i: (i, 0, 0)),
                  pl.BlockSpec((None, NB, C9), lambda i: (i, 0, 0))],
        out_specs=[pl.BlockSpec((None, D, Ca), lambda i: (i, 0, 0)), pl.BlockSpec((None, 1, C9), lambda i: (i, 0, 0))],
        out_shape=[jax.ShapeDtypeStruct((L, D, Ca), F32), jax.ShapeDtypeStruct((L, 1, C9), F32)],
        compiler_params=_cparams(1),
    )(c_all, dmod_cols, dmod_all)


def adamw(w, g, m, v, name, token=None):
    R, C = w.shape
    tr = _tile(R, max(8, ELEMENTWISE_BLOCK // C))
    if token is None:
        token = jnp.zeros((8, LANES), F32)

    def body(w_ref, g_ref, m_ref, v_ref, t_ref, d_ref, mo_ref, vo_ref):
        gv = g_ref[...] + t_ref[0:1, 0:1]
        mn = ADAM_B1 * m_ref[...] + (1.0 - ADAM_B1) * gv
        vn = ADAM_B2 * v_ref[...] + (1.0 - ADAM_B2) * (gv * gv)
        m_hat = mn / (1.0 - ADAM_B1 ** ADAM_STEP)
        v_hat = vn / (1.0 - ADAM_B2 ** ADAM_STEP)
        d_ref[...] = -ADAM_LR * (m_hat / (jnp.sqrt(v_hat) + ADAM_EPS) + ADAM_WD * w_ref[...])
        mo_ref[...] = mn
        vo_ref[...] = vn

    blk = pl.BlockSpec((tr, C), lambda i: (i, 0))
    return pl.pallas_call(
        body, name=name, grid=(R // tr,), in_specs=[blk] * 4 + [pl.BlockSpec((8, LANES), lambda i: (0, 0))],
        out_specs=[blk] * 3, out_shape=[jax.ShapeDtypeStruct((R, C), F32)] * 3, compiler_params=_cparams(1),
    )(w, g, m, v, token)


def sum_devices(a):
    n, R, C = a.shape

    def body(a_ref, o_ref):
        s = a_ref[0]
        for d in range(1, n):
            s = s + a_ref[d]
        o_ref[...] = s

    return pl.pallas_call(
        body, name="sum_devices", out_shape=jax.ShapeDtypeStruct((R, C), F32),
        compiler_params=pltpu.CompilerParams(vmem_limit_bytes=VMEM_LIMIT_V7X),
    )(a)


def _place():
    x, y, c = lax.axis_index("x"), lax.axis_index("y"), lax.axis_index("c")
    return x, y, c


def _other_chips(x, y):
    return [(2 * (1 - x) + y, 1 - x, y), (2 * x + (1 - y), x, 1 - y), (2 * (1 - x) + (1 - y), 1 - x, 1 - y)]


def allgather8(block):
    m_per, n = block.shape

    def body(x_ref, out_ref, send_sems, recv_sems, local_sem):
        x, y, c = _place()
        me, sibling = (x, y, c), (x, y, 1 - c)
        chips = [(1 - x, y), (x, 1 - y), (1 - x, 1 - y)]

        def rows(px, py, pc):
            return out_ref.at[pl.ds((4 * px + 2 * py + pc) * m_per, m_per), :]

        def copy(k, blk, to, src=None):
            return pltpu.make_async_remote_copy(
                src_ref=rows(*blk) if src is None else src, dst_ref=rows(*blk),
                send_sem=send_sems.at[k], recv_sem=recv_sems.at[k], device_id=to, device_id_type=MESH)

        mine = pltpu.make_async_copy(x_ref, rows(*me), local_sem)
        mine.start()
        first = [copy(0, me, sibling, src=x_ref)]
        first += [copy(1 + j, me, (*chip, c), src=x_ref) for j, chip in enumerate(chips)]
        for cp in first:
            cp.start()
        passed = [copy(4 + j, (*chip, c), sibling) for j, chip in enumerate(chips)]
        for j, chip in enumerate(chips):
            copy(1 + j, (*chip, c), me).wait_recv()
            passed[j].start()
        copy(0, sibling, me).wait_recv()
        for j, chip in enumerate(chips):
            copy(4 + j, (*chip, 1 - c), me).wait_recv()
        for cp in first + passed:
            cp.wait_send()
        mine.wait()

    return pl.pallas_call(
        body, name="allgather8", out_shape=jax.ShapeDtypeStruct((N_DEV * m_per, n), block.dtype),
        in_specs=[pl.BlockSpec(memory_space=pltpu.VMEM)], out_specs=pl.BlockSpec(memory_space=pltpu.VMEM),
        scratch_shapes=[pltpu.SemaphoreType.DMA((7,)), pltpu.SemaphoreType.DMA((7,)), pltpu.SemaphoreType.DMA],
        compiler_params=pltpu.CompilerParams(vmem_limit_bytes=VMEM_LIMIT_V7X),
    )(block)


def _half(ref, c, rh):
    return ref.at[pl.ds(pl.multiple_of(c * rh, 16), rh), :]


def pair_exchange(grads):
    K = len(grads)

    def body(*refs):
        ins, outs = refs[:K], refs[K:2 * K]
        send_sems, recv_sems = refs[2 * K:]
        x, y, c = _place()
        sibling = (x, y, 1 - c)
        copies = []
        for k in range(K):
            n, r, _ = ins[k].shape
            rh = r // 2
            cp = pltpu.make_async_remote_copy(
                src_ref=ins[k].at[:, pl.ds(pl.multiple_of((1 - c) * rh, 16), rh), :], dst_ref=outs[k],
                send_sem=send_sems.at[k], recv_sem=recv_sems.at[k], device_id=sibling, device_id_type=MESH)
            cp.start()
            copies.append(cp)
        for cp in copies:
            cp.wait_recv()
        for cp in copies:
            cp.wait_send()

    return pl.pallas_call(
        body, name="pair_exchange",
        out_shape=[jax.ShapeDtypeStruct((g.shape[0], g.shape[1] // 2, g.shape[2]), g.dtype) for g in grads],
        in_specs=[HBM_SPEC] * K, out_specs=[HBM_SPEC] * K,
        scratch_shapes=[pltpu.SemaphoreType.DMA((K,))] * 2,
    )(*grads)


def pair_add(grad, recv, c_idx):
    n, r, C = grad.shape
    rh = r // 2
    tr = _tile(rh, max(16, ELEMENTWISE_BLOCK // C), 16)
    grad = grad.reshape(n, 2, rh, C)

    def body(c_ref, g_ref, r_ref, o_ref):
        o_ref[...] = (g_ref[...].astype(F32) + r_ref[...].astype(F32)).astype(BF16)

    return pl.pallas_call(
        body, name="pair_add",
        grid_spec=pltpu.PrefetchScalarGridSpec(
            num_scalar_prefetch=1, grid=(n, rh // tr),
            in_specs=[pl.BlockSpec((None, None, tr, C), lambda d, i, c_ref: (d, c_ref[0], i, 0)),
                      pl.BlockSpec((None, tr, C), lambda d, i, c_ref: (d, i, 0))],
            out_specs=pl.BlockSpec((None, tr, C), lambda d, i, c_ref: (d, i, 0))),
        out_shape=jax.ShapeDtypeStruct((n, rh, C), BF16), compiler_params=_cparams(2),
    )(c_idx, grad, recv)


def chip_sum(parts, got, where, stack, slot):
    _, rh, C = parts.shape
    tr = _tile(rh, max(16, ELEMENTWISE_BLOCK // C), 16)
    nt = rh // tr

    def body(w_ref, p_ref, g_ref, stack_any, o_ref):
        s = p_ref[...].astype(F32)
        for r in range(3):
            s = s + g_ref[r].astype(F32)
        o_ref[...] = s

    return pl.pallas_call(
        body, name="chip_sum",
        grid_spec=pltpu.PrefetchScalarGridSpec(
            num_scalar_prefetch=1, grid=(nt,),
            in_specs=[pl.BlockSpec((None, tr, C), lambda i, w_ref: (w_ref[0], i, 0)),
                      pl.BlockSpec((3, tr, C), lambda i, w_ref: (0, i, 0)),
                      pl.BlockSpec(memory_space=pl.ANY)],
            out_specs=pl.BlockSpec((None, tr, C), lambda i, w_ref: (slot, w_ref[1] * nt + i, 0))),
        out_shape=jax.ShapeDtypeStruct(stack.shape, F32), input_output_aliases={3: 0},
        compiler_params=_cparams(1),
    )(where, parts, got, stack)


def pair_share(stacks, slots):
    K = len(stacks)
    jobs = [(k, s) for k in range(K) for s in slots[k]]

    def body(*refs):
        ins, outs = refs[:K], refs[K:2 * K]
        send_sems, recv_sems = refs[2 * K:]
        x, y, c = _place()
        sibling = (x, y, 1 - c)
        started = []
        for n, (k, s) in enumerate(jobs):
            rh = ins[k].shape[1] // 2
            cp = pltpu.make_async_remote_copy(
                src_ref=_half(ins[k].at[s], c, rh), dst_ref=_half(outs[k].at[s], c, rh), send_sem=send_sems.at[n],
                recv_sem=recv_sems.at[n], device_id=sibling, device_id_type=MESH)
            cp.start()
            started.append(cp)
        for n, (k, s) in enumerate(jobs):
            rh = ins[k].shape[1] // 2
            theirs = _half(outs[k].at[s], 1 - c, rh)
            pltpu.make_async_remote_copy(
                src_ref=theirs, dst_ref=theirs, send_sem=send_sems.at[n], recv_sem=recv_sems.at[n],
                device_id=sibling, device_id_type=MESH).wait_recv()
        for cp in started:
            cp.wait_send()

    return pl.pallas_call(
        body, name="pair_share",
        out_shape=[jax.ShapeDtypeStruct(s.shape, s.dtype) for s in stacks],
        in_specs=[HBM_SPEC] * K, out_specs=[HBM_SPEC] * K, input_output_aliases={k: k for k in range(K)},
        scratch_shapes=[pltpu.SemaphoreType.DMA((len(jobs),))] * 2,
    )(*stacks)


SEM_SPEC = pl.BlockSpec(memory_space=pltpu.SEMAPHORE)
ANY_SPEC = pl.BlockSpec(memory_space=pl.ANY)
DATAFLOW = pltpu.SideEffectType.DATAFLOW_SIDE_EFFECTING


def _in_hbm(a):
    return pltpu.with_memory_space_constraint(a, pltpu.HBM)


def _ici_copies(srcs, dsts, send_sems, recv_sems, src_slice, dst_slice):
    x, y, c = _place()
    out = []
    for k in range(len(srcs)):
        for r, (pchip, px, py) in enumerate(_other_chips(x, y)):
            out.append(pltpu.make_async_remote_copy(
                src_ref=src_slice(srcs[k], r, pchip), dst_ref=dst_slice(dsts[k], r, pchip),
                send_sem=send_sems.at[3 * k + r], recv_sem=recv_sems.at[3 * k + r], device_id=(px, py, c),
                device_id_type=MESH))
    return out


def _pair_copies(srcs, dsts, send_sems, recv_sems, src_slice, dst_slice):
    x, y, c = _place()
    return [pltpu.make_async_remote_copy(
        src_ref=src_slice(srcs[k]), dst_ref=dst_slice(dsts[k]), send_sem=send_sems.at[k], recv_sem=recv_sems.at[k],
        device_id=(x, y, 1 - c), device_id_type=MESH) for k in range(len(srcs))]


def _exchange_start(bufs, lands, src_slice, dst_slice, name, after=None, copies=_ici_copies, per=3):
    K = len(bufs)
    same = lands is None
    n_thru = K if same else 2 * K
    n_in = n_thru + (after is not None)

    def body(*refs):
        ins = refs[:n_thru]
        send_sems, recv_sems = refs[n_in], refs[n_in + 1]
        token = refs[-1]
        srcs = ins[:K]
        dsts = srcs if same else ins[K:]
        for cp in copies(srcs, dsts, send_sems, recv_sems, src_slice, dst_slice):
            cp.start()
        token[...] = jnp.zeros_like(token)

    thru = list(bufs) + ([] if same else list(lands))
    res = pl.pallas_call(
        body, name=name,
        out_shape=[pltpu.SemaphoreType.DMA((per * K,)), pltpu.SemaphoreType.DMA((per * K,))]
        + [pltpu.HBM(a.shape, a.dtype) for a in thru] + [jax.ShapeDtypeStruct((8, LANES), F32)],
        in_specs=[HBM_SPEC] * n_thru + [ANY_SPEC] * (after is not None),
        out_specs=[SEM_SPEC, SEM_SPEC] + [HBM_SPEC] * n_thru + [pl.BlockSpec(memory_space=pltpu.VMEM)],
        input_output_aliases={i: 2 + i for i in range(n_thru)},
        compiler_params=pltpu.CompilerParams(has_side_effects=DATAFLOW),
    )(*[_in_hbm(a) for a in thru], *([] if after is None else [after]))
    return res[0], res[1], res[2:2 + K], (res[2:2 + K] if same else res[2 + K:2 + 2 * K]), res[-1]


def _exchange_wait(send_sems, recv_sems, bufs, lands, after, src_slice, dst_slice, name, copies=_ici_copies):
    K = len(bufs)
    same = lands is None
    n_thru = K if same else 2 * K

    def body(*refs):
        ins = refs[:n_thru]
        ssem, rsem = refs[n_thru], refs[n_thru + 1]
        srcs = ins[:K]
        dsts = srcs if same else ins[K:]
        started = copies(srcs, dsts, ssem, rsem, src_slice, dst_slice)
        for cp in started:
            cp.wait_send()
        for cp in started:
            cp.wait_recv()

    thru = list(bufs) + ([] if same else list(lands))
    res = pl.pallas_call(
        body, name=name,
        out_shape=[pltpu.HBM(a.shape, a.dtype) for a in thru],
        in_specs=[HBM_SPEC] * n_thru + [SEM_SPEC, SEM_SPEC, ANY_SPEC],
        out_specs=[HBM_SPEC] * n_thru,
        input_output_aliases={i: i for i in range(n_thru)},
        compiler_params=pltpu.CompilerParams(has_side_effects=DATAFLOW),
    )(*thru, send_sems, recv_sems, after)
    return res[:K], (res[:K] if same else res[K:])


def _own_half(ref, r, pchip):
    x, y, c = _place()
    return _half(ref.at[2 * x + y], c, ref.shape[1] // 2)


def _their_half(ref, r, pchip):
    _, _, c = _place()
    return _half(ref.at[pchip], c, ref.shape[1] // 2)


def gather_start(lands, name, after=None):
    return _exchange_start(lands, None, _own_half, _own_half, name, after)


def gather_wait(handle, after, name):
    ssem, rsem, lands, _, _ = handle
    return _exchange_wait(ssem, rsem, lands, None, after, _own_half, _their_half, name)[1]


def pair_forward(lands):
    K = len(lands)

    def body(*refs):
        ins, outs = refs[:K], refs[K:2 * K]
        send_sems, recv_sems = refs[2 * K:]
        x, y, c = _place()
        sibling = (x, y, 1 - c)
        started = []
        for k in range(K):
            rh = ins[k].shape[1] // 2
            for r, (pchip, _, _) in enumerate(_other_chips(x, y)):
                cp = pltpu.make_async_remote_copy(
                    src_ref=_half(ins[k].at[pchip], c, rh), dst_ref=_half(outs[k].at[pchip], c, rh),
                    send_sem=send_sems.at[k, r], recv_sem=recv_sems.at[k, r], device_id=sibling, device_id_type=MESH)
                cp.start()
                started.append(cp)
        for k in range(K):
            rh = ins[k].shape[1] // 2
            for r, (pchip, _, _) in enumerate(_other_chips(x, y)):
                theirs = _half(outs[k].at[pchip], 1 - c, rh)
                pltpu.make_async_remote_copy(
                    src_ref=theirs, dst_ref=theirs, send_sem=send_sems.at[k, r], recv_sem=recv_sems.at[k, r],
                    device_id=sibling, device_id_type=MESH).wait_recv()
        for cp in started:
            cp.wait_send()

    return pl.pallas_call(
        body, name="pair_forward",
        out_shape=[jax.ShapeDtypeStruct(s.shape, s.dtype) for s in lands],
        in_specs=[HBM_SPEC] * K, out_specs=[HBM_SPEC] * K, input_output_aliases={k: k for k in range(K)},
        scratch_shapes=[pltpu.SemaphoreType.DMA((K, 3))] * 2,
    )(*lands)


def _to_chip(ref, r, pchip):
    return ref.at[pchip]


def _from_relation(ref, r, pchip):
    return ref.at[r]


def _other_rows(ref):
    _, _, c = _place()
    rh = ref.shape[1] // 2
    return ref.at[:, pl.ds(pl.multiple_of((1 - c) * rh, 16), rh), :]


def _whole(ref):
    return ref


def pair_start(grads, name):
    lands = [lax.empty((g.shape[0], g.shape[1] // 2, g.shape[2]), g.dtype) for g in grads]
    return _exchange_start(grads, lands, _other_rows, _whole, name, copies=_pair_copies, per=1)


def pair_finish(handle, after, name):
    ssem, rsem, grads, lands, _ = handle
    return _exchange_wait(ssem, rsem, grads, lands, after, _other_rows, _whole, name, copies=_pair_copies)


def reduce_start(grads, c_idx, name, after=None, recv=None):
    if recv is None:
        recv = pair_exchange(grads)
    parts = [pair_add(g, r, c_idx) for g, r in zip(grads, recv)]
    lands = [lax.empty((3,) + p.shape[1:], p.dtype) for p in parts]
    return _exchange_start(parts, lands, _to_chip, _from_relation, name, after)


def reduce_finish(handle, after, where, name, stacks, targets):
    ssem, rsem, parts, lands, _ = handle
    parts, got = _exchange_wait(ssem, rsem, parts, lands, after, _to_chip, _from_relation, name)
    stacks = dict(stacks)
    for p, g, (key, slot) in zip(parts, got, targets):
        stacks[key] = chip_sum(p, g, where, stacks[key], slot)
    keys = list(dict.fromkeys(key for key, _ in targets))
    shared = pair_share([stacks[k] for k in keys], [[s for key, s in targets if key == k] for k in keys])
    stacks.update(zip(keys, shared))
    return stacks


def _pack(arrs):
    flat = jnp.concatenate([a.reshape(-1).astype(F32) for a in arrs])
    pad = (-flat.shape[0]) % (8 * LANES)
    return jnp.pad(flat, (0, pad)).reshape(-1, LANES)


def _unpack(flat, shapes):
    out, off = [], 0
    for s in shapes:
        n = 1
        for d in s:
            n *= d
        out.append(flat[off:off + n].reshape(s))
        off += n
    return out


def _adamw_any(w, g, m, v, name, token=None):
    shp = w.shape
    C = shp[-1]
    d, nm, nv = adamw(w.reshape(-1, C), g.reshape(-1, C), m.reshape(-1, C), v.reshape(-1, C), name, token)
    return d.reshape(shp), nm.reshape(shp), nv.reshape(shp)


def kernel(x, c, norm_g, w_ada, b_ada, w_ffn_in, w_ffn_out, cm_w_glu, cm_b_glu, cm_w_dw, cm_b_dw, cm_ln_g, cm_ln_b, cm_w_pw, cm_b_pw, dn_w_in, dn_w_sconv, dn_a_log, dn_dt_bias, dn_o_g, dn_w_out, final_g, loss_target, m_norm_g, m_w_ada, m_b_ada, m_w_ffn_in, m_w_ffn_out, m_cm_w_glu, m_cm_b_glu, m_cm_w_dw, m_cm_b_dw, m_cm_ln_g, m_cm_ln_b, m_cm_w_pw, m_cm_b_pw, m_dn_w_in, m_dn_w_sconv, m_dn_a_log, m_dn_dt_bias, m_dn_o_g, m_dn_w_out, m_final_g, v_norm_g, v_w_ada, v_b_ada, v_w_ffn_in, v_w_ffn_out, v_cm_w_glu, v_cm_b_glu, v_cm_w_dw, v_cm_b_dw, v_cm_ln_g, v_cm_ln_b, v_cm_w_pw, v_cm_b_pw, v_dn_w_in, v_dn_w_sconv, v_dn_a_log, v_dn_dt_bias, v_dn_o_g, v_dn_w_out, v_final_g):
    weights = dict(norm_g=norm_g, w_ada=w_ada, b_ada=b_ada, w_ffn_in=w_ffn_in, w_ffn_out=w_ffn_out, cm_w_glu=cm_w_glu,
                   cm_b_glu=cm_b_glu, cm_w_dw=cm_w_dw, cm_b_dw=cm_b_dw, cm_ln_g=cm_ln_g, cm_ln_b=cm_ln_b, cm_w_pw=cm_w_pw,
                   cm_b_pw=cm_b_pw, dn_w_in=dn_w_in, dn_w_sconv=dn_w_sconv, dn_a_log=dn_a_log, dn_dt_bias=dn_dt_bias,
                   dn_o_g=dn_o_g, dn_w_out=dn_w_out, final_g=final_g)
    mom_m = dict(norm_g=m_norm_g, w_ada=m_w_ada, b_ada=m_b_ada, w_ffn_in=m_w_ffn_in, w_ffn_out=m_w_ffn_out,
                 cm_w_glu=m_cm_w_glu, cm_b_glu=m_cm_b_glu, cm_w_dw=m_cm_w_dw, cm_b_dw=m_cm_b_dw, cm_ln_g=m_cm_ln_g,
                 cm_ln_b=m_cm_ln_b, cm_w_pw=m_cm_w_pw, cm_b_pw=m_cm_b_pw, dn_w_in=m_dn_w_in, dn_w_sconv=m_dn_w_sconv,
                 dn_a_log=m_dn_a_log, dn_dt_bias=m_dn_dt_bias, dn_o_g=m_dn_o_g, dn_w_out=m_dn_w_out, final_g=m_final_g)
    mom_v = dict(norm_g=v_norm_g, w_ada=v_w_ada, b_ada=v_b_ada, w_ffn_in=v_w_ffn_in, w_ffn_out=v_w_ffn_out,
                 cm_w_glu=v_cm_w_glu, cm_b_glu=v_cm_b_glu, cm_w_dw=v_cm_w_dw, cm_b_dw=v_cm_b_dw, cm_ln_g=v_cm_ln_g,
                 cm_ln_b=v_cm_ln_b, cm_w_pw=v_cm_w_pw, cm_b_pw=v_cm_b_pw, dn_w_in=v_dn_w_in, dn_w_sconv=v_dn_w_sconv,
                 dn_a_log=v_dn_a_log, dn_dt_bias=v_dn_dt_bias, dn_o_g=v_dn_o_g, dn_w_out=v_dn_w_out, final_g=v_final_g)
    names = list(weights)

    BL, T, D = x.shape
    L = norm_g.shape[0]
    NB = BL * N_DEV
    Ca = w_ada.shape[2]
    C9 = b_ada.shape[1]
    H = dn_a_log.shape[1]
    Dh = dn_o_g.shape[1]
    W = H * Dh
    KC = cm_w_dw.shape[1]
    n_cm, n_dn = cm_w_glu.shape[0], dn_w_in.shape[0]
    ax, ay, ac = lax.axis_index("x"), lax.axis_index("y"), lax.axis_index("c")
    chip = 2 * ax + ay
    dev = 2 * chip + ac
    c_idx = ac.astype(jnp.int32).reshape(1)
    where = jnp.stack([chip, ac]).astype(jnp.int32)

    def landing(s, tok=None):
        s = s if tok is None else s + tok
        return lax.dynamic_update_slice(lax.empty((N_CHIPS,) + s.shape, BF16), s.astype(BF16)[None], (chip, 0, 0))

    def layer_shards(i):
        sh = [w_ffn_in[i, 0], w_ffn_in[i, 1], w_ffn_out[i, 0], w_ffn_out[i, 1]]
        if i % 2 == 0:
            sh += [cm_w_glu[i // 2], cm_w_pw[i // 2]]
        else:
            sh += [dn_w_in[i // 2], dn_w_out[i // 2]]
        return sh

    wts = [None] * L

    small_in = [c, norm_g, cm_w_dw, dn_w_sconv]
    gathered = allgather8(_pack(small_in)).reshape(N_DEV, -1)
    per_dev = [_unpack(gathered[d], [a.shape for a in small_in]) for d in range(N_DEV)]
    c_all = jnp.concatenate([p[0] for p in per_dev], axis=0)
    norm_g_full = jnp.concatenate([per_dev[2 * s][1] for s in range(N_CHIPS)], axis=-1)
    w_dw_full = jnp.concatenate([per_dev[2 * s][2] for s in range(N_CHIPS)], axis=-1)
    w_sconv_full = jnp.concatenate([per_dev[2 * s][3] for s in range(N_CHIPS)], axis=-1)

    b_cols = lax.dynamic_slice_in_dim(b_ada, chip * Ca, Ca, axis=1).reshape(L, 1, Ca)
    mod_part = ada_fwd(c_all, w_ada, b_cols)
    mod_g = allgather8(mod_part.reshape(-1, LANES))
    shards0 = layer_shards(0)
    first = gather_start([landing(shards0[0]), landing(shards0[2])], "gather_start_0a", mod_g)
    tok0 = first[4][0, 0]
    rest = gather_start([landing(shards0[k], tok0) for k in (1, 3, 4, 5)], "gather_start_0b", first[4])
    lands = [None] + [[landing(s, tok0) for s in layer_shards(i)] for i in range(1, L)]
    mod_g = mod_g.reshape(N_DEV, L, NB, Ca)
    mod_all = jnp.concatenate([mod_g[2 * s] for s in range(N_CHIPS)], axis=-1)
    mod = lax.dynamic_slice_in_dim(mod_all, dev * BL, BL, axis=1).reshape(L, BL, 9, D)

    def dn_weights(i):
        full = jnp.transpose(wts[i][4], (1, 0, 2)).reshape(D, -1)
        return full[:, :4 * W], jnp.pad(full[:, 4 * W:], ((0, 0), (0, LANES - 2 * H)))

    def row128(v):
        return jnp.pad(v.reshape(1, -1), ((0, 0), (0, LANES - v.shape[-1])))

    def pad_taps(w):
        return jnp.pad(w, ((0, 1), (0, 0)))

    saved = []
    xs = x
    after = mod
    for i in range(L):
        tok = 0.0
        if i == 0:
            wl = wts[0] = [None] * 6
            wl[0], wl[2] = pair_forward(gather_wait(first, after, "gather_wait_0a"))
        else:
            wl = wts[i] = pair_forward(gather_wait(handle, after, "gather_wait_%d" % i))
            if i + 1 < L:
                handle = gather_start(lands[i + 1], "gather_start_%d" % (i + 1), wl[0])
                tok = handle[4][0, 0]
        sv = {}
        m3 = [mod[i, :, 3 * j:3 * j + 3] + tok for j in range(3)]
        gs = [norm_g_full[i, j].reshape(1, D) for j in range(3)]
        sv["x0"] = xs
        xs, sv["y0"], sv["h0"], sv["gu0"] = ffn_fwd(xs, m3[0], gs[0], wl[0], wl[2])
        sv["x1"] = xs
        if i == 0:
            wl[1], wl[3], wl[4], wl[5] = pair_forward(gather_wait(rest, xs, "gather_wait_0b"))
            handle = gather_start(lands[1], "gather_start_1", wl[1])
            m3 = [m + handle[4][0, 0] for m in m3]
        if i % 2 == 0:
            a = i // 2
            sv["u"] = conv_glu_fwd(xs, m3[1], gs[1], wl[4], cm_b_glu[a].reshape(1, -1))
            xs, sv["y1"], sv["u2"] = conv_out_fwd(
                xs, sv["u"], m3[1], pad_taps(w_dw_full[a]), cm_b_dw[a].reshape(1, D), cm_ln_g[a].reshape(1, D),
                cm_ln_b[a].reshape(1, D), wl[5].reshape(D, D), cm_b_pw[a].reshape(1, D))
        else:
            a = i // 2
            w_main, w_ab = dn_weights(i)
            sv["pre"], sv["z"], sv["ab"] = dn_proj_fwd(xs, m3[1], gs[1], w_main, w_ab)
            qkvgb = dn_conv_fwd(sv["pre"], sv["ab"], w_sconv_full[a], row128(dn_a_log[a]), row128(dn_dt_bias[a]), H)
            sv["qkvgb"] = qkvgb
            sv["o"], sv["sp"], sv["inv"] = dn_chunk_fwd(*qkvgb)
            xs, sv["y1"] = dn_out_fwd(xs, sv["o"], sv["z"], m3[1], dn_o_g[a].reshape(1, Dh), wl[5].reshape(W, D))
        sv["x2"] = xs
        xs, sv["y2"], sv["h2"], sv["gu2"] = ffn_fwd(xs, m3[2], gs[2], wl[1], wl[3])
        saved.append(sv)
        after = xs

    dx, d_final_g, loss_part = final_loss(xs, final_g.reshape(1, D), loss_target)

    d_norm_g = [[None] * 3 for _ in range(L)]
    dmod = [[None] * 3 for _ in range(L)]
    g_cm = {k: [None] * n_cm for k in ("b_glu", "w_dw", "b_dw", "ln_g", "ln_b", "b_pw")}
    g_dn = {k: [None] * n_dn for k in ("w_sconv", "a_log", "dt_bias", "o_g")}
    big_names = ("w_ffn_in", "w_ffn_out", "cm_w_glu", "cm_w_pw", "dn_w_in", "dn_w_out")
    stacks = {n: lax.empty((weights[n].size // (weights[n].shape[-2] * weights[n].shape[-1]),) + weights[n].shape[-2:], F32)
              for n in big_names}

    def targets(i, which):
        mix = ("cm_w_glu", "cm_w_pw") if i % 2 == 0 else ("dn_w_in", "dn_w_out")
        full = [("w_ffn_in", 2 * i), ("w_ffn_in", 2 * i + 1), ("w_ffn_out", 2 * i), ("w_ffn_out", 2 * i + 1),
                (mix[0], i // 2), (mix[1], i // 2)]
        return [full[k] for k in which]

    def ffn_back(i, j, slot, dx, tok=0.0):
        wl, sv = wts[i], saved[i]
        m3 = mod[i, :, 3 * j:3 * j + 3] + tok
        g = norm_g_full[i, j].reshape(1, D)
        gu = sv["gu%d" % j]
        ab_, dgu, dyb, dh0, dgate = ffn_bwd_part(0, dx, gu, m3, wl[slot], wl[2 + slot], y=sv["y%d" % j])
        dx, ab_, dgu, dm, dg = ffn_bwd_part(1, dx, gu, m3, wl[slot], wl[2 + slot], first=(ab_, dgu, dyb, dh0),
                                            x=sv["x%d" % j], g=g)
        dm = dm.at[:, 2:3, :].set(dgate)
        hb = sv["h%d" % j]
        dmod[i][j] = dm
        d_norm_g[i][j] = jnp.sum(dg, axis=(0, 1))
        Fc = wl[slot].shape[2]
        dw_in = matmul_tn(hb.reshape(-1, D), dgu.reshape(2, BL * T, 2 * Fc), Fc, "dw_ffn_in")
        dw_out = matmul_tn(ab_.reshape(-1, 2 * Fc), dyb.reshape(1, -1, D), D, "dw_ffn_out")
        return dx, dw_in, dw_out.reshape(N_CHIPS, -1, D)

    pending, paired, tok = None, None, 0.0
    for i in reversed(range(L)):
        wl, sv = wts[i], saved[i]
        a = i // 2
        dx, dw_in1, dw_out1 = ffn_back(i, 2, 1, dx, tok)
        m3 = mod[i, :, 3:6]
        if paired is not None:
            theirs, recv = pair_finish(paired[0], dx, "pair_wait_%d" % paired[1])
            started = reduce_start(theirs, c_idx, "reduce_start_%d" % paired[1], recv=recv)
            pending, paired = (started, paired[1]), None
            m3 = m3 + started[4][0, 0]
        g = norm_g_full[i, 1].reshape(1, D)
        if i % 2 == 0:
            w_pw = wl[5].reshape(D, D)
            wdw = pad_taps(w_dw_full[a])
            du2, u3b, dyb, dgate, vec = conv_out_bwd(dx, sv["y1"], sv["u2"], m3, cm_ln_g[a].reshape(1, D),
                                                     cm_ln_b[a].reshape(1, D), w_pw)
            dx, hb, dab, dwdw, dbglu, dm, dg = conv_glu_bwd(sv["x1"], dx, du2, sv["u"], m3, g, wl[4],
                                                            cm_b_glu[a].reshape(1, -1), wdw)
            dm = dm.at[:, 2:3, :].set(dgate)
            vec = jnp.sum(vec, axis=0)
            g_cm["b_pw"][a], g_cm["ln_g"][a], g_cm["ln_b"][a], g_cm["b_dw"][a] = vec[0], vec[1], vec[2], vec[3]
            g_cm["w_dw"][a] = jnp.sum(dwdw, axis=0)[:KC]
            g_cm["b_glu"][a] = jnp.sum(dbglu, axis=(0, 1))
            dw_a = matmul_tn(hb.reshape(-1, D), dab.reshape(1, -1, 2 * D), D // 2, "dw_glu")
            dw_b = matmul_tn(u3b.reshape(-1, D), dyb.reshape(1, -1, D), D, "dw_sq").reshape(N_CHIPS, -1, D)
        else:
            w_main, w_ab = dn_weights(i)
            w_out = wl[5].reshape(W, D)
            do, dz, ogb, dyb, dgate, dog = dn_out_bwd(dx, sv["y1"], sv["o"], sv["z"], m3, dn_o_g[a].reshape(1, Dh), w_out)
            dq, dk, dv, dgb, dbb = dn_chunk_bwd(*sv["qkvgb"], sv["sp"], sv["inv"], do)
            dc, dab, small = dn_conv_bwd(dq, dk, dv, dgb, dbb, sv["pre"], sv["ab"], w_sconv_full[a],
                                         row128(dn_a_log[a]), row128(dn_dt_bias[a]))
            dx, hb, dproj, dws, dm, dg = dn_proj_bwd(sv["x1"], dx, dc, sv["pre"], dz, dab, m3, g, w_main, w_ab,
                                                     w_sconv_full[a])
            dm = dm.at[:, 2:3, :].set(dgate)
            small = jnp.sum(small, axis=0)
            g_dn["a_log"][a], g_dn["dt_bias"][a] = small[0, :H], small[1, :H]
            g_dn["o_g"][a] = jnp.sum(dog, axis=(0, 1))
            g_dn["w_sconv"][a] = jnp.sum(dws, axis=0)
            dw_main = matmul_tn(hb.reshape(-1, D), dproj.reshape(1, -1, 4 * W), W, "dw_dn_main")
            dw_ab = matmul_tn(hb.reshape(-1, D), dab.reshape(1, -1, LANES), LANES, "dw_dn_ab")
            full = jnp.concatenate([jnp.transpose(dw_main, (1, 0, 2)).reshape(D, 4 * W), dw_ab[0][:, :2 * H]], axis=1)
            dw_a = jnp.transpose(full.reshape(D, N_CHIPS, -1), (1, 0, 2))
            dw_b = matmul_tn(ogb.reshape(-1, W), dyb.reshape(1, -1, D), D, "dw_sq").reshape(N_CHIPS, -1, D)
        dmod[i][1] = dm
        d_norm_g[i][1] = jnp.sum(dg, axis=(0, 1))
        if i > 0:
            dx, dw_in0, dw_out0 = ffn_back(i, 0, 0, dx)
            if pending is not None:
                stacks = reduce_finish(pending[0], dx, where, "reduce_wait_%d" % pending[1], stacks,
                                       targets(pending[1], range(6)))
                pending = None
            handed = pair_start([dw_in0, dw_in1, dw_out0, dw_out1, dw_a, dw_b], "pair_start_%d" % i)
            paired, tok = (handed, i), handed[4][0, 0]
        else:
            part_a = reduce_start([dw_in1, dw_out1, dw_a, dw_b], c_idx, "reduce_start_0a")
            dx, dw_in0, dw_out0 = ffn_back(0, 0, 0, dx, part_a[4][0, 0])
            if pending is not None:
                stacks = reduce_finish(pending[0], dx, where, "reduce_wait_%d" % pending[1], stacks,
                                       targets(pending[1], range(6)))
            stacks = reduce_finish(part_a, dx, where, "reduce_wait_0a", stacks, targets(0, (1, 3, 4, 5)))

    part = dict(
        norm_g=jnp.stack([jnp.stack(r) for r in d_norm_g]),
        cm_b_glu=jnp.stack(g_cm["b_glu"]), cm_w_dw=jnp.stack(g_cm["w_dw"]), cm_b_dw=jnp.stack(g_cm["b_dw"]),
        cm_ln_g=jnp.stack(g_cm["ln_g"]), cm_ln_b=jnp.stack(g_cm["ln_b"]), cm_b_pw=jnp.stack(g_cm["b_pw"]),
        dn_w_sconv=jnp.stack(g_dn["w_sconv"]), dn_a_log=jnp.stack(g_dn["a_log"]), dn_dt_bias=jnp.stack(g_dn["dt_bias"]),
        dn_o_g=jnp.stack(g_dn["o_g"]), final_g=jnp.sum(d_final_g, axis=(0, 1)),
        loss=jnp.sum(loss_part[:, 0, 0]).reshape(1))
    dmod_loc = jnp.stack([jnp.concatenate(r, axis=1) for r in dmod]).reshape(L, BL, C9)
    keys = list(part)
    packed = _pack([part[k] for k in keys] + [dmod_loc])
    R = packed.shape[0]
    gathered = allgather8(packed).reshape(N_DEV, R, LANES)
    summed = _unpack(sum_devices(gathered).reshape(-1), [part[k].shape for k in keys])
    tot = dict(zip(keys, summed))
    n_small = sum(int(part[k].size) for k in keys)
    dmod_all = gathered.reshape(N_DEV, -1)[:, n_small:n_small + L * BL * C9].reshape(N_DEV, L, BL, C9)
    dmod_all = jnp.transpose(dmod_all, (1, 0, 2, 3)).reshape(L, NB, C9)
    dmod_cols = lax.dynamic_slice_in_dim(dmod_all, chip * Ca, Ca, axis=2)
    g_w_ada, g_b_ada = ada_bwd(c_all, dmod_cols, dmod_all)
    delta, new_m, new_v = {}, {}, {}
    part_b = reduce_start([dw_in0, dw_out0], c_idx, "reduce_start_0b", g_w_ada)
    delta["w_ada"], new_m["w_ada"], new_v["w_ada"] = _adamw_any(w_ada, g_w_ada, m_w_ada, v_w_ada, "adamw_w_ada",
                                                                 part_b[4])
    stacks = reduce_finish(part_b, new_v["w_ada"], where, "reduce_wait_0b", stacks, targets(0, (0, 2)))

    def my_cols(full):
        n = full.shape[-1] // N_CHIPS
        return lax.dynamic_slice_in_dim(full, chip * n, n, axis=full.ndim - 1)

    grads = dict(
        norm_g=my_cols(tot["norm_g"]), w_ada=g_w_ada, b_ada=g_b_ada.reshape(L, C9),
        cm_b_glu=tot["cm_b_glu"], cm_w_dw=my_cols(tot["cm_w_dw"]), cm_b_dw=tot["cm_b_dw"], cm_ln_g=tot["cm_ln_g"],
        cm_ln_b=tot["cm_ln_b"], cm_b_pw=tot["cm_b_pw"], dn_w_sconv=my_cols(tot["dn_w_sconv"]),
        dn_a_log=tot["dn_a_log"], dn_dt_bias=tot["dn_dt_bias"], dn_o_g=tot["dn_o_g"], final_g=tot["final_g"],
        **{n: stacks[n].reshape(weights[n].shape) for n in big_names})

    large = ("w_ada", "w_ffn_in", "w_ffn_out", "cm_w_glu", "cm_w_pw", "dn_w_in", "dn_w_out")
    for n in large[1:]:
        delta[n], new_m[n], new_v[n] = _adamw_any(weights[n], grads[n], mom_m[n], mom_v[n], "adamw_" + n)
    rest = [n for n in names if n not in large]
    shapes = [weights[n].shape for n in rest]
    pd, pm, pv = adamw(_pack([weights[n] for n in rest]), _pack([grads[n] for n in rest]),
                       _pack([mom_m[n] for n in rest]), _pack([mom_v[n] for n in rest]), "adamw_small")
    for n, d_, m_, v_ in zip(rest, _unpack(pd.reshape(-1), shapes), _unpack(pm.reshape(-1), shapes),
                             _unpack(pv.reshape(-1), shapes)):
        delta[n], new_m[n], new_v[n] = d_, m_, v_

    return (tot["loss"].reshape(()), dx, *[grads[n] for n in names], *[delta[n] for n in names],
            *[new_m[n] for n in names], *[new_v[n] for n in names])
```
